```python
import jax, jax.numpy as jnp
from jax import lax
import numpy as np

D_MODEL = 1024
BATCH = 2
SEQ = 8192
DEPTH = 1

GRID_W = 64
CTX_LEN = 256

HEAD_DIM = 64
N_HEADS = 16
N_KV_HEADS = 4
GROUP = N_HEADS // N_KV_HEADS
ATTN_W = N_HEADS * HEAD_DIM
KV_W = N_KV_HEADS * HEAD_DIM
WINDOW = 128
ATTN_BLOCK = 128
ATTN_SCALE = HEAD_DIM ** -0.5
ROPE_BASE = 10000.0
ROPE_PAIRS = HEAD_DIM // 4

POOL_WINDOWS = (2, 4, 8, 16)
N_POOL_GROUPS = len(POOL_WINDOWS)
POOL_W = D_MODEL
POOL_GROUP_W = POOL_W // N_POOL_GROUPS

IN_W = ATTN_W + 2 * KV_W + POOL_W + 2 * D_MODEL
SPLIT_POINTS = (ATTN_W, ATTN_W + KV_W, ATTN_W + 2 * KV_W,
                ATTN_W + 2 * KV_W + POOL_W, ATTN_W + 2 * KV_W + POOL_W + D_MODEL)

N_EXPERTS = 32
TOP_K = 4
D_FF = D_MODEL
SWIGLU_ALPHA = 1.702
SWIGLU_LIMIT = 7.0
EXPERT_BLOCK = 128

NORM_EPS = 1e-5
NEG_INF = -1e30

kernel_name = 'hybrid_dit_gqa_pool_moe_block'


def rmsnorm(x, g):
    xf = x.astype(jnp.float32)
    y = xf * lax.rsqrt(jnp.mean(xf * xf, axis=-1, keepdims=True) + NORM_EPS)
    return (y * g.astype(jnp.float32)).astype(x.dtype)


def modulate(h, shift, scale):
    return h * (1 + scale) + shift


def apply_axial_rope(t, cos, sin):
    B, L, H, _ = t.shape
    tr = t.reshape(B, L, H, 2, 2, ROPE_PAIRS)
    t1, t2 = tr[..., 0, :], tr[..., 1, :]
    cs = cos[None, :, None].astype(t.dtype)
    sn = sin[None, :, None].astype(t.dtype)
    out = jnp.stack([t1 * cs - t2 * sn, t2 * cs + t1 * sn], axis=-2)
    return out.reshape(t.shape)


def windowed_attention(q, k, v, k_ctx, v_ctx, sink):
    B, L = q.shape[:2]
    Lc = k_ctx.shape[1]
    nblk = L // ATTN_BLOCK
    qb = q.reshape(B, nblk, ATTN_BLOCK, N_KV_HEADS, GROUP, HEAD_DIM).transpose(1, 0, 2, 3, 4, 5)
    pad = ((0, 0), (ATTN_BLOCK, ATTN_BLOCK), (0, 0), (0, 0))

    def band(t):
        tp = jnp.pad(t, pad).reshape(B, nblk + 2, ATTN_BLOCK, N_KV_HEADS, HEAD_DIM)
        tb = jnp.concatenate([tp[:, :-2], tp[:, 1:-1], tp[:, 2:]], axis=2)
        return tb.transpose(1, 0, 2, 3, 4)

    k_band, v_band = band(k), band(v)
    sink_l = sink.reshape(N_KV_HEADS, GROUP).astype(jnp.float32)
    q_off = jnp.arange(ATTN_BLOCK, dtype=jnp.int32)
    k_off = jnp.arange(3 * ATTN_BLOCK, dtype=jnp.int32) - ATTN_BLOCK
    n_loc = 3 * ATTN_BLOCK

    def one_block(args):
        qn, kn, vn, n = args
        s_loc = jnp.einsum('bqhgd,bshd->bhgqs', qn, kn).astype(jnp.float32) * ATTN_SCALE
        s_ctx = jnp.einsum('bqhgd,bchd->bhgqc', qn, k_ctx).astype(jnp.float32) * ATTN_SCALE
        qpos = n * ATTN_BLOCK + q_off
        kpos = n * ATTN_BLOCK + k_off
        valid = ((jnp.abs(qpos[:, None] - kpos[None, :]) <= WINDOW)
                 & (kpos[None, :] >= 0) & (kpos[None, :] < L))
        s_loc = jnp.where(valid, s_loc, NEG_INF)
        sink_b = jnp.broadcast_to(sink_l[None, :, :, None, None], s_loc.shape[:-1] + (1,))
        probs = jax.nn.softmax(jnp.concatenate([s_loc, s_ctx, sink_b], axis=-1), axis=-1)
        p_loc = probs[..., :n_loc].astype(vn.dtype)
        p_ctx = probs[..., n_loc:n_loc + Lc].astype(v_ctx.dtype)
        return (jnp.einsum('bhgqs,bshd->bqhgd', p_loc, vn)
                + jnp.einsum('bhgqc,bchd->bqhgd', p_ctx, v_ctx))

    out = lax.map(one_block, (qb, k_band, v_band, jnp.arange(nblk, dtype=jnp.int32)))
    return out.transpose(1, 0, 2, 3, 4, 5).reshape(B, L, ATTN_W)


def context_attention(q, k, v, sink):
    B, Lc = q.shape[:2]
    qg = q.reshape(B, Lc, N_KV_HEADS, GROUP, HEAD_DIM)
    s = jnp.einsum('bqhgd,bchd->bhgqc', qg, k).astype(jnp.float32) * ATTN_SCALE
    sink_b = jnp.broadcast_to(
        sink.reshape(N_KV_HEADS, GROUP).astype(jnp.float32)[None, :, :, None, None],
        s.shape[:-1] + (1,))
    p = jax.nn.softmax(jnp.concatenate([s, sink_b], axis=-1), axis=-1)[..., :Lc].astype(v.dtype)
    return jnp.einsum('bhgqc,bchd->bqhgd', p, v).reshape(B, Lc, ATTN_W)


def multiscale_pool(u, w_pool, pool_scale):
    B, L, _ = u.shape
    ug = u.reshape(B, L, N_POOL_GROUPS, POOL_GROUP_W).astype(jnp.float32)
    cs = jnp.concatenate([jnp.zeros_like(ug[:, :1]), jnp.cumsum(ug, axis=1)], axis=1)
    half = jnp.array([w // 2 for w in POOL_WINDOWS], dtype=jnp.int32)
    t = jnp.arange(L, dtype=jnp.int32)[:, None]
    lo = jnp.clip(t - half[None, :], 0, L)
    hi = jnp.clip(t + half[None, :], 0, L)
    g = jnp.arange(N_POOL_GROUPS, dtype=jnp.int32)[None, :]
    window_sum = cs[:, hi, g] - cs[:, lo, g]
    mean = window_sum / (hi - lo).astype(jnp.float32)[None, :, :, None]
    diff = (mean - ug).astype(u.dtype)
    mixed = jnp.einsum('blgc,gcd->blgd', diff, w_pool)
    return mixed.reshape(B, L, POOL_W) * pool_scale


def merge_branches(attn_o, pool_o, g_attn, g_pool, w_attn_br, w_pool_br, w_out):
    merged = (jax.nn.sigmoid(g_attn) * (attn_o @ w_attn_br)
              + jax.nn.sigmoid(g_pool) * (pool_o @ w_pool_br))
    return merged @ w_out


def clamped_swiglu(gu):
    gate = jnp.minimum(gu[..., :D_FF], SWIGLU_LIMIT)
    lin = jnp.clip(gu[..., D_FF:], -SWIGLU_LIMIT, SWIGLU_LIMIT)
    return gate * jax.nn.sigmoid(SWIGLU_ALPHA * gate) * (lin + 1)


def moe_ffn(h, w_router, b_router, w_gu, b_gu, w_down, b_down):
    N, D = h.shape
    logits = (h @ w_router + b_router).astype(jnp.float32)
    top_val, top_idx = lax.top_k(logits, TOP_K)
    gate_w = jax.nn.softmax(top_val, axis=-1).astype(h.dtype)
    NK = N * TOP_K
    flat_e = top_idx.reshape(-1)
    order = jnp.argsort(flat_e)
    sorted_e = flat_e[order]
    tok_sorted = (order // TOP_K).astype(jnp.int32)
    counts = jnp.zeros((N_EXPERTS,), jnp.int32).at[flat_e].add(1)
    padded = (counts + EXPERT_BLOCK - 1) // EXPERT_BLOCK * EXPERT_BLOCK
    start = jnp.cumsum(counts) - counts
    pend = jnp.cumsum(padded)
    pstart = pend - padded
    dest = pstart[sorted_e] + (jnp.arange(NK, dtype=jnp.int32) - start[sorted_e])
    num_blocks = -(-NK // EXPERT_BLOCK) + N_EXPERTS
    P = num_blocks * EXPERT_BLOCK
    tok_buf = jnp.full((P,), N, jnp.int32).at[dest].set(tok_sorted)
    h_pad = jnp.concatenate([h, jnp.zeros((1, D), h.dtype)], axis=0)
    x_blocks = h_pad[tok_buf].reshape(num_blocks, EXPERT_BLOCK, D)
    block_start = jnp.arange(num_blocks, dtype=jnp.int32) * EXPERT_BLOCK
    block_e = jnp.minimum(jnp.searchsorted(pend, block_start, side='right'), N_EXPERTS - 1)

    def expert_block(args):
        xb, e = args
        gu = xb @ w_gu[e] + b_gu[e]
        return clamped_swiglu(gu) @ w_down[e] + b_down[e]

    y_buf = lax.map(expert_block, (x_blocks, block_e)).reshape(P, D)
    y_sorted = y_buf[dest] * gate_w.reshape(-1)[order][:, None]
    return jax.ops.segment_sum(y_sorted, tok_sorted, num_segments=N)


def hybrid_layer(x, ctx, rope_cos, rope_sin, c, c_ctx, w_ada, b_ada, norm1_g, norm2_g,
                 w_in, b_in, attn_sink, w_pool, pool_scale, w_attn_br, w_pool_br, w_out,
                 w_router, b_router, w_gu, b_gu, w_down, b_down, update_ctx):
    B, L, D = x.shape
    Lc = ctx.shape[1]
    mod_x = (jax.nn.silu(c) @ w_ada + b_ada)[:, None, :]
    mod_c = jax.nn.silu(c_ctx) @ w_ada + b_ada
    sh1x, sc1x, g1x, sh2x, sc2x, g2x = jnp.split(mod_x, 6, axis=-1)
    sh1c, sc1c, g1c, sh2c, sc2c, g2c = jnp.split(mod_c, 6, axis=-1)

    hx = modulate(rmsnorm(x, norm1_g), sh1x, sc1x)
    hc = modulate(rmsnorm(ctx, norm1_g), sh1c, sc1c)
    qx, kx, vx, ux, gax, gpx = jnp.split(hx @ w_in + b_in, SPLIT_POINTS, axis=-1)
    kv_sl = slice(ATTN_W, ATTN_W + 2 * KV_W)
    kc, vc = jnp.split(hc @ w_in[:, kv_sl] + b_in[kv_sl], 2, axis=-1)
    kc = kc.reshape(B, Lc, N_KV_HEADS, HEAD_DIM)
    vc = vc.reshape(B, Lc, N_KV_HEADS, HEAD_DIM)
    qx = apply_axial_rope(qx.reshape(B, L, N_HEADS, HEAD_DIM), rope_cos, rope_sin)
    kx = apply_axial_rope(kx.reshape(B, L, N_KV_HEADS, HEAD_DIM), rope_cos, rope_sin)
    vx = vx.reshape(B, L, N_KV_HEADS, HEAD_DIM)
    attn_x = windowed_attention(qx, kx, vx, kc, vc, attn_sink)
    pool_x = multiscale_pool(ux, w_pool, pool_scale)
    x = x + g1x * merge_branches(attn_x, pool_x, gax, gpx, w_attn_br, w_pool_br, w_out)
    if update_ctx:
        qc, _, _, uc, gac, gpc = jnp.split(hc @ w_in + b_in, SPLIT_POINTS, axis=-1)
        attn_c = context_attention(qc.reshape(B, Lc, N_HEADS, HEAD_DIM), kc, vc, attn_sink)
        pool_c = multiscale_pool(uc, w_pool, pool_scale)
        ctx = ctx + g1c * merge_branches(attn_c, pool_c, gac, gpc, w_attn_br, w_pool_br, w_out)

    h2x = modulate(rmsnorm(x, norm2_g), sh2x, sc2x).reshape(B * L, D)
    if update_ctx:
        h2c = modulate(rmsnorm(ctx, norm2_g), sh2c, sc2c).reshape(B * Lc, D)
        y = moe_ffn(jnp.concatenate([h2x, h2c], axis=0),
                    w_router, b_router, w_gu, b_gu, w_down, b_down)
        x = x + g2x * y[:B * L].reshape(B, L, D)
        ctx = ctx + g2c * y[B * L:].reshape(B, Lc, D)
    else:
        y = moe_ffn(h2x, w_router, b_router, w_gu, b_gu, w_down, b_down)
        x = x + g2x * y.reshape(B, L, D)
    return x, ctx


def setup_inputs(seed: int = 0) -> dict:
    key = jax.random.key(seed)
    ks = jax.random.split(key, 24)

    def nrm(k, shape, scale):
        return jax.random.normal(k, shape, jnp.float32) * scale

    D = D_MODEL
    return {
        'x': nrm(ks[0], (BATCH, SEQ, D), 1.0),
        'c': nrm(ks[1], (BATCH, D), 1.0),
        'ctx': nrm(ks[2], (BATCH, CTX_LEN, D), 1.0),
        'c_ctx': nrm(ks[3], (D,), 1.0),
        'w_ada': nrm(ks[4], (DEPTH, D, 6 * D), 0.5 * D ** -0.5),
        'b_ada': nrm(ks[5], (DEPTH, 6 * D), 0.02),
        'norm1_g': 1.0 + nrm(ks[6], (DEPTH, D), 0.02),
        'norm2_g': 1.0 + nrm(ks[7], (DEPTH, D), 0.02),
        'w_in': nrm(ks[8], (DEPTH, D, IN_W), D ** -0.5),
        'b_in': nrm(ks[9], (DEPTH, IN_W), 0.02),
        'attn_sink': nrm(ks[10], (DEPTH, N_HEADS), 0.5),
        'w_pool': nrm(ks[11], (DEPTH, N_POOL_GROUPS, POOL_GROUP_W, POOL_GROUP_W), POOL_GROUP_W ** -0.5),
        'pool_scale': 1.0 + nrm(ks[12], (DEPTH, POOL_W), 0.02),
        'w_attn_br': nrm(ks[13], (DEPTH, ATTN_W, D), ATTN_W ** -0.5),
        'w_pool_br': nrm(ks[14], (DEPTH, POOL_W, D), POOL_W ** -0.5),
        'w_out': nrm(ks[15], (DEPTH, D, D), D ** -0.5),
        'w_router': nrm(ks[16], (DEPTH, D, N_EXPERTS), D ** -0.5),
        'b_router': nrm(ks[17], (DEPTH, N_EXPERTS), 0.01),
        'w_gu': nrm(ks[18], (DEPTH, N_EXPERTS, D, 2 * D_FF), D ** -0.5),
        'b_gu': nrm(ks[19], (DEPTH, N_EXPERTS, 2 * D_FF), 0.02),
        'w_down': nrm(ks[20], (DEPTH, N_EXPERTS, D_FF, D), D_FF ** -0.5),
        'b_down': nrm(ks[21], (DEPTH, N_EXPERTS, D), 0.02),
        'final_g': 1.0 + nrm(ks[22], (D,), 0.02),
    }


def reference(x, c, ctx, c_ctx, w_ada, b_ada, norm1_g, norm2_g, w_in, b_in, attn_sink,
              w_pool, pool_scale, w_attn_br, w_pool_br, w_out, w_router, b_router,
              w_gu, b_gu, w_down, b_down, final_g):
    L = x.shape[1]
    ROWS = L // GRID_W
    row = jnp.broadcast_to(jnp.arange(ROWS, dtype=jnp.float32)[:, None], (ROWS, GRID_W)).reshape(-1)
    col = jnp.broadcast_to(jnp.arange(GRID_W, dtype=jnp.float32)[None, :], (ROWS, GRID_W)).reshape(-1)
    inv_freq = ROPE_BASE ** (-jnp.arange(ROPE_PAIRS, dtype=jnp.float32) / ROPE_PAIRS)
    ang = jnp.stack([row[:, None] * inv_freq, col[:, None] * inv_freq], axis=1)
    rope_cos, rope_sin = jnp.cos(ang), jnp.sin(ang)
    for i in range(DEPTH):
        x, ctx = hybrid_layer(
            x, ctx, rope_cos, rope_sin, c, c_ctx, w_ada[i], b_ada[i], norm1_g[i], norm2_g[i],
            w_in[i], b_in[i], attn_sink[i], w_pool[i], pool_scale[i], w_attn_br[i],
            w_pool_br[i], w_out[i], w_router[i], b_router[i], w_gu[i], b_gu[i],
            w_down[i], b_down[i], update_ctx=(i + 1 < DEPTH))
    return rmsnorm(x, final_g)
```

```python
import functools

import jax
import jax.numpy as jnp
from jax import lax
from jax.experimental import pallas as pl
from jax.experimental.pallas import tpu as pltpu

D_MODEL = 1024
GRID_W = 64
HEAD_DIM = 64
N_HEADS = 16
N_KV_HEADS = 4
GROUP = N_HEADS // N_KV_HEADS
ATTN_W = N_HEADS * HEAD_DIM
KV_W = N_KV_HEADS * HEAD_DIM
WINDOW = 128
ATTN_SCALE = HEAD_DIM ** -0.5
ROPE_BASE = 10000.0
ROPE_PAIRS = HEAD_DIM // 4
POOL_WINDOWS = (2, 4, 8, 16)
POOL_GROUP_W = D_MODEL // len(POOL_WINDOWS)
IN_W = ATTN_W + 2 * KV_W + D_MODEL + 2 * D_MODEL
N_EXPERTS = 32
TOP_K = 4
D_FF = D_MODEL
SWIGLU_ALPHA = 1.702
SWIGLU_LIMIT = 7.0
NORM_EPS = 1e-5
NEG_INF = -1e30

LANES = 128
SUBLANES = 8
VMEM_LIMIT = 56 * 1024 * 1024

TM_INPROJ = 512
TQ = WINDOW
TM_MERGE = 256
TM_EXPERT = 256
TS_MOVE = 256
POOL_HALO = SUBLANES

F32 = jnp.float32
BF16 = jnp.bfloat16


def _params(*sem):
    return pltpu.CompilerParams(dimension_semantics=sem, vmem_limit_bytes=VMEM_LIMIT)


def _const_spec(shape):
    nd = len(shape)
    return pl.BlockSpec(shape, lambda *_: (0,) * nd)


def _adaln_kernel(c_ref, w_ref, b_ref, o_ref):
    c = c_ref[...]
    s = c * jax.nn.sigmoid(c)
    o_ref[...] = jnp.dot(s, w_ref[...], precision=lax.Precision.HIGHEST,
                         preferred_element_type=F32) + b_ref[...]


def _adaln(cond, w_ada, b_ada):
    rows, d = cond.shape
    n = w_ada.shape[1]
    tn = 1024
    return pl.pallas_call(
        _adaln_kernel,
        grid=(n // tn,),
        in_specs=[_const_spec((rows, d)),
                  pl.BlockSpec((d, tn), lambda j: (0, j)),
                  pl.BlockSpec((1, tn), lambda j: (0, j))],
        out_specs=pl.BlockSpec((rows, tn), lambda j: (0, j)),
        out_shape=jax.ShapeDtypeStruct((rows, n), F32),
        compiler_params=_params("arbitrary"),
        name="adaln",
    )(cond, w_ada, b_ada.reshape(1, n))


def _norm_mod(x, g, shift, scale):
    y = x * lax.rsqrt(jnp.mean(x * x, axis=-1, keepdims=True) + NORM_EPS) * g
    return y * (1.0 + scale) + shift


def _rope(t, cos, sin_signed):
    w = t.shape[1]
    half = HEAD_DIM // 2
    lane = lax.broadcasted_iota(jnp.int32, t.shape, 1)
    first = (lane & (HEAD_DIM - 1)) < half
    rot = jnp.where(first, pltpu.roll(t, w - half, 1), pltpu.roll(t, half, 1))
    reps = w // LANES
    cos_w = jnp.concatenate([cos] * reps, axis=1) if reps > 1 else cos
    sin_w = jnp.concatenate([sin_signed] * reps, axis=1) if reps > 1 else sin_signed
    return t * cos_w + rot * sin_w


def _inproj_kernel(x_ref, mod_ref, g_ref, w_ref, b_ref, cos_ref, sin_ref,
                   q_ref, k_ref, v_ref, u_ref, ga_ref, gp_ref):
    h = _norm_mod(x_ref[0], g_ref[...], mod_ref[0, 0:1, :], mod_ref[0, 1:2, :])
    hb = h.astype(BF16)
    cos = cos_ref[...]
    sin = sin_ref[...]

    def proj(c0, c1):
        return jnp.dot(hb, w_ref[:, c0:c1], preferred_element_type=F32) + b_ref[:, c0:c1]

    cw = 512
    for c0 in range(0, ATTN_W, cw):
        q_ref[0, :, c0:c0 + cw] = (_rope(proj(c0, c0 + cw), cos, sin) * ATTN_SCALE).astype(BF16)
    k_ref[0] = _rope(proj(ATTN_W, ATTN_W + KV_W), cos, sin).astype(BF16)
    v_ref[0] = proj(ATTN_W + KV_W, ATTN_W + 2 * KV_W).astype(BF16)
    base = ATTN_W + 2 * KV_W
    for ref in (u_ref, ga_ref, gp_ref):
        for c0 in range(0, D_MODEL, cw):
            ref[0, :, c0:c0 + cw] = proj(base + c0, base + c0 + cw)
        base += D_MODEL


def _inproj(x, mod, norm_g, w_in_bf, b_in, cos_t, sin_t):
    b, l, d = x.shape
    tm = TM_INPROJ
    row = lambda bi, i: (bi, i, 0)
    out_shape = (
        jax.ShapeDtypeStruct((b, l, ATTN_W), BF16),
        jax.ShapeDtypeStruct((b, l, KV_W), BF16),
        jax.ShapeDtypeStruct((b, l, KV_W), BF16),
        jax.ShapeDtypeStruct((b, l, D_MODEL), F32),
        jax.ShapeDtypeStruct((b, l, D_MODEL), F32),
        jax.ShapeDtypeStruct((b, l, D_MODEL), F32),
    )
    return pl.pallas_call(
        _inproj_kernel,
        grid=(b, l // tm),
        in_specs=[pl.BlockSpec((1, tm, d), row),
                  pl.BlockSpec((1, 6, d), lambda bi, i: (bi, 0, 0)),
                  _const_spec((1, d)),
                  _const_spec((d, IN_W)),
                  _const_spec((1, IN_W)),
                  pl.BlockSpec((tm, LANES), lambda bi, i: (i, 0)),
                  pl.BlockSpec((tm, LANES), lambda bi, i: (i, 0))],
        out_specs=(pl.BlockSpec((1, tm, ATTN_W), row),
                   pl.BlockSpec((1, tm, KV_W), row),
                   pl.BlockSpec((1, tm, KV_W), row),
                   pl.BlockSpec((1, tm, D_MODEL), row),
                   pl.BlockSpec((1, tm, D_MODEL), row),
                   pl.BlockSpec((1, tm, D_MODEL), row)),
        out_shape=out_shape,
        compiler_params=_params("arbitrary", "arbitrary"),
        name="inproj",
    )(x, mod, norm_g, w_in_bf, b_in, cos_t, sin_t)


def _ctx_kv_kernel(x_ref, mod_ref, g_ref, w_ref, b_ref, k_ref, v_ref):
    h = _norm_mod(x_ref[0], g_ref[...], mod_ref[0, 0:1, :], mod_ref[0, 1:2, :])
    kv = jnp.dot(h.astype(BF16), w_ref[...], preferred_element_type=F32) + b_ref[...]
    k_ref[0] = kv[:, :KV_W].astype(BF16)
    v_ref[0] = kv[:, KV_W:].astype(BF16)


def _ctx_kv(ctx, mod_c, norm_g, w_kv_bf, b_kv):
    b, lc, d = ctx.shape
    row = lambda bi: (bi, 0, 0)
    return pl.pallas_call(
        _ctx_kv_kernel,
        grid=(b,),
        in_specs=[pl.BlockSpec((1, lc, d), row),
                  _const_spec((1, 6, d)),
                  _const_spec((1, d)),
                  _const_spec((d, 2 * KV_W)),
                  _const_spec((1, 2 * KV_W))],
        out_specs=(pl.BlockSpec((1, lc, KV_W), row), pl.BlockSpec((1, lc, KV_W), row)),
        out_shape=(jax.ShapeDtypeStruct((b, lc, KV_W), BF16),
                   jax.ShapeDtypeStruct((b, lc, KV_W), BF16)),
        compiler_params=_params("arbitrary"),
        name="ctx_kv",
    )(ctx, mod_c, norm_g, w_kv_bf, b_kv)


def _dot_nt(a, b):
    return lax.dot_general(a, b, (((1,), (1,)), ((), ())), preferred_element_type=F32)


def _attn_kernel(sink_ref, q_ref, kp_ref, kc_ref, kn_ref, vp_ref, vc_ref, vn_ref,
                 kx_ref, vx_ref, o_ref):
    n = pl.program_id(1)
    last = pl.num_programs(1) - 1
    rows = GROUP * TQ
    n_loc = 3 * TQ
    qi = lax.broadcasted_iota(jnp.int32, (rows, n_loc), 0) & (TQ - 1)
    col = lax.broadcasted_iota(jnp.int32, (rows, n_loc), 1)
    kj = col - TQ
    valid = (jnp.abs(qi - kj) <= WINDOW)
    valid = valid & ((col >= TQ) | (n > 0)) & ((col < 2 * TQ) | (n < last))
    row_id = lax.broadcasted_iota(jnp.int32, (rows, 1), 0)

    for j in range(N_KV_HEADS):
        hs = slice(j * HEAD_DIM, (j + 1) * HEAD_DIM)
        q4 = jnp.concatenate(
            [q_ref[0, :, (j * GROUP + g) * HEAD_DIM:(j * GROUP + g + 1) * HEAD_DIM]
             for g in range(GROUP)], axis=0)
        k_loc = jnp.concatenate([kp_ref[0, :, hs], kc_ref[0, :, hs], kn_ref[0, :, hs]], axis=0)
        v_loc = jnp.concatenate([vp_ref[0, :, hs], vc_ref[0, :, hs], vn_ref[0, :, hs]], axis=0)
        s_loc = jnp.where(valid, _dot_nt(q4, k_loc), NEG_INF)
        s_ctx = _dot_nt(q4, kx_ref[0, :, hs])
        sink = jnp.zeros((rows, 1), F32)
        for g in range(GROUP):
            sink = jnp.where(row_id // TQ == g, sink_ref[j * GROUP + g], sink)
        m = jnp.maximum(jnp.maximum(jnp.max(s_loc, axis=-1, keepdims=True),
                                    jnp.max(s_ctx, axis=-1, keepdims=True)), sink)
        p_loc = jnp.exp(s_loc - m)
        p_ctx = jnp.exp(s_ctx - m)
        denom = (jnp.sum(p_loc, axis=-1, keepdims=True) + jnp.sum(p_ctx, axis=-1, keepdims=True)
                 + jnp.exp(sink - m))
        o = (jnp.dot(p_loc.astype(BF16), v_loc, preferred_element_type=F32)
             + jnp.dot(p_ctx.astype(BF16), vx_ref[0, :, hs], preferred_element_type=F32))
        o = o / denom
        for g in range(GROUP):
            c0 = (j * GROUP + g) * HEAD_DIM
            o_ref[0, :, c0:c0 + HEAD_DIM] = o[g * TQ:(g + 1) * TQ, :].astype(BF16)


def _attention(q, k, v, kx, vx, sink):
    b, l, _ = q.shape
    lc = kx.shape[1]
    nblk = l // TQ
    cur = lambda bi, n: (bi, n, 0)
    prev = lambda bi, n: (bi, jnp.maximum(n - 1, 0), 0)
    nxt = lambda bi, n: (bi, jnp.minimum(n + 1, nblk - 1), 0)
    kvb = (1, TQ, KV_W)
    return pl.pallas_call(
        _attn_kernel,
        grid=(b, nblk),
        in_specs=[pl.BlockSpec(memory_space=pltpu.SMEM),
                  pl.BlockSpec((1, TQ, ATTN_W), cur),
                  pl.BlockSpec(kvb, prev), pl.BlockSpec(kvb, cur), pl.BlockSpec(kvb, nxt),
                  pl.BlockSpec(kvb, prev), pl.BlockSpec(kvb, cur), pl.BlockSpec(kvb, nxt),
                  pl.BlockSpec((1, lc, KV_W), lambda bi, n: (bi, 0, 0)),
                  pl.BlockSpec((1, lc, KV_W), lambda bi, n: (bi, 0, 0))],
        out_specs=pl.BlockSpec((1, TQ, ATTN_W), cur),
        out_shape=jax.ShapeDtypeStruct((b, l, ATTN_W), BF16),
        compiler_params=_params("arbitrary", "arbitrary"),
        name="attention",
    )(sink, q, k, k, k, v, v, v, kx, vx)


def _shift_rows(a, d):
    n = a.shape[0]
    return pltpu.roll(a, (-d) % n, 0)


def _merge_kernel(ao_ref, u_ref, up_ref, un_ref, ga_ref, gp_ref, x_ref, mod_ref,
                  wpool_ref, pscale_ref, wab_ref, wpb_ref, wout_ref, g2_ref, wr_ref, br_ref,
                  x1_ref, h2_ref, idx_ref, gate_ref, rank_ref, cnt_ref, carry_ref, *, seq_len):
    bi = pl.program_id(0)
    i = pl.program_id(1)
    last = pl.num_programs(1) - 1
    tm = TM_MERGE

    @pl.when((bi == 0) & (i == 0))
    def _():
        carry_ref[...] = jnp.zeros_like(carry_ref)

    u = u_ref[0]
    prev = jnp.where(i > 0, up_ref[0], 0.0)
    nxt = jnp.where(i < last, un_ref[0], 0.0)
    ext = jnp.concatenate([prev, u, nxt], axis=0)
    t = i * tm + lax.broadcasted_iota(jnp.int32, (tm, 1), 0)
    mixed = []
    for g, w in enumerate(POOL_WINDOWS):
        cs = slice(g * POOL_GROUP_W, (g + 1) * POOL_GROUP_W)
        e = ext[:, cs]
        acc = _shift_rows(e, -1) + e
        step = 1
        while 2 * step < w:
            acc = _shift_rows(acc, -step) + _shift_rows(acc, step)
            step *= 2
        win = acc[POOL_HALO:POOL_HALO + tm]
        half = w // 2
        cnt = (jnp.minimum(t + half, seq_len) - jnp.maximum(t - half, 0)).astype(F32)
        diff = (win / cnt - u[:, cs]).astype(BF16)
        mixed.append(jnp.dot(diff, wpool_ref[g], preferred_element_type=F32))
    pool_o = jnp.concatenate(mixed, axis=1) * pscale_ref[...]

    a = jnp.dot(ao_ref[0], wab_ref[...], preferred_element_type=F32)
    p = jnp.dot(pool_o.astype(BF16), wpb_ref[...], preferred_element_type=F32)
    merged = jax.nn.sigmoid(ga_ref[0]) * a + jax.nn.sigmoid(gp_ref[0]) * p
    o = jnp.dot(merged.astype(BF16), wout_ref[...], preferred_element_type=F32)
    x1 = x_ref[0] + mod_ref[0, 2:3, :] * o
    x1_ref[0] = x1

    h2 = _norm_mod(x1, g2_ref[...], mod_ref[0, 3:4, :], mod_ref[0, 4:5, :])
    h2_ref[...] = h2
    logits = jnp.dot(h2, wr_ref[...], precision=lax.Precision.HIGHEST,
                     preferred_element_type=F32) + br_ref[...]
    lane = lax.broadcasted_iota(jnp.int32, (tm, N_EXPERTS), 1).astype(F32)
    work = logits
    vals, idxs, sels = [], [], []
    for _ in range(TOP_K):
        mx = jnp.max(work, axis=-1, keepdims=True)
        ix = jnp.min(jnp.where(work == mx, lane, float(N_EXPERTS)), axis=-1, keepdims=True)
        sel = lane == ix
        work = jnp.where(sel, -jnp.inf, work)
        vals.append(mx)
        idxs.append(ix)
        sels.append(sel)
    ex = [jnp.exp(v - vals[0]) for v in vals]
    tot = ex[0] + ex[1] + ex[2] + ex[3]
    gate_ref[...] = jnp.concatenate([e_ / tot for e_ in ex], axis=1)
    idx_ref[...] = jnp.concatenate(idxs, axis=1).astype(jnp.int32)

    member = (sels[0] | sels[1] | sels[2] | sels[3]).astype(F32)
    r_id = lax.broadcasted_iota(jnp.int32, (tm, tm), 0)
    c_id = lax.broadcasted_iota(jnp.int32, (tm, tm), 1)
    tri = (r_id > c_id).astype(BF16)
    before = jnp.dot(tri, member.astype(BF16), preferred_element_type=F32) + carry_ref[...]
    ranks = [jnp.sum(jnp.where(s, before, 0.0), axis=-1, keepdims=True) for s in sels]
    rank_ref[...] = jnp.concatenate(ranks, axis=1).astype(jnp.int32)
    total = carry_ref[...] + jnp.sum(member, axis=0, keepdims=True)
    carry_ref[...] = total
    cnt_ref[...] = total


def _merge_route(attn_o, u, ga, gp, x, mod, w_pool_bf, pool_scale, wab, wpb, wout,
                 norm2_g, w_router, b_router):
    b, l, d = x.shape
    tm = TM_MERGE
    nt = l // tm
    n = b * l
    hb = tm // POOL_HALO
    row = lambda bi, i: (bi, i, 0)
    flat = lambda bi, i: (bi * nt + i, 0)
    halo_prev = lambda bi, i: (bi, jnp.maximum(i * hb - 1, 0), 0)
    halo_next = lambda bi, i: (bi, jnp.minimum((i + 1) * hb, l // POOL_HALO - 1), 0)
    out_shape = (
        jax.ShapeDtypeStruct((b, l, d), F32),
        jax.ShapeDtypeStruct((n, d), F32),
        jax.ShapeDtypeStruct((n, TOP_K), jnp.int32),
        jax.ShapeDtypeStruct((n, TOP_K), F32),
        jax.ShapeDtypeStruct((n, TOP_K), jnp.int32),
        jax.ShapeDtypeStruct((1, N_EXPERTS), F32),
    )
    return pl.pallas_call(
        functools.partial(_merge_kernel, seq_len=l),
        grid=(b, nt),
        in_specs=[pl.BlockSpec((1, tm, d), row),
                  pl.BlockSpec((1, tm, d), row),
                  pl.BlockSpec((1, POOL_HALO, d), halo_prev),
                  pl.BlockSpec((1, POOL_HALO, d), halo_next),
                  pl.BlockSpec((1, tm, d), row),
                  pl.BlockSpec((1, tm, d), row),
                  pl.BlockSpec((1, tm, d), row),
                  pl.BlockSpec((1, 6, d), lambda bi, i: (bi, 0, 0)),
                  _const_spec(w_pool_bf.shape),
                  _const_spec((1, d)),
                  _const_spec((d, d)), _const_spec((d, d)), _const_spec((d, d)),
                  _const_spec((1, d)),
                  _const_spec((d, N_EXPERTS)),
                  _const_spec((1, N_EXPERTS))],
        out_specs=(pl.BlockSpec((1, tm, d), row),
                   pl.BlockSpec((tm, d), flat),
                   pl.BlockSpec((tm, TOP_K), flat),
                   pl.BlockSpec((tm, TOP_K), flat),
                   pl.BlockSpec((tm, TOP_K), flat),
                   _const_spec((1, N_EXPERTS))),
        out_shape=out_shape,
        scratch_shapes=[pltpu.VMEM((1, N_EXPERTS), F32)],
        compiler_params=_params("arbitrary", "arbitrary"),
        name="merge_route",
    )(attn_o, u, u, u, ga, gp, x, mod, w_pool_bf, pool_scale, wab, wpb, wout,
      norm2_g, w_router, b_router)


def _scatter_kernel(dest_ref, h_ref, xs_in_ref, xs_ref, sem):
    del xs_in_ref
    ts = TS_MOVE

    def row_copy(r, d):
        return pltpu.make_async_copy(h_ref.at[pl.ds(r, 1)], xs_ref.at[pl.ds(d, 1)], sem)

    def issue(r, c):
        for k in range(TOP_K):
            row_copy(r, dest_ref[r * TOP_K + k]).start()
        return c

    lax.fori_loop(0, ts, issue, 0)

    def drain(r, c):
        for k in range(TOP_K):
            row_copy(r, dest_ref[r * TOP_K + k]).wait()
        return c

    lax.fori_loop(0, ts, drain, 0)


def _scatter_rows(dest_flat, h2, p_rows):
    n, d = h2.shape
    ts = TS_MOVE
    xs0 = jnp.zeros((p_rows, d), F32)
    return pl.pallas_call(
        _scatter_kernel,
        grid=(n // ts,),
        in_specs=[pl.BlockSpec((ts * TOP_K,), lambda i: (i,), memory_space=pltpu.SMEM),
                  pl.BlockSpec((ts, d), lambda i: (i, 0)),
                  pl.BlockSpec(memory_space=pl.ANY)],
        out_specs=pl.BlockSpec(memory_space=pl.ANY),
        out_shape=jax.ShapeDtypeStruct((p_rows, d), F32),
        scratch_shapes=[pltpu.SemaphoreType.DMA],
        input_output_aliases={2: 0},
        compiler_params=_params("arbitrary"),
        name="moe_scatter",
    )(dest_flat, h2, xs0)


def _expert_kernel(be_ref, first_ref, nu_ref, x_ref, wgu_ref, bgu_ref, wd_ref, bd_ref,
                   y_ref, wgu_s, wd_s):
    i = pl.program_id(0)

    @pl.when(i < nu_ref[0])
    def _():
        @pl.when(first_ref[i] == 1)
        def _():
            ck = 256
            for r0 in range(0, D_MODEL, ck):
                wgu_s[r0:r0 + ck, :] = wgu_ref[0, r0:r0 + ck, :].astype(BF16)
                wd_s[r0:r0 + ck, :] = wd_ref[0, r0:r0 + ck, :].astype(BF16)

        x = x_ref[...].astype(BF16)
        acc = jnp.zeros((TM_EXPERT, D_MODEL), F32)
        fc = 512
        for c0 in range(0, D_FF, fc):
            gt = jnp.dot(x, wgu_s[:, c0:c0 + fc], preferred_element_type=F32) + bgu_ref[0, :, c0:c0 + fc]
            ln = (jnp.dot(x, wgu_s[:, D_FF + c0:D_FF + c0 + fc], preferred_element_type=F32)
                  + bgu_ref[0, :, D_FF + c0:D_FF + c0 + fc])
            gt = jnp.minimum(gt, SWIGLU_LIMIT)
            ln = jnp.clip(ln, -SWIGLU_LIMIT, SWIGLU_LIMIT)
            act = gt * jax.nn.sigmoid(SWIGLU_ALPHA * gt) * (ln + 1.0)
            acc = acc + jnp.dot(act.astype(BF16), wd_s[c0:c0 + fc, :], preferred_element_type=F32)
        y_ref[...] = acc + bd_ref[0]

    @pl.when(i >= nu_ref[0])
    def _():
        y_ref[...] = jnp.zeros_like(y_ref)


def _experts(block_e, first, n_used, xs, w_gu, b_gu, w_down, b_down):
    p_rows, d = xs.shape
    tm = TM_EXPERT
    nb = p_rows // tm

    def blk(i, be, fi, nu):
        return jnp.minimum(i, nu[0] - 1)

    xmap = lambda i, be, fi, nu: (blk(i, be, fi, nu), 0)
    wmap = lambda i, be, fi, nu: (be[blk(i, be, fi, nu)], 0, 0)
    grid_spec = pltpu.PrefetchScalarGridSpec(
        num_scalar_prefetch=3,
        grid=(nb,),
        in_specs=[pl.BlockSpec((tm, d), xmap),
                  pl.BlockSpec((1, d, 2 * D_FF), wmap),
                  pl.BlockSpec((1, 1, 2 * D_FF), wmap),
                  pl.BlockSpec((1, D_FF, d), wmap),
                  pl.BlockSpec((1, 1, d), wmap)],
        out_specs=pl.BlockSpec((tm, d), lambda i, be, fi, nu: (i, 0)),
        scratch_shapes=[pltpu.VMEM((d, 2 * D_FF), BF16), pltpu.VMEM((D_FF, d), BF16)],
    )
    return pl.pallas_call(
        _expert_kernel,
        grid_spec=grid_spec,
        out_shape=jax.ShapeDtypeStruct((p_rows, d), F32),
        compiler_params=_params("arbitrary"),
        name="moe_experts",
    )(block_e, first, n_used, xs, w_gu, b_gu.reshape(N_EXPERTS, 1, 2 * D_FF),
      w_down, b_down.reshape(N_EXPERTS, 1, d))


def _combine_kernel(dest_ref, ys_ref, gate_ref, x1_ref, mod_ref, fg_ref, o_ref, buf, sem):
    ts = TS_MOVE

    def row_copy(r, k):
        return pltpu.make_async_copy(ys_ref.at[pl.ds(dest_ref[r * TOP_K + k], 1)],
                                     buf.at[k, pl.ds(r, 1)], sem)

    def issue(r, c):
        for k in range(TOP_K):
            row_copy(r, k).start()
        return c

    lax.fori_loop(0, ts, issue, 0)

    def drain(r, c):
        for k in range(TOP_K):
            row_copy(r, k).wait()
        return c

    lax.fori_loop(0, ts, drain, 0)

    gate = gate_ref[...]
    y = gate[:, 0:1] * buf[0]
    for k in range(1, TOP_K):
        y = y + gate[:, k:k + 1] * buf[k]
    x2 = x1_ref[0] + mod_ref[0, 5:6, :] * y
    o_ref[0] = x2 * lax.rsqrt(jnp.mean(x2 * x2, axis=-1, keepdims=True) + NORM_EPS) * fg_ref[...]


def _combine(dest_flat, ys, gate_w, x1, mod, final_g):
    b, l, d = x1.shape
    ts = TS_MOVE
    nt = l // ts
    row = lambda bi, i: (bi, i, 0)
    return pl.pallas_call(
        _combine_kernel,
        grid=(b, nt),
        in_specs=[pl.BlockSpec((ts * TOP_K,), lambda bi, i: (bi * nt + i,), memory_space=pltpu.SMEM),
                  pl.BlockSpec(memory_space=pl.ANY),
                  pl.BlockSpec((ts, TOP_K), lambda bi, i: (bi * nt + i, 0)),
                  pl.BlockSpec((1, ts, d), row),
                  pl.BlockSpec((1, 6, d), lambda bi, i: (bi, 0, 0)),
                  _const_spec((1, d))],
        out_specs=pl.BlockSpec((1, ts, d), row),
        out_shape=jax.ShapeDtypeStruct((b, l, d), F32),
        scratch_shapes=[pltpu.VMEM((TOP_K, ts, d), F32), pltpu.SemaphoreType.DMA],
        compiler_params=_params("arbitrary", "arbitrary"),
        name="moe_combine",
    )(dest_flat, ys, gate_w, x1, mod, final_g)


def _rope_tables(seq_len):
    rows = seq_len // GRID_W
    row = jnp.broadcast_to(jnp.arange(rows, dtype=F32)[:, None], (rows, GRID_W)).reshape(-1)
    col = jnp.broadcast_to(jnp.arange(GRID_W, dtype=F32)[None, :], (rows, GRID_W)).reshape(-1)
    inv_freq = ROPE_BASE ** (-jnp.arange(ROPE_PAIRS, dtype=F32) / ROPE_PAIRS)
    ang = jnp.concatenate([row[:, None] * inv_freq, col[:, None] * inv_freq], axis=1)
    cos, sin = jnp.cos(ang), jnp.sin(ang)
    cos_t = jnp.concatenate([cos, cos, cos, cos], axis=1)
    sin_t = jnp.concatenate([-sin, sin, -sin, sin], axis=1)
    return cos_t, sin_t


def kernel(x, c, ctx, c_ctx, w_ada, b_ada, norm1_g, norm2_g, w_in, b_in, attn_sink, w_pool,
           pool_scale, w_attn_br, w_pool_br, w_out, w_router, b_router, w_gu, b_gu, w_down,
           b_down, final_g):
    b, l, d = x.shape
    n = b * l
    assert w_ada.shape[0] == 1, "single-layer block"

    cond = jnp.zeros((SUBLANES, d), F32).at[:b].set(c).at[b].set(c_ctx)
    mod = _adaln(cond, w_ada[0], b_ada[0])[:b + 1].reshape(b + 1, 6, d)
    mod_x, mod_c = mod[:b], mod[b:b + 1]

    cos_t, sin_t = _rope_tables(l)
    w_in_bf = w_in[0].astype(BF16)
    b_in2 = b_in[0].reshape(1, IN_W)
    g1 = norm1_g[0].reshape(1, d)
    q, k, v, u, ga, gp = _inproj(x, mod_x, g1, w_in_bf, b_in2, cos_t, sin_t)
    kv_sl = slice(ATTN_W, ATTN_W + 2 * KV_W)
    kx, vx = _ctx_kv(ctx, mod_c, g1, w_in_bf[:, kv_sl], b_in2[:, kv_sl])

    attn_o = _attention(q, k, v, kx, vx, attn_sink[0])

    x1, h2, top_idx, gate_w, rank, counts = _merge_route(
        attn_o, u, ga, gp, x, mod_x, w_pool[0].astype(BF16), pool_scale[0].reshape(1, d),
        w_attn_br[0].astype(BF16), w_pool_br[0].astype(BF16), w_out[0].astype(BF16),
        norm2_g[0].reshape(1, d), w_router[0], b_router[0].reshape(1, N_EXPERTS))

    tm = TM_EXPERT
    nb = n * TOP_K // tm + N_EXPERTS
    cnt = counts[0].astype(jnp.int32)
    padded = (cnt + tm - 1) // tm * tm
    pend = jnp.cumsum(padded)
    pstart = pend - padded
    dest = (pstart[top_idx] + rank).reshape(-1)
    n_used = (pend[-1] // tm).reshape(1)
    block_e = jnp.minimum(
        jnp.searchsorted(pend, jnp.arange(nb, dtype=jnp.int32) * tm, side='right'),
        N_EXPERTS - 1).astype(jnp.int32)
    first = jnp.concatenate([jnp.ones((1,), jnp.int32),
                             (block_e[1:] != block_e[:-1]).astype(jnp.int32)])

    xs = _scatter_rows(dest, h2, nb * tm)
    ys = _experts(block_e, first, n_used, xs, w_gu[0], b_gu[0], w_down[0], b_down[0])
    return _combine(dest, ys, gate_w, x1, mod_x, final_g.reshape(1, d))
```

```python
import functools

import jax
import jax.numpy as jnp
from jax import lax
from jax.experimental import pallas as pl
from jax.experimental.pallas import tpu as pltpu

D_MODEL = 1024
GRID_W = 64
HEAD_DIM = 64
N_HEADS = 16
N_KV_HEADS = 4
GROUP = N_HEADS // N_KV_HEADS
ATTN_W = N_HEADS * HEAD_DIM
KV_W = N_KV_HEADS * HEAD_DIM
WINDOW = 128
ATTN_SCALE = HEAD_DIM ** -0.5
ROPE_BASE = 10000.0
ROPE_PAIRS = HEAD_DIM // 4
POOL_WINDOWS = (2, 4, 8, 16)
POOL_GROUP_W = D_MODEL // len(POOL_WINDOWS)
IN_W = ATTN_W + 2 * KV_W + D_MODEL + 2 * D_MODEL
N_EXPERTS = 32
TOP_K = 4
D_FF = D_MODEL
SWIGLU_ALPHA = 1.702
SWIGLU_LIMIT = 7.0
NORM_EPS = 1e-5
NEG_INF = -1e30

LANES = 128
SUBLANES = 8
VMEM_LIMIT = 56 * 1024 * 1024

TM_INPROJ = 512
TQ = WINDOW
TM_MERGE = 256
TM_EXPERT = 256
TS_MOVE = 256
POOL_HALO = SUBLANES

F32 = jnp.float32
BF16 = jnp.bfloat16


def _params(*sem):
    return pltpu.CompilerParams(dimension_semantics=sem, vmem_limit_bytes=VMEM_LIMIT)


def _const_spec(shape):
    nd = len(shape)
    return pl.BlockSpec(shape, lambda *_: (0,) * nd)


def _adaln_kernel(c_ref, w_ref, b_ref, o_ref):
    c = c_ref[...]
    s = c * jax.nn.sigmoid(c)
    o_ref[...] = jnp.dot(s.astype(BF16), w_ref[...].astype(BF16),
                         preferred_element_type=F32) + b_ref[...]


def _adaln(cond, w_ada, b_ada):
    rows, d = cond.shape
    n = w_ada.shape[1]
    tn = 1024
    return pl.pallas_call(
        _adaln_kernel,
        grid=(n // tn,),
        in_specs=[_const_spec((rows, d)),
                  pl.BlockSpec((d, tn), lambda j: (0, j)),
                  pl.BlockSpec((1, tn), lambda j: (0, j))],
        out_specs=pl.BlockSpec((rows, tn), lambda j: (0, j)),
        out_shape=jax.ShapeDtypeStruct((rows, n), F32),
        compiler_params=_params("arbitrary"),
        name="adaln",
    )(cond, w_ada, b_ada.reshape(1, n))


def _norm_mod(x, g, shift, scale):
    y = x * lax.rsqrt(jnp.mean(x * x, axis=-1, keepdims=True) + NORM_EPS) * g
    return y * (1.0 + scale) + shift


def _rope(t, cos, sin_signed):
    w = t.shape[1]
    half = ROPE_PAIRS
    lane = lax.broadcasted_iota(jnp.int32, t.shape, 1)
    first = (lane & (2 * half - 1)) < half
    rot = jnp.where(first, pltpu.roll(t, w - half, 1), pltpu.roll(t, half, 1))
    reps = w // LANES
    cos_w = jnp.concatenate([cos] * reps, axis=1) if reps > 1 else cos
    sin_w = jnp.concatenate([sin_signed] * reps, axis=1) if reps > 1 else sin_signed
    return t * cos_w + rot * sin_w


def _inproj_kernel(x_ref, mod_ref, g_ref, w_ref, b_ref, cos_ref, sin_ref,
                   q_ref, k_ref, v_ref, u_ref, ga_ref, gp_ref):
    h = _norm_mod(x_ref[0], g_ref[...], mod_ref[0, 0:1, :], mod_ref[0, 1:2, :])
    hb = h.astype(BF16)
    cos = cos_ref[...]
    sin = sin_ref[...]

    def proj(c0, c1):
        return jnp.dot(hb, w_ref[:, c0:c1], preferred_element_type=F32) + b_ref[:, c0:c1]

    cw = 512
    for c0 in range(0, ATTN_W, cw):
        q_ref[0, :, c0:c0 + cw] = (_rope(proj(c0, c0 + cw), cos, sin) * ATTN_SCALE).astype(BF16)
    k_ref[0] = _rope(proj(ATTN_W, ATTN_W + KV_W), cos, sin).astype(BF16)
    v_ref[0] = proj(ATTN_W + KV_W, ATTN_W + 2 * KV_W).astype(BF16)
    base = ATTN_W + 2 * KV_W
    for ref in (u_ref, ga_ref, gp_ref):
        for c0 in range(0, D_MODEL, cw):
            ref[0, :, c0:c0 + cw] = proj(base + c0, base + c0 + cw)
        base += D_MODEL


def _inproj(x, mod, norm_g, w_in_bf, b_in, cos_t, sin_t):
    b, l, d = x.shape
    tm = TM_INPROJ
    row = lambda bi, i: (bi, i, 0)
    out_shape = (
        jax.ShapeDtypeStruct((b, l, ATTN_W), BF16),
        jax.ShapeDtypeStruct((b, l, KV_W), BF16),
        jax.ShapeDtypeStruct((b, l, KV_W), BF16),
        jax.ShapeDtypeStruct((b, l, D_MODEL), F32),
        jax.ShapeDtypeStruct((b, l, D_MODEL), F32),
        jax.ShapeDtypeStruct((b, l, D_MODEL), F32),
    )
    return pl.pallas_call(
        _inproj_kernel,
        grid=(b, l // tm),
        in_specs=[pl.BlockSpec((1, tm, d), row),
                  pl.BlockSpec((1, 6, d), lambda bi, i: (bi, 0, 0)),
                  _const_spec((1, d)),
                  _const_spec((d, IN_W)),
                  _const_spec((1, IN_W)),
                  pl.BlockSpec((tm, LANES), lambda bi, i: (i, 0)),
                  pl.BlockSpec((tm, LANES), lambda bi, i: (i, 0))],
        out_specs=(pl.BlockSpec((1, tm, ATTN_W), row),
                   pl.BlockSpec((1, tm, KV_W), row),
                   pl.BlockSpec((1, tm, KV_W), row),
                   pl.BlockSpec((1, tm, D_MODEL), row),
                   pl.BlockSpec((1, tm, D_MODEL), row),
                   pl.BlockSpec((1, tm, D_MODEL), row)),
        out_shape=out_shape,
        compiler_params=_params("arbitrary", "arbitrary"),
        name="inproj",
    )(x, mod, norm_g, w_in_bf, b_in, cos_t, sin_t)


def _ctx_kv_kernel(x_ref, mod_ref, g_ref, w_ref, b_ref, k_ref, v_ref):
    h = _norm_mod(x_ref[0], g_ref[...], mod_ref[0, 0:1, :], mod_ref[0, 1:2, :])
    kv = jnp.dot(h.astype(BF16), w_ref[...], preferred_element_type=F32) + b_ref[...]
    k_ref[0] = kv[:, :KV_W].astype(BF16)
    v_ref[0] = kv[:, KV_W:].astype(BF16)


def _ctx_kv(ctx, mod_c, norm_g, w_kv_bf, b_kv):
    b, lc, d = ctx.shape
    row = lambda bi: (bi, 0, 0)
    return pl.pallas_call(
        _ctx_kv_kernel,
        grid=(b,),
        in_specs=[pl.BlockSpec((1, lc, d), row),
                  _const_spec((1, 6, d)),
                  _const_spec((1, d)),
                  _const_spec((d, 2 * KV_W)),
                  _const_spec((1, 2 * KV_W))],
        out_specs=(pl.BlockSpec((1, lc, KV_W), row), pl.BlockSpec((1, lc, KV_W), row)),
        out_shape=(jax.ShapeDtypeStruct((b, lc, KV_W), BF16),
                   jax.ShapeDtypeStruct((b, lc, KV_W), BF16)),
        compiler_params=_params("arbitrary"),
        name="ctx_kv",
    )(ctx, mod_c, norm_g, w_kv_bf, b_kv)


def _dot_nt(a, b):
    return lax.dot_general(a, b, (((1,), (1,)), ((), ())), preferred_element_type=F32)


def _attn_kernel(sink_ref, q_ref, kp_ref, kc_ref, kn_ref, vp_ref, vc_ref, vn_ref,
                 kx_ref, vx_ref, o_ref):
    n = pl.program_id(1)
    last = pl.num_programs(1) - 1
    rows = GROUP * TQ
    n_loc = 3 * TQ
    qi = lax.broadcasted_iota(jnp.int32, (rows, n_loc), 0) & (TQ - 1)
    col = lax.broadcasted_iota(jnp.int32, (rows, n_loc), 1)
    kj = col - TQ
    valid = (jnp.abs(qi - kj) <= WINDOW)
    valid = valid & ((col >= TQ) | (n > 0)) & ((col < 2 * TQ) | (n < last))
    row_id = lax.broadcasted_iota(jnp.int32, (rows, 1), 0)

    for j in range(N_KV_HEADS):
        hs = slice(j * HEAD_DIM, (j + 1) * HEAD_DIM)
        q4 = jnp.concatenate(
            [q_ref[0, :, (j * GROUP + g) * HEAD_DIM:(j * GROUP + g + 1) * HEAD_DIM]
             for g in range(GROUP)], axis=0)
        k_loc = jnp.concatenate([kp_ref[0, :, hs], kc_ref[0, :, hs], kn_ref[0, :, hs]], axis=0)
        v_loc = jnp.concatenate([vp_ref[0, :, hs], vc_ref[0, :, hs], vn_ref[0, :, hs]], axis=0)
        s_loc = jnp.where(valid, _dot_nt(q4, k_loc), NEG_INF)
        s_ctx = _dot_nt(q4, kx_ref[0, :, hs])
        sink = jnp.zeros((rows, 1), F32)
        for g in range(GROUP):
            sink = jnp.where(row_id // TQ == g, sink_ref[j * GROUP + g], sink)
        m = jnp.maximum(jnp.maximum(jnp.max(s_loc, axis=-1, keepdims=True),
                                    jnp.max(s_ctx, axis=-1, keepdims=True)), sink)
        p_loc = jnp.exp(s_loc - m)
        p_ctx = jnp.exp(s_ctx - m)
        denom = (jnp.sum(p_loc, axis=-1, keepdims=True) + jnp.sum(p_ctx, axis=-1, keepdims=True)
                 + jnp.exp(sink - m))
        o = (jnp.dot(p_loc.astype(BF16), v_loc, preferred_element_type=F32)
             + jnp.dot(p_ctx.astype(BF16), vx_ref[0, :, hs], preferred_element_type=F32))
        o = o / denom
        for g in range(GROUP):
            c0 = (j * GROUP + g) * HEAD_DIM
            o_ref[0, :, c0:c0 + HEAD_DIM] = o[g * TQ:(g + 1) * TQ, :].astype(BF16)


def _attention(q, k, v, kx, vx, sink):
    b, l, _ = q.shape
    lc = kx.shape[1]
    nblk = l // TQ
    cur = lambda bi, n: (bi, n, 0)
    prev = lambda bi, n: (bi, jnp.maximum(n - 1, 0), 0)
    nxt = lambda bi, n: (bi, jnp.minimum(n + 1, nblk - 1), 0)
    kvb = (1, TQ, KV_W)
    return pl.pallas_call(
        _attn_kernel,
        grid=(b, nblk),
        in_specs=[pl.BlockSpec(memory_space=pltpu.SMEM),
                  pl.BlockSpec((1, TQ, ATTN_W), cur),
                  pl.BlockSpec(kvb, prev), pl.BlockSpec(kvb, cur), pl.BlockSpec(kvb, nxt),
                  pl.BlockSpec(kvb, prev), pl.BlockSpec(kvb, cur), pl.BlockSpec(kvb, nxt),
                  pl.BlockSpec((1, lc, KV_W), lambda bi, n: (bi, 0, 0)),
                  pl.BlockSpec((1, lc, KV_W), lambda bi, n: (bi, 0, 0))],
        out_specs=pl.BlockSpec((1, TQ, ATTN_W), cur),
        out_shape=jax.ShapeDtypeStruct((b, l, ATTN_W), BF16),
        compiler_params=_params("arbitrary", "arbitrary"),
        name="attention",
    )(sink, q, k, k, k, v, v, v, kx, vx)


def _shift_rows(a, d):
    n = a.shape[0]
    return pltpu.roll(a, (-d) % n, 0)


def _merge_kernel(ao_ref, u_ref, up_ref, un_ref, ga_ref, gp_ref, x_ref, mod_ref,
                  wpool_ref, pscale_ref, wab_ref, wpb_ref, wout_ref, g2_ref, wr_ref, br_ref,
                  x1_ref, h2_ref, idx_ref, gate_ref, rank_ref, cnt_ref, carry_ref, *, seq_len):
    bi = pl.program_id(0)
    i = pl.program_id(1)
    last = pl.num_programs(1) - 1
    tm = TM_MERGE

    @pl.when((bi == 0) & (i == 0))
    def _():
        carry_ref[...] = jnp.zeros_like(carry_ref)

    u = u_ref[0]
    prev = jnp.where(i > 0, up_ref[0], 0.0)
    nxt = jnp.where(i < last, un_ref[0], 0.0)
    ext = jnp.concatenate([prev, u, nxt], axis=0)
    t = i * tm + lax.broadcasted_iota(jnp.int32, (tm, 1), 0)
    mixed = []
    for g, w in enumerate(POOL_WINDOWS):
        cs = slice(g * POOL_GROUP_W, (g + 1) * POOL_GROUP_W)
        e = ext[:, cs]
        acc = _shift_rows(e, -1) + e
        step = 1
        while 2 * step < w:
            acc = _shift_rows(acc, -step) + _shift_rows(acc, step)
            step *= 2
        win = acc[POOL_HALO:POOL_HALO + tm]
        half = w // 2
        cnt = (jnp.minimum(t + half, seq_len) - jnp.maximum(t - half, 0)).astype(F32)
        diff = (win / cnt - u[:, cs]).astype(BF16)
        mixed.append(jnp.dot(diff, wpool_ref[g], preferred_element_type=F32))
    pool_o = jnp.concatenate(mixed, axis=1) * pscale_ref[...]

    a = jnp.dot(ao_ref[0], wab_ref[...], preferred_element_type=F32)
    p = jnp.dot(pool_o.astype(BF16), wpb_ref[...], preferred_element_type=F32)
    merged = jax.nn.sigmoid(ga_ref[0]) * a + jax.nn.sigmoid(gp_ref[0]) * p
    o = jnp.dot(merged.astype(BF16), wout_ref[...], preferred_element_type=F32)
    x1 = x_ref[0] + mod_ref[0, 2:3, :] * o
    x1_ref[0] = x1

    h2 = _norm_mod(x1, g2_ref[...], mod_ref[0, 3:4, :], mod_ref[0, 4:5, :])
    h2_ref[...] = h2
    logits = jnp.dot(h2.astype(BF16), wr_ref[...], preferred_element_type=F32) + br_ref[...]
    lane = lax.broadcasted_iota(jnp.int32, (tm, N_EXPERTS), 1).astype(F32)
    work = logits
    vals, idxs, sels = [], [], []
    for _ in range(TOP_K):
        mx = jnp.max(work, axis=-1, keepdims=True)
        ix = jnp.min(jnp.where(work == mx, lane, float(N_EXPERTS)), axis=-1, keepdims=True)
        sel = lane == ix
        work = jnp.where(sel, -jnp.inf, work)
        vals.append(mx)
        idxs.append(ix)
        sels.append(sel)
    ex = [jnp.exp(v - vals[0]) for v in vals]
    tot = ex[0] + ex[1] + ex[2] + ex[3]
    gate_ref[...] = jnp.concatenate([e_ / tot for e_ in ex], axis=1)
    idx_ref[...] = jnp.concatenate(idxs, axis=1).astype(jnp.int32)

    member = (sels[0] | sels[1] | sels[2] | sels[3]).astype(F32)
    r_id = lax.broadcasted_iota(jnp.int32, (tm, tm), 0)
    c_id = lax.broadcasted_iota(jnp.int32, (tm, tm), 1)
    tri = (r_id > c_id).astype(BF16)
    before = jnp.dot(tri, member.astype(BF16), preferred_element_type=F32) + carry_ref[...]
    ranks = [jnp.sum(jnp.where(s, before, 0.0), axis=-1, keepdims=True) for s in sels]
    rank_ref[...] = jnp.concatenate(ranks, axis=1).astype(jnp.int32)
    total = carry_ref[...] + jnp.sum(member, axis=0, keepdims=True)
    carry_ref[...] = total
    cnt_ref[...] = total


def _merge_route(attn_o, u, ga, gp, x, mod, w_pool_bf, pool_scale, wab, wpb, wout,
                 norm2_g, w_router, b_router):
    b, l, d = x.shape
    tm = TM_MERGE
    nt = l // tm
    n = b * l
    hb = tm // POOL_HALO
    row = lambda bi, i: (bi, i, 0)
    flat = lambda bi, i: (bi * nt + i, 0)
    halo_prev = lambda bi, i: (bi, jnp.maximum(i * hb - 1, 0), 0)
    halo_next = lambda bi, i: (bi, jnp.minimum((i + 1) * hb, l // POOL_HALO - 1), 0)
    out_shape = (
        jax.ShapeDtypeStruct((b, l, d), F32),
        jax.ShapeDtypeStruct((n, d), F32),
        jax.ShapeDtypeStruct((n, TOP_K), jnp.int32),
        jax.ShapeDtypeStruct((n, TOP_K), F32),
        jax.ShapeDtypeStruct((n, TOP_K), jnp.int32),
        jax.ShapeDtypeStruct((1, N_EXPERTS), F32),
    )
    return pl.pallas_call(
        functools.partial(_merge_kernel, seq_len=l),
        grid=(b, nt),
        in_specs=[pl.BlockSpec((1, tm, d), row),
                  pl.BlockSpec((1, tm, d), row),
                  pl.BlockSpec((1, POOL_HALO, d), halo_prev),
                  pl.BlockSpec((1, POOL_HALO, d), halo_next),
                  pl.BlockSpec((1, tm, d), row),
                  pl.BlockSpec((1, tm, d), row),
                  pl.BlockSpec((1, tm, d), row),
                  pl.BlockSpec((1, 6, d), lambda bi, i: (bi, 0, 0)),
                  _const_spec(w_pool_bf.shape),
                  _const_spec((1, d)),
                  _const_spec((d, d)), _const_spec((d, d)), _const_spec((d, d)),
                  _const_spec((1, d)),
                  _const_spec((d, N_EXPERTS)),
                  _const_spec((1, N_EXPERTS))],
        out_specs=(pl.BlockSpec((1, tm, d), row),
                   pl.BlockSpec((tm, d), flat),
                   pl.BlockSpec((tm, TOP_K), flat),
                   pl.BlockSpec((tm, TOP_K), flat),
                   pl.BlockSpec((tm, TOP_K), flat),
                   _const_spec((1, N_EXPERTS))),
        out_shape=out_shape,
        scratch_shapes=[pltpu.VMEM((1, N_EXPERTS), F32)],
        compiler_params=_params("arbitrary", "arbitrary"),
        name="merge_route",
    )(attn_o, u, u, u, ga, gp, x, mod, w_pool_bf, pool_scale, wab, wpb, wout,
      norm2_g, w_router, b_router)


ROW_ISSUE_UNROLL = 8


def _issue_rows(n_rows, start_row):
    def body(it, c):
        for dr in range(ROW_ISSUE_UNROLL):
            for k in range(TOP_K):
                start_row(it * ROW_ISSUE_UNROLL + dr, k, (dr * TOP_K + k) % 2)
        return c

    lax.fori_loop(0, n_rows // ROW_ISSUE_UNROLL, body, 0)


def _scatter_kernel(meta_ref, dest_ref, h_ref, xs_ref, zbuf, sem):
    i = pl.program_id(0)
    ts = TS_MOVE

    @pl.when(i == 0)
    def _():
        zbuf[...] = jnp.zeros_like(zbuf)

        def pad_copy(e):
            start = pl.multiple_of(meta_ref[e] - TM_EXPERT, TM_EXPERT)
            return pltpu.make_async_copy(zbuf, xs_ref.at[pl.ds(start, TM_EXPERT)], sem)

        n_blocks = xs_ref.shape[0] // TM_EXPERT

        def tail_copy(e):
            start = pl.multiple_of((meta_ref[2 * N_EXPERTS] + e) * TM_EXPERT, TM_EXPERT)
            return pltpu.make_async_copy(zbuf, xs_ref.at[pl.ds(start, TM_EXPERT)], sem)

        for e in range(N_EXPERTS):
            @pl.when(meta_ref[N_EXPERTS + e] > 0)
            def _():
                pad_copy(e).start()

            @pl.when(meta_ref[2 * N_EXPERTS] + e < n_blocks)
            def _():
                tail_copy(e).start()
        for e in range(N_EXPERTS):
            @pl.when(meta_ref[N_EXPERTS + e] > 0)
            def _():
                pad_copy(e).wait()

            @pl.when(meta_ref[2 * N_EXPERTS] + e < n_blocks)
            def _():
                tail_copy(e).wait()

    def start_row(r, k, priority):
        d = dest_ref[r * TOP_K + k]
        pltpu.make_async_copy(h_ref.at[pl.ds(r, 1)], xs_ref.at[pl.ds(d, 1)], sem).start(priority)

    _issue_rows(ts, start_row)
    for _ in range(TOP_K):
        pltpu.make_async_copy(h_ref, xs_ref.at[pl.ds(0, ts)], sem).wait()


def _scatter_rows(meta, dest_flat, h2, p_rows):
    n, d = h2.shape
    ts = TS_MOVE
    grid_spec = pltpu.PrefetchScalarGridSpec(
        num_scalar_prefetch=1,
        grid=(n // ts,),
        in_specs=[pl.BlockSpec((ts * TOP_K,), lambda i, m: (i,), memory_space=pltpu.SMEM),
                  pl.BlockSpec((ts, d), lambda i, m: (i, 0))],
        out_specs=pl.BlockSpec(memory_space=pl.ANY),
        scratch_shapes=[pltpu.VMEM((TM_EXPERT, d), F32), pltpu.SemaphoreType.DMA],
    )
    return pl.pallas_call(
        _scatter_kernel,
        grid_spec=grid_spec,
        out_shape=jax.ShapeDtypeStruct((p_rows, d), F32),
        compiler_params=_params("arbitrary"),
        name="moe_scatter",
    )(meta, dest_flat, h2)


def _expert_kernel(be_ref, first_ref, nu_ref, x_ref, wgu_ref, bgu_ref, wd_ref, bd_ref,
                   y_ref, wgu_s, wd_s):
    i = pl.program_id(0)

    @pl.when(i < nu_ref[0])
    def _():
        @pl.when(first_ref[i] == 1)
        def _():
            ck = 256
            for r0 in range(0, D_MODEL, ck):
                wgu_s[r0:r0 + ck, :] = wgu_ref[0, r0:r0 + ck, :].astype(BF16)
                wd_s[r0:r0 + ck, :] = wd_ref[0, r0:r0 + ck, :].astype(BF16)

        x = x_ref[...].astype(BF16)
        acc = jnp.zeros((TM_EXPERT, D_MODEL), F32)
        fc = 512
        for c0 in range(0, D_FF, fc):
            gt = jnp.dot(x, wgu_s[:, c0:c0 + fc], preferred_element_type=F32) + bgu_ref[0, :, c0:c0 + fc]
            ln = (jnp.dot(x, wgu_s[:, D_FF + c0:D_FF + c0 + fc], preferred_element_type=F32)
                  + bgu_ref[0, :, D_FF + c0:D_FF + c0 + fc])
            gt = jnp.minimum(gt, SWIGLU_LIMIT)
            ln = jnp.clip(ln, -SWIGLU_LIMIT, SWIGLU_LIMIT)
            act = gt * jax.nn.sigmoid(SWIGLU_ALPHA * gt) * (ln + 1.0)
            acc = acc + jnp.dot(act.astype(BF16), wd_s[c0:c0 + fc, :], preferred_element_type=F32)
        y_ref[...] = acc + bd_ref[0]

    @pl.when(i >= nu_ref[0])
    def _():
        y_ref[...] = jnp.zeros_like(y_ref)


def _experts(block_e, first, n_used, xs, w_gu, b_gu, w_down, b_down):
    p_rows, d = xs.shape
    tm = TM_EXPERT
    nb = p_rows // tm

    def blk(i, be, fi, nu):
        return jnp.minimum(i, nu[0] - 1)

    xmap = lambda i, be, fi, nu: (blk(i, be, fi, nu), 0)
    wmap = lambda i, be, fi, nu: (be[blk(i, be, fi, nu)], 0, 0)
    grid_spec = pltpu.PrefetchScalarGridSpec(
        num_scalar_prefetch=3,
        grid=(nb,),
        in_specs=[pl.BlockSpec((tm, d), xmap),
                  pl.BlockSpec((1, d, 2 * D_FF), wmap),
                  pl.BlockSpec((1, 1, 2 * D_FF), wmap),
                  pl.BlockSpec((1, D_FF, d), wmap),
                  pl.BlockSpec((1, 1, d), wmap)],
        out_specs=pl.BlockSpec((tm, d), lambda i, be, fi, nu: (i, 0)),
        scratch_shapes=[pltpu.VMEM((d, 2 * D_FF), BF16), pltpu.VMEM((D_FF, d), BF16)],
    )
    return pl.pallas_call(
        _expert_kernel,
        grid_spec=grid_spec,
        out_shape=jax.ShapeDtypeStruct((p_rows, d), F32),
        compiler_params=_params("arbitrary"),
        name="moe_experts",
    )(block_e, first, n_used, xs, w_gu, b_gu.reshape(N_EXPERTS, 1, 2 * D_FF),
      w_down, b_down.reshape(N_EXPERTS, 1, d))


def _combine_kernel(dest_ref, ys_ref, gate_ref, x1_ref, mod_ref, fg_ref, o_ref, buf, sem):
    ts = TS_MOVE

    def start_row(r, k, priority):
        d = dest_ref[r * TOP_K + k]
        pltpu.make_async_copy(ys_ref.at[pl.ds(d, 1)], buf.at[k, pl.ds(r, 1)], sem).start(priority)

    _issue_rows(ts, start_row)
    for k in range(TOP_K):
        pltpu.make_async_copy(ys_ref.at[pl.ds(0, ts)], buf.at[k], sem).wait()

    gate = gate_ref[...]
    y = gate[:, 0:1] * buf[0]
    for k in range(1, TOP_K):
        y = y + gate[:, k:k + 1] * buf[k]
    x2 = x1_ref[0] + mod_ref[0, 5:6, :] * y
    o_ref[0] = x2 * lax.rsqrt(jnp.mean(x2 * x2, axis=-1, keepdims=True) + NORM_EPS) * fg_ref[...]


def _combine(dest_flat, ys, gate_w, x1, mod, final_g):
    b, l, d = x1.shape
    ts = TS_MOVE
    nt = l // ts
    row = lambda bi, i: (bi, i, 0)
    return pl.pallas_call(
        _combine_kernel,
        grid=(b, nt),
        in_specs=[pl.BlockSpec((ts * TOP_K,), lambda bi, i: (bi * nt + i,), memory_space=pltpu.SMEM),
                  pl.BlockSpec(memory_space=pl.ANY),
                  pl.BlockSpec((ts, TOP_K), lambda bi, i: (bi * nt + i, 0)),
                  pl.BlockSpec((1, ts, d), row),
                  pl.BlockSpec((1, 6, d), lambda bi, i: (bi, 0, 0)),
                  _const_spec((1, d))],
        out_specs=pl.BlockSpec((1, ts, d), row),
        out_shape=jax.ShapeDtypeStruct((b, l, d), F32),
        scratch_shapes=[pltpu.VMEM((TOP_K, ts, d), F32), pltpu.SemaphoreType.DMA],
        compiler_params=_params("arbitrary", "arbitrary"),
        name="moe_combine",
    )(dest_flat, ys, gate_w, x1, mod, final_g)


def _rope_tables(seq_len):
    rows = seq_len // GRID_W
    row = jnp.broadcast_to(jnp.arange(rows, dtype=F32)[:, None], (rows, GRID_W)).reshape(-1)
    col = jnp.broadcast_to(jnp.arange(GRID_W, dtype=F32)[None, :], (rows, GRID_W)).reshape(-1)
    inv_freq = ROPE_BASE ** (-jnp.arange(ROPE_PAIRS, dtype=F32) / ROPE_PAIRS)
    tables = []
    for fn, sign in ((jnp.cos, 1.0), (jnp.sin, -1.0)):
        r, c = fn(row[:, None] * inv_freq), fn(col[:, None] * inv_freq)
        head = jnp.concatenate([sign * r, r, sign * c, c], axis=1)
        tables.append(jnp.concatenate([head] * (LANES // HEAD_DIM), axis=1))
    return tables


def kernel(x, c, ctx, c_ctx, w_ada, b_ada, norm1_g, norm2_g, w_in, b_in, attn_sink, w_pool,
           pool_scale, w_attn_br, w_pool_br, w_out, w_router, b_router, w_gu, b_gu, w_down,
           b_down, final_g):
    b, l, d = x.shape
    n = b * l
    assert w_ada.shape[0] == 1, "single-layer block"

    cond = jnp.zeros((SUBLANES, d), F32).at[:b].set(c).at[b].set(c_ctx)
    mod = _adaln(cond, w_ada[0], b_ada[0])[:b + 1].reshape(b + 1, 6, d)
    mod_x, mod_c = mod[:b], mod[b:b + 1]

    cos_t, sin_t = _rope_tables(l)
    w_in_bf = w_in[0].astype(BF16)
    b_in2 = b_in[0].reshape(1, IN_W)
    g1 = norm1_g[0].reshape(1, d)
    q, k, v, u, ga, gp = _inproj(x, mod_x, g1, w_in_bf, b_in2, cos_t, sin_t)
    kv_sl = slice(ATTN_W, ATTN_W + 2 * KV_W)
    kx, vx = _ctx_kv(ctx, mod_c, g1, w_in_bf[:, kv_sl], b_in2[:, kv_sl])

    attn_o = _attention(q, k, v, kx, vx, attn_sink[0])

    x1, h2, top_idx, gate_w, rank, counts = _merge_route(
        attn_o, u, ga, gp, x, mod_x, w_pool[0].astype(BF16), pool_scale[0].reshape(1, d),
        w_attn_br[0].astype(BF16), w_pool_br[0].astype(BF16), w_out[0].astype(BF16),
        norm2_g[0].reshape(1, d), w_router[0].astype(BF16), b_router[0].reshape(1, N_EXPERTS))

    tm = TM_EXPERT
    nb = n * TOP_K // tm + N_EXPERTS
    cnt = counts[0].astype(jnp.int32)
    padded = (cnt + tm - 1) // tm * tm
    pend = jnp.cumsum(padded)
    pstart = pend - padded
    expert_ids = jnp.arange(N_EXPERTS, dtype=jnp.int32)
    dest = (jnp.sum(jnp.where(top_idx[..., None] == expert_ids, pstart, 0), axis=-1)
            + rank).reshape(-1)
    n_used = (pend[-1] // tm).reshape(1)
    block_start = jnp.arange(nb, dtype=jnp.int32) * tm
    block_e = jnp.minimum(jnp.sum((pend[None, :] <= block_start[:, None]).astype(jnp.int32), axis=1),
                          N_EXPERTS - 1)
    first = jnp.concatenate([jnp.ones((1,), jnp.int32),
                             (block_e[1:] != block_e[:-1]).astype(jnp.int32)])

    xs = _scatter_rows(jnp.concatenate([pend, padded, n_used]), dest, h2, nb * tm)
    ys = _experts(block_e, first, n_used, xs, w_gu[0], b_gu[0], w_down[0], b_down[0])
    return _combine(dest, ys, gate_w, x1, mod_x, final_g.reshape(1, d))
```

```python
import functools

import jax
import jax.numpy as jnp
from jax import lax
from jax.experimental import pallas as pl
from jax.experimental.pallas import tpu as pltpu

D_MODEL = 1024
GRID_W = 64
HEAD_DIM = 64
N_HEADS = 16
N_KV_HEADS = 4
GROUP = N_HEADS // N_KV_HEADS
ATTN_W = N_HEADS * HEAD_DIM
KV_W = N_KV_HEADS * HEAD_DIM
WINDOW = 128
ATTN_SCALE = HEAD_DIM ** -0.5
ROPE_BASE = 10000.0
ROPE_PAIRS = HEAD_DIM // 4
POOL_WINDOWS = (2, 4, 8, 16)
POOL_GROUP_W = D_MODEL // len(POOL_WINDOWS)
IN_W = ATTN_W + 2 * KV_W + D_MODEL + 2 * D_MODEL
N_EXPERTS = 32
TOP_K = 4
D_FF = D_MODEL
SWIGLU_ALPHA = 1.702
SWIGLU_LIMIT = 7.0
NORM_EPS = 1e-5
NEG_INF = -1e30

LANES = 128
SUBLANES = 8
VMEM_LIMIT = 56 * 1024 * 1024

TM_INPROJ = 512
TQ = WINDOW
TM_MERGE = 256
TM_EXPERT = 256
TS_MOVE = 256
POOL_HALO = SUBLANES

F32 = jnp.float32
BF16 = jnp.bfloat16


ROW_WORDS = D_MODEL // 2
U32 = jnp.uint32


def _pack_rows(value):
    bits = lax.bitcast_convert_type(value.astype(BF16).astype(F32), U32)
    return bits[:, :ROW_WORDS] | (bits[:, ROW_WORDS:] >> 16)


def _unpack_rows(words):
    hi = lax.bitcast_convert_type(words & jnp.uint32(0xFFFF0000), F32)
    lo = lax.bitcast_convert_type(words << 16, F32)
    return jnp.concatenate([hi, lo], axis=1)


def _params(*sem):
    return pltpu.CompilerParams(dimension_semantics=sem, vmem_limit_bytes=VMEM_LIMIT)


def _const_spec(shape):
    nd = len(shape)
    return pl.BlockSpec(shape, lambda *_: (0,) * nd)


def _adaln_kernel(c_ref, w_ref, b_ref, o_ref):
    c = c_ref[...]
    s = c * jax.nn.sigmoid(c)
    o_ref[...] = jnp.dot(s.astype(BF16), w_ref[...].astype(BF16),
                         preferred_element_type=F32) + b_ref[...]


def _adaln(cond, w_ada, b_ada):
    rows, d = cond.shape
    n = w_ada.shape[1]
    tn = 1024
    return pl.pallas_call(
        _adaln_kernel,
        grid=(n // tn,),
        in_specs=[_const_spec((rows, d)),
                  pl.BlockSpec((d, tn), lambda j: (0, j)),
                  pl.BlockSpec((1, tn), lambda j: (0, j))],
        out_specs=pl.BlockSpec((rows, tn), lambda j: (0, j)),
        out_shape=jax.ShapeDtypeStruct((rows, n), F32),
        compiler_params=_params("arbitrary"),
        name="adaln",
    )(cond, w_ada, b_ada.reshape(1, n))


def _norm_mod(x, g, shift, scale):
    y = x * lax.rsqrt(jnp.mean(x * x, axis=-1, keepdims=True) + NORM_EPS) * g
    return y * (1.0 + scale) + shift


def _rope(t, cos, sin_signed):
    w = t.shape[1]
    half = ROPE_PAIRS
    lane = lax.broadcasted_iota(jnp.int32, t.shape, 1)
    first = (lane & (2 * half - 1)) < half
    rot = jnp.where(first, pltpu.roll(t, w - half, 1), pltpu.roll(t, half, 1))
    reps = w // LANES
    cos_w = jnp.concatenate([cos] * reps, axis=1) if reps > 1 else cos
    sin_w = jnp.concatenate([sin_signed] * reps, axis=1) if reps > 1 else sin_signed
    return t * cos_w + rot * sin_w


def _token_table(row_ref, col_ref):
    gr = row_ref.shape[0]
    shape = (gr, GRID_W, LANES)
    full = (jnp.broadcast_to(row_ref[...][:, None, :], shape)
            + jnp.broadcast_to(col_ref[...][None, :, :], shape))
    return full.reshape(gr * GRID_W, LANES)


def _inproj_kernel(x_ref, mod_ref, g_ref, w_ref, b_ref, cr_ref, sr_ref, cc_ref, sc_ref,
                   q_ref, k_ref, v_ref, u_ref, ga_ref, gp_ref):
    h = _norm_mod(x_ref[0], g_ref[...], mod_ref[0, 0:1, :], mod_ref[0, 1:2, :])
    hb = h.astype(BF16)
    cos = _token_table(cr_ref, cc_ref)
    sin = _token_table(sr_ref, sc_ref)

    def proj(c0, c1):
        return jnp.dot(hb, w_ref[:, c0:c1], preferred_element_type=F32) + b_ref[:, c0:c1]

    cw = 512
    for c0 in range(0, ATTN_W, cw):
        q_ref[0, :, c0:c0 + cw] = (_rope(proj(c0, c0 + cw), cos, sin) * ATTN_SCALE).astype(BF16)
    k_ref[0] = _rope(proj(ATTN_W, ATTN_W + KV_W), cos, sin).astype(BF16)
    v_ref[0] = proj(ATTN_W + KV_W, ATTN_W + 2 * KV_W).astype(BF16)
    base = ATTN_W + 2 * KV_W
    for ref in (u_ref, ga_ref, gp_ref):
        for c0 in range(0, D_MODEL, cw):
            ref[0, :, c0:c0 + cw] = proj(base + c0, base + c0 + cw)
        base += D_MODEL


def _inproj(x, mod, norm_g, w_in_bf, b_in, rope):
    b, l, d = x.shape
    tm = TM_INPROJ
    row = lambda bi, i: (bi, i, 0)
    out_shape = (
        jax.ShapeDtypeStruct((b, l, ATTN_W), BF16),
        jax.ShapeDtypeStruct((b, l, KV_W), BF16),
        jax.ShapeDtypeStruct((b, l, KV_W), BF16),
        jax.ShapeDtypeStruct((b, l, D_MODEL), F32),
        jax.ShapeDtypeStruct((b, l, D_MODEL), F32),
        jax.ShapeDtypeStruct((b, l, D_MODEL), F32),
    )
    return pl.pallas_call(
        _inproj_kernel,
        grid=(b, l // tm),
        in_specs=[pl.BlockSpec((1, tm, d), row),
                  pl.BlockSpec((1, 6, d), lambda bi, i: (bi, 0, 0)),
                  _const_spec((1, d)),
                  _const_spec((d, IN_W)),
                  _const_spec((1, IN_W)),
                  pl.BlockSpec((tm // GRID_W, LANES), lambda bi, i: (i, 0)),
                  pl.BlockSpec((tm // GRID_W, LANES), lambda bi, i: (i, 0)),
                  _const_spec((GRID_W, LANES)),
                  _const_spec((GRID_W, LANES))],
        out_specs=(pl.BlockSpec((1, tm, ATTN_W), row),
                   pl.BlockSpec((1, tm, KV_W), row),
                   pl.BlockSpec((1, tm, KV_W), row),
                   pl.BlockSpec((1, tm, D_MODEL), row),
                   pl.BlockSpec((1, tm, D_MODEL), row),
                   pl.BlockSpec((1, tm, D_MODEL), row)),
        out_shape=out_shape,
        compiler_params=_params("arbitrary", "arbitrary"),
        name="inproj",
    )(x, mod, norm_g, w_in_bf, b_in, *rope)


def _ctx_kv_kernel(x_ref, mod_ref, g_ref, w_ref, b_ref, k_ref, v_ref):
    h = _norm_mod(x_ref[0], g_ref[...], mod_ref[0, 0:1, :], mod_ref[0, 1:2, :])
    kv = jnp.dot(h.astype(BF16), w_ref[...], preferred_element_type=F32) + b_ref[...]
    k_ref[0] = kv[:, :KV_W].astype(BF16)
    v_ref[0] = kv[:, KV_W:].astype(BF16)


def _ctx_kv(ctx, mod_c, norm_g, w_kv_bf, b_kv):
    b, lc, d = ctx.shape
    row = lambda bi: (bi, 0, 0)
    return pl.pallas_call(
        _ctx_kv_kernel,
        grid=(b,),
        in_specs=[pl.BlockSpec((1, lc, d), row),
                  _const_spec((1, 6, d)),
                  _const_spec((1, d)),
                  _const_spec((d, 2 * KV_W)),
                  _const_spec((1, 2 * KV_W))],
        out_specs=(pl.BlockSpec((1, lc, KV_W), row), pl.BlockSpec((1, lc, KV_W), row)),
        out_shape=(jax.ShapeDtypeStruct((b, lc, KV_W), BF16),
                   jax.ShapeDtypeStruct((b, lc, KV_W), BF16)),
        compiler_params=_params("arbitrary"),
        name="ctx_kv",
    )(ctx, mod_c, norm_g, w_kv_bf, b_kv)


def _dot_nt(a, b):
    return lax.dot_general(a, b, (((1,), (1,)), ((), ())), preferred_element_type=F32)


def _attn_kernel(sink_ref, q_ref, kp_ref, kc_ref, kn_ref, vp_ref, vc_ref, vn_ref,
                 kx_ref, vx_ref, o_ref):
    n = pl.program_id(1)
    last = pl.num_programs(1) - 1
    rows = GROUP * TQ
    n_loc = 3 * TQ
    qi = lax.broadcasted_iota(jnp.int32, (rows, n_loc), 0) & (TQ - 1)
    col = lax.broadcasted_iota(jnp.int32, (rows, n_loc), 1)
    kj = col - TQ
    valid = (jnp.abs(qi - kj) <= WINDOW)
    valid = valid & ((col >= TQ) | (n > 0)) & ((col < 2 * TQ) | (n < last))
    row_id = lax.broadcasted_iota(jnp.int32, (rows, 1), 0)

    for j in range(N_KV_HEADS):
        hs = slice(j * HEAD_DIM, (j + 1) * HEAD_DIM)
        q4 = jnp.concatenate(
            [q_ref[0, :, (j * GROUP + g) * HEAD_DIM:(j * GROUP + g + 1) * HEAD_DIM]
             for g in range(GROUP)], axis=0)
        k_loc = jnp.concatenate([kp_ref[0, :, hs], kc_ref[0, :, hs], kn_ref[0, :, hs]], axis=0)
        v_loc = jnp.concatenate([vp_ref[0, :, hs], vc_ref[0, :, hs], vn_ref[0, :, hs]], axis=0)
        s_loc = jnp.where(valid, _dot_nt(q4, k_loc), NEG_INF)
        s_ctx = _dot_nt(q4, kx_ref[0, :, hs])
        sink = jnp.zeros((rows, 1), F32)
        for g in range(GROUP):
            sink = jnp.where(row_id // TQ == g, sink_ref[j * GROUP + g], sink)
        m = jnp.maximum(jnp.maximum(jnp.max(s_loc, axis=-1, keepdims=True),
                                    jnp.max(s_ctx, axis=-1, keepdims=True)), sink)
        p_loc = jnp.exp(s_loc - m)
        p_ctx = jnp.exp(s_ctx - m)
        denom = (jnp.sum(p_loc, axis=-1, keepdims=True) + jnp.sum(p_ctx, axis=-1, keepdims=True)
                 + jnp.exp(sink - m))
        o = (jnp.dot(p_loc.astype(BF16), v_loc, preferred_element_type=F32)
             + jnp.dot(p_ctx.astype(BF16), vx_ref[0, :, hs], preferred_element_type=F32))
        o = o / denom
        for g in range(GROUP):
            c0 = (j * GROUP + g) * HEAD_DIM
            o_ref[0, :, c0:c0 + HEAD_DIM] = o[g * TQ:(g + 1) * TQ, :].astype(BF16)


def _attention(q, k, v, kx, vx, sink):
    b, l, _ = q.shape
    lc = kx.shape[1]
    nblk = l // TQ
    cur = lambda bi, n: (bi, n, 0)
    prev = lambda bi, n: (bi, jnp.maximum(n - 1, 0), 0)
    nxt = lambda bi, n: (bi, jnp.minimum(n + 1, nblk - 1), 0)
    kvb = (1, TQ, KV_W)
    return pl.pallas_call(
        _attn_kernel,
        grid=(b, nblk),
        in_specs=[pl.BlockSpec(memory_space=pltpu.SMEM),
                  pl.BlockSpec((1, TQ, ATTN_W), cur),
                  pl.BlockSpec(kvb, prev), pl.BlockSpec(kvb, cur), pl.BlockSpec(kvb, nxt),
                  pl.BlockSpec(kvb, prev), pl.BlockSpec(kvb, cur), pl.BlockSpec(kvb, nxt),
                  pl.BlockSpec((1, lc, KV_W), lambda bi, n: (bi, 0, 0)),
                  pl.BlockSpec((1, lc, KV_W), lambda bi, n: (bi, 0, 0))],
        out_specs=pl.BlockSpec((1, TQ, ATTN_W), cur),
        out_shape=jax.ShapeDtypeStruct((b, l, ATTN_W), BF16),
        compiler_params=_params("arbitrary", "arbitrary"),
        name="attention",
    )(sink, q, k, k, k, v, v, v, kx, vx)


def _shift_rows(a, d):
    n = a.shape[0]
    return pltpu.roll(a, (-d) % n, 0)


def _merge_kernel(ao_ref, u_ref, up_ref, un_ref, ga_ref, gp_ref, x_ref, mod_ref,
                  wpool_ref, pscale_ref, wab_ref, wpb_ref, wout_ref, g2_ref, wr_ref, br_ref,
                  x1_ref, h2_ref, idx_ref, gate_ref, rank_ref, cnt_ref, carry_ref, *, seq_len):
    bi = pl.program_id(0)
    i = pl.program_id(1)
    last = pl.num_programs(1) - 1
    tm = TM_MERGE

    @pl.when((bi == 0) & (i == 0))
    def _():
        carry_ref[...] = jnp.zeros_like(carry_ref)

    u = u_ref[0]
    prev = jnp.where(i > 0, up_ref[0], 0.0)
    nxt = jnp.where(i < last, un_ref[0], 0.0)
    ext = jnp.concatenate([prev, u, nxt], axis=0)
    t = i * tm + lax.broadcasted_iota(jnp.int32, (tm, 1), 0)
    mixed = []
    for g, w in enumerate(POOL_WINDOWS):
        cs = slice(g * POOL_GROUP_W, (g + 1) * POOL_GROUP_W)
        e = ext[:, cs]
        acc = _shift_rows(e, -1) + e
        step = 1
        while 2 * step < w:
            acc = _shift_rows(acc, -step) + _shift_rows(acc, step)
            step *= 2
        win = acc[POOL_HALO:POOL_HALO + tm]
        half = w // 2
        cnt = (jnp.minimum(t + half, seq_len) - jnp.maximum(t - half, 0)).astype(F32)
        diff = (win / cnt - u[:, cs]).astype(BF16)
        mixed.append(jnp.dot(diff, wpool_ref[g], preferred_element_type=F32))
    pool_o = jnp.concatenate(mixed, axis=1) * pscale_ref[...]

    a = jnp.dot(ao_ref[0], wab_ref[...], preferred_element_type=F32)
    p = jnp.dot(pool_o.astype(BF16), wpb_ref[...], preferred_element_type=F32)
    merged = jax.nn.sigmoid(ga_ref[0]) * a + jax.nn.sigmoid(gp_ref[0]) * p
    o = jnp.dot(merged.astype(BF16), wout_ref[...], preferred_element_type=F32)
    x1 = x_ref[0] + mod_ref[0, 2:3, :] * o
    x1_ref[0] = x1

    h2 = _norm_mod(x1, g2_ref[...], mod_ref[0, 3:4, :], mod_ref[0, 4:5, :])
    h2_ref[...] = _pack_rows(h2)
    logits = jnp.dot(h2.astype(BF16), wr_ref[...], preferred_element_type=F32) + br_ref[...]
    lane = lax.broadcasted_iota(jnp.int32, (tm, N_EXPERTS), 1).astype(F32)
    work = logits
    vals, idxs, sels = [], [], []
    for _ in range(TOP_K):
        mx = jnp.max(work, axis=-1, keepdims=True)
        ix = jnp.min(jnp.where(work == mx, lane, float(N_EXPERTS)), axis=-1, keepdims=True)
        sel = lane == ix
        work = jnp.where(sel, -jnp.inf, work)
        vals.append(mx)
        idxs.append(ix)
        sels.append(sel)
    ex = [jnp.exp(v - vals[0]) for v in vals]
    tot = ex[0] + ex[1] + ex[2] + ex[3]
    gate_ref[...] = jnp.concatenate([e_ / tot for e_ in ex], axis=1)
    idx_ref[...] = jnp.concatenate(idxs, axis=1).astype(jnp.int32)

    member = (sels[0] | sels[1] | sels[2] | sels[3]).astype(F32)
    r_id = lax.broadcasted_iota(jnp.int32, (tm, tm), 0)
    c_id = lax.broadcasted_iota(jnp.int32, (tm, tm), 1)
    tri = (r_id > c_id).astype(BF16)
    before = jnp.dot(tri, member.astype(BF16), preferred_element_type=F32) + carry_ref[...]
    ranks = [jnp.sum(jnp.where(s, before, 0.0), axis=-1, keepdims=True) for s in sels]
    rank_ref[...] = jnp.concatenate(ranks, axis=1).astype(jnp.int32)
    total = carry_ref[...] + jnp.sum(member, axis=0, keepdims=True)
    carry_ref[...] = total
    cnt_ref[...] = total


def _merge_route(attn_o, u, ga, gp, x, mod, w_pool_bf, pool_scale, wab, wpb, wout,
                 norm2_g, w_router, b_router):
    b, l, d = x.shape
    tm = TM_MERGE
    nt = l // tm
    n = b * l
    hb = tm // POOL_HALO
    row = lambda bi, i: (bi, i, 0)
    flat = lambda bi, i: (bi * nt + i, 0)
    halo_prev = lambda bi, i: (bi, jnp.maximum(i * hb - 1, 0), 0)
    halo_next = lambda bi, i: (bi, jnp.minimum((i + 1) * hb, l // POOL_HALO - 1), 0)
    out_shape = (
        jax.ShapeDtypeStruct((b, l, d), F32),
        jax.ShapeDtypeStruct((n, ROW_WORDS), U32),
        jax.ShapeDtypeStruct((n, TOP_K), jnp.int32),
        jax.ShapeDtypeStruct((n, TOP_K), F32),
        jax.ShapeDtypeStruct((n, TOP_K), jnp.int32),
        jax.ShapeDtypeStruct((1, N_EXPERTS), F32),
    )
    return pl.pallas_call(
        functools.partial(_merge_kernel, seq_len=l),
        grid=(b, nt),
        in_specs=[pl.BlockSpec((1, tm, d), row),
                  pl.BlockSpec((1, tm, d), row),
                  pl.BlockSpec((1, POOL_HALO, d), halo_prev),
                  pl.BlockSpec((1, POOL_HALO, d), halo_next),
                  pl.BlockSpec((1, tm, d), row),
                  pl.BlockSpec((1, tm, d), row),
                  pl.BlockSpec((1, tm, d), row),
                  pl.BlockSpec((1, 6, d), lambda bi, i: (bi, 0, 0)),
                  _const_spec(w_pool_bf.shape),
                  _const_spec((1, d)),
                  _const_spec((d, d)), _const_spec((d, d)), _const_spec((d, d)),
                  _const_spec((1, d)),
                  _const_spec((d, N_EXPERTS)),
                  _const_spec((1, N_EXPERTS))],
        out_specs=(pl.BlockSpec((1, tm, d), row),
                   pl.BlockSpec((tm, ROW_WORDS), flat),
                   pl.BlockSpec((tm, TOP_K), flat),
                   pl.BlockSpec((tm, TOP_K), flat),
                   pl.BlockSpec((tm, TOP_K), flat),
                   _const_spec((1, N_EXPERTS))),
        out_shape=out_shape,
        scratch_shapes=[pltpu.VMEM((1, N_EXPERTS), F32)],
        compiler_params=_params("arbitrary", "arbitrary"),
        name="merge_route",
    )(attn_o, u, u, u, ga, gp, x, mod, w_pool_bf, pool_scale, wab, wpb, wout,
      norm2_g, w_router, b_router)


ROW_ISSUE_UNROLL = 8


def _issue_rows(n_rows, start_row):
    def body(it, c):
        for dr in range(ROW_ISSUE_UNROLL):
            for k in range(TOP_K):
                start_row(it * ROW_ISSUE_UNROLL + dr, k, (dr * TOP_K + k) % 2)
        return c

    lax.fori_loop(0, n_rows // ROW_ISSUE_UNROLL, body, 0)


def _scatter_kernel(meta_ref, dest_ref, h_ref, xs_ref, zbuf, sem):
    i = pl.program_id(0)
    ts = TS_MOVE

    @pl.when(i == 0)
    def _():
        zbuf[...] = jnp.zeros_like(zbuf)

        def pad_copy(e):
            start = pl.multiple_of(meta_ref[e] - TM_EXPERT, TM_EXPERT)
            return pltpu.make_async_copy(zbuf, xs_ref.at[pl.ds(start, TM_EXPERT)], sem)

        n_blocks = xs_ref.shape[0] // TM_EXPERT

        def tail_copy(e):
            start = pl.multiple_of((meta_ref[2 * N_EXPERTS] + e) * TM_EXPERT, TM_EXPERT)
            return pltpu.make_async_copy(zbuf, xs_ref.at[pl.ds(start, TM_EXPERT)], sem)

        for e in range(N_EXPERTS):
            @pl.when(meta_ref[N_EXPERTS + e] > 0)
            def _():
                pad_copy(e).start()

            @pl.when(meta_ref[2 * N_EXPERTS] + e < n_blocks)
            def _():
                tail_copy(e).start()
        for e in range(N_EXPERTS):
            @pl.when(meta_ref[N_EXPERTS + e] > 0)
            def _():
                pad_copy(e).wait()

            @pl.when(meta_ref[2 * N_EXPERTS] + e < n_blocks)
            def _():
                tail_copy(e).wait()

    def start_row(r, k, priority):
        d = dest_ref[r * TOP_K + k]
        pltpu.make_async_copy(h_ref.at[pl.ds(r, 1)], xs_ref.at[pl.ds(d, 1)], sem).start(priority)

    _issue_rows(ts, start_row)
    for _ in range(TOP_K):
        pltpu.make_async_copy(h_ref, xs_ref.at[pl.ds(0, ts)], sem).wait()


def _scatter_rows(meta, dest_flat, h2, p_rows):
    n = h2.shape[0]
    ts = TS_MOVE
    grid_spec = pltpu.PrefetchScalarGridSpec(
        num_scalar_prefetch=1,
        grid=(n // ts,),
        in_specs=[pl.BlockSpec((ts * TOP_K,), lambda i, m: (i,), memory_space=pltpu.SMEM),
                  pl.BlockSpec((ts, ROW_WORDS), lambda i, m: (i, 0))],
        out_specs=pl.BlockSpec(memory_space=pl.ANY),
        scratch_shapes=[pltpu.VMEM((TM_EXPERT, ROW_WORDS), U32), pltpu.SemaphoreType.DMA],
    )
    return pl.pallas_call(
        _scatter_kernel,
        grid_spec=grid_spec,
        out_shape=jax.ShapeDtypeStruct((p_rows, ROW_WORDS), U32),
        compiler_params=_params("arbitrary"),
        name="moe_scatter",
    )(meta, dest_flat, h2)


def _expert_kernel(be_ref, first_ref, nu_ref, x_ref, wgu_ref, bgu_ref, wd_ref, bd_ref,
                   y_ref, wgu_s, wd_s):
    i = pl.program_id(0)

    @pl.when(i < nu_ref[0])
    def _():
        @pl.when(first_ref[i] == 1)
        def _():
            ck = 256
            for r0 in range(0, D_MODEL, ck):
                wgu_s[r0:r0 + ck, :] = wgu_ref[0, r0:r0 + ck, :].astype(BF16)
                wd_s[r0:r0 + ck, :] = wd_ref[0, r0:r0 + ck, :].astype(BF16)

        x = _unpack_rows(x_ref[...]).astype(BF16)
        acc = jnp.zeros((TM_EXPERT, D_MODEL), F32)
        fc = 512
        for c0 in range(0, D_FF, fc):
            gt = jnp.dot(x, wgu_s[:, c0:c0 + fc], preferred_element_type=F32) + bgu_ref[0, :, c0:c0 + fc]
            ln = (jnp.dot(x, wgu_s[:, D_FF + c0:D_FF + c0 + fc], preferred_element_type=F32)
                  + bgu_ref[0, :, D_FF + c0:D_FF + c0 + fc])
            gt = jnp.minimum(gt, SWIGLU_LIMIT)
            ln = jnp.clip(ln, -SWIGLU_LIMIT, SWIGLU_LIMIT)
            act = gt * jax.nn.sigmoid(SWIGLU_ALPHA * gt) * (ln + 1.0)
            acc = acc + jnp.dot(act.astype(BF16), wd_s[c0:c0 + fc, :], preferred_element_type=F32)
        y_ref[...] = _pack_rows(acc + bd_ref[0])

    @pl.when(i >= nu_ref[0])
    def _():
        y_ref[...] = jnp.zeros_like(y_ref)


def _experts(block_e, first, n_used, xs, w_gu, b_gu, w_down, b_down):
    p_rows = xs.shape[0]
    d = D_MODEL
    tm = TM_EXPERT
    nb = p_rows // tm

    def blk(i, be, fi, nu):
        return jnp.minimum(i, nu[0] - 1)

    xmap = lambda i, be, fi, nu: (blk(i, be, fi, nu), 0)
    wmap = lambda i, be, fi, nu: (be[blk(i, be, fi, nu)], 0, 0)
    grid_spec = pltpu.PrefetchScalarGridSpec(
        num_scalar_prefetch=3,
        grid=(nb,),
        in_specs=[pl.BlockSpec((tm, ROW_WORDS), xmap),
                  pl.BlockSpec((1, d, 2 * D_FF), wmap),
                  pl.BlockSpec((1, 1, 2 * D_FF), wmap),
                  pl.BlockSpec((1, D_FF, d), wmap),
                  pl.BlockSpec((1, 1, d), wmap)],
        out_specs=pl.BlockSpec((tm, ROW_WORDS), lambda i, be, fi, nu: (i, 0)),
        scratch_shapes=[pltpu.VMEM((d, 2 * D_FF), BF16), pltpu.VMEM((D_FF, d), BF16)],
    )
    return pl.pallas_call(
        _expert_kernel,
        grid_spec=grid_spec,
        out_shape=jax.ShapeDtypeStruct((p_rows, ROW_WORDS), U32),
        compiler_params=_params("arbitrary"),
        name="moe_experts",
    )(block_e, first, n_used, xs, w_gu, b_gu.reshape(N_EXPERTS, 1, 2 * D_FF),
      w_down, b_down.reshape(N_EXPERTS, 1, d))


def _combine_kernel(dest_ref, ys_ref, gate_ref, x1_ref, mod_ref, fg_ref, o_ref, buf, sem):
    ts = TS_MOVE

    def start_row(r, k, priority):
        d = dest_ref[r * TOP_K + k]
        pltpu.make_async_copy(ys_ref.at[pl.ds(d, 1)], buf.at[k, pl.ds(r, 1)], sem).start(priority)

    _issue_rows(ts, start_row)
    for k in range(TOP_K):
        pltpu.make_async_copy(ys_ref.at[pl.ds(0, ts)], buf.at[k], sem).wait()

    gate = gate_ref[...]
    y = gate[:, 0:1] * _unpack_rows(buf[0])
    for k in range(1, TOP_K):
        y = y + gate[:, k:k + 1] * _unpack_rows(buf[k])
    x2 = x1_ref[0] + mod_ref[0, 5:6, :] * y
    o_ref[0] = x2 * lax.rsqrt(jnp.mean(x2 * x2, axis=-1, keepdims=True) + NORM_EPS) * fg_ref[...]


def _combine(dest_flat, ys, gate_w, x1, mod, final_g):
    b, l, d = x1.shape
    ts = TS_MOVE
    nt = l // ts
    row = lambda bi, i: (bi, i, 0)
    return pl.pallas_call(
        _combine_kernel,
        grid=(b, nt),
        in_specs=[pl.BlockSpec((ts * TOP_K,), lambda bi, i: (bi * nt + i,), memory_space=pltpu.SMEM),
                  pl.BlockSpec(memory_space=pl.ANY),
                  pl.BlockSpec((ts, TOP_K), lambda bi, i: (bi * nt + i, 0)),
                  pl.BlockSpec((1, ts, d), row),
                  pl.BlockSpec((1, 6, d), lambda bi, i: (bi, 0, 0)),
                  _const_spec((1, d))],
        out_specs=pl.BlockSpec((1, ts, d), row),
        out_shape=jax.ShapeDtypeStruct((b, l, d), F32),
        scratch_shapes=[pltpu.VMEM((TOP_K, ts, ROW_WORDS), U32), pltpu.SemaphoreType.DMA],
        compiler_params=_params("arbitrary", "arbitrary"),
        name="moe_combine",
    )(dest_flat, ys, gate_w, x1, mod, final_g)


def _rope_tables(seq_len):
    inv_freq = ROPE_BASE ** (-jnp.arange(ROPE_PAIRS, dtype=F32) / ROPE_PAIRS)
    reps = LANES // HEAD_DIM
    out = []
    for n_pos, on_row in ((seq_len // GRID_W, True), (GRID_W, False)):
        ang = jnp.arange(n_pos, dtype=F32)[:, None] * inv_freq
        zero = jnp.zeros_like(ang)
        for fn, sign in ((jnp.cos, 1.0), (jnp.sin, -1.0)):
            v = fn(ang)
            head = ([sign * v, v, zero, zero] if on_row else [zero, zero, sign * v, v])
            out.append(jnp.concatenate(head * reps, axis=1))
    return out


def kernel(x, c, ctx, c_ctx, w_ada, b_ada, norm1_g, norm2_g, w_in, b_in, attn_sink, w_pool,
           pool_scale, w_attn_br, w_pool_br, w_out, w_router, b_router, w_gu, b_gu, w_down,
           b_down, final_g):
    b, l, d = x.shape
    n = b * l
    assert w_ada.shape[0] == 1, "single-layer block"

    cond = jnp.zeros((SUBLANES, d), F32).at[:b].set(c).at[b].set(c_ctx)
    mod = _adaln(cond, w_ada[0], b_ada[0])[:b + 1].reshape(b + 1, 6, d)
    mod_x, mod_c = mod[:b], mod[b:b + 1]

    rope = _rope_tables(l)
    w_in_bf = w_in[0].astype(BF16)
    b_in2 = b_in[0].reshape(1, IN_W)
    g1 = norm1_g[0].reshape(1, d)
    q, k, v, u, ga, gp = _inproj(x, mod_x, g1, w_in_bf, b_in2, rope)
    kv_sl = slice(ATTN_W, ATTN_W + 2 * KV_W)
    kx, vx = _ctx_kv(ctx, mod_c, g1, w_in_bf[:, kv_sl], b_in2[:, kv_sl])

    attn_o = _attention(q, k, v, kx, vx, attn_sink[0])

    x1, h2, top_idx, gate_w, rank, counts = _merge_route(
        attn_o, u, ga, gp, x, mod_x, w_pool[0].astype(BF16), pool_scale[0].reshape(1, d),
        w_attn_br[0].astype(BF16), w_pool_br[0].astype(BF16), w_out[0].astype(BF16),
        norm2_g[0].reshape(1, d), w_router[0].astype(BF16), b_router[0].reshape(1, N_EXPERTS))

    tm = TM_EXPERT
    nb = n * TOP_K // tm + N_EXPERTS
    cnt = counts[0].astype(jnp.int32)
    padded = (cnt + tm - 1) // tm * tm
    pend = jnp.cumsum(padded)
    pstart = pend - padded
    expert_ids = jnp.arange(N_EXPERTS, dtype=jnp.int32)
    dest = (jnp.sum(jnp.where(top_idx[..., None] == expert_ids, pstart, 0), axis=-1)
            + rank).reshape(-1)
    n_used = (pend[-1] // tm).reshape(1)
    block_start = jnp.arange(nb, dtype=jnp.int32) * tm
    block_e = jnp.minimum(jnp.sum((pend[None, :] <= block_start[:, None]).astype(jnp.int32), axis=1),
                          N_EXPERTS - 1)
    first = jnp.concatenate([jnp.ones((1,), jnp.int32),
                             (block_e[1:] != block_e[:-1]).astype(jnp.int32)])

    xs = _scatter_rows(jnp.concatenate([pend, padded, n_used]), dest, h2, nb * tm)
    ys = _experts(block_e, first, n_used, xs, w_gu[0], b_gu[0], w_down[0], b_down[0])
    return _combine(dest, ys, gate_w, x1, mod_x, final_g.reshape(1, d))
```

```python
import functools

import jax
import jax.numpy as jnp
from jax import lax
from jax.experimental import pallas as pl
from jax.experimental.pallas import tpu as pltpu

D_MODEL = 1024
GRID_W = 64
HEAD_DIM = 64
N_HEADS = 16
N_KV_HEADS = 4
GROUP = N_HEADS // N_KV_HEADS
ATTN_W = N_HEADS * HEAD_DIM
KV_W = N_KV_HEADS * HEAD_DIM
WINDOW = 128
ATTN_SCALE = HEAD_DIM ** -0.5
ROPE_BASE = 10000.0
ROPE_PAIRS = HEAD_DIM // 4
POOL_WINDOWS = (2, 4, 8, 16)
POOL_GROUP_W = D_MODEL // len(POOL_WINDOWS)
IN_W = ATTN_W + 2 * KV_W + D_MODEL + 2 * D_MODEL
N_EXPERTS = 32
TOP_K = 4
D_FF = D_MODEL
SWIGLU_ALPHA = 1.702
SWIGLU_LIMIT = 7.0
NORM_EPS = 1e-5
NEG_INF = -1e30

LANES = 128
SUBLANES = 8
VMEM_LIMIT = 56 * 1024 * 1024

TM_INPROJ = 512
TQ = WINDOW
TM_MERGE = 256
TM_EXPERT = 256
TS_MOVE = 256
POOL_HALO = SUBLANES

F32 = jnp.float32
BF16 = jnp.bfloat16


ROW_WORDS = D_MODEL // 2
U32 = jnp.uint32


def _pack_rows(value):
    bits = lax.bitcast_convert_type(value.astype(BF16).astype(F32), U32)
    return bits[:, :ROW_WORDS] | (bits[:, ROW_WORDS:] >> 16)


def _unpack_rows(words):
    hi = lax.bitcast_convert_type(words & jnp.uint32(0xFFFF0000), F32)
    lo = lax.bitcast_convert_type(words << 16, F32)
    return jnp.concatenate([hi, lo], axis=1)


def _params(*sem):
    return pltpu.CompilerParams(dimension_semantics=sem, vmem_limit_bytes=VMEM_LIMIT)


def _const_spec(shape):
    nd = len(shape)
    return pl.BlockSpec(shape, lambda *_: (0,) * nd)


def _adaln_kernel(c_ref, w_ref, b_ref, o_ref):
    c = c_ref[...]
    s = c * jax.nn.sigmoid(c)
    o_ref[...] = jnp.dot(s.astype(BF16), w_ref[...].astype(BF16),
                         preferred_element_type=F32) + b_ref[...]


def _adaln(cond, w_ada, b_ada):
    rows, d = cond.shape
    n = w_ada.shape[1]
    tn = 1024
    return pl.pallas_call(
        _adaln_kernel,
        grid=(n // tn,),
        in_specs=[_const_spec((rows, d)),
                  pl.BlockSpec((d, tn), lambda j: (0, j)),
                  pl.BlockSpec((1, tn), lambda j: (0, j))],
        out_specs=pl.BlockSpec((rows, tn), lambda j: (0, j)),
        out_shape=jax.ShapeDtypeStruct((rows, n), F32),
        compiler_params=_params("arbitrary"),
        name="adaln",
    )(cond, w_ada, b_ada.reshape(1, n))


def _norm_mod(x, g, shift, scale):
    y = x * lax.rsqrt(jnp.mean(x * x, axis=-1, keepdims=True) + NORM_EPS) * g
    return y * (1.0 + scale) + shift


def _rope(t, cos, sin_signed):
    w = t.shape[1]
    half = ROPE_PAIRS
    lane = lax.broadcasted_iota(jnp.int32, t.shape, 1)
    first = (lane & (2 * half - 1)) < half
    rot = jnp.where(first, pltpu.roll(t, w - half, 1), pltpu.roll(t, half, 1))
    reps = w // LANES
    cos_w = jnp.concatenate([cos] * reps, axis=1) if reps > 1 else cos
    sin_w = jnp.concatenate([sin_signed] * reps, axis=1) if reps > 1 else sin_signed
    return t * cos_w + rot * sin_w


def _token_table(row_ref, col_ref):
    gr = row_ref.shape[0]
    shape = (gr, GRID_W, LANES)
    full = (jnp.broadcast_to(row_ref[...][:, None, :], shape)
            + jnp.broadcast_to(col_ref[...][None, :, :], shape))
    return full.reshape(gr * GRID_W, LANES)


def _inproj_kernel(x_ref, mod_ref, g_ref, w_ref, b_ref, cr_ref, sr_ref, cc_ref, sc_ref,
                   q_ref, k_ref, v_ref, u_ref, ga_ref, gp_ref):
    h = _norm_mod(x_ref[0], g_ref[...], mod_ref[0, 0:1, :], mod_ref[0, 1:2, :])
    hb = h.astype(BF16)
    cos = _token_table(cr_ref, cc_ref)
    sin = _token_table(sr_ref, sc_ref)

    def proj(c0, c1):
        return jnp.dot(hb, w_ref[:, c0:c1], preferred_element_type=F32) + b_ref[:, c0:c1]

    cw = 512
    for c0 in range(0, ATTN_W, cw):
        q_ref[0, :, c0:c0 + cw] = (_rope(proj(c0, c0 + cw), cos, sin) * ATTN_SCALE).astype(BF16)
    k_ref[0] = _rope(proj(ATTN_W, ATTN_W + KV_W), cos, sin).astype(BF16)
    v_ref[0] = proj(ATTN_W + KV_W, ATTN_W + 2 * KV_W).astype(BF16)
    base = ATTN_W + 2 * KV_W
    for ref in (u_ref, ga_ref, gp_ref):
        for c0 in range(0, D_MODEL, cw):
            ref[0, :, c0:c0 + cw] = proj(base + c0, base + c0 + cw)
        base += D_MODEL


def _inproj(x, mod, norm_g, w_in_bf, b_in, rope):
    b, l, d = x.shape
    tm = TM_INPROJ
    row = lambda bi, i: (bi, i, 0)
    out_shape = (
        jax.ShapeDtypeStruct((b, l, ATTN_W), BF16),
        jax.ShapeDtypeStruct((b, l, KV_W), BF16),
        jax.ShapeDtypeStruct((b, l, KV_W), BF16),
        jax.ShapeDtypeStruct((b, l, D_MODEL), F32),
        jax.ShapeDtypeStruct((b, l, D_MODEL), F32),
        jax.ShapeDtypeStruct((b, l, D_MODEL), F32),
    )
    return pl.pallas_call(
        _inproj_kernel,
        grid=(b, l // tm),
        in_specs=[pl.BlockSpec((1, tm, d), row),
                  pl.BlockSpec((1, 6, d), lambda bi, i: (bi, 0, 0)),
                  _const_spec((1, d)),
                  _const_spec((d, IN_W)),
                  _const_spec((1, IN_W)),
                  pl.BlockSpec((tm // GRID_W, LANES), lambda bi, i: (i, 0)),
                  pl.BlockSpec((tm // GRID_W, LANES), lambda bi, i: (i, 0)),
                  _const_spec((GRID_W, LANES)),
                  _const_spec((GRID_W, LANES))],
        out_specs=(pl.BlockSpec((1, tm, ATTN_W), row),
                   pl.BlockSpec((1, tm, KV_W), row),
                   pl.BlockSpec((1, tm, KV_W), row),
                   pl.BlockSpec((1, tm, D_MODEL), row),
                   pl.BlockSpec((1, tm, D_MODEL), row),
                   pl.BlockSpec((1, tm, D_MODEL), row)),
        out_shape=out_shape,
        compiler_params=_params("arbitrary", "arbitrary"),
        name="inproj",
    )(x, mod, norm_g, w_in_bf, b_in, *rope)


def _ctx_kv_kernel(x_ref, mod_ref, g_ref, w_ref, b_ref, k_ref, v_ref):
    h = _norm_mod(x_ref[0], g_ref[...], mod_ref[0, 0:1, :], mod_ref[0, 1:2, :])
    kv = jnp.dot(h.astype(BF16), w_ref[...], preferred_element_type=F32) + b_ref[...]
    k_ref[0] = kv[:, :KV_W].astype(BF16)
    v_ref[0] = kv[:, KV_W:].astype(BF16)


def _ctx_kv(ctx, mod_c, norm_g, w_kv_bf, b_kv):
    b, lc, d = ctx.shape
    row = lambda bi: (bi, 0, 0)
    return pl.pallas_call(
        _ctx_kv_kernel,
        grid=(b,),
        in_specs=[pl.BlockSpec((1, lc, d), row),
                  _const_spec((1, 6, d)),
                  _const_spec((1, d)),
                  _const_spec((d, 2 * KV_W)),
                  _const_spec((1, 2 * KV_W))],
        out_specs=(pl.BlockSpec((1, lc, KV_W), row), pl.BlockSpec((1, lc, KV_W), row)),
        out_shape=(jax.ShapeDtypeStruct((b, lc, KV_W), BF16),
                   jax.ShapeDtypeStruct((b, lc, KV_W), BF16)),
        compiler_params=_params("arbitrary"),
        name="ctx_kv",
    )(ctx, mod_c, norm_g, w_kv_bf, b_kv)


def _dot_nt(a, b):
    return lax.dot_general(a, b, (((1,), (1,)), ((), ())), preferred_element_type=F32)


def _attn_kernel(sink_ref, q_ref, kp_ref, kc_ref, kn_ref, vp_ref, vc_ref, vn_ref,
                 kx_ref, vx_ref, o_ref):
    n = pl.program_id(1)
    last = pl.num_programs(1) - 1
    rows = GROUP * TQ
    n_loc = 3 * TQ
    qi = lax.broadcasted_iota(jnp.int32, (rows, n_loc), 0) & (TQ - 1)
    col = lax.broadcasted_iota(jnp.int32, (rows, n_loc), 1)
    kj = col - TQ
    valid = (jnp.abs(qi - kj) <= WINDOW)
    valid = valid & ((col >= TQ) | (n > 0)) & ((col < 2 * TQ) | (n < last))
    row_id = lax.broadcasted_iota(jnp.int32, (rows, 1), 0)

    for j in range(N_KV_HEADS):
        hs = slice(j * HEAD_DIM, (j + 1) * HEAD_DIM)
        q4 = jnp.concatenate(
            [q_ref[0, :, (j * GROUP + g) * HEAD_DIM:(j * GROUP + g + 1) * HEAD_DIM]
             for g in range(GROUP)], axis=0)
        k_loc = jnp.concatenate([kp_ref[0, :, hs], kc_ref[0, :, hs], kn_ref[0, :, hs]], axis=0)
        v_loc = jnp.concatenate([vp_ref[0, :, hs], vc_ref[0, :, hs], vn_ref[0, :, hs]], axis=0)
        s_loc = jnp.where(valid, _dot_nt(q4, k_loc), NEG_INF)
        s_ctx = _dot_nt(q4, kx_ref[0, :, hs])
        sink = jnp.zeros((rows, 1), F32)
        for g in range(GROUP):
            sink = jnp.where(row_id // TQ == g, sink_ref[j * GROUP + g], sink)
        m = jnp.maximum(jnp.maximum(jnp.max(s_loc, axis=-1, keepdims=True),
                                    jnp.max(s_ctx, axis=-1, keepdims=True)), sink)
        p_loc = jnp.exp(s_loc - m)
        p_ctx = jnp.exp(s_ctx - m)
        denom = (jnp.sum(p_loc, axis=-1, keepdims=True) + jnp.sum(p_ctx, axis=-1, keepdims=True)
                 + jnp.exp(sink - m))
        o = (jnp.dot(p_loc.astype(BF16), v_loc, preferred_element_type=F32)
             + jnp.dot(p_ctx.astype(BF16), vx_ref[0, :, hs], preferred_element_type=F32))
        o = o / denom
        for g in range(GROUP):
            c0 = (j * GROUP + g) * HEAD_DIM
            o_ref[0, :, c0:c0 + HEAD_DIM] = o[g * TQ:(g + 1) * TQ, :].astype(BF16)


def _attention(q, k, v, kx, vx, sink):
    b, l, _ = q.shape
    lc = kx.shape[1]
    nblk = l // TQ
    cur = lambda bi, n: (bi, n, 0)
    prev = lambda bi, n: (bi, jnp.maximum(n - 1, 0), 0)
    nxt = lambda bi, n: (bi, jnp.minimum(n + 1, nblk - 1), 0)
    kvb = (1, TQ, KV_W)
    return pl.pallas_call(
        _attn_kernel,
        grid=(b, nblk),
        in_specs=[pl.BlockSpec(memory_space=pltpu.SMEM),
                  pl.BlockSpec((1, TQ, ATTN_W), cur),
                  pl.BlockSpec(kvb, prev), pl.BlockSpec(kvb, cur), pl.BlockSpec(kvb, nxt),
                  pl.BlockSpec(kvb, prev), pl.BlockSpec(kvb, cur), pl.BlockSpec(kvb, nxt),
                  pl.BlockSpec((1, lc, KV_W), lambda bi, n: (bi, 0, 0)),
                  pl.BlockSpec((1, lc, KV_W), lambda bi, n: (bi, 0, 0))],
        out_specs=pl.BlockSpec((1, TQ, ATTN_W), cur),
        out_shape=jax.ShapeDtypeStruct((b, l, ATTN_W), BF16),
        compiler_params=_params("arbitrary", "arbitrary"),
        name="attention",
    )(sink, q, k, k, k, v, v, v, kx, vx)


def _shift_rows(a, d):
    n = a.shape[0]
    return pltpu.roll(a, (-d) % n, 0)


def _merge_kernel(ao_ref, u_ref, up_ref, un_ref, ga_ref, gp_ref, x_ref, mod_ref,
                  wpool_ref, pscale_ref, wab_ref, wpb_ref, wout_ref, g2_ref, wr_ref, br_ref,
                  x1_ref, h2_ref, idx_ref, gate_ref, rank_ref, cnt_ref, carry_ref, *, seq_len):
    bi = pl.program_id(0)
    i = pl.program_id(1)
    last = pl.num_programs(1) - 1
    tm = TM_MERGE

    @pl.when((bi == 0) & (i == 0))
    def _():
        carry_ref[...] = jnp.zeros_like(carry_ref)

    u = u_ref[0]
    prev = jnp.where(i > 0, up_ref[0], 0.0)
    nxt = jnp.where(i < last, un_ref[0], 0.0)
    ext = jnp.concatenate([prev, u, nxt], axis=0)
    t = i * tm + lax.broadcasted_iota(jnp.int32, (tm, 1), 0)
    mixed = []
    for g, w in enumerate(POOL_WINDOWS):
        cs = slice(g * POOL_GROUP_W, (g + 1) * POOL_GROUP_W)
        e = ext[:, cs]
        acc = _shift_rows(e, -1) + e
        step = 1
        while 2 * step < w:
            acc = _shift_rows(acc, -step) + _shift_rows(acc, step)
            step *= 2
        win = acc[POOL_HALO:POOL_HALO + tm]
        half = w // 2
        cnt = (jnp.minimum(t + half, seq_len) - jnp.maximum(t - half, 0)).astype(F32)
        diff = (win / cnt - u[:, cs]).astype(BF16)
        mixed.append(jnp.dot(diff, wpool_ref[g], preferred_element_type=F32))
    pool_o = jnp.concatenate(mixed, axis=1) * pscale_ref[...]

    a = jnp.dot(ao_ref[0], wab_ref[...], preferred_element_type=F32)
    p = jnp.dot(pool_o.astype(BF16), wpb_ref[...], preferred_element_type=F32)
    merged = jax.nn.sigmoid(ga_ref[0]) * a + jax.nn.sigmoid(gp_ref[0]) * p
    o = jnp.dot(merged.astype(BF16), wout_ref[...], preferred_element_type=F32)
    x1 = x_ref[0] + mod_ref[0, 2:3, :] * o
    x1_ref[0] = x1

    h2 = _norm_mod(x1, g2_ref[...], mod_ref[0, 3:4, :], mod_ref[0, 4:5, :])
    h2_ref[...] = _pack_rows(h2)
    logits = jnp.dot(h2.astype(BF16), wr_ref[...], preferred_element_type=F32) + br_ref[...]
    lane = lax.broadcasted_iota(jnp.int32, (tm, N_EXPERTS), 1).astype(F32)
    work = logits
    vals, idxs, sels = [], [], []
    for _ in range(TOP_K):
        mx = jnp.max(work, axis=-1, keepdims=True)
        ix = jnp.min(jnp.where(work == mx, lane, float(N_EXPERTS)), axis=-1, keepdims=True)
        sel = lane == ix
        work = jnp.where(sel, -jnp.inf, work)
        vals.append(mx)
        idxs.append(ix)
        sels.append(sel)
    ex = [jnp.exp(v - vals[0]) for v in vals]
    tot = ex[0] + ex[1] + ex[2] + ex[3]
    gate_ref[...] = jnp.concatenate([e_ / tot for e_ in ex], axis=1)
    idx_ref[...] = jnp.concatenate(idxs, axis=1).astype(jnp.int32)

    member = (sels[0] | sels[1] | sels[2] | sels[3]).astype(F32)
    r_id = lax.broadcasted_iota(jnp.int32, (tm, tm), 0)
    c_id = lax.broadcasted_iota(jnp.int32, (tm, tm), 1)
    tri = (r_id > c_id).astype(BF16)
    before = jnp.dot(tri, member.astype(BF16), preferred_element_type=F32) + carry_ref[...]
    ranks = [jnp.sum(jnp.where(s, before, 0.0), axis=-1, keepdims=True) for s in sels]
    rank_ref[...] = jnp.concatenate(ranks, axis=1).astype(jnp.int32)
    total = carry_ref[...] + jnp.sum(member, axis=0, keepdims=True)
    carry_ref[...] = total
    cnt_ref[...] = total


def _merge_route(attn_o, u, ga, gp, x, mod, w_pool_bf, pool_scale, wab, wpb, wout,
                 norm2_g, w_router, b_router):
    b, l, d = x.shape
    tm = TM_MERGE
    nt = l // tm
    n = b * l
    hb = tm // POOL_HALO
    row = lambda bi, i: (bi, i, 0)
    flat = lambda bi, i: (bi * nt + i, 0)
    halo_prev = lambda bi, i: (bi, jnp.maximum(i * hb - 1, 0), 0)
    halo_next = lambda bi, i: (bi, jnp.minimum((i + 1) * hb, l // POOL_HALO - 1), 0)
    out_shape = (
        jax.ShapeDtypeStruct((b, l, d), F32),
        jax.ShapeDtypeStruct((n, ROW_WORDS), U32),
        jax.ShapeDtypeStruct((n, TOP_K), jnp.int32),
        jax.ShapeDtypeStruct((n, TOP_K), F32),
        jax.ShapeDtypeStruct((n, TOP_K), jnp.int32),
        jax.ShapeDtypeStruct((1, N_EXPERTS), F32),
    )
    return pl.pallas_call(
        functools.partial(_merge_kernel, seq_len=l),
        grid=(b, nt),
        in_specs=[pl.BlockSpec((1, tm, d), row),
                  pl.BlockSpec((1, tm, d), row),
                  pl.BlockSpec((1, POOL_HALO, d), halo_prev),
                  pl.BlockSpec((1, POOL_HALO, d), halo_next),
                  pl.BlockSpec((1, tm, d), row),
                  pl.BlockSpec((1, tm, d), row),
                  pl.BlockSpec((1, tm, d), row),
                  pl.BlockSpec((1, 6, d), lambda bi, i: (bi, 0, 0)),
                  _const_spec(w_pool_bf.shape),
                  _const_spec((1, d)),
                  _const_spec((d, d)), _const_spec((d, d)), _const_spec((d, d)),
                  _const_spec((1, d)),
                  _const_spec((d, N_EXPERTS)),
                  _const_spec((1, N_EXPERTS))],
        out_specs=(pl.BlockSpec((1, tm, d), row),
                   pl.BlockSpec((tm, ROW_WORDS), flat),
                   pl.BlockSpec((tm, TOP_K), flat),
                   pl.BlockSpec((tm, TOP_K), flat),
                   pl.BlockSpec((tm, TOP_K), flat),
                   _const_spec((1, N_EXPERTS))),
        out_shape=out_shape,
        scratch_shapes=[pltpu.VMEM((1, N_EXPERTS), F32)],
        compiler_params=_params("arbitrary", "arbitrary"),
        name="merge_route",
    )(attn_o, u, u, u, ga, gp, x, mod, w_pool_bf, pool_scale, wab, wpb, wout,
      norm2_g, w_router, b_router)


ROW_ISSUE_UNROLL = 8


def _issue_rows(n_rows, start_row):
    def body(it, c):
        for dr in range(ROW_ISSUE_UNROLL):
            for k in range(TOP_K):
                start_row(it * ROW_ISSUE_UNROLL + dr, k, (dr * TOP_K + k) % 2)
        return c

    lax.fori_loop(0, n_rows // ROW_ISSUE_UNROLL, body, 0)


def _scatter_kernel(meta_ref, dest_ref, h_ref, xs_ref, zbuf, sem):
    i = pl.program_id(0)
    ts = TS_MOVE

    @pl.when(i == 0)
    def _():
        zbuf[...] = jnp.zeros_like(zbuf)

        def pad_copy(e):
            start = pl.multiple_of(meta_ref[e] - TM_EXPERT, TM_EXPERT)
            return pltpu.make_async_copy(zbuf, xs_ref.at[pl.ds(start, TM_EXPERT)], sem)

        n_blocks = xs_ref.shape[0] // TM_EXPERT

        def tail_copy(e):
            start = pl.multiple_of((meta_ref[2 * N_EXPERTS] + e) * TM_EXPERT, TM_EXPERT)
            return pltpu.make_async_copy(zbuf, xs_ref.at[pl.ds(start, TM_EXPERT)], sem)

        for e in range(N_EXPERTS):
            @pl.when(meta_ref[N_EXPERTS + e] > 0)
            def _():
                pad_copy(e).start()

            @pl.when(meta_ref[2 * N_EXPERTS] + e < n_blocks)
            def _():
                tail_copy(e).start()
        for e in range(N_EXPERTS):
            @pl.when(meta_ref[N_EXPERTS + e] > 0)
            def _():
                pad_copy(e).wait()

            @pl.when(meta_ref[2 * N_EXPERTS] + e < n_blocks)
            def _():
                tail_copy(e).wait()

    def start_row(r, k, priority):
        d = dest_ref[r * TOP_K + k]
        pltpu.make_async_copy(h_ref.at[pl.ds(r, 1)], xs_ref.at[pl.ds(d, 1)], sem).start(priority)

    _issue_rows(ts, start_row)
    for _ in range(TOP_K):
        pltpu.make_async_copy(h_ref, xs_ref.at[pl.ds(0, ts)], sem).wait()


def _scatter_rows(meta, dest_flat, h2, p_rows):
    n = h2.shape[0]
    ts = TS_MOVE
    grid_spec = pltpu.PrefetchScalarGridSpec(
        num_scalar_prefetch=1,
        grid=(n // ts,),
        in_specs=[pl.BlockSpec((ts * TOP_K,), lambda i, m: (i,), memory_space=pltpu.SMEM),
                  pl.BlockSpec((ts, ROW_WORDS), lambda i, m: (i, 0))],
        out_specs=pl.BlockSpec(memory_space=pl.ANY),
        scratch_shapes=[pltpu.VMEM((TM_EXPERT, ROW_WORDS), U32), pltpu.SemaphoreType.DMA],
    )
    return pl.pallas_call(
        _scatter_kernel,
        grid_spec=grid_spec,
        out_shape=jax.ShapeDtypeStruct((p_rows, ROW_WORDS), U32),
        compiler_params=_params("arbitrary"),
        name="moe_scatter",
    )(meta, dest_flat, h2)


def _expert_kernel(be_ref, first_ref, next_ref, nu_ref, x_ref, wgu_hbm, bgu_ref, wd_hbm, bd_ref,
                   y_ref, wgu_f, wd_f, wgu_s, wd_s, sem):
    i = pl.program_id(0)

    def weight_copies(e):
        return (pltpu.make_async_copy(wgu_hbm.at[e], wgu_f, sem.at[0]),
                pltpu.make_async_copy(wd_hbm.at[e], wd_f, sem.at[1]))

    @pl.when(i == 0)
    def _():
        for cp in weight_copies(be_ref[0]):
            cp.start()

    @pl.when(i < nu_ref[0])
    def _():
        @pl.when(first_ref[i] == 1)
        def _():
            for cp in weight_copies(be_ref[i]):
                cp.wait()
            ck = 256
            for r0 in range(0, D_MODEL, ck):
                wgu_s[r0:r0 + ck, :] = wgu_f[r0:r0 + ck, :].astype(BF16)
                wd_s[r0:r0 + ck, :] = wd_f[r0:r0 + ck, :].astype(BF16)

            @pl.when(next_ref[i] >= 0)
            def _():
                for cp in weight_copies(next_ref[i]):
                    cp.start()

        x = _unpack_rows(x_ref[...]).astype(BF16)
        acc = jnp.zeros((TM_EXPERT, D_MODEL), F32)
        fc = 512
        for c0 in range(0, D_FF, fc):
            gt = jnp.dot(x, wgu_s[:, c0:c0 + fc], preferred_element_type=F32) + bgu_ref[0, :, c0:c0 + fc]
            ln = (jnp.dot(x, wgu_s[:, D_FF + c0:D_FF + c0 + fc], preferred_element_type=F32)
                  + bgu_ref[0, :, D_FF + c0:D_FF + c0 + fc])
            gt = jnp.minimum(gt, SWIGLU_LIMIT)
            ln = jnp.clip(ln, -SWIGLU_LIMIT, SWIGLU_LIMIT)
            act = gt * jax.nn.sigmoid(SWIGLU_ALPHA * gt) * (ln + 1.0)
            acc = acc + jnp.dot(act.astype(BF16), wd_s[c0:c0 + fc, :], preferred_element_type=F32)
        y_ref[...] = _pack_rows(acc + bd_ref[0])

    @pl.when(i >= nu_ref[0])
    def _():
        y_ref[...] = jnp.zeros_like(y_ref)


def _experts(block_e, first, next_e, n_used, xs, w_gu, b_gu, w_down, b_down):
    p_rows = xs.shape[0]
    d = D_MODEL
    tm = TM_EXPERT
    nb = p_rows // tm

    def blk(i, be, fi, nx, nu):
        return jnp.minimum(i, nu[0] - 1)

    xmap = lambda i, be, fi, nx, nu: (blk(i, be, fi, nx, nu), 0)
    bmap = lambda i, be, fi, nx, nu: (be[blk(i, be, fi, nx, nu)], 0, 0)
    grid_spec = pltpu.PrefetchScalarGridSpec(
        num_scalar_prefetch=4,
        grid=(nb,),
        in_specs=[pl.BlockSpec((tm, ROW_WORDS), xmap),
                  pl.BlockSpec(memory_space=pl.ANY),
                  pl.BlockSpec((1, 1, 2 * D_FF), bmap),
                  pl.BlockSpec(memory_space=pl.ANY),
                  pl.BlockSpec((1, 1, d), bmap)],
        out_specs=pl.BlockSpec((tm, ROW_WORDS), lambda i, be, fi, nx, nu: (i, 0)),
        scratch_shapes=[pltpu.VMEM((d, 2 * D_FF), F32), pltpu.VMEM((D_FF, d), F32),
                        pltpu.VMEM((d, 2 * D_FF), BF16), pltpu.VMEM((D_FF, d), BF16),
                        pltpu.SemaphoreType.DMA((2,))],
    )
    return pl.pallas_call(
        _expert_kernel,
        grid_spec=grid_spec,
        out_shape=jax.ShapeDtypeStruct((p_rows, ROW_WORDS), U32),
        compiler_params=_params("arbitrary"),
        name="moe_experts",
    )(block_e, first, next_e, n_used, xs, w_gu, b_gu.reshape(N_EXPERTS, 1, 2 * D_FF),
      w_down, b_down.reshape(N_EXPERTS, 1, d))


def _combine_kernel(dest_ref, dest_next_ref, ys_ref, gate_ref, x1_ref, mod_ref, fg_ref, o_ref,
                    buf, sem):
    ts = TS_MOVE
    s = pl.program_id(0)
    last = pl.num_programs(0) - 1
    slot = s % 2

    def gather(d_ref, sl):
        def start_row(r, k, priority):
            d = d_ref[r * TOP_K + k]
            pltpu.make_async_copy(ys_ref.at[pl.ds(d, 1)], buf.at[sl, k, pl.ds(r, 1)],
                                  sem.at[sl]).start(priority)

        _issue_rows(ts, start_row)

    @pl.when(s == 0)
    def _():
        gather(dest_ref, 0)

    @pl.when(s < last)
    def _():
        gather(dest_next_ref, 1 - slot)

    for k in range(TOP_K):
        pltpu.make_async_copy(ys_ref.at[pl.ds(0, ts)], buf.at[slot, k], sem.at[slot]).wait()

    gate = gate_ref[...]
    y = gate[:, 0:1] * _unpack_rows(buf[slot, 0])
    for k in range(1, TOP_K):
        y = y + gate[:, k:k + 1] * _unpack_rows(buf[slot, k])
    x2 = x1_ref[...] + mod_ref[0, 5:6, :] * y
    o_ref[...] = x2 * lax.rsqrt(jnp.mean(x2 * x2, axis=-1, keepdims=True) + NORM_EPS) * fg_ref[...]


def _combine(dest_flat, ys, gate_w, x1, mod, final_g):
    b, l, d = x1.shape
    ts = TS_MOVE
    nt = l // ts
    steps = b * nt
    return pl.pallas_call(
        _combine_kernel,
        grid=(steps,),
        in_specs=[pl.BlockSpec((ts * TOP_K,), lambda s: (s,), memory_space=pltpu.SMEM),
                  pl.BlockSpec((ts * TOP_K,), lambda s: (jnp.minimum(s + 1, steps - 1),),
                               memory_space=pltpu.SMEM),
                  pl.BlockSpec(memory_space=pl.ANY),
                  pl.BlockSpec((ts, TOP_K), lambda s: (s, 0)),
                  pl.BlockSpec((ts, d), lambda s: (s, 0)),
                  pl.BlockSpec((1, 6, d), lambda s: (s // nt, 0, 0)),
                  _const_spec((1, d))],
        out_specs=pl.BlockSpec((ts, d), lambda s: (s, 0)),
        out_shape=jax.ShapeDtypeStruct((b * l, d), F32),
        scratch_shapes=[pltpu.VMEM((2, TOP_K, ts, ROW_WORDS), U32), pltpu.SemaphoreType.DMA((2,))],
        compiler_params=_params("arbitrary"),
        name="moe_combine",
    )(dest_flat, dest_flat, ys, gate_w, x1.reshape(b * l, d), mod, final_g).reshape(b, l, d)


def _rope_tables(seq_len):
    inv_freq = ROPE_BASE ** (-jnp.arange(ROPE_PAIRS, dtype=F32) / ROPE_PAIRS)
    reps = LANES // HEAD_DIM
    out = []
    for n_pos, on_row in ((seq_len // GRID_W, True), (GRID_W, False)):
        ang = jnp.arange(n_pos, dtype=F32)[:, None] * inv_freq
        zero = jnp.zeros_like(ang)
        for fn, sign in ((jnp.cos, 1.0), (jnp.sin, -1.0)):
            v = fn(ang)
            head = ([sign * v, v, zero, zero] if on_row else [zero, zero, sign * v, v])
            out.append(jnp.concatenate(head * reps, axis=1))
    return out


def kernel(x, c, ctx, c_ctx, w_ada, b_ada, norm1_g, norm2_g, w_in, b_in, attn_sink, w_pool,
           pool_scale, w_attn_br, w_pool_br, w_out, w_router, b_router, w_gu, b_gu, w_down,
           b_down, final_g):
    b, l, d = x.shape
    n = b * l
    assert w_ada.shape[0] == 1, "single-layer block"

    cond = jnp.zeros((SUBLANES, d), F32).at[:b].set(c).at[b].set(c_ctx)
    mod = _adaln(cond, w_ada[0], b_ada[0])[:b + 1].reshape(b + 1, 6, d)
    mod_x, mod_c = mod[:b], mod[b:b + 1]

    rope = _rope_tables(l)
    w_in_bf = w_in[0].astype(BF16)
    b_in2 = b_in[0].reshape(1, IN_W)
    g1 = norm1_g[0].reshape(1, d)
    q, k, v, u, ga, gp = _inproj(x, mod_x, g1, w_in_bf, b_in2, rope)
    kv_sl = slice(ATTN_W, ATTN_W + 2 * KV_W)
    kx, vx = _ctx_kv(ctx, mod_c, g1, w_in_bf[:, kv_sl], b_in2[:, kv_sl])

    attn_o = _attention(q, k, v, kx, vx, attn_sink[0])

    x1, h2, top_idx, gate_w, rank, counts = _merge_route(
        attn_o, u, ga, gp, x, mod_x, w_pool[0].astype(BF16), pool_scale[0].reshape(1, d),
        w_attn_br[0].astype(BF16), w_pool_br[0].astype(BF16), w_out[0].astype(BF16),
        norm2_g[0].reshape(1, d), w_router[0].astype(BF16), b_router[0].reshape(1, N_EXPERTS))

    tm = TM_EXPERT
    nb = n * TOP_K // tm + N_EXPERTS
    cnt = counts[0].astype(jnp.int32)
    padded = (cnt + tm - 1) // tm * tm
    pend = jnp.cumsum(padded)
    pstart = pend - padded
    expert_ids = jnp.arange(N_EXPERTS, dtype=jnp.int32)
    dest = (jnp.sum(jnp.where(top_idx[..., None] == expert_ids, pstart, 0), axis=-1)
            + rank).reshape(-1)
    n_used = (pend[-1] // tm).reshape(1)
    block_start = jnp.arange(nb, dtype=jnp.int32) * tm
    block_e = jnp.minimum(jnp.sum((pend[None, :] <= block_start[:, None]).astype(jnp.int32), axis=1),
                          N_EXPERTS - 1)
    first = jnp.concatenate([jnp.ones((1,), jnp.int32),
                             (block_e[1:] != block_e[:-1]).astype(jnp.int32)])
    later_used = (expert_ids[None, :] > expert_ids[:, None]) & (cnt[None, :] > 0)
    next_of_expert = jnp.min(jnp.where(later_used, expert_ids[None, :], N_EXPERTS), axis=1)
    next_of_expert = jnp.where(next_of_expert < N_EXPERTS, next_of_expert, -1)
    next_e = jnp.sum(jnp.where(block_e[:, None] == expert_ids, next_of_expert, 0), axis=-1)

    xs = _scatter_rows(jnp.concatenate([pend, padded, n_used]), dest, h2, nb * tm)
    ys = _experts(block_e, first, next_e, n_used, xs, w_gu[0], b_gu[0], w_down[0], b_down[0])
    return _combine(dest, ys, gate_w, x1, mod_x, final_g.reshape(1, d))
```

```python
import functools

import jax
import jax.numpy as jnp
from jax import lax
from jax.experimental import pallas as pl
from jax.experimental.pallas import tpu as pltpu
from jax.experimental.pallas import tpu_sc as plsc

D_MODEL = 1024
GRID_W = 64
HEAD_DIM = 64
N_HEADS = 16
N_KV_HEADS = 4
GROUP = N_HEADS // N_KV_HEADS
ATTN_W = N_HEADS * HEAD_DIM
KV_W = N_KV_HEADS * HEAD_DIM
WINDOW = 128
ATTN_SCALE = HEAD_DIM ** -0.5
ROPE_BASE = 10000.0
ROPE_PAIRS = HEAD_DIM // 4
POOL_WINDOWS = (2, 4, 8, 16)
POOL_GROUP_W = D_MODEL // len(POOL_WINDOWS)
IN_W = ATTN_W + 2 * KV_W + D_MODEL + 2 * D_MODEL
N_EXPERTS = 32
TOP_K = 4
D_FF = D_MODEL
SWIGLU_ALPHA = 1.702
SWIGLU_LIMIT = 7.0
NORM_EPS = 1e-5
NEG_INF = -1e30

LANES = 128
SUBLANES = 8
VMEM_LIMIT = 56 * 1024 * 1024

TM_INPROJ = 512
TQ = WINDOW
TM_MERGE = 256
TM_EXPERT = 256
TS_MOVE = 256
POOL_HALO = SUBLANES

F32 = jnp.float32
BF16 = jnp.bfloat16


ROW_WORDS = D_MODEL // 2
U32 = jnp.uint32


def _pack_rows(value):
    bits = lax.bitcast_convert_type(value.astype(BF16).astype(F32), U32)
    return bits[:, :ROW_WORDS] | (bits[:, ROW_WORDS:] >> 16)


def _unpack_rows(words):
    hi = lax.bitcast_convert_type(words & jnp.uint32(0xFFFF0000), F32)
    lo = lax.bitcast_convert_type(words << 16, F32)
    return jnp.concatenate([hi, lo], axis=1)


def _params(*sem):
    return pltpu.CompilerParams(dimension_semantics=sem, vmem_limit_bytes=VMEM_LIMIT)


def _const_spec(shape):
    nd = len(shape)
    return pl.BlockSpec(shape, lambda *_: (0,) * nd)


def _adaln_kernel(c_ref, w_ref, b_ref, o_ref):
    c = c_ref[...]
    s = c * jax.nn.sigmoid(c)
    o_ref[...] = jnp.dot(s.astype(BF16), w_ref[...].astype(BF16),
                         preferred_element_type=F32) + b_ref[...]


def _adaln(cond, w_ada, b_ada):
    rows, d = cond.shape
    n = w_ada.shape[1]
    tn = 1024
    return pl.pallas_call(
        _adaln_kernel,
        grid=(n // tn,),
        in_specs=[_const_spec((rows, d)),
                  pl.BlockSpec((d, tn), lambda j: (0, j)),
                  pl.BlockSpec((1, tn), lambda j: (0, j))],
        out_specs=pl.BlockSpec((rows, tn), lambda j: (0, j)),
        out_shape=jax.ShapeDtypeStruct((rows, n), F32),
        compiler_params=_params("arbitrary"),
        name="adaln",
    )(cond, w_ada, b_ada.reshape(1, n))


def _norm_mod(x, g, shift, scale):
    y = x * lax.rsqrt(jnp.mean(x * x, axis=-1, keepdims=True) + NORM_EPS) * g
    return y * (1.0 + scale) + shift


def _rope(t, cos, sin_signed):
    w = t.shape[1]
    half = ROPE_PAIRS
    lane = lax.broadcasted_iota(jnp.int32, t.shape, 1)
    first = (lane & (2 * half - 1)) < half
    rot = jnp.where(first, pltpu.roll(t, w - half, 1), pltpu.roll(t, half, 1))
    reps = w // LANES
    cos_w = jnp.concatenate([cos] * reps, axis=1) if reps > 1 else cos
    sin_w = jnp.concatenate([sin_signed] * reps, axis=1) if reps > 1 else sin_signed
    return t * cos_w + rot * sin_w


def _token_table(row_ref, col_ref):
    gr = row_ref.shape[0]
    shape = (gr, GRID_W, LANES)
    full = (jnp.broadcast_to(row_ref[...][:, None, :], shape)
            + jnp.broadcast_to(col_ref[...][None, :, :], shape))
    return full.reshape(gr * GRID_W, LANES)


def _inproj_kernel(x_ref, mod_ref, g_ref, w_ref, b_ref, cr_ref, sr_ref, cc_ref, sc_ref,
                   q_ref, k_ref, v_ref, u_ref, ga_ref, gp_ref):
    h = _norm_mod(x_ref[0], g_ref[...], mod_ref[0, 0:1, :], mod_ref[0, 1:2, :])
    hb = h.astype(BF16)
    cos = _token_table(cr_ref, cc_ref)
    sin = _token_table(sr_ref, sc_ref)

    def proj(c0, c1):
        return jnp.dot(hb, w_ref[:, c0:c1], preferred_element_type=F32) + b_ref[:, c0:c1]

    cw = 512
    for c0 in range(0, ATTN_W, cw):
        q_ref[0, :, c0:c0 + cw] = (_rope(proj(c0, c0 + cw), cos, sin) * ATTN_SCALE).astype(BF16)
    k_ref[0] = _rope(proj(ATTN_W, ATTN_W + KV_W), cos, sin).astype(BF16)
    v_ref[0] = proj(ATTN_W + KV_W, ATTN_W + 2 * KV_W).astype(BF16)
    base = ATTN_W + 2 * KV_W
    for ref in (u_ref, ga_ref, gp_ref):
        for c0 in range(0, D_MODEL, cw):
            ref[0, :, c0:c0 + cw] = proj(base + c0, base + c0 + cw)
        base += D_MODEL


def _inproj(x, mod, norm_g, w_in_bf, b_in, rope):
    b, l, d = x.shape
    tm = TM_INPROJ
    row = lambda bi, i: (bi, i, 0)
    out_shape = (
        jax.ShapeDtypeStruct((b, l, ATTN_W), BF16),
        jax.ShapeDtypeStruct((b, l, KV_W), BF16),
        jax.ShapeDtypeStruct((b, l, KV_W), BF16),
        jax.ShapeDtypeStruct((b, l, D_MODEL), F32),
        jax.ShapeDtypeStruct((b, l, D_MODEL), F32),
        jax.ShapeDtypeStruct((b, l, D_MODEL), F32),
    )
    return pl.pallas_call(
        _inproj_kernel,
        grid=(b, l // tm),
        in_specs=[pl.BlockSpec((1, tm, d), row),
                  pl.BlockSpec((1, 6, d), lambda bi, i: (bi, 0, 0)),
                  _const_spec((1, d)),
                  _const_spec((d, IN_W)),
                  _const_spec((1, IN_W)),
                  pl.BlockSpec((tm // GRID_W, LANES), lambda bi, i: (i, 0)),
                  pl.BlockSpec((tm // GRID_W, LANES), lambda bi, i: (i, 0)),
                  _const_spec((GRID_W, LANES)),
                  _const_spec((GRID_W, LANES))],
        out_specs=(pl.BlockSpec((1, tm, ATTN_W), row),
                   pl.BlockSpec((1, tm, KV_W), row),
                   pl.BlockSpec((1, tm, KV_W), row),
                   pl.BlockSpec((1, tm, D_MODEL), row),
                   pl.BlockSpec((1, tm, D_MODEL), row),
                   pl.BlockSpec((1, tm, D_MODEL), row)),
        out_shape=out_shape,
        compiler_params=_params("arbitrary", "arbitrary"),
        name="inproj",
    )(x, mod, norm_g, w_in_bf, b_in, *rope)


def _ctx_kv_kernel(x_ref, mod_ref, g_ref, w_ref, b_ref, k_ref, v_ref):
    h = _norm_mod(x_ref[0], g_ref[...], mod_ref[0, 0:1, :], mod_ref[0, 1:2, :])
    kv = jnp.dot(h.astype(BF16), w_ref[...], preferred_element_type=F32) + b_ref[...]
    k_ref[0] = kv[:, :KV_W].astype(BF16)
    v_ref[0] = kv[:, KV_W:].astype(BF16)


def _ctx_kv(ctx, mod_c, norm_g, w_kv_bf, b_kv):
    b, lc, d = ctx.shape
    row = lambda bi: (bi, 0, 0)
    return pl.pallas_call(
        _ctx_kv_kernel,
        grid=(b,),
        in_specs=[pl.BlockSpec((1, lc, d), row),
                  _const_spec((1, 6, d)),
                  _const_spec((1, d)),
                  _const_spec((d, 2 * KV_W)),
                  _const_spec((1, 2 * KV_W))],
        out_specs=(pl.BlockSpec((1, lc, KV_W), row), pl.BlockSpec((1, lc, KV_W), row)),
        out_shape=(jax.ShapeDtypeStruct((b, lc, KV_W), BF16),
                   jax.ShapeDtypeStruct((b, lc, KV_W), BF16)),
        compiler_params=_params("arbitrary"),
        name="ctx_kv",
    )(ctx, mod_c, norm_g, w_kv_bf, b_kv)


def _dot_nt(a, b):
    return lax.dot_general(a, b, (((1,), (1,)), ((), ())), preferred_element_type=F32)


def _attn_kernel(sink_ref, q_ref, kp_ref, kc_ref, kn_ref, vp_ref, vc_ref, vn_ref,
                 kx_ref, vx_ref, o_ref):
    n = pl.program_id(1)
    last = pl.num_programs(1) - 1
    rows = GROUP * TQ
    n_loc = 3 * TQ
    qi = lax.broadcasted_iota(jnp.int32, (rows, n_loc), 0) & (TQ - 1)
    col = lax.broadcasted_iota(jnp.int32, (rows, n_loc), 1)
    kj = col - TQ
    valid = (jnp.abs(qi - kj) <= WINDOW)
    valid = valid & ((col >= TQ) | (n > 0)) & ((col < 2 * TQ) | (n < last))
    row_id = lax.broadcasted_iota(jnp.int32, (rows, 1), 0)

    for j in range(N_KV_HEADS):
        hs = slice(j * HEAD_DIM, (j + 1) * HEAD_DIM)
        q4 = jnp.concatenate(
            [q_ref[0, :, (j * GROUP + g) * HEAD_DIM:(j * GROUP + g + 1) * HEAD_DIM]
             for g in range(GROUP)], axis=0)
        k_loc = jnp.concatenate([kp_ref[0, :, hs], kc_ref[0, :, hs], kn_ref[0, :, hs]], axis=0)
        v_loc = jnp.concatenate([vp_ref[0, :, hs], vc_ref[0, :, hs], vn_ref[0, :, hs]], axis=0)
        s_loc = jnp.where(valid, _dot_nt(q4, k_loc), NEG_INF)
        s_ctx = _dot_nt(q4, kx_ref[0, :, hs])
        sink = jnp.zeros((rows, 1), F32)
        for g in range(GROUP):
            sink = jnp.where(row_id // TQ == g, sink_ref[j * GROUP + g], sink)
        m = jnp.maximum(jnp.maximum(jnp.max(s_loc, axis=-1, keepdims=True),
                                    jnp.max(s_ctx, axis=-1, keepdims=True)), sink)
        p_loc = jnp.exp(s_loc - m)
        p_ctx = jnp.exp(s_ctx - m)
        denom = (jnp.sum(p_loc, axis=-1, keepdims=True) + jnp.sum(p_ctx, axis=-1, keepdims=True)
                 + jnp.exp(sink - m))
        o = (jnp.dot(p_loc.astype(BF16), v_loc, preferred_element_type=F32)
             + jnp.dot(p_ctx.astype(BF16), vx_ref[0, :, hs], preferred_element_type=F32))
        o = o / denom
        for g in range(GROUP):
            c0 = (j * GROUP + g) * HEAD_DIM
            o_ref[0, :, c0:c0 + HEAD_DIM] = o[g * TQ:(g + 1) * TQ, :].astype(BF16)


def _attention(q, k, v, kx, vx, sink):
    b, l, _ = q.shape
    lc = kx.shape[1]
    nblk = l // TQ
    cur = lambda bi, n: (bi, n, 0)
    prev = lambda bi, n: (bi, jnp.maximum(n - 1, 0), 0)
    nxt = lambda bi, n: (bi, jnp.minimum(n + 1, nblk - 1), 0)
    kvb = (1, TQ, KV_W)
    return pl.pallas_call(
        _attn_kernel,
        grid=(b, nblk),
        in_specs=[pl.BlockSpec(memory_space=pltpu.SMEM),
                  pl.BlockSpec((1, TQ, ATTN_W), cur),
                  pl.BlockSpec(kvb, prev), pl.BlockSpec(kvb, cur), pl.BlockSpec(kvb, nxt),
                  pl.BlockSpec(kvb, prev), pl.BlockSpec(kvb, cur), pl.BlockSpec(kvb, nxt),
                  pl.BlockSpec((1, lc, KV_W), lambda bi, n: (bi, 0, 0)),
                  pl.BlockSpec((1, lc, KV_W), lambda bi, n: (bi, 0, 0))],
        out_specs=pl.BlockSpec((1, TQ, ATTN_W), cur),
        out_shape=jax.ShapeDtypeStruct((b, l, ATTN_W), BF16),
        compiler_params=_params("arbitrary", "arbitrary"),
        name="attention",
    )(sink, q, k, k, k, v, v, v, kx, vx)


def _shift_rows(a, d):
    n = a.shape[0]
    return pltpu.roll(a, (-d) % n, 0)


def _merge_kernel(ao_ref, u_ref, up_ref, un_ref, ga_ref, gp_ref, x_ref, mod_ref,
                  wpool_ref, pscale_ref, wab_ref, wpb_ref, wout_ref, g2_ref, wr_ref, br_ref,
                  x1_ref, h2_ref, idx_ref, gate_ref, rank_ref, cnt_ref, carry_ref, *, seq_len):
    bi = pl.program_id(0)
    i = pl.program_id(1)
    last = pl.num_programs(1) - 1
    tm = TM_MERGE

    @pl.when((bi == 0) & (i == 0))
    def _():
        carry_ref[...] = jnp.zeros_like(carry_ref)

    u = u_ref[0]
    prev = jnp.where(i > 0, up_ref[0], 0.0)
    nxt = jnp.where(i < last, un_ref[0], 0.0)
    ext = jnp.concatenate([prev, u, nxt], axis=0)
    t = i * tm + lax.broadcasted_iota(jnp.int32, (tm, 1), 0)
    mixed = []
    for g, w in enumerate(POOL_WINDOWS):
        cs = slice(g * POOL_GROUP_W, (g + 1) * POOL_GROUP_W)
        e = ext[:, cs]
        acc = _shift_rows(e, -1) + e
        step = 1
        while 2 * step < w:
            acc = _shift_rows(acc, -step) + _shift_rows(acc, step)
            step *= 2
        win = acc[POOL_HALO:POOL_HALO + tm]
        half = w // 2
        cnt = (jnp.minimum(t + half, seq_len) - jnp.maximum(t - half, 0)).astype(F32)
        diff = (win / cnt - u[:, cs]).astype(BF16)
        mixed.append(jnp.dot(diff, wpool_ref[g], preferred_element_type=F32))
    pool_o = jnp.concatenate(mixed, axis=1) * pscale_ref[...]

    a = jnp.dot(ao_ref[0], wab_ref[...], preferred_element_type=F32)
    p = jnp.dot(pool_o.astype(BF16), wpb_ref[...], preferred_element_type=F32)
    merged = jax.nn.sigmoid(ga_ref[0]) * a + jax.nn.sigmoid(gp_ref[0]) * p
    o = jnp.dot(merged.astype(BF16), wout_ref[...], preferred_element_type=F32)
    x1 = x_ref[0] + mod_ref[0, 2:3, :] * o
    x1_ref[0] = x1

    h2 = _norm_mod(x1, g2_ref[...], mod_ref[0, 3:4, :], mod_ref[0, 4:5, :])
    h2_ref[...] = _pack_rows(h2)
    logits = jnp.dot(h2.astype(BF16), wr_ref[...], preferred_element_type=F32) + br_ref[...]
    lane = lax.broadcasted_iota(jnp.int32, (tm, N_EXPERTS), 1).astype(F32)
    work = logits
    vals, idxs, sels = [], [], []
    for _ in range(TOP_K):
        mx = jnp.max(work, axis=-1, keepdims=True)
        ix = jnp.min(jnp.where(work == mx, lane, float(N_EXPERTS)), axis=-1, keepdims=True)
        sel = lane == ix
        work = jnp.where(sel, -jnp.inf, work)
        vals.append(mx)
        idxs.append(ix)
        sels.append(sel)
    ex = [jnp.exp(v - vals[0]) for v in vals]
    tot = ex[0] + ex[1] + ex[2] + ex[3]
    gate_ref[...] = jnp.concatenate([e_ / tot for e_ in ex], axis=1)
    idx_ref[...] = jnp.concatenate(idxs, axis=1).astype(jnp.int32)

    member = (sels[0] | sels[1] | sels[2] | sels[3]).astype(F32)
    r_id = lax.broadcasted_iota(jnp.int32, (tm, tm), 0)
    c_id = lax.broadcasted_iota(jnp.int32, (tm, tm), 1)
    tri = (r_id > c_id).astype(BF16)
    before = jnp.dot(tri, member.astype(BF16), preferred_element_type=F32) + carry_ref[...]
    ranks = [jnp.sum(jnp.where(s, before, 0.0), axis=-1, keepdims=True) for s in sels]
    rank_ref[...] = jnp.concatenate(ranks, axis=1).astype(jnp.int32)
    total = carry_ref[...] + jnp.sum(member, axis=0, keepdims=True)
    carry_ref[...] = total
    cnt_ref[...] = total


def _merge_route(attn_o, u, ga, gp, x, mod, w_pool_bf, pool_scale, wab, wpb, wout,
                 norm2_g, w_router, b_router):
    b, l, d = x.shape
    tm = TM_MERGE
    nt = l // tm
    n = b * l
    hb = tm // POOL_HALO
    row = lambda bi, i: (bi, i, 0)
    flat = lambda bi, i: (bi * nt + i, 0)
    halo_prev = lambda bi, i: (bi, jnp.maximum(i * hb - 1, 0), 0)
    halo_next = lambda bi, i: (bi, jnp.minimum((i + 1) * hb, l // POOL_HALO - 1), 0)
    out_shape = (
        jax.ShapeDtypeStruct((b, l, d), F32),
        jax.ShapeDtypeStruct((n, ROW_WORDS), U32),
        jax.ShapeDtypeStruct((n, TOP_K), jnp.int32),
        jax.ShapeDtypeStruct((n, TOP_K), F32),
        jax.ShapeDtypeStruct((n, TOP_K), jnp.int32),
        jax.ShapeDtypeStruct((1, N_EXPERTS), F32),
    )
    return pl.pallas_call(
        functools.partial(_merge_kernel, seq_len=l),
        grid=(b, nt),
        in_specs=[pl.BlockSpec((1, tm, d), row),
                  pl.BlockSpec((1, tm, d), row),
                  pl.BlockSpec((1, POOL_HALO, d), halo_prev),
                  pl.BlockSpec((1, POOL_HALO, d), halo_next),
                  pl.BlockSpec((1, tm, d), row),
                  pl.BlockSpec((1, tm, d), row),
                  pl.BlockSpec((1, tm, d), row),
                  pl.BlockSpec((1, 6, d), lambda bi, i: (bi, 0, 0)),
                  _const_spec(w_pool_bf.shape),
                  _const_spec((1, d)),
                  _const_spec((d, d)), _const_spec((d, d)), _const_spec((d, d)),
                  _const_spec((1, d)),
                  _const_spec((d, N_EXPERTS)),
                  _const_spec((1, N_EXPERTS))],
        out_specs=(pl.BlockSpec((1, tm, d), row),
                   pl.BlockSpec((tm, ROW_WORDS), flat),
                   pl.BlockSpec((tm, TOP_K), flat),
                   pl.BlockSpec((tm, TOP_K), flat),
                   pl.BlockSpec((tm, TOP_K), flat),
                   _const_spec((1, N_EXPERTS))),
        out_shape=out_shape,
        scratch_shapes=[pltpu.VMEM((1, N_EXPERTS), F32)],
        compiler_params=_params("arbitrary", "arbitrary"),
        name="merge_route",
    )(attn_o, u, u, u, ga, gp, x, mod, w_pool_bf, pool_scale, wab, wpb, wout,
      norm2_g, w_router, b_router)


ROW_ISSUE_UNROLL = 8


def _issue_rows(n_rows, start_row):
    def body(it, c):
        for dr in range(ROW_ISSUE_UNROLL):
            for k in range(TOP_K):
                start_row(it * ROW_ISSUE_UNROLL + dr, k, (dr * TOP_K + k) % 2)
        return c

    lax.fori_loop(0, n_rows // ROW_ISSUE_UNROLL, body, 0)


def _scatter_kernel(meta_ref, dest_ref, h_ref, xs_ref, zbuf, sem):
    i = pl.program_id(0)
    ts = TS_MOVE

    @pl.when(i == 0)
    def _():
        zbuf[...] = jnp.zeros_like(zbuf)

        def pad_copy(e):
            start = pl.multiple_of(meta_ref[e] - TM_EXPERT, TM_EXPERT)
            return pltpu.make_async_copy(zbuf, xs_ref.at[pl.ds(start, TM_EXPERT)], sem)

        n_blocks = xs_ref.shape[0] // TM_EXPERT

        def tail_copy(e):
            start = pl.multiple_of((meta_ref[2 * N_EXPERTS] + e) * TM_EXPERT, TM_EXPERT)
            return pltpu.make_async_copy(zbuf, xs_ref.at[pl.ds(start, TM_EXPERT)], sem)

        for e in range(N_EXPERTS):
            @pl.when(meta_ref[N_EXPERTS + e] > 0)
            def _():
                pad_copy(e).start()

            @pl.when(meta_ref[2 * N_EXPERTS] + e < n_blocks)
            def _():
                tail_copy(e).start()
        for e in range(N_EXPERTS):
            @pl.when(meta_ref[N_EXPERTS + e] > 0)
            def _():
                pad_copy(e).wait()

            @pl.when(meta_ref[2 * N_EXPERTS] + e < n_blocks)
            def _():
                tail_copy(e).wait()

    def start_row(r, k, priority):
        d = dest_ref[r * TOP_K + k]
        pltpu.make_async_copy(h_ref.at[pl.ds(r, 1)], xs_ref.at[pl.ds(d, 1)], sem).start(priority)

    _issue_rows(ts, start_row)
    for _ in range(TOP_K):
        pltpu.make_async_copy(h_ref, xs_ref.at[pl.ds(0, ts)], sem).wait()


def _scatter_rows(meta, dest_flat, h2, p_rows):
    n = h2.shape[0]
    ts = TS_MOVE
    grid_spec = pltpu.PrefetchScalarGridSpec(
        num_scalar_prefetch=1,
        grid=(n // ts,),
        in_specs=[pl.BlockSpec((ts * TOP_K,), lambda i, m: (i,), memory_space=pltpu.SMEM),
                  pl.BlockSpec((ts, ROW_WORDS), lambda i, m: (i, 0))],
        out_specs=pl.BlockSpec(memory_space=pl.ANY),
        scratch_shapes=[pltpu.VMEM((TM_EXPERT, ROW_WORDS), U32), pltpu.SemaphoreType.DMA],
    )
    return pl.pallas_call(
        _scatter_kernel,
        grid_spec=grid_spec,
        out_shape=jax.ShapeDtypeStruct((p_rows, ROW_WORDS), U32),
        compiler_params=_params("arbitrary"),
        name="moe_scatter",
    )(meta, dest_flat, h2)


def _expert_kernel(be_ref, first_ref, next_ref, nu_ref, x_ref, wgu_hbm, bgu_ref, wd_hbm, bd_ref,
                   y_ref, wgu_f, wd_f, wgu_s, wd_s, sem):
    i = pl.program_id(0)

    def weight_copies(e):
        return (pltpu.make_async_copy(wgu_hbm.at[e], wgu_f, sem.at[0]),
                pltpu.make_async_copy(wd_hbm.at[e], wd_f, sem.at[1]))

    @pl.when(i == 0)
    def _():
        for cp in weight_copies(be_ref[0]):
            cp.start()

    @pl.when(i < nu_ref[0])
    def _():
        @pl.when(first_ref[i] == 1)
        def _():
            for cp in weight_copies(be_ref[i]):
                cp.wait()
            ck = 256
            for r0 in range(0, D_MODEL, ck):
                wgu_s[r0:r0 + ck, :] = wgu_f[r0:r0 + ck, :].astype(BF16)
                wd_s[r0:r0 + ck, :] = wd_f[r0:r0 + ck, :].astype(BF16)

            @pl.when(next_ref[i] >= 0)
            def _():
                for cp in weight_copies(next_ref[i]):
                    cp.start()

        x = _unpack_rows(x_ref[...]).astype(BF16)
        acc = jnp.zeros((TM_EXPERT, D_MODEL), F32)
        fc = 512
        for c0 in range(0, D_FF, fc):
            gt = jnp.dot(x, wgu_s[:, c0:c0 + fc], preferred_element_type=F32) + bgu_ref[0, :, c0:c0 + fc]
            ln = (jnp.dot(x, wgu_s[:, D_FF + c0:D_FF + c0 + fc], preferred_element_type=F32)
                  + bgu_ref[0, :, D_FF + c0:D_FF + c0 + fc])
            gt = jnp.minimum(gt, SWIGLU_LIMIT)
            ln = jnp.clip(ln, -SWIGLU_LIMIT, SWIGLU_LIMIT)
            act = gt * jax.nn.sigmoid(SWIGLU_ALPHA * gt) * (ln + 1.0)
            acc = acc + jnp.dot(act.astype(BF16), wd_s[c0:c0 + fc, :], preferred_element_type=F32)
        y_ref[...] = _pack_rows(acc + bd_ref[0])

    @pl.when(i >= nu_ref[0])
    def _():
        y_ref[...] = jnp.zeros_like(y_ref)


def _experts(block_e, first, next_e, n_used, xs, w_gu, b_gu, w_down, b_down):
    p_rows = xs.shape[0]
    d = D_MODEL
    tm = TM_EXPERT
    nb = p_rows // tm

    def blk(i, be, fi, nx, nu):
        return jnp.minimum(i, nu[0] - 1)

    xmap = lambda i, be, fi, nx, nu: (blk(i, be, fi, nx, nu), 0)
    bmap = lambda i, be, fi, nx, nu: (be[blk(i, be, fi, nx, nu)], 0, 0)
    grid_spec = pltpu.PrefetchScalarGridSpec(
        num_scalar_prefetch=4,
        grid=(nb,),
        in_specs=[pl.BlockSpec((tm, ROW_WORDS), xmap),
                  pl.BlockSpec(memory_space=pl.ANY),
                  pl.BlockSpec((1, 1, 2 * D_FF), bmap),
                  pl.BlockSpec(memory_space=pl.ANY),
                  pl.BlockSpec((1, 1, d), bmap)],
        out_specs=pl.BlockSpec((tm, ROW_WORDS), lambda i, be, fi, nx, nu: (i, 0)),
        scratch_shapes=[pltpu.VMEM((d, 2 * D_FF), F32), pltpu.VMEM((D_FF, d), F32),
                        pltpu.VMEM((d, 2 * D_FF), BF16), pltpu.VMEM((D_FF, d), BF16),
                        pltpu.SemaphoreType.DMA((2,))],
    )
    return pl.pallas_call(
        _expert_kernel,
        grid_spec=grid_spec,
        out_shape=jax.ShapeDtypeStruct((p_rows, ROW_WORDS), U32),
        compiler_params=_params("arbitrary"),
        name="moe_experts",
    )(block_e, first, next_e, n_used, xs, w_gu, b_gu.reshape(N_EXPERTS, 1, 2 * D_FF),
      w_down, b_down.reshape(N_EXPERTS, 1, d))


def _combine_kernel(dest_ref, dest_next_ref, ys_ref, gate_ref, x1_ref, mod_ref, fg_ref, o_ref,
                    buf, sem):
    ts = TS_MOVE
    s = pl.program_id(0)
    last = pl.num_programs(0) - 1
    slot = s % 2

    def gather(d_ref, sl):
        def start_row(r, k, priority):
            d = d_ref[r * TOP_K + k]
            pltpu.make_async_copy(ys_ref.at[pl.ds(d, 1)], buf.at[sl, k, pl.ds(r, 1)],
                                  sem.at[sl]).start(priority)

        _issue_rows(ts, start_row)

    @pl.when(s == 0)
    def _():
        gather(dest_ref, 0)

    @pl.when(s < last)
    def _():
        gather(dest_next_ref, 1 - slot)

    for k in range(TOP_K):
        pltpu.make_async_copy(ys_ref.at[pl.ds(0, ts)], buf.at[slot, k], sem.at[slot]).wait()

    gate = gate_ref[...]
    y = gate[:, 0:1] * _unpack_rows(buf[slot, 0])
    for k in range(1, TOP_K):
        y = y + gate[:, k:k + 1] * _unpack_rows(buf[slot, k])
    x2 = x1_ref[...] + mod_ref[0, 5:6, :] * y
    o_ref[...] = x2 * lax.rsqrt(jnp.mean(x2 * x2, axis=-1, keepdims=True) + NORM_EPS) * fg_ref[...]


def _combine(dest_flat, ys, gate_w, x1, mod, final_g):
    b, l, d = x1.shape
    ts = TS_MOVE
    nt = l // ts
    steps = b * nt
    return pl.pallas_call(
        _combine_kernel,
        grid=(steps,),
        in_specs=[pl.BlockSpec((ts * TOP_K,), lambda s: (s,), memory_space=pltpu.SMEM),
                  pl.BlockSpec((ts * TOP_K,), lambda s: (jnp.minimum(s + 1, steps - 1),),
                               memory_space=pltpu.SMEM),
                  pl.BlockSpec(memory_space=pl.ANY),
                  pl.BlockSpec((ts, TOP_K), lambda s: (s, 0)),
                  pl.BlockSpec((ts, d), lambda s: (s, 0)),
                  pl.BlockSpec((1, 6, d), lambda s: (s // nt, 0, 0)),
                  _const_spec((1, d))],
        out_specs=pl.BlockSpec((ts, d), lambda s: (s, 0)),
        out_shape=jax.ShapeDtypeStruct((b * l, d), F32),
        scratch_shapes=[pltpu.VMEM((2, TOP_K, ts, ROW_WORDS), U32), pltpu.SemaphoreType.DMA((2,))],
        compiler_params=_params("arbitrary"),
        name="moe_combine",
    )(dest_flat, dest_flat, ys, gate_w, x1.reshape(b * l, d), mod, final_g).reshape(b, l, d)


SC_CORES = 2
SC_SUBCORES = 16
SC_WORKERS = SC_CORES * SC_SUBCORES
SC_WINDOW = 128


def _sc_gather_rows(table, idx):
    m = idx.shape[0]
    w = table.shape[1]
    per_worker = m // SC_WORKERS
    n_win = per_worker // SC_WINDOW
    assert per_worker * SC_WORKERS == m and n_win * SC_WINDOW == per_worker
    mesh = plsc.VectorSubcoreMesh(core_axis_name="c", subcore_axis_name="s")

    @functools.partial(
        pl.kernel, mesh=mesh,
        out_type=jax.ShapeDtypeStruct((m, w), table.dtype),
        scratch_types=[pltpu.VMEM((SC_WINDOW,), jnp.int32),
                       pltpu.VMEM((SC_WINDOW, w), table.dtype),
                       pltpu.SemaphoreType.DMA],
        name="sc_gather_rows",
    )
    def gather(table_hbm, idx_hbm, out_hbm, idx_v, rows_v, sem):
        wid = lax.axis_index("s") * SC_CORES + lax.axis_index("c")

        @pl.loop(0, n_win)
        def _(j):
            base = wid * per_worker + j * SC_WINDOW
            pltpu.sync_copy(idx_hbm.at[pl.ds(base, SC_WINDOW)], idx_v)
            pltpu.async_copy(table_hbm.at[idx_v], rows_v, sem).wait()
            pltpu.sync_copy(rows_v, out_hbm.at[pl.ds(base, SC_WINDOW)])

    return gather(table, idx)


def _combine_dense_kernel(y4_ref, gate_ref, x1_ref, mod_ref, fg_ref, o_ref):
    gate = gate_ref[...]
    y = gate[:, 0:1] * _unpack_rows(y4_ref[0])
    for k in range(1, TOP_K):
        y = y + gate[:, k:k + 1] * _unpack_rows(y4_ref[k])
    x2 = x1_ref[...] + mod_ref[0, 5:6, :] * y
    o_ref[...] = x2 * lax.rsqrt(jnp.mean(x2 * x2, axis=-1, keepdims=True) + NORM_EPS) * fg_ref[...]


def _combine_dense(y4, gate_w, x1, mod, final_g):
    b, l, d = x1.shape
    ts = 512
    nt = l // ts
    return pl.pallas_call(
        _combine_dense_kernel,
        grid=(b * nt,),
        in_specs=[pl.BlockSpec((TOP_K, ts, ROW_WORDS), lambda s: (0, s, 0)),
                  pl.BlockSpec((ts, TOP_K), lambda s: (s, 0)),
                  pl.BlockSpec((ts, d), lambda s: (s, 0)),
                  pl.BlockSpec((1, 6, d), lambda s: (s // nt, 0, 0)),
                  _const_spec((1, d))],
        out_specs=pl.BlockSpec((ts, d), lambda s: (s, 0)),
        out_shape=jax.ShapeDtypeStruct((b * l, d), F32),
        compiler_params=_params("arbitrary"),
        name="moe_combine",
    )(y4, gate_w, x1.reshape(b * l, d), mod, final_g).reshape(b, l, d)


def _rope_tables(seq_len):
    inv_freq = ROPE_BASE ** (-jnp.arange(ROPE_PAIRS, dtype=F32) / ROPE_PAIRS)
    reps = LANES // HEAD_DIM
    out = []
    for n_pos, on_row in ((seq_len // GRID_W, True), (GRID_W, False)):
        ang = jnp.arange(n_pos, dtype=F32)[:, None] * inv_freq
        zero = jnp.zeros_like(ang)
        for fn, sign in ((jnp.cos, 1.0), (jnp.sin, -1.0)):
            v = fn(ang)
            head = ([sign * v, v, zero, zero] if on_row else [zero, zero, sign * v, v])
            out.append(jnp.concatenate(head * reps, axis=1))
    return out


def kernel(x, c, ctx, c_ctx, w_ada, b_ada, norm1_g, norm2_g, w_in, b_in, attn_sink, w_pool,
           pool_scale, w_attn_br, w_pool_br, w_out, w_router, b_router, w_gu, b_gu, w_down,
           b_down, final_g):
    b, l, d = x.shape
    n = b * l
    assert w_ada.shape[0] == 1, "single-layer block"

    cond = jnp.zeros((SUBLANES, d), F32).at[:b].set(c).at[b].set(c_ctx)
    mod = _adaln(cond, w_ada[0], b_ada[0])[:b + 1].reshape(b + 1, 6, d)
    mod_x, mod_c = mod[:b], mod[b:b + 1]

    rope = _rope_tables(l)
    w_in_bf = w_in[0].astype(BF16)
    b_in2 = b_in[0].reshape(1, IN_W)
    g1 = norm1_g[0].reshape(1, d)
    q, k, v, u, ga, gp = _inproj(x, mod_x, g1, w_in_bf, b_in2, rope)
    kv_sl = slice(ATTN_W, ATTN_W + 2 * KV_W)
    kx, vx = _ctx_kv(ctx, mod_c, g1, w_in_bf[:, kv_sl], b_in2[:, kv_sl])

    attn_o = _attention(q, k, v, kx, vx, attn_sink[0])

    x1, h2, top_idx, gate_w, rank, counts = _merge_route(
        attn_o, u, ga, gp, x, mod_x, w_pool[0].astype(BF16), pool_scale[0].reshape(1, d),
        w_attn_br[0].astype(BF16), w_pool_br[0].astype(BF16), w_out[0].astype(BF16),
        norm2_g[0].reshape(1, d), w_router[0].astype(BF16), b_router[0].reshape(1, N_EXPERTS))

    tm = TM_EXPERT
    nb = n * TOP_K // tm + N_EXPERTS
    cnt = counts[0].astype(jnp.int32)
    padded = (cnt + tm - 1) // tm * tm
    pend = jnp.cumsum(padded)
    pstart = pend - padded
    expert_ids = jnp.arange(N_EXPERTS, dtype=jnp.int32)
    dest = (jnp.sum(jnp.where(top_idx[..., None] == expert_ids, pstart, 0), axis=-1)
            + rank).reshape(-1)
    n_used = (pend[-1] // tm).reshape(1)
    block_start = jnp.arange(nb, dtype=jnp.int32) * tm
    block_e = jnp.minimum(jnp.sum((pend[None, :] <= block_start[:, None]).astype(jnp.int32), axis=1),
                          N_EXPERTS - 1)
    first = jnp.concatenate([jnp.ones((1,), jnp.int32),
                             (block_e[1:] != block_e[:-1]).astype(jnp.int32)])
    later_used = (expert_ids[None, :] > expert_ids[:, None]) & (cnt[None, :] > 0)
    next_of_expert = jnp.min(jnp.where(later_used, expert_ids[None, :], N_EXPERTS), axis=1)
    next_of_expert = jnp.where(next_of_expert < N_EXPERTS, next_of_expert, -1)
    next_e = jnp.sum(jnp.where(block_e[:, None] == expert_ids, next_of_expert, 0), axis=-1)

    xs = _scatter_rows(jnp.concatenate([pend, padded, n_used]), dest, h2, nb * tm)
    ys = _experts(block_e, first, next_e, n_used, xs, w_gu[0], b_gu[0], w_down[0], b_down[0])
    dest_slot_major = dest.reshape(n, TOP_K).T.reshape(-1)
    y4 = _sc_gather_rows(ys, dest_slot_major).reshape(TOP_K, n, ROW_WORDS)
    return _combine_dense(y4, gate_w, x1, mod_x, final_g.reshape(1, d))
```

```python
import functools

import jax
import jax.numpy as jnp
from jax import lax
from jax.experimental import pallas as pl
from jax.experimental.pallas import tpu as pltpu
from jax.experimental.pallas import tpu_sc as plsc

D_MODEL = 1024
GRID_W = 64
HEAD_DIM = 64
N_HEADS = 16
N_KV_HEADS = 4
GROUP = N_HEADS // N_KV_HEADS
ATTN_W = N_HEADS * HEAD_DIM
KV_W = N_KV_HEADS * HEAD_DIM
WINDOW = 128
ATTN_SCALE = HEAD_DIM ** -0.5
ROPE_BASE = 10000.0
ROPE_PAIRS = HEAD_DIM // 4
POOL_WINDOWS = (2, 4, 8, 16)
POOL_GROUP_W = D_MODEL // len(POOL_WINDOWS)
IN_W = ATTN_W + 2 * KV_W + D_MODEL + 2 * D_MODEL
N_EXPERTS = 32
TOP_K = 4
D_FF = D_MODEL
SWIGLU_ALPHA = 1.702
SWIGLU_LIMIT = 7.0
NORM_EPS = 1e-5
NEG_INF = -1e30

LANES = 128
SUBLANES = 8
VMEM_LIMIT = 56 * 1024 * 1024

TM_INPROJ = 512
TQ = WINDOW
TM_MERGE = 256
TM_EXPERT = 256
TS_MOVE = 256
POOL_HALO = SUBLANES

F32 = jnp.float32
BF16 = jnp.bfloat16


ROW_WORDS = D_MODEL // 2
U32 = jnp.uint32


def _pack_rows(value):
    bits = lax.bitcast_convert_type(value.astype(BF16).astype(F32), U32)
    return bits[:, :ROW_WORDS] | (bits[:, ROW_WORDS:] >> 16)


def _unpack_rows(words):
    hi = lax.bitcast_convert_type(words & jnp.uint32(0xFFFF0000), F32)
    lo = lax.bitcast_convert_type(words << 16, F32)
    return jnp.concatenate([hi, lo], axis=1)


def _params(*sem):
    return pltpu.CompilerParams(dimension_semantics=sem, vmem_limit_bytes=VMEM_LIMIT)


def _const_spec(shape):
    nd = len(shape)
    return pl.BlockSpec(shape, lambda *_: (0,) * nd)


def _adaln_kernel(c_ref, w_ref, b_ref, o_ref):
    c = c_ref[...]
    s = c * jax.nn.sigmoid(c)
    o_ref[...] = jnp.dot(s.astype(BF16), w_ref[...].astype(BF16),
                         preferred_element_type=F32) + b_ref[...]


def _adaln(cond, w_ada, b_ada):
    rows, d = cond.shape
    n = w_ada.shape[1]
    tn = 1024
    return pl.pallas_call(
        _adaln_kernel,
        grid=(n // tn,),
        in_specs=[_const_spec((rows, d)),
                  pl.BlockSpec((d, tn), lambda j: (0, j)),
                  pl.BlockSpec((1, tn), lambda j: (0, j))],
        out_specs=pl.BlockSpec((rows, tn), lambda j: (0, j)),
        out_shape=jax.ShapeDtypeStruct((rows, n), F32),
        compiler_params=_params("arbitrary"),
        name="adaln",
    )(cond, w_ada, b_ada.reshape(1, n))


def _norm_mod(x, g, shift, scale):
    y = x * lax.rsqrt(jnp.mean(x * x, axis=-1, keepdims=True) + NORM_EPS) * g
    return y * (1.0 + scale) + shift


def _rope(t, cos, sin_signed):
    w = t.shape[1]
    half = ROPE_PAIRS
    lane = lax.broadcasted_iota(jnp.int32, t.shape, 1)
    first = (lane & (2 * half - 1)) < half
    rot = jnp.where(first, pltpu.roll(t, w - half, 1), pltpu.roll(t, half, 1))
    reps = w // LANES
    cos_w = jnp.concatenate([cos] * reps, axis=1) if reps > 1 else cos
    sin_w = jnp.concatenate([sin_signed] * reps, axis=1) if reps > 1 else sin_signed
    return t * cos_w + rot * sin_w


def _token_table(row_ref, col_ref):
    gr = row_ref.shape[0]
    shape = (gr, GRID_W, LANES)
    full = (jnp.broadcast_to(row_ref[...][:, None, :], shape)
            + jnp.broadcast_to(col_ref[...][None, :, :], shape))
    return full.reshape(gr * GRID_W, LANES)


def _inproj_kernel(x_ref, mod_ref, g_ref, w_ref, b_ref, cr_ref, sr_ref, cc_ref, sc_ref,
                   q_ref, k_ref, v_ref, u_ref, ga_ref, gp_ref):
    h = _norm_mod(x_ref[0], g_ref[...], mod_ref[0, 0:1, :], mod_ref[0, 1:2, :])
    hb = h.astype(BF16)
    cos = _token_table(cr_ref, cc_ref)
    sin = _token_table(sr_ref, sc_ref)

    def proj(c0, c1):
        return jnp.dot(hb, w_ref[:, c0:c1], preferred_element_type=F32) + b_ref[:, c0:c1]

    cw = 512
    for c0 in range(0, ATTN_W, cw):
        q_ref[0, :, c0:c0 + cw] = (_rope(proj(c0, c0 + cw), cos, sin) * ATTN_SCALE).astype(BF16)
    k_ref[0] = _rope(proj(ATTN_W, ATTN_W + KV_W), cos, sin).astype(BF16)
    v_ref[0] = proj(ATTN_W + KV_W, ATTN_W + 2 * KV_W).astype(BF16)
    base = ATTN_W + 2 * KV_W
    for ref in (u_ref, ga_ref, gp_ref):
        for c0 in range(0, D_MODEL, cw):
            ref[0, :, c0:c0 + cw] = proj(base + c0, base + c0 + cw)
        base += D_MODEL


def _inproj(x, mod, norm_g, w_in_bf, b_in, rope):
    b, l, d = x.shape
    tm = TM_INPROJ
    row = lambda bi, i: (bi, i, 0)
    out_shape = (
        jax.ShapeDtypeStruct((b, l, ATTN_W), BF16),
        jax.ShapeDtypeStruct((b, l, KV_W), BF16),
        jax.ShapeDtypeStruct((b, l, KV_W), BF16),
        jax.ShapeDtypeStruct((b, l, D_MODEL), F32),
        jax.ShapeDtypeStruct((b, l, D_MODEL), F32),
        jax.ShapeDtypeStruct((b, l, D_MODEL), F32),
    )
    return pl.pallas_call(
        _inproj_kernel,
        grid=(b, l // tm),
        in_specs=[pl.BlockSpec((1, tm, d), row),
                  pl.BlockSpec((1, 6, d), lambda bi, i: (bi, 0, 0)),
                  _const_spec((1, d)),
                  _const_spec((d, IN_W)),
                  _const_spec((1, IN_W)),
                  pl.BlockSpec((tm // GRID_W, LANES), lambda bi, i: (i, 0)),
                  pl.BlockSpec((tm // GRID_W, LANES), lambda bi, i: (i, 0)),
                  _const_spec((GRID_W, LANES)),
                  _const_spec((GRID_W, LANES))],
        out_specs=(pl.BlockSpec((1, tm, ATTN_W), row),
                   pl.BlockSpec((1, tm, KV_W), row),
                   pl.BlockSpec((1, tm, KV_W), row),
                   pl.BlockSpec((1, tm, D_MODEL), row),
                   pl.BlockSpec((1, tm, D_MODEL), row),
                   pl.BlockSpec((1, tm, D_MODEL), row)),
        out_shape=out_shape,
        compiler_params=_params("arbitrary", "arbitrary"),
        name="inproj",
    )(x, mod, norm_g, w_in_bf, b_in, *rope)


def _ctx_kv_kernel(x_ref, mod_ref, g_ref, w_ref, b_ref, k_ref, v_ref):
    h = _norm_mod(x_ref[0], g_ref[...], mod_ref[0, 0:1, :], mod_ref[0, 1:2, :])
    kv = jnp.dot(h.astype(BF16), w_ref[...], preferred_element_type=F32) + b_ref[...]
    k_ref[0] = kv[:, :KV_W].astype(BF16)
    v_ref[0] = kv[:, KV_W:].astype(BF16)


def _ctx_kv(ctx, mod_c, norm_g, w_kv_bf, b_kv):
    b, lc, d = ctx.shape
    row = lambda bi: (bi, 0, 0)
    return pl.pallas_call(
        _ctx_kv_kernel,
        grid=(b,),
        in_specs=[pl.BlockSpec((1, lc, d), row),
                  _const_spec((1, 6, d)),
                  _const_spec((1, d)),
                  _const_spec((d, 2 * KV_W)),
                  _const_spec((1, 2 * KV_W))],
        out_specs=(pl.BlockSpec((1, lc, KV_W), row), pl.BlockSpec((1, lc, KV_W), row)),
        out_shape=(jax.ShapeDtypeStruct((b, lc, KV_W), BF16),
                   jax.ShapeDtypeStruct((b, lc, KV_W), BF16)),
        compiler_params=_params("arbitrary"),
        name="ctx_kv",
    )(ctx, mod_c, norm_g, w_kv_bf, b_kv)


def _dot_nt(a, b):
    return lax.dot_general(a, b, (((1,), (1,)), ((), ())), preferred_element_type=F32)


def _attn_kernel(sink_ref, q_ref, kp_ref, kc_ref, kn_ref, vp_ref, vc_ref, vn_ref,
                 kx_ref, vx_ref, o_ref):
    n = pl.program_id(1)
    last = pl.num_programs(1) - 1
    rows = GROUP * TQ
    n_loc = 3 * TQ
    qi = lax.broadcasted_iota(jnp.int32, (rows, n_loc), 0) & (TQ - 1)
    col = lax.broadcasted_iota(jnp.int32, (rows, n_loc), 1)
    kj = col - TQ
    valid = (jnp.abs(qi - kj) <= WINDOW)
    valid = valid & ((col >= TQ) | (n > 0)) & ((col < 2 * TQ) | (n < last))
    row_id = lax.broadcasted_iota(jnp.int32, (rows, 1), 0)

    for j in range(N_KV_HEADS):
        hs = slice(j * HEAD_DIM, (j + 1) * HEAD_DIM)
        q4 = jnp.concatenate(
            [q_ref[0, :, (j * GROUP + g) * HEAD_DIM:(j * GROUP + g + 1) * HEAD_DIM]
             for g in range(GROUP)], axis=0)
        k_loc = jnp.concatenate([kp_ref[0, :, hs], kc_ref[0, :, hs], kn_ref[0, :, hs]], axis=0)
        v_loc = jnp.concatenate([vp_ref[0, :, hs], vc_ref[0, :, hs], vn_ref[0, :, hs]], axis=0)
        s_loc = jnp.where(valid, _dot_nt(q4, k_loc), NEG_INF)
        s_ctx = _dot_nt(q4, kx_ref[0, :, hs])
        sink = jnp.zeros((rows, 1), F32)
        for g in range(GROUP):
            sink = jnp.where(row_id // TQ == g, sink_ref[j * GROUP + g], sink)
        m = jnp.maximum(jnp.maximum(jnp.max(s_loc, axis=-1, keepdims=True),
                                    jnp.max(s_ctx, axis=-1, keepdims=True)), sink)
        p_loc = jnp.exp(s_loc - m)
        p_ctx = jnp.exp(s_ctx - m)
        denom = (jnp.sum(p_loc, axis=-1, keepdims=True) + jnp.sum(p_ctx, axis=-1, keepdims=True)
                 + jnp.exp(sink - m))
        o = (jnp.dot(p_loc.astype(BF16), v_loc, preferred_element_type=F32)
             + jnp.dot(p_ctx.astype(BF16), vx_ref[0, :, hs], preferred_element_type=F32))
        o = o / denom
        for g in range(GROUP):
            c0 = (j * GROUP + g) * HEAD_DIM
            o_ref[0, :, c0:c0 + HEAD_DIM] = o[g * TQ:(g + 1) * TQ, :].astype(BF16)


def _attention(q, k, v, kx, vx, sink):
    b, l, _ = q.shape
    lc = kx.shape[1]
    nblk = l // TQ
    cur = lambda bi, n: (bi, n, 0)
    prev = lambda bi, n: (bi, jnp.maximum(n - 1, 0), 0)
    nxt = lambda bi, n: (bi, jnp.minimum(n + 1, nblk - 1), 0)
    kvb = (1, TQ, KV_W)
    return pl.pallas_call(
        _attn_kernel,
        grid=(b, nblk),
        in_specs=[pl.BlockSpec(memory_space=pltpu.SMEM),
                  pl.BlockSpec((1, TQ, ATTN_W), cur),
                  pl.BlockSpec(kvb, prev), pl.BlockSpec(kvb, cur), pl.BlockSpec(kvb, nxt),
                  pl.BlockSpec(kvb, prev), pl.BlockSpec(kvb, cur), pl.BlockSpec(kvb, nxt),
                  pl.BlockSpec((1, lc, KV_W), lambda bi, n: (bi, 0, 0)),
                  pl.BlockSpec((1, lc, KV_W), lambda bi, n: (bi, 0, 0))],
        out_specs=pl.BlockSpec((1, TQ, ATTN_W), cur),
        out_shape=jax.ShapeDtypeStruct((b, l, ATTN_W), BF16),
        compiler_params=_params("arbitrary", "arbitrary"),
        name="attention",
    )(sink, q, k, k, k, v, v, v, kx, vx)


def _shift_rows(a, d):
    n = a.shape[0]
    return pltpu.roll(a, (-d) % n, 0)


def _merge_kernel(ao_ref, u_ref, up_ref, un_ref, ga_ref, gp_ref, x_ref, mod_ref,
                  wpool_ref, pscale_ref, wab_ref, wpb_ref, wout_ref, g2_ref, wr_ref, br_ref,
                  x1_ref, h2_ref, idx_ref, gate_ref, rank_ref, cnt_ref, carry_ref, *, seq_len):
    bi = pl.program_id(0)
    i = pl.program_id(1)
    last = pl.num_programs(1) - 1
    tm = TM_MERGE

    @pl.when((bi == 0) & (i == 0))
    def _():
        carry_ref[...] = jnp.zeros_like(carry_ref)

    u = u_ref[0]
    prev = jnp.where(i > 0, up_ref[0], 0.0)
    nxt = jnp.where(i < last, un_ref[0], 0.0)
    ext = jnp.concatenate([prev, u, nxt], axis=0)
    t = i * tm + lax.broadcasted_iota(jnp.int32, (tm, 1), 0)
    mixed = []
    for g, w in enumerate(POOL_WINDOWS):
        cs = slice(g * POOL_GROUP_W, (g + 1) * POOL_GROUP_W)
        e = ext[:, cs]
        acc = _shift_rows(e, -1) + e
        step = 1
        while 2 * step < w:
            acc = _shift_rows(acc, -step) + _shift_rows(acc, step)
            step *= 2
        win = acc[POOL_HALO:POOL_HALO + tm]
        half = w // 2
        cnt = (jnp.minimum(t + half, seq_len) - jnp.maximum(t - half, 0)).astype(F32)
        diff = (win / cnt - u[:, cs]).astype(BF16)
        mixed.append(jnp.dot(diff, wpool_ref[g], preferred_element_type=F32))
    pool_o = jnp.concatenate(mixed, axis=1) * pscale_ref[...]

    a = jnp.dot(ao_ref[0], wab_ref[...], preferred_element_type=F32)
    p = jnp.dot(pool_o.astype(BF16), wpb_ref[...], preferred_element_type=F32)
    merged = jax.nn.sigmoid(ga_ref[0]) * a + jax.nn.sigmoid(gp_ref[0]) * p
    o = jnp.dot(merged.astype(BF16), wout_ref[...], preferred_element_type=F32)
    x1 = x_ref[0] + mod_ref[0, 2:3, :] * o
    x1_ref[0] = x1

    h2 = _norm_mod(x1, g2_ref[...], mod_ref[0, 3:4, :], mod_ref[0, 4:5, :])
    h2_ref[...] = _pack_rows(h2)
    logits = jnp.dot(h2.astype(BF16), wr_ref[...], preferred_element_type=F32) + br_ref[...]
    lane = lax.broadcasted_iota(jnp.int32, (tm, N_EXPERTS), 1).astype(F32)
    work = logits
    vals, idxs, sels = [], [], []
    for _ in range(TOP_K):
        mx = jnp.max(work, axis=-1, keepdims=True)
        ix = jnp.min(jnp.where(work == mx, lane, float(N_EXPERTS)), axis=-1, keepdims=True)
        sel = lane == ix
        work = jnp.where(sel, -jnp.inf, work)
        vals.append(mx)
        idxs.append(ix)
        sels.append(sel)
    ex = [jnp.exp(v - vals[0]) for v in vals]
    tot = ex[0] + ex[1] + ex[2] + ex[3]
    gate_ref[...] = jnp.concatenate([e_ / tot for e_ in ex], axis=1)
    idx_ref[...] = jnp.concatenate(idxs, axis=1).astype(jnp.int32)

    member = (sels[0] | sels[1] | sels[2] | sels[3]).astype(F32)
    r_id = lax.broadcasted_iota(jnp.int32, (tm, tm), 0)
    c_id = lax.broadcasted_iota(jnp.int32, (tm, tm), 1)
    tri = (r_id > c_id).astype(BF16)
    before = jnp.dot(tri, member.astype(BF16), preferred_element_type=F32) + carry_ref[...]
    ranks = [jnp.sum(jnp.where(s, before, 0.0), axis=-1, keepdims=True) for s in sels]
    rank_ref[...] = jnp.concatenate(ranks, axis=1).astype(jnp.int32)
    total = carry_ref[...] + jnp.sum(member, axis=0, keepdims=True)
    carry_ref[...] = total
    cnt_ref[...] = total


def _merge_route(attn_o, u, ga, gp, x, mod, w_pool_bf, pool_scale, wab, wpb, wout,
                 norm2_g, w_router, b_router):
    b, l, d = x.shape
    tm = TM_MERGE
    nt = l // tm
    n = b * l
    hb = tm // POOL_HALO
    row = lambda bi, i: (bi, i, 0)
    flat = lambda bi, i: (bi * nt + i, 0)
    halo_prev = lambda bi, i: (bi, jnp.maximum(i * hb - 1, 0), 0)
    halo_next = lambda bi, i: (bi, jnp.minimum((i + 1) * hb, l // POOL_HALO - 1), 0)
    out_shape = (
        jax.ShapeDtypeStruct((b, l, d), F32),
        jax.ShapeDtypeStruct((n, ROW_WORDS), U32),
        jax.ShapeDtypeStruct((n, TOP_K), jnp.int32),
        jax.ShapeDtypeStruct((n, TOP_K), F32),
        jax.ShapeDtypeStruct((n, TOP_K), jnp.int32),
        jax.ShapeDtypeStruct((1, N_EXPERTS), F32),
    )
    return pl.pallas_call(
        functools.partial(_merge_kernel, seq_len=l),
        grid=(b, nt),
        in_specs=[pl.BlockSpec((1, tm, d), row),
                  pl.BlockSpec((1, tm, d), row),
                  pl.BlockSpec((1, POOL_HALO, d), halo_prev),
                  pl.BlockSpec((1, POOL_HALO, d), halo_next),
                  pl.BlockSpec((1, tm, d), row),
                  pl.BlockSpec((1, tm, d), row),
                  pl.BlockSpec((1, tm, d), row),
                  pl.BlockSpec((1, 6, d), lambda bi, i: (bi, 0, 0)),
                  _const_spec(w_pool_bf.shape),
                  _const_spec((1, d)),
                  _const_spec((d, d)), _const_spec((d, d)), _const_spec((d, d)),
                  _const_spec((1, d)),
                  _const_spec((d, N_EXPERTS)),
                  _const_spec((1, N_EXPERTS))],
        out_specs=(pl.BlockSpec((1, tm, d), row),
                   pl.BlockSpec((tm, ROW_WORDS), flat),
                   pl.BlockSpec((tm, TOP_K), flat),
                   pl.BlockSpec((tm, TOP_K), flat),
                   pl.BlockSpec((tm, TOP_K), flat),
                   _const_spec((1, N_EXPERTS))),
        out_shape=out_shape,
        scratch_shapes=[pltpu.VMEM((1, N_EXPERTS), F32)],
        compiler_params=_params("arbitrary", "arbitrary"),
        name="merge_route",
    )(attn_o, u, u, u, ga, gp, x, mod, w_pool_bf, pool_scale, wab, wpb, wout,
      norm2_g, w_router, b_router)


SC_CORES = 2
SC_SUBCORES = 16
SC_WORKERS = SC_CORES * SC_SUBCORES
SC_WINDOW = 128
TRASH_ROWS = SUBLANES


def _sc_worker_id():
    return lax.axis_index("s") * SC_CORES + lax.axis_index("c")


def _sc_scatter_rows(rows, dest, pad_dest, out_rows):
    n, w = rows.shape
    n_win = dest.shape[1]
    n_pad_win = pad_dest.shape[1]
    assert dest.shape == (SC_WORKERS, n_win, TOP_K, SC_WINDOW) and SC_WORKERS * n_win * SC_WINDOW == n
    mesh = plsc.VectorSubcoreMesh(core_axis_name="c", subcore_axis_name="s")
    zeros = jnp.zeros((SC_WINDOW, w), rows.dtype)

    @functools.partial(
        pl.kernel, mesh=mesh,
        out_type=jax.ShapeDtypeStruct((out_rows, w), rows.dtype),
        scratch_types=[pltpu.VMEM((TOP_K, SC_WINDOW), jnp.int32),
                       pltpu.VMEM((n_pad_win, SC_WINDOW), jnp.int32),
                       pltpu.VMEM((SC_WINDOW, w), rows.dtype)],
        name="sc_scatter_rows",
    )
    def scatter(rows_hbm, dest_hbm, pad_hbm, zeros_hbm, out_hbm, idx_v, pad_v, rows_v):
        wid = _sc_worker_id()

        @pl.loop(0, n_win)
        def _(j):
            base = (wid * n_win + j) * SC_WINDOW
            pltpu.sync_copy(rows_hbm.at[pl.ds(base, SC_WINDOW)], rows_v)
            pltpu.sync_copy(dest_hbm.at[wid, j], idx_v)
            for k in range(TOP_K):
                pltpu.sync_copy(rows_v, out_hbm.at[idx_v.at[k]])

        pltpu.sync_copy(zeros_hbm, rows_v)
        pltpu.sync_copy(pad_hbm.at[wid], pad_v)
        for j in range(n_pad_win):
            pltpu.sync_copy(rows_v, out_hbm.at[pad_v.at[j]])

    return scatter(rows, dest, pad_dest, zeros)


def _expert_kernel(be_ref, first_ref, next_ref, nu_ref, x_ref, wgu_hbm, bgu_ref, wd_hbm, bd_ref,
                   y_ref, wgu_f, wd_f, wgu_s, wd_s, sem):
    i = pl.program_id(0)

    def weight_copies(e):
        return (pltpu.make_async_copy(wgu_hbm.at[e], wgu_f, sem.at[0]),
                pltpu.make_async_copy(wd_hbm.at[e], wd_f, sem.at[1]))

    @pl.when(i == 0)
    def _():
        for cp in weight_copies(be_ref[0]):
            cp.start()

    @pl.when(i < nu_ref[0])
    def _():
        @pl.when(first_ref[i] == 1)
        def _():
            for cp in weight_copies(be_ref[i]):
                cp.wait()
            ck = 256
            for r0 in range(0, D_MODEL, ck):
                wgu_s[r0:r0 + ck, :] = wgu_f[r0:r0 + ck, :].astype(BF16)
                wd_s[r0:r0 + ck, :] = wd_f[r0:r0 + ck, :].astype(BF16)

            @pl.when(next_ref[i] >= 0)
            def _():
                for cp in weight_copies(next_ref[i]):
                    cp.start()

        x = _unpack_rows(x_ref[...]).astype(BF16)
        acc = jnp.zeros((TM_EXPERT, D_MODEL), F32)
        fc = 512
        for c0 in range(0, D_FF, fc):
            gt = jnp.dot(x, wgu_s[:, c0:c0 + fc], preferred_element_type=F32) + bgu_ref[0, :, c0:c0 + fc]
            ln = (jnp.dot(x, wgu_s[:, D_FF + c0:D_FF + c0 + fc], preferred_element_type=F32)
                  + bgu_ref[0, :, D_FF + c0:D_FF + c0 + fc])
            gt = jnp.minimum(gt, SWIGLU_LIMIT)
            ln = jnp.clip(ln, -SWIGLU_LIMIT, SWIGLU_LIMIT)
            act = gt * jax.nn.sigmoid(SWIGLU_ALPHA * gt) * (ln + 1.0)
            acc = acc + jnp.dot(act.astype(BF16), wd_s[c0:c0 + fc, :], preferred_element_type=F32)
        y_ref[...] = _pack_rows(acc + bd_ref[0])

    @pl.when(i >= nu_ref[0])
    def _():
        y_ref[...] = jnp.zeros_like(y_ref)


def _experts(block_e, first, next_e, n_used, xs, w_gu, b_gu, w_down, b_down):
    p_rows = xs.shape[0]
    d = D_MODEL
    tm = TM_EXPERT
    nb = p_rows // tm

    def blk(i, be, fi, nx, nu):
        return jnp.minimum(i, nu[0] - 1)

    xmap = lambda i, be, fi, nx, nu: (blk(i, be, fi, nx, nu), 0)
    bmap = lambda i, be, fi, nx, nu: (be[blk(i, be, fi, nx, nu)], 0, 0)
    grid_spec = pltpu.PrefetchScalarGridSpec(
        num_scalar_prefetch=4,
        grid=(nb,),
        in_specs=[pl.BlockSpec((tm, ROW_WORDS), xmap),
                  pl.BlockSpec(memory_space=pl.ANY),
                  pl.BlockSpec((1, 1, 2 * D_FF), bmap),
                  pl.BlockSpec(memory_space=pl.ANY),
                  pl.BlockSpec((1, 1, d), bmap)],
        out_specs=pl.BlockSpec((tm, ROW_WORDS), lambda i, be, fi, nx, nu: (i, 0)),
        scratch_shapes=[pltpu.VMEM((d, 2 * D_FF), F32), pltpu.VMEM((D_FF, d), F32),
                        pltpu.VMEM((d, 2 * D_FF), BF16), pltpu.VMEM((D_FF, d), BF16),
                        pltpu.SemaphoreType.DMA((2,))],
    )
    return pl.pallas_call(
        _expert_kernel,
        grid_spec=grid_spec,
        out_shape=jax.ShapeDtypeStruct((nb * tm, ROW_WORDS), U32),
        compiler_params=_params("arbitrary"),
        name="moe_experts",
    )(block_e, first, next_e, n_used, xs, w_gu, b_gu.reshape(N_EXPERTS, 1, 2 * D_FF),
      w_down, b_down.reshape(N_EXPERTS, 1, d))


def _sc_gather_rows(table, idx):
    m = idx.shape[0]
    w = table.shape[1]
    per_worker = m // SC_WORKERS
    n_win = per_worker // SC_WINDOW
    assert per_worker * SC_WORKERS == m and n_win * SC_WINDOW == per_worker
    mesh = plsc.VectorSubcoreMesh(core_axis_name="c", subcore_axis_name="s")

    @functools.partial(
        pl.kernel, mesh=mesh,
        out_type=jax.ShapeDtypeStruct((m, w), table.dtype),
        scratch_types=[pltpu.VMEM((SC_WINDOW,), jnp.int32),
                       pltpu.VMEM((SC_WINDOW, w), table.dtype),
                       pltpu.SemaphoreType.DMA],
        name="sc_gather_rows",
    )
    def gather(table_hbm, idx_hbm, out_hbm, idx_v, rows_v, sem):
        wid = _sc_worker_id()

        @pl.loop(0, n_win)
        def _(j):
            base = wid * per_worker + j * SC_WINDOW
            pltpu.sync_copy(idx_hbm.at[pl.ds(base, SC_WINDOW)], idx_v)
            pltpu.async_copy(table_hbm.at[idx_v], rows_v, sem).wait()
            pltpu.sync_copy(rows_v, out_hbm.at[pl.ds(base, SC_WINDOW)])

    return gather(table, idx)


def _combine_dense_kernel(y4_ref, gate_ref, x1_ref, mod_ref, fg_ref, o_ref):
    gate = gate_ref[...]
    y = gate[:, 0:1] * _unpack_rows(y4_ref[0])
    for k in range(1, TOP_K):
        y = y + gate[:, k:k + 1] * _unpack_rows(y4_ref[k])
    x2 = x1_ref[...] + mod_ref[0, 5:6, :] * y
    o_ref[...] = x2 * lax.rsqrt(jnp.mean(x2 * x2, axis=-1, keepdims=True) + NORM_EPS) * fg_ref[...]


def _combine_dense(y4, gate_w, x1, mod, final_g):
    b, l, d = x1.shape
    ts = 512
    nt = l // ts
    return pl.pallas_call(
        _combine_dense_kernel,
        grid=(b * nt,),
        in_specs=[pl.BlockSpec((TOP_K, ts, ROW_WORDS), lambda s: (0, s, 0)),
                  pl.BlockSpec((ts, TOP_K), lambda s: (s, 0)),
                  pl.BlockSpec((ts, d), lambda s: (s, 0)),
                  pl.BlockSpec((1, 6, d), lambda s: (s // nt, 0, 0)),
                  _const_spec((1, d))],
        out_specs=pl.BlockSpec((ts, d), lambda s: (s, 0)),
        out_shape=jax.ShapeDtypeStruct((b * l, d), F32),
        compiler_params=_params("arbitrary"),
        name="moe_combine",
    )(y4, gate_w, x1.reshape(b * l, d), mod, final_g).reshape(b, l, d)


def _rope_tables(seq_len):
    inv_freq = ROPE_BASE ** (-jnp.arange(ROPE_PAIRS, dtype=F32) / ROPE_PAIRS)
    reps = LANES // HEAD_DIM
    out = []
    for n_pos, on_row in ((seq_len // GRID_W, True), (GRID_W, False)):
        ang = jnp.arange(n_pos, dtype=F32)[:, None] * inv_freq
        zero = jnp.zeros_like(ang)
        for fn, sign in ((jnp.cos, 1.0), (jnp.sin, -1.0)):
            v = fn(ang)
            head = ([sign * v, v, zero, zero] if on_row else [zero, zero, sign * v, v])
            out.append(jnp.concatenate(head * reps, axis=1))
    return out


def kernel(x, c, ctx, c_ctx, w_ada, b_ada, norm1_g, norm2_g, w_in, b_in, attn_sink, w_pool,
           pool_scale, w_attn_br, w_pool_br, w_out, w_router, b_router, w_gu, b_gu, w_down,
           b_down, final_g):
    b, l, d = x.shape
    n = b * l
    assert w_ada.shape[0] == 1, "single-layer block"

    cond = jnp.zeros((SUBLANES, d), F32).at[:b].set(c).at[b].set(c_ctx)
    mod = _adaln(cond, w_ada[0], b_ada[0])[:b + 1].reshape(b + 1, 6, d)
    mod_x, mod_c = mod[:b], mod[b:b + 1]

    rope = _rope_tables(l)
    w_in_bf = w_in[0].astype(BF16)
    b_in2 = b_in[0].reshape(1, IN_W)
    g1 = norm1_g[0].reshape(1, d)
    q, k, v, u, ga, gp = _inproj(x, mod_x, g1, w_in_bf, b_in2, rope)
    kv_sl = slice(ATTN_W, ATTN_W + 2 * KV_W)
    kx, vx = _ctx_kv(ctx, mod_c, g1, w_in_bf[:, kv_sl], b_in2[:, kv_sl])

    attn_o = _attention(q, k, v, kx, vx, attn_sink[0])

    x1, h2, top_idx, gate_w, rank, counts = _merge_route(
        attn_o, u, ga, gp, x, mod_x, w_pool[0].astype(BF16), pool_scale[0].reshape(1, d),
        w_attn_br[0].astype(BF16), w_pool_br[0].astype(BF16), w_out[0].astype(BF16),
        norm2_g[0].reshape(1, d), w_router[0].astype(BF16), b_router[0].reshape(1, N_EXPERTS))

    tm = TM_EXPERT
    nb = n * TOP_K // tm + N_EXPERTS
    cnt = counts[0].astype(jnp.int32)
    padded = (cnt + tm - 1) // tm * tm
    pend = jnp.cumsum(padded)
    pstart = pend - padded
    expert_ids = jnp.arange(N_EXPERTS, dtype=jnp.int32)
    dest = (jnp.sum(jnp.where(top_idx[..., None] == expert_ids, pstart, 0), axis=-1)
            + rank).reshape(-1)
    n_used = (pend[-1] // tm).reshape(1)
    block_start = jnp.arange(nb, dtype=jnp.int32) * tm
    block_e = jnp.minimum(jnp.sum((pend[None, :] <= block_start[:, None]).astype(jnp.int32), axis=1),
                          N_EXPERTS - 1)
    first = jnp.concatenate([jnp.ones((1,), jnp.int32),
                             (block_e[1:] != block_e[:-1]).astype(jnp.int32)])
    later_used = (expert_ids[None, :] > expert_ids[:, None]) & (cnt[None, :] > 0)
    next_of_expert = jnp.min(jnp.where(later_used, expert_ids[None, :], N_EXPERTS), axis=1)
    next_of_expert = jnp.where(next_of_expert < N_EXPERTS, next_of_expert, -1)
    next_e = jnp.sum(jnp.where(block_e[:, None] == expert_ids, next_of_expert, 0), axis=-1)

    n_win = n // (SC_WORKERS * SC_WINDOW)
    dest_w = dest.reshape(SC_WORKERS, n_win, SC_WINDOW, TOP_K).transpose(0, 1, 3, 2)
    pad_j = jnp.arange(tm, dtype=jnp.int32)[None, :]
    pad_dest = jnp.where(pad_j < (padded - cnt)[:, None], (pstart + cnt)[:, None] + pad_j, nb * tm)
    pad_dest = pad_dest.reshape(SC_WORKERS, N_EXPERTS * tm // (SC_WORKERS * SC_WINDOW), SC_WINDOW)

    xs = _sc_scatter_rows(h2, dest_w, pad_dest, nb * tm + TRASH_ROWS)
    ys = _experts(block_e, first, next_e, n_used, xs, w_gu[0], b_gu[0], w_down[0], b_down[0])
    dest_slot_major = dest.reshape(n, TOP_K).T.reshape(-1)
    y4 = _sc_gather_rows(ys, dest_slot_major).reshape(TOP_K, n, ROW_WORDS)
    return _combine_dense(y4, gate_w, x1, mod_x, final_g.reshape(1, d))
```

```python
import functools

import jax
import jax.numpy as jnp
from jax import lax
from jax.experimental import pallas as pl
from jax.experimental.pallas import tpu as pltpu
from jax.experimental.pallas import tpu_sc as plsc

D_MODEL = 1024
GRID_W = 64
HEAD_DIM = 64
N_HEADS = 16
N_KV_HEADS = 4
GROUP = N_HEADS // N_KV_HEADS
ATTN_W = N_HEADS * HEAD_DIM
KV_W = N_KV_HEADS * HEAD_DIM
WINDOW = 128
ATTN_SCALE = HEAD_DIM ** -0.5
ROPE_BASE = 10000.0
ROPE_PAIRS = HEAD_DIM // 4
POOL_WINDOWS = (2, 4, 8, 16)
POOL_GROUP_W = D_MODEL // len(POOL_WINDOWS)
IN_W = ATTN_W + 2 * KV_W + D_MODEL + 2 * D_MODEL
N_EXPERTS = 32
TOP_K = 4
D_FF = D_MODEL
SWIGLU_ALPHA = 1.702
SWIGLU_LIMIT = 7.0
NORM_EPS = 1e-5
NEG_INF = -1e30

LANES = 128
SUBLANES = 8
VMEM_LIMIT = 56 * 1024 * 1024

TM_INPROJ = 512
TQ = WINDOW
TM_MERGE = 256
TM_EXPERT = 256
POOL_HALO = SUBLANES

F32 = jnp.float32
BF16 = jnp.bfloat16


ROW_WORDS = D_MODEL // 2
U32 = jnp.uint32


def _pack_rows(value):
    bits = lax.bitcast_convert_type(value.astype(BF16).astype(F32), U32)
    return bits[:, :ROW_WORDS] | (bits[:, ROW_WORDS:] >> 16)


def _unpack_rows(words):
    hi = lax.bitcast_convert_type(words & jnp.uint32(0xFFFF0000), F32)
    lo = lax.bitcast_convert_type(words << 16, F32)
    return jnp.concatenate([hi, lo], axis=1)


def _params(*sem):
    return pltpu.CompilerParams(dimension_semantics=sem, vmem_limit_bytes=VMEM_LIMIT)


def _const_spec(shape):
    nd = len(shape)
    return pl.BlockSpec(shape, lambda *_: (0,) * nd)


def _adaln_kernel(c_ref, w_ref, b_ref, o_ref):
    c = c_ref[...]
    s = c * jax.nn.sigmoid(c)
    o_ref[...] = jnp.dot(s.astype(BF16), w_ref[...].astype(BF16),
                         preferred_element_type=F32) + b_ref[...]


def _adaln(cond, w_ada, b_ada):
    rows, d = cond.shape
    n = w_ada.shape[1]
    tn = 1024
    return pl.pallas_call(
        _adaln_kernel,
        grid=(n // tn,),
        in_specs=[_const_spec((rows, d)),
                  pl.BlockSpec((d, tn), lambda j: (0, j)),
                  pl.BlockSpec((1, tn), lambda j: (0, j))],
        out_specs=pl.BlockSpec((rows, tn), lambda j: (0, j)),
        out_shape=jax.ShapeDtypeStruct((rows, n), F32),
        compiler_params=_params("arbitrary"),
        name="adaln",
    )(cond, w_ada, b_ada.reshape(1, n))


def _norm_mod(x, g, shift, scale):
    y = x * lax.rsqrt(jnp.mean(x * x, axis=-1, keepdims=True) + NORM_EPS) * g
    return y * (1.0 + scale) + shift


def _rope(t, cos, sin_signed):
    w = t.shape[1]
    half = ROPE_PAIRS
    lane = lax.broadcasted_iota(jnp.int32, t.shape, 1)
    first = (lane & (2 * half - 1)) < half
    rot = jnp.where(first, pltpu.roll(t, w - half, 1), pltpu.roll(t, half, 1))
    reps = w // LANES
    cos_w = jnp.concatenate([cos] * reps, axis=1) if reps > 1 else cos
    sin_w = jnp.concatenate([sin_signed] * reps, axis=1) if reps > 1 else sin_signed
    return t * cos_w + rot * sin_w


def _token_table(row_ref, col_ref):
    gr = row_ref.shape[0]
    shape = (gr, GRID_W, LANES)
    full = (jnp.broadcast_to(row_ref[...][:, None, :], shape)
            + jnp.broadcast_to(col_ref[...][None, :, :], shape))
    return full.reshape(gr * GRID_W, LANES)


def _inproj_kernel(x_ref, mod_ref, g_ref, w_ref, b_ref, cr_ref, sr_ref, cc_ref, sc_ref,
                   q_ref, k_ref, v_ref, u_ref, ga_ref, gp_ref):
    h = _norm_mod(x_ref[0], g_ref[...], mod_ref[0, 0:1, :], mod_ref[0, 1:2, :])
    hb = h.astype(BF16)
    cos = _token_table(cr_ref, cc_ref)
    sin = _token_table(sr_ref, sc_ref)

    def proj(c0, c1):
        return jnp.dot(hb, w_ref[:, c0:c1], preferred_element_type=F32) + b_ref[:, c0:c1]

    cw = 512
    for c0 in range(0, ATTN_W, cw):
        q_ref[0, :, c0:c0 + cw] = (_rope(proj(c0, c0 + cw), cos, sin) * ATTN_SCALE).astype(BF16)
    k_ref[0] = _rope(proj(ATTN_W, ATTN_W + KV_W), cos, sin).astype(BF16)
    v_ref[0] = proj(ATTN_W + KV_W, ATTN_W + 2 * KV_W).astype(BF16)
    base = ATTN_W + 2 * KV_W
    for ref in (u_ref, ga_ref, gp_ref):
        for c0 in range(0, D_MODEL, cw):
            ref[0, :, c0:c0 + cw] = proj(base + c0, base + c0 + cw)
        base += D_MODEL


def _inproj(x, mod, norm_g, w_in_bf, b_in, rope):
    b, l, d = x.shape
    tm = TM_INPROJ
    row = lambda bi, i: (bi, i, 0)
    out_shape = (
        jax.ShapeDtypeStruct((b, l, ATTN_W), BF16),
        jax.ShapeDtypeStruct((b, l, KV_W), BF16),
        jax.ShapeDtypeStruct((b, l, KV_W), BF16),
        jax.ShapeDtypeStruct((b, l, D_MODEL), F32),
        jax.ShapeDtypeStruct((b, l, D_MODEL), F32),
        jax.ShapeDtypeStruct((b, l, D_MODEL), F32),
    )
    return pl.pallas_call(
        _inproj_kernel,
        grid=(b, l // tm),
        in_specs=[pl.BlockSpec((1, tm, d), row),
                  pl.BlockSpec((1, 6, d), lambda bi, i: (bi, 0, 0)),
                  _const_spec((1, d)),
                  _const_spec((d, IN_W)),
                  _const_spec((1, IN_W)),
                  pl.BlockSpec((tm // GRID_W, LANES), lambda bi, i: (i, 0)),
                  pl.BlockSpec((tm // GRID_W, LANES), lambda bi, i: (i, 0)),
                  _const_spec((GRID_W, LANES)),
                  _const_spec((GRID_W, LANES))],
        out_specs=(pl.BlockSpec((1, tm, ATTN_W), row),
                   pl.BlockSpec((1, tm, KV_W), row),
                   pl.BlockSpec((1, tm, KV_W), row),
                   pl.BlockSpec((1, tm, D_MODEL), row),
                   pl.BlockSpec((1, tm, D_MODEL), row),
                   pl.BlockSpec((1, tm, D_MODEL), row)),
        out_shape=out_shape,
        compiler_params=_params("arbitrary", "arbitrary"),
        name="inproj",
    )(x, mod, norm_g, w_in_bf, b_in, *rope)


def _ctx_kv_kernel(x_ref, mod_ref, g_ref, w_ref, b_ref, k_ref, v_ref):
    h = _norm_mod(x_ref[0], g_ref[...], mod_ref[0, 0:1, :], mod_ref[0, 1:2, :])
    kv = jnp.dot(h.astype(BF16), w_ref[...], preferred_element_type=F32) + b_ref[...]
    k_ref[0] = kv[:, :KV_W].astype(BF16)
    v_ref[0] = kv[:, KV_W:].astype(BF16)


def _ctx_kv(ctx, mod_c, norm_g, w_kv_bf, b_kv):
    b, lc, d = ctx.shape
    row = lambda bi: (bi, 0, 0)
    return pl.pallas_call(
        _ctx_kv_kernel,
        grid=(b,),
        in_specs=[pl.BlockSpec((1, lc, d), row),
                  _const_spec((1, 6, d)),
                  _const_spec((1, d)),
                  _const_spec((d, 2 * KV_W)),
                  _const_spec((1, 2 * KV_W))],
        out_specs=(pl.BlockSpec((1, lc, KV_W), row), pl.BlockSpec((1, lc, KV_W), row)),
        out_shape=(jax.ShapeDtypeStruct((b, lc, KV_W), BF16),
                   jax.ShapeDtypeStruct((b, lc, KV_W), BF16)),
        compiler_params=_params("arbitrary"),
        name="ctx_kv",
    )(ctx, mod_c, norm_g, w_kv_bf, b_kv)


def _dot_nt(a, b):
    return lax.dot_general(a, b, (((1,), (1,)), ((), ())), preferred_element_type=F32)


def _attn_kernel(sink_ref, q_ref, kp_ref, kc_ref, kn_ref, vp_ref, vc_ref, vn_ref,
                 kx_ref, vx_ref, o_ref):
    n = pl.program_id(1)
    last = pl.num_programs(1) - 1
    rows = GROUP * TQ
    n_loc = 3 * TQ
    qi = lax.broadcasted_iota(jnp.int32, (rows, n_loc), 0) & (TQ - 1)
    col = lax.broadcasted_iota(jnp.int32, (rows, n_loc), 1)
    kj = col - TQ
    valid = (jnp.abs(qi - kj) <= WINDOW)
    valid = valid & ((col >= TQ) | (n > 0)) & ((col < 2 * TQ) | (n < last))
    row_id = lax.broadcasted_iota(jnp.int32, (rows, 1), 0)

    for j in range(N_KV_HEADS):
        hs = slice(j * HEAD_DIM, (j + 1) * HEAD_DIM)
        q4 = jnp.concatenate(
            [q_ref[0, :, (j * GROUP + g) * HEAD_DIM:(j * GROUP + g + 1) * HEAD_DIM]
             for g in range(GROUP)], axis=0)
        k_loc = jnp.concatenate([kp_ref[0, :, hs], kc_ref[0, :, hs], kn_ref[0, :, hs]], axis=0)
        v_loc = jnp.concatenate([vp_ref[0, :, hs], vc_ref[0, :, hs], vn_ref[0, :, hs]], axis=0)
        s_loc = jnp.where(valid, _dot_nt(q4, k_loc), NEG_INF)
        s_ctx = _dot_nt(q4, kx_ref[0, :, hs])
        sink = jnp.zeros((rows, 1), F32)
        for g in range(GROUP):
            sink = jnp.where(row_id // TQ == g, sink_ref[j * GROUP + g], sink)
        m = jnp.maximum(jnp.maximum(jnp.max(s_loc, axis=-1, keepdims=True),
                                    jnp.max(s_ctx, axis=-1, keepdims=True)), sink)
        p_loc = jnp.exp(s_loc - m)
        p_ctx = jnp.exp(s_ctx - m)
        denom = (jnp.sum(p_loc, axis=-1, keepdims=True) + jnp.sum(p_ctx, axis=-1, keepdims=True)
                 + jnp.exp(sink - m))
        o = (jnp.dot(p_loc.astype(BF16), v_loc, preferred_element_type=F32)
             + jnp.dot(p_ctx.astype(BF16), vx_ref[0, :, hs], preferred_element_type=F32))
        o = o / denom
        for g in range(GROUP):
            c0 = (j * GROUP + g) * HEAD_DIM
            o_ref[0, :, c0:c0 + HEAD_DIM] = o[g * TQ:(g + 1) * TQ, :].astype(BF16)


def _attention(q, k, v, kx, vx, sink):
    b, l, _ = q.shape
    lc = kx.shape[1]
    nblk = l // TQ
    cur = lambda bi, n: (bi, n, 0)
    prev = lambda bi, n: (bi, jnp.maximum(n - 1, 0), 0)
    nxt = lambda bi, n: (bi, jnp.minimum(n + 1, nblk - 1), 0)
    kvb = (1, TQ, KV_W)
    return pl.pallas_call(
        _attn_kernel,
        grid=(b, nblk),
        in_specs=[pl.BlockSpec(memory_space=pltpu.SMEM),
                  pl.BlockSpec((1, TQ, ATTN_W), cur),
                  pl.BlockSpec(kvb, prev), pl.BlockSpec(kvb, cur), pl.BlockSpec(kvb, nxt),
                  pl.BlockSpec(kvb, prev), pl.BlockSpec(kvb, cur), pl.BlockSpec(kvb, nxt),
                  pl.BlockSpec((1, lc, KV_W), lambda bi, n: (bi, 0, 0)),
                  pl.BlockSpec((1, lc, KV_W), lambda bi, n: (bi, 0, 0))],
        out_specs=pl.BlockSpec((1, TQ, ATTN_W), cur),
        out_shape=jax.ShapeDtypeStruct((b, l, ATTN_W), BF16),
        compiler_params=_params("arbitrary", "arbitrary"),
        name="attention",
    )(sink, q, k, k, k, v, v, v, kx, vx)


def _shift_rows(a, d):
    n = a.shape[0]
    return pltpu.roll(a, (-d) % n, 0)


def _merge_kernel(ao_ref, u_ref, up_ref, un_ref, ga_ref, gp_ref, x_ref, mod_ref,
                  wpool_ref, pscale_ref, wab_ref, wpb_ref, wout_ref, g2_ref, wr_ref, br_ref,
                  x1_ref, h2_ref, idx_ref, gate_ref, rank_ref, cnt_ref, carry_ref, *, seq_len):
    bi = pl.program_id(0)
    i = pl.program_id(1)
    last = pl.num_programs(1) - 1
    tm = TM_MERGE

    @pl.when((bi == 0) & (i == 0))
    def _():
        carry_ref[...] = jnp.zeros_like(carry_ref)

    u = u_ref[0]
    prev = jnp.where(i > 0, up_ref[0], 0.0)
    nxt = jnp.where(i < last, un_ref[0], 0.0)
    ext = jnp.concatenate([prev, u, nxt], axis=0)
    t = i * tm + lax.broadcasted_iota(jnp.int32, (tm, 1), 0)
    mixed = []
    for g, w in enumerate(POOL_WINDOWS):
        cs = slice(g * POOL_GROUP_W, (g + 1) * POOL_GROUP_W)
        e = ext[:, cs]
        acc = _shift_rows(e, -1) + e
        step = 1
        while 2 * step < w:
            acc = _shift_rows(acc, -step) + _shift_rows(acc, step)
            step *= 2
        win = acc[POOL_HALO:POOL_HALO + tm]
        half = w // 2
        cnt = (jnp.minimum(t + half, seq_len) - jnp.maximum(t - half, 0)).astype(F32)
        diff = (win / cnt - u[:, cs]).astype(BF16)
        mixed.append(jnp.dot(diff, wpool_ref[g], preferred_element_type=F32))
    pool_o = jnp.concatenate(mixed, axis=1) * pscale_ref[...]

    a = jnp.dot(ao_ref[0], wab_ref[...], preferred_element_type=F32)
    p = jnp.dot(pool_o.astype(BF16), wpb_ref[...], preferred_element_type=F32)
    merged = jax.nn.sigmoid(ga_ref[0]) * a + jax.nn.sigmoid(gp_ref[0]) * p
    o = jnp.dot(merged.astype(BF16), wout_ref[...], preferred_element_type=F32)
    x1 = x_ref[0] + mod_ref[0, 2:3, :] * o
    x1_ref[0] = x1

    h2 = _norm_mod(x1, g2_ref[...], mod_ref[0, 3:4, :], mod_ref[0, 4:5, :])
    h2_ref[...] = _pack_rows(h2)
    logits = jnp.dot(h2.astype(BF16), wr_ref[...], preferred_element_type=F32) + br_ref[...]
    lane = lax.broadcasted_iota(jnp.int32, (tm, N_EXPERTS), 1).astype(F32)
    work = logits
    vals, idxs, sels = [], [], []
    for _ in range(TOP_K):
        mx = jnp.max(work, axis=-1, keepdims=True)
        ix = jnp.min(jnp.where(work == mx, lane, float(N_EXPERTS)), axis=-1, keepdims=True)
        sel = lane == ix
        work = jnp.where(sel, -jnp.inf, work)
        vals.append(mx)
        idxs.append(ix)
        sels.append(sel)
    ex = [jnp.exp(v - vals[0]) for v in vals]
    tot = ex[0] + ex[1] + ex[2] + ex[3]
    gate_ref[...] = jnp.concatenate([e_ / tot for e_ in ex], axis=1)
    idx_ref[...] = jnp.concatenate(idxs, axis=1).astype(jnp.int32)

    member = (sels[0] | sels[1] | sels[2] | sels[3]).astype(F32)
    r_id = lax.broadcasted_iota(jnp.int32, (tm, tm), 0)
    c_id = lax.broadcasted_iota(jnp.int32, (tm, tm), 1)
    tri = (r_id > c_id).astype(BF16)
    before = jnp.dot(tri, member.astype(BF16), preferred_element_type=F32) + carry_ref[...]
    ranks = [jnp.sum(jnp.where(s, before, 0.0), axis=-1, keepdims=True) for s in sels]
    rank_ref[...] = jnp.concatenate(ranks, axis=1).astype(jnp.int32)
    total = carry_ref[...] + jnp.sum(member, axis=0, keepdims=True)
    carry_ref[...] = total
    cnt_ref[...] = total


def _merge_route(attn_o, u, ga, gp, x, mod, w_pool_bf, pool_scale, wab, wpb, wout,
                 norm2_g, w_router, b_router):
    b, l, d = x.shape
    tm = TM_MERGE
    nt = l // tm
    n = b * l
    hb = tm // POOL_HALO
    row = lambda bi, i: (bi, i, 0)
    flat = lambda bi, i: (bi * nt + i, 0)
    halo_prev = lambda bi, i: (bi, jnp.maximum(i * hb - 1, 0), 0)
    halo_next = lambda bi, i: (bi, jnp.minimum((i + 1) * hb, l // POOL_HALO - 1), 0)
    out_shape = (
        jax.ShapeDtypeStruct((b, l, d), F32),
        jax.ShapeDtypeStruct((n, ROW_WORDS), U32),
        jax.ShapeDtypeStruct((n, TOP_K), jnp.int32),
        jax.ShapeDtypeStruct((n, TOP_K), F32),
        jax.ShapeDtypeStruct((n, TOP_K), jnp.int32),
        jax.ShapeDtypeStruct((1, N_EXPERTS), F32),
    )
    return pl.pallas_call(
        functools.partial(_merge_kernel, seq_len=l),
        grid=(b, nt),
        in_specs=[pl.BlockSpec((1, tm, d), row),
                  pl.BlockSpec((1, tm, d), row),
                  pl.BlockSpec((1, POOL_HALO, d), halo_prev),
                  pl.BlockSpec((1, POOL_HALO, d), halo_next),
                  pl.BlockSpec((1, tm, d), row),
                  pl.BlockSpec((1, tm, d), row),
                  pl.BlockSpec((1, tm, d), row),
                  pl.BlockSpec((1, 6, d), lambda bi, i: (bi, 0, 0)),
                  _const_spec(w_pool_bf.shape),
                  _const_spec((1, d)),
                  _const_spec((d, d)), _const_spec((d, d)), _const_spec((d, d)),
                  _const_spec((1, d)),
                  _const_spec((d, N_EXPERTS)),
                  _const_spec((1, N_EXPERTS))],
        out_specs=(pl.BlockSpec((1, tm, d), row),
                   pl.BlockSpec((tm, ROW_WORDS), flat),
                   pl.BlockSpec((tm, TOP_K), flat),
                   pl.BlockSpec((tm, TOP_K), flat),
                   pl.BlockSpec((tm, TOP_K), flat),
                   _const_spec((1, N_EXPERTS))),
        out_shape=out_shape,
        scratch_shapes=[pltpu.VMEM((1, N_EXPERTS), F32)],
        compiler_params=_params("arbitrary", "arbitrary"),
        name="merge_route",
    )(attn_o, u, u, u, ga, gp, x, mod, w_pool_bf, pool_scale, wab, wpb, wout,
      norm2_g, w_router, b_router)


SC_CORES = 2
SC_SUBCORES = 16
SC_WORKERS = SC_CORES * SC_SUBCORES
SC_WINDOW = 128


def _sc_worker_id():
    return lax.axis_index("s") * SC_CORES + lax.axis_index("c")


def _sc_scatter_rows(rows, dest, pad_dest, out_rows):
    n, w = rows.shape
    n_win = dest.shape[1]
    n_pad_win = pad_dest.shape[1]
    assert dest.shape == (SC_WORKERS, n_win, TOP_K, SC_WINDOW) and SC_WORKERS * n_win * SC_WINDOW == n
    mesh = plsc.VectorSubcoreMesh(core_axis_name="c", subcore_axis_name="s")
    zeros = jnp.zeros((SC_WINDOW, w), rows.dtype)

    @functools.partial(
        pl.kernel, mesh=mesh,
        out_type=jax.ShapeDtypeStruct((out_rows, w), rows.dtype),
        scratch_types=[pltpu.VMEM((TOP_K, SC_WINDOW), jnp.int32),
                       pltpu.VMEM((n_pad_win, SC_WINDOW), jnp.int32),
                       pltpu.VMEM((SC_WINDOW, w), rows.dtype),
                       pltpu.SemaphoreType.DMA],
        name="sc_scatter_rows",
    )
    def scatter(rows_hbm, dest_hbm, pad_hbm, zeros_hbm, out_hbm, idx_v, pad_v, rows_v, sem):
        wid = _sc_worker_id()

        @pl.loop(0, n_win)
        def _(j):
            base = (wid * n_win + j) * SC_WINDOW
            pltpu.sync_copy(rows_hbm.at[pl.ds(base, SC_WINDOW)], rows_v)
            pltpu.sync_copy(dest_hbm.at[wid, j], idx_v)
            copies = [pltpu.async_copy(rows_v, out_hbm.at[idx_v.at[k]], sem) for k in range(TOP_K)]
            for cp in copies:
                cp.wait()

        pltpu.sync_copy(zeros_hbm, rows_v)
        pltpu.sync_copy(pad_hbm.at[wid], pad_v)
        copies = [pltpu.async_copy(rows_v, out_hbm.at[pad_v.at[j]], sem) for j in range(n_pad_win)]
        for cp in copies:
            cp.wait()

    return scatter(rows, dest, pad_dest, zeros)


def _expert_kernel(be_ref, first_ref, next_ref, nu_ref, x_ref, wgu_hbm, bgu_ref, wd_hbm, bd_ref,
                   y_ref, wgu_f, wd_f, wgu_s, wd_s, sem):
    i = pl.program_id(0)

    def weight_copies(e):
        return (pltpu.make_async_copy(wgu_hbm.at[e], wgu_f, sem.at[0]),
                pltpu.make_async_copy(wd_hbm.at[e], wd_f, sem.at[1]))

    @pl.when(i == 0)
    def _():
        for cp in weight_copies(be_ref[0]):
            cp.start()

    @pl.when(i < nu_ref[0])
    def _():
        @pl.when(first_ref[i] == 1)
        def _():
            for cp in weight_copies(be_ref[i]):
                cp.wait()
            ck = 256
            for r0 in range(0, D_MODEL, ck):
                wgu_s[r0:r0 + ck, :] = wgu_f[r0:r0 + ck, :].astype(BF16)
                wd_s[r0:r0 + ck, :] = wd_f[r0:r0 + ck, :].astype(BF16)

            @pl.when(next_ref[i] >= 0)
            def _():
                for cp in weight_copies(next_ref[i]):
                    cp.start()

        x = _unpack_rows(x_ref[...]).astype(BF16)
        acc = jnp.zeros((TM_EXPERT, D_MODEL), F32)
        fc = 512
        for c0 in range(0, D_FF, fc):
            gt = jnp.dot(x, wgu_s[:, c0:c0 + fc], preferred_element_type=F32) + bgu_ref[0, :, c0:c0 + fc]
            ln = (jnp.dot(x, wgu_s[:, D_FF + c0:D_FF + c0 + fc], preferred_element_type=F32)
                  + bgu_ref[0, :, D_FF + c0:D_FF + c0 + fc])
            gt = jnp.minimum(gt, SWIGLU_LIMIT)
            ln = jnp.clip(ln, -SWIGLU_LIMIT, SWIGLU_LIMIT)
            act = gt * jax.nn.sigmoid(SWIGLU_ALPHA * gt) * (ln + 1.0)
            acc = acc + jnp.dot(act.astype(BF16), wd_s[c0:c0 + fc, :], preferred_element_type=F32)
        y_ref[...] = _pack_rows(acc + bd_ref[0])

    @pl.when(i >= nu_ref[0])
    def _():
        y_ref[...] = jnp.zeros_like(y_ref)


def _experts(block_e, first, next_e, n_used, xs, w_gu, b_gu, w_down, b_down):
    d = D_MODEL
    tm = TM_EXPERT
    nb = block_e.shape[0]

    def blk(i, be, fi, nx, nu):
        return jnp.minimum(i, nu[0] - 1)

    xmap = lambda i, be, fi, nx, nu: (blk(i, be, fi, nx, nu), 0)
    bmap = lambda i, be, fi, nx, nu: (be[blk(i, be, fi, nx, nu)], 0, 0)
    grid_spec = pltpu.PrefetchScalarGridSpec(
        num_scalar_prefetch=4,
        grid=(nb,),
        in_specs=[pl.BlockSpec((tm, ROW_WORDS), xmap),
                  pl.BlockSpec(memory_space=pl.ANY),
                  pl.BlockSpec((1, 1, 2 * D_FF), bmap),
                  pl.BlockSpec(memory_space=pl.ANY),
                  pl.BlockSpec((1, 1, d), bmap)],
        out_specs=pl.BlockSpec((tm, ROW_WORDS), lambda i, be, fi, nx, nu: (i, 0)),
        scratch_shapes=[pltpu.VMEM((d, 2 * D_FF), F32), pltpu.VMEM((D_FF, d), F32),
                        pltpu.VMEM((d, 2 * D_FF), BF16), pltpu.VMEM((D_FF, d), BF16),
                        pltpu.SemaphoreType.DMA((2,))],
    )
    return pl.pallas_call(
        _expert_kernel,
        grid_spec=grid_spec,
        out_shape=jax.ShapeDtypeStruct((nb * tm, ROW_WORDS), U32),
        compiler_params=_params("arbitrary"),
        name="moe_experts",
    )(block_e, first, next_e, n_used, xs, w_gu, b_gu.reshape(N_EXPERTS, 1, 2 * D_FF),
      w_down, b_down.reshape(N_EXPERTS, 1, d))


def _sc_gather_rows(table, idx):
    m = idx.shape[0]
    w = table.shape[1]
    per_worker = m // SC_WORKERS
    n_win = per_worker // SC_WINDOW
    assert per_worker * SC_WORKERS == m and n_win * SC_WINDOW == per_worker
    mesh = plsc.VectorSubcoreMesh(core_axis_name="c", subcore_axis_name="s")

    @functools.partial(
        pl.kernel, mesh=mesh,
        out_type=jax.ShapeDtypeStruct((m, w), table.dtype),
        scratch_types=[pltpu.VMEM((SC_WINDOW,), jnp.int32),
                       pltpu.VMEM((SC_WINDOW, w), table.dtype),
                       pltpu.SemaphoreType.DMA],
        name="sc_gather_rows",
    )
    def gather(table_hbm, idx_hbm, out_hbm, idx_v, rows_v, sem):
        wid = _sc_worker_id()

        @pl.loop(0, n_win)
        def _(j):
            base = wid * per_worker + j * SC_WINDOW
            pltpu.sync_copy(idx_hbm.at[pl.ds(base, SC_WINDOW)], idx_v)
            pltpu.async_copy(table_hbm.at[idx_v], rows_v, sem).wait()
            pltpu.sync_copy(rows_v, out_hbm.at[pl.ds(base, SC_WINDOW)])

    return gather(table, idx)


def _combine_dense_kernel(y4_ref, gate_ref, x1_ref, mod_ref, fg_ref, o_ref):
    gate = gate_ref[...]
    y = gate[:, 0:1] * _unpack_rows(y4_ref[0])
    for k in range(1, TOP_K):
        y = y + gate[:, k:k + 1] * _unpack_rows(y4_ref[k])
    x2 = x1_ref[...] + mod_ref[0, 5:6, :] * y
    o_ref[...] = x2 * lax.rsqrt(jnp.mean(x2 * x2, axis=-1, keepdims=True) + NORM_EPS) * fg_ref[...]


def _combine_dense(y4, gate_w, x1, mod, final_g):
    b, l, d = x1.shape
    ts = 512
    nt = l // ts
    return pl.pallas_call(
        _combine_dense_kernel,
        grid=(b * nt,),
        in_specs=[pl.BlockSpec((TOP_K, ts, ROW_WORDS), lambda s: (0, s, 0)),
                  pl.BlockSpec((ts, TOP_K), lambda s: (s, 0)),
                  pl.BlockSpec((ts, d), lambda s: (s, 0)),
                  pl.BlockSpec((1, 6, d), lambda s: (s // nt, 0, 0)),
                  _const_spec((1, d))],
        out_specs=pl.BlockSpec((ts, d), lambda s: (s, 0)),
        out_shape=jax.ShapeDtypeStruct((b * l, d), F32),
        compiler_params=_params("arbitrary"),
        name="moe_combine",
    )(y4, gate_w, x1.reshape(b * l, d), mod, final_g).reshape(b, l, d)


def _rope_tables(seq_len):
    inv_freq = ROPE_BASE ** (-jnp.arange(ROPE_PAIRS, dtype=F32) / ROPE_PAIRS)
    reps = LANES // HEAD_DIM
    out = []
    for n_pos, on_row in ((seq_len // GRID_W, True), (GRID_W, False)):
        ang = jnp.arange(n_pos, dtype=F32)[:, None] * inv_freq
        zero = jnp.zeros_like(ang)
        for fn, sign in ((jnp.cos, 1.0), (jnp.sin, -1.0)):
            v = fn(ang)
            head = ([sign * v, v, zero, zero] if on_row else [zero, zero, sign * v, v])
            out.append(jnp.concatenate(head * reps, axis=1))
    return out


def kernel(x, c, ctx, c_ctx, w_ada, b_ada, norm1_g, norm2_g, w_in, b_in, attn_sink, w_pool,
           pool_scale, w_attn_br, w_pool_br, w_out, w_router, b_router, w_gu, b_gu, w_down,
           b_down, final_g):
    b, l, d = x.shape
    n = b * l
    assert w_ada.shape[0] == 1, "single-layer block"

    cond = jnp.zeros((SUBLANES, d), F32).at[:b].set(c).at[b].set(c_ctx)
    mod = _adaln(cond, w_ada[0], b_ada[0])[:b + 1].reshape(b + 1, 6, d)
    mod_x, mod_c = mod[:b], mod[b:b + 1]

    rope = _rope_tables(l)
    w_in_bf = w_in[0].astype(BF16)
    b_in2 = b_in[0].reshape(1, IN_W)
    g1 = norm1_g[0].reshape(1, d)
    q, k, v, u, ga, gp = _inproj(x, mod_x, g1, w_in_bf, b_in2, rope)
    kv_sl = slice(ATTN_W, ATTN_W + 2 * KV_W)
    kx, vx = _ctx_kv(ctx, mod_c, g1, w_in_bf[:, kv_sl], b_in2[:, kv_sl])

    attn_o = _attention(q, k, v, kx, vx, attn_sink[0])

    x1, h2, top_idx, gate_w, rank, counts = _merge_route(
        attn_o, u, ga, gp, x, mod_x, w_pool[0].astype(BF16), pool_scale[0].reshape(1, d),
        w_attn_br[0].astype(BF16), w_pool_br[0].astype(BF16), w_out[0].astype(BF16),
        norm2_g[0].reshape(1, d), w_router[0].astype(BF16), b_router[0].reshape(1, N_EXPERTS))

    tm = TM_EXPERT
    nb = n * TOP_K // tm + N_EXPERTS
    cnt = counts[0].astype(jnp.int32)
    padded = (cnt + tm - 1) // tm * tm
    pend = jnp.cumsum(padded)
    pstart = pend - padded
    expert_ids = jnp.arange(N_EXPERTS, dtype=jnp.int32)
    dest = (jnp.sum(jnp.where(top_idx[..., None] == expert_ids, pstart, 0), axis=-1)
            + rank).reshape(-1)
    n_used = (pend[-1] // tm).reshape(1)
    block_start = jnp.arange(nb, dtype=jnp.int32) * tm
    block_e = jnp.minimum(jnp.sum((pend[None, :] <= block_start[:, None]).astype(jnp.int32), axis=1),
                          N_EXPERTS - 1)
    first = jnp.concatenate([jnp.ones((1,), jnp.int32),
                             (block_e[1:] != block_e[:-1]).astype(jnp.int32)])
    later_used = (expert_ids[None, :] > expert_ids[:, None]) & (cnt[None, :] > 0)
    next_of_expert = jnp.min(jnp.where(later_used, expert_ids[None, :], N_EXPERTS), axis=1)
    next_of_expert = jnp.where(next_of_expert < N_EXPERTS, next_of_expert, -1)
    next_e = jnp.sum(jnp.where(block_e[:, None] == expert_ids, next_of_expert, 0), axis=-1)

    n_win = n // (SC_WORKERS * SC_WINDOW)
    dest_w = dest.reshape(SC_WORKERS, n_win, SC_WINDOW, TOP_K).transpose(0, 1, 3, 2)
    pad_j = jnp.arange(tm, dtype=jnp.int32)[None, :]
    spare = nb * tm + expert_ids[:, None] * tm + pad_j
    pad_dest = jnp.where(pad_j < (padded - cnt)[:, None], (pstart + cnt)[:, None] + pad_j, spare)
    pad_dest = pad_dest.reshape(SC_WORKERS, N_EXPERTS * tm // (SC_WORKERS * SC_WINDOW), SC_WINDOW)

    xs = _sc_scatter_rows(h2, dest_w, pad_dest, (nb + N_EXPERTS) * tm)
    ys = _experts(block_e, first, next_e, n_used, xs, w_gu[0], b_gu[0], w_down[0], b_down[0])
    dest_slot_major = dest.reshape(n, TOP_K).T.reshape(-1)
    y4 = _sc_gather_rows(ys, dest_slot_major).reshape(TOP_K, n, ROW_WORDS)
    return _combine_dense(y4, gate_w, x1, mod_x, final_g.reshape(1, d))
```

```python
import functools

import jax
import jax.numpy as jnp
from jax import lax
from jax.experimental import pallas as pl
from jax.experimental.pallas import tpu as pltpu
from jax.experimental.pallas import tpu_sc as plsc

D_MODEL = 1024
GRID_W = 64
HEAD_DIM = 64
N_HEADS = 16
N_KV_HEADS = 4
GROUP = N_HEADS // N_KV_HEADS
ATTN_W = N_HEADS * HEAD_DIM
KV_W = N_KV_HEADS * HEAD_DIM
WINDOW = 128
ATTN_SCALE = HEAD_DIM ** -0.5
ROPE_BASE = 10000.0
ROPE_PAIRS = HEAD_DIM // 4
POOL_WINDOWS = (2, 4, 8, 16)
POOL_GROUP_W = D_MODEL // len(POOL_WINDOWS)
IN_W = ATTN_W + 2 * KV_W + D_MODEL + 2 * D_MODEL
VEXT_W = N_KV_HEADS * 2 * HEAD_DIM
IN_W_EXT = IN_W - KV_W + VEXT_W
N_EXPERTS = 32
TOP_K = 4
D_FF = D_MODEL
SWIGLU_ALPHA = 1.702
SWIGLU_LIMIT = 7.0
NORM_EPS = 1e-5
NEG_INF = -1e30

LANES = 128
SUBLANES = 8
VMEM_LIMIT = 56 * 1024 * 1024

TM_INPROJ = 512
TQ = WINDOW
TM_MERGE = 256
TM_EXPERT = 256
EXPERT_FF_CHUNK = 1024
POOL_HALO = SUBLANES

F32 = jnp.float32
BF16 = jnp.bfloat16


ROW_WORDS = D_MODEL // 2
U32 = jnp.uint32


def _pack_rows(value):
    bits = lax.bitcast_convert_type(value.astype(BF16).astype(F32), U32)
    return bits[:, :ROW_WORDS] | (bits[:, ROW_WORDS:] >> 16)


def _unpack_rows(words):
    hi = lax.bitcast_convert_type(words & jnp.uint32(0xFFFF0000), F32)
    lo = lax.bitcast_convert_type(words << 16, F32)
    return jnp.concatenate([hi, lo], axis=1)


def _params(*sem):
    return pltpu.CompilerParams(dimension_semantics=sem, vmem_limit_bytes=VMEM_LIMIT)


def _const_spec(shape):
    nd = len(shape)
    return pl.BlockSpec(shape, lambda *_: (0,) * nd)


def _adaln_kernel(c_ref, w_ref, b_ref, o_ref):
    c = c_ref[...]
    s = c * jax.nn.sigmoid(c)
    o_ref[...] = jnp.dot(s.astype(BF16), w_ref[...].astype(BF16),
                         preferred_element_type=F32) + b_ref[...]


def _adaln(cond, w_ada, b_ada):
    rows, d = cond.shape
    n = w_ada.shape[1]
    tn = 1024
    return pl.pallas_call(
        _adaln_kernel,
        grid=(n // tn,),
        in_specs=[_const_spec((rows, d)),
                  pl.BlockSpec((d, tn), lambda j: (0, j)),
                  pl.BlockSpec((1, tn), lambda j: (0, j))],
        out_specs=pl.BlockSpec((rows, tn), lambda j: (0, j)),
        out_shape=jax.ShapeDtypeStruct((rows, n), F32),
        compiler_params=_params("arbitrary"),
        name="adaln",
    )(cond, w_ada, b_ada.reshape(1, n))


def _norm_mod(x, g, shift, scale):
    y = x * lax.rsqrt(jnp.mean(x * x, axis=-1, keepdims=True) + NORM_EPS) * g
    return y * (1.0 + scale) + shift


def _rope(t, cos, sin_signed):
    w = t.shape[1]
    half = ROPE_PAIRS
    lane = lax.broadcasted_iota(jnp.int32, t.shape, 1)
    first = (lane & (2 * half - 1)) < half
    rot = jnp.where(first, pltpu.roll(t, w - half, 1), pltpu.roll(t, half, 1))
    reps = w // LANES
    cos_w = jnp.concatenate([cos] * reps, axis=1) if reps > 1 else cos
    sin_w = jnp.concatenate([sin_signed] * reps, axis=1) if reps > 1 else sin_signed
    return t * cos_w + rot * sin_w


def _token_table(row_ref, col_ref):
    gr = row_ref.shape[0]
    shape = (gr, GRID_W, LANES)
    full = (jnp.broadcast_to(row_ref[...][:, None, :], shape)
            + jnp.broadcast_to(col_ref[...][None, :, :], shape))
    return full.reshape(gr * GRID_W, LANES)


def _inproj_kernel(x_ref, mod_ref, g_ref, w_ref, b_ref, cr_ref, sr_ref, cc_ref, sc_ref,
                   q_ref, k_ref, v_ref, u_ref, ga_ref, gp_ref):
    h = _norm_mod(x_ref[0], g_ref[...], mod_ref[0, 0:1, :], mod_ref[0, 1:2, :])
    hb = h.astype(BF16)
    cos = _token_table(cr_ref, cc_ref)
    sin = _token_table(sr_ref, sc_ref)

    def proj(c0, c1):
        return jnp.dot(hb, w_ref[:, c0:c1], preferred_element_type=F32) + b_ref[:, c0:c1]

    cw = 512
    for c0 in range(0, ATTN_W, cw):
        q_ref[0, :, c0:c0 + cw] = (_rope(proj(c0, c0 + cw), cos, sin) * ATTN_SCALE).astype(BF16)
    k_ref[0] = _rope(proj(ATTN_W, ATTN_W + KV_W), cos, sin).T.astype(BF16)
    v_ref[0] = proj(ATTN_W + KV_W, ATTN_W + KV_W + VEXT_W).astype(BF16)
    base = ATTN_W + KV_W + VEXT_W
    for ref in (u_ref, ga_ref, gp_ref):
        for c0 in range(0, D_MODEL, cw):
            ref[0, :, c0:c0 + cw] = proj(base + c0, base + c0 + cw)
        base += D_MODEL


def _inproj(x, mod, norm_g, w_in_bf, b_in, rope):
    b, l, d = x.shape
    tm = TM_INPROJ
    row = lambda bi, i: (bi, i, 0)
    out_shape = (
        jax.ShapeDtypeStruct((b, l, ATTN_W), BF16),
        jax.ShapeDtypeStruct((b, KV_W, l), BF16),
        jax.ShapeDtypeStruct((b, l, VEXT_W), BF16),
        jax.ShapeDtypeStruct((b, l, D_MODEL), F32),
        jax.ShapeDtypeStruct((b, l, D_MODEL), F32),
        jax.ShapeDtypeStruct((b, l, D_MODEL), F32),
    )
    return pl.pallas_call(
        _inproj_kernel,
        grid=(b, l // tm),
        in_specs=[pl.BlockSpec((1, tm, d), row),
                  pl.BlockSpec((1, 6, d), lambda bi, i: (bi, 0, 0)),
                  _const_spec((1, d)),
                  _const_spec((d, IN_W_EXT)),
                  _const_spec((1, IN_W_EXT)),
                  pl.BlockSpec((tm // GRID_W, LANES), lambda bi, i: (i, 0)),
                  pl.BlockSpec((tm // GRID_W, LANES), lambda bi, i: (i, 0)),
                  _const_spec((GRID_W, LANES)),
                  _const_spec((GRID_W, LANES))],
        out_specs=(pl.BlockSpec((1, tm, ATTN_W), row),
                   pl.BlockSpec((1, KV_W, tm), lambda bi, i: (bi, 0, i)),
                   pl.BlockSpec((1, tm, VEXT_W), row),
                   pl.BlockSpec((1, tm, D_MODEL), row),
                   pl.BlockSpec((1, tm, D_MODEL), row),
                   pl.BlockSpec((1, tm, D_MODEL), row)),
        out_shape=out_shape,
        compiler_params=_params("arbitrary", "arbitrary"),
        name="inproj",
    )(x, mod, norm_g, w_in_bf, b_in, *rope)


def _ctx_kv_kernel(x_ref, mod_ref, g_ref, w_ref, b_ref, k_ref, v_ref):
    h = _norm_mod(x_ref[0], g_ref[...], mod_ref[0, 0:1, :], mod_ref[0, 1:2, :])
    kv = jnp.dot(h.astype(BF16), w_ref[...], preferred_element_type=F32) + b_ref[...]
    k_ref[0] = kv[:, :KV_W].T.astype(BF16)
    v_ref[0] = kv[:, KV_W:].astype(BF16)


def _ctx_kv(ctx, mod_c, norm_g, w_kv_bf, b_kv):
    b, lc, d = ctx.shape
    row = lambda bi: (bi, 0, 0)
    return pl.pallas_call(
        _ctx_kv_kernel,
        grid=(b,),
        in_specs=[pl.BlockSpec((1, lc, d), row),
                  _const_spec((1, 6, d)),
                  _const_spec((1, d)),
                  _const_spec((d, KV_W + VEXT_W)),
                  _const_spec((1, KV_W + VEXT_W))],
        out_specs=(pl.BlockSpec((1, KV_W, lc), row), pl.BlockSpec((1, lc, VEXT_W), row)),
        out_shape=(jax.ShapeDtypeStruct((b, KV_W, lc), BF16),
                   jax.ShapeDtypeStruct((b, lc, VEXT_W), BF16)),
        compiler_params=_params("arbitrary"),
        name="ctx_kv",
    )(ctx, mod_c, norm_g, w_kv_bf, b_kv)


def _attn_kernel(sink_ref, q_ref, kp_ref, kc_ref, kn_ref, vp_ref, vc_ref, vn_ref,
                 kx_ref, vx_ref, o_ref):
    n = pl.program_id(1)
    last = pl.num_programs(1) - 1
    rows = GROUP * TQ
    pair_w = 2 * HEAD_DIM
    qi = lax.broadcasted_iota(jnp.int32, (rows, TQ), 0) & (TQ - 1)
    col = lax.broadcasted_iota(jnp.int32, (rows, TQ), 1)
    keep_prev = (col >= qi) & (n > 0)
    keep_next = (col <= qi) & (n < last)
    low_half = lax.broadcasted_iota(jnp.int32, (TQ, pair_w), 1) < HEAD_DIM
    row_id = lax.broadcasted_iota(jnp.int32, (rows, 1), 0)

    def scores(j):
        parts = []
        for g in range(GROUP):
            h = j * GROUP + g
            pair = q_ref[0, :, (h // 2) * pair_w:(h // 2 + 1) * pair_w]
            parts.append(jnp.where(low_half if h % 2 == 0 else ~low_half, pair, jnp.zeros_like(pair)))
        q4 = jnp.concatenate(parts, axis=0)
        ks = slice(j * HEAD_DIM, (j + 1) * HEAD_DIM)
        kt = jnp.concatenate([kp_ref[0, ks, :], kc_ref[0, ks, :], kn_ref[0, ks, :]], axis=1)
        s_loc = jnp.dot(q4, jnp.concatenate([kt, kt], axis=0), preferred_element_type=F32)
        kxt = kx_ref[0, ks, :]
        s_ctx = jnp.dot(q4, jnp.concatenate([kxt, kxt], axis=0), preferred_element_type=F32)
        return s_loc, s_ctx

    def probs(j, s_loc, s_ctx):
        pieces = [jnp.where(keep_prev, s_loc[:, :TQ], NEG_INF), s_loc[:, TQ:2 * TQ],
                  jnp.where(keep_next, s_loc[:, 2 * TQ:], NEG_INF)]
        pieces += [s_ctx[:, c0:c0 + TQ] for c0 in range(0, s_ctx.shape[1], TQ)]
        sink = jnp.zeros((rows, 1), F32)
        for g in range(GROUP):
            sink = jnp.where(row_id // TQ == g, sink_ref[j * GROUP + g], sink)
        mx = pieces[0]
        for pc in pieces[1:]:
            mx = jnp.maximum(mx, pc)
        m = jnp.maximum(jnp.max(mx, axis=-1, keepdims=True), sink)
        p = jnp.concatenate([jnp.exp(pc - m).astype(BF16) for pc in pieces], axis=1)
        return p, jnp.exp(sink - m)

    def output(j, p, sink_p):
        vs = slice(j * pair_w, (j + 1) * pair_w)
        v_all = jnp.concatenate([vp_ref[0, :, vs], vc_ref[0, :, vs], vn_ref[0, :, vs],
                                 vx_ref[0, :, vs]], axis=0)
        r = jnp.dot(p, v_all, preferred_element_type=F32)
        den = pltpu.roll(r, HEAD_DIM, 1) + sink_p
        o = r / den
        for g in range(0, GROUP, 2):
            even = o[g * TQ:(g + 1) * TQ]
            odd = pltpu.roll(o[(g + 1) * TQ:(g + 2) * TQ], HEAD_DIM, 1)
            c0 = (j * GROUP + g) * HEAD_DIM
            o_ref[0, :, c0:c0 + pair_w] = jnp.where(low_half, even, odd).astype(BF16)

    s_queue = [scores(0), scores(1)]
    p_queue = [probs(0, *s_queue.pop(0))]
    for j in range(N_KV_HEADS):
        if j + 2 < N_KV_HEADS:
            s_queue.append(scores(j + 2))
        ready = p_queue.pop(0)
        if j + 1 < N_KV_HEADS:
            p_queue.append(probs(j + 1, *s_queue.pop(0)))
        output(j, *ready)


def _attention(q, kt, v, kxt, vx, sink):
    b, l, _ = q.shape
    lc = kxt.shape[2]
    nblk = l // TQ
    cur = lambda bi, n: (bi, n, 0)
    prev = lambda bi, n: (bi, jnp.maximum(n - 1, 0), 0)
    nxt = lambda bi, n: (bi, jnp.minimum(n + 1, nblk - 1), 0)
    kcur = lambda bi, n: (bi, 0, n)
    kprev = lambda bi, n: (bi, 0, jnp.maximum(n - 1, 0))
    knxt = lambda bi, n: (bi, 0, jnp.minimum(n + 1, nblk - 1))
    kb = (1, KV_W, TQ)
    vb = (1, TQ, VEXT_W)
    return pl.pallas_call(
        _attn_kernel,
        grid=(b, nblk),
        in_specs=[pl.BlockSpec(memory_space=pltpu.SMEM),
                  pl.BlockSpec((1, TQ, ATTN_W), cur),
                  pl.BlockSpec(kb, kprev), pl.BlockSpec(kb, kcur), pl.BlockSpec(kb, knxt),
                  pl.BlockSpec(vb, prev), pl.BlockSpec(vb, cur), pl.BlockSpec(vb, nxt),
                  pl.BlockSpec((1, KV_W, lc), lambda bi, n: (bi, 0, 0)),
                  pl.BlockSpec((1, lc, VEXT_W), lambda bi, n: (bi, 0, 0))],
        out_specs=pl.BlockSpec((1, TQ, ATTN_W), cur),
        out_shape=jax.ShapeDtypeStruct((b, l, ATTN_W), BF16),
        compiler_params=_params("arbitrary", "arbitrary"),
        name="attention",
    )(sink, q, kt, kt, kt, v, v, v, kxt, vx)


def _shift_rows(a, d):
    n = a.shape[0]
    return pltpu.roll(a, (-d) % n, 0)


def _merge_kernel(ao_ref, u_ref, up_ref, un_ref, ga_ref, gp_ref, x_ref, mod_ref,
                  wpool_ref, pscale_ref, wab_ref, wpb_ref, wout_ref, g2_ref, wr_ref, br_ref,
                  x1_ref, h2_ref, idx_ref, gate_ref, rank_ref, cnt_ref, carry_ref, *, seq_len):
    bi = pl.program_id(0)
    i = pl.program_id(1)
    last = pl.num_programs(1) - 1
    tm = TM_MERGE

    @pl.when((bi == 0) & (i == 0))
    def _():
        carry_ref[...] = jnp.zeros_like(carry_ref)

    u = u_ref[0]
    prev = jnp.where(i > 0, up_ref[0], 0.0)
    nxt = jnp.where(i < last, un_ref[0], 0.0)
    ext = jnp.concatenate([prev, u, nxt], axis=0)
    t = i * tm + lax.broadcasted_iota(jnp.int32, (tm, 1), 0)
    mixed = []
    for g, w in enumerate(POOL_WINDOWS):
        cs = slice(g * POOL_GROUP_W, (g + 1) * POOL_GROUP_W)
        e = ext[:, cs]
        acc = _shift_rows(e, -1) + e
        step = 1
        while 2 * step < w:
            acc = _shift_rows(acc, -step) + _shift_rows(acc, step)
            step *= 2
        win = acc[POOL_HALO:POOL_HALO + tm]
        half = w // 2
        cnt = (jnp.minimum(t + half, seq_len) - jnp.maximum(t - half, 0)).astype(F32)
        diff = (win / cnt - u[:, cs]).astype(BF16)
        mixed.append(jnp.dot(diff, wpool_ref[g], preferred_element_type=F32))
    pool_o = jnp.concatenate(mixed, axis=1) * pscale_ref[...]

    a = jnp.dot(ao_ref[0], wab_ref[...], preferred_element_type=F32)
    p = jnp.dot(pool_o.astype(BF16), wpb_ref[...], preferred_element_type=F32)
    merged = jax.nn.sigmoid(ga_ref[0]) * a + jax.nn.sigmoid(gp_ref[0]) * p
    o = jnp.dot(merged.astype(BF16), wout_ref[...], preferred_element_type=F32)
    x1 = x_ref[0] + mod_ref[0, 2:3, :] * o
    x1_ref[0] = x1

    h2 = _norm_mod(x1, g2_ref[...], mod_ref[0, 3:4, :], mod_ref[0, 4:5, :])
    h2_ref[...] = _pack_rows(h2)
    logits = jnp.dot(h2.astype(BF16), wr_ref[...], preferred_element_type=F32) + br_ref[...]
    lane = lax.broadcasted_iota(jnp.int32, (tm, N_EXPERTS), 1).astype(F32)
    work = logits
    vals, idxs, sels = [], [], []
    for _ in range(TOP_K):
        mx = jnp.max(work, axis=-1, keepdims=True)
        ix = jnp.min(jnp.where(work == mx, lane, float(N_EXPERTS)), axis=-1, keepdims=True)
        sel = lane == ix
        work = jnp.where(sel, -jnp.inf, work)
        vals.append(mx)
        idxs.append(ix)
        sels.append(sel)
    ex = [jnp.exp(v - vals[0]) for v in vals]
    tot = ex[0] + ex[1] + ex[2] + ex[3]
    gate_ref[...] = jnp.concatenate([e_ / tot for e_ in ex], axis=1)
    idx_ref[...] = jnp.concatenate(idxs, axis=1).astype(jnp.int32)

    member = (sels[0] | sels[1] | sels[2] | sels[3]).astype(F32)
    r_id = lax.broadcasted_iota(jnp.int32, (tm, tm), 0)
    c_id = lax.broadcasted_iota(jnp.int32, (tm, tm), 1)
    tri = (r_id > c_id).astype(BF16)
    before = jnp.dot(tri, member.astype(BF16), preferred_element_type=F32) + carry_ref[...]
    ranks = [jnp.sum(jnp.where(s, before, 0.0), axis=-1, keepdims=True) for s in sels]
    rank_ref[...] = jnp.concatenate(ranks, axis=1).astype(jnp.int32)
    total = carry_ref[...] + jnp.sum(member, axis=0, keepdims=True)
    carry_ref[...] = total
    cnt_ref[...] = total


def _merge_route(attn_o, u, ga, gp, x, mod, w_pool_bf, pool_scale, wab, wpb, wout,
                 norm2_g, w_router, b_router):
    b, l, d = x.shape
    tm = TM_MERGE
    nt = l // tm
    n = b * l
    hb = tm // POOL_HALO
    row = lambda bi, i: (bi, i, 0)
    flat = lambda bi, i: (bi * nt + i, 0)
    halo_prev = lambda bi, i: (bi, jnp.maximum(i * hb - 1, 0), 0)
    halo_next = lambda bi, i: (bi, jnp.minimum((i + 1) * hb, l // POOL_HALO - 1), 0)
    out_shape = (
        jax.ShapeDtypeStruct((b, l, d), F32),
        jax.ShapeDtypeStruct((n, ROW_WORDS), U32),
        jax.ShapeDtypeStruct((n, TOP_K), jnp.int32),
        jax.ShapeDtypeStruct((n, TOP_K), F32),
        jax.ShapeDtypeStruct((n, TOP_K), jnp.int32),
        jax.ShapeDtypeStruct((1, N_EXPERTS), F32),
    )
    return pl.pallas_call(
        functools.partial(_merge_kernel, seq_len=l),
        grid=(b, nt),
        in_specs=[pl.BlockSpec((1, tm, d), row),
                  pl.BlockSpec((1, tm, d), row),
                  pl.BlockSpec((1, POOL_HALO, d), halo_prev),
                  pl.BlockSpec((1, POOL_HALO, d), halo_next),
                  pl.BlockSpec((1, tm, d), row),
                  pl.BlockSpec((1, tm, d), row),
                  pl.BlockSpec((1, tm, d), row),
                  pl.BlockSpec((1, 6, d), lambda bi, i: (bi, 0, 0)),
                  _const_spec(w_pool_bf.shape),
                  _const_spec((1, d)),
                  _const_spec((d, d)), _const_spec((d, d)), _const_spec((d, d)),
                  _const_spec((1, d)),
                  _const_spec((d, N_EXPERTS)),
                  _const_spec((1, N_EXPERTS))],
        out_specs=(pl.BlockSpec((1, tm, d), row),
                   pl.BlockSpec((tm, ROW_WORDS), flat),
                   pl.BlockSpec((tm, TOP_K), flat),
                   pl.BlockSpec((tm, TOP_K), flat),
                   pl.BlockSpec((tm, TOP_K), flat),
                   _const_spec((1, N_EXPERTS))),
        out_shape=out_shape,
        scratch_shapes=[pltpu.VMEM((1, N_EXPERTS), F32)],
        compiler_params=_params("arbitrary", "arbitrary"),
        name="merge_route",
    )(attn_o, u, u, u, ga, gp, x, mod, w_pool_bf, pool_scale, wab, wpb, wout,
      norm2_g, w_router, b_router)


SC_CORES = 2
SC_SUBCORES = 16
SC_WORKERS = SC_CORES * SC_SUBCORES
SC_WINDOW = 128


def _sc_worker_id():
    return lax.axis_index("s") * SC_CORES + lax.axis_index("c")


def _sc_scatter_rows(rows, dest, pad_dest, out_rows):
    n, w = rows.shape
    n_win = dest.shape[1]
    n_pad_win = pad_dest.shape[1]
    assert dest.shape == (SC_WORKERS, n_win, TOP_K, SC_WINDOW) and SC_WORKERS * n_win * SC_WINDOW == n
    mesh = plsc.VectorSubcoreMesh(core_axis_name="c", subcore_axis_name="s")
    zeros = jnp.zeros((SC_WINDOW, w), rows.dtype)

    @functools.partial(
        pl.kernel, mesh=mesh,
        out_type=jax.ShapeDtypeStruct((out_rows, w), rows.dtype),
        scratch_types=[pltpu.VMEM((TOP_K, SC_WINDOW), jnp.int32),
                       pltpu.VMEM((n_pad_win, SC_WINDOW), jnp.int32),
                       pltpu.VMEM((SC_WINDOW, w), rows.dtype),
                       pltpu.SemaphoreType.DMA],
        name="sc_scatter_rows",
    )
    def scatter(rows_hbm, dest_hbm, pad_hbm, zeros_hbm, out_hbm, idx_v, pad_v, rows_v, sem):
        wid = _sc_worker_id()

        @pl.loop(0, n_win)
        def _(j):
            base = (wid * n_win + j) * SC_WINDOW
            pltpu.sync_copy(rows_hbm.at[pl.ds(base, SC_WINDOW)], rows_v)
            pltpu.sync_copy(dest_hbm.at[wid, j], idx_v)
            copies = [pltpu.async_copy(rows_v, out_hbm.at[idx_v.at[k]], sem) for k in range(TOP_K)]
            for cp in copies:
                cp.wait()

        pltpu.sync_copy(zeros_hbm, rows_v)
        pltpu.sync_copy(pad_hbm.at[wid], pad_v)
        copies = [pltpu.async_copy(rows_v, out_hbm.at[pad_v.at[j]], sem) for j in range(n_pad_win)]
        for cp in copies:
            cp.wait()

    return scatter(rows, dest, pad_dest, zeros)


def _expert_kernel(be_ref, first_ref, next_ref, nu_ref, x_ref, wgu_hbm, bgu_ref, wd_hbm, bd_ref,
                   y_ref, wgu_f, wd_f, wgu_s, wd_s, sem):
    i = pl.program_id(0)

    def weight_copies(e):
        return (pltpu.make_async_copy(wgu_hbm.at[e], wgu_f, sem.at[0]),
                pltpu.make_async_copy(wd_hbm.at[e], wd_f, sem.at[1]))

    @pl.when(i == 0)
    def _():
        for cp in weight_copies(be_ref[0]):
            cp.start()

    @pl.when(i < nu_ref[0])
    def _():
        @pl.when(first_ref[i] == 1)
        def _():
            for cp in weight_copies(be_ref[i]):
                cp.wait()
            ck = 256
            for r0 in range(0, D_MODEL, ck):
                wgu_s[r0:r0 + ck, :] = wgu_f[r0:r0 + ck, :].astype(BF16)
                wd_s[r0:r0 + ck, :] = wd_f[r0:r0 + ck, :].astype(BF16)

            @pl.when(next_ref[i] >= 0)
            def _():
                for cp in weight_copies(next_ref[i]):
                    cp.start()

        x = _unpack_rows(x_ref[...]).astype(BF16)
        fc = EXPERT_FF_CHUNK

        def up(c0):
            gt = jnp.dot(x, wgu_s[:, c0:c0 + fc], preferred_element_type=F32)
            ln = jnp.dot(x, wgu_s[:, D_FF + c0:D_FF + c0 + fc], preferred_element_type=F32)
            return gt, ln

        def down(c0, gt, ln):
            gt = jnp.minimum(gt + bgu_ref[0, :, c0:c0 + fc], SWIGLU_LIMIT)
            ln = jnp.clip(ln + bgu_ref[0, :, D_FF + c0:D_FF + c0 + fc], -SWIGLU_LIMIT, SWIGLU_LIMIT)
            act = gt * jax.nn.sigmoid(SWIGLU_ALPHA * gt) * (ln + 1.0)
            return jnp.dot(act.astype(BF16), wd_s[c0:c0 + fc, :], preferred_element_type=F32)

        chunks = list(range(0, D_FF, fc))
        pending = [up(chunks[0])]
        acc = jnp.zeros((TM_EXPERT, D_MODEL), F32)
        for ci, c0 in enumerate(chunks):
            if ci + 1 < len(chunks):
                pending.append(up(chunks[ci + 1]))
            acc = acc + down(c0, *pending.pop(0))
        y_ref[...] = _pack_rows(acc + bd_ref[0])

    @pl.when(i >= nu_ref[0])
    def _():
        y_ref[...] = jnp.zeros_like(y_ref)


def _experts(block_e, first, next_e, n_used, xs, w_gu, b_gu, w_down, b_down):
    d = D_MODEL
    tm = TM_EXPERT
    nb = block_e.shape[0]

    def blk(i, be, fi, nx, nu):
        return jnp.minimum(i, nu[0] - 1)

    xmap = lambda i, be, fi, nx, nu: (blk(i, be, fi, nx, nu), 0)
    bmap = lambda i, be, fi, nx, nu: (be[blk(i, be, fi, nx, nu)], 0, 0)
    grid_spec = pltpu.PrefetchScalarGridSpec(
        num_scalar_prefetch=4,
        grid=(nb,),
        in_specs=[pl.BlockSpec((tm, ROW_WORDS), xmap),
                  pl.BlockSpec(memory_space=pl.ANY),
                  pl.BlockSpec((1, 1, 2 * D_FF), bmap),
                  pl.BlockSpec(memory_space=pl.ANY),
                  pl.BlockSpec((1, 1, d), bmap)],
        out_specs=pl.BlockSpec((tm, ROW_WORDS), lambda i, be, fi, nx, nu: (i, 0)),
        scratch_shapes=[pltpu.VMEM((d, 2 * D_FF), F32), pltpu.VMEM((D_FF, d), F32),
                        pltpu.VMEM((d, 2 * D_FF), BF16), pltpu.VMEM((D_FF, d), BF16),
                        pltpu.SemaphoreType.DMA((2,))],
    )
    return pl.pallas_call(
        _expert_kernel,
        grid_spec=grid_spec,
        out_shape=jax.ShapeDtypeStruct((nb * tm, ROW_WORDS), U32),
        compiler_params=_params("arbitrary"),
        name="moe_experts",
    )(block_e, first, next_e, n_used, xs, w_gu, b_gu.reshape(N_EXPERTS, 1, 2 * D_FF),
      w_down, b_down.reshape(N_EXPERTS, 1, d))


def _sc_gather_rows(table, idx):
    m = idx.shape[0]
    w = table.shape[1]
    per_worker = m // SC_WORKERS
    n_win = per_worker // SC_WINDOW
    assert per_worker * SC_WORKERS == m and n_win * SC_WINDOW == per_worker
    mesh = plsc.VectorSubcoreMesh(core_axis_name="c", subcore_axis_name="s")

    @functools.partial(
        pl.kernel, mesh=mesh,
        out_type=jax.ShapeDtypeStruct((m, w), table.dtype),
        scratch_types=[pltpu.VMEM((SC_WINDOW,), jnp.int32),
                       pltpu.VMEM((SC_WINDOW, w), table.dtype),
                       pltpu.SemaphoreType.DMA],
        name="sc_gather_rows",
    )
    def gather(table_hbm, idx_hbm, out_hbm, idx_v, rows_v, sem):
        wid = _sc_worker_id()

        @pl.loop(0, n_win)
        def _(j):
            base = wid * per_worker + j * SC_WINDOW
            pltpu.sync_copy(idx_hbm.at[pl.ds(base, SC_WINDOW)], idx_v)
            pltpu.async_copy(table_hbm.at[idx_v], rows_v, sem).wait()
            pltpu.sync_copy(rows_v, out_hbm.at[pl.ds(base, SC_WINDOW)])

    return gather(table, idx)


def _combine_dense_kernel(y4_ref, gate_ref, x1_ref, mod_ref, fg_ref, o_ref):
    gate = gate_ref[...]
    y = gate[:, 0:1] * _unpack_rows(y4_ref[0])
    for k in range(1, TOP_K):
        y = y + gate[:, k:k + 1] * _unpack_rows(y4_ref[k])
    x2 = x1_ref[...] + mod_ref[0, 5:6, :] * y
    o_ref[...] = x2 * lax.rsqrt(jnp.mean(x2 * x2, axis=-1, keepdims=True) + NORM_EPS) * fg_ref[...]


def _combine_dense(y4, gate_w, x1, mod, final_g):
    b, l, d = x1.shape
    ts = 512
    nt = l // ts
    return pl.pallas_call(
        _combine_dense_kernel,
        grid=(b * nt,),
        in_specs=[pl.BlockSpec((TOP_K, ts, ROW_WORDS), lambda s: (0, s, 0)),
                  pl.BlockSpec((ts, TOP_K), lambda s: (s, 0)),
                  pl.BlockSpec((ts, d), lambda s: (s, 0)),
                  pl.BlockSpec((1, 6, d), lambda s: (s // nt, 0, 0)),
                  _const_spec((1, d))],
        out_specs=pl.BlockSpec((ts, d), lambda s: (s, 0)),
        out_shape=jax.ShapeDtypeStruct((b * l, d), F32),
        compiler_params=_params("arbitrary"),
        name="moe_combine",
    )(y4, gate_w, x1.reshape(b * l, d), mod, final_g).reshape(b, l, d)


def _rope_tables(seq_len):
    inv_freq = ROPE_BASE ** (-jnp.arange(ROPE_PAIRS, dtype=F32) / ROPE_PAIRS)
    reps = LANES // HEAD_DIM
    out = []
    for n_pos, on_row in ((seq_len // GRID_W, True), (GRID_W, False)):
        ang = jnp.arange(n_pos, dtype=F32)[:, None] * inv_freq
        zero = jnp.zeros_like(ang)
        for fn, sign in ((jnp.cos, 1.0), (jnp.sin, -1.0)):
            v = fn(ang)
            head = ([sign * v, v, zero, zero] if on_row else [zero, zero, sign * v, v])
            out.append(jnp.concatenate(head * reps, axis=1))
    return out


def kernel(x, c, ctx, c_ctx, w_ada, b_ada, norm1_g, norm2_g, w_in, b_in, attn_sink, w_pool,
           pool_scale, w_attn_br, w_pool_br, w_out, w_router, b_router, w_gu, b_gu, w_down,
           b_down, final_g):
    b, l, d = x.shape
    n = b * l
    assert w_ada.shape[0] == 1, "single-layer block"

    cond = jnp.zeros((SUBLANES, d), F32).at[:b].set(c).at[b].set(c_ctx)
    mod = _adaln(cond, w_ada[0], b_ada[0])[:b + 1].reshape(b + 1, 6, d)
    mod_x, mod_c = mod[:b], mod[b:b + 1]

    rope = _rope_tables(l)
    v0 = ATTN_W + KV_W
    w_v = w_in[0][:, v0:v0 + KV_W].reshape(d, N_KV_HEADS, HEAD_DIM)
    w_v = jnp.concatenate([w_v, jnp.zeros_like(w_v)], axis=-1).reshape(d, VEXT_W)
    b_v = b_in[0][v0:v0 + KV_W].reshape(N_KV_HEADS, HEAD_DIM)
    b_v = jnp.concatenate([b_v, jnp.ones_like(b_v)], axis=-1).reshape(VEXT_W)
    w_in_bf = jnp.concatenate([w_in[0][:, :v0], w_v, w_in[0][:, v0 + KV_W:]], axis=1).astype(BF16)
    b_in2 = jnp.concatenate([b_in[0][:v0], b_v, b_in[0][v0 + KV_W:]]).reshape(1, IN_W_EXT)
    g1 = norm1_g[0].reshape(1, d)
    q, kt, v, u, ga, gp = _inproj(x, mod_x, g1, w_in_bf, b_in2, rope)
    kv_sl = slice(ATTN_W, ATTN_W + KV_W + VEXT_W)
    kxt, vx = _ctx_kv(ctx, mod_c, g1, w_in_bf[:, kv_sl], b_in2[:, kv_sl])

    attn_o = _attention(q, kt, v, kxt, vx, attn_sink[0])

    x1, h2, top_idx, gate_w, rank, counts = _merge_route(
        attn_o, u, ga, gp, x, mod_x, w_pool[0].astype(BF16), pool_scale[0].reshape(1, d),
        w_attn_br[0].astype(BF16), w_pool_br[0].astype(BF16), w_out[0].astype(BF16),
        norm2_g[0].reshape(1, d), w_router[0].astype(BF16), b_router[0].reshape(1, N_EXPERTS))

    tm = TM_EXPERT
    nb = n * TOP_K // tm + N_EXPERTS
    cnt = counts[0].astype(jnp.int32)
    padded = (cnt + tm - 1) // tm * tm
    pend = jnp.cumsum(padded)
    pstart = pend - padded
    expert_ids = jnp.arange(N_EXPERTS, dtype=jnp.int32)
    dest = (jnp.sum(jnp.where(top_idx[..., None] == expert_ids, pstart, 0), axis=-1)
            + rank).reshape(-1)
    n_used = (pend[-1] // tm).reshape(1)
    block_start = jnp.arange(nb, dtype=jnp.int32) * tm
    block_e = jnp.minimum(jnp.sum((pend[None, :] <= block_start[:, None]).astype(jnp.int32), axis=1),
                          N_EXPERTS - 1)
    first = jnp.concatenate([jnp.ones((1,), jnp.int32),
                             (block_e[1:] != block_e[:-1]).astype(jnp.int32)])
    later_used = (expert_ids[None, :] > expert_ids[:, None]) & (cnt[None, :] > 0)
    next_of_expert = jnp.min(jnp.where(later_used, expert_ids[None, :], N_EXPERTS), axis=1)
    next_of_expert = jnp.where(next_of_expert < N_EXPERTS, next_of_expert, -1)
    next_e = jnp.sum(jnp.where(block_e[:, None] == expert_ids, next_of_expert, 0), axis=-1)

    n_win = n // (SC_WORKERS * SC_WINDOW)
    dest_w = dest.reshape(SC_WORKERS, n_win, SC_WINDOW, TOP_K).transpose(0, 1, 3, 2)
    pad_j = jnp.arange(tm, dtype=jnp.int32)[None, :]
    spare = nb * tm + expert_ids[:, None] * tm + pad_j
    pad_dest = jnp.where(pad_j < (padded - cnt)[:, None], (pstart + cnt)[:, None] + pad_j, spare)
    pad_dest = pad_dest.reshape(SC_WORKERS, N_EXPERTS * tm // (SC_WORKERS * SC_WINDOW), SC_WINDOW)

    xs = _sc_scatter_rows(h2, dest_w, pad_dest, (nb + N_EXPERTS) * tm)
    ys = _experts(block_e, first, next_e, n_used, xs, w_gu[0], b_gu[0], w_down[0], b_down[0])
    dest_slot_major = dest.reshape(n, TOP_K).T.reshape(-1)
    y4 = _sc_gather_rows(ys, dest_slot_major).reshape(TOP_K, n, ROW_WORDS)
    return _combine_dense(y4, gate_w, x1, mod_x, final_g.reshape(1, d))
```

```python
import functools

import jax
import jax.numpy as jnp
from jax import lax
from jax.experimental import pallas as pl
from jax.experimental.pallas import tpu as pltpu
from jax.experimental.pallas import tpu_sc as plsc

D_MODEL = 1024
GRID_W = 64
HEAD_DIM = 64
N_HEADS = 16
N_KV_HEADS = 4
GROUP = N_HEADS // N_KV_HEADS
ATTN_W = N_HEADS * HEAD_DIM
KV_W = N_KV_HEADS * HEAD_DIM
WINDOW = 128
ATTN_SCALE = HEAD_DIM ** -0.5
ROPE_BASE = 10000.0
ROPE_PAIRS = HEAD_DIM // 4
POOL_WINDOWS = (2, 4, 8, 16)
POOL_GROUP_W = D_MODEL // len(POOL_WINDOWS)
IN_W = ATTN_W + 2 * KV_W + D_MODEL + 2 * D_MODEL
VEXT_W = N_KV_HEADS * 2 * HEAD_DIM
IN_W_EXT = IN_W - KV_W + VEXT_W
N_EXPERTS = 32
TOP_K = 4
D_FF = D_MODEL
SWIGLU_ALPHA = 1.702
SWIGLU_LIMIT = 7.0
NORM_EPS = 1e-5
NEG_INF = -1e30

LANES = 128
SUBLANES = 8
VMEM_LIMIT = 56 * 1024 * 1024

TM_INPROJ = 512
TQ = WINDOW
TM_MERGE = 512
MERGE_SUB = 256
TM_EXPERT = 256
EXPERT_FF_CHUNK = 1024
POOL_HALO = SUBLANES

F32 = jnp.float32
BF16 = jnp.bfloat16


ROW_WORDS = D_MODEL // 2
U32 = jnp.uint32


def _pack_rows(value):
    bits = lax.bitcast_convert_type(value.astype(BF16).astype(F32), U32)
    return bits[:, :ROW_WORDS] | (bits[:, ROW_WORDS:] >> 16)


def _unpack_rows(words):
    hi = lax.bitcast_convert_type(words & jnp.uint32(0xFFFF0000), F32)
    lo = lax.bitcast_convert_type(words << 16, F32)
    return jnp.concatenate([hi, lo], axis=1)


def _params(*sem):
    return pltpu.CompilerParams(dimension_semantics=sem, vmem_limit_bytes=VMEM_LIMIT)


def _const_spec(shape):
    nd = len(shape)
    return pl.BlockSpec(shape, lambda *_: (0,) * nd)


def _adaln_kernel(c_ref, w_ref, b_ref, o_ref):
    c = c_ref[...]
    s = c * jax.nn.sigmoid(c)
    o_ref[...] = jnp.dot(s.astype(BF16), w_ref[...].astype(BF16),
                         preferred_element_type=F32) + b_ref[...]


def _adaln(cond, w_ada, b_ada):
    rows, d = cond.shape
    n = w_ada.shape[1]
    tn = 1024
    return pl.pallas_call(
        _adaln_kernel,
        grid=(n // tn,),
        in_specs=[_const_spec((rows, d)),
                  pl.BlockSpec((d, tn), lambda j: (0, j)),
                  pl.BlockSpec((1, tn), lambda j: (0, j))],
        out_specs=pl.BlockSpec((rows, tn), lambda j: (0, j)),
        out_shape=jax.ShapeDtypeStruct((rows, n), F32),
        compiler_params=_params("arbitrary"),
        name="adaln",
    )(cond, w_ada, b_ada.reshape(1, n))


def _norm_mod(x, g, shift, scale):
    y = x * lax.rsqrt(jnp.mean(x * x, axis=-1, keepdims=True) + NORM_EPS) * g
    return y * (1.0 + scale) + shift


def _rope(t, cos, sin_signed):
    w = t.shape[1]
    half = ROPE_PAIRS
    lane = lax.broadcasted_iota(jnp.int32, t.shape, 1)
    first = (lane & (2 * half - 1)) < half
    rot = jnp.where(first, pltpu.roll(t, w - half, 1), pltpu.roll(t, half, 1))
    reps = w // LANES
    cos_w = jnp.concatenate([cos] * reps, axis=1) if reps > 1 else cos
    sin_w = jnp.concatenate([sin_signed] * reps, axis=1) if reps > 1 else sin_signed
    return t * cos_w + rot * sin_w


def _token_table(row_ref, col_ref):
    gr = row_ref.shape[0]
    shape = (gr, GRID_W, LANES)
    full = (jnp.broadcast_to(row_ref[...][:, None, :], shape)
            + jnp.broadcast_to(col_ref[...][None, :, :], shape))
    return full.reshape(gr * GRID_W, LANES)


def _inproj_kernel(x_ref, mod_ref, g_ref, w_ref, b_ref, cr_ref, sr_ref, cc_ref, sc_ref,
                   q_ref, k_ref, v_ref, u_ref, ga_ref, gp_ref):
    h = _norm_mod(x_ref[0], g_ref[...], mod_ref[0, 0:1, :], mod_ref[0, 1:2, :])
    hb = h.astype(BF16)
    cos = _token_table(cr_ref, cc_ref)
    sin = _token_table(sr_ref, sc_ref)

    def proj(c0, c1):
        return jnp.dot(hb, w_ref[:, c0:c1], preferred_element_type=F32) + b_ref[:, c0:c1]

    cw = 512
    for c0 in range(0, ATTN_W, cw):
        q_ref[0, :, c0:c0 + cw] = (_rope(proj(c0, c0 + cw), cos, sin) * ATTN_SCALE).astype(BF16)
    k_ref[0] = _rope(proj(ATTN_W, ATTN_W + KV_W), cos, sin).T.astype(BF16)
    v_ref[0] = proj(ATTN_W + KV_W, ATTN_W + KV_W + VEXT_W).astype(BF16)
    base = ATTN_W + KV_W + VEXT_W
    for c0 in range(0, D_MODEL, cw):
        u_ref[0, :, c0:c0 + cw] = proj(base + c0, base + c0 + cw)
    for ref in (ga_ref, gp_ref):
        base += D_MODEL
        for c0 in range(0, D_MODEL, cw):
            ref[0, :, c0:c0 + cw] = jax.nn.sigmoid(proj(base + c0, base + c0 + cw)).astype(BF16)


def _inproj(x, mod, norm_g, w_in_bf, b_in, rope):
    b, l, d = x.shape
    tm = TM_INPROJ
    row = lambda bi, i: (bi, i, 0)
    out_shape = (
        jax.ShapeDtypeStruct((b, l, ATTN_W), BF16),
        jax.ShapeDtypeStruct((b, KV_W, l), BF16),
        jax.ShapeDtypeStruct((b, l, VEXT_W), BF16),
        jax.ShapeDtypeStruct((b, l, D_MODEL), F32),
        jax.ShapeDtypeStruct((b, l, D_MODEL), BF16),
        jax.ShapeDtypeStruct((b, l, D_MODEL), BF16),
    )
    return pl.pallas_call(
        _inproj_kernel,
        grid=(b, l // tm),
        in_specs=[pl.BlockSpec((1, tm, d), row),
                  pl.BlockSpec((1, 6, d), lambda bi, i: (bi, 0, 0)),
                  _const_spec((1, d)),
                  _const_spec((d, IN_W_EXT)),
                  _const_spec((1, IN_W_EXT)),
                  pl.BlockSpec((tm // GRID_W, LANES), lambda bi, i: (i, 0)),
                  pl.BlockSpec((tm // GRID_W, LANES), lambda bi, i: (i, 0)),
                  _const_spec((GRID_W, LANES)),
                  _const_spec((GRID_W, LANES))],
        out_specs=(pl.BlockSpec((1, tm, ATTN_W), row),
                   pl.BlockSpec((1, KV_W, tm), lambda bi, i: (bi, 0, i)),
                   pl.BlockSpec((1, tm, VEXT_W), row),
                   pl.BlockSpec((1, tm, D_MODEL), row),
                   pl.BlockSpec((1, tm, D_MODEL), row),
                   pl.BlockSpec((1, tm, D_MODEL), row)),
        out_shape=out_shape,
        compiler_params=_params("arbitrary", "arbitrary"),
        name="inproj",
    )(x, mod, norm_g, w_in_bf, b_in, *rope)


def _ctx_kv_kernel(x_ref, mod_ref, g_ref, w_ref, b_ref, k_ref, v_ref):
    h = _norm_mod(x_ref[0], g_ref[...], mod_ref[0, 0:1, :], mod_ref[0, 1:2, :])
    kv = jnp.dot(h.astype(BF16), w_ref[...], preferred_element_type=F32) + b_ref[...]
    k_ref[0] = kv[:, :KV_W].T.astype(BF16)
    v_ref[0] = kv[:, KV_W:].astype(BF16)


def _ctx_kv(ctx, mod_c, norm_g, w_kv_bf, b_kv):
    b, lc, d = ctx.shape
    row = lambda bi: (bi, 0, 0)
    return pl.pallas_call(
        _ctx_kv_kernel,
        grid=(b,),
        in_specs=[pl.BlockSpec((1, lc, d), row),
                  _const_spec((1, 6, d)),
                  _const_spec((1, d)),
                  _const_spec((d, KV_W + VEXT_W)),
                  _const_spec((1, KV_W + VEXT_W))],
        out_specs=(pl.BlockSpec((1, KV_W, lc), row), pl.BlockSpec((1, lc, VEXT_W), row)),
        out_shape=(jax.ShapeDtypeStruct((b, KV_W, lc), BF16),
                   jax.ShapeDtypeStruct((b, lc, VEXT_W), BF16)),
        compiler_params=_params("arbitrary"),
        name="ctx_kv",
    )(ctx, mod_c, norm_g, w_kv_bf, b_kv)


def _attn_kernel(sink_ref, q_ref, kp_ref, kc_ref, kn_ref, vp_ref, vc_ref, vn_ref,
                 kx_ref, vx_ref, o_ref):
    n = pl.program_id(1)
    last = pl.num_programs(1) - 1
    rows = GROUP * TQ
    pair_w = 2 * HEAD_DIM
    qi = lax.broadcasted_iota(jnp.int32, (rows, TQ), 0) & (TQ - 1)
    col = lax.broadcasted_iota(jnp.int32, (rows, TQ), 1)
    keep_prev = (col >= qi) & (n > 0)
    keep_next = (col <= qi) & (n < last)
    low_half = lax.broadcasted_iota(jnp.int32, (TQ, pair_w), 1) < HEAD_DIM
    row_id = lax.broadcasted_iota(jnp.int32, (rows, 1), 0)

    def scores(j):
        parts = []
        for g in range(GROUP):
            h = j * GROUP + g
            pair = q_ref[0, :, (h // 2) * pair_w:(h // 2 + 1) * pair_w]
            parts.append(jnp.where(low_half if h % 2 == 0 else ~low_half, pair, jnp.zeros_like(pair)))
        q4 = jnp.concatenate(parts, axis=0)
        ks = slice(j * HEAD_DIM, (j + 1) * HEAD_DIM)
        kt = jnp.concatenate([kp_ref[0, ks, :], kc_ref[0, ks, :], kn_ref[0, ks, :]], axis=1)
        s_loc = jnp.dot(q4, jnp.concatenate([kt, kt], axis=0), preferred_element_type=F32)
        kxt = kx_ref[0, ks, :]
        s_ctx = jnp.dot(q4, jnp.concatenate([kxt, kxt], axis=0), preferred_element_type=F32)
        return s_loc, s_ctx

    def probs(j, s_loc, s_ctx):
        pieces = [jnp.where(keep_prev, s_loc[:, :TQ], NEG_INF), s_loc[:, TQ:2 * TQ],
                  jnp.where(keep_next, s_loc[:, 2 * TQ:], NEG_INF)]
        pieces += [s_ctx[:, c0:c0 + TQ] for c0 in range(0, s_ctx.shape[1], TQ)]
        sink = jnp.zeros((rows, 1), F32)
        for g in range(GROUP):
            sink = jnp.where(row_id // TQ == g, sink_ref[j * GROUP + g], sink)
        mx = pieces[0]
        for pc in pieces[1:]:
            mx = jnp.maximum(mx, pc)
        m = jnp.maximum(jnp.max(mx, axis=-1, keepdims=True), sink)
        p = jnp.concatenate([jnp.exp(pc - m).astype(BF16) for pc in pieces], axis=1)
        return p, jnp.exp(sink - m)

    def output(j, p, sink_p):
        vs = slice(j * pair_w, (j + 1) * pair_w)
        v_all = jnp.concatenate([vp_ref[0, :, vs], vc_ref[0, :, vs], vn_ref[0, :, vs],
                                 vx_ref[0, :, vs]], axis=0)
        r = jnp.dot(p, v_all, preferred_element_type=F32)
        den = pltpu.roll(r, HEAD_DIM, 1) + sink_p
        o = r / den
        for g in range(0, GROUP, 2):
            even = o[g * TQ:(g + 1) * TQ]
            odd = pltpu.roll(o[(g + 1) * TQ:(g + 2) * TQ], HEAD_DIM, 1)
            c0 = (j * GROUP + g) * HEAD_DIM
            o_ref[0, :, c0:c0 + pair_w] = jnp.where(low_half, even, odd).astype(BF16)

    s_queue = [scores(0), scores(1)]
    p_queue = [probs(0, *s_queue.pop(0))]
    for j in range(N_KV_HEADS):
        if j + 2 < N_KV_HEADS:
            s_queue.append(scores(j + 2))
        ready = p_queue.pop(0)
        if j + 1 < N_KV_HEADS:
            p_queue.append(probs(j + 1, *s_queue.pop(0)))
        output(j, *ready)


def _attention(q, kt, v, kxt, vx, sink):
    b, l, _ = q.shape
    lc = kxt.shape[2]
    nblk = l // TQ
    cur = lambda bi, n: (bi, n, 0)
    prev = lambda bi, n: (bi, jnp.maximum(n - 1, 0), 0)
    nxt = lambda bi, n: (bi, jnp.minimum(n + 1, nblk - 1), 0)
    kcur = lambda bi, n: (bi, 0, n)
    kprev = lambda bi, n: (bi, 0, jnp.maximum(n - 1, 0))
    knxt = lambda bi, n: (bi, 0, jnp.minimum(n + 1, nblk - 1))
    kb = (1, KV_W, TQ)
    vb = (1, TQ, VEXT_W)
    return pl.pallas_call(
        _attn_kernel,
        grid=(b, nblk),
        in_specs=[pl.BlockSpec(memory_space=pltpu.SMEM),
                  pl.BlockSpec((1, TQ, ATTN_W), cur),
                  pl.BlockSpec(kb, kprev), pl.BlockSpec(kb, kcur), pl.BlockSpec(kb, knxt),
                  pl.BlockSpec(vb, prev), pl.BlockSpec(vb, cur), pl.BlockSpec(vb, nxt),
                  pl.BlockSpec((1, KV_W, lc), lambda bi, n: (bi, 0, 0)),
                  pl.BlockSpec((1, lc, VEXT_W), lambda bi, n: (bi, 0, 0))],
        out_specs=pl.BlockSpec((1, TQ, ATTN_W), cur),
        out_shape=jax.ShapeDtypeStruct((b, l, ATTN_W), BF16),
        compiler_params=_params("arbitrary", "arbitrary"),
        name="attention",
    )(sink, q, kt, kt, kt, v, v, v, kxt, vx)


def _shift_rows(a, d):
    n = a.shape[0]
    return pltpu.roll(a, (-d) % n, 0)


def _merge_kernel(ao_ref, u_ref, up_ref, un_ref, sa_ref, sp_ref, x_ref, mod_ref,
                  wpool_ref, pscale_ref, wab_ref, wpb_ref, wout_ref, g2_ref, wr_ref, br_ref,
                  x1_ref, h2_ref, idx_ref, gate_ref, rank_ref, cnt_ref, carry_ref, *, seq_len):
    bi = pl.program_id(0)
    i = pl.program_id(1)
    last = pl.num_programs(1) - 1
    tm, sub = TM_MERGE, MERGE_SUB

    @pl.when((bi == 0) & (i == 0))
    def _():
        carry_ref[...] = jnp.zeros_like(carry_ref)

    u = u_ref[0]
    prev = jnp.where(i > 0, up_ref[0], 0.0)
    nxt = jnp.where(i < last, un_ref[0], 0.0)
    ext = jnp.concatenate([prev, u, nxt], axis=0)

    def pool_diff(r0):
        t = i * tm + r0 + lax.broadcasted_iota(jnp.int32, (sub, 1), 0)
        diffs = []
        for g, w in enumerate(POOL_WINDOWS):
            cs = slice(g * POOL_GROUP_W, (g + 1) * POOL_GROUP_W)
            e = ext[r0:r0 + sub + 2 * POOL_HALO, cs]
            acc = _shift_rows(e, -1) + e
            step = 1
            while 2 * step < w:
                acc = _shift_rows(acc, -step) + _shift_rows(acc, step)
                step *= 2
            win = acc[POOL_HALO:POOL_HALO + sub]
            half = w // 2
            cnt = (jnp.minimum(t + half, seq_len) - jnp.maximum(t - half, 0)).astype(F32)
            diffs.append((win / cnt - u[r0:r0 + sub, cs]).astype(BF16))
        return diffs

    def branches(r0, diffs):
        rs = slice(r0, r0 + sub)
        a = jnp.dot(ao_ref[0, rs, :], wab_ref[...], preferred_element_type=F32)
        mixed = [jnp.dot(df, wpool_ref[g], preferred_element_type=F32) for g, df in enumerate(diffs)]
        pool_o = jnp.concatenate(mixed, axis=1) * pscale_ref[...]
        p = jnp.dot(pool_o.astype(BF16), wpb_ref[...], preferred_element_type=F32)
        merged = sa_ref[0, rs, :].astype(F32) * a + sp_ref[0, rs, :].astype(F32) * p
        return merged.astype(BF16)

    def project(r0, merged):
        rs = slice(r0, r0 + sub)
        o = jnp.dot(merged, wout_ref[...], preferred_element_type=F32)
        x1 = x_ref[0, rs, :] + mod_ref[0, 2:3, :] * o
        x1_ref[0, rs, :] = x1
        h2 = _norm_mod(x1, g2_ref[...], mod_ref[0, 3:4, :], mod_ref[0, 4:5, :])
        h2_ref[rs, :] = _pack_rows(h2)
        return jnp.dot(h2.astype(BF16), wr_ref[...], preferred_element_type=F32) + br_ref[...]

    def route(r0, logits, carry):
        lt = logits.T
        eid = lax.broadcasted_iota(jnp.int32, (N_EXPERTS, sub), 0).astype(F32)
        work = lt
        vals, idxs, sels = [], [], []
        for _ in range(TOP_K):
            mx = jnp.max(work, axis=0, keepdims=True)
            ix = jnp.min(jnp.where(work == mx, eid, float(N_EXPERTS)), axis=0, keepdims=True)
            sel = eid == ix
            work = jnp.where(sel, -jnp.inf, work)
            vals.append(mx)
            idxs.append(ix)
            sels.append(sel)
        ex = [jnp.exp(v - vals[0]) for v in vals]
        tot = ex[0] + ex[1] + ex[2] + ex[3]
        pad = [jnp.zeros_like(tot)] * (SUBLANES - TOP_K)
        gates = jnp.concatenate([e_ / tot for e_ in ex] + pad, axis=0)
        gate_ref[r0:r0 + sub, :] = gates.T[:, :TOP_K]
        idx_ref[:, r0:r0 + sub] = jnp.concatenate(idxs, axis=0).astype(jnp.int32)
        member = (sels[0] | sels[1] | sels[2] | sels[3]).astype(F32)
        t_row = lax.broadcasted_iota(jnp.int32, (sub, sub), 0)
        t_col = lax.broadcasted_iota(jnp.int32, (sub, sub), 1)
        earlier = (t_row < t_col).astype(BF16)
        before = jnp.dot(member.astype(BF16), earlier, preferred_element_type=F32) + carry
        ranks = [jnp.sum(jnp.where(s, before, 0.0), axis=0, keepdims=True) for s in sels]
        rank_ref[:, r0:r0 + sub] = jnp.concatenate(ranks, axis=0).astype(jnp.int32)
        return carry + jnp.sum(member, axis=1, keepdims=True)

    starts = list(range(0, tm, sub))
    carry = carry_ref[...]
    merged_q = [branches(starts[0], pool_diff(starts[0]))]
    logits_q = []
    for si, r0 in enumerate(starts):
        if si + 1 < len(starts):
            diffs_next = pool_diff(starts[si + 1])
        logits_q.append(project(r0, merged_q.pop(0)))
        if si + 1 < len(starts):
            merged_q.append(branches(starts[si + 1], diffs_next))
        if si >= 1:
            carry = route(starts[si - 1], logits_q.pop(0), carry)
    carry = route(starts[-1], logits_q.pop(0), carry)
    carry_ref[...] = carry
    cnt_ref[...] = carry


def _merge_kernel_old(ao_ref, u_ref, up_ref, un_ref, ga_ref, gp_ref, x_ref, mod_ref,
                  wpool_ref, pscale_ref, wab_ref, wpb_ref, wout_ref, g2_ref, wr_ref, br_ref,
                  x1_ref, h2_ref, idx_ref, gate_ref, rank_ref, cnt_ref, carry_ref, *, seq_len):
    bi = pl.program_id(0)
    i = pl.program_id(1)
    last = pl.num_programs(1) - 1
    tm = TM_MERGE

    @pl.when((bi == 0) & (i == 0))
    def _():
        carry_ref[...] = jnp.zeros_like(carry_ref)

    u = u_ref[0]
    prev = jnp.where(i > 0, up_ref[0], 0.0)
    nxt = jnp.where(i < last, un_ref[0], 0.0)
    ext = jnp.concatenate([prev, u, nxt], axis=0)
    t = i * tm + lax.broadcasted_iota(jnp.int32, (tm, 1), 0)
    mixed = []
    for g, w in enumerate(POOL_WINDOWS):
        cs = slice(g * POOL_GROUP_W, (g + 1) * POOL_GROUP_W)
        e = ext[:, cs]
        acc = _shift_rows(e, -1) + e
        step = 1
        while 2 * step < w:
            acc = _shift_rows(acc, -step) + _shift_rows(acc, step)
            step *= 2
        win = acc[POOL_HALO:POOL_HALO + tm]
        half = w // 2
        cnt = (jnp.minimum(t + half, seq_len) - jnp.maximum(t - half, 0)).astype(F32)
        diff = (win / cnt - u[:, cs]).astype(BF16)
        mixed.append(jnp.dot(diff, wpool_ref[g], preferred_element_type=F32))
    pool_o = jnp.concatenate(mixed, axis=1) * pscale_ref[...]

    a = jnp.dot(ao_ref[0], wab_ref[...], preferred_element_type=F32)
    p = jnp.dot(pool_o.astype(BF16), wpb_ref[...], preferred_element_type=F32)
    merged = jax.nn.sigmoid(ga_ref[0]) * a + jax.nn.sigmoid(gp_ref[0]) * p
    o = jnp.dot(merged.astype(BF16), wout_ref[...], preferred_element_type=F32)
    x1 = x_ref[0] + mod_ref[0, 2:3, :] * o
    x1_ref[0] = x1

    h2 = _norm_mod(x1, g2_ref[...], mod_ref[0, 3:4, :], mod_ref[0, 4:5, :])
    h2_ref[...] = _pack_rows(h2)
    logits = jnp.dot(h2.astype(BF16), wr_ref[...], preferred_element_type=F32) + br_ref[...]
    lane = lax.broadcasted_iota(jnp.int32, (tm, N_EXPERTS), 1).astype(F32)
    work = logits
    vals, idxs, sels = [], [], []
    for _ in range(TOP_K):
        mx = jnp.max(work, axis=-1, keepdims=True)
        ix = jnp.min(jnp.where(work == mx, lane, float(N_EXPERTS)), axis=-1, keepdims=True)
        sel = lane == ix
        work = jnp.where(sel, -jnp.inf, work)
        vals.append(mx)
        idxs.append(ix)
        sels.append(sel)
    ex = [jnp.exp(v - vals[0]) for v in vals]
    tot = ex[0] + ex[1] + ex[2] + ex[3]
    gate_ref[...] = jnp.concatenate([e_ / tot for e_ in ex], axis=1)
    idx_ref[...] = jnp.concatenate(idxs, axis=1).astype(jnp.int32)

    member = (sels[0] | sels[1] | sels[2] | sels[3]).astype(F32)
    r_id = lax.broadcasted_iota(jnp.int32, (tm, tm), 0)
    c_id = lax.broadcasted_iota(jnp.int32, (tm, tm), 1)
    tri = (r_id > c_id).astype(BF16)
    before = jnp.dot(tri, member.astype(BF16), preferred_element_type=F32) + carry_ref[...]
    ranks = [jnp.sum(jnp.where(s, before, 0.0), axis=-1, keepdims=True) for s in sels]
    rank_ref[...] = jnp.concatenate(ranks, axis=1).astype(jnp.int32)
    total = carry_ref[...] + jnp.sum(member, axis=0, keepdims=True)
    carry_ref[...] = total
    cnt_ref[...] = total


def _merge_route(attn_o, u, ga, gp, x, mod, w_pool_bf, pool_scale, wab, wpb, wout,
                 norm2_g, w_router, b_router):
    b, l, d = x.shape
    tm = TM_MERGE
    nt = l // tm
    n = b * l
    hb = tm // POOL_HALO
    row = lambda bi, i: (bi, i, 0)
    flat = lambda bi, i: (bi * nt + i, 0)
    halo_prev = lambda bi, i: (bi, jnp.maximum(i * hb - 1, 0), 0)
    halo_next = lambda bi, i: (bi, jnp.minimum((i + 1) * hb, l // POOL_HALO - 1), 0)
    slot_major = lambda bi, i: (0, bi * nt + i)
    out_shape = (
        jax.ShapeDtypeStruct((b, l, d), F32),
        jax.ShapeDtypeStruct((n, ROW_WORDS), U32),
        jax.ShapeDtypeStruct((TOP_K, n), jnp.int32),
        jax.ShapeDtypeStruct((n, TOP_K), F32),
        jax.ShapeDtypeStruct((TOP_K, n), jnp.int32),
        jax.ShapeDtypeStruct((N_EXPERTS, 1), F32),
    )
    return pl.pallas_call(
        functools.partial(_merge_kernel, seq_len=l),
        grid=(b, nt),
        in_specs=[pl.BlockSpec((1, tm, d), row),
                  pl.BlockSpec((1, tm, d), row),
                  pl.BlockSpec((1, POOL_HALO, d), halo_prev),
                  pl.BlockSpec((1, POOL_HALO, d), halo_next),
                  pl.BlockSpec((1, tm, d), row),
                  pl.BlockSpec((1, tm, d), row),
                  pl.BlockSpec((1, tm, d), row),
                  pl.BlockSpec((1, 6, d), lambda bi, i: (bi, 0, 0)),
                  _const_spec(w_pool_bf.shape),
                  _const_spec((1, d)),
                  _const_spec((d, d)), _const_spec((d, d)), _const_spec((d, d)),
                  _const_spec((1, d)),
                  _const_spec((d, N_EXPERTS)),
                  _const_spec((1, N_EXPERTS))],
        out_specs=(pl.BlockSpec((1, tm, d), row),
                   pl.BlockSpec((tm, ROW_WORDS), flat),
                   pl.BlockSpec((TOP_K, tm), slot_major),
                   pl.BlockSpec((tm, TOP_K), flat),
                   pl.BlockSpec((TOP_K, tm), slot_major),
                   _const_spec((N_EXPERTS, 1))),
        out_shape=out_shape,
        scratch_shapes=[pltpu.VMEM((N_EXPERTS, 1), F32)],
        compiler_params=_params("arbitrary", "arbitrary"),
        name="merge_route",
    )(attn_o, u, u, u, ga, gp, x, mod, w_pool_bf, pool_scale, wab, wpb, wout,
      norm2_g, w_router, b_router)


SC_CORES = 2
SC_SUBCORES = 16
SC_WORKERS = SC_CORES * SC_SUBCORES
SC_WINDOW = 128


def _sc_worker_id():
    return lax.axis_index("s") * SC_CORES + lax.axis_index("c")


def _sc_scatter_rows(rows, dest, pad_dest, out_rows):
    n, w = rows.shape
    n_win = n // (SC_WORKERS * SC_WINDOW)
    n_pad_win = pad_dest.shape[1]
    assert dest.shape == (TOP_K, n) and SC_WORKERS * n_win * SC_WINDOW == n
    mesh = plsc.VectorSubcoreMesh(core_axis_name="c", subcore_axis_name="s")
    zeros = jnp.zeros((SC_WINDOW, w), rows.dtype)

    @functools.partial(
        pl.kernel, mesh=mesh,
        out_type=jax.ShapeDtypeStruct((out_rows, w), rows.dtype),
        scratch_types=[pltpu.VMEM((TOP_K, SC_WINDOW), jnp.int32),
                       pltpu.VMEM((n_pad_win, SC_WINDOW), jnp.int32),
                       pltpu.VMEM((SC_WINDOW, w), rows.dtype),
                       pltpu.SemaphoreType.DMA],
        name="sc_scatter_rows",
    )
    def scatter(rows_hbm, dest_hbm, pad_hbm, zeros_hbm, out_hbm, idx_v, pad_v, rows_v, sem):
        wid = _sc_worker_id()

        @pl.loop(0, n_win)
        def _(j):
            base = (wid * n_win + j) * SC_WINDOW
            pltpu.sync_copy(rows_hbm.at[pl.ds(base, SC_WINDOW)], rows_v)
            pltpu.sync_copy(dest_hbm.at[:, pl.ds(base, SC_WINDOW)], idx_v)
            copies = [pltpu.async_copy(rows_v, out_hbm.at[idx_v.at[k]], sem) for k in range(TOP_K)]
            for cp in copies:
                cp.wait()

        pltpu.sync_copy(zeros_hbm, rows_v)
        pltpu.sync_copy(pad_hbm.at[wid], pad_v)
        copies = [pltpu.async_copy(rows_v, out_hbm.at[pad_v.at[j]], sem) for j in range(n_pad_win)]
        for cp in copies:
            cp.wait()

    return scatter(rows, dest, pad_dest, zeros)


def _expert_kernel(be_ref, first_ref, next_ref, nu_ref, x_ref, wgu_hbm, bgu_ref, wd_hbm, bd_ref,
                   y_ref, wgu_f, wd_f, wgu_s, wd_s, sem):
    i = pl.program_id(0)

    def weight_copies(e):
        return (pltpu.make_async_copy(wgu_hbm.at[e], wgu_f, sem.at[0]),
                pltpu.make_async_copy(wd_hbm.at[e], wd_f, sem.at[1]))

    @pl.when(i == 0)
    def _():
        for cp in weight_copies(be_ref[0]):
            cp.start()

    @pl.when(i < nu_ref[0])
    def _():
        @pl.when(first_ref[i] == 1)
        def _():
            for cp in weight_copies(be_ref[i]):
                cp.wait()
            ck = 256
            for r0 in range(0, D_MODEL, ck):
                wgu_s[r0:r0 + ck, :] = wgu_f[r0:r0 + ck, :].astype(BF16)
                wd_s[r0:r0 + ck, :] = wd_f[r0:r0 + ck, :].astype(BF16)

            @pl.when(next_ref[i] >= 0)
            def _():
                for cp in weight_copies(next_ref[i]):
                    cp.start()

        x = _unpack_rows(x_ref[...]).astype(BF16)
        fc = EXPERT_FF_CHUNK

        def up(c0):
            gt = jnp.dot(x, wgu_s[:, c0:c0 + fc], preferred_element_type=F32)
            ln = jnp.dot(x, wgu_s[:, D_FF + c0:D_FF + c0 + fc], preferred_element_type=F32)
            return gt, ln

        def down(c0, gt, ln):
            gt = jnp.minimum(gt + bgu_ref[0, :, c0:c0 + fc], SWIGLU_LIMIT)
            ln = jnp.clip(ln + bgu_ref[0, :, D_FF + c0:D_FF + c0 + fc], -SWIGLU_LIMIT, SWIGLU_LIMIT)
            act = gt * jax.nn.sigmoid(SWIGLU_ALPHA * gt) * (ln + 1.0)
            return jnp.dot(act.astype(BF16), wd_s[c0:c0 + fc, :], preferred_element_type=F32)

        chunks = list(range(0, D_FF, fc))
        pending = [up(chunks[0])]
        acc = jnp.zeros((TM_EXPERT, D_MODEL), F32)
        for ci, c0 in enumerate(chunks):
            if ci + 1 < len(chunks):
                pending.append(up(chunks[ci + 1]))
            acc = acc + down(c0, *pending.pop(0))
        y_ref[...] = _pack_rows(acc + bd_ref[0])

    @pl.when(i >= nu_ref[0])
    def _():
        y_ref[...] = jnp.zeros_like(y_ref)


def _experts(block_e, first, next_e, n_used, xs, w_gu, b_gu, w_down, b_down):
    d = D_MODEL
    tm = TM_EXPERT
    nb = block_e.shape[0]

    def blk(i, be, fi, nx, nu):
        return jnp.minimum(i, nu[0] - 1)

    xmap = lambda i, be, fi, nx, nu: (blk(i, be, fi, nx, nu), 0)
    bmap = lambda i, be, fi, nx, nu: (be[blk(i, be, fi, nx, nu)], 0, 0)
    grid_spec = pltpu.PrefetchScalarGridSpec(
        num_scalar_prefetch=4,
        grid=(nb,),
        in_specs=[pl.BlockSpec((tm, ROW_WORDS), xmap),
                  pl.BlockSpec(memory_space=pl.ANY),
                  pl.BlockSpec((1, 1, 2 * D_FF), bmap),
                  pl.BlockSpec(memory_space=pl.ANY),
                  pl.BlockSpec((1, 1, d), bmap)],
        out_specs=pl.BlockSpec((tm, ROW_WORDS), lambda i, be, fi, nx, nu: (i, 0)),
        scratch_shapes=[pltpu.VMEM((d, 2 * D_FF), F32), pltpu.VMEM((D_FF, d), F32),
                        pltpu.VMEM((d, 2 * D_FF), BF16), pltpu.VMEM((D_FF, d), BF16),
                        pltpu.SemaphoreType.DMA((2,))],
    )
    return pl.pallas_call(
        _expert_kernel,
        grid_spec=grid_spec,
        out_shape=jax.ShapeDtypeStruct((nb * tm, ROW_WORDS), U32),
        compiler_params=_params("arbitrary"),
        name="moe_experts",
    )(block_e, first, next_e, n_used, xs, w_gu, b_gu.reshape(N_EXPERTS, 1, 2 * D_FF),
      w_down, b_down.reshape(N_EXPERTS, 1, d))


def _sc_gather_rows(table, idx):
    m = idx.shape[0]
    w = table.shape[1]
    per_worker = m // SC_WORKERS
    n_win = per_worker // SC_WINDOW
    assert per_worker * SC_WORKERS == m and n_win * SC_WINDOW == per_worker
    mesh = plsc.VectorSubcoreMesh(core_axis_name="c", subcore_axis_name="s")

    @functools.partial(
        pl.kernel, mesh=mesh,
        out_type=jax.ShapeDtypeStruct((m, w), table.dtype),
        scratch_types=[pltpu.VMEM((SC_WINDOW,), jnp.int32),
                       pltpu.VMEM((SC_WINDOW, w), table.dtype),
                       pltpu.SemaphoreType.DMA],
        name="sc_gather_rows",
    )
    def gather(table_hbm, idx_hbm, out_hbm, idx_v, rows_v, sem):
        wid = _sc_worker_id()

        @pl.loop(0, n_win)
        def _(j):
            base = wid * per_worker + j * SC_WINDOW
            pltpu.sync_copy(idx_hbm.at[pl.ds(base, SC_WINDOW)], idx_v)
            pltpu.async_copy(table_hbm.at[idx_v], rows_v, sem).wait()
            pltpu.sync_copy(rows_v, out_hbm.at[pl.ds(base, SC_WINDOW)])

    return gather(table, idx)


def _combine_dense_kernel(y4_ref, gate_ref, x1_ref, mod_ref, fg_ref, o_ref):
    gate = gate_ref[...]
    y = gate[:, 0:1] * _unpack_rows(y4_ref[0])
    for k in range(1, TOP_K):
        y = y + gate[:, k:k + 1] * _unpack_rows(y4_ref[k])
    x2 = x1_ref[...] + mod_ref[0, 5:6, :] * y
    o_ref[...] = x2 * lax.rsqrt(jnp.mean(x2 * x2, axis=-1, keepdims=True) + NORM_EPS) * fg_ref[...]


def _combine_dense(y4, gate_w, x1, mod, final_g):
    b, l, d = x1.shape
    ts = 512
    nt = l // ts
    return pl.pallas_call(
        _combine_dense_kernel,
        grid=(b * nt,),
        in_specs=[pl.BlockSpec((TOP_K, ts, ROW_WORDS), lambda s: (0, s, 0)),
                  pl.BlockSpec((ts, TOP_K), lambda s: (s, 0)),
                  pl.BlockSpec((ts, d), lambda s: (s, 0)),
                  pl.BlockSpec((1, 6, d), lambda s: (s // nt, 0, 0)),
                  _const_spec((1, d))],
        out_specs=pl.BlockSpec((ts, d), lambda s: (s, 0)),
        out_shape=jax.ShapeDtypeStruct((b * l, d), F32),
        compiler_params=_params("arbitrary"),
        name="moe_combine",
    )(y4, gate_w, x1.reshape(b * l, d), mod, final_g).reshape(b, l, d)


def _rope_tables(seq_len):
    inv_freq = ROPE_BASE ** (-jnp.arange(ROPE_PAIRS, dtype=F32) / ROPE_PAIRS)
    reps = LANES // HEAD_DIM
    out = []
    for n_pos, on_row in ((seq_len // GRID_W, True), (GRID_W, False)):
        ang = jnp.arange(n_pos, dtype=F32)[:, None] * inv_freq
        zero = jnp.zeros_like(ang)
        for fn, sign in ((jnp.cos, 1.0), (jnp.sin, -1.0)):
            v = fn(ang)
            head = ([sign * v, v, zero, zero] if on_row else [zero, zero, sign * v, v])
            out.append(jnp.concatenate(head * reps, axis=1))
    return out


def kernel(x, c, ctx, c_ctx, w_ada, b_ada, norm1_g, norm2_g, w_in, b_in, attn_sink, w_pool,
           pool_scale, w_attn_br, w_pool_br, w_out, w_router, b_router, w_gu, b_gu, w_down,
           b_down, final_g):
    b, l, d = x.shape
    n = b * l
    assert w_ada.shape[0] == 1, "single-layer block"

    cond = jnp.zeros((SUBLANES, d), F32).at[:b].set(c).at[b].set(c_ctx)
    mod = _adaln(cond, w_ada[0], b_ada[0])[:b + 1].reshape(b + 1, 6, d)
    mod_x, mod_c = mod[:b], mod[b:b + 1]

    rope = _rope_tables(l)
    v0 = ATTN_W + KV_W
    w_v = w_in[0][:, v0:v0 + KV_W].reshape(d, N_KV_HEADS, HEAD_DIM)
    w_v = jnp.concatenate([w_v, jnp.zeros_like(w_v)], axis=-1).reshape(d, VEXT_W)
    b_v = b_in[0][v0:v0 + KV_W].reshape(N_KV_HEADS, HEAD_DIM)
    b_v = jnp.concatenate([b_v, jnp.ones_like(b_v)], axis=-1).reshape(VEXT_W)
    w_in_bf = jnp.concatenate([w_in[0][:, :v0], w_v, w_in[0][:, v0 + KV_W:]], axis=1).astype(BF16)
    b_in2 = jnp.concatenate([b_in[0][:v0], b_v, b_in[0][v0 + KV_W:]]).reshape(1, IN_W_EXT)
    g1 = norm1_g[0].reshape(1, d)
    q, kt, v, u, ga, gp = _inproj(x, mod_x, g1, w_in_bf, b_in2, rope)
    kv_sl = slice(ATTN_W, ATTN_W + KV_W + VEXT_W)
    kxt, vx = _ctx_kv(ctx, mod_c, g1, w_in_bf[:, kv_sl], b_in2[:, kv_sl])

    attn_o = _attention(q, kt, v, kxt, vx, attn_sink[0])

    x1, h2, top_idx, gate_w, rank, counts = _merge_route(
        attn_o, u, ga, gp, x, mod_x, w_pool[0].astype(BF16), pool_scale[0].reshape(1, d),
        w_attn_br[0].astype(BF16), w_pool_br[0].astype(BF16), w_out[0].astype(BF16),
        norm2_g[0].reshape(1, d), w_router[0].astype(BF16), b_router[0].reshape(1, N_EXPERTS))

    tm = TM_EXPERT
    nb = n * TOP_K // tm + N_EXPERTS
    cnt = counts[:, 0].astype(jnp.int32)
    padded = (cnt + tm - 1) // tm * tm
    pend = jnp.cumsum(padded)
    pstart = pend - padded
    expert_ids = jnp.arange(N_EXPERTS, dtype=jnp.int32)
    dest = jnp.sum(jnp.where(top_idx[..., None] == expert_ids, pstart, 0), axis=-1) + rank
    n_used = (pend[-1] // tm).reshape(1)
    block_start = jnp.arange(nb, dtype=jnp.int32) * tm
    block_e = jnp.minimum(jnp.sum((pend[None, :] <= block_start[:, None]).astype(jnp.int32), axis=1),
                          N_EXPERTS - 1)
    first = jnp.concatenate([jnp.ones((1,), jnp.int32),
                             (block_e[1:] != block_e[:-1]).astype(jnp.int32)])
    later_used = (expert_ids[None, :] > expert_ids[:, None]) & (cnt[None, :] > 0)
    next_of_expert = jnp.min(jnp.where(later_used, expert_ids[None, :], N_EXPERTS), axis=1)
    next_of_expert = jnp.where(next_of_expert < N_EXPERTS, next_of_expert, -1)
    next_e = jnp.sum(jnp.where(block_e[:, None] == expert_ids, next_of_expert, 0), axis=-1)

    pad_j = jnp.arange(tm, dtype=jnp.int32)[None, :]
    spare = nb * tm + expert_ids[:, None] * tm + pad_j
    pad_dest = jnp.where(pad_j < (padded - cnt)[:, None], (pstart + cnt)[:, None] + pad_j, spare)
    pad_dest = pad_dest.reshape(SC_WORKERS, N_EXPERTS * tm // (SC_WORKERS * SC_WINDOW), SC_WINDOW)

    xs = _sc_scatter_rows(h2, dest, pad_dest, (nb + N_EXPERTS) * tm)
    ys = _experts(block_e, first, next_e, n_used, xs, w_gu[0], b_gu[0], w_down[0], b_down[0])
    y4 = _sc_gather_rows(ys, dest.reshape(-1)).reshape(TOP_K, n, ROW_WORDS)
    return _combine_dense(y4, gate_w, x1, mod_x, final_g.reshape(1, d))
```

```python
import functools

import jax
import jax.numpy as jnp
from jax import lax
from jax.experimental import pallas as pl
from jax.experimental.pallas import tpu as pltpu
from jax.experimental.pallas import tpu_sc as plsc

D_MODEL = 1024
GRID_W = 64
HEAD_DIM = 64
N_HEADS = 16
N_KV_HEADS = 4
GROUP = N_HEADS // N_KV_HEADS
ATTN_W = N_HEADS * HEAD_DIM
KV_W = N_KV_HEADS * HEAD_DIM
WINDOW = 128
ATTN_SCALE = HEAD_DIM ** -0.5
ROPE_BASE = 10000.0
ROPE_PAIRS = HEAD_DIM // 4
POOL_WINDOWS = (2, 4, 8, 16)
POOL_GROUP_W = D_MODEL // len(POOL_WINDOWS)
IN_W = ATTN_W + 2 * KV_W + D_MODEL + 2 * D_MODEL
VEXT_W = N_KV_HEADS * 2 * HEAD_DIM
IN_W_EXT = IN_W - KV_W + VEXT_W
N_EXPERTS = 32
TOP_K = 4
D_FF = D_MODEL
SWIGLU_ALPHA = 1.702
SWIGLU_LIMIT = 7.0
NORM_EPS = 1e-5
NEG_INF = -1e30

LANES = 128
SUBLANES = 8
VMEM_LIMIT = 56 * 1024 * 1024

TM_INPROJ = 512
TQ = WINDOW
TM_MERGE = 512
MERGE_SUB = 256
TM_EXPERT = 256
EXPERT_FF_CHUNK = 1024
EXPERT_BLOCKS_PER_STEP = 2
POOL_HALO = SUBLANES

F32 = jnp.float32
BF16 = jnp.bfloat16


ROW_WORDS = D_MODEL // 2
U32 = jnp.uint32


def _pack_rows(value):
    bits = lax.bitcast_convert_type(value.astype(BF16).astype(F32), U32)
    return bits[:, :ROW_WORDS] | (bits[:, ROW_WORDS:] >> 16)


def _unpack_rows(words):
    hi = lax.bitcast_convert_type(words & jnp.uint32(0xFFFF0000), F32)
    lo = lax.bitcast_convert_type(words << 16, F32)
    return jnp.concatenate([hi, lo], axis=1)


def _params(*sem):
    return pltpu.CompilerParams(dimension_semantics=sem, vmem_limit_bytes=VMEM_LIMIT)


def _const_spec(shape):
    nd = len(shape)
    return pl.BlockSpec(shape, lambda *_: (0,) * nd)


def _adaln_kernel(c_ref, w_ref, b_ref, o_ref):
    c = c_ref[...]
    s = c * jax.nn.sigmoid(c)
    o_ref[...] = jnp.dot(s.astype(BF16), w_ref[...].astype(BF16),
                         preferred_element_type=F32) + b_ref[...]


def _adaln(cond, w_ada, b_ada):
    rows, d = cond.shape
    n = w_ada.shape[1]
    tn = 1024
    return pl.pallas_call(
        _adaln_kernel,
        grid=(n // tn,),
        in_specs=[_const_spec((rows, d)),
                  pl.BlockSpec((d, tn), lambda j: (0, j)),
                  pl.BlockSpec((1, tn), lambda j: (0, j))],
        out_specs=pl.BlockSpec((rows, tn), lambda j: (0, j)),
        out_shape=jax.ShapeDtypeStruct((rows, n), F32),
        compiler_params=_params("arbitrary"),
        name="adaln",
    )(cond, w_ada, b_ada.reshape(1, n))


def _norm_mod(x, g, shift, scale):
    y = x * lax.rsqrt(jnp.mean(x * x, axis=-1, keepdims=True) + NORM_EPS) * g
    return y * (1.0 + scale) + shift


def _rope(t, cos, sin_signed):
    w = t.shape[1]
    half = ROPE_PAIRS
    lane = lax.broadcasted_iota(jnp.int32, t.shape, 1)
    first = (lane & (2 * half - 1)) < half
    rot = jnp.where(first, pltpu.roll(t, w - half, 1), pltpu.roll(t, half, 1))
    reps = w // LANES
    cos_w = jnp.concatenate([cos] * reps, axis=1) if reps > 1 else cos
    sin_w = jnp.concatenate([sin_signed] * reps, axis=1) if reps > 1 else sin_signed
    return t * cos_w + rot * sin_w


def _token_table(row_ref, col_ref):
    gr = row_ref.shape[0]
    shape = (gr, GRID_W, LANES)
    full = (jnp.broadcast_to(row_ref[...][:, None, :], shape)
            + jnp.broadcast_to(col_ref[...][None, :, :], shape))
    return full.reshape(gr * GRID_W, LANES)


def _inproj_kernel(x_ref, mod_ref, g_ref, w_ref, b_ref, cr_ref, sr_ref, cc_ref, sc_ref,
                   q_ref, k_ref, v_ref, u_ref, ga_ref, gp_ref):
    h = _norm_mod(x_ref[0], g_ref[...], mod_ref[0, 0:1, :], mod_ref[0, 1:2, :])
    hb = h.astype(BF16)
    cos = _token_table(cr_ref, cc_ref)
    sin = _token_table(sr_ref, sc_ref)

    def proj(c0, c1):
        return jnp.dot(hb, w_ref[:, c0:c1], preferred_element_type=F32) + b_ref[:, c0:c1]

    cw = 512
    for c0 in range(0, ATTN_W, cw):
        q_ref[0, :, c0:c0 + cw] = (_rope(proj(c0, c0 + cw), cos, sin) * ATTN_SCALE).astype(BF16)
    k_ref[0] = _rope(proj(ATTN_W, ATTN_W + KV_W), cos, sin).T.astype(BF16)
    v_ref[0] = proj(ATTN_W + KV_W, ATTN_W + KV_W + VEXT_W).astype(BF16)
    base = ATTN_W + KV_W + VEXT_W
    for c0 in range(0, D_MODEL, cw):
        u_ref[0, :, c0:c0 + cw] = proj(base + c0, base + c0 + cw)
    for ref in (ga_ref, gp_ref):
        base += D_MODEL
        for c0 in range(0, D_MODEL, cw):
            ref[0, :, c0:c0 + cw] = jax.nn.sigmoid(proj(base + c0, base + c0 + cw)).astype(BF16)


def _inproj(x, mod, norm_g, w_in_bf, b_in, rope):
    b, l, d = x.shape
    tm = TM_INPROJ
    row = lambda bi, i: (bi, i, 0)
    out_shape = (
        jax.ShapeDtypeStruct((b, l, ATTN_W), BF16),
        jax.ShapeDtypeStruct((b, KV_W, l), BF16),
        jax.ShapeDtypeStruct((b, l, VEXT_W), BF16),
        jax.ShapeDtypeStruct((b, l, D_MODEL), F32),
        jax.ShapeDtypeStruct((b, l, D_MODEL), BF16),
        jax.ShapeDtypeStruct((b, l, D_MODEL), BF16),
    )
    return pl.pallas_call(
        _inproj_kernel,
        grid=(b, l // tm),
        in_specs=[pl.BlockSpec((1, tm, d), row),
                  pl.BlockSpec((1, 6, d), lambda bi, i: (bi, 0, 0)),
                  _const_spec((1, d)),
                  _const_spec((d, IN_W_EXT)),
                  _const_spec((1, IN_W_EXT)),
                  pl.BlockSpec((tm // GRID_W, LANES), lambda bi, i: (i, 0)),
                  pl.BlockSpec((tm // GRID_W, LANES), lambda bi, i: (i, 0)),
                  _const_spec((GRID_W, LANES)),
                  _const_spec((GRID_W, LANES))],
        out_specs=(pl.BlockSpec((1, tm, ATTN_W), row),
                   pl.BlockSpec((1, KV_W, tm), lambda bi, i: (bi, 0, i)),
                   pl.BlockSpec((1, tm, VEXT_W), row),
                   pl.BlockSpec((1, tm, D_MODEL), row),
                   pl.BlockSpec((1, tm, D_MODEL), row),
                   pl.BlockSpec((1, tm, D_MODEL), row)),
        out_shape=out_shape,
        compiler_params=_params("arbitrary", "arbitrary"),
        name="inproj",
    )(x, mod, norm_g, w_in_bf, b_in, *rope)


def _ctx_kv_kernel(x_ref, mod_ref, g_ref, w_ref, b_ref, k_ref, v_ref):
    h = _norm_mod(x_ref[0], g_ref[...], mod_ref[0, 0:1, :], mod_ref[0, 1:2, :])
    kv = jnp.dot(h.astype(BF16), w_ref[...], preferred_element_type=F32) + b_ref[...]
    k_ref[0] = kv[:, :KV_W].T.astype(BF16)
    v_ref[0] = kv[:, KV_W:].astype(BF16)


def _ctx_kv(ctx, mod_c, norm_g, w_kv_bf, b_kv):
    b, lc, d = ctx.shape
    row = lambda bi: (bi, 0, 0)
    return pl.pallas_call(
        _ctx_kv_kernel,
        grid=(b,),
        in_specs=[pl.BlockSpec((1, lc, d), row),
                  _const_spec((1, 6, d)),
                  _const_spec((1, d)),
                  _const_spec((d, KV_W + VEXT_W)),
                  _const_spec((1, KV_W + VEXT_W))],
        out_specs=(pl.BlockSpec((1, KV_W, lc), row), pl.BlockSpec((1, lc, VEXT_W), row)),
        out_shape=(jax.ShapeDtypeStruct((b, KV_W, lc), BF16),
                   jax.ShapeDtypeStruct((b, lc, VEXT_W), BF16)),
        compiler_params=_params("arbitrary"),
        name="ctx_kv",
    )(ctx, mod_c, norm_g, w_kv_bf, b_kv)


def _attn_kernel(sink_ref, q_ref, kp_ref, kc_ref, kn_ref, vp_ref, vc_ref, vn_ref,
                 kx_ref, vx_ref, o_ref):
    n = pl.program_id(1)
    last = pl.num_programs(1) - 1
    rows = GROUP * TQ
    pair_w = 2 * HEAD_DIM
    qi = lax.broadcasted_iota(jnp.int32, (rows, TQ), 0) & (TQ - 1)
    col = lax.broadcasted_iota(jnp.int32, (rows, TQ), 1)
    keep_prev = (col >= qi) & (n > 0)
    keep_next = (col <= qi) & (n < last)
    low_half = lax.broadcasted_iota(jnp.int32, (TQ, pair_w), 1) < HEAD_DIM
    row_id = lax.broadcasted_iota(jnp.int32, (rows, 1), 0)

    def scores(j):
        parts = []
        for g in range(GROUP):
            h = j * GROUP + g
            pair = q_ref[0, :, (h // 2) * pair_w:(h // 2 + 1) * pair_w]
            parts.append(jnp.where(low_half if h % 2 == 0 else ~low_half, pair, jnp.zeros_like(pair)))
        q4 = jnp.concatenate(parts, axis=0)
        ks = slice(j * HEAD_DIM, (j + 1) * HEAD_DIM)
        kt = jnp.concatenate([kp_ref[0, ks, :], kc_ref[0, ks, :], kn_ref[0, ks, :]], axis=1)
        s_loc = jnp.dot(q4, jnp.concatenate([kt, kt], axis=0), preferred_element_type=F32)
        kxt = kx_ref[0, ks, :]
        s_ctx = jnp.dot(q4, jnp.concatenate([kxt, kxt], axis=0), preferred_element_type=F32)
        return s_loc, s_ctx

    def probs(j, s_loc, s_ctx):
        pieces = [jnp.where(keep_prev, s_loc[:, :TQ], NEG_INF), s_loc[:, TQ:2 * TQ],
                  jnp.where(keep_next, s_loc[:, 2 * TQ:], NEG_INF)]
        pieces += [s_ctx[:, c0:c0 + TQ] for c0 in range(0, s_ctx.shape[1], TQ)]
        sink = jnp.zeros((rows, 1), F32)
        for g in range(GROUP):
            sink = jnp.where(row_id // TQ == g, sink_ref[j * GROUP + g], sink)
        mx = pieces[0]
        for pc in pieces[1:]:
            mx = jnp.maximum(mx, pc)
        m = jnp.maximum(jnp.max(mx, axis=-1, keepdims=True), sink)
        p = jnp.concatenate([jnp.exp(pc - m).astype(BF16) for pc in pieces], axis=1)
        return p, jnp.exp(sink - m)

    def output(j, p, sink_p):
        vs = slice(j * pair_w, (j + 1) * pair_w)
        v_all = jnp.concatenate([vp_ref[0, :, vs], vc_ref[0, :, vs], vn_ref[0, :, vs],
                                 vx_ref[0, :, vs]], axis=0)
        r = jnp.dot(p, v_all, preferred_element_type=F32)
        den = pltpu.roll(r, HEAD_DIM, 1) + sink_p
        o = r / den
        for g in range(0, GROUP, 2):
            even = o[g * TQ:(g + 1) * TQ]
            odd = pltpu.roll(o[(g + 1) * TQ:(g + 2) * TQ], HEAD_DIM, 1)
            c0 = (j * GROUP + g) * HEAD_DIM
            o_ref[0, :, c0:c0 + pair_w] = jnp.where(low_half, even, odd).astype(BF16)

    s_queue = [scores(0), scores(1)]
    p_queue = [probs(0, *s_queue.pop(0))]
    for j in range(N_KV_HEADS):
        if j + 2 < N_KV_HEADS:
            s_queue.append(scores(j + 2))
        ready = p_queue.pop(0)
        if j + 1 < N_KV_HEADS:
            p_queue.append(probs(j + 1, *s_queue.pop(0)))
        output(j, *ready)


def _attention(q, kt, v, kxt, vx, sink):
    b, l, _ = q.shape
    lc = kxt.shape[2]
    nblk = l // TQ
    cur = lambda bi, n: (bi, n, 0)
    prev = lambda bi, n: (bi, jnp.maximum(n - 1, 0), 0)
    nxt = lambda bi, n: (bi, jnp.minimum(n + 1, nblk - 1), 0)
    kcur = lambda bi, n: (bi, 0, n)
    kprev = lambda bi, n: (bi, 0, jnp.maximum(n - 1, 0))
    knxt = lambda bi, n: (bi, 0, jnp.minimum(n + 1, nblk - 1))
    kb = (1, KV_W, TQ)
    vb = (1, TQ, VEXT_W)
    return pl.pallas_call(
        _attn_kernel,
        grid=(b, nblk),
        in_specs=[pl.BlockSpec(memory_space=pltpu.SMEM),
                  pl.BlockSpec((1, TQ, ATTN_W), cur),
                  pl.BlockSpec(kb, kprev), pl.BlockSpec(kb, kcur), pl.BlockSpec(kb, knxt),
                  pl.BlockSpec(vb, prev), pl.BlockSpec(vb, cur), pl.BlockSpec(vb, nxt),
                  pl.BlockSpec((1, KV_W, lc), lambda bi, n: (bi, 0, 0)),
                  pl.BlockSpec((1, lc, VEXT_W), lambda bi, n: (bi, 0, 0))],
        out_specs=pl.BlockSpec((1, TQ, ATTN_W), cur),
        out_shape=jax.ShapeDtypeStruct((b, l, ATTN_W), BF16),
        compiler_params=_params("arbitrary", "arbitrary"),
        name="attention",
    )(sink, q, kt, kt, kt, v, v, v, kxt, vx)


def _shift_rows(a, d):
    n = a.shape[0]
    return pltpu.roll(a, (-d) % n, 0)


def _merge_kernel(ao_ref, u_ref, up_ref, un_ref, sa_ref, sp_ref, x_ref, mod_ref,
                  wpool_ref, pscale_ref, wab_ref, wpb_ref, wout_ref, g2_ref, wr_ref, br_ref,
                  x1_ref, h2_ref, idx_ref, gate_ref, rank_ref, cnt_ref, carry_ref, *, seq_len):
    bi = pl.program_id(0)
    i = pl.program_id(1)
    last = pl.num_programs(1) - 1
    tm, sub = TM_MERGE, MERGE_SUB

    @pl.when((bi == 0) & (i == 0))
    def _():
        carry_ref[...] = jnp.zeros_like(carry_ref)

    u = u_ref[0]
    prev = jnp.where(i > 0, up_ref[0], 0.0)
    nxt = jnp.where(i < last, un_ref[0], 0.0)
    ext = jnp.concatenate([prev, u, nxt], axis=0)

    def pool_diff(r0):
        t = i * tm + r0 + lax.broadcasted_iota(jnp.int32, (sub, 1), 0)
        diffs = []
        for g, w in enumerate(POOL_WINDOWS):
            cs = slice(g * POOL_GROUP_W, (g + 1) * POOL_GROUP_W)
            e = ext[r0:r0 + sub + 2 * POOL_HALO, cs]
            acc = _shift_rows(e, -1) + e
            step = 1
            while 2 * step < w:
                acc = _shift_rows(acc, -step) + _shift_rows(acc, step)
                step *= 2
            win = acc[POOL_HALO:POOL_HALO + sub]
            half = w // 2
            cnt = (jnp.minimum(t + half, seq_len) - jnp.maximum(t - half, 0)).astype(F32)
            diffs.append((win / cnt - u[r0:r0 + sub, cs]).astype(BF16))
        return diffs

    def branches(r0, diffs):
        rs = slice(r0, r0 + sub)
        a = jnp.dot(ao_ref[0, rs, :], wab_ref[...], preferred_element_type=F32)
        mixed = [jnp.dot(df, wpool_ref[g], preferred_element_type=F32) for g, df in enumerate(diffs)]
        pool_o = jnp.concatenate(mixed, axis=1) * pscale_ref[...]
        p = jnp.dot(pool_o.astype(BF16), wpb_ref[...], preferred_element_type=F32)
        merged = sa_ref[0, rs, :].astype(F32) * a + sp_ref[0, rs, :].astype(F32) * p
        return merged.astype(BF16)

    def project(r0, merged):
        rs = slice(r0, r0 + sub)
        o = jnp.dot(merged, wout_ref[...], preferred_element_type=F32)
        x1 = x_ref[0, rs, :] + mod_ref[0, 2:3, :] * o
        x1_ref[0, rs, :] = x1
        h2 = _norm_mod(x1, g2_ref[...], mod_ref[0, 3:4, :], mod_ref[0, 4:5, :])
        h2_ref[rs, :] = _pack_rows(h2)
        return jnp.dot(h2.astype(BF16), wr_ref[...], preferred_element_type=F32) + br_ref[...]

    def route(r0, logits, carry):
        lt = logits.T
        eid = lax.broadcasted_iota(jnp.int32, (N_EXPERTS, sub), 0).astype(F32)
        work = lt
        vals, idxs, sels = [], [], []
        for _ in range(TOP_K):
            mx = jnp.max(work, axis=0, keepdims=True)
            ix = jnp.min(jnp.where(work == mx, eid, float(N_EXPERTS)), axis=0, keepdims=True)
            sel = eid == ix
            work = jnp.where(sel, -jnp.inf, work)
            vals.append(mx)
            idxs.append(ix)
            sels.append(sel)
        ex = [jnp.exp(v - vals[0]) for v in vals]
        tot = ex[0] + ex[1] + ex[2] + ex[3]
        pad = [jnp.zeros_like(tot)] * (SUBLANES - TOP_K)
        gates = jnp.concatenate([e_ / tot for e_ in ex] + pad, axis=0)
        gate_ref[r0:r0 + sub, :] = gates.T[:, :TOP_K]
        idx_ref[:, r0:r0 + sub] = jnp.concatenate(idxs, axis=0).astype(jnp.int32)
        member = (sels[0] | sels[1] | sels[2] | sels[3]).astype(F32)
        t_row = lax.broadcasted_iota(jnp.int32, (sub, sub), 0)
        t_col = lax.broadcasted_iota(jnp.int32, (sub, sub), 1)
        earlier = (t_row < t_col).astype(BF16)
        before = jnp.dot(member.astype(BF16), earlier, preferred_element_type=F32) + carry
        ranks = [jnp.sum(jnp.where(s, before, 0.0), axis=0, keepdims=True) for s in sels]
        rank_ref[:, r0:r0 + sub] = jnp.concatenate(ranks, axis=0).astype(jnp.int32)
        return carry + jnp.sum(member, axis=1, keepdims=True)

    starts = list(range(0, tm, sub))
    carry = carry_ref[...]
    merged_q = [branches(starts[0], pool_diff(starts[0]))]
    logits_q = []
    for si, r0 in enumerate(starts):
        if si + 1 < len(starts):
            diffs_next = pool_diff(starts[si + 1])
        logits_q.append(project(r0, merged_q.pop(0)))
        if si + 1 < len(starts):
            merged_q.append(branches(starts[si + 1], diffs_next))
        if si >= 1:
            carry = route(starts[si - 1], logits_q.pop(0), carry)
    carry = route(starts[-1], logits_q.pop(0), carry)
    carry_ref[...] = carry
    cnt_ref[...] = carry


def _merge_route(attn_o, u, ga, gp, x, mod, w_pool_bf, pool_scale, wab, wpb, wout,
                 norm2_g, w_router, b_router):
    b, l, d = x.shape
    tm = TM_MERGE
    nt = l // tm
    n = b * l
    hb = tm // POOL_HALO
    row = lambda bi, i: (bi, i, 0)
    flat = lambda bi, i: (bi * nt + i, 0)
    halo_prev = lambda bi, i: (bi, jnp.maximum(i * hb - 1, 0), 0)
    halo_next = lambda bi, i: (bi, jnp.minimum((i + 1) * hb, l // POOL_HALO - 1), 0)
    slot_major = lambda bi, i: (0, bi * nt + i)
    out_shape = (
        jax.ShapeDtypeStruct((b, l, d), F32),
        jax.ShapeDtypeStruct((n, ROW_WORDS), U32),
        jax.ShapeDtypeStruct((TOP_K, n), jnp.int32),
        jax.ShapeDtypeStruct((n, TOP_K), F32),
        jax.ShapeDtypeStruct((TOP_K, n), jnp.int32),
        jax.ShapeDtypeStruct((N_EXPERTS, 1), F32),
    )
    return pl.pallas_call(
        functools.partial(_merge_kernel, seq_len=l),
        grid=(b, nt),
        in_specs=[pl.BlockSpec((1, tm, d), row),
                  pl.BlockSpec((1, tm, d), row),
                  pl.BlockSpec((1, POOL_HALO, d), halo_prev),
                  pl.BlockSpec((1, POOL_HALO, d), halo_next),
                  pl.BlockSpec((1, tm, d), row),
                  pl.BlockSpec((1, tm, d), row),
                  pl.BlockSpec((1, tm, d), row),
                  pl.BlockSpec((1, 6, d), lambda bi, i: (bi, 0, 0)),
                  _const_spec(w_pool_bf.shape),
                  _const_spec((1, d)),
                  _const_spec((d, d)), _const_spec((d, d)), _const_spec((d, d)),
                  _const_spec((1, d)),
                  _const_spec((d, N_EXPERTS)),
                  _const_spec((1, N_EXPERTS))],
        out_specs=(pl.BlockSpec((1, tm, d), row),
                   pl.BlockSpec((tm, ROW_WORDS), flat),
                   pl.BlockSpec((TOP_K, tm), slot_major),
                   pl.BlockSpec((tm, TOP_K), flat),
                   pl.BlockSpec((TOP_K, tm), slot_major),
                   _const_spec((N_EXPERTS, 1))),
        out_shape=out_shape,
        scratch_shapes=[pltpu.VMEM((N_EXPERTS, 1), F32)],
        compiler_params=_params("arbitrary", "arbitrary"),
        name="merge_route",
    )(attn_o, u, u, u, ga, gp, x, mod, w_pool_bf, pool_scale, wab, wpb, wout,
      norm2_g, w_router, b_router)


SC_CORES = 2
SC_SUBCORES = 16
SC_WORKERS = SC_CORES * SC_SUBCORES
SC_WINDOW = 128


def _sc_worker_id():
    return lax.axis_index("s") * SC_CORES + lax.axis_index("c")


def _sc_scatter_rows(rows, dest, pad_dest, out_rows):
    n, w = rows.shape
    n_win = n // (SC_WORKERS * SC_WINDOW)
    n_pad_win = pad_dest.shape[1]
    assert dest.shape == (TOP_K, n) and SC_WORKERS * n_win * SC_WINDOW == n
    mesh = plsc.VectorSubcoreMesh(core_axis_name="c", subcore_axis_name="s")
    zeros = jnp.zeros((SC_WINDOW, w), rows.dtype)

    @functools.partial(
        pl.kernel, mesh=mesh,
        out_type=jax.ShapeDtypeStruct((out_rows, w), rows.dtype),
        scratch_types=[pltpu.VMEM((TOP_K, SC_WINDOW), jnp.int32),
                       pltpu.VMEM((n_pad_win, SC_WINDOW), jnp.int32),
                       pltpu.VMEM((SC_WINDOW, w), rows.dtype),
                       pltpu.SemaphoreType.DMA],
        name="sc_scatter_rows",
    )
    def scatter(rows_hbm, dest_hbm, pad_hbm, zeros_hbm, out_hbm, idx_v, pad_v, rows_v, sem):
        wid = _sc_worker_id()

        @pl.loop(0, n_win)
        def _(j):
            base = (wid * n_win + j) * SC_WINDOW
            pltpu.sync_copy(rows_hbm.at[pl.ds(base, SC_WINDOW)], rows_v)
            pltpu.sync_copy(dest_hbm.at[:, pl.ds(base, SC_WINDOW)], idx_v)
            copies = [pltpu.async_copy(rows_v, out_hbm.at[idx_v.at[k]], sem) for k in range(TOP_K)]
            for cp in copies:
                cp.wait()

        pltpu.sync_copy(zeros_hbm, rows_v)
        pltpu.sync_copy(pad_hbm.at[wid], pad_v)
        copies = [pltpu.async_copy(rows_v, out_hbm.at[pad_v.at[j]], sem) for j in range(n_pad_win)]
        for cp in copies:
            cp.wait()

    return scatter(rows, dest, pad_dest, zeros)


def _expert_kernel(be_ref, first_ref, next_ref, nu_ref, x_ref, wgu_hbm, bgu_ref, wd_hbm, bd_ref,
                   y_ref, wgu_f, wd_f, wgu_s, wd_s, sem):
    step = pl.program_id(0)

    def weight_copies(e):
        return (pltpu.make_async_copy(wgu_hbm.at[e], wgu_f, sem.at[0]),
                pltpu.make_async_copy(wd_hbm.at[e], wd_f, sem.at[1]))

    def start_weights(e):
        for cp in weight_copies(e):
            cp.start(priority=1)

    @pl.when(step == 0)
    def _():
        start_weights(be_ref[0])

    for sub in range(EXPERT_BLOCKS_PER_STEP):
        _expert_block(step * EXPERT_BLOCKS_PER_STEP + sub,
                      slice(sub * TM_EXPERT, (sub + 1) * TM_EXPERT),
                      be_ref, first_ref, next_ref, nu_ref, x_ref, bgu_ref, bd_ref, y_ref,
                      wgu_f, wd_f, wgu_s, wd_s, weight_copies, start_weights)


def _expert_block(i, rows, be_ref, first_ref, next_ref, nu_ref, x_ref, bgu_ref, bd_ref, y_ref,
                  wgu_f, wd_f, wgu_s, wd_s, weight_copies, start_weights):
    @pl.when(i < nu_ref[0])
    def _():
        e = be_ref[i]

        @pl.when(first_ref[i] == 1)
        def _():
            for cp in weight_copies(e):
                cp.wait()
            ck = 256
            for r0 in range(0, D_MODEL, ck):
                wgu_s[r0:r0 + ck, :] = wgu_f[r0:r0 + ck, :].astype(BF16)
                wd_s[r0:r0 + ck, :] = wd_f[r0:r0 + ck, :].astype(BF16)

            @pl.when(next_ref[i] >= 0)
            def _():
                start_weights(next_ref[i])

        x = _unpack_rows(x_ref[rows, :]).astype(BF16)
        bgu = bgu_ref[e]
        fc = EXPERT_FF_CHUNK

        def up(c0):
            gt = jnp.dot(x, wgu_s[:, c0:c0 + fc], preferred_element_type=F32)
            ln = jnp.dot(x, wgu_s[:, D_FF + c0:D_FF + c0 + fc], preferred_element_type=F32)
            return gt, ln

        def down(c0, gt, ln):
            gt = jnp.minimum(gt + bgu[:, c0:c0 + fc], SWIGLU_LIMIT)
            ln = jnp.clip(ln + bgu[:, D_FF + c0:D_FF + c0 + fc], -SWIGLU_LIMIT, SWIGLU_LIMIT)
            act = gt * jax.nn.sigmoid(SWIGLU_ALPHA * gt) * (ln + 1.0)
            return jnp.dot(act.astype(BF16), wd_s[c0:c0 + fc, :], preferred_element_type=F32)

        chunks = list(range(0, D_FF, fc))
        pending = [up(chunks[0])]
        acc = jnp.zeros((TM_EXPERT, D_MODEL), F32)
        for ci, c0 in enumerate(chunks):
            if ci + 1 < len(chunks):
                pending.append(up(chunks[ci + 1]))
            acc = acc + down(c0, *pending.pop(0))
        y_ref[rows, :] = _pack_rows(acc + bd_ref[e])

    @pl.when(i >= nu_ref[0])
    def _():
        y_ref[rows, :] = jnp.zeros((TM_EXPERT, ROW_WORDS), U32)


def _experts(block_e, first, next_e, n_used, xs, w_gu, b_gu, w_down, b_down):
    d = D_MODEL
    nb = block_e.shape[0]
    per_step = EXPERT_BLOCKS_PER_STEP
    tm = TM_EXPERT * per_step
    assert nb % per_step == 0

    xmap = lambda i, be, fi, nx, nu: (jnp.minimum(i, (nu[0] - 1) // per_step), 0)
    grid_spec = pltpu.PrefetchScalarGridSpec(
        num_scalar_prefetch=4,
        grid=(nb // per_step,),
        in_specs=[pl.BlockSpec((tm, ROW_WORDS), xmap),
                  pl.BlockSpec(memory_space=pl.ANY),
                  _const_spec((N_EXPERTS, 1, 2 * D_FF)),
                  pl.BlockSpec(memory_space=pl.ANY),
                  _const_spec((N_EXPERTS, 1, d))],
        out_specs=pl.BlockSpec((tm, ROW_WORDS), lambda i, be, fi, nx, nu: (i, 0)),
        scratch_shapes=[pltpu.VMEM((d, 2 * D_FF), F32), pltpu.VMEM((D_FF, d), F32),
                        pltpu.VMEM((d, 2 * D_FF), BF16), pltpu.VMEM((D_FF, d), BF16),
                        pltpu.SemaphoreType.DMA((2,))],
    )
    return pl.pallas_call(
        _expert_kernel,
        grid_spec=grid_spec,
        out_shape=jax.ShapeDtypeStruct((nb * TM_EXPERT, ROW_WORDS), U32),
        compiler_params=_params("arbitrary"),
        name="moe_experts",
    )(block_e, first, next_e, n_used, xs, w_gu, b_gu.reshape(N_EXPERTS, 1, 2 * D_FF),
      w_down, b_down.reshape(N_EXPERTS, 1, d))


def _sc_gather_rows(table, idx):
    m = idx.shape[0]
    w = table.shape[1]
    per_worker = m // SC_WORKERS
    n_win = per_worker // SC_WINDOW
    assert per_worker * SC_WORKERS == m and n_win * SC_WINDOW == per_worker
    mesh = plsc.VectorSubcoreMesh(core_axis_name="c", subcore_axis_name="s")

    @functools.partial(
        pl.kernel, mesh=mesh,
        out_type=jax.ShapeDtypeStruct((m, w), table.dtype),
        scratch_types=[pltpu.VMEM((SC_WINDOW,), jnp.int32),
                       pltpu.VMEM((SC_WINDOW, w), table.dtype),
                       pltpu.SemaphoreType.DMA],
        name="sc_gather_rows",
    )
    def gather(table_hbm, idx_hbm, out_hbm, idx_v, rows_v, sem):
        wid = _sc_worker_id()

        @pl.loop(0, n_win)
        def _(j):
            base = wid * per_worker + j * SC_WINDOW
            pltpu.sync_copy(idx_hbm.at[pl.ds(base, SC_WINDOW)], idx_v)
            pltpu.async_copy(table_hbm.at[idx_v], rows_v, sem).wait()
            pltpu.sync_copy(rows_v, out_hbm.at[pl.ds(base, SC_WINDOW)])

    return gather(table, idx)


def _combine_dense_kernel(y4_ref, gate_ref, x1_ref, mod_ref, fg_ref, o_ref):
    gate = gate_ref[...]
    y = gate[:, 0:1] * _unpack_rows(y4_ref[0])
    for k in range(1, TOP_K):
        y = y + gate[:, k:k + 1] * _unpack_rows(y4_ref[k])
    x2 = x1_ref[...] + mod_ref[0, 5:6, :] * y
    o_ref[...] = x2 * lax.rsqrt(jnp.mean(x2 * x2, axis=-1, keepdims=True) + NORM_EPS) * fg_ref[...]


def _combine_dense(y4, gate_w, x1, mod, final_g):
    b, l, d = x1.shape
    ts = 512
    nt = l // ts
    return pl.pallas_call(
        _combine_dense_kernel,
        grid=(b * nt,),
        in_specs=[pl.BlockSpec((TOP_K, ts, ROW_WORDS), lambda s: (0, s, 0)),
                  pl.BlockSpec((ts, TOP_K), lambda s: (s, 0)),
                  pl.BlockSpec((ts, d), lambda s: (s, 0)),
                  pl.BlockSpec((1, 6, d), lambda s: (s // nt, 0, 0)),
                  _const_spec((1, d))],
        out_specs=pl.BlockSpec((ts, d), lambda s: (s, 0)),
        out_shape=jax.ShapeDtypeStruct((b * l, d), F32),
        compiler_params=_params("arbitrary"),
        name="moe_combine",
    )(y4, gate_w, x1.reshape(b * l, d), mod, final_g).reshape(b, l, d)


def _rope_tables(seq_len):
    inv_freq = ROPE_BASE ** (-jnp.arange(ROPE_PAIRS, dtype=F32) / ROPE_PAIRS)
    reps = LANES // HEAD_DIM
    out = []
    for n_pos, on_row in ((seq_len // GRID_W, True), (GRID_W, False)):
        ang = jnp.arange(n_pos, dtype=F32)[:, None] * inv_freq
        zero = jnp.zeros_like(ang)
        for fn, sign in ((jnp.cos, 1.0), (jnp.sin, -1.0)):
            v = fn(ang)
            head = ([sign * v, v, zero, zero] if on_row else [zero, zero, sign * v, v])
            out.append(jnp.concatenate(head * reps, axis=1))
    return out


def kernel(x, c, ctx, c_ctx, w_ada, b_ada, norm1_g, norm2_g, w_in, b_in, attn_sink, w_pool,
           pool_scale, w_attn_br, w_pool_br, w_out, w_router, b_router, w_gu, b_gu, w_down,
           b_down, final_g):
    b, l, d = x.shape
    n = b * l
    assert w_ada.shape[0] == 1, "single-layer block"

    cond = jnp.zeros((SUBLANES, d), F32).at[:b].set(c).at[b].set(c_ctx)
    mod = _adaln(cond, w_ada[0], b_ada[0])[:b + 1].reshape(b + 1, 6, d)
    mod_x, mod_c = mod[:b], mod[b:b + 1]

    rope = _rope_tables(l)
    v0 = ATTN_W + KV_W
    w_v = w_in[0][:, v0:v0 + KV_W].reshape(d, N_KV_HEADS, HEAD_DIM)
    w_v = jnp.concatenate([w_v, jnp.zeros_like(w_v)], axis=-1).reshape(d, VEXT_W)
    b_v = b_in[0][v0:v0 + KV_W].reshape(N_KV_HEADS, HEAD_DIM)
    b_v = jnp.concatenate([b_v, jnp.ones_like(b_v)], axis=-1).reshape(VEXT_W)
    w_in_bf = jnp.concatenate([w_in[0][:, :v0], w_v, w_in[0][:, v0 + KV_W:]], axis=1).astype(BF16)
    b_in2 = jnp.concatenate([b_in[0][:v0], b_v, b_in[0][v0 + KV_W:]]).reshape(1, IN_W_EXT)
    g1 = norm1_g[0].reshape(1, d)
    q, kt, v, u, ga, gp = _inproj(x, mod_x, g1, w_in_bf, b_in2, rope)
    kv_sl = slice(ATTN_W, ATTN_W + KV_W + VEXT_W)
    kxt, vx = _ctx_kv(ctx, mod_c, g1, w_in_bf[:, kv_sl], b_in2[:, kv_sl])

    attn_o = _attention(q, kt, v, kxt, vx, attn_sink[0])

    x1, h2, top_idx, gate_w, rank, counts = _merge_route(
        attn_o, u, ga, gp, x, mod_x, w_pool[0].astype(BF16), pool_scale[0].reshape(1, d),
        w_attn_br[0].astype(BF16), w_pool_br[0].astype(BF16), w_out[0].astype(BF16),
        norm2_g[0].reshape(1, d), w_router[0].astype(BF16), b_router[0].reshape(1, N_EXPERTS))

    tm = TM_EXPERT
    nb = n * TOP_K // tm + N_EXPERTS
    cnt = counts[:, 0].astype(jnp.int32)
    padded = (cnt + tm - 1) // tm * tm
    pend = jnp.cumsum(padded)
    pstart = pend - padded
    expert_ids = jnp.arange(N_EXPERTS, dtype=jnp.int32)
    dest = jnp.sum(jnp.where(top_idx[..., None] == expert_ids, pstart, 0), axis=-1) + rank
    n_used = (pend[-1] // tm).reshape(1)
    block_start = jnp.arange(nb, dtype=jnp.int32) * tm
    block_e = jnp.minimum(jnp.sum((pend[None, :] <= block_start[:, None]).astype(jnp.int32), axis=1),
                          N_EXPERTS - 1)
    first = jnp.concatenate([jnp.ones((1,), jnp.int32),
                             (block_e[1:] != block_e[:-1]).astype(jnp.int32)])
    later_used = (expert_ids[None, :] > expert_ids[:, None]) & (cnt[None, :] > 0)
    next_of_expert = jnp.min(jnp.where(later_used, expert_ids[None, :], N_EXPERTS), axis=1)
    next_of_expert = jnp.where(next_of_expert < N_EXPERTS, next_of_expert, -1)
    next_e = jnp.sum(jnp.where(block_e[:, None] == expert_ids, next_of_expert, 0), axis=-1)

    pad_j = jnp.arange(tm, dtype=jnp.int32)[None, :]
    spare = nb * tm + expert_ids[:, None] * tm + pad_j
    pad_dest = jnp.where(pad_j < (padded - cnt)[:, None], (pstart + cnt)[:, None] + pad_j, spare)
    pad_dest = pad_dest.reshape(SC_WORKERS, N_EXPERTS * tm // (SC_WORKERS * SC_WINDOW), SC_WINDOW)

    xs = _sc_scatter_rows(h2, dest, pad_dest, (nb + N_EXPERTS) * tm)
    ys = _experts(block_e, first, next_e, n_used, xs, w_gu[0], b_gu[0], w_down[0], b_down[0])
    y4 = _sc_gather_rows(ys, dest.reshape(-1)).reshape(TOP_K, n, ROW_WORDS)
    return _combine_dense(y4, gate_w, x1, mod_x, final_g.reshape(1, d))
```

```python
import functools

import jax
import jax.numpy as jnp
from jax import lax
from jax.experimental import pallas as pl
from jax.experimental.pallas import tpu as pltpu
from jax.experimental.pallas import tpu_sc as plsc

D_MODEL = 1024
GRID_W = 64
HEAD_DIM = 64
N_HEADS = 16
N_KV_HEADS = 4
GROUP = N_HEADS // N_KV_HEADS
ATTN_W = N_HEADS * HEAD_DIM
KV_W = N_KV_HEADS * HEAD_DIM
WINDOW = 128
ATTN_SCALE = HEAD_DIM ** -0.5
ROPE_BASE = 10000.0
ROPE_PAIRS = HEAD_DIM // 4
POOL_WINDOWS = (2, 4, 8, 16)
POOL_GROUP_W = D_MODEL // len(POOL_WINDOWS)
IN_W = ATTN_W + 2 * KV_W + D_MODEL + 2 * D_MODEL
VEXT_W = N_KV_HEADS * 2 * HEAD_DIM
IN_W_EXT = IN_W - KV_W + VEXT_W
N_EXPERTS = 32
TOP_K = 4
D_FF = D_MODEL
SWIGLU_ALPHA = 1.702
SWIGLU_LIMIT = 7.0
NORM_EPS = 1e-5
NEG_INF = -1e30

LANES = 128
SUBLANES = 8
VMEM_LIMIT = 56 * 1024 * 1024

TM_INPROJ = 512
TQ = WINDOW
TM_MERGE = 512
MERGE_SUB = 256
TM_EXPERT = 512
EXPERT_FF_CHUNK = 1024
EXPERT_BLOCKS_PER_STEP = 1
POOL_HALO = SUBLANES

F32 = jnp.float32
BF16 = jnp.bfloat16


ROW_WORDS = D_MODEL // 2
U32 = jnp.uint32


def _pack_rows(value):
    bits = lax.bitcast_convert_type(value.astype(BF16).astype(F32), U32)
    return bits[:, :ROW_WORDS] | (bits[:, ROW_WORDS:] >> 16)


def _unpack_rows(words):
    hi = lax.bitcast_convert_type(words & jnp.uint32(0xFFFF0000), F32)
    lo = lax.bitcast_convert_type(words << 16, F32)
    return jnp.concatenate([hi, lo], axis=1)


def _params(*sem):
    return pltpu.CompilerParams(dimension_semantics=sem, vmem_limit_bytes=VMEM_LIMIT)


def _const_spec(shape):
    nd = len(shape)
    return pl.BlockSpec(shape, lambda *_: (0,) * nd)


def _adaln_kernel(c_ref, w_ref, b_ref, o_ref):
    c = c_ref[...]
    s = c * jax.nn.sigmoid(c)
    o_ref[...] = jnp.dot(s.astype(BF16), w_ref[...].astype(BF16),
                         preferred_element_type=F32) + b_ref[...]


def _adaln(cond, w_ada, b_ada):
    rows, d = cond.shape
    n = w_ada.shape[1]
    tn = 1024
    return pl.pallas_call(
        _adaln_kernel,
        grid=(n // tn,),
        in_specs=[_const_spec((rows, d)),
                  pl.BlockSpec((d, tn), lambda j: (0, j)),
                  pl.BlockSpec((1, tn), lambda j: (0, j))],
        out_specs=pl.BlockSpec((rows, tn), lambda j: (0, j)),
        out_shape=jax.ShapeDtypeStruct((rows, n), F32),
        compiler_params=_params("arbitrary"),
        name="adaln",
    )(cond, w_ada, b_ada.reshape(1, n))


def _norm_mod(x, g, shift, scale):
    y = x * lax.rsqrt(jnp.mean(x * x, axis=-1, keepdims=True) + NORM_EPS) * g
    return y * (1.0 + scale) + shift


def _rope(t, cos, sin_signed):
    w = t.shape[1]
    half = ROPE_PAIRS
    lane = lax.broadcasted_iota(jnp.int32, t.shape, 1)
    first = (lane & (2 * half - 1)) < half
    rot = jnp.where(first, pltpu.roll(t, w - half, 1), pltpu.roll(t, half, 1))
    reps = w // LANES
    cos_w = jnp.concatenate([cos] * reps, axis=1) if reps > 1 else cos
    sin_w = jnp.concatenate([sin_signed] * reps, axis=1) if reps > 1 else sin_signed
    return t * cos_w + rot * sin_w


def _token_table(row_ref, col_ref):
    gr = row_ref.shape[0]
    shape = (gr, GRID_W, LANES)
    full = (jnp.broadcast_to(row_ref[...][:, None, :], shape)
            + jnp.broadcast_to(col_ref[...][None, :, :], shape))
    return full.reshape(gr * GRID_W, LANES)


def _inproj_kernel(x_ref, mod_ref, g_ref, w_ref, b_ref, cr_ref, sr_ref, cc_ref, sc_ref,
                   q_ref, k_ref, v_ref, u_ref, ga_ref, gp_ref):
    h = _norm_mod(x_ref[0], g_ref[...], mod_ref[0, 0:1, :], mod_ref[0, 1:2, :])
    hb = h.astype(BF16)
    cos = _token_table(cr_ref, cc_ref)
    sin = _token_table(sr_ref, sc_ref)

    def proj(c0, c1):
        return jnp.dot(hb, w_ref[:, c0:c1], preferred_element_type=F32) + b_ref[:, c0:c1]

    cw = 512
    for c0 in range(0, ATTN_W, cw):
        q_ref[0, :, c0:c0 + cw] = (_rope(proj(c0, c0 + cw), cos, sin) * ATTN_SCALE).astype(BF16)
    k_ref[0] = _rope(proj(ATTN_W, ATTN_W + KV_W), cos, sin).T.astype(BF16)
    v_ref[0] = proj(ATTN_W + KV_W, ATTN_W + KV_W + VEXT_W).astype(BF16)
    base = ATTN_W + KV_W + VEXT_W
    for c0 in range(0, D_MODEL, cw):
        u_ref[0, :, c0:c0 + cw] = proj(base + c0, base + c0 + cw)
    for ref in (ga_ref, gp_ref):
        base += D_MODEL
        for c0 in range(0, D_MODEL, cw):
            ref[0, :, c0:c0 + cw] = jax.nn.sigmoid(proj(base + c0, base + c0 + cw)).astype(BF16)


def _inproj(x, mod, norm_g, w_in_bf, b_in, rope):
    b, l, d = x.shape
    tm = TM_INPROJ
    row = lambda bi, i: (bi, i, 0)
    out_shape = (
        jax.ShapeDtypeStruct((b, l, ATTN_W), BF16),
        jax.ShapeDtypeStruct((b, KV_W, l), BF16),
        jax.ShapeDtypeStruct((b, l, VEXT_W), BF16),
        jax.ShapeDtypeStruct((b, l, D_MODEL), F32),
        jax.ShapeDtypeStruct((b, l, D_MODEL), BF16),
        jax.ShapeDtypeStruct((b, l, D_MODEL), BF16),
    )
    return pl.pallas_call(
        _inproj_kernel,
        grid=(b, l // tm),
        in_specs=[pl.BlockSpec((1, tm, d), row),
                  pl.BlockSpec((1, 6, d), lambda bi, i: (bi, 0, 0)),
                  _const_spec((1, d)),
                  _const_spec((d, IN_W_EXT)),
                  _const_spec((1, IN_W_EXT)),
                  pl.BlockSpec((tm // GRID_W, LANES), lambda bi, i: (i, 0)),
                  pl.BlockSpec((tm // GRID_W, LANES), lambda bi, i: (i, 0)),
                  _const_spec((GRID_W, LANES)),
                  _const_spec((GRID_W, LANES))],
        out_specs=(pl.BlockSpec((1, tm, ATTN_W), row),
                   pl.BlockSpec((1, KV_W, tm), lambda bi, i: (bi, 0, i)),
                   pl.BlockSpec((1, tm, VEXT_W), row),
                   pl.BlockSpec((1, tm, D_MODEL), row),
                   pl.BlockSpec((1, tm, D_MODEL), row),
                   pl.BlockSpec((1, tm, D_MODEL), row)),
        out_shape=out_shape,
        compiler_params=_params("arbitrary", "arbitrary"),
        name="inproj",
    )(x, mod, norm_g, w_in_bf, b_in, *rope)


def _ctx_kv_kernel(x_ref, mod_ref, g_ref, w_ref, b_ref, k_ref, v_ref):
    h = _norm_mod(x_ref[0], g_ref[...], mod_ref[0, 0:1, :], mod_ref[0, 1:2, :])
    kv = jnp.dot(h.astype(BF16), w_ref[...], preferred_element_type=F32) + b_ref[...]
    k_ref[0] = kv[:, :KV_W].T.astype(BF16)
    v_ref[0] = kv[:, KV_W:].astype(BF16)


def _ctx_kv(ctx, mod_c, norm_g, w_kv_bf, b_kv):
    b, lc, d = ctx.shape
    row = lambda bi: (bi, 0, 0)
    return pl.pallas_call(
        _ctx_kv_kernel,
        grid=(b,),
        in_specs=[pl.BlockSpec((1, lc, d), row),
                  _const_spec((1, 6, d)),
                  _const_spec((1, d)),
                  _const_spec((d, KV_W + VEXT_W)),
                  _const_spec((1, KV_W + VEXT_W))],
        out_specs=(pl.BlockSpec((1, KV_W, lc), row), pl.BlockSpec((1, lc, VEXT_W), row)),
        out_shape=(jax.ShapeDtypeStruct((b, KV_W, lc), BF16),
                   jax.ShapeDtypeStruct((b, lc, VEXT_W), BF16)),
        compiler_params=_params("arbitrary"),
        name="ctx_kv",
    )(ctx, mod_c, norm_g, w_kv_bf, b_kv)


def _attn_kernel(sink_ref, q_ref, kp_ref, kc_ref, kn_ref, vp_ref, vc_ref, vn_ref,
                 kx_ref, vx_ref, o_ref):
    n = pl.program_id(1)
    last = pl.num_programs(1) - 1
    rows = GROUP * TQ
    pair_w = 2 * HEAD_DIM
    qi = lax.broadcasted_iota(jnp.int32, (rows, TQ), 0) & (TQ - 1)
    col = lax.broadcasted_iota(jnp.int32, (rows, TQ), 1)
    keep_prev = (col >= qi) & (n > 0)
    keep_next = (col <= qi) & (n < last)
    low_half = lax.broadcasted_iota(jnp.int32, (TQ, pair_w), 1) < HEAD_DIM
    row_id = lax.broadcasted_iota(jnp.int32, (rows, 1), 0)

    def scores(j):
        parts = []
        for g in range(GROUP):
            h = j * GROUP + g
            pair = q_ref[0, :, (h // 2) * pair_w:(h // 2 + 1) * pair_w]
            parts.append(jnp.where(low_half if h % 2 == 0 else ~low_half, pair, jnp.zeros_like(pair)))
        q4 = jnp.concatenate(parts, axis=0)
        ks = slice(j * HEAD_DIM, (j + 1) * HEAD_DIM)
        kt = jnp.concatenate([kp_ref[0, ks, :], kc_ref[0, ks, :], kn_ref[0, ks, :]], axis=1)
        s_loc = jnp.dot(q4, jnp.concatenate([kt, kt], axis=0), preferred_element_type=F32)
        kxt = kx_ref[0, ks, :]
        s_ctx = jnp.dot(q4, jnp.concatenate([kxt, kxt], axis=0), preferred_element_type=F32)
        return s_loc, s_ctx

    def probs(j, s_loc, s_ctx):
        pieces = [jnp.where(keep_prev, s_loc[:, :TQ], NEG_INF), s_loc[:, TQ:2 * TQ],
                  jnp.where(keep_next, s_loc[:, 2 * TQ:], NEG_INF)]
        pieces += [s_ctx[:, c0:c0 + TQ] for c0 in range(0, s_ctx.shape[1], TQ)]
        sink = jnp.zeros((rows, 1), F32)
        for g in range(GROUP):
            sink = jnp.where(row_id // TQ == g, sink_ref[j * GROUP + g], sink)
        mx = pieces[0]
        for pc in pieces[1:]:
            mx = jnp.maximum(mx, pc)
        m = jnp.maximum(jnp.max(mx, axis=-1, keepdims=True), sink)
        p = jnp.concatenate([jnp.exp(pc - m).astype(BF16) for pc in pieces], axis=1)
        return p, jnp.exp(sink - m)

    def output(j, p, sink_p):
        vs = slice(j * pair_w, (j + 1) * pair_w)
        v_all = jnp.concatenate([vp_ref[0, :, vs], vc_ref[0, :, vs], vn_ref[0, :, vs],
                                 vx_ref[0, :, vs]], axis=0)
        r = jnp.dot(p, v_all, preferred_element_type=F32)
        den = pltpu.roll(r, HEAD_DIM, 1) + sink_p
        o = r / den
        for g in range(0, GROUP, 2):
            even = o[g * TQ:(g + 1) * TQ]
            odd = pltpu.roll(o[(g + 1) * TQ:(g + 2) * TQ], HEAD_DIM, 1)
            c0 = (j * GROUP + g) * HEAD_DIM
            o_ref[0, :, c0:c0 + pair_w] = jnp.where(low_half, even, odd).astype(BF16)

    s_queue = [scores(0), scores(1)]
    p_queue = [probs(0, *s_queue.pop(0))]
    for j in range(N_KV_HEADS):
        if j + 2 < N_KV_HEADS:
            s_queue.append(scores(j + 2))
        ready = p_queue.pop(0)
        if j + 1 < N_KV_HEADS:
            p_queue.append(probs(j + 1, *s_queue.pop(0)))
        output(j, *ready)


def _attention(q, kt, v, kxt, vx, sink):
    b, l, _ = q.shape
    lc = kxt.shape[2]
    nblk = l // TQ
    cur = lambda bi, n: (bi, n, 0)
    prev = lambda bi, n: (bi, jnp.maximum(n - 1, 0), 0)
    nxt = lambda bi, n: (bi, jnp.minimum(n + 1, nblk - 1), 0)
    kcur = lambda bi, n: (bi, 0, n)
    kprev = lambda bi, n: (bi, 0, jnp.maximum(n - 1, 0))
    knxt = lambda bi, n: (bi, 0, jnp.minimum(n + 1, nblk - 1))
    kb = (1, KV_W, TQ)
    vb = (1, TQ, VEXT_W)
    return pl.pallas_call(
        _attn_kernel,
        grid=(b, nblk),
        in_specs=[pl.BlockSpec(memory_space=pltpu.SMEM),
                  pl.BlockSpec((1, TQ, ATTN_W), cur),
                  pl.BlockSpec(kb, kprev), pl.BlockSpec(kb, kcur), pl.BlockSpec(kb, knxt),
                  pl.BlockSpec(vb, prev), pl.BlockSpec(vb, cur), pl.BlockSpec(vb, nxt),
                  pl.BlockSpec((1, KV_W, lc), lambda bi, n: (bi, 0, 0)),
                  pl.BlockSpec((1, lc, VEXT_W), lambda bi, n: (bi, 0, 0))],
        out_specs=pl.BlockSpec((1, TQ, ATTN_W), cur),
        out_shape=jax.ShapeDtypeStruct((b, l, ATTN_W), BF16),
        compiler_params=_params("arbitrary", "arbitrary"),
        name="attention",
    )(sink, q, kt, kt, kt, v, v, v, kxt, vx)


def _shift_rows(a, d):
    n = a.shape[0]
    return pltpu.roll(a, (-d) % n, 0)


def _merge_kernel(ao_ref, u_ref, up_ref, un_ref, sa_ref, sp_ref, x_ref, mod_ref,
                  wpool_ref, pscale_ref, wab_ref, wpb_ref, wout_ref, g2_ref, wr_ref, br_ref,
                  x1_ref, h2_ref, idx_ref, gate_ref, rank_ref, cnt_ref, carry_ref, *, seq_len):
    bi = pl.program_id(0)
    i = pl.program_id(1)
    last = pl.num_programs(1) - 1
    tm, sub = TM_MERGE, MERGE_SUB

    @pl.when((bi == 0) & (i == 0))
    def _():
        carry_ref[...] = jnp.zeros_like(carry_ref)

    u = u_ref[0]
    prev = jnp.where(i > 0, up_ref[0], 0.0)
    nxt = jnp.where(i < last, un_ref[0], 0.0)
    ext = jnp.concatenate([prev, u, nxt], axis=0)

    def pool_diff(r0):
        t = i * tm + r0 + lax.broadcasted_iota(jnp.int32, (sub, 1), 0)
        diffs = []
        for g, w in enumerate(POOL_WINDOWS):
            cs = slice(g * POOL_GROUP_W, (g + 1) * POOL_GROUP_W)
            e = ext[r0:r0 + sub + 2 * POOL_HALO, cs]
            acc = _shift_rows(e, -1) + e
            step = 1
            while 2 * step < w:
                acc = _shift_rows(acc, -step) + _shift_rows(acc, step)
                step *= 2
            win = acc[POOL_HALO:POOL_HALO + sub]
            half = w // 2
            cnt = (jnp.minimum(t + half, seq_len) - jnp.maximum(t - half, 0)).astype(F32)
            diffs.append((win / cnt - u[r0:r0 + sub, cs]).astype(BF16))
        return diffs

    def branches(r0, diffs):
        rs = slice(r0, r0 + sub)
        a = jnp.dot(ao_ref[0, rs, :], wab_ref[...], preferred_element_type=F32)
        mixed = [jnp.dot(df, wpool_ref[g], preferred_element_type=F32) for g, df in enumerate(diffs)]
        pool_o = jnp.concatenate(mixed, axis=1) * pscale_ref[...]
        p = jnp.dot(pool_o.astype(BF16), wpb_ref[...], preferred_element_type=F32)
        merged = sa_ref[0, rs, :].astype(F32) * a + sp_ref[0, rs, :].astype(F32) * p
        return merged.astype(BF16)

    def project(r0, merged):
        rs = slice(r0, r0 + sub)
        o = jnp.dot(merged, wout_ref[...], preferred_element_type=F32)
        x1 = x_ref[0, rs, :] + mod_ref[0, 2:3, :] * o
        x1_ref[0, rs, :] = x1
        h2 = _norm_mod(x1, g2_ref[...], mod_ref[0, 3:4, :], mod_ref[0, 4:5, :])
        h2_ref[rs, :] = _pack_rows(h2)
        return jnp.dot(h2.astype(BF16), wr_ref[...], preferred_element_type=F32) + br_ref[...]

    def route(r0, logits, carry):
        lt = logits.T
        eid = lax.broadcasted_iota(jnp.int32, (N_EXPERTS, sub), 0).astype(F32)
        work = lt
        vals, idxs, sels = [], [], []
        for _ in range(TOP_K):
            mx = jnp.max(work, axis=0, keepdims=True)
            ix = jnp.min(jnp.where(work == mx, eid, float(N_EXPERTS)), axis=0, keepdims=True)
            sel = eid == ix
            work = jnp.where(sel, -jnp.inf, work)
            vals.append(mx)
            idxs.append(ix)
            sels.append(sel)
        ex = [jnp.exp(v - vals[0]) for v in vals]
        tot = ex[0] + ex[1] + ex[2] + ex[3]
        pad = [jnp.zeros_like(tot)] * (SUBLANES - TOP_K)
        gates = jnp.concatenate([e_ / tot for e_ in ex] + pad, axis=0)
        gate_ref[r0:r0 + sub, :] = gates.T[:, :TOP_K]
        idx_ref[:, r0:r0 + sub] = jnp.concatenate(idxs, axis=0).astype(jnp.int32)
        member = (sels[0] | sels[1] | sels[2] | sels[3]).astype(F32)
        t_row = lax.broadcasted_iota(jnp.int32, (sub, sub), 0)
        t_col = lax.broadcasted_iota(jnp.int32, (sub, sub), 1)
        earlier = (t_row < t_col).astype(BF16)
        before = jnp.dot(member.astype(BF16), earlier, preferred_element_type=F32) + carry
        ranks = [jnp.sum(jnp.where(s, before, 0.0), axis=0, keepdims=True) for s in sels]
        rank_ref[:, r0:r0 + sub] = jnp.concatenate(ranks, axis=0).astype(jnp.int32)
        return carry + jnp.sum(member, axis=1, keepdims=True)

    starts = list(range(0, tm, sub))
    carry = carry_ref[...]
    merged_q = [branches(starts[0], pool_diff(starts[0]))]
    logits_q = []
    for si, r0 in enumerate(starts):
        if si + 1 < len(starts):
            diffs_next = pool_diff(starts[si + 1])
        logits_q.append(project(r0, merged_q.pop(0)))
        if si + 1 < len(starts):
            merged_q.append(branches(starts[si + 1], diffs_next))
        if si >= 1:
            carry = route(starts[si - 1], logits_q.pop(0), carry)
    carry = route(starts[-1], logits_q.pop(0), carry)
    carry_ref[...] = carry
    cnt_ref[...] = carry


def _merge_route(attn_o, u, ga, gp, x, mod, w_pool_bf, pool_scale, wab, wpb, wout,
                 norm2_g, w_router, b_router):
    b, l, d = x.shape
    tm = TM_MERGE
    nt = l // tm
    n = b * l
    hb = tm // POOL_HALO
    row = lambda bi, i: (bi, i, 0)
    flat = lambda bi, i: (bi * nt + i, 0)
    halo_prev = lambda bi, i: (bi, jnp.maximum(i * hb - 1, 0), 0)
    halo_next = lambda bi, i: (bi, jnp.minimum((i + 1) * hb, l // POOL_HALO - 1), 0)
    slot_major = lambda bi, i: (0, bi * nt + i)
    out_shape = (
        jax.ShapeDtypeStruct((b, l, d), F32),
        jax.ShapeDtypeStruct((n, ROW_WORDS), U32),
        jax.ShapeDtypeStruct((TOP_K, n), jnp.int32),
        jax.ShapeDtypeStruct((n, TOP_K), F32),
        jax.ShapeDtypeStruct((TOP_K, n), jnp.int32),
        jax.ShapeDtypeStruct((N_EXPERTS, 1), F32),
    )
    return pl.pallas_call(
        functools.partial(_merge_kernel, seq_len=l),
        grid=(b, nt),
        in_specs=[pl.BlockSpec((1, tm, d), row),
                  pl.BlockSpec((1, tm, d), row),
                  pl.BlockSpec((1, POOL_HALO, d), halo_prev),
                  pl.BlockSpec((1, POOL_HALO, d), halo_next),
                  pl.BlockSpec((1, tm, d), row),
                  pl.BlockSpec((1, tm, d), row),
                  pl.BlockSpec((1, tm, d), row),
                  pl.BlockSpec((1, 6, d), lambda bi, i: (bi, 0, 0)),
                  _const_spec(w_pool_bf.shape),
                  _const_spec((1, d)),
                  _const_spec((d, d)), _const_spec((d, d)), _const_spec((d, d)),
                  _const_spec((1, d)),
                  _const_spec((d, N_EXPERTS)),
                  _const_spec((1, N_EXPERTS))],
        out_specs=(pl.BlockSpec((1, tm, d), row),
                   pl.BlockSpec((tm, ROW_WORDS), flat),
                   pl.BlockSpec((TOP_K, tm), slot_major),
                   pl.BlockSpec((tm, TOP_K), flat),
                   pl.BlockSpec((TOP_K, tm), slot_major),
                   _const_spec((N_EXPERTS, 1))),
        out_shape=out_shape,
        scratch_shapes=[pltpu.VMEM((N_EXPERTS, 1), F32)],
        compiler_params=_params("arbitrary", "arbitrary"),
        name="merge_route",
    )(attn_o, u, u, u, ga, gp, x, mod, w_pool_bf, pool_scale, wab, wpb, wout,
      norm2_g, w_router, b_router)


SC_CORES = 2
SC_SUBCORES = 16
SC_WORKERS = SC_CORES * SC_SUBCORES
SC_WINDOW = 128


def _sc_worker_id():
    return lax.axis_index("s") * SC_CORES + lax.axis_index("c")


def _sc_scatter_rows(rows, dest, pad_dest, out_rows):
    n, w = rows.shape
    n_win = n // (SC_WORKERS * SC_WINDOW)
    n_pad_win = pad_dest.shape[1]
    assert dest.shape == (TOP_K, n) and SC_WORKERS * n_win * SC_WINDOW == n
    mesh = plsc.VectorSubcoreMesh(core_axis_name="c", subcore_axis_name="s")
    zeros = jnp.zeros((SC_WINDOW, w), rows.dtype)

    @functools.partial(
        pl.kernel, mesh=mesh,
        out_type=jax.ShapeDtypeStruct((out_rows, w), rows.dtype),
        scratch_types=[pltpu.VMEM((TOP_K, SC_WINDOW), jnp.int32),
                       pltpu.VMEM((n_pad_win, SC_WINDOW), jnp.int32),
                       pltpu.VMEM((SC_WINDOW, w), rows.dtype),
                       pltpu.SemaphoreType.DMA],
        name="sc_scatter_rows",
    )
    def scatter(rows_hbm, dest_hbm, pad_hbm, zeros_hbm, out_hbm, idx_v, pad_v, rows_v, sem):
        wid = _sc_worker_id()

        @pl.loop(0, n_win)
        def _(j):
            base = (wid * n_win + j) * SC_WINDOW
            pltpu.sync_copy(rows_hbm.at[pl.ds(base, SC_WINDOW)], rows_v)
            pltpu.sync_copy(dest_hbm.at[:, pl.ds(base, SC_WINDOW)], idx_v)
            copies = [pltpu.async_copy(rows_v, out_hbm.at[idx_v.at[k]], sem) for k in range(TOP_K)]
            for cp in copies:
                cp.wait()

        pltpu.sync_copy(zeros_hbm, rows_v)
        pltpu.sync_copy(pad_hbm.at[wid], pad_v)
        copies = [pltpu.async_copy(rows_v, out_hbm.at[pad_v.at[j]], sem) for j in range(n_pad_win)]
        for cp in copies:
            cp.wait()

    return scatter(rows, dest, pad_dest, zeros)


def _expert_kernel(be_ref, first_ref, next_ref, nu_ref, x_ref, wgu_hbm, bgu_ref, wd_hbm, bd_ref,
                   y_ref, wgu_f, wd_f, wgu_s, wd_s, sem):
    step = pl.program_id(0)

    def weight_copies(e):
        return (pltpu.make_async_copy(wgu_hbm.at[e], wgu_f, sem.at[0]),
                pltpu.make_async_copy(wd_hbm.at[e], wd_f, sem.at[1]))

    def start_weights(e):
        for cp in weight_copies(e):
            cp.start(priority=1)

    @pl.when(step == 0)
    def _():
        start_weights(be_ref[0])

    for sub in range(EXPERT_BLOCKS_PER_STEP):
        _expert_block(step * EXPERT_BLOCKS_PER_STEP + sub,
                      slice(sub * TM_EXPERT, (sub + 1) * TM_EXPERT),
                      be_ref, first_ref, next_ref, nu_ref, x_ref, bgu_ref, bd_ref, y_ref,
                      wgu_f, wd_f, wgu_s, wd_s, weight_copies, start_weights)


def _expert_block(i, rows, be_ref, first_ref, next_ref, nu_ref, x_ref, bgu_ref, bd_ref, y_ref,
                  wgu_f, wd_f, wgu_s, wd_s, weight_copies, start_weights):
    @pl.when(i < nu_ref[0])
    def _():
        e = be_ref[i]

        @pl.when(first_ref[i] == 1)
        def _():
            for cp in weight_copies(e):
                cp.wait()
            ck = 256
            for r0 in range(0, D_MODEL, ck):
                wgu_s[r0:r0 + ck, :] = wgu_f[r0:r0 + ck, :].astype(BF16)
                wd_s[r0:r0 + ck, :] = wd_f[r0:r0 + ck, :].astype(BF16)

            @pl.when(next_ref[i] >= 0)
            def _():
                start_weights(next_ref[i])

        x = _unpack_rows(x_ref[rows, :]).astype(BF16)
        bgu = bgu_ref[e]
        fc = EXPERT_FF_CHUNK

        def up(c0):
            gt = jnp.dot(x, wgu_s[:, c0:c0 + fc], preferred_element_type=F32)
            ln = jnp.dot(x, wgu_s[:, D_FF + c0:D_FF + c0 + fc], preferred_element_type=F32)
            return gt, ln

        def down(c0, gt, ln):
            gt = jnp.minimum(gt + bgu[:, c0:c0 + fc], SWIGLU_LIMIT)
            ln = jnp.clip(ln + bgu[:, D_FF + c0:D_FF + c0 + fc], -SWIGLU_LIMIT, SWIGLU_LIMIT)
            act = gt * jax.nn.sigmoid(SWIGLU_ALPHA * gt) * (ln + 1.0)
            return jnp.dot(act.astype(BF16), wd_s[c0:c0 + fc, :], preferred_element_type=F32)

        chunks = list(range(0, D_FF, fc))
        pending = [up(chunks[0])]
        acc = jnp.zeros((TM_EXPERT, D_MODEL), F32)
        for ci, c0 in enumerate(chunks):
            if ci + 1 < len(chunks):
                pending.append(up(chunks[ci + 1]))
            acc = acc + down(c0, *pending.pop(0))
        y_ref[rows, :] = _pack_rows(acc + bd_ref[e])

    @pl.when(i >= nu_ref[0])
    def _():
        y_ref[rows, :] = jnp.zeros((TM_EXPERT, ROW_WORDS), U32)


def _experts(block_e, first, next_e, n_used, xs, w_gu, b_gu, w_down, b_down):
    d = D_MODEL
    nb = block_e.shape[0]
    per_step = EXPERT_BLOCKS_PER_STEP
    tm = TM_EXPERT * per_step
    assert nb % per_step == 0

    xmap = lambda i, be, fi, nx, nu: (jnp.minimum(i, (nu[0] - 1) // per_step), 0)
    grid_spec = pltpu.PrefetchScalarGridSpec(
        num_scalar_prefetch=4,
        grid=(nb // per_step,),
        in_specs=[pl.BlockSpec((tm, ROW_WORDS), xmap),
                  pl.BlockSpec(memory_space=pl.ANY),
                  _const_spec((N_EXPERTS, 1, 2 * D_FF)),
                  pl.BlockSpec(memory_space=pl.ANY),
                  _const_spec((N_EXPERTS, 1, d))],
        out_specs=pl.BlockSpec((tm, ROW_WORDS), lambda i, be, fi, nx, nu: (i, 0)),
        scratch_shapes=[pltpu.VMEM((d, 2 * D_FF), F32), pltpu.VMEM((D_FF, d), F32),
                        pltpu.VMEM((d, 2 * D_FF), BF16), pltpu.VMEM((D_FF, d), BF16),
                        pltpu.SemaphoreType.DMA((2,))],
    )
    return pl.pallas_call(
        _expert_kernel,
        grid_spec=grid_spec,
        out_shape=jax.ShapeDtypeStruct((nb * TM_EXPERT, ROW_WORDS), U32),
        compiler_params=_params("arbitrary"),
        name="moe_experts",
    )(block_e, first, next_e, n_used, xs, w_gu, b_gu.reshape(N_EXPERTS, 1, 2 * D_FF),
      w_down, b_down.reshape(N_EXPERTS, 1, d))


def _sc_gather_rows(table, idx):
    m = idx.shape[0]
    w = table.shape[1]
    per_worker = m // SC_WORKERS
    n_win = per_worker // SC_WINDOW
    assert per_worker * SC_WORKERS == m and n_win * SC_WINDOW == per_worker
    mesh = plsc.VectorSubcoreMesh(core_axis_name="c", subcore_axis_name="s")

    @functools.partial(
        pl.kernel, mesh=mesh,
        out_type=jax.ShapeDtypeStruct((m, w), table.dtype),
        scratch_types=[pltpu.VMEM((SC_WINDOW,), jnp.int32),
                       pltpu.VMEM((SC_WINDOW, w), table.dtype),
                       pltpu.SemaphoreType.DMA],
        name="sc_gather_rows",
    )
    def gather(table_hbm, idx_hbm, out_hbm, idx_v, rows_v, sem):
        wid = _sc_worker_id()

        @pl.loop(0, n_win)
        def _(j):
            base = wid * per_worker + j * SC_WINDOW
            pltpu.sync_copy(idx_hbm.at[pl.ds(base, SC_WINDOW)], idx_v)
            pltpu.async_copy(table_hbm.at[idx_v], rows_v, sem).wait()
            pltpu.sync_copy(rows_v, out_hbm.at[pl.ds(base, SC_WINDOW)])

    return gather(table, idx)


def _combine_dense_kernel(y4_ref, gate_ref, x1_ref, mod_ref, fg_ref, o_ref):
    gate = gate_ref[...]
    y = gate[:, 0:1] * _unpack_rows(y4_ref[0])
    for k in range(1, TOP_K):
        y = y + gate[:, k:k + 1] * _unpack_rows(y4_ref[k])
    x2 = x1_ref[...] + mod_ref[0, 5:6, :] * y
    o_ref[...] = x2 * lax.rsqrt(jnp.mean(x2 * x2, axis=-1, keepdims=True) + NORM_EPS) * fg_ref[...]


def _combine_dense(y4, gate_w, x1, mod, final_g):
    b, l, d = x1.shape
    ts = 512
    nt = l // ts
    return pl.pallas_call(
        _combine_dense_kernel,
        grid=(b * nt,),
        in_specs=[pl.BlockSpec((TOP_K, ts, ROW_WORDS), lambda s: (0, s, 0)),
                  pl.BlockSpec((ts, TOP_K), lambda s: (s, 0)),
                  pl.BlockSpec((ts, d), lambda s: (s, 0)),
                  pl.BlockSpec((1, 6, d), lambda s: (s // nt, 0, 0)),
                  _const_spec((1, d))],
        out_specs=pl.BlockSpec((ts, d), lambda s: (s, 0)),
        out_shape=jax.ShapeDtypeStruct((b * l, d), F32),
        compiler_params=_params("arbitrary"),
        name="moe_combine",
    )(y4, gate_w, x1.reshape(b * l, d), mod, final_g).reshape(b, l, d)


def _rope_tables(seq_len):
    inv_freq = ROPE_BASE ** (-jnp.arange(ROPE_PAIRS, dtype=F32) / ROPE_PAIRS)
    reps = LANES // HEAD_DIM
    out = []
    for n_pos, on_row in ((seq_len // GRID_W, True), (GRID_W, False)):
        ang = jnp.arange(n_pos, dtype=F32)[:, None] * inv_freq
        zero = jnp.zeros_like(ang)
        for fn, sign in ((jnp.cos, 1.0), (jnp.sin, -1.0)):
            v = fn(ang)
            head = ([sign * v, v, zero, zero] if on_row else [zero, zero, sign * v, v])
            out.append(jnp.concatenate(head * reps, axis=1))
    return out


def kernel(x, c, ctx, c_ctx, w_ada, b_ada, norm1_g, norm2_g, w_in, b_in, attn_sink, w_pool,
           pool_scale, w_attn_br, w_pool_br, w_out, w_router, b_router, w_gu, b_gu, w_down,
           b_down, final_g):
    b, l, d = x.shape
    n = b * l
    assert w_ada.shape[0] == 1, "single-layer block"

    cond = jnp.zeros((SUBLANES, d), F32).at[:b].set(c).at[b].set(c_ctx)
    mod = _adaln(cond, w_ada[0], b_ada[0])[:b + 1].reshape(b + 1, 6, d)
    mod_x, mod_c = mod[:b], mod[b:b + 1]

    rope = _rope_tables(l)
    v0 = ATTN_W + KV_W
    w_v = w_in[0][:, v0:v0 + KV_W].reshape(d, N_KV_HEADS, HEAD_DIM)
    w_v = jnp.concatenate([w_v, jnp.zeros_like(w_v)], axis=-1).reshape(d, VEXT_W)
    b_v = b_in[0][v0:v0 + KV_W].reshape(N_KV_HEADS, HEAD_DIM)
    b_v = jnp.concatenate([b_v, jnp.ones_like(b_v)], axis=-1).reshape(VEXT_W)
    w_in_bf = jnp.concatenate([w_in[0][:, :v0], w_v, w_in[0][:, v0 + KV_W:]], axis=1).astype(BF16)
    b_in2 = jnp.concatenate([b_in[0][:v0], b_v, b_in[0][v0 + KV_W:]]).reshape(1, IN_W_EXT)
    g1 = norm1_g[0].reshape(1, d)
    q, kt, v, u, ga, gp = _inproj(x, mod_x, g1, w_in_bf, b_in2, rope)
    kv_sl = slice(ATTN_W, ATTN_W + KV_W + VEXT_W)
    kxt, vx = _ctx_kv(ctx, mod_c, g1, w_in_bf[:, kv_sl], b_in2[:, kv_sl])

    attn_o = _attention(q, kt, v, kxt, vx, attn_sink[0])

    x1, h2, top_idx, gate_w, rank, counts = _merge_route(
        attn_o, u, ga, gp, x, mod_x, w_pool[0].astype(BF16), pool_scale[0].reshape(1, d),
        w_attn_br[0].astype(BF16), w_pool_br[0].astype(BF16), w_out[0].astype(BF16),
        norm2_g[0].reshape(1, d), w_router[0].astype(BF16), b_router[0].reshape(1, N_EXPERTS))

    tm = TM_EXPERT
    nb = n * TOP_K // tm + N_EXPERTS
    cnt = counts[:, 0].astype(jnp.int32)
    padded = (cnt + tm - 1) // tm * tm
    pend = jnp.cumsum(padded)
    pstart = pend - padded
    expert_ids = jnp.arange(N_EXPERTS, dtype=jnp.int32)
    dest = jnp.sum(jnp.where(top_idx[..., None] == expert_ids, pstart, 0), axis=-1) + rank
    n_used = (pend[-1] // tm).reshape(1)
    block_start = jnp.arange(nb, dtype=jnp.int32) * tm
    block_e = jnp.minimum(jnp.sum((pend[None, :] <= block_start[:, None]).astype(jnp.int32), axis=1),
                          N_EXPERTS - 1)
    first = jnp.concatenate([jnp.ones((1,), jnp.int32),
                             (block_e[1:] != block_e[:-1]).astype(jnp.int32)])
    later_used = (expert_ids[None, :] > expert_ids[:, None]) & (cnt[None, :] > 0)
    next_of_expert = jnp.min(jnp.where(later_used, expert_ids[None, :], N_EXPERTS), axis=1)
    next_of_expert = jnp.where(next_of_expert < N_EXPERTS, next_of_expert, -1)
    next_e = jnp.sum(jnp.where(block_e[:, None] == expert_ids, next_of_expert, 0), axis=-1)

    pad_j = jnp.arange(tm, dtype=jnp.int32)[None, :]
    spare = nb * tm + expert_ids[:, None] * tm + pad_j
    pad_dest = jnp.where(pad_j < (padded - cnt)[:, None], (pstart + cnt)[:, None] + pad_j, spare)
    pad_dest = pad_dest.reshape(SC_WORKERS, N_EXPERTS * tm // (SC_WORKERS * SC_WINDOW), SC_WINDOW)

    xs = _sc_scatter_rows(h2, dest, pad_dest, (nb + N_EXPERTS) * tm)
    ys = _experts(block_e, first, next_e, n_used, xs, w_gu[0], b_gu[0], w_down[0], b_down[0])
    y4 = _sc_gather_rows(ys, dest.reshape(-1)).reshape(TOP_K, n, ROW_WORDS)
    return _combine_dense(y4, gate_w, x1, mod_x, final_g.reshape(1, d))
```

```python
import functools

import jax
import jax.numpy as jnp
from jax import lax
from jax.experimental import pallas as pl
from jax.experimental.pallas import tpu as pltpu
from jax.experimental.pallas import tpu_sc as plsc

D_MODEL = 1024
GRID_W = 64
HEAD_DIM = 64
N_HEADS = 16
N_KV_HEADS = 4
GROUP = N_HEADS // N_KV_HEADS
ATTN_W = N_HEADS * HEAD_DIM
KV_W = N_KV_HEADS * HEAD_DIM
WINDOW = 128
ATTN_SCALE = HEAD_DIM ** -0.5
ROPE_BASE = 10000.0
ROPE_PAIRS = HEAD_DIM // 4
POOL_WINDOWS = (2, 4, 8, 16)
POOL_GROUP_W = D_MODEL // len(POOL_WINDOWS)
IN_W = ATTN_W + 2 * KV_W + D_MODEL + 2 * D_MODEL
VEXT_W = N_KV_HEADS * 2 * HEAD_DIM
IN_W_EXT = IN_W - KV_W + VEXT_W
N_EXPERTS = 32
TOP_K = 4
D_FF = D_MODEL
SWIGLU_ALPHA = 1.702
SWIGLU_LIMIT = 7.0
NORM_EPS = 1e-5
NEG_INF = -1e30

LANES = 128
SUBLANES = 8
VMEM_LIMIT = 56 * 1024 * 1024

TM_INPROJ = 512
TQ = WINDOW
TM_MERGE = 512
MERGE_SUB = 256
TM_EXPERT = 512
EXPERT_FF_CHUNK = 1024
EXPERT_BLOCKS_PER_STEP = 2
POOL_HALO = SUBLANES

F32 = jnp.float32
BF16 = jnp.bfloat16


ROW_WORDS = D_MODEL // 2
U32 = jnp.uint32


def _pack_rows(value):
    bits = lax.bitcast_convert_type(value.astype(BF16).astype(F32), U32)
    return bits[:, :ROW_WORDS] | (bits[:, ROW_WORDS:] >> 16)


def _unpack_rows(words):
    hi = lax.bitcast_convert_type(words & jnp.uint32(0xFFFF0000), F32)
    lo = lax.bitcast_convert_type(words << 16, F32)
    return jnp.concatenate([hi, lo], axis=1)


def _params(*sem):
    return pltpu.CompilerParams(dimension_semantics=sem, vmem_limit_bytes=VMEM_LIMIT)


def _const_spec(shape):
    nd = len(shape)
    return pl.BlockSpec(shape, lambda *_: (0,) * nd)


def _adaln_kernel(c_ref, w_ref, b_ref, o_ref):
    c = c_ref[...]
    s = c * jax.nn.sigmoid(c)
    o_ref[...] = jnp.dot(s.astype(BF16), w_ref[...].astype(BF16),
                         preferred_element_type=F32) + b_ref[...]


def _adaln(cond, w_ada, b_ada):
    rows, d = cond.shape
    n = w_ada.shape[1]
    tn = 1024
    return pl.pallas_call(
        _adaln_kernel,
        grid=(n // tn,),
        in_specs=[_const_spec((rows, d)),
                  pl.BlockSpec((d, tn), lambda j: (0, j)),
                  pl.BlockSpec((1, tn), lambda j: (0, j))],
        out_specs=pl.BlockSpec((rows, tn), lambda j: (0, j)),
        out_shape=jax.ShapeDtypeStruct((rows, n), F32),
        compiler_params=_params("arbitrary"),
        name="adaln",
    )(cond, w_ada, b_ada.reshape(1, n))


def _norm_mod(x, g, shift, scale):
    y = x * lax.rsqrt(jnp.mean(x * x, axis=-1, keepdims=True) + NORM_EPS) * g
    return y * (1.0 + scale) + shift


def _rope(t, cos, sin_signed):
    w = t.shape[1]
    half = ROPE_PAIRS
    lane = lax.broadcasted_iota(jnp.int32, t.shape, 1)
    first = (lane & (2 * half - 1)) < half
    rot = jnp.where(first, pltpu.roll(t, w - half, 1), pltpu.roll(t, half, 1))
    reps = w // LANES
    cos_w = jnp.concatenate([cos] * reps, axis=1) if reps > 1 else cos
    sin_w = jnp.concatenate([sin_signed] * reps, axis=1) if reps > 1 else sin_signed
    return t * cos_w + rot * sin_w


def _token_table(row_ref, col_ref):
    gr = row_ref.shape[0]
    shape = (gr, GRID_W, LANES)
    full = (jnp.broadcast_to(row_ref[...][:, None, :], shape)
            + jnp.broadcast_to(col_ref[...][None, :, :], shape))
    return full.reshape(gr * GRID_W, LANES)


def _inproj_kernel(x_ref, mod_ref, g_ref, w_ref, b_ref, cr_ref, sr_ref, cc_ref, sc_ref,
                   q_ref, k_ref, v_ref, u_ref, ga_ref, gp_ref):
    h = _norm_mod(x_ref[0], g_ref[...], mod_ref[0, 0:1, :], mod_ref[0, 1:2, :])
    hb = h.astype(BF16)
    cos = _token_table(cr_ref, cc_ref)
    sin = _token_table(sr_ref, sc_ref)

    def proj(c0, c1):
        return jnp.dot(hb, w_ref[:, c0:c1], preferred_element_type=F32) + b_ref[:, c0:c1]

    cw = 512
    for c0 in range(0, ATTN_W, cw):
        q_ref[0, :, c0:c0 + cw] = (_rope(proj(c0, c0 + cw), cos, sin) * ATTN_SCALE).astype(BF16)
    k_ref[0] = _rope(proj(ATTN_W, ATTN_W + KV_W), cos, sin).T.astype(BF16)
    v_ref[0] = proj(ATTN_W + KV_W, ATTN_W + KV_W + VEXT_W).astype(BF16)
    base = ATTN_W + KV_W + VEXT_W
    for c0 in range(0, D_MODEL, cw):
        u_ref[0, :, c0:c0 + cw] = proj(base + c0, base + c0 + cw)
    for ref in (ga_ref, gp_ref):
        base += D_MODEL
        for c0 in range(0, D_MODEL, cw):
            ref[0, :, c0:c0 + cw] = jax.nn.sigmoid(proj(base + c0, base + c0 + cw)).astype(BF16)


def _inproj(x, mod, norm_g, w_in_bf, b_in, rope):
    b, l, d = x.shape
    tm = TM_INPROJ
    row = lambda bi, i: (bi, i, 0)
    out_shape = (
        jax.ShapeDtypeStruct((b, l, ATTN_W), BF16),
        jax.ShapeDtypeStruct((b, KV_W, l), BF16),
        jax.ShapeDtypeStruct((b, l, VEXT_W), BF16),
        jax.ShapeDtypeStruct((b, l, D_MODEL), F32),
        jax.ShapeDtypeStruct((b, l, D_MODEL), BF16),
        jax.ShapeDtypeStruct((b, l, D_MODEL), BF16),
    )
    return pl.pallas_call(
        _inproj_kernel,
        grid=(b, l // tm),
        in_specs=[pl.BlockSpec((1, tm, d), row),
                  pl.BlockSpec((1, 6, d), lambda bi, i: (bi, 0, 0)),
                  _const_spec((1, d)),
                  _const_spec((d, IN_W_EXT)),
                  _const_spec((1, IN_W_EXT)),
                  pl.BlockSpec((tm // GRID_W, LANES), lambda bi, i: (i, 0)),
                  pl.BlockSpec((tm // GRID_W, LANES), lambda bi, i: (i, 0)),
                  _const_spec((GRID_W, LANES)),
                  _const_spec((GRID_W, LANES))],
        out_specs=(pl.BlockSpec((1, tm, ATTN_W), row),
                   pl.BlockSpec((1, KV_W, tm), lambda bi, i: (bi, 0, i)),
                   pl.BlockSpec((1, tm, VEXT_W), row),
                   pl.BlockSpec((1, tm, D_MODEL), row),
                   pl.BlockSpec((1, tm, D_MODEL), row),
                   pl.BlockSpec((1, tm, D_MODEL), row)),
        out_shape=out_shape,
        compiler_params=_params("arbitrary", "arbitrary"),
        name="inproj",
    )(x, mod, norm_g, w_in_bf, b_in, *rope)


def _ctx_kv_kernel(x_ref, mod_ref, g_ref, w_ref, b_ref, k_ref, v_ref):
    h = _norm_mod(x_ref[0], g_ref[...], mod_ref[0, 0:1, :], mod_ref[0, 1:2, :])
    kv = jnp.dot(h.astype(BF16), w_ref[...], preferred_element_type=F32) + b_ref[...]
    k_ref[0] = kv[:, :KV_W].T.astype(BF16)
    v_ref[0] = kv[:, KV_W:].astype(BF16)


def _ctx_kv(ctx, mod_c, norm_g, w_kv_bf, b_kv):
    b, lc, d = ctx.shape
    row = lambda bi: (bi, 0, 0)
    return pl.pallas_call(
        _ctx_kv_kernel,
        grid=(b,),
        in_specs=[pl.BlockSpec((1, lc, d), row),
                  _const_spec((1, 6, d)),
                  _const_spec((1, d)),
                  _const_spec((d, KV_W + VEXT_W)),
                  _const_spec((1, KV_W + VEXT_W))],
        out_specs=(pl.BlockSpec((1, KV_W, lc), row), pl.BlockSpec((1, lc, VEXT_W), row)),
        out_shape=(jax.ShapeDtypeStruct((b, KV_W, lc), BF16),
                   jax.ShapeDtypeStruct((b, lc, VEXT_W), BF16)),
        compiler_params=_params("arbitrary"),
        name="ctx_kv",
    )(ctx, mod_c, norm_g, w_kv_bf, b_kv)


def _attn_kernel(sink_ref, q_ref, kp_ref, kc_ref, kn_ref, vp_ref, vc_ref, vn_ref,
                 kx_ref, vx_ref, o_ref):
    n = pl.program_id(1)
    last = pl.num_programs(1) - 1
    rows = GROUP * TQ
    pair_w = 2 * HEAD_DIM
    qi = lax.broadcasted_iota(jnp.int32, (rows, TQ), 0) & (TQ - 1)
    col = lax.broadcasted_iota(jnp.int32, (rows, TQ), 1)
    keep_prev = (col >= qi) & (n > 0)
    keep_next = (col <= qi) & (n < last)
    low_half = lax.broadcasted_iota(jnp.int32, (TQ, pair_w), 1) < HEAD_DIM
    row_id = lax.broadcasted_iota(jnp.int32, (rows, 1), 0)

    def scores(j):
        parts = []
        for g in range(GROUP):
            h = j * GROUP + g
            pair = q_ref[0, :, (h // 2) * pair_w:(h // 2 + 1) * pair_w]
            parts.append(jnp.where(low_half if h % 2 == 0 else ~low_half, pair, jnp.zeros_like(pair)))
        q4 = jnp.concatenate(parts, axis=0)
        ks = slice(j * HEAD_DIM, (j + 1) * HEAD_DIM)
        kt = jnp.concatenate([kp_ref[0, ks, :], kc_ref[0, ks, :], kn_ref[0, ks, :]], axis=1)
        s_loc = jnp.dot(q4, jnp.concatenate([kt, kt], axis=0), preferred_element_type=F32)
        kxt = kx_ref[0, ks, :]
        s_ctx = jnp.dot(q4, jnp.concatenate([kxt, kxt], axis=0), preferred_element_type=F32)
        return s_loc, s_ctx

    def probs(j, s_loc, s_ctx):
        pieces = [jnp.where(keep_prev, s_loc[:, :TQ], NEG_INF), s_loc[:, TQ:2 * TQ],
                  jnp.where(keep_next, s_loc[:, 2 * TQ:], NEG_INF)]
        pieces += [s_ctx[:, c0:c0 + TQ] for c0 in range(0, s_ctx.shape[1], TQ)]
        sink = jnp.zeros((rows, 1), F32)
        for g in range(GROUP):
            sink = jnp.where(row_id // TQ == g, sink_ref[j * GROUP + g], sink)
        mx = pieces[0]
        for pc in pieces[1:]:
            mx = jnp.maximum(mx, pc)
        m = jnp.maximum(jnp.max(mx, axis=-1, keepdims=True), sink)
        p = jnp.concatenate([jnp.exp(pc - m).astype(BF16) for pc in pieces], axis=1)
        return p, jnp.exp(sink - m)

    def output(j, p, sink_p):
        vs = slice(j * pair_w, (j + 1) * pair_w)
        v_all = jnp.concatenate([vp_ref[0, :, vs], vc_ref[0, :, vs], vn_ref[0, :, vs],
                                 vx_ref[0, :, vs]], axis=0)
        r = jnp.dot(p, v_all, preferred_element_type=F32)
        den = pltpu.roll(r, HEAD_DIM, 1) + sink_p
        o = r / den
        for g in range(0, GROUP, 2):
            even = o[g * TQ:(g + 1) * TQ]
            odd = pltpu.roll(o[(g + 1) * TQ:(g + 2) * TQ], HEAD_DIM, 1)
            c0 = (j * GROUP + g) * HEAD_DIM
            o_ref[0, :, c0:c0 + pair_w] = jnp.where(low_half, even, odd).astype(BF16)

    s_queue = [scores(0), scores(1)]
    p_queue = [probs(0, *s_queue.pop(0))]
    for j in range(N_KV_HEADS):
        if j + 2 < N_KV_HEADS:
            s_queue.append(scores(j + 2))
        ready = p_queue.pop(0)
        if j + 1 < N_KV_HEADS:
            p_queue.append(probs(j + 1, *s_queue.pop(0)))
        output(j, *ready)


def _attention(q, kt, v, kxt, vx, sink):
    b, l, _ = q.shape
    lc = kxt.shape[2]
    nblk = l // TQ
    cur = lambda bi, n: (bi, n, 0)
    prev = lambda bi, n: (bi, jnp.maximum(n - 1, 0), 0)
    nxt = lambda bi, n: (bi, jnp.minimum(n + 1, nblk - 1), 0)
    kcur = lambda bi, n: (bi, 0, n)
    kprev = lambda bi, n: (bi, 0, jnp.maximum(n - 1, 0))
    knxt = lambda bi, n: (bi, 0, jnp.minimum(n + 1, nblk - 1))
    kb = (1, KV_W, TQ)
    vb = (1, TQ, VEXT_W)
    return pl.pallas_call(
        _attn_kernel,
        grid=(b, nblk),
        in_specs=[pl.BlockSpec(memory_space=pltpu.SMEM),
                  pl.BlockSpec((1, TQ, ATTN_W), cur),
                  pl.BlockSpec(kb, kprev), pl.BlockSpec(kb, kcur), pl.BlockSpec(kb, knxt),
                  pl.BlockSpec(vb, prev), pl.BlockSpec(vb, cur), pl.BlockSpec(vb, nxt),
                  pl.BlockSpec((1, KV_W, lc), lambda bi, n: (bi, 0, 0)),
                  pl.BlockSpec((1, lc, VEXT_W), lambda bi, n: (bi, 0, 0))],
        out_specs=pl.BlockSpec((1, TQ, ATTN_W), cur),
        out_shape=jax.ShapeDtypeStruct((b, l, ATTN_W), BF16),
        compiler_params=_params("arbitrary", "arbitrary"),
        name="attention",
    )(sink, q, kt, kt, kt, v, v, v, kxt, vx)


def _shift_rows(a, d):
    n = a.shape[0]
    return pltpu.roll(a, (-d) % n, 0)


def _merge_kernel(ao_ref, u_ref, up_ref, un_ref, sa_ref, sp_ref, x_ref, mod_ref,
                  wpool_ref, pscale_ref, wab_ref, wpb_ref, wout_ref, g2_ref, wr_ref, br_ref,
                  x1_ref, h2_ref, idx_ref, gate_ref, rank_ref, cnt_ref, carry_ref, *, seq_len):
    bi = pl.program_id(0)
    i = pl.program_id(1)
    last = pl.num_programs(1) - 1
    tm, sub = TM_MERGE, MERGE_SUB

    @pl.when((bi == 0) & (i == 0))
    def _():
        carry_ref[...] = jnp.zeros_like(carry_ref)

    u = u_ref[0]
    prev = jnp.where(i > 0, up_ref[0], 0.0)
    nxt = jnp.where(i < last, un_ref[0], 0.0)
    ext = jnp.concatenate([prev, u, nxt], axis=0)

    def pool_diff(r0):
        t = i * tm + r0 + lax.broadcasted_iota(jnp.int32, (sub, 1), 0)
        diffs = []
        for g, w in enumerate(POOL_WINDOWS):
            cs = slice(g * POOL_GROUP_W, (g + 1) * POOL_GROUP_W)
            e = ext[r0:r0 + sub + 2 * POOL_HALO, cs]
            acc = _shift_rows(e, -1) + e
            step = 1
            while 2 * step < w:
                acc = _shift_rows(acc, -step) + _shift_rows(acc, step)
                step *= 2
            win = acc[POOL_HALO:POOL_HALO + sub]
            half = w // 2
            cnt = (jnp.minimum(t + half, seq_len) - jnp.maximum(t - half, 0)).astype(F32)
            diffs.append((win / cnt - u[r0:r0 + sub, cs]).astype(BF16))
        return diffs

    def branches(r0, diffs):
        rs = slice(r0, r0 + sub)
        a = jnp.dot(ao_ref[0, rs, :], wab_ref[...], preferred_element_type=F32)
        mixed = [jnp.dot(df, wpool_ref[g], preferred_element_type=F32) for g, df in enumerate(diffs)]
        pool_o = jnp.concatenate(mixed, axis=1) * pscale_ref[...]
        p = jnp.dot(pool_o.astype(BF16), wpb_ref[...], preferred_element_type=F32)
        merged = sa_ref[0, rs, :].astype(F32) * a + sp_ref[0, rs, :].astype(F32) * p
        return merged.astype(BF16)

    def project(r0, merged):
        rs = slice(r0, r0 + sub)
        o = jnp.dot(merged, wout_ref[...], preferred_element_type=F32)
        x1 = x_ref[0, rs, :] + mod_ref[0, 2:3, :] * o
        x1_ref[0, rs, :] = x1
        h2 = _norm_mod(x1, g2_ref[...], mod_ref[0, 3:4, :], mod_ref[0, 4:5, :])
        h2_ref[rs, :] = _pack_rows(h2)
        return jnp.dot(h2.astype(BF16), wr_ref[...], preferred_element_type=F32) + br_ref[...]

    def route(r0, logits, carry):
        lt = logits.T
        eid = lax.broadcasted_iota(jnp.int32, (N_EXPERTS, sub), 0).astype(F32)
        work = lt
        vals, idxs, sels = [], [], []
        for _ in range(TOP_K):
            mx = jnp.max(work, axis=0, keepdims=True)
            ix = jnp.min(jnp.where(work == mx, eid, float(N_EXPERTS)), axis=0, keepdims=True)
            sel = eid == ix
            work = jnp.where(sel, -jnp.inf, work)
            vals.append(mx)
            idxs.append(ix)
            sels.append(sel)
        ex = [jnp.exp(v - vals[0]) for v in vals]
        tot = ex[0] + ex[1] + ex[2] + ex[3]
        pad = [jnp.zeros_like(tot)] * (SUBLANES - TOP_K)
        gates = jnp.concatenate([e_ / tot for e_ in ex] + pad, axis=0)
        gate_ref[r0:r0 + sub, :] = gates.T[:, :TOP_K]
        idx_ref[:, r0:r0 + sub] = jnp.concatenate(idxs, axis=0).astype(jnp.int32)
        member = (sels[0] | sels[1] | sels[2] | sels[3]).astype(F32)
        t_row = lax.broadcasted_iota(jnp.int32, (sub, sub), 0)
        t_col = lax.broadcasted_iota(jnp.int32, (sub, sub), 1)
        earlier = (t_row < t_col).astype(BF16)
        before = jnp.dot(member.astype(BF16), earlier, preferred_element_type=F32) + carry
        ranks = [jnp.sum(jnp.where(s, before, 0.0), axis=0, keepdims=True) for s in sels]
        rank_ref[:, r0:r0 + sub] = jnp.concatenate(ranks, axis=0).astype(jnp.int32)
        return carry + jnp.sum(member, axis=1, keepdims=True)

    starts = list(range(0, tm, sub))
    carry = carry_ref[...]
    merged_q = [branches(starts[0], pool_diff(starts[0]))]
    logits_q = []
    for si, r0 in enumerate(starts):
        if si + 1 < len(starts):
            diffs_next = pool_diff(starts[si + 1])
        logits_q.append(project(r0, merged_q.pop(0)))
        if si + 1 < len(starts):
            merged_q.append(branches(starts[si + 1], diffs_next))
        if si >= 1:
            carry = route(starts[si - 1], logits_q.pop(0), carry)
    carry = route(starts[-1], logits_q.pop(0), carry)
    carry_ref[...] = carry
    cnt_ref[...] = carry


def _merge_route(attn_o, u, ga, gp, x, mod, w_pool_bf, pool_scale, wab, wpb, wout,
                 norm2_g, w_router, b_router):
    b, l, d = x.shape
    tm = TM_MERGE
    nt = l // tm
    n = b * l
    hb = tm // POOL_HALO
    row = lambda bi, i: (bi, i, 0)
    flat = lambda bi, i: (bi * nt + i, 0)
    halo_prev = lambda bi, i: (bi, jnp.maximum(i * hb - 1, 0), 0)
    halo_next = lambda bi, i: (bi, jnp.minimum((i + 1) * hb, l // POOL_HALO - 1), 0)
    slot_major = lambda bi, i: (0, bi * nt + i)
    out_shape = (
        jax.ShapeDtypeStruct((b, l, d), F32),
        jax.ShapeDtypeStruct((n, ROW_WORDS), U32),
        jax.ShapeDtypeStruct((TOP_K, n), jnp.int32),
        jax.ShapeDtypeStruct((n, TOP_K), F32),
        jax.ShapeDtypeStruct((TOP_K, n), jnp.int32),
        jax.ShapeDtypeStruct((N_EXPERTS, 1), F32),
    )
    return pl.pallas_call(
        functools.partial(_merge_kernel, seq_len=l),
        grid=(b, nt),
        in_specs=[pl.BlockSpec((1, tm, d), row),
                  pl.BlockSpec((1, tm, d), row),
                  pl.BlockSpec((1, POOL_HALO, d), halo_prev),
                  pl.BlockSpec((1, POOL_HALO, d), halo_next),
                  pl.BlockSpec((1, tm, d), row),
                  pl.BlockSpec((1, tm, d), row),
                  pl.BlockSpec((1, tm, d), row),
                  pl.BlockSpec((1, 6, d), lambda bi, i: (bi, 0, 0)),
                  _const_spec(w_pool_bf.shape),
                  _const_spec((1, d)),
                  _const_spec((d, d)), _const_spec((d, d)), _const_spec((d, d)),
                  _const_spec((1, d)),
                  _const_spec((d, N_EXPERTS)),
                  _const_spec((1, N_EXPERTS))],
        out_specs=(pl.BlockSpec((1, tm, d), row),
                   pl.BlockSpec((tm, ROW_WORDS), flat),
                   pl.BlockSpec((TOP_K, tm), slot_major),
                   pl.BlockSpec((tm, TOP_K), flat),
                   pl.BlockSpec((TOP_K, tm), slot_major),
                   _const_spec((N_EXPERTS, 1))),
        out_shape=out_shape,
        scratch_shapes=[pltpu.VMEM((N_EXPERTS, 1), F32)],
        compiler_params=_params("arbitrary", "arbitrary"),
        name="merge_route",
    )(attn_o, u, u, u, ga, gp, x, mod, w_pool_bf, pool_scale, wab, wpb, wout,
      norm2_g, w_router, b_router)


SC_CORES = 2
SC_SUBCORES = 16
SC_WORKERS = SC_CORES * SC_SUBCORES
SC_WINDOW = 128
SC_GATHER_WINDOW = 64


def _sc_worker_id():
    return lax.axis_index("s") * SC_CORES + lax.axis_index("c")


def _sc_scatter_rows(rows, dest, pad_dest, out_rows):
    n, w = rows.shape
    n_win = n // (SC_WORKERS * SC_WINDOW)
    n_pad_win = pad_dest.shape[1]
    assert dest.shape == (TOP_K, n) and SC_WORKERS * n_win * SC_WINDOW == n
    mesh = plsc.VectorSubcoreMesh(core_axis_name="c", subcore_axis_name="s")
    zeros = jnp.zeros((SC_WINDOW, w), rows.dtype)

    @functools.partial(
        pl.kernel, mesh=mesh,
        out_type=jax.ShapeDtypeStruct((out_rows, w), rows.dtype),
        scratch_types=[pltpu.VMEM((TOP_K, SC_WINDOW), jnp.int32),
                       pltpu.VMEM((n_pad_win, SC_WINDOW), jnp.int32),
                       pltpu.VMEM((SC_WINDOW, w), rows.dtype),
                       pltpu.SemaphoreType.DMA],
        name="sc_scatter_rows",
    )
    def scatter(rows_hbm, dest_hbm, pad_hbm, zeros_hbm, out_hbm, idx_v, pad_v, rows_v, sem):
        wid = _sc_worker_id()

        @pl.loop(0, n_win)
        def _(j):
            base = (wid * n_win + j) * SC_WINDOW
            pltpu.sync_copy(rows_hbm.at[pl.ds(base, SC_WINDOW)], rows_v)
            pltpu.sync_copy(dest_hbm.at[:, pl.ds(base, SC_WINDOW)], idx_v)
            copies = [pltpu.async_copy(rows_v, out_hbm.at[idx_v.at[k]], sem) for k in range(TOP_K)]
            for cp in copies:
                cp.wait()

        pltpu.sync_copy(zeros_hbm, rows_v)
        pltpu.sync_copy(pad_hbm.at[wid], pad_v)
        copies = [pltpu.async_copy(rows_v, out_hbm.at[pad_v.at[j]], sem) for j in range(n_pad_win)]
        for cp in copies:
            cp.wait()

    return scatter(rows, dest, pad_dest, zeros)


def _expert_kernel(be_ref, first_ref, next_ref, nu_ref, x_ref, wgu_hbm, bgu_ref, wd_hbm, bd_ref,
                   y_ref, wgu_f, wd_f, wgu_s, wd_s, sem):
    step = pl.program_id(0)

    def weight_copies(e):
        return (pltpu.make_async_copy(wgu_hbm.at[e], wgu_f, sem.at[0]),
                pltpu.make_async_copy(wd_hbm.at[e], wd_f, sem.at[1]))

    def start_weights(e):
        for cp in weight_copies(e):
            cp.start(priority=1)

    @pl.when(step == 0)
    def _():
        start_weights(be_ref[0])

    for sub in range(EXPERT_BLOCKS_PER_STEP):
        _expert_block(step * EXPERT_BLOCKS_PER_STEP + sub,
                      slice(sub * TM_EXPERT, (sub + 1) * TM_EXPERT),
                      be_ref, first_ref, next_ref, nu_ref, x_ref, bgu_ref, bd_ref, y_ref,
                      wgu_f, wd_f, wgu_s, wd_s, weight_copies, start_weights)


def _expert_block(i, rows, be_ref, first_ref, next_ref, nu_ref, x_ref, bgu_ref, bd_ref, y_ref,
                  wgu_f, wd_f, wgu_s, wd_s, weight_copies, start_weights):
    @pl.when(i < nu_ref[0])
    def _():
        e = be_ref[i]

        @pl.when(first_ref[i] == 1)
        def _():
            for cp in weight_copies(e):
                cp.wait()
            ck = 256
            for r0 in range(0, D_MODEL, ck):
                wgu_s[r0:r0 + ck, :] = wgu_f[r0:r0 + ck, :].astype(BF16)
                wd_s[r0:r0 + ck, :] = wd_f[r0:r0 + ck, :].astype(BF16)

            @pl.when(next_ref[i] >= 0)
            def _():
                start_weights(next_ref[i])

        x = _unpack_rows(x_ref[rows, :]).astype(BF16)
        bgu = bgu_ref[e]
        fc = EXPERT_FF_CHUNK

        def up(c0):
            gt = jnp.dot(x, wgu_s[:, c0:c0 + fc], preferred_element_type=F32)
            ln = jnp.dot(x, wgu_s[:, D_FF + c0:D_FF + c0 + fc], preferred_element_type=F32)
            return gt, ln

        def down(c0, gt, ln):
            gt = jnp.minimum(gt + bgu[:, c0:c0 + fc], SWIGLU_LIMIT)
            ln = jnp.clip(ln + bgu[:, D_FF + c0:D_FF + c0 + fc], -SWIGLU_LIMIT, SWIGLU_LIMIT)
            act = gt * jax.nn.sigmoid(SWIGLU_ALPHA * gt) * (ln + 1.0)
            return jnp.dot(act.astype(BF16), wd_s[c0:c0 + fc, :], preferred_element_type=F32)

        chunks = list(range(0, D_FF, fc))
        pending = [up(chunks[0])]
        acc = jnp.zeros((TM_EXPERT, D_MODEL), F32)
        for ci, c0 in enumerate(chunks):
            if ci + 1 < len(chunks):
                pending.append(up(chunks[ci + 1]))
            acc = acc + down(c0, *pending.pop(0))
        y_ref[rows, :] = _pack_rows(acc + bd_ref[e])

    @pl.when(i >= nu_ref[0])
    def _():
        y_ref[rows, :] = jnp.zeros((TM_EXPERT, ROW_WORDS), U32)


def _experts(block_e, first, next_e, n_used, xs, w_gu, b_gu, w_down, b_down):
    d = D_MODEL
    nb = block_e.shape[0]
    per_step = EXPERT_BLOCKS_PER_STEP
    tm = TM_EXPERT * per_step
    assert nb % per_step == 0

    xmap = lambda i, be, fi, nx, nu: (jnp.minimum(i, (nu[0] - 1) // per_step), 0)
    grid_spec = pltpu.PrefetchScalarGridSpec(
        num_scalar_prefetch=4,
        grid=(nb // per_step,),
        in_specs=[pl.BlockSpec((tm, ROW_WORDS), xmap),
                  pl.BlockSpec(memory_space=pl.ANY),
                  _const_spec((N_EXPERTS, 1, 2 * D_FF)),
                  pl.BlockSpec(memory_space=pl.ANY),
                  _const_spec((N_EXPERTS, 1, d))],
        out_specs=pl.BlockSpec((tm, ROW_WORDS), lambda i, be, fi, nx, nu: (i, 0)),
        scratch_shapes=[pltpu.VMEM((d, 2 * D_FF), F32), pltpu.VMEM((D_FF, d), F32),
                        pltpu.VMEM((d, 2 * D_FF), BF16), pltpu.VMEM((D_FF, d), BF16),
                        pltpu.SemaphoreType.DMA((2,))],
    )
    return pl.pallas_call(
        _expert_kernel,
        grid_spec=grid_spec,
        out_shape=jax.ShapeDtypeStruct((nb * TM_EXPERT, ROW_WORDS), U32),
        compiler_params=_params("arbitrary"),
        name="moe_experts",
    )(block_e, first, next_e, n_used, xs, w_gu, b_gu.reshape(N_EXPERTS, 1, 2 * D_FF),
      w_down, b_down.reshape(N_EXPERTS, 1, d))


def _sc_gather_rows(table, idx):
    m = idx.shape[0]
    w = table.shape[1]
    win = SC_GATHER_WINDOW
    per_worker = m // SC_WORKERS
    n_win = per_worker // win
    assert per_worker * SC_WORKERS == m and n_win * win == per_worker and n_win % 2 == 0
    mesh = plsc.VectorSubcoreMesh(core_axis_name="c", subcore_axis_name="s")

    @functools.partial(
        pl.kernel, mesh=mesh,
        out_type=jax.ShapeDtypeStruct((m, w), table.dtype),
        scratch_types=[pltpu.VMEM((win,), jnp.int32), pltpu.VMEM((win,), jnp.int32),
                       pltpu.VMEM((win, w), table.dtype), pltpu.VMEM((win, w), table.dtype),
                       pltpu.SemaphoreType.DMA((4,))],
        name="sc_gather_rows",
    )
    def gather(table_hbm, idx_hbm, out_hbm, idx0, idx1, rows0, rows1, sems):
        wid = _sc_worker_id()
        idxs, rows = (idx0, idx1), (rows0, rows1)

        def out_window(j):
            return out_hbm.at[pl.ds(wid * per_worker + j * win, win)]

        def start_gather(j, slot):
            pltpu.sync_copy(idx_hbm.at[pl.ds(wid * per_worker + j * win, win)], idxs[slot])
            pltpu.async_copy(table_hbm.at[idxs[slot]], rows[slot], sems.at[slot])

        def wait_gather(slot):
            pltpu.make_async_copy(table_hbm.at[idxs[slot]], rows[slot], sems.at[slot]).wait()

        def wait_write(j, slot):
            pltpu.make_async_copy(rows[slot], out_window(j), sems.at[2 + slot]).wait()

        start_gather(0, 0)

        @pl.loop(0, n_win, step=2)
        def _(j0):
            for slot in range(2):
                j = j0 + slot
                other = 1 - slot

                @pl.when(j >= 1)
                def _():
                    wait_write(j - 1, other)

                @pl.when(j + 1 < n_win)
                def _():
                    start_gather(j + 1, other)

                wait_gather(slot)
                pltpu.async_copy(rows[slot], out_window(j), sems.at[2 + slot])

        wait_write(n_win - 1, (n_win - 1) % 2)

    return gather(table, idx)


def _combine_dense_kernel(y4_ref, gate_ref, x1_ref, mod_ref, fg_ref, o_ref):
    gate = gate_ref[...]
    y = gate[:, 0:1] * _unpack_rows(y4_ref[0])
    for k in range(1, TOP_K):
        y = y + gate[:, k:k + 1] * _unpack_rows(y4_ref[k])
    x2 = x1_ref[...] + mod_ref[0, 5:6, :] * y
    o_ref[...] = x2 * lax.rsqrt(jnp.mean(x2 * x2, axis=-1, keepdims=True) + NORM_EPS) * fg_ref[...]


def _combine_dense(y4, gate_w, x1, mod, final_g):
    b, l, d = x1.shape
    ts = 512
    nt = l // ts
    return pl.pallas_call(
        _combine_dense_kernel,
        grid=(b * nt,),
        in_specs=[pl.BlockSpec((TOP_K, ts, ROW_WORDS), lambda s: (0, s, 0)),
                  pl.BlockSpec((ts, TOP_K), lambda s: (s, 0)),
                  pl.BlockSpec((ts, d), lambda s: (s, 0)),
                  pl.BlockSpec((1, 6, d), lambda s: (s // nt, 0, 0)),
                  _const_spec((1, d))],
        out_specs=pl.BlockSpec((ts, d), lambda s: (s, 0)),
        out_shape=jax.ShapeDtypeStruct((b * l, d), F32),
        compiler_params=_params("arbitrary"),
        name="moe_combine",
    )(y4, gate_w, x1.reshape(b * l, d), mod, final_g).reshape(b, l, d)


def _rope_tables(seq_len):
    inv_freq = ROPE_BASE ** (-jnp.arange(ROPE_PAIRS, dtype=F32) / ROPE_PAIRS)
    reps = LANES // HEAD_DIM
    out = []
    for n_pos, on_row in ((seq_len // GRID_W, True), (GRID_W, False)):
        ang = jnp.arange(n_pos, dtype=F32)[:, None] * inv_freq
        zero = jnp.zeros_like(ang)
        for fn, sign in ((jnp.cos, 1.0), (jnp.sin, -1.0)):
            v = fn(ang)
            head = ([sign * v, v, zero, zero] if on_row else [zero, zero, sign * v, v])
            out.append(jnp.concatenate(head * reps, axis=1))
    return out


def kernel(x, c, ctx, c_ctx, w_ada, b_ada, norm1_g, norm2_g, w_in, b_in, attn_sink, w_pool,
           pool_scale, w_attn_br, w_pool_br, w_out, w_router, b_router, w_gu, b_gu, w_down,
           b_down, final_g):
    b, l, d = x.shape
    n = b * l
    assert w_ada.shape[0] == 1, "single-layer block"

    cond = jnp.zeros((SUBLANES, d), F32).at[:b].set(c).at[b].set(c_ctx)
    mod = _adaln(cond, w_ada[0], b_ada[0])[:b + 1].reshape(b + 1, 6, d)
    mod_x, mod_c = mod[:b], mod[b:b + 1]

    rope = _rope_tables(l)
    v0 = ATTN_W + KV_W
    w_v = w_in[0][:, v0:v0 + KV_W].reshape(d, N_KV_HEADS, HEAD_DIM)
    w_v = jnp.concatenate([w_v, jnp.zeros_like(w_v)], axis=-1).reshape(d, VEXT_W)
    b_v = b_in[0][v0:v0 + KV_W].reshape(N_KV_HEADS, HEAD_DIM)
    b_v = jnp.concatenate([b_v, jnp.ones_like(b_v)], axis=-1).reshape(VEXT_W)
    w_in_bf = jnp.concatenate([w_in[0][:, :v0], w_v, w_in[0][:, v0 + KV_W:]], axis=1).astype(BF16)
    b_in2 = jnp.concatenate([b_in[0][:v0], b_v, b_in[0][v0 + KV_W:]]).reshape(1, IN_W_EXT)
    g1 = norm1_g[0].reshape(1, d)
    q, kt, v, u, ga, gp = _inproj(x, mod_x, g1, w_in_bf, b_in2, rope)
    kv_sl = slice(ATTN_W, ATTN_W + KV_W + VEXT_W)
    kxt, vx = _ctx_kv(ctx, mod_c, g1, w_in_bf[:, kv_sl], b_in2[:, kv_sl])

    attn_o = _attention(q, kt, v, kxt, vx, attn_sink[0])

    x1, h2, top_idx, gate_w, rank, counts = _merge_route(
        attn_o, u, ga, gp, x, mod_x, w_pool[0].astype(BF16), pool_scale[0].reshape(1, d),
        w_attn_br[0].astype(BF16), w_pool_br[0].astype(BF16), w_out[0].astype(BF16),
        norm2_g[0].reshape(1, d), w_router[0].astype(BF16), b_router[0].reshape(1, N_EXPERTS))

    tm = TM_EXPERT
    nb = n * TOP_K // tm + N_EXPERTS
    cnt = counts[:, 0].astype(jnp.int32)
    padded = (cnt + tm - 1) // tm * tm
    pend = jnp.cumsum(padded)
    pstart = pend - padded
    expert_ids = jnp.arange(N_EXPERTS, dtype=jnp.int32)
    dest = jnp.sum(jnp.where(top_idx[..., None] == expert_ids, pstart, 0), axis=-1) + rank
    n_used = (pend[-1] // tm).reshape(1)
    block_start = jnp.arange(nb, dtype=jnp.int32) * tm
    block_e = jnp.minimum(jnp.sum((pend[None, :] <= block_start[:, None]).astype(jnp.int32), axis=1),
                          N_EXPERTS - 1)
    first = jnp.concatenate([jnp.ones((1,), jnp.int32),
                             (block_e[1:] != block_e[:-1]).astype(jnp.int32)])
    later_used = (expert_ids[None, :] > expert_ids[:, None]) & (cnt[None, :] > 0)
    next_of_expert = jnp.min(jnp.where(later_used, expert_ids[None, :], N_EXPERTS), axis=1)
    next_of_expert = jnp.where(next_of_expert < N_EXPERTS, next_of_expert, -1)
    next_e = jnp.sum(jnp.where(block_e[:, None] == expert_ids, next_of_expert, 0), axis=-1)

    pad_j = jnp.arange(tm, dtype=jnp.int32)[None, :]
    spare = nb * tm + expert_ids[:, None] * tm + pad_j
    pad_dest = jnp.where(pad_j < (padded - cnt)[:, None], (pstart + cnt)[:, None] + pad_j, spare)
    pad_dest = pad_dest.reshape(SC_WORKERS, N_EXPERTS * tm // (SC_WORKERS * SC_WINDOW), SC_WINDOW)

    xs = _sc_scatter_rows(h2, dest, pad_dest, (nb + N_EXPERTS) * tm)
    ys = _experts(block_e, first, next_e, n_used, xs, w_gu[0], b_gu[0], w_down[0], b_down[0])
    y4 = _sc_gather_rows(ys, dest.reshape(-1)).reshape(TOP_K, n, ROW_WORDS)
    return _combine_dense(y4, gate_w, x1, mod_x, final_g.reshape(1, d))
```

```python
import functools

import jax
import jax.numpy as jnp
from jax import lax
from jax.experimental import pallas as pl
from jax.experimental.pallas import tpu as pltpu
from jax.experimental.pallas import tpu_sc as plsc

D_MODEL = 1024
GRID_W = 64
HEAD_DIM = 64
N_HEADS = 16
N_KV_HEADS = 4
GROUP = N_HEADS // N_KV_HEADS
ATTN_W = N_HEADS * HEAD_DIM
KV_W = N_KV_HEADS * HEAD_DIM
WINDOW = 128
ATTN_SCALE = HEAD_DIM ** -0.5
ROPE_BASE = 10000.0
ROPE_PAIRS = HEAD_DIM // 4
POOL_WINDOWS = (2, 4, 8, 16)
POOL_GROUP_W = D_MODEL // len(POOL_WINDOWS)
IN_W = ATTN_W + 2 * KV_W + D_MODEL + 2 * D_MODEL
VEXT_W = N_KV_HEADS * 2 * HEAD_DIM
IN_W_EXT = IN_W - KV_W + VEXT_W
N_EXPERTS = 32
TOP_K = 4
D_FF = D_MODEL
SWIGLU_ALPHA = 1.702
SWIGLU_LIMIT = 7.0
NORM_EPS = 1e-5
NEG_INF = -1e30

LANES = 128
SUBLANES = 8
VMEM_LIMIT = 56 * 1024 * 1024

TM_INPROJ = 512
TQ = WINDOW
TM_MERGE = 512
MERGE_SUB = 256
TM_EXPERT = 512
EXPERT_ROW_QUANTUM = 128
EXPERT_BLOCKS_PER_STEP = 1
POOL_HALO = SUBLANES

F32 = jnp.float32
BF16 = jnp.bfloat16


ROW_WORDS = D_MODEL // 2
U32 = jnp.uint32


def _pack_rows(value):
    bits = lax.bitcast_convert_type(value.astype(BF16).astype(F32), U32)
    return bits[:, :ROW_WORDS] | (bits[:, ROW_WORDS:] >> 16)


def _unpack_rows(words):
    hi = lax.bitcast_convert_type(words & jnp.uint32(0xFFFF0000), F32)
    lo = lax.bitcast_convert_type(words << 16, F32)
    return jnp.concatenate([hi, lo], axis=1)


def _params(*sem):
    return pltpu.CompilerParams(dimension_semantics=sem, vmem_limit_bytes=VMEM_LIMIT)


def _const_spec(shape):
    nd = len(shape)
    return pl.BlockSpec(shape, lambda *_: (0,) * nd)


def _adaln_kernel(c_ref, w_ref, b_ref, o_ref):
    c = c_ref[...]
    s = c * jax.nn.sigmoid(c)
    o_ref[...] = jnp.dot(s.astype(BF16), w_ref[...].astype(BF16),
                         preferred_element_type=F32) + b_ref[...]


def _adaln(cond, w_ada, b_ada):
    rows, d = cond.shape
    n = w_ada.shape[1]
    tn = 1024
    return pl.pallas_call(
        _adaln_kernel,
        grid=(n // tn,),
        in_specs=[_const_spec((rows, d)),
                  pl.BlockSpec((d, tn), lambda j: (0, j)),
                  pl.BlockSpec((1, tn), lambda j: (0, j))],
        out_specs=pl.BlockSpec((rows, tn), lambda j: (0, j)),
        out_shape=jax.ShapeDtypeStruct((rows, n), F32),
        compiler_params=_params("arbitrary"),
        name="adaln",
    )(cond, w_ada, b_ada.reshape(1, n))


def _norm_mod(x, g, shift, scale):
    y = x * lax.rsqrt(jnp.mean(x * x, axis=-1, keepdims=True) + NORM_EPS) * g
    return y * (1.0 + scale) + shift


def _rope(t, cos, sin_signed):
    w = t.shape[1]
    half = ROPE_PAIRS
    lane = lax.broadcasted_iota(jnp.int32, t.shape, 1)
    first = (lane & (2 * half - 1)) < half
    rot = jnp.where(first, pltpu.roll(t, w - half, 1), pltpu.roll(t, half, 1))
    reps = w // LANES
    cos_w = jnp.concatenate([cos] * reps, axis=1) if reps > 1 else cos
    sin_w = jnp.concatenate([sin_signed] * reps, axis=1) if reps > 1 else sin_signed
    return t * cos_w + rot * sin_w


def _token_table(row_ref, col_ref):
    gr = row_ref.shape[0]
    shape = (gr, GRID_W, LANES)
    full = (jnp.broadcast_to(row_ref[...][:, None, :], shape)
            + jnp.broadcast_to(col_ref[...][None, :, :], shape))
    return full.reshape(gr * GRID_W, LANES)


def _inproj_kernel(x_ref, mod_ref, g_ref, w_ref, b_ref, cr_ref, sr_ref, cc_ref, sc_ref,
                   q_ref, k_ref, v_ref, u_ref, ga_ref, gp_ref):
    h = _norm_mod(x_ref[0], g_ref[...], mod_ref[0, 0:1, :], mod_ref[0, 1:2, :])
    hb = h.astype(BF16)
    cos = _token_table(cr_ref, cc_ref)
    sin = _token_table(sr_ref, sc_ref)

    def proj(c0, c1):
        return jnp.dot(hb, w_ref[:, c0:c1], preferred_element_type=F32) + b_ref[:, c0:c1]

    cw = 512
    chunks = []
    for c0 in range(0, ATTN_W, cw):
        def put_q(t, c0=c0):
            q_ref[0, :, c0:c0 + cw] = (_rope(t, cos, sin) * ATTN_SCALE).astype(BF16)
        chunks.append((c0, cw, put_q))

    def put_k(t):
        k_ref[0] = _rope(t, cos, sin).T.astype(BF16)

    def put_v(t):
        v_ref[0] = t.astype(BF16)

    chunks.append((ATTN_W, KV_W, put_k))
    chunks.append((ATTN_W + KV_W, VEXT_W, put_v))
    base = ATTN_W + KV_W + VEXT_W
    for c0 in range(0, D_MODEL, cw):
        def put_u(t, c0=c0):
            u_ref[0, :, c0:c0 + cw] = t
        chunks.append((base + c0, cw, put_u))
    for ref in (ga_ref, gp_ref):
        base += D_MODEL
        for c0 in range(0, D_MODEL, cw):
            def put_gate(t, ref=ref, c0=c0):
                ref[0, :, c0:c0 + cw] = jax.nn.sigmoid(t).astype(BF16)
            chunks.append((base + c0, cw, put_gate))

    pending = proj(chunks[0][0], chunks[0][0] + chunks[0][1])
    for ci, (_, _, epilogue) in enumerate(chunks):
        ready = pending
        if ci + 1 < len(chunks):
            n0, nw, _ = chunks[ci + 1]
            pending = proj(n0, n0 + nw)
        epilogue(ready)


def _inproj(x, mod, norm_g, w_in_bf, b_in, rope):
    b, l, d = x.shape
    tm = TM_INPROJ
    row = lambda bi, i: (bi, i, 0)
    out_shape = (
        jax.ShapeDtypeStruct((b, l, ATTN_W), BF16),
        jax.ShapeDtypeStruct((b, KV_W, l), BF16),
        jax.ShapeDtypeStruct((b, l, VEXT_W), BF16),
        jax.ShapeDtypeStruct((b, l, D_MODEL), F32),
        jax.ShapeDtypeStruct((b, l, D_MODEL), BF16),
        jax.ShapeDtypeStruct((b, l, D_MODEL), BF16),
    )
    return pl.pallas_call(
        _inproj_kernel,
        grid=(b, l // tm),
        in_specs=[pl.BlockSpec((1, tm, d), row),
                  pl.BlockSpec((1, 6, d), lambda bi, i: (bi, 0, 0)),
                  _const_spec((1, d)),
                  _const_spec((d, IN_W_EXT)),
                  _const_spec((1, IN_W_EXT)),
                  pl.BlockSpec((tm // GRID_W, LANES), lambda bi, i: (i, 0)),
                  pl.BlockSpec((tm // GRID_W, LANES), lambda bi, i: (i, 0)),
                  _const_spec((GRID_W, LANES)),
                  _const_spec((GRID_W, LANES))],
        out_specs=(pl.BlockSpec((1, tm, ATTN_W), row),
                   pl.BlockSpec((1, KV_W, tm), lambda bi, i: (bi, 0, i)),
                   pl.BlockSpec((1, tm, VEXT_W), row),
                   pl.BlockSpec((1, tm, D_MODEL), row),
                   pl.BlockSpec((1, tm, D_MODEL), row),
                   pl.BlockSpec((1, tm, D_MODEL), row)),
        out_shape=out_shape,
        compiler_params=_params("arbitrary", "arbitrary"),
        name="inproj",
    )(x, mod, norm_g, w_in_bf, b_in, *rope)


def _ctx_kv_kernel(x_ref, mod_ref, g_ref, w_ref, b_ref, k_ref, v_ref):
    h = _norm_mod(x_ref[0], g_ref[...], mod_ref[0, 0:1, :], mod_ref[0, 1:2, :])
    kv = jnp.dot(h.astype(BF16), w_ref[...], preferred_element_type=F32) + b_ref[...]
    k_ref[0] = kv[:, :KV_W].T.astype(BF16)
    v_ref[0] = kv[:, KV_W:].astype(BF16)


def _ctx_kv(ctx, mod_c, norm_g, w_kv_bf, b_kv):
    b, lc, d = ctx.shape
    row = lambda bi: (bi, 0, 0)
    return pl.pallas_call(
        _ctx_kv_kernel,
        grid=(b,),
        in_specs=[pl.BlockSpec((1, lc, d), row),
                  _const_spec((1, 6, d)),
                  _const_spec((1, d)),
                  _const_spec((d, KV_W + VEXT_W)),
                  _const_spec((1, KV_W + VEXT_W))],
        out_specs=(pl.BlockSpec((1, KV_W, lc), row), pl.BlockSpec((1, lc, VEXT_W), row)),
        out_shape=(jax.ShapeDtypeStruct((b, KV_W, lc), BF16),
                   jax.ShapeDtypeStruct((b, lc, VEXT_W), BF16)),
        compiler_params=_params("arbitrary"),
        name="ctx_kv",
    )(ctx, mod_c, norm_g, w_kv_bf, b_kv)


def _attn_kernel(sink_ref, q_ref, kp_ref, kc_ref, kn_ref, vp_ref, vc_ref, vn_ref,
                 kx_ref, vx_ref, o_ref):
    n = pl.program_id(1)
    last = pl.num_programs(1) - 1
    rows = GROUP * TQ
    pair_w = 2 * HEAD_DIM
    qi = lax.broadcasted_iota(jnp.int32, (rows, TQ), 0) & (TQ - 1)
    col = lax.broadcasted_iota(jnp.int32, (rows, TQ), 1)
    keep_prev = (col >= qi) & (n > 0)
    keep_next = (col <= qi) & (n < last)
    low_half = lax.broadcasted_iota(jnp.int32, (TQ, pair_w), 1) < HEAD_DIM
    row_id = lax.broadcasted_iota(jnp.int32, (rows, 1), 0)

    def scores(j):
        parts = []
        for g in range(GROUP):
            h = j * GROUP + g
            pair = q_ref[0, :, (h // 2) * pair_w:(h // 2 + 1) * pair_w]
            parts.append(jnp.where(low_half if h % 2 == 0 else ~low_half, pair, jnp.zeros_like(pair)))
        q4 = jnp.concatenate(parts, axis=0)
        ks = slice(j * HEAD_DIM, (j + 1) * HEAD_DIM)
        kt = jnp.concatenate([kp_ref[0, ks, :], kc_ref[0, ks, :], kn_ref[0, ks, :]], axis=1)
        s_loc = jnp.dot(q4, jnp.concatenate([kt, kt], axis=0), preferred_element_type=F32)
        kxt = kx_ref[0, ks, :]
        s_ctx = jnp.dot(q4, jnp.concatenate([kxt, kxt], axis=0), preferred_element_type=F32)
        return s_loc, s_ctx

    def probs(j, s_loc, s_ctx):
        pieces = [jnp.where(keep_prev, s_loc[:, :TQ], NEG_INF), s_loc[:, TQ:2 * TQ],
                  jnp.where(keep_next, s_loc[:, 2 * TQ:], NEG_INF)]
        pieces += [s_ctx[:, c0:c0 + TQ] for c0 in range(0, s_ctx.shape[1], TQ)]
        sink = jnp.zeros((rows, 1), F32)
        for g in range(GROUP):
            sink = jnp.where(row_id // TQ == g, sink_ref[j * GROUP + g], sink)
        mx = pieces[0]
        for pc in pieces[1:]:
            mx = jnp.maximum(mx, pc)
        m = jnp.maximum(jnp.max(mx, axis=-1, keepdims=True), sink)
        p = jnp.concatenate([jnp.exp(pc - m).astype(BF16) for pc in pieces], axis=1)
        return p, jnp.exp(sink - m)

    def output(j, p, sink_p):
        vs = slice(j * pair_w, (j + 1) * pair_w)
        v_all = jnp.concatenate([vp_ref[0, :, vs], vc_ref[0, :, vs], vn_ref[0, :, vs],
                                 vx_ref[0, :, vs]], axis=0)
        r = jnp.dot(p, v_all, preferred_element_type=F32)
        den = pltpu.roll(r, HEAD_DIM, 1) + sink_p
        o = r / den
        for g in range(0, GROUP, 2):
            even = o[g * TQ:(g + 1) * TQ]
            odd = pltpu.roll(o[(g + 1) * TQ:(g + 2) * TQ], HEAD_DIM, 1)
            c0 = (j * GROUP + g) * HEAD_DIM
            o_ref[0, :, c0:c0 + pair_w] = jnp.where(low_half, even, odd).astype(BF16)

    s_queue = [scores(0), scores(1)]
    p_queue = [probs(0, *s_queue.pop(0))]
    for j in range(N_KV_HEADS):
        if j + 2 < N_KV_HEADS:
            s_queue.append(scores(j + 2))
        ready = p_queue.pop(0)
        if j + 1 < N_KV_HEADS:
            p_queue.append(probs(j + 1, *s_queue.pop(0)))
        output(j, *ready)


def _attention(q, kt, v, kxt, vx, sink):
    b, l, _ = q.shape
    lc = kxt.shape[2]
    nblk = l // TQ
    cur = lambda bi, n: (bi, n, 0)
    prev = lambda bi, n: (bi, jnp.maximum(n - 1, 0), 0)
    nxt = lambda bi, n: (bi, jnp.minimum(n + 1, nblk - 1), 0)
    kcur = lambda bi, n: (bi, 0, n)
    kprev = lambda bi, n: (bi, 0, jnp.maximum(n - 1, 0))
    knxt = lambda bi, n: (bi, 0, jnp.minimum(n + 1, nblk - 1))
    kb = (1, KV_W, TQ)
    vb = (1, TQ, VEXT_W)
    return pl.pallas_call(
        _attn_kernel,
        grid=(b, nblk),
        in_specs=[pl.BlockSpec(memory_space=pltpu.SMEM),
                  pl.BlockSpec((1, TQ, ATTN_W), cur),
                  pl.BlockSpec(kb, kprev), pl.BlockSpec(kb, kcur), pl.BlockSpec(kb, knxt),
                  pl.BlockSpec(vb, prev), pl.BlockSpec(vb, cur), pl.BlockSpec(vb, nxt),
                  pl.BlockSpec((1, KV_W, lc), lambda bi, n: (bi, 0, 0)),
                  pl.BlockSpec((1, lc, VEXT_W), lambda bi, n: (bi, 0, 0))],
        out_specs=pl.BlockSpec((1, TQ, ATTN_W), cur),
        out_shape=jax.ShapeDtypeStruct((b, l, ATTN_W), BF16),
        compiler_params=_params("arbitrary", "arbitrary"),
        name="attention",
    )(sink, q, kt, kt, kt, v, v, v, kxt, vx)


def _shift_rows(a, d):
    n = a.shape[0]
    return pltpu.roll(a, (-d) % n, 0)


def _merge_kernel(ao_ref, u_ref, up_ref, un_ref, sa_ref, sp_ref, x_ref, mod_ref,
                  wpool_ref, pscale_ref, wab_ref, wpb_ref, wout_ref, g2_ref, wr_ref, br_ref,
                  x1_ref, h2_ref, idx_ref, gate_ref, rank_ref, cnt_ref, carry_ref, *, seq_len):
    bi = pl.program_id(0)
    i = pl.program_id(1)
    last = pl.num_programs(1) - 1
    tm, sub = TM_MERGE, MERGE_SUB

    @pl.when((bi == 0) & (i == 0))
    def _():
        carry_ref[...] = jnp.zeros_like(carry_ref)

    u = u_ref[0]
    prev = jnp.where(i > 0, up_ref[0], 0.0)
    nxt = jnp.where(i < last, un_ref[0], 0.0)
    ext = jnp.concatenate([prev, u, nxt], axis=0)

    def pool_diff(r0):
        t = i * tm + r0 + lax.broadcasted_iota(jnp.int32, (sub, 1), 0)
        diffs = []
        for g, w in enumerate(POOL_WINDOWS):
            cs = slice(g * POOL_GROUP_W, (g + 1) * POOL_GROUP_W)
            e = ext[r0:r0 + sub + 2 * POOL_HALO, cs]
            acc = _shift_rows(e, -1) + e
            step = 1
            while 2 * step < w:
                acc = _shift_rows(acc, -step) + _shift_rows(acc, step)
                step *= 2
            win = acc[POOL_HALO:POOL_HALO + sub]
            half = w // 2
            cnt = (jnp.minimum(t + half, seq_len) - jnp.maximum(t - half, 0)).astype(F32)
            diffs.append((win / cnt - u[r0:r0 + sub, cs]).astype(BF16))
        return diffs

    def branches(r0, diffs):
        rs = slice(r0, r0 + sub)
        a = jnp.dot(ao_ref[0, rs, :], wab_ref[...], preferred_element_type=F32)
        mixed = [jnp.dot(df, wpool_ref[g], preferred_element_type=F32) for g, df in enumerate(diffs)]
        pool_o = jnp.concatenate(mixed, axis=1) * pscale_ref[...]
        p = jnp.dot(pool_o.astype(BF16), wpb_ref[...], preferred_element_type=F32)
        merged = sa_ref[0, rs, :].astype(F32) * a + sp_ref[0, rs, :].astype(F32) * p
        return merged.astype(BF16)

    def out_proj(merged):
        return jnp.dot(merged, wout_ref[...], preferred_element_type=F32)

    def finish(r0, o):
        rs = slice(r0, r0 + sub)
        x1 = x_ref[0, rs, :] + mod_ref[0, 2:3, :] * o
        x1_ref[0, rs, :] = x1
        h2 = _norm_mod(x1, g2_ref[...], mod_ref[0, 3:4, :], mod_ref[0, 4:5, :])
        h2_ref[rs, :] = _pack_rows(h2)
        return jnp.dot(h2.astype(BF16), wr_ref[...], preferred_element_type=F32) + br_ref[...]

    def route(r0, logits, carry):
        lt = logits.T
        eid = lax.broadcasted_iota(jnp.int32, (N_EXPERTS, sub), 0).astype(F32)
        work = lt
        vals, idxs, sels = [], [], []
        for _ in range(TOP_K):
            mx = jnp.max(work, axis=0, keepdims=True)
            ix = jnp.min(jnp.where(work == mx, eid, float(N_EXPERTS)), axis=0, keepdims=True)
            sel = eid == ix
            work = jnp.where(sel, -jnp.inf, work)
            vals.append(mx)
            idxs.append(ix)
            sels.append(sel)
        ex = [jnp.exp(v - vals[0]) for v in vals]
        tot = ex[0] + ex[1] + ex[2] + ex[3]
        pad = [jnp.zeros_like(tot)] * (SUBLANES - TOP_K)
        gates = jnp.concatenate([e_ / tot for e_ in ex] + pad, axis=0)
        gate_ref[r0:r0 + sub, :] = gates.T[:, :TOP_K]
        idx_ref[:, r0:r0 + sub] = jnp.concatenate(idxs, axis=0).astype(jnp.int32)
        member = (sels[0] | sels[1] | sels[2] | sels[3]).astype(F32)
        t_row = lax.broadcasted_iota(jnp.int32, (sub, sub), 0)
        t_col = lax.broadcasted_iota(jnp.int32, (sub, sub), 1)
        earlier = (t_row < t_col).astype(BF16)
        before = jnp.dot(member.astype(BF16), earlier, preferred_element_type=F32) + carry
        ranks = [jnp.sum(jnp.where(s, before, 0.0), axis=0, keepdims=True) for s in sels]
        rank_ref[:, r0:r0 + sub] = jnp.concatenate(ranks, axis=0).astype(jnp.int32)
        return carry + jnp.sum(member, axis=1, keepdims=True)

    starts = list(range(0, tm, sub))
    carry = carry_ref[...]
    merged = branches(starts[0], pool_diff(starts[0]))
    for si, r0 in enumerate(starts):
        o = out_proj(merged)
        if si + 1 < len(starts):
            merged = branches(starts[si + 1], pool_diff(starts[si + 1]))
        carry = route(r0, finish(r0, o), carry)
    carry_ref[...] = carry
    cnt_ref[...] = carry


def _merge_route(attn_o, u, ga, gp, x, mod, w_pool_bf, pool_scale, wab, wpb, wout,
                 norm2_g, w_router, b_router):
    b, l, d = x.shape
    tm = TM_MERGE
    nt = l // tm
    n = b * l
    hb = tm // POOL_HALO
    row = lambda bi, i: (bi, i, 0)
    flat = lambda bi, i: (bi * nt + i, 0)
    halo_prev = lambda bi, i: (bi, jnp.maximum(i * hb - 1, 0), 0)
    halo_next = lambda bi, i: (bi, jnp.minimum((i + 1) * hb, l // POOL_HALO - 1), 0)
    slot_major = lambda bi, i: (0, bi * nt + i)
    out_shape = (
        jax.ShapeDtypeStruct((b, l, d), F32),
        jax.ShapeDtypeStruct((n, ROW_WORDS), U32),
        jax.ShapeDtypeStruct((TOP_K, n), jnp.int32),
        jax.ShapeDtypeStruct((n, TOP_K), F32),
        jax.ShapeDtypeStruct((TOP_K, n), jnp.int32),
        jax.ShapeDtypeStruct((N_EXPERTS, 1), F32),
    )
    return pl.pallas_call(
        functools.partial(_merge_kernel, seq_len=l),
        grid=(b, nt),
        in_specs=[pl.BlockSpec((1, tm, d), row),
                  pl.BlockSpec((1, tm, d), row),
                  pl.BlockSpec((1, POOL_HALO, d), halo_prev),
                  pl.BlockSpec((1, POOL_HALO, d), halo_next),
                  pl.BlockSpec((1, tm, d), row),
                  pl.BlockSpec((1, tm, d), row),
                  pl.BlockSpec((1, tm, d), row),
                  pl.BlockSpec((1, 6, d), lambda bi, i: (bi, 0, 0)),
                  _const_spec(w_pool_bf.shape),
                  _const_spec((1, d)),
                  _const_spec((d, d)), _const_spec((d, d)), _const_spec((d, d)),
                  _const_spec((1, d)),
                  _const_spec((d, N_EXPERTS)),
                  _const_spec((1, N_EXPERTS))],
        out_specs=(pl.BlockSpec((1, tm, d), row),
                   pl.BlockSpec((tm, ROW_WORDS), flat),
                   pl.BlockSpec((TOP_K, tm), slot_major),
                   pl.BlockSpec((tm, TOP_K), flat),
                   pl.BlockSpec((TOP_K, tm), slot_major),
                   _const_spec((N_EXPERTS, 1))),
        out_shape=out_shape,
        scratch_shapes=[pltpu.VMEM((N_EXPERTS, 1), F32)],
        compiler_params=_params("arbitrary", "arbitrary"),
        name="merge_route",
    )(attn_o, u, u, u, ga, gp, x, mod, w_pool_bf, pool_scale, wab, wpb, wout,
      norm2_g, w_router, b_router)


SC_CORES = 2
SC_SUBCORES = 16
SC_WORKERS = SC_CORES * SC_SUBCORES
SC_WINDOW = 128
SC_GATHER_WINDOW = 64


def _sc_worker_id():
    return lax.axis_index("s") * SC_CORES + lax.axis_index("c")


def _sc_scatter_rows(rows, dest, pad_dest, out_rows):
    n, w = rows.shape
    n_win = n // (SC_WORKERS * SC_WINDOW)
    n_pad_win = pad_dest.shape[1]
    assert dest.shape == (TOP_K, n) and SC_WORKERS * n_win * SC_WINDOW == n
    mesh = plsc.VectorSubcoreMesh(core_axis_name="c", subcore_axis_name="s")
    zeros = jnp.zeros((SC_WINDOW, w), rows.dtype)

    @functools.partial(
        pl.kernel, mesh=mesh,
        out_type=jax.ShapeDtypeStruct((out_rows, w), rows.dtype),
        scratch_types=[pltpu.VMEM((TOP_K, SC_WINDOW), jnp.int32),
                       pltpu.VMEM((n_pad_win, SC_WINDOW), jnp.int32),
                       pltpu.VMEM((SC_WINDOW, w), rows.dtype),
                       pltpu.SemaphoreType.DMA],
        name="sc_scatter_rows",
    )
    def scatter(rows_hbm, dest_hbm, pad_hbm, zeros_hbm, out_hbm, idx_v, pad_v, rows_v, sem):
        wid = _sc_worker_id()

        @pl.loop(0, n_win)
        def _(j):
            base = (wid * n_win + j) * SC_WINDOW
            pltpu.sync_copy(rows_hbm.at[pl.ds(base, SC_WINDOW)], rows_v)
            pltpu.sync_copy(dest_hbm.at[:, pl.ds(base, SC_WINDOW)], idx_v)
            copies = [pltpu.async_copy(rows_v, out_hbm.at[idx_v.at[k]], sem) for k in range(TOP_K)]
            for cp in copies:
                cp.wait()

        pltpu.sync_copy(zeros_hbm, rows_v)
        pltpu.sync_copy(pad_hbm.at[wid], pad_v)
        copies = [pltpu.async_copy(rows_v, out_hbm.at[pad_v.at[j]], sem) for j in range(n_pad_win)]
        for cp in copies:
            cp.wait()

    return scatter(rows, dest, pad_dest, zeros)


def _expert_kernel(be_ref, first_ref, next_ref, valid_ref, nu_ref, x_ref, wgu_hbm, bgu_ref, wd_hbm, bd_ref,
                   y_ref, wgu_f, wd_f, wgu_s, wd_s, sem):
    step = pl.program_id(0)

    def weight_copies(e):
        return (pltpu.make_async_copy(wgu_hbm.at[e], wgu_f, sem.at[0]),
                pltpu.make_async_copy(wd_hbm.at[e], wd_f, sem.at[1]))

    def start_weights(e):
        for cp in weight_copies(e):
            cp.start(priority=1)

    @pl.when(step == 0)
    def _():
        start_weights(be_ref[0])

    for sub in range(EXPERT_BLOCKS_PER_STEP):
        _expert_block(step * EXPERT_BLOCKS_PER_STEP + sub,
                      slice(sub * TM_EXPERT, (sub + 1) * TM_EXPERT),
                      be_ref, first_ref, next_ref, valid_ref, nu_ref, x_ref, bgu_ref, bd_ref, y_ref,
                      wgu_f, wd_f, wgu_s, wd_s, weight_copies, start_weights)


def _expert_block(i, rows, be_ref, first_ref, next_ref, valid_ref, nu_ref, x_ref, bgu_ref, bd_ref, y_ref,
                  wgu_f, wd_f, wgu_s, wd_s, weight_copies, start_weights):
    @pl.when(i < nu_ref[0])
    def _():
        e = be_ref[i]

        @pl.when(first_ref[i] == 1)
        def _():
            for cp in weight_copies(e):
                cp.wait()
            ck = 256
            for r0 in range(0, D_MODEL, ck):
                wgu_s[r0:r0 + ck, :] = wgu_f[r0:r0 + ck, :].astype(BF16)
                wd_s[r0:r0 + ck, :] = wd_f[r0:r0 + ck, :].astype(BF16)

            @pl.when(next_ref[i] >= 0)
            def _():
                start_weights(next_ref[i])

        def ffn(m):
            r = slice(rows.start, rows.start + m)
            x = _unpack_rows(x_ref[r, :]).astype(BF16)
            bgu = bgu_ref[e]
            gt = jnp.dot(x, wgu_s[:, :D_FF], preferred_element_type=F32) + bgu[:, :D_FF]
            ln = jnp.dot(x, wgu_s[:, D_FF:], preferred_element_type=F32) + bgu[:, D_FF:]
            gt = jnp.minimum(gt, SWIGLU_LIMIT)
            ln = jnp.clip(ln, -SWIGLU_LIMIT, SWIGLU_LIMIT)
            act = gt * jax.nn.sigmoid(SWIGLU_ALPHA * gt) * (ln + 1.0)
            y = jnp.dot(act.astype(BF16), wd_s[...], preferred_element_type=F32) + bd_ref[e]
            y_ref[r, :] = _pack_rows(y)
            if m < TM_EXPERT:
                y_ref[rows.start + m:rows.stop, :] = jnp.zeros((TM_EXPERT - m, ROW_WORDS), U32)

        quanta = (valid_ref[i] + EXPERT_ROW_QUANTUM - 1) // EXPERT_ROW_QUANTUM
        for q in range(1, TM_EXPERT // EXPERT_ROW_QUANTUM + 1):
            pl.when(quanta == q)(functools.partial(ffn, q * EXPERT_ROW_QUANTUM))

    @pl.when(i >= nu_ref[0])
    def _():
        y_ref[rows, :] = jnp.zeros((TM_EXPERT, ROW_WORDS), U32)


def _experts(block_e, first, next_e, valid, n_used, xs, w_gu, b_gu, w_down, b_down):
    d = D_MODEL
    nb = block_e.shape[0]
    per_step = EXPERT_BLOCKS_PER_STEP
    tm = TM_EXPERT * per_step
    assert nb % per_step == 0

    xmap = lambda i, be, fi, nx, va, nu: (jnp.minimum(i, (nu[0] - 1) // per_step), 0)
    grid_spec = pltpu.PrefetchScalarGridSpec(
        num_scalar_prefetch=5,
        grid=(nb // per_step,),
        in_specs=[pl.BlockSpec((tm, ROW_WORDS), xmap),
                  pl.BlockSpec(memory_space=pl.ANY),
                  _const_spec((N_EXPERTS, 1, 2 * D_FF)),
                  pl.BlockSpec(memory_space=pl.ANY),
                  _const_spec((N_EXPERTS, 1, d))],
        out_specs=pl.BlockSpec((tm, ROW_WORDS), lambda i, be, fi, nx, va, nu: (i, 0)),
        scratch_shapes=[pltpu.VMEM((d, 2 * D_FF), F32), pltpu.VMEM((D_FF, d), F32),
                        pltpu.VMEM((d, 2 * D_FF), BF16), pltpu.VMEM((D_FF, d), BF16),
                        pltpu.SemaphoreType.DMA((2,))],
    )
    return pl.pallas_call(
        _expert_kernel,
        grid_spec=grid_spec,
        out_shape=jax.ShapeDtypeStruct((nb * TM_EXPERT, ROW_WORDS), U32),
        compiler_params=_params("arbitrary"),
        name="moe_experts",
    )(block_e, first, next_e, valid, n_used, xs, w_gu, b_gu.reshape(N_EXPERTS, 1, 2 * D_FF),
      w_down, b_down.reshape(N_EXPERTS, 1, d))


def _sc_gather_rows(table, idx):
    m = idx.shape[0]
    w = table.shape[1]
    win = SC_GATHER_WINDOW
    per_worker = m // SC_WORKERS
    n_win = per_worker // win
    assert per_worker * SC_WORKERS == m and n_win * win == per_worker and n_win % 2 == 0
    mesh = plsc.VectorSubcoreMesh(core_axis_name="c", subcore_axis_name="s")

    @functools.partial(
        pl.kernel, mesh=mesh,
        out_type=jax.ShapeDtypeStruct((m, w), table.dtype),
        scratch_types=[pltpu.VMEM((win,), jnp.int32), pltpu.VMEM((win,), jnp.int32),
                       pltpu.VMEM((win, w), table.dtype), pltpu.VMEM((win, w), table.dtype),
                       pltpu.SemaphoreType.DMA((4,))],
        name="sc_gather_rows",
    )
    def gather(table_hbm, idx_hbm, out_hbm, idx0, idx1, rows0, rows1, sems):
        wid = _sc_worker_id()
        idxs, rows = (idx0, idx1), (rows0, rows1)

        def out_window(j):
            return out_hbm.at[pl.ds(wid * per_worker + j * win, win)]

        def start_gather(j, slot):
            pltpu.sync_copy(idx_hbm.at[pl.ds(wid * per_worker + j * win, win)], idxs[slot])
            pltpu.async_copy(table_hbm.at[idxs[slot]], rows[slot], sems.at[slot])

        def wait_gather(slot):
            pltpu.make_async_copy(table_hbm.at[idxs[slot]], rows[slot], sems.at[slot]).wait()

        def wait_write(j, slot):
            pltpu.make_async_copy(rows[slot], out_window(j), sems.at[2 + slot]).wait()

        start_gather(0, 0)

        @pl.loop(0, n_win, step=2)
        def _(j0):
            for slot in range(2):
                j = j0 + slot
                other = 1 - slot

                @pl.when(j >= 1)
                def _():
                    wait_write(j - 1, other)

                @pl.when(j + 1 < n_win)
                def _():
                    start_gather(j + 1, other)

                wait_gather(slot)
                pltpu.async_copy(rows[slot], out_window(j), sems.at[2 + slot])

        wait_write(n_win - 1, (n_win - 1) % 2)

    return gather(table, idx)


def _combine_dense_kernel(y4_ref, gate_ref, x1_ref, mod_ref, fg_ref, o_ref):
    gate = gate_ref[...]
    y = gate[:, 0:1] * _unpack_rows(y4_ref[0])
    for k in range(1, TOP_K):
        y = y + gate[:, k:k + 1] * _unpack_rows(y4_ref[k])
    x2 = x1_ref[...] + mod_ref[0, 5:6, :] * y
    o_ref[...] = x2 * lax.rsqrt(jnp.mean(x2 * x2, axis=-1, keepdims=True) + NORM_EPS) * fg_ref[...]


def _combine_dense(y4, gate_w, x1, mod, final_g):
    b, l, d = x1.shape
    ts = 512
    nt = l // ts
    return pl.pallas_call(
        _combine_dense_kernel,
        grid=(b * nt,),
        in_specs=[pl.BlockSpec((TOP_K, ts, ROW_WORDS), lambda s: (0, s, 0)),
                  pl.BlockSpec((ts, TOP_K), lambda s: (s, 0)),
                  pl.BlockSpec((ts, d), lambda s: (s, 0)),
                  pl.BlockSpec((1, 6, d), lambda s: (s // nt, 0, 0)),
                  _const_spec((1, d))],
        out_specs=pl.BlockSpec((ts, d), lambda s: (s, 0)),
        out_shape=jax.ShapeDtypeStruct((b * l, d), F32),
        compiler_params=_params("arbitrary"),
        name="moe_combine",
    )(y4, gate_w, x1.reshape(b * l, d), mod, final_g).reshape(b, l, d)


def _rope_tables(seq_len):
    inv_freq = ROPE_BASE ** (-jnp.arange(ROPE_PAIRS, dtype=F32) / ROPE_PAIRS)
    reps = LANES // HEAD_DIM
    out = []
    for n_pos, on_row in ((seq_len // GRID_W, True), (GRID_W, False)):
        ang = jnp.arange(n_pos, dtype=F32)[:, None] * inv_freq
        zero = jnp.zeros_like(ang)
        for fn, sign in ((jnp.cos, 1.0), (jnp.sin, -1.0)):
            v = fn(ang)
            head = ([sign * v, v, zero, zero] if on_row else [zero, zero, sign * v, v])
            out.append(jnp.concatenate(head * reps, axis=1))
    return out


def kernel(x, c, ctx, c_ctx, w_ada, b_ada, norm1_g, norm2_g, w_in, b_in, attn_sink, w_pool,
           pool_scale, w_attn_br, w_pool_br, w_out, w_router, b_router, w_gu, b_gu, w_down,
           b_down, final_g):
    b, l, d = x.shape
    n = b * l
    assert w_ada.shape[0] == 1, "single-layer block"

    cond = jnp.zeros((SUBLANES, d), F32).at[:b].set(c).at[b].set(c_ctx)
    mod = _adaln(cond, w_ada[0], b_ada[0])[:b + 1].reshape(b + 1, 6, d)
    mod_x, mod_c = mod[:b], mod[b:b + 1]

    rope = _rope_tables(l)
    v0 = ATTN_W + KV_W
    w_v = w_in[0][:, v0:v0 + KV_W].reshape(d, N_KV_HEADS, HEAD_DIM)
    w_v = jnp.concatenate([w_v, jnp.zeros_like(w_v)], axis=-1).reshape(d, VEXT_W)
    b_v = b_in[0][v0:v0 + KV_W].reshape(N_KV_HEADS, HEAD_DIM)
    b_v = jnp.concatenate([b_v, jnp.ones_like(b_v)], axis=-1).reshape(VEXT_W)
    w_in_bf = jnp.concatenate([w_in[0][:, :v0], w_v, w_in[0][:, v0 + KV_W:]], axis=1).astype(BF16)
    b_in2 = jnp.concatenate([b_in[0][:v0], b_v, b_in[0][v0 + KV_W:]]).reshape(1, IN_W_EXT)
    g1 = norm1_g[0].reshape(1, d)
    q, kt, v, u, ga, gp = _inproj(x, mod_x, g1, w_in_bf, b_in2, rope)
    kv_sl = slice(ATTN_W, ATTN_W + KV_W + VEXT_W)
    kxt, vx = _ctx_kv(ctx, mod_c, g1, w_in_bf[:, kv_sl], b_in2[:, kv_sl])

    attn_o = _attention(q, kt, v, kxt, vx, attn_sink[0])

    x1, h2, top_idx, gate_w, rank, counts = _merge_route(
        attn_o, u, ga, gp, x, mod_x, w_pool[0].astype(BF16), pool_scale[0].reshape(1, d),
        w_attn_br[0].astype(BF16), w_pool_br[0].astype(BF16), w_out[0].astype(BF16),
        norm2_g[0].reshape(1, d), w_router[0].astype(BF16), b_router[0].reshape(1, N_EXPERTS))

    tm = TM_EXPERT
    nb = n * TOP_K // tm + N_EXPERTS
    cnt = counts[:, 0].astype(jnp.int32)
    padded = (cnt + tm - 1) // tm * tm
    pend = jnp.cumsum(padded)
    pstart = pend - padded
    expert_ids = jnp.arange(N_EXPERTS, dtype=jnp.int32)
    dest = jnp.sum(jnp.where(top_idx[..., None] == expert_ids, pstart, 0), axis=-1) + rank
    n_used = (pend[-1] // tm).reshape(1)
    block_start = jnp.arange(nb, dtype=jnp.int32) * tm
    block_e = jnp.minimum(jnp.sum((pend[None, :] <= block_start[:, None]).astype(jnp.int32), axis=1),
                          N_EXPERTS - 1)
    first = jnp.concatenate([jnp.ones((1,), jnp.int32),
                             (block_e[1:] != block_e[:-1]).astype(jnp.int32)])
    later_used = (expert_ids[None, :] > expert_ids[:, None]) & (cnt[None, :] > 0)
    next_of_expert = jnp.min(jnp.where(later_used, expert_ids[None, :], N_EXPERTS), axis=1)
    next_of_expert = jnp.where(next_of_expert < N_EXPERTS, next_of_expert, -1)
    next_e = jnp.sum(jnp.where(block_e[:, None] == expert_ids, next_of_expert, 0), axis=-1)

    pad_j = jnp.arange(tm, dtype=jnp.int32)[None, :]
    spare = nb * tm + expert_ids[:, None] * tm + pad_j
    pad_dest = jnp.where(pad_j < (padded - cnt)[:, None], (pstart + cnt)[:, None] + pad_j, spare)
    pad_dest = pad_dest.reshape(SC_WORKERS, N_EXPERTS * tm // (SC_WORKERS * SC_WINDOW), SC_WINDOW)

    xs = _sc_scatter_rows(h2, dest, pad_dest, (nb + N_EXPERTS) * tm)
    used_end = jnp.sum(jnp.where(block_e[:, None] == expert_ids, pstart + cnt, 0), axis=-1)
    valid = jnp.clip(used_end - block_start, 0, tm)
    ys = _experts(block_e, first, next_e, valid, n_used, xs, w_gu[0], b_gu[0], w_down[0], b_down[0])
    y4 = _sc_gather_rows(ys, dest.reshape(-1)).reshape(TOP_K, n, ROW_WORDS)
    return _combine_dense(y4, gate_w, x1, mod_x, final_g.reshape(1, d))
```

```python
import functools

import jax
import jax.numpy as jnp
from jax import lax
from jax.experimental import pallas as pl
from jax.experimental.pallas import tpu as pltpu
from jax.experimental.pallas import tpu_sc as plsc

D_MODEL = 1024
GRID_W = 64
HEAD_DIM = 64
N_HEADS = 16
N_KV_HEADS = 4
GROUP = N_HEADS // N_KV_HEADS
ATTN_W = N_HEADS * HEAD_DIM
KV_W = N_KV_HEADS * HEAD_DIM
WINDOW = 128
ATTN_SCALE = HEAD_DIM ** -0.5
ROPE_BASE = 10000.0
ROPE_PAIRS = HEAD_DIM // 4
POOL_WINDOWS = (2, 4, 8, 16)
POOL_GROUP_W = D_MODEL // len(POOL_WINDOWS)
IN_W = ATTN_W + 2 * KV_W + D_MODEL + 2 * D_MODEL
VEXT_W = N_KV_HEADS * 2 * HEAD_DIM
IN_W_EXT = IN_W - KV_W + VEXT_W
N_EXPERTS = 32
TOP_K = 4
D_FF = D_MODEL
SWIGLU_ALPHA = 1.702
SWIGLU_LIMIT = 7.0
NORM_EPS = 1e-5
NEG_INF = -1e30

LANES = 128
SUBLANES = 8
VMEM_LIMIT = 56 * 1024 * 1024

TM_INPROJ = 512
TQ = WINDOW
TM_MERGE = 512
MERGE_SUB = 256
TM_EXPERT = 512
EXPERT_ROW_QUANTUM = 128
EXPERT_BLOCKS_PER_STEP = 1
POOL_HALO = SUBLANES

F32 = jnp.float32
BF16 = jnp.bfloat16


ROW_WORDS = D_MODEL // 2
U32 = jnp.uint32


def _pack_rows(value):
    bits = lax.bitcast_convert_type(value.astype(BF16).astype(F32), U32)
    return bits[:, :ROW_WORDS] | (bits[:, ROW_WORDS:] >> 16)


def _unpack_rows(words):
    hi = lax.bitcast_convert_type(words & jnp.uint32(0xFFFF0000), F32)
    lo = lax.bitcast_convert_type(words << 16, F32)
    return jnp.concatenate([hi, lo], axis=1)


def _params(*sem):
    return pltpu.CompilerParams(dimension_semantics=sem, vmem_limit_bytes=VMEM_LIMIT)


def _const_spec(shape):
    nd = len(shape)
    return pl.BlockSpec(shape, lambda *_: (0,) * nd)


def _adaln_kernel(c_ref, w_ref, b_ref, o_ref):
    c = c_ref[...]
    s = c * jax.nn.sigmoid(c)
    o_ref[...] = jnp.dot(s.astype(BF16), w_ref[...].astype(BF16),
                         preferred_element_type=F32) + b_ref[...]


def _adaln(cond, w_ada, b_ada):
    rows, d = cond.shape
    n = w_ada.shape[1]
    tn = 1024
    return pl.pallas_call(
        _adaln_kernel,
        grid=(n // tn,),
        in_specs=[_const_spec((rows, d)),
                  pl.BlockSpec((d, tn), lambda j: (0, j)),
                  pl.BlockSpec((1, tn), lambda j: (0, j))],
        out_specs=pl.BlockSpec((rows, tn), lambda j: (0, j)),
        out_shape=jax.ShapeDtypeStruct((rows, n), F32),
        compiler_params=_params("arbitrary"),
        name="adaln",
    )(cond, w_ada, b_ada.reshape(1, n))


def _norm_mod(x, g, shift, scale):
    y = x * lax.rsqrt(jnp.mean(x * x, axis=-1, keepdims=True) + NORM_EPS) * g
    return y * (1.0 + scale) + shift


def _rope(t, cos, sin_signed):
    w = t.shape[1]
    half = ROPE_PAIRS
    lane = lax.broadcasted_iota(jnp.int32, t.shape, 1)
    first = (lane & (2 * half - 1)) < half
    rot = jnp.where(first, pltpu.roll(t, w - half, 1), pltpu.roll(t, half, 1))
    reps = w // LANES
    cos_w = jnp.concatenate([cos] * reps, axis=1) if reps > 1 else cos
    sin_w = jnp.concatenate([sin_signed] * reps, axis=1) if reps > 1 else sin_signed
    return t * cos_w + rot * sin_w


def _token_table(row_ref, col_ref):
    gr = row_ref.shape[0]
    shape = (gr, GRID_W, LANES)
    full = (jnp.broadcast_to(row_ref[...][:, None, :], shape)
            + jnp.broadcast_to(col_ref[...][None, :, :], shape))
    return full.reshape(gr * GRID_W, LANES)


def _inproj_kernel(x_ref, mod_ref, g_ref, w_ref, b_ref, cr_ref, sr_ref, cc_ref, sc_ref,
                   q_ref, k_ref, v_ref, u_ref, ga_ref, gp_ref):
    h = _norm_mod(x_ref[0], g_ref[...], mod_ref[0, 0:1, :], mod_ref[0, 1:2, :])
    hb = h.astype(BF16)
    cos = _token_table(cr_ref, cc_ref)
    sin = _token_table(sr_ref, sc_ref)

    def proj(c0, c1):
        return jnp.dot(hb, w_ref[:, c0:c1], preferred_element_type=F32) + b_ref[:, c0:c1]

    cw = 512
    chunks = []
    for c0 in range(0, ATTN_W, cw):
        def put_q(t, c0=c0):
            q_ref[0, :, c0:c0 + cw] = (_rope(t, cos, sin) * ATTN_SCALE).astype(BF16)
        chunks.append((c0, cw, put_q))

    def put_k(t):
        k_ref[0] = _rope(t, cos, sin).T.astype(BF16)

    def put_v(t):
        v_ref[0] = t.astype(BF16)

    chunks.append((ATTN_W, KV_W, put_k))
    chunks.append((ATTN_W + KV_W, VEXT_W, put_v))
    base = ATTN_W + KV_W + VEXT_W
    for c0 in range(0, D_MODEL, cw):
        def put_u(t, c0=c0):
            u_ref[0, :, c0:c0 + cw] = t
        chunks.append((base + c0, cw, put_u))
    for ref in (ga_ref, gp_ref):
        base += D_MODEL
        for c0 in range(0, D_MODEL, cw):
            def put_gate(t, ref=ref, c0=c0):
                ref[0, :, c0:c0 + cw] = jax.nn.sigmoid(t).astype(BF16)
            chunks.append((base + c0, cw, put_gate))

    pending = proj(chunks[0][0], chunks[0][0] + chunks[0][1])
    for ci, (_, _, epilogue) in enumerate(chunks):
        ready = pending
        if ci + 1 < len(chunks):
            n0, nw, _ = chunks[ci + 1]
            pending = proj(n0, n0 + nw)
        epilogue(ready)


def _inproj(x, mod, norm_g, w_in_bf, b_in, rope):
    b, l, d = x.shape
    tm = TM_INPROJ
    row = lambda bi, i: (bi, i, 0)
    out_shape = (
        jax.ShapeDtypeStruct((b, l, ATTN_W), BF16),
        jax.ShapeDtypeStruct((b, KV_W, l), BF16),
        jax.ShapeDtypeStruct((b, l, VEXT_W), BF16),
        jax.ShapeDtypeStruct((b, l, D_MODEL), F32),
        jax.ShapeDtypeStruct((b, l, D_MODEL), BF16),
        jax.ShapeDtypeStruct((b, l, D_MODEL), BF16),
    )
    return pl.pallas_call(
        _inproj_kernel,
        grid=(b, l // tm),
        in_specs=[pl.BlockSpec((1, tm, d), row),
                  pl.BlockSpec((1, 6, d), lambda bi, i: (bi, 0, 0)),
                  _const_spec((1, d)),
                  _const_spec((d, IN_W_EXT)),
                  _const_spec((1, IN_W_EXT)),
                  pl.BlockSpec((tm // GRID_W, LANES), lambda bi, i: (i, 0)),
                  pl.BlockSpec((tm // GRID_W, LANES), lambda bi, i: (i, 0)),
                  _const_spec((GRID_W, LANES)),
                  _const_spec((GRID_W, LANES))],
        out_specs=(pl.BlockSpec((1, tm, ATTN_W), row),
                   pl.BlockSpec((1, KV_W, tm), lambda bi, i: (bi, 0, i)),
                   pl.BlockSpec((1, tm, VEXT_W), row),
                   pl.BlockSpec((1, tm, D_MODEL), row),
                   pl.BlockSpec((1, tm, D_MODEL), row),
                   pl.BlockSpec((1, tm, D_MODEL), row)),
        out_shape=out_shape,
        compiler_params=_params("arbitrary", "arbitrary"),
        name="inproj",
    )(x, mod, norm_g, w_in_bf, b_in, *rope)


def _ctx_kv_kernel(x_ref, mod_ref, g_ref, w_ref, b_ref, k_ref, v_ref):
    h = _norm_mod(x_ref[0], g_ref[...], mod_ref[0, 0:1, :], mod_ref[0, 1:2, :])
    kv = jnp.dot(h.astype(BF16), w_ref[...], preferred_element_type=F32) + b_ref[...]
    k_ref[0] = kv[:, :KV_W].T.astype(BF16)
    v_ref[0] = kv[:, KV_W:].astype(BF16)


def _ctx_kv(ctx, mod_c, norm_g, w_kv_bf, b_kv):
    b, lc, d = ctx.shape
    row = lambda bi: (bi, 0, 0)
    return pl.pallas_call(
        _ctx_kv_kernel,
        grid=(b,),
        in_specs=[pl.BlockSpec((1, lc, d), row),
                  _const_spec((1, 6, d)),
                  _const_spec((1, d)),
                  _const_spec((d, KV_W + VEXT_W)),
                  _const_spec((1, KV_W + VEXT_W))],
        out_specs=(pl.BlockSpec((1, KV_W, lc), row), pl.BlockSpec((1, lc, VEXT_W), row)),
        out_shape=(jax.ShapeDtypeStruct((b, KV_W, lc), BF16),
                   jax.ShapeDtypeStruct((b, lc, VEXT_W), BF16)),
        compiler_params=_params("arbitrary"),
        name="ctx_kv",
    )(ctx, mod_c, norm_g, w_kv_bf, b_kv)


def _attn_kernel(sink_ref, q_ref, kp_ref, kc_ref, kn_ref, vp_ref, vc_ref, vn_ref,
                 kx_ref, vx_ref, o_ref):
    n = pl.program_id(1)
    last = pl.num_programs(1) - 1
    rows = GROUP * TQ
    pair_w = 2 * HEAD_DIM
    qi = lax.broadcasted_iota(jnp.int32, (rows, TQ), 0) & (TQ - 1)
    col = lax.broadcasted_iota(jnp.int32, (rows, TQ), 1)
    keep_prev = (col >= qi) & (n > 0)
    keep_next = (col <= qi) & (n < last)
    low_half = lax.broadcasted_iota(jnp.int32, (TQ, pair_w), 1) < HEAD_DIM
    row_id = lax.broadcasted_iota(jnp.int32, (rows, 1), 0)

    def scores(j):
        parts = []
        for g in range(GROUP):
            h = j * GROUP + g
            pair = q_ref[0, :, (h // 2) * pair_w:(h // 2 + 1) * pair_w]
            parts.append(jnp.where(low_half if h % 2 == 0 else ~low_half, pair, jnp.zeros_like(pair)))
        q4 = jnp.concatenate(parts, axis=0)
        ks = slice(j * HEAD_DIM, (j + 1) * HEAD_DIM)
        kt = jnp.concatenate([kp_ref[0, ks, :], kc_ref[0, ks, :], kn_ref[0, ks, :]], axis=1)
        s_loc = jnp.dot(q4, jnp.concatenate([kt, kt], axis=0), preferred_element_type=F32)
        kxt = kx_ref[0, ks, :]
        s_ctx = jnp.dot(q4, jnp.concatenate([kxt, kxt], axis=0), preferred_element_type=F32)
        return s_loc, s_ctx

    def probs(j, s_loc, s_ctx):
        pieces = [jnp.where(keep_prev, s_loc[:, :TQ], NEG_INF), s_loc[:, TQ:2 * TQ],
                  jnp.where(keep_next, s_loc[:, 2 * TQ:], NEG_INF)]
        pieces += [s_ctx[:, c0:c0 + TQ] for c0 in range(0, s_ctx.shape[1], TQ)]
        sink = jnp.zeros((rows, 1), F32)
        for g in range(GROUP):
            sink = jnp.where(row_id // TQ == g, sink_ref[j * GROUP + g], sink)
        mx = pieces[0]
        for pc in pieces[1:]:
            mx = jnp.maximum(mx, pc)
        m = jnp.maximum(jnp.max(mx, axis=-1, keepdims=True), sink)
        p = jnp.concatenate([jnp.exp(pc - m).astype(BF16) for pc in pieces], axis=1)
        return p, jnp.exp(sink - m)

    def output(j, p, sink_p):
        vs = slice(j * pair_w, (j + 1) * pair_w)
        v_all = jnp.concatenate([vp_ref[0, :, vs], vc_ref[0, :, vs], vn_ref[0, :, vs],
                                 vx_ref[0, :, vs]], axis=0)
        r = jnp.dot(p, v_all, preferred_element_type=F32)
        den = pltpu.roll(r, HEAD_DIM, 1) + sink_p
        o = r / den
        for g in range(0, GROUP, 2):
            even = o[g * TQ:(g + 1) * TQ]
            odd = pltpu.roll(o[(g + 1) * TQ:(g + 2) * TQ], HEAD_DIM, 1)
            c0 = (j * GROUP + g) * HEAD_DIM
            o_ref[0, :, c0:c0 + pair_w] = jnp.where(low_half, even, odd).astype(BF16)

    s_queue = [scores(0), scores(1)]
    p_queue = [probs(0, *s_queue.pop(0))]
    for j in range(N_KV_HEADS):
        if j + 2 < N_KV_HEADS:
            s_queue.append(scores(j + 2))
        ready = p_queue.pop(0)
        if j + 1 < N_KV_HEADS:
            p_queue.append(probs(j + 1, *s_queue.pop(0)))
        output(j, *ready)


def _attention(q, kt, v, kxt, vx, sink):
    b, l, _ = q.shape
    lc = kxt.shape[2]
    nblk = l // TQ
    cur = lambda bi, n: (bi, n, 0)
    prev = lambda bi, n: (bi, jnp.maximum(n - 1, 0), 0)
    nxt = lambda bi, n: (bi, jnp.minimum(n + 1, nblk - 1), 0)
    kcur = lambda bi, n: (bi, 0, n)
    kprev = lambda bi, n: (bi, 0, jnp.maximum(n - 1, 0))
    knxt = lambda bi, n: (bi, 0, jnp.minimum(n + 1, nblk - 1))
    kb = (1, KV_W, TQ)
    vb = (1, TQ, VEXT_W)
    return pl.pallas_call(
        _attn_kernel,
        grid=(b, nblk),
        in_specs=[pl.BlockSpec(memory_space=pltpu.SMEM),
                  pl.BlockSpec((1, TQ, ATTN_W), cur),
                  pl.BlockSpec(kb, kprev), pl.BlockSpec(kb, kcur), pl.BlockSpec(kb, knxt),
                  pl.BlockSpec(vb, prev), pl.BlockSpec(vb, cur), pl.BlockSpec(vb, nxt),
                  pl.BlockSpec((1, KV_W, lc), lambda bi, n: (bi, 0, 0)),
                  pl.BlockSpec((1, lc, VEXT_W), lambda bi, n: (bi, 0, 0))],
        out_specs=pl.BlockSpec((1, TQ, ATTN_W), cur),
        out_shape=jax.ShapeDtypeStruct((b, l, ATTN_W), BF16),
        compiler_params=_params("arbitrary", "arbitrary"),
        name="attention",
    )(sink, q, kt, kt, kt, v, v, v, kxt, vx)


def _shift_rows(a, d):
    n = a.shape[0]
    return pltpu.roll(a, (-d) % n, 0)


def _merge_kernel(ao_ref, u_ref, up_ref, un_ref, sa_ref, sp_ref, x_ref, mod_ref,
                  wpool_ref, pscale_ref, wab_ref, wpb_ref, wout_ref, g2_ref, wr_ref, br_ref,
                  x1_ref, h2_ref, idx_ref, gate_ref, rank_ref, cnt_ref, carry_ref, *, seq_len):
    bi = pl.program_id(0)
    i = pl.program_id(1)
    last = pl.num_programs(1) - 1
    tm, sub = TM_MERGE, MERGE_SUB

    @pl.when((bi == 0) & (i == 0))
    def _():
        carry_ref[...] = jnp.zeros_like(carry_ref)

    u = u_ref[0]
    prev = jnp.where(i > 0, up_ref[0], 0.0)
    nxt = jnp.where(i < last, un_ref[0], 0.0)
    ext = jnp.concatenate([prev, u, nxt], axis=0)

    def pool_diff(r0):
        t = i * tm + r0 + lax.broadcasted_iota(jnp.int32, (sub, 1), 0)
        diffs = []
        for g, w in enumerate(POOL_WINDOWS):
            cs = slice(g * POOL_GROUP_W, (g + 1) * POOL_GROUP_W)
            e = ext[r0:r0 + sub + 2 * POOL_HALO, cs]
            acc = _shift_rows(e, -1) + e
            step = 1
            while 2 * step < w:
                acc = _shift_rows(acc, -step) + _shift_rows(acc, step)
                step *= 2
            win = acc[POOL_HALO:POOL_HALO + sub]
            half = w // 2
            cnt = (jnp.minimum(t + half, seq_len) - jnp.maximum(t - half, 0)).astype(F32)
            diffs.append((win / cnt - u[r0:r0 + sub, cs]).astype(BF16))
        return diffs

    def branches(r0, diffs):
        rs = slice(r0, r0 + sub)
        a = jnp.dot(ao_ref[0, rs, :], wab_ref[...], preferred_element_type=F32)
        mixed = [jnp.dot(df, wpool_ref[g], preferred_element_type=F32) for g, df in enumerate(diffs)]
        pool_o = jnp.concatenate(mixed, axis=1) * pscale_ref[...]
        p = jnp.dot(pool_o.astype(BF16), wpb_ref[...], preferred_element_type=F32)
        merged = sa_ref[0, rs, :].astype(F32) * a + sp_ref[0, rs, :].astype(F32) * p
        return merged.astype(BF16)

    def out_proj(merged):
        return jnp.dot(merged, wout_ref[...], preferred_element_type=F32)

    def finish(r0, o):
        rs = slice(r0, r0 + sub)
        x1 = x_ref[0, rs, :] + mod_ref[0, 2:3, :] * o
        x1_ref[0, rs, :] = x1
        h2 = _norm_mod(x1, g2_ref[...], mod_ref[0, 3:4, :], mod_ref[0, 4:5, :])
        h2_ref[rs, :] = _pack_rows(h2)
        return jnp.dot(h2.astype(BF16), wr_ref[...], preferred_element_type=F32) + br_ref[...]

    def route(r0, logits, carry):
        lt = logits.T
        eid = lax.broadcasted_iota(jnp.int32, (N_EXPERTS, sub), 0).astype(F32)
        work = lt
        vals, idxs, sels = [], [], []
        for _ in range(TOP_K):
            mx = jnp.max(work, axis=0, keepdims=True)
            ix = jnp.min(jnp.where(work == mx, eid, float(N_EXPERTS)), axis=0, keepdims=True)
            sel = eid == ix
            work = jnp.where(sel, -jnp.inf, work)
            vals.append(mx)
            idxs.append(ix)
            sels.append(sel)
        ex = [jnp.exp(v - vals[0]) for v in vals]
        tot = ex[0] + ex[1] + ex[2] + ex[3]
        pad = [jnp.zeros_like(tot)] * (SUBLANES - TOP_K)
        gates = jnp.concatenate([e_ / tot for e_ in ex] + pad, axis=0)
        gate_ref[r0:r0 + sub, :] = gates.T[:, :TOP_K]
        idx_ref[:, r0:r0 + sub] = jnp.concatenate(idxs, axis=0).astype(jnp.int32)
        member = (sels[0] | sels[1] | sels[2] | sels[3]).astype(F32)
        t_row = lax.broadcasted_iota(jnp.int32, (sub, sub), 0)
        t_col = lax.broadcasted_iota(jnp.int32, (sub, sub), 1)
        earlier = (t_row < t_col).astype(BF16)
        before = jnp.dot(member.astype(BF16), earlier, preferred_element_type=F32) + carry
        ranks = [jnp.sum(jnp.where(s, before, 0.0), axis=0, keepdims=True) for s in sels]
        rank_ref[:, r0:r0 + sub] = jnp.concatenate(ranks, axis=0).astype(jnp.int32)
        return carry + jnp.sum(member, axis=1, keepdims=True)

    starts = list(range(0, tm, sub))
    carry = carry_ref[...]
    merged = branches(starts[0], pool_diff(starts[0]))
    for si, r0 in enumerate(starts):
        o = out_proj(merged)
        if si + 1 < len(starts):
            merged = branches(starts[si + 1], pool_diff(starts[si + 1]))
        carry = route(r0, finish(r0, o), carry)
    carry_ref[...] = carry
    cnt_ref[...] = carry


def _merge_route(attn_o, u, ga, gp, x, mod, w_pool_bf, pool_scale, wab, wpb, wout,
                 norm2_g, w_router, b_router):
    b, l, d = x.shape
    tm = TM_MERGE
    nt = l // tm
    n = b * l
    hb = tm // POOL_HALO
    row = lambda bi, i: (bi, i, 0)
    flat = lambda bi, i: (bi * nt + i, 0)
    halo_prev = lambda bi, i: (bi, jnp.maximum(i * hb - 1, 0), 0)
    halo_next = lambda bi, i: (bi, jnp.minimum((i + 1) * hb, l // POOL_HALO - 1), 0)
    slot_major = lambda bi, i: (0, bi * nt + i)
    out_shape = (
        jax.ShapeDtypeStruct((b, l, d), F32),
        jax.ShapeDtypeStruct((n, ROW_WORDS), U32),
        jax.ShapeDtypeStruct((TOP_K, n), jnp.int32),
        jax.ShapeDtypeStruct((n, TOP_K), F32),
        jax.ShapeDtypeStruct((TOP_K, n), jnp.int32),
        jax.ShapeDtypeStruct((N_EXPERTS, 1), F32),
    )
    return pl.pallas_call(
        functools.partial(_merge_kernel, seq_len=l),
        grid=(b, nt),
        in_specs=[pl.BlockSpec((1, tm, d), row),
                  pl.BlockSpec((1, tm, d), row),
                  pl.BlockSpec((1, POOL_HALO, d), halo_prev),
                  pl.BlockSpec((1, POOL_HALO, d), halo_next),
                  pl.BlockSpec((1, tm, d), row),
                  pl.BlockSpec((1, tm, d), row),
                  pl.BlockSpec((1, tm, d), row),
                  pl.BlockSpec((1, 6, d), lambda bi, i: (bi, 0, 0)),
                  _const_spec(w_pool_bf.shape),
                  _const_spec((1, d)),
                  _const_spec((d, d)), _const_spec((d, d)), _const_spec((d, d)),
                  _const_spec((1, d)),
                  _const_spec((d, N_EXPERTS)),
                  _const_spec((1, N_EXPERTS))],
        out_specs=(pl.BlockSpec((1, tm, d), row),
                   pl.BlockSpec((tm, ROW_WORDS), flat),
                   pl.BlockSpec((TOP_K, tm), slot_major),
                   pl.BlockSpec((tm, TOP_K), flat),
                   pl.BlockSpec((TOP_K, tm), slot_major),
                   _const_spec((N_EXPERTS, 1))),
        out_shape=out_shape,
        scratch_shapes=[pltpu.VMEM((N_EXPERTS, 1), F32)],
        compiler_params=_params("arbitrary", "arbitrary"),
        name="merge_route",
    )(attn_o, u, u, u, ga, gp, x, mod, w_pool_bf, pool_scale, wab, wpb, wout,
      norm2_g, w_router, b_router)


SC_CORES = 2
SC_SUBCORES = 16
SC_WORKERS = SC_CORES * SC_SUBCORES
SC_WINDOW = 128
SC_GATHER_WINDOW = 64


def _sc_worker_id():
    return lax.axis_index("s") * SC_CORES + lax.axis_index("c")


def _sc_scatter_rows(rows, dest, pad_dest, out_rows):
    n, w = rows.shape
    n_win = n // (SC_WORKERS * SC_WINDOW)
    n_pad_win = pad_dest.shape[1]
    assert dest.shape == (TOP_K, n) and SC_WORKERS * n_win * SC_WINDOW == n
    mesh = plsc.VectorSubcoreMesh(core_axis_name="c", subcore_axis_name="s")
    zeros = jnp.zeros((SC_WINDOW, w), rows.dtype)

    @functools.partial(
        pl.kernel, mesh=mesh,
        out_type=jax.ShapeDtypeStruct((out_rows, w), rows.dtype),
        scratch_types=[pltpu.VMEM((TOP_K, SC_WINDOW), jnp.int32),
                       pltpu.VMEM((n_pad_win, SC_WINDOW), jnp.int32),
                       pltpu.VMEM((SC_WINDOW, w), rows.dtype),
                       pltpu.SemaphoreType.DMA],
        name="sc_scatter_rows",
    )
    def scatter(rows_hbm, dest_hbm, pad_hbm, zeros_hbm, out_hbm, idx_v, pad_v, rows_v, sem):
        wid = _sc_worker_id()

        @pl.loop(0, n_win)
        def _(j):
            base = (wid * n_win + j) * SC_WINDOW
            pltpu.sync_copy(rows_hbm.at[pl.ds(base, SC_WINDOW)], rows_v)
            pltpu.sync_copy(dest_hbm.at[:, pl.ds(base, SC_WINDOW)], idx_v)
            copies = [pltpu.async_copy(rows_v, out_hbm.at[idx_v.at[k]], sem) for k in range(TOP_K)]
            for cp in copies:
                cp.wait()

        pltpu.sync_copy(zeros_hbm, rows_v)
        pltpu.sync_copy(pad_hbm.at[wid], pad_v)
        copies = [pltpu.async_copy(rows_v, out_hbm.at[pad_v.at[j]], sem) for j in range(n_pad_win)]
        for cp in copies:
            cp.wait()

    return scatter(rows, dest, pad_dest, zeros)


def _expert_kernel(be_ref, first_ref, next_ref, valid_ref, nu_ref, x_ref, wgu_hbm, bgu_ref, wd_hbm, bd_ref,
                   y_ref, wgu_f, wd_f, wgu_s, wd_s, sem):
    step = pl.program_id(0)

    def weight_copies(e):
        return (pltpu.make_async_copy(wgu_hbm.at[e], wgu_f, sem.at[0]),
                pltpu.make_async_copy(wd_hbm.at[e], wd_f, sem.at[1]))

    def start_weights(e):
        for cp in weight_copies(e):
            cp.start(priority=1)

    @pl.when(step == 0)
    def _():
        start_weights(be_ref[0])

    for sub in range(EXPERT_BLOCKS_PER_STEP):
        _expert_block(step * EXPERT_BLOCKS_PER_STEP + sub,
                      slice(sub * TM_EXPERT, (sub + 1) * TM_EXPERT),
                      be_ref, first_ref, next_ref, valid_ref, nu_ref, x_ref, bgu_ref, bd_ref, y_ref,
                      wgu_f, wd_f, wgu_s, wd_s, weight_copies, start_weights)


def _expert_block(i, rows, be_ref, first_ref, next_ref, valid_ref, nu_ref, x_ref, bgu_ref, bd_ref, y_ref,
                  wgu_f, wd_f, wgu_s, wd_s, weight_copies, start_weights):
    @pl.when(i < nu_ref[0])
    def _():
        e = be_ref[i]

        @pl.when(first_ref[i] == 1)
        def _():
            for cp in weight_copies(e):
                cp.wait()
            ck = 16

            def cast_rows(c, carry):
                r = pl.ds(pl.multiple_of(c * ck, ck), ck)
                wgu_s[r, :] = wgu_f[r, :].astype(BF16)
                wd_s[r, :] = wd_f[r, :].astype(BF16)
                return carry

            lax.fori_loop(0, D_MODEL // ck, cast_rows, 0, unroll=2)

            @pl.when(next_ref[i] >= 0)
            def _():
                start_weights(next_ref[i])

        def ffn(m):
            r = slice(rows.start, rows.start + m)
            x = _unpack_rows(x_ref[r, :]).astype(BF16)
            bgu = bgu_ref[e]
            gt = jnp.dot(x, wgu_s[:, :D_FF], preferred_element_type=F32) + bgu[:, :D_FF]
            ln = jnp.dot(x, wgu_s[:, D_FF:], preferred_element_type=F32) + bgu[:, D_FF:]
            gt = jnp.minimum(gt, SWIGLU_LIMIT)
            ln = jnp.clip(ln, -SWIGLU_LIMIT, SWIGLU_LIMIT)
            act = gt * jax.nn.sigmoid(SWIGLU_ALPHA * gt) * (ln + 1.0)
            y = jnp.dot(act.astype(BF16), wd_s[...], preferred_element_type=F32) + bd_ref[e]
            y_ref[r, :] = _pack_rows(y)
            if m < TM_EXPERT:
                y_ref[rows.start + m:rows.stop, :] = jnp.zeros((TM_EXPERT - m, ROW_WORDS), U32)

        quanta = (valid_ref[i] + EXPERT_ROW_QUANTUM - 1) // EXPERT_ROW_QUANTUM
        for q in range(1, TM_EXPERT // EXPERT_ROW_QUANTUM + 1):
            pl.when(quanta == q)(functools.partial(ffn, q * EXPERT_ROW_QUANTUM))

    @pl.when(i >= nu_ref[0])
    def _():
        y_ref[rows, :] = jnp.zeros((TM_EXPERT, ROW_WORDS), U32)


def _experts(block_e, first, next_e, valid, n_used, xs, w_gu, b_gu, w_down, b_down):
    d = D_MODEL
    nb = block_e.shape[0]
    per_step = EXPERT_BLOCKS_PER_STEP
    tm = TM_EXPERT * per_step
    assert nb % per_step == 0

    xmap = lambda i, be, fi, nx, va, nu: (jnp.minimum(i, (nu[0] - 1) // per_step), 0)
    grid_spec = pltpu.PrefetchScalarGridSpec(
        num_scalar_prefetch=5,
        grid=(nb // per_step,),
        in_specs=[pl.BlockSpec((tm, ROW_WORDS), xmap),
                  pl.BlockSpec(memory_space=pl.ANY),
                  _const_spec((N_EXPERTS, 1, 2 * D_FF)),
                  pl.BlockSpec(memory_space=pl.ANY),
                  _const_spec((N_EXPERTS, 1, d))],
        out_specs=pl.BlockSpec((tm, ROW_WORDS), lambda i, be, fi, nx, va, nu: (i, 0)),
        scratch_shapes=[pltpu.VMEM((d, 2 * D_FF), F32), pltpu.VMEM((D_FF, d), F32),
                        pltpu.VMEM((d, 2 * D_FF), BF16), pltpu.VMEM((D_FF, d), BF16),
                        pltpu.SemaphoreType.DMA((2,))],
    )
    return pl.pallas_call(
        _expert_kernel,
        grid_spec=grid_spec,
        out_shape=jax.ShapeDtypeStruct((nb * TM_EXPERT, ROW_WORDS), U32),
        compiler_params=_params("arbitrary"),
        name="moe_experts",
    )(block_e, first, next_e, valid, n_used, xs, w_gu, b_gu.reshape(N_EXPERTS, 1, 2 * D_FF),
      w_down, b_down.reshape(N_EXPERTS, 1, d))


def _sc_gather_rows(table, idx):
    m = idx.shape[0]
    w = table.shape[1]
    win = SC_GATHER_WINDOW
    per_worker = m // SC_WORKERS
    n_win = per_worker // win
    assert per_worker * SC_WORKERS == m and n_win * win == per_worker and n_win % 2 == 0
    mesh = plsc.VectorSubcoreMesh(core_axis_name="c", subcore_axis_name="s")

    @functools.partial(
        pl.kernel, mesh=mesh,
        out_type=jax.ShapeDtypeStruct((m, w), table.dtype),
        scratch_types=[pltpu.VMEM((win,), jnp.int32), pltpu.VMEM((win,), jnp.int32),
                       pltpu.VMEM((win, w), table.dtype), pltpu.VMEM((win, w), table.dtype),
                       pltpu.SemaphoreType.DMA((4,))],
        name="sc_gather_rows",
    )
    def gather(table_hbm, idx_hbm, out_hbm, idx0, idx1, rows0, rows1, sems):
        wid = _sc_worker_id()
        idxs, rows = (idx0, idx1), (rows0, rows1)

        def out_window(j):
            return out_hbm.at[pl.ds(wid * per_worker + j * win, win)]

        def start_gather(j, slot):
            pltpu.sync_copy(idx_hbm.at[pl.ds(wid * per_worker + j * win, win)], idxs[slot])
            pltpu.async_copy(table_hbm.at[idxs[slot]], rows[slot], sems.at[slot])

        def wait_gather(slot):
            pltpu.make_async_copy(table_hbm.at[idxs[slot]], rows[slot], sems.at[slot]).wait()

        def wait_write(j, slot):
            pltpu.make_async_copy(rows[slot], out_window(j), sems.at[2 + slot]).wait()

        start_gather(0, 0)

        @pl.loop(0, n_win, step=2)
        def _(j0):
            for slot in range(2):
                j = j0 + slot
                other = 1 - slot

                @pl.when(j >= 1)
                def _():
                    wait_write(j - 1, other)

                @pl.when(j + 1 < n_win)
                def _():
                    start_gather(j + 1, other)

                wait_gather(slot)
                pltpu.async_copy(rows[slot], out_window(j), sems.at[2 + slot])

        wait_write(n_win - 1, (n_win - 1) % 2)

    return gather(table, idx)


def _combine_dense_kernel(y4_ref, gate_ref, x1_ref, mod_ref, fg_ref, o_ref):
    gate = gate_ref[...]
    y = gate[:, 0:1] * _unpack_rows(y4_ref[0])
    for k in range(1, TOP_K):
        y = y + gate[:, k:k + 1] * _unpack_rows(y4_ref[k])
    x2 = x1_ref[...] + mod_ref[0, 5:6, :] * y
    o_ref[...] = x2 * lax.rsqrt(jnp.mean(x2 * x2, axis=-1, keepdims=True) + NORM_EPS) * fg_ref[...]


def _combine_dense(y4, gate_w, x1, mod, final_g):
    b, l, d = x1.shape
    ts = 512
    nt = l // ts
    return pl.pallas_call(
        _combine_dense_kernel,
        grid=(b * nt,),
        in_specs=[pl.BlockSpec((TOP_K, ts, ROW_WORDS), lambda s: (0, s, 0)),
                  pl.BlockSpec((ts, TOP_K), lambda s: (s, 0)),
                  pl.BlockSpec((ts, d), lambda s: (s, 0)),
                  pl.BlockSpec((1, 6, d), lambda s: (s // nt, 0, 0)),
                  _const_spec((1, d))],
        out_specs=pl.BlockSpec((ts, d), lambda s: (s, 0)),
        out_shape=jax.ShapeDtypeStruct((b * l, d), F32),
        compiler_params=_params("arbitrary"),
        name="moe_combine",
    )(y4, gate_w, x1.reshape(b * l, d), mod, final_g).reshape(b, l, d)


def _rope_tables(seq_len):
    inv_freq = ROPE_BASE ** (-jnp.arange(ROPE_PAIRS, dtype=F32) / ROPE_PAIRS)
    reps = LANES // HEAD_DIM
    out = []
    for n_pos, on_row in ((seq_len // GRID_W, True), (GRID_W, False)):
        ang = jnp.arange(n_pos, dtype=F32)[:, None] * inv_freq
        zero = jnp.zeros_like(ang)
        for fn, sign in ((jnp.cos, 1.0), (jnp.sin, -1.0)):
            v = fn(ang)
            head = ([sign * v, v, zero, zero] if on_row else [zero, zero, sign * v, v])
            out.append(jnp.concatenate(head * reps, axis=1))
    return out


def kernel(x, c, ctx, c_ctx, w_ada, b_ada, norm1_g, norm2_g, w_in, b_in, attn_sink, w_pool,
           pool_scale, w_attn_br, w_pool_br, w_out, w_router, b_router, w_gu, b_gu, w_down,
           b_down, final_g):
    b, l, d = x.shape
    n = b * l
    assert w_ada.shape[0] == 1, "single-layer block"

    cond = jnp.zeros((SUBLANES, d), F32).at[:b].set(c).at[b].set(c_ctx)
    mod = _adaln(cond, w_ada[0], b_ada[0])[:b + 1].reshape(b + 1, 6, d)
    mod_x, mod_c = mod[:b], mod[b:b + 1]

    rope = _rope_tables(l)
    v0 = ATTN_W + KV_W
    w_v = w_in[0][:, v0:v0 + KV_W].reshape(d, N_KV_HEADS, HEAD_DIM)
    w_v = jnp.concatenate([w_v, jnp.zeros_like(w_v)], axis=-1).reshape(d, VEXT_W)
    b_v = b_in[0][v0:v0 + KV_W].reshape(N_KV_HEADS, HEAD_DIM)
    b_v = jnp.concatenate([b_v, jnp.ones_like(b_v)], axis=-1).reshape(VEXT_W)
    w_in_bf = jnp.concatenate([w_in[0][:, :v0], w_v, w_in[0][:, v0 + KV_W:]], axis=1).astype(BF16)
    b_in2 = jnp.concatenate([b_in[0][:v0], b_v, b_in[0][v0 + KV_W:]]).reshape(1, IN_W_EXT)
    g1 = norm1_g[0].reshape(1, d)
    q, kt, v, u, ga, gp = _inproj(x, mod_x, g1, w_in_bf, b_in2, rope)
    kv_sl = slice(ATTN_W, ATTN_W + KV_W + VEXT_W)
    kxt, vx = _ctx_kv(ctx, mod_c, g1, w_in_bf[:, kv_sl], b_in2[:, kv_sl])

    attn_o = _attention(q, kt, v, kxt, vx, attn_sink[0])

    x1, h2, top_idx, gate_w, rank, counts = _merge_route(
        attn_o, u, ga, gp, x, mod_x, w_pool[0].astype(BF16), pool_scale[0].reshape(1, d),
        w_attn_br[0].astype(BF16), w_pool_br[0].astype(BF16), w_out[0].astype(BF16),
        norm2_g[0].reshape(1, d), w_router[0].astype(BF16), b_router[0].reshape(1, N_EXPERTS))

    tm = TM_EXPERT
    nb = n * TOP_K // tm + N_EXPERTS
    cnt = counts[:, 0].astype(jnp.int32)
    padded = (cnt + tm - 1) // tm * tm
    pend = jnp.cumsum(padded)
    pstart = pend - padded
    expert_ids = jnp.arange(N_EXPERTS, dtype=jnp.int32)
    dest = jnp.sum(jnp.where(top_idx[..., None] == expert_ids, pstart, 0), axis=-1) + rank
    n_used = (pend[-1] // tm).reshape(1)
    block_start = jnp.arange(nb, dtype=jnp.int32) * tm
    block_e = jnp.minimum(jnp.sum((pend[None, :] <= block_start[:, None]).astype(jnp.int32), axis=1),
                          N_EXPERTS - 1)
    first = jnp.concatenate([jnp.ones((1,), jnp.int32),
                             (block_e[1:] != block_e[:-1]).astype(jnp.int32)])
    later_used = (expert_ids[None, :] > expert_ids[:, None]) & (cnt[None, :] > 0)
    next_of_expert = jnp.min(jnp.where(later_used, expert_ids[None, :], N_EXPERTS), axis=1)
    next_of_expert = jnp.where(next_of_expert < N_EXPERTS, next_of_expert, -1)
    next_e = jnp.sum(jnp.where(block_e[:, None] == expert_ids, next_of_expert, 0), axis=-1)

    qm = EXPERT_ROW_QUANTUM
    pad_j = jnp.arange(qm, dtype=jnp.int32)[None, :]
    n_pad = (cnt + qm - 1) // qm * qm - cnt
    spare = nb * tm + expert_ids[:, None] * qm + pad_j
    pad_dest = jnp.where(pad_j < n_pad[:, None], (pstart + cnt)[:, None] + pad_j, spare)
    pad_dest = pad_dest.reshape(SC_WORKERS, N_EXPERTS * qm // (SC_WORKERS * SC_WINDOW), SC_WINDOW)

    xs = _sc_scatter_rows(h2, dest, pad_dest, nb * tm + N_EXPERTS * qm)
    used_end = jnp.sum(jnp.where(block_e[:, None] == expert_ids, pstart + cnt, 0), axis=-1)
    valid = jnp.clip(used_end - block_start, 0, tm)
    ys = _experts(block_e, first, next_e, valid, n_used, xs, w_gu[0], b_gu[0], w_down[0], b_down[0])
    y4 = _sc_gather_rows(ys, dest.reshape(-1)).reshape(TOP_K, n, ROW_WORDS)
    return _combine_dense(y4, gate_w, x1, mod_x, final_g.reshape(1, d))
```

```python
import functools

import jax
import jax.numpy as jnp
from jax import lax
from jax.experimental import pallas as pl
from jax.experimental.pallas import tpu as pltpu
from jax.experimental.pallas import tpu_sc as plsc

D_MODEL = 1024
GRID_W = 64
HEAD_DIM = 64
N_HEADS = 16
N_KV_HEADS = 4
GROUP = N_HEADS // N_KV_HEADS
ATTN_W = N_HEADS * HEAD_DIM
KV_W = N_KV_HEADS * HEAD_DIM
WINDOW = 128
ATTN_SCALE = HEAD_DIM ** -0.5
ROPE_BASE = 10000.0
ROPE_PAIRS = HEAD_DIM // 4
POOL_WINDOWS = (2, 4, 8, 16)
POOL_GROUP_W = D_MODEL // len(POOL_WINDOWS)
IN_W = ATTN_W + 2 * KV_W + D_MODEL + 2 * D_MODEL
VEXT_W = N_KV_HEADS * 2 * HEAD_DIM
IN_W_EXT = IN_W - KV_W + VEXT_W
N_EXPERTS = 32
TOP_K = 4
D_FF = D_MODEL
SWIGLU_ALPHA = 1.702
SWIGLU_LIMIT = 7.0
NORM_EPS = 1e-5
NEG_INF = -1e30

LANES = 128
SUBLANES = 8
VMEM_LIMIT = 56 * 1024 * 1024

TM_INPROJ = 512
TQ = WINDOW
ATTN_Q_BLOCKS = 4
TM_MERGE = 512
MERGE_SUB = 256
TM_EXPERT = 512
EXPERT_ROW_QUANTUM = 128
EXPERT_BLOCKS_PER_STEP = 1
POOL_HALO = SUBLANES

F32 = jnp.float32
BF16 = jnp.bfloat16


ROW_WORDS = D_MODEL // 2
U32 = jnp.uint32


def _pack_rows(value):
    bits = lax.bitcast_convert_type(value.astype(BF16).astype(F32), U32)
    return bits[:, :ROW_WORDS] | (bits[:, ROW_WORDS:] >> 16)


def _unpack_rows(words):
    hi = lax.bitcast_convert_type(words & jnp.uint32(0xFFFF0000), F32)
    lo = lax.bitcast_convert_type(words << 16, F32)
    return jnp.concatenate([hi, lo], axis=1)


def _params(*sem):
    return pltpu.CompilerParams(dimension_semantics=sem, vmem_limit_bytes=VMEM_LIMIT)


def _const_spec(shape):
    nd = len(shape)
    return pl.BlockSpec(shape, lambda *_: (0,) * nd)


def _adaln_kernel(c_ref, w_ref, b_ref, o_ref):
    c = c_ref[...]
    s = c * jax.nn.sigmoid(c)
    o_ref[...] = jnp.dot(s.astype(BF16), w_ref[...].astype(BF16),
                         preferred_element_type=F32) + b_ref[...]


def _adaln(cond, w_ada, b_ada):
    rows, d = cond.shape
    n = w_ada.shape[1]
    tn = 1024
    return pl.pallas_call(
        _adaln_kernel,
        grid=(n // tn,),
        in_specs=[_const_spec((rows, d)),
                  pl.BlockSpec((d, tn), lambda j: (0, j)),
                  pl.BlockSpec((1, tn), lambda j: (0, j))],
        out_specs=pl.BlockSpec((rows, tn), lambda j: (0, j)),
        out_shape=jax.ShapeDtypeStruct((rows, n), F32),
        compiler_params=_params("arbitrary"),
        name="adaln",
    )(cond, w_ada, b_ada.reshape(1, n))


def _norm_mod(x, g, shift, scale):
    y = x * lax.rsqrt(jnp.mean(x * x, axis=-1, keepdims=True) + NORM_EPS) * g
    return y * (1.0 + scale) + shift


def _rope(t, cos, sin_signed):
    w = t.shape[1]
    half = ROPE_PAIRS
    lane = lax.broadcasted_iota(jnp.int32, t.shape, 1)
    first = (lane & (2 * half - 1)) < half
    rot = jnp.where(first, pltpu.roll(t, w - half, 1), pltpu.roll(t, half, 1))
    reps = w // LANES
    cos_w = jnp.concatenate([cos] * reps, axis=1) if reps > 1 else cos
    sin_w = jnp.concatenate([sin_signed] * reps, axis=1) if reps > 1 else sin_signed
    return t * cos_w + rot * sin_w


def _token_table(row_ref, col_ref):
    gr = row_ref.shape[0]
    shape = (gr, GRID_W, LANES)
    full = (jnp.broadcast_to(row_ref[...][:, None, :], shape)
            + jnp.broadcast_to(col_ref[...][None, :, :], shape))
    return full.reshape(gr * GRID_W, LANES)


def _inproj_kernel(x_ref, mod_ref, g_ref, w_ref, b_ref, cr_ref, sr_ref, cc_ref, sc_ref,
                   q_ref, k_ref, v_ref, u_ref, ga_ref, gp_ref):
    h = _norm_mod(x_ref[0], g_ref[...], mod_ref[0, 0:1, :], mod_ref[0, 1:2, :])
    hb = h.astype(BF16)
    cos = _token_table(cr_ref, cc_ref)
    sin = _token_table(sr_ref, sc_ref)

    def proj(c0, c1):
        return jnp.dot(hb, w_ref[:, c0:c1], preferred_element_type=F32) + b_ref[:, c0:c1]

    cw = 512
    chunks = []
    for c0 in range(0, ATTN_W, cw):
        def put_q(t, c0=c0):
            q_ref[0, :, c0:c0 + cw] = (_rope(t, cos, sin) * ATTN_SCALE).astype(BF16)
        chunks.append((c0, cw, put_q))

    def put_k(t):
        k_ref[0] = _rope(t, cos, sin).T.astype(BF16)

    def put_v(t):
        v_ref[0] = t.astype(BF16)

    chunks.append((ATTN_W, KV_W, put_k))
    chunks.append((ATTN_W + KV_W, VEXT_W, put_v))
    base = ATTN_W + KV_W + VEXT_W
    for c0 in range(0, D_MODEL, cw):
        def put_u(t, c0=c0):
            u_ref[0, :, c0:c0 + cw] = t
        chunks.append((base + c0, cw, put_u))
    for ref in (ga_ref, gp_ref):
        base += D_MODEL
        for c0 in range(0, D_MODEL, cw):
            def put_gate(t, ref=ref, c0=c0):
                ref[0, :, c0:c0 + cw] = jax.nn.sigmoid(t).astype(BF16)
            chunks.append((base + c0, cw, put_gate))

    pending = proj(chunks[0][0], chunks[0][0] + chunks[0][1])
    for ci, (_, _, epilogue) in enumerate(chunks):
        ready = pending
        if ci + 1 < len(chunks):
            n0, nw, _ = chunks[ci + 1]
            pending = proj(n0, n0 + nw)
        epilogue(ready)


def _inproj(x, mod, norm_g, w_in_bf, b_in, rope):
    b, l, d = x.shape
    tm = TM_INPROJ
    row = lambda bi, i: (bi, i, 0)
    out_shape = (
        jax.ShapeDtypeStruct((b, l, ATTN_W), BF16),
        jax.ShapeDtypeStruct((b, KV_W, l), BF16),
        jax.ShapeDtypeStruct((b, l, VEXT_W), BF16),
        jax.ShapeDtypeStruct((b, l, D_MODEL), F32),
        jax.ShapeDtypeStruct((b, l, D_MODEL), BF16),
        jax.ShapeDtypeStruct((b, l, D_MODEL), BF16),
    )
    return pl.pallas_call(
        _inproj_kernel,
        grid=(b, l // tm),
        in_specs=[pl.BlockSpec((1, tm, d), row),
                  pl.BlockSpec((1, 6, d), lambda bi, i: (bi, 0, 0)),
                  _const_spec((1, d)),
                  _const_spec((d, IN_W_EXT)),
                  _const_spec((1, IN_W_EXT)),
                  pl.BlockSpec((tm // GRID_W, LANES), lambda bi, i: (i, 0)),
                  pl.BlockSpec((tm // GRID_W, LANES), lambda bi, i: (i, 0)),
                  _const_spec((GRID_W, LANES)),
                  _const_spec((GRID_W, LANES))],
        out_specs=(pl.BlockSpec((1, tm, ATTN_W), row),
                   pl.BlockSpec((1, KV_W, tm), lambda bi, i: (bi, 0, i)),
                   pl.BlockSpec((1, tm, VEXT_W), row),
                   pl.BlockSpec((1, tm, D_MODEL), row),
                   pl.BlockSpec((1, tm, D_MODEL), row),
                   pl.BlockSpec((1, tm, D_MODEL), row)),
        out_shape=out_shape,
        compiler_params=_params("arbitrary", "arbitrary"),
        name="inproj",
    )(x, mod, norm_g, w_in_bf, b_in, *rope)


def _ctx_kv_kernel(x_ref, mod_ref, g_ref, w_ref, b_ref, k_ref, v_ref):
    h = _norm_mod(x_ref[0], g_ref[...], mod_ref[0, 0:1, :], mod_ref[0, 1:2, :])
    kv = jnp.dot(h.astype(BF16), w_ref[...], preferred_element_type=F32) + b_ref[...]
    k_ref[0] = kv[:, :KV_W].T.astype(BF16)
    v_ref[0] = kv[:, KV_W:].astype(BF16)


def _ctx_kv(ctx, mod_c, norm_g, w_kv_bf, b_kv):
    b, lc, d = ctx.shape
    row = lambda bi: (bi, 0, 0)
    return pl.pallas_call(
        _ctx_kv_kernel,
        grid=(b,),
        in_specs=[pl.BlockSpec((1, lc, d), row),
                  _const_spec((1, 6, d)),
                  _const_spec((1, d)),
                  _const_spec((d, KV_W + VEXT_W)),
                  _const_spec((1, KV_W + VEXT_W))],
        out_specs=(pl.BlockSpec((1, KV_W, lc), row), pl.BlockSpec((1, lc, VEXT_W), row)),
        out_shape=(jax.ShapeDtypeStruct((b, KV_W, lc), BF16),
                   jax.ShapeDtypeStruct((b, lc, VEXT_W), BF16)),
        compiler_params=_params("arbitrary"),
        name="ctx_kv",
    )(ctx, mod_c, norm_g, w_kv_bf, b_kv)


def _attn_kernel(sink_ref, q_ref, *refs):
    nq = ATTN_Q_BLOCKS
    k_refs, v_refs = refs[:nq + 2], refs[nq + 2:2 * nq + 4]
    kx_ref, vx_ref, o_ref = refs[2 * nq + 4:]
    first_blk = pl.program_id(1) * nq
    last_blk = pl.num_programs(1) * nq - 1
    rows = GROUP * TQ
    pair_w = 2 * HEAD_DIM
    qi = lax.broadcasted_iota(jnp.int32, (rows, TQ), 0) & (TQ - 1)
    col = lax.broadcasted_iota(jnp.int32, (rows, TQ), 1)
    low_half = lax.broadcasted_iota(jnp.int32, (TQ, pair_w), 1) < HEAD_DIM
    row_id = lax.broadcasted_iota(jnp.int32, (rows, 1), 0)

    def scores(b, j):
        qrows = slice(b * TQ, (b + 1) * TQ)
        parts = []
        for g in range(GROUP):
            h = j * GROUP + g
            pair = q_ref[0, qrows, (h // 2) * pair_w:(h // 2 + 1) * pair_w]
            parts.append(jnp.where(low_half if h % 2 == 0 else ~low_half, pair, jnp.zeros_like(pair)))
        q4 = jnp.concatenate(parts, axis=0)
        ks = slice(j * HEAD_DIM, (j + 1) * HEAD_DIM)
        kt = jnp.concatenate([k_refs[b + d][0, ks, :] for d in range(3)], axis=1)
        s_loc = jnp.dot(q4, jnp.concatenate([kt, kt], axis=0), preferred_element_type=F32)
        kxt = kx_ref[0, ks, :]
        s_ctx = jnp.dot(q4, jnp.concatenate([kxt, kxt], axis=0), preferred_element_type=F32)
        return s_loc, s_ctx

    def probs(b, j, s_loc, s_ctx):
        keep_prev = (col >= qi) & (first_blk + b > 0)
        keep_next = (col <= qi) & (first_blk + b < last_blk)
        pieces = [jnp.where(keep_prev, s_loc[:, :TQ], NEG_INF), s_loc[:, TQ:2 * TQ],
                  jnp.where(keep_next, s_loc[:, 2 * TQ:], NEG_INF)]
        pieces += [s_ctx[:, c0:c0 + TQ] for c0 in range(0, s_ctx.shape[1], TQ)]
        sink = jnp.zeros((rows, 1), F32)
        for g in range(GROUP):
            sink = jnp.where(row_id // TQ == g, sink_ref[j * GROUP + g], sink)
        mx = pieces[0]
        for pc in pieces[1:]:
            mx = jnp.maximum(mx, pc)
        m = jnp.maximum(jnp.max(mx, axis=-1, keepdims=True), sink)
        p = jnp.concatenate([jnp.exp(pc - m).astype(BF16) for pc in pieces], axis=1)
        return p, jnp.exp(sink - m)

    def output(b, j, p, sink_p):
        vs = slice(j * pair_w, (j + 1) * pair_w)
        v_all = jnp.concatenate([v_refs[b + d][0, :, vs] for d in range(3)]
                                + [vx_ref[0, :, vs]], axis=0)
        r = jnp.dot(p, v_all, preferred_element_type=F32)
        den = pltpu.roll(r, HEAD_DIM, 1) + sink_p
        o = r / den
        for g in range(0, GROUP, 2):
            even = o[g * TQ:(g + 1) * TQ]
            odd = pltpu.roll(o[(g + 1) * TQ:(g + 2) * TQ], HEAD_DIM, 1)
            c0 = (j * GROUP + g) * HEAD_DIM
            o_ref[0, b * TQ:(b + 1) * TQ, c0:c0 + pair_w] = jnp.where(low_half, even, odd).astype(BF16)

    items = [(b, j) for b in range(nq) for j in range(N_KV_HEADS)]
    s_queue = [scores(*items[0]), scores(*items[1])]
    p_queue = [probs(*items[0], *s_queue.pop(0))]
    for i, item in enumerate(items):
        if i + 2 < len(items):
            s_queue.append(scores(*items[i + 2]))
        ready = p_queue.pop(0)
        if i + 1 < len(items):
            p_queue.append(probs(*items[i + 1], *s_queue.pop(0)))
        output(*item, *ready)


def _attention(q, kt, v, kxt, vx, sink):
    b, l, _ = q.shape
    lc = kxt.shape[2]
    nq = ATTN_Q_BLOCKS
    nblk = l // TQ
    assert nblk % nq == 0

    def blk(n, d):
        return jnp.clip(n * nq + d - 1, 0, nblk - 1)

    k_specs = [pl.BlockSpec((1, KV_W, TQ), lambda bi, n, d=d: (bi, 0, blk(n, d))) for d in range(nq + 2)]
    v_specs = [pl.BlockSpec((1, TQ, VEXT_W), lambda bi, n, d=d: (bi, blk(n, d), 0)) for d in range(nq + 2)]
    return pl.pallas_call(
        _attn_kernel,
        grid=(b, nblk // nq),
        in_specs=[pl.BlockSpec(memory_space=pltpu.SMEM),
                  pl.BlockSpec((1, nq * TQ, ATTN_W), lambda bi, n: (bi, n, 0))]
                 + k_specs + v_specs
                 + [pl.BlockSpec((1, KV_W, lc), lambda bi, n: (bi, 0, 0)),
                    pl.BlockSpec((1, lc, VEXT_W), lambda bi, n: (bi, 0, 0))],
        out_specs=pl.BlockSpec((1, nq * TQ, ATTN_W), lambda bi, n: (bi, n, 0)),
        out_shape=jax.ShapeDtypeStruct((b, l, ATTN_W), BF16),
        compiler_params=_params("arbitrary", "arbitrary"),
        name="attention",
    )(sink, q, *([kt] * (nq + 2)), *([v] * (nq + 2)), kxt, vx)


def _shift_rows(a, d):
    n = a.shape[0]
    return pltpu.roll(a, (-d) % n, 0)


def _merge_kernel(ao_ref, u_ref, up_ref, un_ref, sa_ref, sp_ref, x_ref, mod_ref,
                  wpool_ref, pscale_ref, wab_ref, wpb_ref, wout_ref, g2_ref, wr_ref, br_ref,
                  x1_ref, h2_ref, idx_ref, gate_ref, rank_ref, cnt_ref, carry_ref, *, seq_len):
    bi = pl.program_id(0)
    i = pl.program_id(1)
    last = pl.num_programs(1) - 1
    tm, sub = TM_MERGE, MERGE_SUB

    @pl.when((bi == 0) & (i == 0))
    def _():
        carry_ref[...] = jnp.zeros_like(carry_ref)

    u = u_ref[0]
    prev = jnp.where(i > 0, up_ref[0], 0.0)
    nxt = jnp.where(i < last, un_ref[0], 0.0)
    ext = jnp.concatenate([prev, u, nxt], axis=0)

    def pool_diff(r0):
        t = i * tm + r0 + lax.broadcasted_iota(jnp.int32, (sub, 1), 0)
        diffs = []
        for g, w in enumerate(POOL_WINDOWS):
            cs = slice(g * POOL_GROUP_W, (g + 1) * POOL_GROUP_W)
            e = ext[r0:r0 + sub + 2 * POOL_HALO, cs]
            acc = _shift_rows(e, -1) + e
            step = 1
            while 2 * step < w:
                acc = _shift_rows(acc, -step) + _shift_rows(acc, step)
                step *= 2
            win = acc[POOL_HALO:POOL_HALO + sub]
            half = w // 2
            cnt = (jnp.minimum(t + half, seq_len) - jnp.maximum(t - half, 0)).astype(F32)
            diffs.append((win / cnt - u[r0:r0 + sub, cs]).astype(BF16))
        return diffs

    def branches(r0, diffs):
        rs = slice(r0, r0 + sub)
        a = jnp.dot(ao_ref[0, rs, :], wab_ref[...], preferred_element_type=F32)
        mixed = [jnp.dot(df, wpool_ref[g], preferred_element_type=F32) for g, df in enumerate(diffs)]
        pool_o = jnp.concatenate(mixed, axis=1) * pscale_ref[...]
        p = jnp.dot(pool_o.astype(BF16), wpb_ref[...], preferred_element_type=F32)
        merged = sa_ref[0, rs, :].astype(F32) * a + sp_ref[0, rs, :].astype(F32) * p
        return merged.astype(BF16)

    def out_proj(merged):
        return jnp.dot(merged, wout_ref[...], preferred_element_type=F32)

    def finish(r0, o):
        rs = slice(r0, r0 + sub)
        x1 = x_ref[0, rs, :] + mod_ref[0, 2:3, :] * o
        x1_ref[0, rs, :] = x1
        h2 = _norm_mod(x1, g2_ref[...], mod_ref[0, 3:4, :], mod_ref[0, 4:5, :])
        h2_ref[rs, :] = _pack_rows(h2)
        return jnp.dot(h2.astype(BF16), wr_ref[...], preferred_element_type=F32) + br_ref[...]

    def route(r0, logits, carry):
        lt = logits.T
        eid = lax.broadcasted_iota(jnp.int32, (N_EXPERTS, sub), 0).astype(F32)
        work = lt
        vals, idxs, sels = [], [], []
        for _ in range(TOP_K):
            mx = jnp.max(work, axis=0, keepdims=True)
            ix = jnp.min(jnp.where(work == mx, eid, float(N_EXPERTS)), axis=0, keepdims=True)
            sel = eid == ix
            work = jnp.where(sel, -jnp.inf, work)
            vals.append(mx)
            idxs.append(ix)
            sels.append(sel)
        ex = [jnp.exp(v - vals[0]) for v in vals]
        tot = ex[0] + ex[1] + ex[2] + ex[3]
        pad = [jnp.zeros_like(tot)] * (SUBLANES - TOP_K)
        gates = jnp.concatenate([e_ / tot for e_ in ex] + pad, axis=0)
        gate_ref[r0:r0 + sub, :] = gates.T[:, :TOP_K]
        idx_ref[:, r0:r0 + sub] = jnp.concatenate(idxs, axis=0).astype(jnp.int32)
        member = (sels[0] | sels[1] | sels[2] | sels[3]).astype(F32)
        t_row = lax.broadcasted_iota(jnp.int32, (sub, sub), 0)
        t_col = lax.broadcasted_iota(jnp.int32, (sub, sub), 1)
        earlier = (t_row < t_col).astype(BF16)
        before = jnp.dot(member.astype(BF16), earlier, preferred_element_type=F32) + carry
        ranks = [jnp.sum(jnp.where(s, before, 0.0), axis=0, keepdims=True) for s in sels]
        rank_ref[:, r0:r0 + sub] = jnp.concatenate(ranks, axis=0).astype(jnp.int32)
        return carry + jnp.sum(member, axis=1, keepdims=True)

    starts = list(range(0, tm, sub))
    carry = carry_ref[...]
    merged = branches(starts[0], pool_diff(starts[0]))
    for si, r0 in enumerate(starts):
        o = out_proj(merged)
        if si + 1 < len(starts):
            merged = branches(starts[si + 1], pool_diff(starts[si + 1]))
        carry = route(r0, finish(r0, o), carry)
    carry_ref[...] = carry
    cnt_ref[...] = carry


def _merge_route(attn_o, u, ga, gp, x, mod, w_pool_bf, pool_scale, wab, wpb, wout,
                 norm2_g, w_router, b_router):
    b, l, d = x.shape
    tm = TM_MERGE
    nt = l // tm
    n = b * l
    hb = tm // POOL_HALO
    row = lambda bi, i: (bi, i, 0)
    flat = lambda bi, i: (bi * nt + i, 0)
    halo_prev = lambda bi, i: (bi, jnp.maximum(i * hb - 1, 0), 0)
    halo_next = lambda bi, i: (bi, jnp.minimum((i + 1) * hb, l // POOL_HALO - 1), 0)
    slot_major = lambda bi, i: (0, bi * nt + i)
    out_shape = (
        jax.ShapeDtypeStruct((b, l, d), F32),
        jax.ShapeDtypeStruct((n, ROW_WORDS), U32),
        jax.ShapeDtypeStruct((TOP_K, n), jnp.int32),
        jax.ShapeDtypeStruct((n, TOP_K), F32),
        jax.ShapeDtypeStruct((TOP_K, n), jnp.int32),
        jax.ShapeDtypeStruct((N_EXPERTS, 1), F32),
    )
    return pl.pallas_call(
        functools.partial(_merge_kernel, seq_len=l),
        grid=(b, nt),
        in_specs=[pl.BlockSpec((1, tm, d), row),
                  pl.BlockSpec((1, tm, d), row),
                  pl.BlockSpec((1, POOL_HALO, d), halo_prev),
                  pl.BlockSpec((1, POOL_HALO, d), halo_next),
                  pl.BlockSpec((1, tm, d), row),
                  pl.BlockSpec((1, tm, d), row),
                  pl.BlockSpec((1, tm, d), row),
                  pl.BlockSpec((1, 6, d), lambda bi, i: (bi, 0, 0)),
                  _const_spec(w_pool_bf.shape),
                  _const_spec((1, d)),
                  _const_spec((d, d)), _const_spec((d, d)), _const_spec((d, d)),
                  _const_spec((1, d)),
                  _const_spec((d, N_EXPERTS)),
                  _const_spec((1, N_EXPERTS))],
        out_specs=(pl.BlockSpec((1, tm, d), row),
                   pl.BlockSpec((tm, ROW_WORDS), flat),
                   pl.BlockSpec((TOP_K, tm), slot_major),
                   pl.BlockSpec((tm, TOP_K), flat),
                   pl.BlockSpec((TOP_K, tm), slot_major),
                   _const_spec((N_EXPERTS, 1))),
        out_shape=out_shape,
        scratch_shapes=[pltpu.VMEM((N_EXPERTS, 1), F32)],
        compiler_params=_params("arbitrary", "arbitrary"),
        name="merge_route",
    )(attn_o, u, u, u, ga, gp, x, mod, w_pool_bf, pool_scale, wab, wpb, wout,
      norm2_g, w_router, b_router)


SC_CORES = 2
SC_SUBCORES = 16
SC_WORKERS = SC_CORES * SC_SUBCORES
SC_WINDOW = 128
SC_GATHER_WINDOW = 64


def _sc_worker_id():
    return lax.axis_index("s") * SC_CORES + lax.axis_index("c")


def _sc_scatter_rows(rows, dest, pad_dest, out_rows):
    n, w = rows.shape
    n_win = n // (SC_WORKERS * SC_WINDOW)
    n_pad_win = pad_dest.shape[1]
    assert dest.shape == (TOP_K, n) and SC_WORKERS * n_win * SC_WINDOW == n
    mesh = plsc.VectorSubcoreMesh(core_axis_name="c", subcore_axis_name="s")
    zeros = jnp.zeros((SC_WINDOW, w), rows.dtype)

    @functools.partial(
        pl.kernel, mesh=mesh,
        out_type=jax.ShapeDtypeStruct((out_rows, w), rows.dtype),
        scratch_types=[pltpu.VMEM((TOP_K, SC_WINDOW), jnp.int32),
                       pltpu.VMEM((n_pad_win, SC_WINDOW), jnp.int32),
                       pltpu.VMEM((SC_WINDOW, w), rows.dtype),
                       pltpu.SemaphoreType.DMA],
        name="sc_scatter_rows",
    )
    def scatter(rows_hbm, dest_hbm, pad_hbm, zeros_hbm, out_hbm, idx_v, pad_v, rows_v, sem):
        wid = _sc_worker_id()

        @pl.loop(0, n_win)
        def _(j):
            base = (wid * n_win + j) * SC_WINDOW
            pltpu.sync_copy(rows_hbm.at[pl.ds(base, SC_WINDOW)], rows_v)
            pltpu.sync_copy(dest_hbm.at[:, pl.ds(base, SC_WINDOW)], idx_v)
            copies = [pltpu.async_copy(rows_v, out_hbm.at[idx_v.at[k]], sem) for k in range(TOP_K)]
            for cp in copies:
                cp.wait()

        pltpu.sync_copy(zeros_hbm, rows_v)
        pltpu.sync_copy(pad_hbm.at[wid], pad_v)
        copies = [pltpu.async_copy(rows_v, out_hbm.at[pad_v.at[j]], sem) for j in range(n_pad_win)]
        for cp in copies:
            cp.wait()

    return scatter(rows, dest, pad_dest, zeros)


def _expert_kernel(be_ref, first_ref, next_ref, valid_ref, nu_ref, x_ref, wgu_hbm, bgu_ref, wd_hbm, bd_ref,
                   y_ref, wgu_f, wd_f, wgu_s, wd_s, sem):
    step = pl.program_id(0)

    def weight_copies(e):
        return (pltpu.make_async_copy(wgu_hbm.at[e], wgu_f, sem.at[0]),
                pltpu.make_async_copy(wd_hbm.at[e], wd_f, sem.at[1]))

    def start_weights(e):
        for cp in weight_copies(e):
            cp.start(priority=1)

    @pl.when(step == 0)
    def _():
        start_weights(be_ref[0])

    for sub in range(EXPERT_BLOCKS_PER_STEP):
        _expert_block(step * EXPERT_BLOCKS_PER_STEP + sub,
                      slice(sub * TM_EXPERT, (sub + 1) * TM_EXPERT),
                      be_ref, first_ref, next_ref, valid_ref, nu_ref, x_ref, bgu_ref, bd_ref, y_ref,
                      wgu_f, wd_f, wgu_s, wd_s, weight_copies, start_weights)


def _expert_block(i, rows, be_ref, first_ref, next_ref, valid_ref, nu_ref, x_ref, bgu_ref, bd_ref, y_ref,
                  wgu_f, wd_f, wgu_s, wd_s, weight_copies, start_weights):
    @pl.when(i < nu_ref[0])
    def _():
        e = be_ref[i]

        @pl.when(first_ref[i] == 1)
        def _():
            for cp in weight_copies(e):
                cp.wait()
            ck = 16

            def cast_rows(c, carry):
                r = pl.ds(pl.multiple_of(c * ck, ck), ck)
                wgu_s[r, :] = wgu_f[r, :].astype(BF16)
                wd_s[r, :] = wd_f[r, :].astype(BF16)
                return carry

            lax.fori_loop(0, D_MODEL // ck, cast_rows, 0, unroll=2)

            @pl.when(next_ref[i] >= 0)
            def _():
                start_weights(next_ref[i])

        def ffn(m):
            r = slice(rows.start, rows.start + m)
            x = _unpack_rows(x_ref[r, :]).astype(BF16)
            bgu = bgu_ref[e]
            gt = jnp.dot(x, wgu_s[:, :D_FF], preferred_element_type=F32) + bgu[:, :D_FF]
            ln = jnp.dot(x, wgu_s[:, D_FF:], preferred_element_type=F32) + bgu[:, D_FF:]
            gt = jnp.minimum(gt, SWIGLU_LIMIT)
            ln = jnp.clip(ln, -SWIGLU_LIMIT, SWIGLU_LIMIT)
            act = gt * jax.nn.sigmoid(SWIGLU_ALPHA * gt) * (ln + 1.0)
            y = jnp.dot(act.astype(BF16), wd_s[...], preferred_element_type=F32) + bd_ref[e]
            y_ref[r, :] = _pack_rows(y)
            if m < TM_EXPERT:
                y_ref[rows.start + m:rows.stop, :] = jnp.zeros((TM_EXPERT - m, ROW_WORDS), U32)

        quanta = (valid_ref[i] + EXPERT_ROW_QUANTUM - 1) // EXPERT_ROW_QUANTUM
        for q in range(1, TM_EXPERT // EXPERT_ROW_QUANTUM + 1):
            pl.when(quanta == q)(functools.partial(ffn, q * EXPERT_ROW_QUANTUM))

    @pl.when(i >= nu_ref[0])
    def _():
        y_ref[rows, :] = jnp.zeros((TM_EXPERT, ROW_WORDS), U32)


def _experts(block_e, first, next_e, valid, n_used, xs, w_gu, b_gu, w_down, b_down):
    d = D_MODEL
    nb = block_e.shape[0]
    per_step = EXPERT_BLOCKS_PER_STEP
    tm = TM_EXPERT * per_step
    assert nb % per_step == 0

    xmap = lambda i, be, fi, nx, va, nu: (jnp.minimum(i, (nu[0] - 1) // per_step), 0)
    grid_spec = pltpu.PrefetchScalarGridSpec(
        num_scalar_prefetch=5,
        grid=(nb // per_step,),
        in_specs=[pl.BlockSpec((tm, ROW_WORDS), xmap),
                  pl.BlockSpec(memory_space=pl.ANY),
                  _const_spec((N_EXPERTS, 1, 2 * D_FF)),
                  pl.BlockSpec(memory_space=pl.ANY),
                  _const_spec((N_EXPERTS, 1, d))],
        out_specs=pl.BlockSpec((tm, ROW_WORDS), lambda i, be, fi, nx, va, nu: (i, 0)),
        scratch_shapes=[pltpu.VMEM((d, 2 * D_FF), F32), pltpu.VMEM((D_FF, d), F32),
                        pltpu.VMEM((d, 2 * D_FF), BF16), pltpu.VMEM((D_FF, d), BF16),
                        pltpu.SemaphoreType.DMA((2,))],
    )
    return pl.pallas_call(
        _expert_kernel,
        grid_spec=grid_spec,
        out_shape=jax.ShapeDtypeStruct((nb * TM_EXPERT, ROW_WORDS), U32),
        compiler_params=_params("arbitrary"),
        name="moe_experts",
    )(block_e, first, next_e, valid, n_used, xs, w_gu, b_gu.reshape(N_EXPERTS, 1, 2 * D_FF),
      w_down, b_down.reshape(N_EXPERTS, 1, d))


def _sc_gather_rows(table, idx):
    m = idx.shape[0]
    w = table.shape[1]
    win = SC_GATHER_WINDOW
    per_worker = m // SC_WORKERS
    n_win = per_worker // win
    assert per_worker * SC_WORKERS == m and n_win * win == per_worker and n_win % 2 == 0
    mesh = plsc.VectorSubcoreMesh(core_axis_name="c", subcore_axis_name="s")

    @functools.partial(
        pl.kernel, mesh=mesh,
        out_type=jax.ShapeDtypeStruct((m, w), table.dtype),
        scratch_types=[pltpu.VMEM((win,), jnp.int32), pltpu.VMEM((win,), jnp.int32),
                       pltpu.VMEM((win, w), table.dtype), pltpu.VMEM((win, w), table.dtype),
                       pltpu.SemaphoreType.DMA((4,))],
        name="sc_gather_rows",
    )
    def gather(table_hbm, idx_hbm, out_hbm, idx0, idx1, rows0, rows1, sems):
        wid = _sc_worker_id()
        idxs, rows = (idx0, idx1), (rows0, rows1)

        def out_window(j):
            return out_hbm.at[pl.ds(wid * per_worker + j * win, win)]

        def start_gather(j, slot):
            pltpu.sync_copy(idx_hbm.at[pl.ds(wid * per_worker + j * win, win)], idxs[slot])
            pltpu.async_copy(table_hbm.at[idxs[slot]], rows[slot], sems.at[slot])

        def wait_gather(slot):
            pltpu.make_async_copy(table_hbm.at[idxs[slot]], rows[slot], sems.at[slot]).wait()

        def wait_write(j, slot):
            pltpu.make_async_copy(rows[slot], out_window(j), sems.at[2 + slot]).wait()

        start_gather(0, 0)

        @pl.loop(0, n_win, step=2)
        def _(j0):
            for slot in range(2):
                j = j0 + slot
                other = 1 - slot

                @pl.when(j >= 1)
                def _():
                    wait_write(j - 1, other)

                @pl.when(j + 1 < n_win)
                def _():
                    start_gather(j + 1, other)

                wait_gather(slot)
                pltpu.async_copy(rows[slot], out_window(j), sems.at[2 + slot])

        wait_write(n_win - 1, (n_win - 1) % 2)

    return gather(table, idx)


def _combine_dense_kernel(y4_ref, gate_ref, x1_ref, mod_ref, fg_ref, o_ref):
    gate = gate_ref[...]
    y = gate[:, 0:1] * _unpack_rows(y4_ref[0])
    for k in range(1, TOP_K):
        y = y + gate[:, k:k + 1] * _unpack_rows(y4_ref[k])
    x2 = x1_ref[...] + mod_ref[0, 5:6, :] * y
    o_ref[...] = x2 * lax.rsqrt(jnp.mean(x2 * x2, axis=-1, keepdims=True) + NORM_EPS) * fg_ref[...]


def _combine_dense(y4, gate_w, x1, mod, final_g):
    b, l, d = x1.shape
    ts = 512
    nt = l // ts
    return pl.pallas_call(
        _combine_dense_kernel,
        grid=(b * nt,),
        in_specs=[pl.BlockSpec((TOP_K, ts, ROW_WORDS), lambda s: (0, s, 0)),
                  pl.BlockSpec((ts, TOP_K), lambda s: (s, 0)),
                  pl.BlockSpec((ts, d), lambda s: (s, 0)),
                  pl.BlockSpec((1, 6, d), lambda s: (s // nt, 0, 0)),
                  _const_spec((1, d))],
        out_specs=pl.BlockSpec((ts, d), lambda s: (s, 0)),
        out_shape=jax.ShapeDtypeStruct((b * l, d), F32),
        compiler_params=_params("arbitrary"),
        name="moe_combine",
    )(y4, gate_w, x1.reshape(b * l, d), mod, final_g).reshape(b, l, d)


def _rope_tables(seq_len):
    inv_freq = ROPE_BASE ** (-jnp.arange(ROPE_PAIRS, dtype=F32) / ROPE_PAIRS)
    reps = LANES // HEAD_DIM
    out = []
    for n_pos, on_row in ((seq_len // GRID_W, True), (GRID_W, False)):
        ang = jnp.arange(n_pos, dtype=F32)[:, None] * inv_freq
        zero = jnp.zeros_like(ang)
        for fn, sign in ((jnp.cos, 1.0), (jnp.sin, -1.0)):
            v = fn(ang)
            head = ([sign * v, v, zero, zero] if on_row else [zero, zero, sign * v, v])
            out.append(jnp.concatenate(head * reps, axis=1))
    return out


def kernel(x, c, ctx, c_ctx, w_ada, b_ada, norm1_g, norm2_g, w_in, b_in, attn_sink, w_pool,
           pool_scale, w_attn_br, w_pool_br, w_out, w_router, b_router, w_gu, b_gu, w_down,
           b_down, final_g):
    b, l, d = x.shape
    n = b * l
    assert w_ada.shape[0] == 1, "single-layer block"

    cond = jnp.zeros((SUBLANES, d), F32).at[:b].set(c).at[b].set(c_ctx)
    mod = _adaln(cond, w_ada[0], b_ada[0])[:b + 1].reshape(b + 1, 6, d)
    mod_x, mod_c = mod[:b], mod[b:b + 1]

    rope = _rope_tables(l)
    v0 = ATTN_W + KV_W
    w_v = w_in[0][:, v0:v0 + KV_W].reshape(d, N_KV_HEADS, HEAD_DIM)
    w_v = jnp.concatenate([w_v, jnp.zeros_like(w_v)], axis=-1).reshape(d, VEXT_W)
    b_v = b_in[0][v0:v0 + KV_W].reshape(N_KV_HEADS, HEAD_DIM)
    b_v = jnp.concatenate([b_v, jnp.ones_like(b_v)], axis=-1).reshape(VEXT_W)
    w_in_bf = jnp.concatenate([w_in[0][:, :v0], w_v, w_in[0][:, v0 + KV_W:]], axis=1).astype(BF16)
    b_in2 = jnp.concatenate([b_in[0][:v0], b_v, b_in[0][v0 + KV_W:]]).reshape(1, IN_W_EXT)
    g1 = norm1_g[0].reshape(1, d)
    q, kt, v, u, ga, gp = _inproj(x, mod_x, g1, w_in_bf, b_in2, rope)
    kv_sl = slice(ATTN_W, ATTN_W + KV_W + VEXT_W)
    kxt, vx = _ctx_kv(ctx, mod_c, g1, w_in_bf[:, kv_sl], b_in2[:, kv_sl])

    attn_o = _attention(q, kt, v, kxt, vx, attn_sink[0])

    x1, h2, top_idx, gate_w, rank, counts = _merge_route(
        attn_o, u, ga, gp, x, mod_x, w_pool[0].astype(BF16), pool_scale[0].reshape(1, d),
        w_attn_br[0].astype(BF16), w_pool_br[0].astype(BF16), w_out[0].astype(BF16),
        norm2_g[0].reshape(1, d), w_router[0].astype(BF16), b_router[0].reshape(1, N_EXPERTS))

    tm = TM_EXPERT
    nb = n * TOP_K // tm + N_EXPERTS
    cnt = counts[:, 0].astype(jnp.int32)
    padded = (cnt + tm - 1) // tm * tm
    pend = jnp.cumsum(padded)
    pstart = pend - padded
    expert_ids = jnp.arange(N_EXPERTS, dtype=jnp.int32)
    dest = jnp.sum(jnp.where(top_idx[..., None] == expert_ids, pstart, 0), axis=-1) + rank
    n_used = (pend[-1] // tm).reshape(1)
    block_start = jnp.arange(nb, dtype=jnp.int32) * tm
    block_e = jnp.minimum(jnp.sum((pend[None, :] <= block_start[:, None]).astype(jnp.int32), axis=1),
                          N_EXPERTS - 1)
    first = jnp.concatenate([jnp.ones((1,), jnp.int32),
                             (block_e[1:] != block_e[:-1]).astype(jnp.int32)])
    later_used = (expert_ids[None, :] > expert_ids[:, None]) & (cnt[None, :] > 0)
    next_of_expert = jnp.min(jnp.where(later_used, expert_ids[None, :], N_EXPERTS), axis=1)
    next_of_expert = jnp.where(next_of_expert < N_EXPERTS, next_of_expert, -1)
    next_e = jnp.sum(jnp.where(block_e[:, None] == expert_ids, next_of_expert, 0), axis=-1)

    qm = EXPERT_ROW_QUANTUM
    pad_j = jnp.arange(qm, dtype=jnp.int32)[None, :]
    n_pad = (cnt + qm - 1) // qm * qm - cnt
    spare = nb * tm + expert_ids[:, None] * qm + pad_j
    pad_dest = jnp.where(pad_j < n_pad[:, None], (pstart + cnt)[:, None] + pad_j, spare)
    pad_dest = pad_dest.reshape(SC_WORKERS, N_EXPERTS * qm // (SC_WORKERS * SC_WINDOW), SC_WINDOW)

    xs = _sc_scatter_rows(h2, dest, pad_dest, nb * tm + N_EXPERTS * qm)
    used_end = jnp.sum(jnp.where(block_e[:, None] == expert_ids, pstart + cnt, 0), axis=-1)
    valid = jnp.clip(used_end - block_start, 0, tm)
    ys = _experts(block_e, first, next_e, valid, n_used, xs, w_gu[0], b_gu[0], w_down[0], b_down[0])
    y4 = _sc_gather_rows(ys, dest.reshape(-1)).reshape(TOP_K, n, ROW_WORDS)
    return _combine_dense(y4, gate_w, x1, mod_x, final_g.reshape(1, d))
```

```python
import functools

import jax
import jax.numpy as jnp
from jax import lax
from jax.experimental import pallas as pl
from jax.experimental.pallas import tpu as pltpu
from jax.experimental.pallas import tpu_sc as plsc

D_MODEL = 1024
GRID_W = 64
HEAD_DIM = 64
N_HEADS = 16
N_KV_HEADS = 4
GROUP = N_HEADS // N_KV_HEADS
ATTN_W = N_HEADS * HEAD_DIM
KV_W = N_KV_HEADS * HEAD_DIM
WINDOW = 128
ATTN_SCALE = HEAD_DIM ** -0.5
ROPE_BASE = 10000.0
ROPE_PAIRS = HEAD_DIM // 4
POOL_WINDOWS = (2, 4, 8, 16)
POOL_GROUP_W = D_MODEL // len(POOL_WINDOWS)
IN_W = ATTN_W + 2 * KV_W + D_MODEL + 2 * D_MODEL
VEXT_W = N_KV_HEADS * 2 * HEAD_DIM
IN_W_EXT = IN_W - KV_W + VEXT_W
N_EXPERTS = 32
TOP_K = 4
D_FF = D_MODEL
SWIGLU_ALPHA = 1.702
SWIGLU_LIMIT = 7.0
NORM_EPS = 1e-5
NEG_INF = -1e30

LANES = 128
SUBLANES = 8
VMEM_LIMIT = 56 * 1024 * 1024

TM_INPROJ = 1024
INPROJ_SUB = 512
TQ = WINDOW
ATTN_Q_BLOCKS = 4
TM_MERGE = 1024
MERGE_SUB = 256
TM_EXPERT = 512
EXPERT_ROW_QUANTUM = 128
EXPERT_BLOCKS_PER_STEP = 1
POOL_HALO = SUBLANES

F32 = jnp.float32
BF16 = jnp.bfloat16


ROW_WORDS = D_MODEL // 2
U32 = jnp.uint32


def _pack_rows(value):
    bits = lax.bitcast_convert_type(value.astype(BF16).astype(F32), U32)
    return bits[:, :ROW_WORDS] | (bits[:, ROW_WORDS:] >> 16)


def _unpack_rows(words):
    hi = lax.bitcast_convert_type(words & jnp.uint32(0xFFFF0000), F32)
    lo = lax.bitcast_convert_type(words << 16, F32)
    return jnp.concatenate([hi, lo], axis=1)


def _params(*sem):
    return pltpu.CompilerParams(dimension_semantics=sem, vmem_limit_bytes=VMEM_LIMIT)


def _const_spec(shape):
    nd = len(shape)
    return pl.BlockSpec(shape, lambda *_: (0,) * nd, pipeline_mode=pl.Buffered(1))


def _adaln_kernel(c_ref, w_ref, b_ref, o_ref):
    c = c_ref[...]
    s = c * jax.nn.sigmoid(c)
    o_ref[...] = jnp.dot(s.astype(BF16), w_ref[...].astype(BF16),
                         preferred_element_type=F32) + b_ref[...]


def _adaln(cond, w_ada, b_ada):
    rows, d = cond.shape
    n = w_ada.shape[1]
    tn = 1024
    return pl.pallas_call(
        _adaln_kernel,
        grid=(n // tn,),
        in_specs=[_const_spec((rows, d)),
                  pl.BlockSpec((d, tn), lambda j: (0, j)),
                  pl.BlockSpec((1, tn), lambda j: (0, j))],
        out_specs=pl.BlockSpec((rows, tn), lambda j: (0, j)),
        out_shape=jax.ShapeDtypeStruct((rows, n), F32),
        compiler_params=_params("arbitrary"),
        name="adaln",
    )(cond, w_ada, b_ada.reshape(1, n))


def _norm_mod(x, g, shift, scale):
    y = x * lax.rsqrt(jnp.mean(x * x, axis=-1, keepdims=True) + NORM_EPS) * g
    return y * (1.0 + scale) + shift


def _rope(t, cos, sin_signed):
    w = t.shape[1]
    half = ROPE_PAIRS
    lane = lax.broadcasted_iota(jnp.int32, t.shape, 1)
    first = (lane & (2 * half - 1)) < half
    rot = jnp.where(first, pltpu.roll(t, w - half, 1), pltpu.roll(t, half, 1))
    reps = w // LANES
    cos_w = jnp.concatenate([cos] * reps, axis=1) if reps > 1 else cos
    sin_w = jnp.concatenate([sin_signed] * reps, axis=1) if reps > 1 else sin_signed
    return t * cos_w + rot * sin_w


def _token_table(row_ref, col_ref):
    gr = row_ref.shape[0]
    shape = (gr, GRID_W, LANES)
    full = (jnp.broadcast_to(row_ref[...][:, None, :], shape)
            + jnp.broadcast_to(col_ref[...][None, :, :], shape))
    return full.reshape(gr * GRID_W, LANES)


def _inproj_kernel(x_ref, mod_ref, g_ref, w_ref, b_ref, cr_ref, sr_ref, cc_ref, sc_ref,
                   q_ref, k_ref, v_ref, u_ref, ga_ref, gp_ref):
    sub = INPROJ_SUB
    cos_all = _token_table(cr_ref, cc_ref)
    sin_all = _token_table(sr_ref, sc_ref)
    cw = 512

    def work_items(r0):
        rs = slice(r0, r0 + sub)
        cos, sin = cos_all[rs], sin_all[rs]
        hb = _norm_mod(x_ref[0, rs, :], g_ref[...], mod_ref[0, 0:1, :], mod_ref[0, 1:2, :]).astype(BF16)

        def put_k(t):
            k_ref[0, :, rs] = _rope(t, cos, sin).T.astype(BF16)

        def put_v(t):
            v_ref[0, rs, :] = t.astype(BF16)

        items = []
        for c0 in range(0, ATTN_W, cw):
            def put_q(t, c0=c0):
                q_ref[0, rs, c0:c0 + cw] = (_rope(t, cos, sin) * ATTN_SCALE).astype(BF16)
            items.append((c0, cw, put_q))
        items.append((ATTN_W, KV_W, put_k))
        items.append((ATTN_W + KV_W, VEXT_W, put_v))
        base = ATTN_W + KV_W + VEXT_W
        for c0 in range(0, D_MODEL, cw):
            def put_u(t, c0=c0):
                u_ref[0, rs, c0:c0 + cw] = t
            items.append((base + c0, cw, put_u))
        for ref in (ga_ref, gp_ref):
            base += D_MODEL
            for c0 in range(0, D_MODEL, cw):
                def put_gate(t, ref=ref, c0=c0):
                    ref[0, rs, c0:c0 + cw] = jax.nn.sigmoid(t).astype(BF16)
                items.append((base + c0, cw, put_gate))
        return [(hb, c0, w, fn) for c0, w, fn in items]

    def proj(hb, c0, w):
        return jnp.dot(hb, w_ref[:, c0:c0 + w], preferred_element_type=F32) + b_ref[:, c0:c0 + w]

    items = [it for r0 in range(0, x_ref.shape[1], sub) for it in work_items(r0)]
    pending = proj(*items[0][:3])
    for ci, item in enumerate(items):
        ready = pending
        if ci + 1 < len(items):
            pending = proj(*items[ci + 1][:3])
        item[3](ready)


def _inproj(x, mod, norm_g, w_in_bf, b_in, rope):
    b, l, d = x.shape
    tm = TM_INPROJ
    row = lambda bi, i: (bi, i, 0)
    out_shape = (
        jax.ShapeDtypeStruct((b, l, ATTN_W), BF16),
        jax.ShapeDtypeStruct((b, KV_W, l), BF16),
        jax.ShapeDtypeStruct((b, l, VEXT_W), BF16),
        jax.ShapeDtypeStruct((b, l, D_MODEL), F32),
        jax.ShapeDtypeStruct((b, l, D_MODEL), BF16),
        jax.ShapeDtypeStruct((b, l, D_MODEL), BF16),
    )
    return pl.pallas_call(
        _inproj_kernel,
        grid=(b, l // tm),
        in_specs=[pl.BlockSpec((1, tm, d), row),
                  pl.BlockSpec((1, 6, d), lambda bi, i: (bi, 0, 0)),
                  _const_spec((1, d)),
                  _const_spec((d, IN_W_EXT)),
                  _const_spec((1, IN_W_EXT)),
                  pl.BlockSpec((tm // GRID_W, LANES), lambda bi, i: (i, 0)),
                  pl.BlockSpec((tm // GRID_W, LANES), lambda bi, i: (i, 0)),
                  _const_spec((GRID_W, LANES)),
                  _const_spec((GRID_W, LANES))],
        out_specs=(pl.BlockSpec((1, tm, ATTN_W), row),
                   pl.BlockSpec((1, KV_W, tm), lambda bi, i: (bi, 0, i)),
                   pl.BlockSpec((1, tm, VEXT_W), row),
                   pl.BlockSpec((1, tm, D_MODEL), row),
                   pl.BlockSpec((1, tm, D_MODEL), row),
                   pl.BlockSpec((1, tm, D_MODEL), row)),
        out_shape=out_shape,
        compiler_params=_params("arbitrary", "arbitrary"),
        name="inproj",
    )(x, mod, norm_g, w_in_bf, b_in, *rope)


def _ctx_kv_kernel(x_ref, mod_ref, g_ref, w_ref, b_ref, k_ref, v_ref):
    h = _norm_mod(x_ref[0], g_ref[...], mod_ref[0, 0:1, :], mod_ref[0, 1:2, :])
    kv = jnp.dot(h.astype(BF16), w_ref[...], preferred_element_type=F32) + b_ref[...]
    k_ref[0] = kv[:, :KV_W].T.astype(BF16)
    v_ref[0] = kv[:, KV_W:].astype(BF16)


def _ctx_kv(ctx, mod_c, norm_g, w_kv_bf, b_kv):
    b, lc, d = ctx.shape
    row = lambda bi: (bi, 0, 0)
    return pl.pallas_call(
        _ctx_kv_kernel,
        grid=(b,),
        in_specs=[pl.BlockSpec((1, lc, d), row),
                  _const_spec((1, 6, d)),
                  _const_spec((1, d)),
                  _const_spec((d, KV_W + VEXT_W)),
                  _const_spec((1, KV_W + VEXT_W))],
        out_specs=(pl.BlockSpec((1, KV_W, lc), row), pl.BlockSpec((1, lc, VEXT_W), row)),
        out_shape=(jax.ShapeDtypeStruct((b, KV_W, lc), BF16),
                   jax.ShapeDtypeStruct((b, lc, VEXT_W), BF16)),
        compiler_params=_params("arbitrary"),
        name="ctx_kv",
    )(ctx, mod_c, norm_g, w_kv_bf, b_kv)


def _attn_kernel(sink_ref, q_ref, *refs):
    nq = ATTN_Q_BLOCKS
    k_refs, v_refs = refs[:nq + 2], refs[nq + 2:2 * nq + 4]
    kx_ref, vx_ref, o_ref = refs[2 * nq + 4:]
    first_blk = pl.program_id(1) * nq
    last_blk = pl.num_programs(1) * nq - 1
    rows = GROUP * TQ
    pair_w = 2 * HEAD_DIM
    qi = lax.broadcasted_iota(jnp.int32, (rows, TQ), 0) & (TQ - 1)
    col = lax.broadcasted_iota(jnp.int32, (rows, TQ), 1)
    low_half = lax.broadcasted_iota(jnp.int32, (TQ, pair_w), 1) < HEAD_DIM
    row_id = lax.broadcasted_iota(jnp.int32, (rows, 1), 0)

    def scores(b, j):
        qrows = slice(b * TQ, (b + 1) * TQ)
        parts = []
        for g in range(GROUP):
            h = j * GROUP + g
            pair = q_ref[0, qrows, (h // 2) * pair_w:(h // 2 + 1) * pair_w]
            parts.append(jnp.where(low_half if h % 2 == 0 else ~low_half, pair, jnp.zeros_like(pair)))
        q4 = jnp.concatenate(parts, axis=0)
        ks = slice(j * HEAD_DIM, (j + 1) * HEAD_DIM)
        kt = jnp.concatenate([k_refs[b + d][0, ks, :] for d in range(3)], axis=1)
        s_loc = jnp.dot(q4, jnp.concatenate([kt, kt], axis=0), preferred_element_type=F32)
        kxt = kx_ref[0, ks, :]
        s_ctx = jnp.dot(q4, jnp.concatenate([kxt, kxt], axis=0), preferred_element_type=F32)
        return s_loc, s_ctx

    def probs(b, j, s_loc, s_ctx):
        keep_prev = (col >= qi) & (first_blk + b > 0)
        keep_next = (col <= qi) & (first_blk + b < last_blk)
        pieces = [jnp.where(keep_prev, s_loc[:, :TQ], NEG_INF), s_loc[:, TQ:2 * TQ],
                  jnp.where(keep_next, s_loc[:, 2 * TQ:], NEG_INF)]
        pieces += [s_ctx[:, c0:c0 + TQ] for c0 in range(0, s_ctx.shape[1], TQ)]
        sink = jnp.zeros((rows, 1), F32)
        for g in range(GROUP):
            sink = jnp.where(row_id // TQ == g, sink_ref[j * GROUP + g], sink)
        mx = pieces[0]
        for pc in pieces[1:]:
            mx = jnp.maximum(mx, pc)
        m = jnp.maximum(jnp.max(mx, axis=-1, keepdims=True), sink)
        p = jnp.concatenate([jnp.exp(pc - m).astype(BF16) for pc in pieces], axis=1)
        return p, jnp.exp(sink - m)

    def output(b, j, p, sink_p):
        vs = slice(j * pair_w, (j + 1) * pair_w)
        v_all = jnp.concatenate([v_refs[b + d][0, :, vs] for d in range(3)]
                                + [vx_ref[0, :, vs]], axis=0)
        r = jnp.dot(p, v_all, preferred_element_type=F32)
        den = pltpu.roll(r, HEAD_DIM, 1) + sink_p
        o = r / den
        for g in range(0, GROUP, 2):
            even = o[g * TQ:(g + 1) * TQ]
            odd = pltpu.roll(o[(g + 1) * TQ:(g + 2) * TQ], HEAD_DIM, 1)
            c0 = (j * GROUP + g) * HEAD_DIM
            o_ref[0, b * TQ:(b + 1) * TQ, c0:c0 + pair_w] = jnp.where(low_half, even, odd).astype(BF16)

    items = [(b, j) for b in range(nq) for j in range(N_KV_HEADS)]
    s_queue = [scores(*items[0]), scores(*items[1])]
    p_queue = [probs(*items[0], *s_queue.pop(0))]
    for i, item in enumerate(items):
        if i + 2 < len(items):
            s_queue.append(scores(*items[i + 2]))
        ready = p_queue.pop(0)
        if i + 1 < len(items):
            p_queue.append(probs(*items[i + 1], *s_queue.pop(0)))
        output(*item, *ready)


def _attention(q, kt, v, kxt, vx, sink):
    b, l, _ = q.shape
    lc = kxt.shape[2]
    nq = ATTN_Q_BLOCKS
    nblk = l // TQ
    assert nblk % nq == 0

    def blk(n, d):
        return jnp.clip(n * nq + d - 1, 0, nblk - 1)

    k_specs = [pl.BlockSpec((1, KV_W, TQ), lambda bi, n, d=d: (bi, 0, blk(n, d))) for d in range(nq + 2)]
    v_specs = [pl.BlockSpec((1, TQ, VEXT_W), lambda bi, n, d=d: (bi, blk(n, d), 0)) for d in range(nq + 2)]
    return pl.pallas_call(
        _attn_kernel,
        grid=(b, nblk // nq),
        in_specs=[pl.BlockSpec(memory_space=pltpu.SMEM),
                  pl.BlockSpec((1, nq * TQ, ATTN_W), lambda bi, n: (bi, n, 0))]
                 + k_specs + v_specs
                 + [pl.BlockSpec((1, KV_W, lc), lambda bi, n: (bi, 0, 0)),
                    pl.BlockSpec((1, lc, VEXT_W), lambda bi, n: (bi, 0, 0))],
        out_specs=pl.BlockSpec((1, nq * TQ, ATTN_W), lambda bi, n: (bi, n, 0)),
        out_shape=jax.ShapeDtypeStruct((b, l, ATTN_W), BF16),
        compiler_params=_params("arbitrary", "arbitrary"),
        name="attention",
    )(sink, q, *([kt] * (nq + 2)), *([v] * (nq + 2)), kxt, vx)


def _shift_rows(a, d):
    n = a.shape[0]
    return pltpu.roll(a, (-d) % n, 0)


def _merge_kernel(ao_ref, u_ref, up_ref, un_ref, sa_ref, sp_ref, x_ref, mod_ref,
                  wpool_ref, pscale_ref, wab_ref, wpb_ref, wout_ref, g2_ref, wr_ref, br_ref,
                  x1_ref, h2_ref, idx_ref, gate_ref, rank_ref, cnt_ref, carry_ref, *, seq_len):
    bi = pl.program_id(0)
    i = pl.program_id(1)
    last = pl.num_programs(1) - 1
    tm, sub = TM_MERGE, MERGE_SUB

    @pl.when((bi == 0) & (i == 0))
    def _():
        carry_ref[...] = jnp.zeros_like(carry_ref)

    u = u_ref[0]
    prev = jnp.where(i > 0, up_ref[0], 0.0)
    nxt = jnp.where(i < last, un_ref[0], 0.0)
    ext = jnp.concatenate([prev, u, nxt], axis=0)

    def pool_diff(r0):
        t = i * tm + r0 + lax.broadcasted_iota(jnp.int32, (sub, 1), 0)
        diffs = []
        for g, w in enumerate(POOL_WINDOWS):
            cs = slice(g * POOL_GROUP_W, (g + 1) * POOL_GROUP_W)
            e = ext[r0:r0 + sub + 2 * POOL_HALO, cs]
            acc = _shift_rows(e, -1) + e
            step = 1
            while 2 * step < w:
                acc = _shift_rows(acc, -step) + _shift_rows(acc, step)
                step *= 2
            win = acc[POOL_HALO:POOL_HALO + sub]
            half = w // 2
            cnt = (jnp.minimum(t + half, seq_len) - jnp.maximum(t - half, 0)).astype(F32)
            diffs.append((win / cnt - u[r0:r0 + sub, cs]).astype(BF16))
        return diffs

    def branches(r0, diffs):
        rs = slice(r0, r0 + sub)
        a = jnp.dot(ao_ref[0, rs, :], wab_ref[...], preferred_element_type=F32)
        mixed = [jnp.dot(df, wpool_ref[g], preferred_element_type=F32) for g, df in enumerate(diffs)]
        pool_o = jnp.concatenate(mixed, axis=1) * pscale_ref[...]
        p = jnp.dot(pool_o.astype(BF16), wpb_ref[...], preferred_element_type=F32)
        merged = sa_ref[0, rs, :].astype(F32) * a + sp_ref[0, rs, :].astype(F32) * p
        return merged.astype(BF16)

    def out_proj(merged):
        return jnp.dot(merged, wout_ref[...], preferred_element_type=F32)

    def finish(r0, o):
        rs = slice(r0, r0 + sub)
        x1 = x_ref[0, rs, :] + mod_ref[0, 2:3, :] * o
        x1_ref[0, rs, :] = x1
        h2 = _norm_mod(x1, g2_ref[...], mod_ref[0, 3:4, :], mod_ref[0, 4:5, :])
        h2_ref[rs, :] = _pack_rows(h2)
        return jnp.dot(h2.astype(BF16), wr_ref[...], preferred_element_type=F32) + br_ref[...]

    def route(r0, logits, carry):
        lt = logits.T
        eid = lax.broadcasted_iota(jnp.int32, (N_EXPERTS, sub), 0).astype(F32)
        work = lt
        vals, idxs, sels = [], [], []
        for _ in range(TOP_K):
            mx = jnp.max(work, axis=0, keepdims=True)
            ix = jnp.min(jnp.where(work == mx, eid, float(N_EXPERTS)), axis=0, keepdims=True)
            sel = eid == ix
            work = jnp.where(sel, -jnp.inf, work)
            vals.append(mx)
            idxs.append(ix)
            sels.append(sel)
        ex = [jnp.exp(v - vals[0]) for v in vals]
        tot = ex[0] + ex[1] + ex[2] + ex[3]
        pad = [jnp.zeros_like(tot)] * (SUBLANES - TOP_K)
        gates = jnp.concatenate([e_ / tot for e_ in ex] + pad, axis=0)
        gate_ref[r0:r0 + sub, :] = gates.T[:, :TOP_K]
        idx_ref[:, r0:r0 + sub] = jnp.concatenate(idxs, axis=0).astype(jnp.int32)
        member = (sels[0] | sels[1] | sels[2] | sels[3]).astype(F32)
        t_row = lax.broadcasted_iota(jnp.int32, (sub, sub), 0)
        t_col = lax.broadcasted_iota(jnp.int32, (sub, sub), 1)
        earlier = (t_row < t_col).astype(BF16)
        before = jnp.dot(member.astype(BF16), earlier, preferred_element_type=F32) + carry
        ranks = [jnp.sum(jnp.where(s, before, 0.0), axis=0, keepdims=True) for s in sels]
        rank_ref[:, r0:r0 + sub] = jnp.concatenate(ranks, axis=0).astype(jnp.int32)
        return carry + jnp.sum(member, axis=1, keepdims=True)

    starts = list(range(0, tm, sub))
    carry = carry_ref[...]
    merged = branches(starts[0], pool_diff(starts[0]))
    for si, r0 in enumerate(starts):
        o = out_proj(merged)
        if si + 1 < len(starts):
            merged = branches(starts[si + 1], pool_diff(starts[si + 1]))
        carry = route(r0, finish(r0, o), carry)
    carry_ref[...] = carry
    cnt_ref[...] = carry


def _merge_route(attn_o, u, ga, gp, x, mod, w_pool_bf, pool_scale, wab, wpb, wout,
                 norm2_g, w_router, b_router):
    b, l, d = x.shape
    tm = TM_MERGE
    nt = l // tm
    n = b * l
    hb = tm // POOL_HALO
    row = lambda bi, i: (bi, i, 0)
    flat = lambda bi, i: (bi * nt + i, 0)
    halo_prev = lambda bi, i: (bi, jnp.maximum(i * hb - 1, 0), 0)
    halo_next = lambda bi, i: (bi, jnp.minimum((i + 1) * hb, l // POOL_HALO - 1), 0)
    slot_major = lambda bi, i: (0, bi * nt + i)
    out_shape = (
        jax.ShapeDtypeStruct((b, l, d), F32),
        jax.ShapeDtypeStruct((n, ROW_WORDS), U32),
        jax.ShapeDtypeStruct((TOP_K, n), jnp.int32),
        jax.ShapeDtypeStruct((n, TOP_K), F32),
        jax.ShapeDtypeStruct((TOP_K, n), jnp.int32),
        jax.ShapeDtypeStruct((N_EXPERTS, 1), F32),
    )
    return pl.pallas_call(
        functools.partial(_merge_kernel, seq_len=l),
        grid=(b, nt),
        in_specs=[pl.BlockSpec((1, tm, d), row),
                  pl.BlockSpec((1, tm, d), row),
                  pl.BlockSpec((1, POOL_HALO, d), halo_prev),
                  pl.BlockSpec((1, POOL_HALO, d), halo_next),
                  pl.BlockSpec((1, tm, d), row),
                  pl.BlockSpec((1, tm, d), row),
                  pl.BlockSpec((1, tm, d), row),
                  pl.BlockSpec((1, 6, d), lambda bi, i: (bi, 0, 0)),
                  _const_spec(w_pool_bf.shape),
                  _const_spec((1, d)),
                  _const_spec((d, d)), _const_spec((d, d)), _const_spec((d, d)),
                  _const_spec((1, d)),
                  _const_spec((d, N_EXPERTS)),
                  _const_spec((1, N_EXPERTS))],
        out_specs=(pl.BlockSpec((1, tm, d), row),
                   pl.BlockSpec((tm, ROW_WORDS), flat),
                   pl.BlockSpec((TOP_K, tm), slot_major),
                   pl.BlockSpec((tm, TOP_K), flat),
                   pl.BlockSpec((TOP_K, tm), slot_major),
                   _const_spec((N_EXPERTS, 1))),
        out_shape=out_shape,
        scratch_shapes=[pltpu.VMEM((N_EXPERTS, 1), F32)],
        compiler_params=_params("arbitrary", "arbitrary"),
        name="merge_route",
    )(attn_o, u, u, u, ga, gp, x, mod, w_pool_bf, pool_scale, wab, wpb, wout,
      norm2_g, w_router, b_router)


SC_CORES = 2
SC_SUBCORES = 16
SC_WORKERS = SC_CORES * SC_SUBCORES
SC_WINDOW = 128
SC_GATHER_WINDOW = 64


def _sc_worker_id():
    return lax.axis_index("s") * SC_CORES + lax.axis_index("c")


def _sc_scatter_rows(rows, dest, pad_dest, out_rows):
    n, w = rows.shape
    n_win = n // (SC_WORKERS * SC_WINDOW)
    n_pad_win = pad_dest.shape[1]
    assert dest.shape == (TOP_K, n) and SC_WORKERS * n_win * SC_WINDOW == n
    mesh = plsc.VectorSubcoreMesh(core_axis_name="c", subcore_axis_name="s")
    zeros = jnp.zeros((SC_WINDOW, w), rows.dtype)

    @functools.partial(
        pl.kernel, mesh=mesh,
        out_type=jax.ShapeDtypeStruct((out_rows, w), rows.dtype),
        scratch_types=[pltpu.VMEM((TOP_K, SC_WINDOW), jnp.int32),
                       pltpu.VMEM((n_pad_win, SC_WINDOW), jnp.int32),
                       pltpu.VMEM((SC_WINDOW, w), rows.dtype),
                       pltpu.SemaphoreType.DMA],
        name="sc_scatter_rows",
    )
    def scatter(rows_hbm, dest_hbm, pad_hbm, zeros_hbm, out_hbm, idx_v, pad_v, rows_v, sem):
        wid = _sc_worker_id()

        @pl.loop(0, n_win)
        def _(j):
            base = (wid * n_win + j) * SC_WINDOW
            pltpu.sync_copy(rows_hbm.at[pl.ds(base, SC_WINDOW)], rows_v)
            pltpu.sync_copy(dest_hbm.at[:, pl.ds(base, SC_WINDOW)], idx_v)
            copies = [pltpu.async_copy(rows_v, out_hbm.at[idx_v.at[k]], sem) for k in range(TOP_K)]
            for cp in copies:
                cp.wait()

        pltpu.sync_copy(zeros_hbm, rows_v)
        pltpu.sync_copy(pad_hbm.at[wid], pad_v)
        copies = [pltpu.async_copy(rows_v, out_hbm.at[pad_v.at[j]], sem) for j in range(n_pad_win)]
        for cp in copies:
            cp.wait()

    return scatter(rows, dest, pad_dest, zeros)


def _expert_kernel(be_ref, first_ref, next_ref, valid_ref, nu_ref, x_ref, wgu_hbm, bgu_ref, wd_hbm, bd_ref,
                   y_ref, wgu_f, wd_f, wgu_s, wd_s, sem):
    step = pl.program_id(0)

    def weight_copies(e):
        return (pltpu.make_async_copy(wgu_hbm.at[e], wgu_f, sem.at[0]),
                pltpu.make_async_copy(wd_hbm.at[e], wd_f, sem.at[1]))

    def start_weights(e):
        for cp in weight_copies(e):
            cp.start(priority=1)

    @pl.when(step == 0)
    def _():
        start_weights(be_ref[0])

    for sub in range(EXPERT_BLOCKS_PER_STEP):
        _expert_block(step * EXPERT_BLOCKS_PER_STEP + sub,
                      slice(sub * TM_EXPERT, (sub + 1) * TM_EXPERT),
                      be_ref, first_ref, next_ref, valid_ref, nu_ref, x_ref, bgu_ref, bd_ref, y_ref,
                      wgu_f, wd_f, wgu_s, wd_s, weight_copies, start_weights)


def _expert_block(i, rows, be_ref, first_ref, next_ref, valid_ref, nu_ref, x_ref, bgu_ref, bd_ref, y_ref,
                  wgu_f, wd_f, wgu_s, wd_s, weight_copies, start_weights):
    @pl.when(i < nu_ref[0])
    def _():
        e = be_ref[i]

        @pl.when(first_ref[i] == 1)
        def _():
            for cp in weight_copies(e):
                cp.wait()
            ck = 16

            def cast_rows(c, carry):
                r = pl.ds(pl.multiple_of(c * ck, ck), ck)
                wgu_s[r, :] = wgu_f[r, :].astype(BF16)
                wd_s[r, :] = wd_f[r, :].astype(BF16)
                return carry

            lax.fori_loop(0, D_MODEL // ck, cast_rows, 0, unroll=2)

            @pl.when(next_ref[i] >= 0)
            def _():
                start_weights(next_ref[i])

        def ffn(m):
            r = slice(rows.start, rows.start + m)
            x = _unpack_rows(x_ref[r, :]).astype(BF16)
            bgu = bgu_ref[e]
            gt = jnp.dot(x, wgu_s[:, :D_FF], preferred_element_type=F32) + bgu[:, :D_FF]
            ln = jnp.dot(x, wgu_s[:, D_FF:], preferred_element_type=F32) + bgu[:, D_FF:]
            gt = jnp.minimum(gt, SWIGLU_LIMIT)
            ln = jnp.clip(ln, -SWIGLU_LIMIT, SWIGLU_LIMIT)
            act = gt * jax.nn.sigmoid(SWIGLU_ALPHA * gt) * (ln + 1.0)
            y = jnp.dot(act.astype(BF16), wd_s[...], preferred_element_type=F32) + bd_ref[e]
            y_ref[r, :] = _pack_rows(y)
            if m < TM_EXPERT:
                y_ref[rows.start + m:rows.stop, :] = jnp.zeros((TM_EXPERT - m, ROW_WORDS), U32)

        quanta = (valid_ref[i] + EXPERT_ROW_QUANTUM - 1) // EXPERT_ROW_QUANTUM
        for q in range(1, TM_EXPERT // EXPERT_ROW_QUANTUM + 1):
            pl.when(quanta == q)(functools.partial(ffn, q * EXPERT_ROW_QUANTUM))

    @pl.when(i >= nu_ref[0])
    def _():
        y_ref[rows, :] = jnp.zeros((TM_EXPERT, ROW_WORDS), U32)


def _experts(block_e, first, next_e, valid, n_used, xs, w_gu, b_gu, w_down, b_down):
    d = D_MODEL
    nb = block_e.shape[0]
    per_step = EXPERT_BLOCKS_PER_STEP
    tm = TM_EXPERT * per_step
    assert nb % per_step == 0

    xmap = lambda i, be, fi, nx, va, nu: (jnp.minimum(i, (nu[0] - 1) // per_step), 0)
    grid_spec = pltpu.PrefetchScalarGridSpec(
        num_scalar_prefetch=5,
        grid=(nb // per_step,),
        in_specs=[pl.BlockSpec((tm, ROW_WORDS), xmap),
                  pl.BlockSpec(memory_space=pl.ANY),
                  _const_spec((N_EXPERTS, 1, 2 * D_FF)),
                  pl.BlockSpec(memory_space=pl.ANY),
                  _const_spec((N_EXPERTS, 1, d))],
        out_specs=pl.BlockSpec((tm, ROW_WORDS), lambda i, be, fi, nx, va, nu: (i, 0)),
        scratch_shapes=[pltpu.VMEM((d, 2 * D_FF), F32), pltpu.VMEM((D_FF, d), F32),
                        pltpu.VMEM((d, 2 * D_FF), BF16), pltpu.VMEM((D_FF, d), BF16),
                        pltpu.SemaphoreType.DMA((2,))],
    )
    return pl.pallas_call(
        _expert_kernel,
        grid_spec=grid_spec,
        out_shape=jax.ShapeDtypeStruct((nb * TM_EXPERT, ROW_WORDS), U32),
        compiler_params=_params("arbitrary"),
        name="moe_experts",
    )(block_e, first, next_e, valid, n_used, xs, w_gu, b_gu.reshape(N_EXPERTS, 1, 2 * D_FF),
      w_down, b_down.reshape(N_EXPERTS, 1, d))


def _sc_gather_rows(table, idx):
    m = idx.shape[0]
    w = table.shape[1]
    win = SC_GATHER_WINDOW
    per_worker = m // SC_WORKERS
    n_win = per_worker // win
    assert per_worker * SC_WORKERS == m and n_win * win == per_worker and n_win % 2 == 0
    mesh = plsc.VectorSubcoreMesh(core_axis_name="c", subcore_axis_name="s")

    @functools.partial(
        pl.kernel, mesh=mesh,
        out_type=jax.ShapeDtypeStruct((m, w), table.dtype),
        scratch_types=[pltpu.VMEM((win,), jnp.int32), pltpu.VMEM((win,), jnp.int32),
                       pltpu.VMEM((win, w), table.dtype), pltpu.VMEM((win, w), table.dtype),
                       pltpu.SemaphoreType.DMA((4,))],
        name="sc_gather_rows",
    )
    def gather(table_hbm, idx_hbm, out_hbm, idx0, idx1, rows0, rows1, sems):
        wid = _sc_worker_id()
        idxs, rows = (idx0, idx1), (rows0, rows1)

        def out_window(j):
            return out_hbm.at[pl.ds(wid * per_worker + j * win, win)]

        def start_gather(j, slot):
            pltpu.sync_copy(idx_hbm.at[pl.ds(wid * per_worker + j * win, win)], idxs[slot])
            pltpu.async_copy(table_hbm.at[idxs[slot]], rows[slot], sems.at[slot])

        def wait_gather(slot):
            pltpu.make_async_copy(table_hbm.at[idxs[slot]], rows[slot], sems.at[slot]).wait()

        def wait_write(j, slot):
            pltpu.make_async_copy(rows[slot], out_window(j), sems.at[2 + slot]).wait()

        start_gather(0, 0)

        @pl.loop(0, n_win, step=2)
        def _(j0):
            for slot in range(2):
                j = j0 + slot
                other = 1 - slot

                @pl.when(j >= 1)
                def _():
                    wait_write(j - 1, other)

                @pl.when(j + 1 < n_win)
                def _():
                    start_gather(j + 1, other)

                wait_gather(slot)
                pltpu.async_copy(rows[slot], out_window(j), sems.at[2 + slot])

        wait_write(n_win - 1, (n_win - 1) % 2)

    return gather(table, idx)


def _combine_dense_kernel(y4_ref, gate_ref, x1_ref, mod_ref, fg_ref, o_ref):
    gate = gate_ref[...]
    y = gate[:, 0:1] * _unpack_rows(y4_ref[0])
    for k in range(1, TOP_K):
        y = y + gate[:, k:k + 1] * _unpack_rows(y4_ref[k])
    x2 = x1_ref[...] + mod_ref[0, 5:6, :] * y
    o_ref[...] = x2 * lax.rsqrt(jnp.mean(x2 * x2, axis=-1, keepdims=True) + NORM_EPS) * fg_ref[...]


def _combine_dense(y4, gate_w, x1, mod, final_g):
    b, l, d = x1.shape
    ts = 512
    nt = l // ts
    return pl.pallas_call(
        _combine_dense_kernel,
        grid=(b * nt,),
        in_specs=[pl.BlockSpec((TOP_K, ts, ROW_WORDS), lambda s: (0, s, 0)),
                  pl.BlockSpec((ts, TOP_K), lambda s: (s, 0)),
                  pl.BlockSpec((ts, d), lambda s: (s, 0)),
                  pl.BlockSpec((1, 6, d), lambda s: (s // nt, 0, 0)),
                  _const_spec((1, d))],
        out_specs=pl.BlockSpec((ts, d), lambda s: (s, 0)),
        out_shape=jax.ShapeDtypeStruct((b * l, d), F32),
        compiler_params=_params("arbitrary"),
        name="moe_combine",
    )(y4, gate_w, x1.reshape(b * l, d), mod, final_g).reshape(b, l, d)


def _rope_tables(seq_len):
    inv_freq = ROPE_BASE ** (-jnp.arange(ROPE_PAIRS, dtype=F32) / ROPE_PAIRS)
    reps = LANES // HEAD_DIM
    out = []
    for n_pos, on_row in ((seq_len // GRID_W, True), (GRID_W, False)):
        ang = jnp.arange(n_pos, dtype=F32)[:, None] * inv_freq
        zero = jnp.zeros_like(ang)
        for fn, sign in ((jnp.cos, 1.0), (jnp.sin, -1.0)):
            v = fn(ang)
            head = ([sign * v, v, zero, zero] if on_row else [zero, zero, sign * v, v])
            out.append(jnp.concatenate(head * reps, axis=1))
    return out


def kernel(x, c, ctx, c_ctx, w_ada, b_ada, norm1_g, norm2_g, w_in, b_in, attn_sink, w_pool,
           pool_scale, w_attn_br, w_pool_br, w_out, w_router, b_router, w_gu, b_gu, w_down,
           b_down, final_g):
    b, l, d = x.shape
    n = b * l
    assert w_ada.shape[0] == 1, "single-layer block"

    cond = jnp.zeros((SUBLANES, d), F32).at[:b].set(c).at[b].set(c_ctx)
    mod = _adaln(cond, w_ada[0], b_ada[0])[:b + 1].reshape(b + 1, 6, d)
    mod_x, mod_c = mod[:b], mod[b:b + 1]

    rope = _rope_tables(l)
    v0 = ATTN_W + KV_W
    w_v = w_in[0][:, v0:v0 + KV_W].reshape(d, N_KV_HEADS, HEAD_DIM)
    w_v = jnp.concatenate([w_v, jnp.zeros_like(w_v)], axis=-1).reshape(d, VEXT_W)
    b_v = b_in[0][v0:v0 + KV_W].reshape(N_KV_HEADS, HEAD_DIM)
    b_v = jnp.concatenate([b_v, jnp.ones_like(b_v)], axis=-1).reshape(VEXT_W)
    w_in_bf = jnp.concatenate([w_in[0][:, :v0], w_v, w_in[0][:, v0 + KV_W:]], axis=1).astype(BF16)
    b_in2 = jnp.concatenate([b_in[0][:v0], b_v, b_in[0][v0 + KV_W:]]).reshape(1, IN_W_EXT)
    g1 = norm1_g[0].reshape(1, d)
    q, kt, v, u, ga, gp = _inproj(x, mod_x, g1, w_in_bf, b_in2, rope)
    kv_sl = slice(ATTN_W, ATTN_W + KV_W + VEXT_W)
    kxt, vx = _ctx_kv(ctx, mod_c, g1, w_in_bf[:, kv_sl], b_in2[:, kv_sl])

    attn_o = _attention(q, kt, v, kxt, vx, attn_sink[0])

    x1, h2, top_idx, gate_w, rank, counts = _merge_route(
        attn_o, u, ga, gp, x, mod_x, w_pool[0].astype(BF16), pool_scale[0].reshape(1, d),
        w_attn_br[0].astype(BF16), w_pool_br[0].astype(BF16), w_out[0].astype(BF16),
        norm2_g[0].reshape(1, d), w_router[0].astype(BF16), b_router[0].reshape(1, N_EXPERTS))

    tm = TM_EXPERT
    nb = n * TOP_K // tm + N_EXPERTS
    cnt = counts[:, 0].astype(jnp.int32)
    padded = (cnt + tm - 1) // tm * tm
    pend = jnp.cumsum(padded)
    pstart = pend - padded
    expert_ids = jnp.arange(N_EXPERTS, dtype=jnp.int32)
    dest = jnp.sum(jnp.where(top_idx[..., None] == expert_ids, pstart, 0), axis=-1) + rank
    n_used = (pend[-1] // tm).reshape(1)
    block_start = jnp.arange(nb, dtype=jnp.int32) * tm
    block_e = jnp.minimum(jnp.sum((pend[None, :] <= block_start[:, None]).astype(jnp.int32), axis=1),
                          N_EXPERTS - 1)
    first = jnp.concatenate([jnp.ones((1,), jnp.int32),
                             (block_e[1:] != block_e[:-1]).astype(jnp.int32)])
    later_used = (expert_ids[None, :] > expert_ids[:, None]) & (cnt[None, :] > 0)
    next_of_expert = jnp.min(jnp.where(later_used, expert_ids[None, :], N_EXPERTS), axis=1)
    next_of_expert = jnp.where(next_of_expert < N_EXPERTS, next_of_expert, -1)
    next_e = jnp.sum(jnp.where(block_e[:, None] == expert_ids, next_of_expert, 0), axis=-1)

    qm = EXPERT_ROW_QUANTUM
    pad_j = jnp.arange(qm, dtype=jnp.int32)[None, :]
    n_pad = (cnt + qm - 1) // qm * qm - cnt
    spare = nb * tm + expert_ids[:, None] * qm + pad_j
    pad_dest = jnp.where(pad_j < n_pad[:, None], (pstart + cnt)[:, None] + pad_j, spare)
    pad_dest = pad_dest.reshape(SC_WORKERS, N_EXPERTS * qm // (SC_WORKERS * SC_WINDOW), SC_WINDOW)

    xs = _sc_scatter_rows(h2, dest, pad_dest, nb * tm + N_EXPERTS * qm)
    used_end = jnp.sum(jnp.where(block_e[:, None] == expert_ids, pstart + cnt, 0), axis=-1)
    valid = jnp.clip(used_end - block_start, 0, tm)
    ys = _experts(block_e, first, next_e, valid, n_used, xs, w_gu[0], b_gu[0], w_down[0], b_down[0])
    y4 = _sc_gather_rows(ys, dest.reshape(-1)).reshape(TOP_K, n, ROW_WORDS)
    return _combine_dense(y4, gate_w, x1, mod_x, final_g.reshape(1, d))
```

```python
import functools

import jax
import jax.numpy as jnp
from jax import lax
from jax.experimental import pallas as pl
from jax.experimental.pallas import tpu as pltpu
from jax.experimental.pallas import tpu_sc as plsc

D_MODEL = 1024
GRID_W = 64
HEAD_DIM = 64
N_HEADS = 16
N_KV_HEADS = 4
GROUP = N_HEADS // N_KV_HEADS
ATTN_W = N_HEADS * HEAD_DIM
KV_W = N_KV_HEADS * HEAD_DIM
WINDOW = 128
ATTN_SCALE = HEAD_DIM ** -0.5
ROPE_BASE = 10000.0
ROPE_PAIRS = HEAD_DIM // 4
POOL_WINDOWS = (2, 4, 8, 16)
POOL_GROUP_W = D_MODEL // len(POOL_WINDOWS)
IN_W = ATTN_W + 2 * KV_W + D_MODEL + 2 * D_MODEL
VEXT_W = N_KV_HEADS * 2 * HEAD_DIM
IN_W_EXT = IN_W - KV_W + VEXT_W
N_EXPERTS = 32
TOP_K = 4
D_FF = D_MODEL
SWIGLU_ALPHA = 1.702
SWIGLU_LIMIT = 7.0
NORM_EPS = 1e-5
NEG_INF = -1e30

LANES = 128
SUBLANES = 8
VMEM_LIMIT = 56 * 1024 * 1024

TM_INPROJ = 1024
INPROJ_SUB = 512
TQ = WINDOW
ATTN_Q_BLOCKS = 4
TM_MERGE = 1024
MERGE_SUB = 256
TM_EXPERT = 512
EXPERT_ROW_QUANTUM = 128
EXPERT_BLOCKS_PER_STEP = 1
POOL_HALO = SUBLANES

F32 = jnp.float32
BF16 = jnp.bfloat16


ROW_WORDS = D_MODEL // 2
U32 = jnp.uint32


def _pack_rows(value):
    bits = lax.bitcast_convert_type(value.astype(BF16).astype(F32), U32)
    return bits[:, :ROW_WORDS] | (bits[:, ROW_WORDS:] >> 16)


def _unpack_rows(words):
    hi = lax.bitcast_convert_type(words & jnp.uint32(0xFFFF0000), F32)
    lo = lax.bitcast_convert_type(words << 16, F32)
    return jnp.concatenate([hi, lo], axis=1)


def _params(*sem):
    return pltpu.CompilerParams(dimension_semantics=sem, vmem_limit_bytes=VMEM_LIMIT)


def _const_spec(shape):
    nd = len(shape)
    return pl.BlockSpec(shape, lambda *_: (0,) * nd, pipeline_mode=pl.Buffered(1))


def _adaln_kernel(c_ref, w_ref, b_ref, o_ref):
    c = c_ref[...]
    s = c * jax.nn.sigmoid(c)
    o_ref[...] = jnp.dot(s.astype(BF16), w_ref[...].astype(BF16),
                         preferred_element_type=F32) + b_ref[...]


def _adaln(cond, w_ada, b_ada):
    rows, d = cond.shape
    n = w_ada.shape[1]
    tn = 1024
    return pl.pallas_call(
        _adaln_kernel,
        grid=(n // tn,),
        in_specs=[_const_spec((rows, d)),
                  pl.BlockSpec((d, tn), lambda j: (0, j)),
                  pl.BlockSpec((1, tn), lambda j: (0, j))],
        out_specs=pl.BlockSpec((rows, tn), lambda j: (0, j)),
        out_shape=jax.ShapeDtypeStruct((rows, n), F32),
        compiler_params=_params("arbitrary"),
        name="adaln",
    )(cond, w_ada, b_ada.reshape(1, n))


def _norm_mod(x, g, shift, scale):
    y = x * lax.rsqrt(jnp.mean(x * x, axis=-1, keepdims=True) + NORM_EPS) * g
    return y * (1.0 + scale) + shift


def _rope(t, cos, sin_signed):
    w = t.shape[1]
    half = ROPE_PAIRS
    lane = lax.broadcasted_iota(jnp.int32, t.shape, 1)
    first = (lane & (2 * half - 1)) < half
    rot = jnp.where(first, pltpu.roll(t, w - half, 1), pltpu.roll(t, half, 1))
    reps = w // LANES
    cos_w = jnp.concatenate([cos] * reps, axis=1) if reps > 1 else cos
    sin_w = jnp.concatenate([sin_signed] * reps, axis=1) if reps > 1 else sin_signed
    return t * cos_w + rot * sin_w


def _token_table(row_ref, col_ref):
    gr = row_ref.shape[0]
    shape = (gr, GRID_W, LANES)
    full = (jnp.broadcast_to(row_ref[...][:, None, :], shape)
            + jnp.broadcast_to(col_ref[...][None, :, :], shape))
    return full.reshape(gr * GRID_W, LANES)


def _inproj_kernel(x_ref, mod_ref, g_ref, w_ref, b_ref, cr_ref, sr_ref, cc_ref, sc_ref,
                   q_ref, k_ref, v_ref, u_ref, ga_ref, gp_ref):
    sub = INPROJ_SUB
    cos_all = _token_table(cr_ref, cc_ref)
    sin_all = _token_table(sr_ref, sc_ref)
    cw = 512

    def work_items(r0):
        rs = slice(r0, r0 + sub)
        cos, sin = cos_all[rs], sin_all[rs]
        hb = _norm_mod(x_ref[0, rs, :], g_ref[...], mod_ref[0, 0:1, :], mod_ref[0, 1:2, :]).astype(BF16)

        def put_k(t):
            k_ref[0, :, rs] = _rope(t, cos, sin).T.astype(BF16)

        def put_v(t):
            v_ref[0, rs, :] = t.astype(BF16)

        items = []
        for c0 in range(0, ATTN_W, cw):
            def put_q(t, c0=c0):
                q_ref[0, rs, c0:c0 + cw] = (_rope(t, cos, sin) * ATTN_SCALE).astype(BF16)
            items.append((c0, cw, put_q))
        items.append((ATTN_W, KV_W, put_k))
        items.append((ATTN_W + KV_W, VEXT_W, put_v))
        base = ATTN_W + KV_W + VEXT_W
        for c0 in range(0, D_MODEL, cw):
            def put_u(t, c0=c0):
                u_ref[0, rs, c0:c0 + cw] = t
            items.append((base + c0, cw, put_u))
        for ref in (ga_ref, gp_ref):
            base += D_MODEL
            for c0 in range(0, D_MODEL, cw):
                def put_gate(t, ref=ref, c0=c0):
                    ref[0, rs, c0:c0 + cw] = jax.nn.sigmoid(t).astype(BF16)
                items.append((base + c0, cw, put_gate))
        return [(hb, c0, w, fn) for c0, w, fn in items]

    def proj(hb, c0, w):
        return jnp.dot(hb, w_ref[:, c0:c0 + w], preferred_element_type=F32) + b_ref[:, c0:c0 + w]

    items = [it for r0 in range(0, x_ref.shape[1], sub) for it in work_items(r0)]
    pending = proj(*items[0][:3])
    for ci, item in enumerate(items):
        ready = pending
        if ci + 1 < len(items):
            pending = proj(*items[ci + 1][:3])
        item[3](ready)


def _inproj(x, mod, norm_g, w_in_bf, b_in, rope):
    b, l, d = x.shape
    tm = TM_INPROJ
    row = lambda bi, i: (bi, i, 0)
    out_shape = (
        jax.ShapeDtypeStruct((b, l, ATTN_W), BF16),
        jax.ShapeDtypeStruct((b, KV_W, l), BF16),
        jax.ShapeDtypeStruct((b, l, VEXT_W), BF16),
        jax.ShapeDtypeStruct((b, l, D_MODEL), F32),
        jax.ShapeDtypeStruct((b, l, D_MODEL), BF16),
        jax.ShapeDtypeStruct((b, l, D_MODEL), BF16),
    )
    return pl.pallas_call(
        _inproj_kernel,
        grid=(b, l // tm),
        in_specs=[pl.BlockSpec((1, tm, d), row),
                  pl.BlockSpec((1, 6, d), lambda bi, i: (bi, 0, 0)),
                  _const_spec((1, d)),
                  _const_spec((d, IN_W_EXT)),
                  _const_spec((1, IN_W_EXT)),
                  pl.BlockSpec((tm // GRID_W, LANES), lambda bi, i: (i, 0)),
                  pl.BlockSpec((tm // GRID_W, LANES), lambda bi, i: (i, 0)),
                  _const_spec((GRID_W, LANES)),
                  _const_spec((GRID_W, LANES))],
        out_specs=(pl.BlockSpec((1, tm, ATTN_W), row),
                   pl.BlockSpec((1, KV_W, tm), lambda bi, i: (bi, 0, i)),
                   pl.BlockSpec((1, tm, VEXT_W), row),
                   pl.BlockSpec((1, tm, D_MODEL), row),
                   pl.BlockSpec((1, tm, D_MODEL), row),
                   pl.BlockSpec((1, tm, D_MODEL), row)),
        out_shape=out_shape,
        compiler_params=_params("arbitrary", "arbitrary"),
        name="inproj",
    )(x, mod, norm_g, w_in_bf, b_in, *rope)


def _ctx_kv_kernel(x_ref, mod_ref, g_ref, w_ref, b_ref, k_ref, v_ref):
    h = _norm_mod(x_ref[0], g_ref[...], mod_ref[0, 0:1, :], mod_ref[0, 1:2, :])
    kv = jnp.dot(h.astype(BF16), w_ref[...], preferred_element_type=F32) + b_ref[...]
    k_ref[0] = kv[:, :KV_W].T.astype(BF16)
    v_ref[0] = kv[:, KV_W:].astype(BF16)


def _ctx_kv(ctx, mod_c, norm_g, w_kv_bf, b_kv):
    b, lc, d = ctx.shape
    row = lambda bi: (bi, 0, 0)
    return pl.pallas_call(
        _ctx_kv_kernel,
        grid=(b,),
        in_specs=[pl.BlockSpec((1, lc, d), row),
                  _const_spec((1, 6, d)),
                  _const_spec((1, d)),
                  _const_spec((d, KV_W + VEXT_W)),
                  _const_spec((1, KV_W + VEXT_W))],
        out_specs=(pl.BlockSpec((1, KV_W, lc), row), pl.BlockSpec((1, lc, VEXT_W), row)),
        out_shape=(jax.ShapeDtypeStruct((b, KV_W, lc), BF16),
                   jax.ShapeDtypeStruct((b, lc, VEXT_W), BF16)),
        compiler_params=_params("arbitrary"),
        name="ctx_kv",
    )(ctx, mod_c, norm_g, w_kv_bf, b_kv)


def _attn_kernel(sink_ref, q_ref, *refs):
    nq = ATTN_Q_BLOCKS
    k_refs, v_refs = refs[:nq + 2], refs[nq + 2:2 * nq + 4]
    kx_ref, vx_ref, o_ref = refs[2 * nq + 4:]
    first_blk = pl.program_id(1) * nq
    last_blk = pl.num_programs(1) * nq - 1
    rows = GROUP * TQ
    pair_w = 2 * HEAD_DIM
    qi = lax.broadcasted_iota(jnp.int32, (rows, TQ), 0) & (TQ - 1)
    col = lax.broadcasted_iota(jnp.int32, (rows, TQ), 1)
    low_half = lax.broadcasted_iota(jnp.int32, (TQ, pair_w), 1) < HEAD_DIM
    row_id = lax.broadcasted_iota(jnp.int32, (rows, 1), 0)

    def scores(b, j):
        qrows = slice(b * TQ, (b + 1) * TQ)
        parts = []
        for g in range(GROUP):
            h = j * GROUP + g
            pair = q_ref[0, qrows, (h // 2) * pair_w:(h // 2 + 1) * pair_w]
            parts.append(jnp.where(low_half if h % 2 == 0 else ~low_half, pair, jnp.zeros_like(pair)))
        q4 = jnp.concatenate(parts, axis=0)
        ks = slice(j * HEAD_DIM, (j + 1) * HEAD_DIM)
        kt = jnp.concatenate([k_refs[b + d][0, ks, :] for d in range(3)], axis=1)
        s_loc = jnp.dot(q4, jnp.concatenate([kt, kt], axis=0), preferred_element_type=F32)
        kxt = kx_ref[0, ks, :]
        s_ctx = jnp.dot(q4, jnp.concatenate([kxt, kxt], axis=0), preferred_element_type=F32)
        return s_loc, s_ctx

    def probs(b, j, s_loc, s_ctx):
        keep_prev = (col >= qi) & (first_blk + b > 0)
        keep_next = (col <= qi) & (first_blk + b < last_blk)
        pieces = [jnp.where(keep_prev, s_loc[:, :TQ], NEG_INF), s_loc[:, TQ:2 * TQ],
                  jnp.where(keep_next, s_loc[:, 2 * TQ:], NEG_INF)]
        pieces += [s_ctx[:, c0:c0 + TQ] for c0 in range(0, s_ctx.shape[1], TQ)]
        sink = jnp.zeros((rows, 1), F32)
        for g in range(GROUP):
            sink = jnp.where(row_id // TQ == g, sink_ref[j * GROUP + g], sink)
        mx = pieces[0]
        for pc in pieces[1:]:
            mx = jnp.maximum(mx, pc)
        m = jnp.maximum(jnp.max(mx, axis=-1, keepdims=True), sink)
        p = jnp.concatenate([jnp.exp(pc - m).astype(BF16) for pc in pieces], axis=1)
        return p, jnp.exp(sink - m)

    def output(b, j, p, sink_p):
        vs = slice(j * pair_w, (j + 1) * pair_w)
        v_all = jnp.concatenate([v_refs[b + d][0, :, vs] for d in range(3)]
                                + [vx_ref[0, :, vs]], axis=0)
        r = jnp.dot(p, v_all, preferred_element_type=F32)
        den = pltpu.roll(r, HEAD_DIM, 1) + sink_p
        o = r / den
        for g in range(0, GROUP, 2):
            even = o[g * TQ:(g + 1) * TQ]
            odd = pltpu.roll(o[(g + 1) * TQ:(g + 2) * TQ], HEAD_DIM, 1)
            c0 = (j * GROUP + g) * HEAD_DIM
            o_ref[0, b * TQ:(b + 1) * TQ, c0:c0 + pair_w] = jnp.where(low_half, even, odd).astype(BF16)

    items = [(b, j) for b in range(nq) for j in range(N_KV_HEADS)]
    s_queue = [scores(*items[0]), scores(*items[1])]
    p_queue = [probs(*items[0], *s_queue.pop(0))]
    for i, item in enumerate(items):
        if i + 2 < len(items):
            s_queue.append(scores(*items[i + 2]))
        ready = p_queue.pop(0)
        if i + 1 < len(items):
            p_queue.append(probs(*items[i + 1], *s_queue.pop(0)))
        output(*item, *ready)


def _attention(q, kt, v, kxt, vx, sink):
    b, l, _ = q.shape
    lc = kxt.shape[2]
    nq = ATTN_Q_BLOCKS
    nblk = l // TQ
    assert nblk % nq == 0

    def blk(n, d):
        return jnp.clip(n * nq + d - 1, 0, nblk - 1)

    k_specs = [pl.BlockSpec((1, KV_W, TQ), lambda bi, n, d=d: (bi, 0, blk(n, d))) for d in range(nq + 2)]
    v_specs = [pl.BlockSpec((1, TQ, VEXT_W), lambda bi, n, d=d: (bi, blk(n, d), 0)) for d in range(nq + 2)]
    return pl.pallas_call(
        _attn_kernel,
        grid=(b, nblk // nq),
        in_specs=[pl.BlockSpec(memory_space=pltpu.SMEM),
                  pl.BlockSpec((1, nq * TQ, ATTN_W), lambda bi, n: (bi, n, 0))]
                 + k_specs + v_specs
                 + [pl.BlockSpec((1, KV_W, lc), lambda bi, n: (bi, 0, 0)),
                    pl.BlockSpec((1, lc, VEXT_W), lambda bi, n: (bi, 0, 0))],
        out_specs=pl.BlockSpec((1, nq * TQ, ATTN_W), lambda bi, n: (bi, n, 0)),
        out_shape=jax.ShapeDtypeStruct((b, l, ATTN_W), BF16),
        compiler_params=_params("arbitrary", "arbitrary"),
        name="attention",
    )(sink, q, *([kt] * (nq + 2)), *([v] * (nq + 2)), kxt, vx)


def _shift_rows(a, d):
    n = a.shape[0]
    return pltpu.roll(a, (-d) % n, 0)


def _merge_kernel(ao_ref, u_ref, up_ref, un_ref, sa_ref, sp_ref, x_ref, mod_ref,
                  wpool_ref, pscale_ref, wab_ref, wpb_ref, wout_ref, g2_ref, wr_ref, br_ref,
                  x1_ref, h2_ref, idx_ref, gate_ref, rank_ref, cnt_ref, carry_ref, *, seq_len):
    bi = pl.program_id(0)
    i = pl.program_id(1)
    last = pl.num_programs(1) - 1
    tm, sub = TM_MERGE, MERGE_SUB

    @pl.when((bi == 0) & (i == 0))
    def _():
        carry_ref[...] = jnp.zeros_like(carry_ref)

    u = u_ref[0]
    prev = jnp.where(i > 0, up_ref[0], 0.0)
    nxt = jnp.where(i < last, un_ref[0], 0.0)
    ext = jnp.concatenate([prev, u, nxt], axis=0)

    def pool_diff(r0):
        t = i * tm + r0 + lax.broadcasted_iota(jnp.int32, (sub, 1), 0)
        diffs = []
        for g, w in enumerate(POOL_WINDOWS):
            cs = slice(g * POOL_GROUP_W, (g + 1) * POOL_GROUP_W)
            e = ext[r0:r0 + sub + 2 * POOL_HALO, cs]
            acc = _shift_rows(e, -1) + e
            step = 1
            while 2 * step < w:
                acc = _shift_rows(acc, -step) + _shift_rows(acc, step)
                step *= 2
            win = acc[POOL_HALO:POOL_HALO + sub]
            half = w // 2
            cnt = (jnp.minimum(t + half, seq_len) - jnp.maximum(t - half, 0)).astype(F32)
            diffs.append((win / cnt - u[r0:r0 + sub, cs]).astype(BF16))
        return diffs

    def branches(r0, diffs):
        rs = slice(r0, r0 + sub)
        a = jnp.dot(ao_ref[0, rs, :], wab_ref[...], preferred_element_type=F32)
        mixed = [jnp.dot(df, wpool_ref[g], preferred_element_type=F32) for g, df in enumerate(diffs)]
        pool_o = jnp.concatenate(mixed, axis=1) * pscale_ref[...]
        p = jnp.dot(pool_o.astype(BF16), wpb_ref[...], preferred_element_type=F32)
        merged = sa_ref[0, rs, :].astype(F32) * a + sp_ref[0, rs, :].astype(F32) * p
        return merged.astype(BF16)

    def out_proj(merged):
        return jnp.dot(merged, wout_ref[...], preferred_element_type=F32)

    def finish(r0, o):
        rs = slice(r0, r0 + sub)
        x1 = x_ref[0, rs, :] + mod_ref[0, 2:3, :] * o
        x1_ref[0, rs, :] = x1
        h2 = _norm_mod(x1, g2_ref[...], mod_ref[0, 3:4, :], mod_ref[0, 4:5, :])
        h2_ref[rs, :] = _pack_rows(h2)
        return jnp.dot(h2.astype(BF16), wr_ref[...], preferred_element_type=F32) + br_ref[...]

    def route(r0, logits, carry):
        lt = logits.T
        eid = lax.broadcasted_iota(jnp.int32, (N_EXPERTS, sub), 0).astype(F32)
        work = lt
        vals, idxs, sels = [], [], []
        for _ in range(TOP_K):
            mx = jnp.max(work, axis=0, keepdims=True)
            ix = jnp.min(jnp.where(work == mx, eid, float(N_EXPERTS)), axis=0, keepdims=True)
            sel = eid == ix
            work = jnp.where(sel, -jnp.inf, work)
            vals.append(mx)
            idxs.append(ix)
            sels.append(sel)
        ex = [jnp.exp(v - vals[0]) for v in vals]
        tot = ex[0] + ex[1] + ex[2] + ex[3]
        pad = [jnp.zeros_like(tot)] * (SUBLANES - TOP_K)
        gates = jnp.concatenate([e_ / tot for e_ in ex] + pad, axis=0)
        gate_ref[r0:r0 + sub, :] = gates.T[:, :TOP_K]
        idx_ref[:, r0:r0 + sub] = jnp.concatenate(idxs, axis=0).astype(jnp.int32)
        member = (sels[0] | sels[1] | sels[2] | sels[3]).astype(F32)
        t_row = lax.broadcasted_iota(jnp.int32, (sub, sub), 0)
        t_col = lax.broadcasted_iota(jnp.int32, (sub, sub), 1)
        earlier = (t_row < t_col).astype(BF16)
        before = jnp.dot(member.astype(BF16), earlier, preferred_element_type=F32) + carry
        ranks = [jnp.sum(jnp.where(s, before, 0.0), axis=0, keepdims=True) for s in sels]
        rank_ref[:, r0:r0 + sub] = jnp.concatenate(ranks, axis=0).astype(jnp.int32)
        return carry + jnp.sum(member, axis=1, keepdims=True)

    starts = list(range(0, tm, sub))
    carry = carry_ref[...]
    merged = branches(starts[0], pool_diff(starts[0]))
    for si, r0 in enumerate(starts):
        o = out_proj(merged)
        if si + 1 < len(starts):
            merged = branches(starts[si + 1], pool_diff(starts[si + 1]))
        carry = route(r0, finish(r0, o), carry)
    carry_ref[...] = carry
    cnt_ref[...] = carry


def _merge_route(batch, attn_o, u, ga, gp, x, mod, w_pool_bf, pool_scale, wab, wpb, wout,
                 norm2_g, w_router, b_router):
    _, l, d = x.shape
    b = 1
    tm = TM_MERGE
    nt = l // tm
    n = b * l
    hb = tm // POOL_HALO
    row = lambda bi, i: (bi, i, 0)
    in_row = lambda bi, i: (batch + bi, i, 0)
    flat = lambda bi, i: (bi * nt + i, 0)
    halo_prev = lambda bi, i: (batch + bi, jnp.maximum(i * hb - 1, 0), 0)
    halo_next = lambda bi, i: (batch + bi, jnp.minimum((i + 1) * hb, l // POOL_HALO - 1), 0)
    slot_major = lambda bi, i: (0, bi * nt + i)
    out_shape = (
        jax.ShapeDtypeStruct((b, l, d), F32),
        jax.ShapeDtypeStruct((n, ROW_WORDS), U32),
        jax.ShapeDtypeStruct((TOP_K, n), jnp.int32),
        jax.ShapeDtypeStruct((n, TOP_K), F32),
        jax.ShapeDtypeStruct((TOP_K, n), jnp.int32),
        jax.ShapeDtypeStruct((N_EXPERTS, 1), F32),
    )
    return pl.pallas_call(
        functools.partial(_merge_kernel, seq_len=l),
        grid=(b, nt),
        in_specs=[pl.BlockSpec((1, tm, d), in_row),
                  pl.BlockSpec((1, tm, d), in_row),
                  pl.BlockSpec((1, POOL_HALO, d), halo_prev),
                  pl.BlockSpec((1, POOL_HALO, d), halo_next),
                  pl.BlockSpec((1, tm, d), in_row),
                  pl.BlockSpec((1, tm, d), in_row),
                  pl.BlockSpec((1, tm, d), in_row),
                  pl.BlockSpec((1, 6, d), lambda bi, i: (batch + bi, 0, 0)),
                  _const_spec(w_pool_bf.shape),
                  _const_spec((1, d)),
                  _const_spec((d, d)), _const_spec((d, d)), _const_spec((d, d)),
                  _const_spec((1, d)),
                  _const_spec((d, N_EXPERTS)),
                  _const_spec((1, N_EXPERTS))],
        out_specs=(pl.BlockSpec((1, tm, d), row),
                   pl.BlockSpec((tm, ROW_WORDS), flat),
                   pl.BlockSpec((TOP_K, tm), slot_major),
                   pl.BlockSpec((tm, TOP_K), flat),
                   pl.BlockSpec((TOP_K, tm), slot_major),
                   _const_spec((N_EXPERTS, 1))),
        out_shape=out_shape,
        scratch_shapes=[pltpu.VMEM((N_EXPERTS, 1), F32)],
        compiler_params=_params("arbitrary", "arbitrary"),
        name="merge_route",
    )(attn_o, u, u, u, ga, gp, x, mod, w_pool_bf, pool_scale, wab, wpb, wout,
      norm2_g, w_router, b_router)


SC_CORES = 2
SC_SUBCORES = 16
SC_WORKERS = SC_CORES * SC_SUBCORES
SC_WINDOW = 128
SC_GATHER_WINDOW = 64


def _sc_worker_id():
    return lax.axis_index("s") * SC_CORES + lax.axis_index("c")


def _sc_scatter_rows(rows, dest, pad_dest, out_rows):
    n, w = rows.shape
    n_win = n // (SC_WORKERS * SC_WINDOW)
    n_pad_win = pad_dest.shape[1]
    assert dest.shape == (TOP_K, n) and SC_WORKERS * n_win * SC_WINDOW == n
    mesh = plsc.VectorSubcoreMesh(core_axis_name="c", subcore_axis_name="s")
    zeros = jnp.zeros((SC_WINDOW, w), rows.dtype)

    @functools.partial(
        pl.kernel, mesh=mesh,
        out_type=jax.ShapeDtypeStruct((out_rows, w), rows.dtype),
        scratch_types=[pltpu.VMEM((TOP_K, SC_WINDOW), jnp.int32),
                       pltpu.VMEM((n_pad_win, SC_WINDOW), jnp.int32),
                       pltpu.VMEM((SC_WINDOW, w), rows.dtype),
                       pltpu.SemaphoreType.DMA],
        name="sc_scatter_rows",
    )
    def scatter(rows_hbm, dest_hbm, pad_hbm, zeros_hbm, out_hbm, idx_v, pad_v, rows_v, sem):
        wid = _sc_worker_id()

        @pl.loop(0, n_win)
        def _(j):
            base = (wid * n_win + j) * SC_WINDOW
            pltpu.sync_copy(rows_hbm.at[pl.ds(base, SC_WINDOW)], rows_v)
            pltpu.sync_copy(dest_hbm.at[:, pl.ds(base, SC_WINDOW)], idx_v)
            copies = [pltpu.async_copy(rows_v, out_hbm.at[idx_v.at[k]], sem) for k in range(TOP_K)]
            for cp in copies:
                cp.wait()

        pltpu.sync_copy(zeros_hbm, rows_v)
        pltpu.sync_copy(pad_hbm.at[wid], pad_v)
        copies = [pltpu.async_copy(rows_v, out_hbm.at[pad_v.at[j]], sem) for j in range(n_pad_win)]
        for cp in copies:
            cp.wait()

    return scatter(rows, dest, pad_dest, zeros)


def _expert_kernel(be_ref, first_ref, next_ref, valid_ref, nu_ref, x_ref, wgu_hbm, bgu_ref, wd_hbm, bd_ref,
                   y_ref, wgu_f, wd_f, wgu_s, wd_s, sem):
    step = pl.program_id(0)

    def weight_copies(e):
        return (pltpu.make_async_copy(wgu_hbm.at[e], wgu_f, sem.at[0]),
                pltpu.make_async_copy(wd_hbm.at[e], wd_f, sem.at[1]))

    def start_weights(e):
        for cp in weight_copies(e):
            cp.start(priority=1)

    @pl.when(step == 0)
    def _():
        start_weights(be_ref[0])

    for sub in range(EXPERT_BLOCKS_PER_STEP):
        _expert_block(step * EXPERT_BLOCKS_PER_STEP + sub,
                      slice(sub * TM_EXPERT, (sub + 1) * TM_EXPERT),
                      be_ref, first_ref, next_ref, valid_ref, nu_ref, x_ref, bgu_ref, bd_ref, y_ref,
                      wgu_f, wd_f, wgu_s, wd_s, weight_copies, start_weights)


def _expert_block(i, rows, be_ref, first_ref, next_ref, valid_ref, nu_ref, x_ref, bgu_ref, bd_ref, y_ref,
                  wgu_f, wd_f, wgu_s, wd_s, weight_copies, start_weights):
    @pl.when(i < nu_ref[0])
    def _():
        e = be_ref[i]

        @pl.when(first_ref[i] == 1)
        def _():
            for cp in weight_copies(e):
                cp.wait()
            ck = 16

            def cast_rows(c, carry):
                r = pl.ds(pl.multiple_of(c * ck, ck), ck)
                wgu_s[r, :] = wgu_f[r, :].astype(BF16)
                wd_s[r, :] = wd_f[r, :].astype(BF16)
                return carry

            lax.fori_loop(0, D_MODEL // ck, cast_rows, 0, unroll=2)

            @pl.when(next_ref[i] >= 0)
            def _():
                start_weights(next_ref[i])

        def ffn(m):
            r = slice(rows.start, rows.start + m)
            x = _unpack_rows(x_ref[r, :]).astype(BF16)
            bgu = bgu_ref[e]
            gt = jnp.dot(x, wgu_s[:, :D_FF], preferred_element_type=F32) + bgu[:, :D_FF]
            ln = jnp.dot(x, wgu_s[:, D_FF:], preferred_element_type=F32) + bgu[:, D_FF:]
            gt = jnp.minimum(gt, SWIGLU_LIMIT)
            ln = jnp.clip(ln, -SWIGLU_LIMIT, SWIGLU_LIMIT)
            act = gt * jax.nn.sigmoid(SWIGLU_ALPHA * gt) * (ln + 1.0)
            y = jnp.dot(act.astype(BF16), wd_s[...], preferred_element_type=F32) + bd_ref[e]
            y_ref[r, :] = _pack_rows(y)
            if m < TM_EXPERT:
                y_ref[rows.start + m:rows.stop, :] = jnp.zeros((TM_EXPERT - m, ROW_WORDS), U32)

        quanta = (valid_ref[i] + EXPERT_ROW_QUANTUM - 1) // EXPERT_ROW_QUANTUM
        for q in range(1, TM_EXPERT // EXPERT_ROW_QUANTUM + 1):
            pl.when(quanta == q)(functools.partial(ffn, q * EXPERT_ROW_QUANTUM))

    @pl.when(i >= nu_ref[0])
    def _():
        y_ref[rows, :] = jnp.zeros((TM_EXPERT, ROW_WORDS), U32)


def _experts(block_e, first, next_e, valid, n_used, xs, w_gu, b_gu, w_down, b_down):
    d = D_MODEL
    nb = block_e.shape[0]
    per_step = EXPERT_BLOCKS_PER_STEP
    tm = TM_EXPERT * per_step
    assert nb % per_step == 0

    xmap = lambda i, be, fi, nx, va, nu: (jnp.minimum(i, (nu[0] - 1) // per_step), 0)
    grid_spec = pltpu.PrefetchScalarGridSpec(
        num_scalar_prefetch=5,
        grid=(nb // per_step,),
        in_specs=[pl.BlockSpec((tm, ROW_WORDS), xmap),
                  pl.BlockSpec(memory_space=pl.ANY),
                  _const_spec((N_EXPERTS, 1, 2 * D_FF)),
                  pl.BlockSpec(memory_space=pl.ANY),
                  _const_spec((N_EXPERTS, 1, d))],
        out_specs=pl.BlockSpec((tm, ROW_WORDS), lambda i, be, fi, nx, va, nu: (i, 0)),
        scratch_shapes=[pltpu.VMEM((d, 2 * D_FF), F32), pltpu.VMEM((D_FF, d), F32),
                        pltpu.VMEM((d, 2 * D_FF), BF16), pltpu.VMEM((D_FF, d), BF16),
                        pltpu.SemaphoreType.DMA((2,))],
    )
    return pl.pallas_call(
        _expert_kernel,
        grid_spec=grid_spec,
        out_shape=jax.ShapeDtypeStruct((nb * TM_EXPERT, ROW_WORDS), U32),
        compiler_params=_params("arbitrary"),
        name="moe_experts",
    )(block_e, first, next_e, valid, n_used, xs, w_gu, b_gu.reshape(N_EXPERTS, 1, 2 * D_FF),
      w_down, b_down.reshape(N_EXPERTS, 1, d))


def _sc_gather_rows(table, idx):
    m = idx.shape[0]
    w = table.shape[1]
    win = SC_GATHER_WINDOW
    per_worker = m // SC_WORKERS
    n_win = per_worker // win
    assert per_worker * SC_WORKERS == m and n_win * win == per_worker and n_win % 2 == 0
    mesh = plsc.VectorSubcoreMesh(core_axis_name="c", subcore_axis_name="s")

    @functools.partial(
        pl.kernel, mesh=mesh,
        out_type=jax.ShapeDtypeStruct((m, w), table.dtype),
        scratch_types=[pltpu.VMEM((win,), jnp.int32), pltpu.VMEM((win,), jnp.int32),
                       pltpu.VMEM((win, w), table.dtype), pltpu.VMEM((win, w), table.dtype),
                       pltpu.SemaphoreType.DMA((4,))],
        name="sc_gather_rows",
    )
    def gather(table_hbm, idx_hbm, out_hbm, idx0, idx1, rows0, rows1, sems):
        wid = _sc_worker_id()
        idxs, rows = (idx0, idx1), (rows0, rows1)

        def out_window(j):
            return out_hbm.at[pl.ds(wid * per_worker + j * win, win)]

        def start_gather(j, slot):
            pltpu.sync_copy(idx_hbm.at[pl.ds(wid * per_worker + j * win, win)], idxs[slot])
            pltpu.async_copy(table_hbm.at[idxs[slot]], rows[slot], sems.at[slot])

        def wait_gather(slot):
            pltpu.make_async_copy(table_hbm.at[idxs[slot]], rows[slot], sems.at[slot]).wait()

        def wait_write(j, slot):
            pltpu.make_async_copy(rows[slot], out_window(j), sems.at[2 + slot]).wait()

        start_gather(0, 0)

        @pl.loop(0, n_win, step=2)
        def _(j0):
            for slot in range(2):
                j = j0 + slot
                other = 1 - slot

                @pl.when(j >= 1)
                def _():
                    wait_write(j - 1, other)

                @pl.when(j + 1 < n_win)
                def _():
                    start_gather(j + 1, other)

                wait_gather(slot)
                pltpu.async_copy(rows[slot], out_window(j), sems.at[2 + slot])

        wait_write(n_win - 1, (n_win - 1) % 2)

    return gather(table, idx)


def _combine_dense_kernel(*refs, n_samples, steps_per_sample):
    mod_ref, fg_ref, o_ref = refs[3 * n_samples:]
    sample = pl.program_id(0) // steps_per_sample
    for bt in range(n_samples):
        y4_ref, gate_ref, x1_ref = refs[3 * bt:3 * bt + 3]

        @pl.when(sample == bt)
        def _():
            gate = gate_ref[...]
            y = gate[:, 0:1] * _unpack_rows(y4_ref[0])
            for k in range(1, TOP_K):
                y = y + gate[:, k:k + 1] * _unpack_rows(y4_ref[k])
            x2 = x1_ref[0] + mod_ref[0, 5:6, :] * y
            o_ref[0] = (x2 * lax.rsqrt(jnp.mean(x2 * x2, axis=-1, keepdims=True) + NORM_EPS)
                        * fg_ref[...])


def _combine_dense(per_sample, mod, final_g):
    b = len(per_sample)
    _, l, d = per_sample[0][2].shape
    ts = 512
    nt = l // ts

    def window(bt):
        return lambda s: jnp.clip(s - bt * nt, 0, nt - 1)

    in_specs, args = [], []
    for bt, (y4, gate_w, x1) in enumerate(per_sample):
        w = window(bt)
        in_specs += [pl.BlockSpec((TOP_K, ts, ROW_WORDS), lambda s, w=w: (0, w(s), 0)),
                     pl.BlockSpec((ts, TOP_K), lambda s, w=w: (w(s), 0)),
                     pl.BlockSpec((1, ts, d), lambda s, w=w: (0, w(s), 0))]
        args += [y4, gate_w, x1]
    in_specs += [pl.BlockSpec((1, 6, d), lambda s: (s // nt, 0, 0)), _const_spec((1, d))]
    return pl.pallas_call(
        functools.partial(_combine_dense_kernel, n_samples=b, steps_per_sample=nt),
        grid=(b * nt,),
        in_specs=in_specs,
        out_specs=pl.BlockSpec((1, ts, d), lambda s: (s // nt, s % nt, 0)),
        out_shape=jax.ShapeDtypeStruct((b, l, d), F32),
        compiler_params=_params("arbitrary"),
        name="moe_combine",
    )(*args, mod, final_g)


def _rope_tables(seq_len):
    inv_freq = ROPE_BASE ** (-jnp.arange(ROPE_PAIRS, dtype=F32) / ROPE_PAIRS)
    reps = LANES // HEAD_DIM
    out = []
    for n_pos, on_row in ((seq_len // GRID_W, True), (GRID_W, False)):
        ang = jnp.arange(n_pos, dtype=F32)[:, None] * inv_freq
        zero = jnp.zeros_like(ang)
        for fn, sign in ((jnp.cos, 1.0), (jnp.sin, -1.0)):
            v = fn(ang)
            head = ([sign * v, v, zero, zero] if on_row else [zero, zero, sign * v, v])
            out.append(jnp.concatenate(head * reps, axis=1))
    return out


def kernel(x, c, ctx, c_ctx, w_ada, b_ada, norm1_g, norm2_g, w_in, b_in, attn_sink, w_pool,
           pool_scale, w_attn_br, w_pool_br, w_out, w_router, b_router, w_gu, b_gu, w_down,
           b_down, final_g):
    b, l, d = x.shape
    n = b * l
    assert w_ada.shape[0] == 1, "single-layer block"

    cond = jnp.zeros((SUBLANES, d), F32).at[:b].set(c).at[b].set(c_ctx)
    mod = _adaln(cond, w_ada[0], b_ada[0])[:b + 1].reshape(b + 1, 6, d)
    mod_x, mod_c = mod[:b], mod[b:b + 1]

    rope = _rope_tables(l)
    v0 = ATTN_W + KV_W
    w_v = w_in[0][:, v0:v0 + KV_W].reshape(d, N_KV_HEADS, HEAD_DIM)
    w_v = jnp.concatenate([w_v, jnp.zeros_like(w_v)], axis=-1).reshape(d, VEXT_W)
    b_v = b_in[0][v0:v0 + KV_W].reshape(N_KV_HEADS, HEAD_DIM)
    b_v = jnp.concatenate([b_v, jnp.ones_like(b_v)], axis=-1).reshape(VEXT_W)
    w_in_bf = jnp.concatenate([w_in[0][:, :v0], w_v, w_in[0][:, v0 + KV_W:]], axis=1).astype(BF16)
    b_in2 = jnp.concatenate([b_in[0][:v0], b_v, b_in[0][v0 + KV_W:]]).reshape(1, IN_W_EXT)
    g1 = norm1_g[0].reshape(1, d)
    q, kt, v, u, ga, gp = _inproj(x, mod_x, g1, w_in_bf, b_in2, rope)
    kv_sl = slice(ATTN_W, ATTN_W + KV_W + VEXT_W)
    kxt, vx = _ctx_kv(ctx, mod_c, g1, w_in_bf[:, kv_sl], b_in2[:, kv_sl])

    attn_o = _attention(q, kt, v, kxt, vx, attn_sink[0])

    merge_weights = (w_pool[0].astype(BF16), pool_scale[0].reshape(1, d),
                     w_attn_br[0].astype(BF16), w_pool_br[0].astype(BF16), w_out[0].astype(BF16),
                     norm2_g[0].reshape(1, d), w_router[0].astype(BF16),
                     b_router[0].reshape(1, N_EXPERTS))
    per_sample = []
    for bt in range(b):
        x1, h2, top_idx, gate_w, rank, counts = _merge_route(
            bt, attn_o, u, ga, gp, x, mod_x, *merge_weights)
        y4 = _moe_expert_rows(h2, top_idx, rank, counts, w_gu[0], b_gu[0], w_down[0], b_down[0])
        per_sample.append((y4, gate_w, x1))
    return _combine_dense(per_sample, mod_x, final_g.reshape(1, d))


def _moe_expert_rows(h2, top_idx, rank, counts, w_gu, b_gu, w_down, b_down):
    n = h2.shape[0]
    tm = TM_EXPERT
    nb = n * TOP_K // tm + N_EXPERTS
    cnt = counts[:, 0].astype(jnp.int32)
    padded = (cnt + tm - 1) // tm * tm
    pend = jnp.cumsum(padded)
    pstart = pend - padded
    expert_ids = jnp.arange(N_EXPERTS, dtype=jnp.int32)
    dest = jnp.sum(jnp.where(top_idx[..., None] == expert_ids, pstart, 0), axis=-1) + rank
    n_used = (pend[-1] // tm).reshape(1)
    block_start = jnp.arange(nb, dtype=jnp.int32) * tm
    block_e = jnp.minimum(jnp.sum((pend[None, :] <= block_start[:, None]).astype(jnp.int32), axis=1),
                          N_EXPERTS - 1)
    first = jnp.concatenate([jnp.ones((1,), jnp.int32),
                             (block_e[1:] != block_e[:-1]).astype(jnp.int32)])
    later_used = (expert_ids[None, :] > expert_ids[:, None]) & (cnt[None, :] > 0)
    next_of_expert = jnp.min(jnp.where(later_used, expert_ids[None, :], N_EXPERTS), axis=1)
    next_of_expert = jnp.where(next_of_expert < N_EXPERTS, next_of_expert, -1)
    next_e = jnp.sum(jnp.where(block_e[:, None] == expert_ids, next_of_expert, 0), axis=-1)

    qm = EXPERT_ROW_QUANTUM
    pad_j = jnp.arange(qm, dtype=jnp.int32)[None, :]
    n_pad = (cnt + qm - 1) // qm * qm - cnt
    spare = nb * tm + expert_ids[:, None] * qm + pad_j
    pad_dest = jnp.where(pad_j < n_pad[:, None], (pstart + cnt)[:, None] + pad_j, spare)
    pad_dest = pad_dest.reshape(SC_WORKERS, N_EXPERTS * qm // (SC_WORKERS * SC_WINDOW), SC_WINDOW)

    xs = _sc_scatter_rows(h2, dest, pad_dest, nb * tm + N_EXPERTS * qm)
    used_end = jnp.sum(jnp.where(block_e[:, None] == expert_ids, pstart + cnt, 0), axis=-1)
    valid = jnp.clip(used_end - block_start, 0, tm)
    ys = _experts(block_e, first, next_e, valid, n_used, xs, w_gu, b_gu, w_down, b_down)
    return _sc_gather_rows(ys, dest.reshape(-1)).reshape(TOP_K, n, ROW_WORDS)
```

```python
import functools

import jax
import jax.numpy as jnp
from jax import lax
from jax.experimental import pallas as pl
from jax.experimental.pallas import tpu as pltpu
from jax.experimental.pallas import tpu_sc as plsc

D_MODEL = 1024
GRID_W = 64
HEAD_DIM = 64
N_HEADS = 16
N_KV_HEADS = 4
GROUP = N_HEADS // N_KV_HEADS
ATTN_W = N_HEADS * HEAD_DIM
KV_W = N_KV_HEADS * HEAD_DIM
WINDOW = 128
ATTN_SCALE = HEAD_DIM ** -0.5
ROPE_BASE = 10000.0
ROPE_PAIRS = HEAD_DIM // 4
POOL_WINDOWS = (2, 4, 8, 16)
POOL_GROUP_W = D_MODEL // len(POOL_WINDOWS)
IN_W = ATTN_W + 2 * KV_W + D_MODEL + 2 * D_MODEL
VEXT_W = N_KV_HEADS * 2 * HEAD_DIM
IN_W_EXT = IN_W - KV_W + VEXT_W
N_EXPERTS = 32
TOP_K = 4
D_FF = D_MODEL
SWIGLU_ALPHA = 1.702
SWIGLU_LIMIT = 7.0
NORM_EPS = 1e-5
NEG_INF = -1e30

LANES = 128
SUBLANES = 8
VMEM_LIMIT = 56 * 1024 * 1024

SC_CORES = 2
SC_SUBCORES = 16
SC_WORKERS = SC_CORES * SC_SUBCORES
SC_WINDOW = 128
SC_GATHER_WINDOW = 64

TM_INPROJ = 1024
INPROJ_SUB = 512
TQ = WINDOW
ATTN_Q_BLOCKS = 4
TM_MERGE = 1024
MERGE_SUB = 256
TM_EXPERT = 512
EXPERT_ROW_QUANTUM = 128
TM_COMBINE = 1024
POOL_HALO = SUBLANES

F32 = jnp.float32
BF16 = jnp.bfloat16
U32 = jnp.uint32
ROW_WORDS = D_MODEL // 2


def _pack_rows(value):
    bits = lax.bitcast_convert_type(value.astype(BF16).astype(F32), U32)
    return bits[:, :ROW_WORDS] | (bits[:, ROW_WORDS:] >> 16)


def _unpack_rows(words):
    hi = lax.bitcast_convert_type(words & jnp.uint32(0xFFFF0000), F32)
    lo = lax.bitcast_convert_type(words << 16, F32)
    return jnp.concatenate([hi, lo], axis=1)


def _params(*sem):
    return pltpu.CompilerParams(dimension_semantics=sem, vmem_limit_bytes=VMEM_LIMIT)


def _const_spec(shape):
    nd = len(shape)
    return pl.BlockSpec(shape, lambda *_: (0,) * nd, pipeline_mode=pl.Buffered(1))


def _adaln_kernel(c_ref, w_ref, b_ref, o_ref):
    c = c_ref[...]
    s = c * jax.nn.sigmoid(c)
    o_ref[...] = jnp.dot(s.astype(BF16), w_ref[...].astype(BF16),
                         preferred_element_type=F32) + b_ref[...]


def _adaln(cond, w_ada, b_ada):
    rows, d = cond.shape
    n = w_ada.shape[1]
    tn = 1024
    return pl.pallas_call(
        _adaln_kernel,
        grid=(n // tn,),
        in_specs=[_const_spec((rows, d)),
                  pl.BlockSpec((d, tn), lambda j: (0, j)),
                  pl.BlockSpec((1, tn), lambda j: (0, j))],
        out_specs=pl.BlockSpec((rows, tn), lambda j: (0, j)),
        out_shape=jax.ShapeDtypeStruct((rows, n), F32),
        compiler_params=_params("arbitrary"),
        name="adaln",
    )(cond, w_ada, b_ada.reshape(1, n))


def _norm_mod(x, g, shift, scale):
    y = x * lax.rsqrt(jnp.mean(x * x, axis=-1, keepdims=True) + NORM_EPS) * g
    return y * (1.0 + scale) + shift


def _rope(t, cos, sin_signed):
    w = t.shape[1]
    half = ROPE_PAIRS
    lane = lax.broadcasted_iota(jnp.int32, t.shape, 1)
    first = (lane & (2 * half - 1)) < half
    rot = jnp.where(first, pltpu.roll(t, w - half, 1), pltpu.roll(t, half, 1))
    reps = w // LANES
    cos_w = jnp.concatenate([cos] * reps, axis=1) if reps > 1 else cos
    sin_w = jnp.concatenate([sin_signed] * reps, axis=1) if reps > 1 else sin_signed
    return t * cos_w + rot * sin_w


def _token_table(row_ref, col_ref):
    gr = row_ref.shape[0]
    shape = (gr, GRID_W, LANES)
    full = (jnp.broadcast_to(row_ref[...][:, None, :], shape)
            + jnp.broadcast_to(col_ref[...][None, :, :], shape))
    return full.reshape(gr * GRID_W, LANES)


def _inproj_kernel(x_ref, mod_ref, g_ref, w_ref, b_ref, cr_ref, sr_ref, cc_ref, sc_ref,
                   q_ref, k_ref, v_ref, u_ref, ga_ref, gp_ref):
    sub = INPROJ_SUB
    cos_all = _token_table(cr_ref, cc_ref)
    sin_all = _token_table(sr_ref, sc_ref)
    cw = 512

    def work_items(r0):
        rs = slice(r0, r0 + sub)
        cos, sin = cos_all[rs], sin_all[rs]
        hb = _norm_mod(x_ref[0, rs, :], g_ref[...], mod_ref[0, 0:1, :], mod_ref[0, 1:2, :]).astype(BF16)

        def put_k(t):
            k_ref[0, :, rs] = _rope(t, cos, sin).T.astype(BF16)

        def put_v(t):
            v_ref[0, rs, :] = t.astype(BF16)

        items = []
        for c0 in range(0, ATTN_W, cw):
            def put_q(t, c0=c0):
                q_ref[0, rs, c0:c0 + cw] = (_rope(t, cos, sin) * ATTN_SCALE).astype(BF16)
            items.append((c0, cw, put_q))
        items.append((ATTN_W, KV_W, put_k))
        items.append((ATTN_W + KV_W, VEXT_W, put_v))
        base = ATTN_W + KV_W + VEXT_W
        for c0 in range(0, D_MODEL, cw):
            def put_u(t, c0=c0):
                u_ref[0, rs, c0:c0 + cw] = t
            items.append((base + c0, cw, put_u))
        for ref in (ga_ref, gp_ref):
            base += D_MODEL
            for c0 in range(0, D_MODEL, cw):
                def put_gate(t, ref=ref, c0=c0):
                    ref[0, rs, c0:c0 + cw] = jax.nn.sigmoid(t).astype(BF16)
                items.append((base + c0, cw, put_gate))
        return [(hb, c0, w, fn) for c0, w, fn in items]

    def proj(hb, c0, w):
        return jnp.dot(hb, w_ref[:, c0:c0 + w], preferred_element_type=F32) + b_ref[:, c0:c0 + w]

    items = [it for r0 in range(0, x_ref.shape[1], sub) for it in work_items(r0)]
    pending = proj(*items[0][:3])
    for ci, item in enumerate(items):
        ready = pending
        if ci + 1 < len(items):
            pending = proj(*items[ci + 1][:3])
        item[3](ready)


def _inproj(x, mod, norm_g, w_in_bf, b_in, rope):
    b, l, d = x.shape
    tm = TM_INPROJ
    row = lambda bi, i: (bi, i, 0)
    out_shape = (
        jax.ShapeDtypeStruct((b, l, ATTN_W), BF16),
        jax.ShapeDtypeStruct((b, KV_W, l), BF16),
        jax.ShapeDtypeStruct((b, l, VEXT_W), BF16),
        jax.ShapeDtypeStruct((b, l, D_MODEL), F32),
        jax.ShapeDtypeStruct((b, l, D_MODEL), BF16),
        jax.ShapeDtypeStruct((b, l, D_MODEL), BF16),
    )
    return pl.pallas_call(
        _inproj_kernel,
        grid=(b, l // tm),
        in_specs=[pl.BlockSpec((1, tm, d), row),
                  pl.BlockSpec((1, 6, d), lambda bi, i: (bi, 0, 0)),
                  _const_spec((1, d)),
                  _const_spec((d, IN_W_EXT)),
                  _const_spec((1, IN_W_EXT)),
                  pl.BlockSpec((tm // GRID_W, LANES), lambda bi, i: (i, 0)),
                  pl.BlockSpec((tm // GRID_W, LANES), lambda bi, i: (i, 0)),
                  _const_spec((GRID_W, LANES)),
                  _const_spec((GRID_W, LANES))],
        out_specs=(pl.BlockSpec((1, tm, ATTN_W), row),
                   pl.BlockSpec((1, KV_W, tm), lambda bi, i: (bi, 0, i)),
                   pl.BlockSpec((1, tm, VEXT_W), row),
                   pl.BlockSpec((1, tm, D_MODEL), row),
                   pl.BlockSpec((1, tm, D_MODEL), row),
                   pl.BlockSpec((1, tm, D_MODEL), row)),
        out_shape=out_shape,
        compiler_params=_params("arbitrary", "arbitrary"),
        name="inproj",
    )(x, mod, norm_g, w_in_bf, b_in, *rope)


def _ctx_kv_kernel(x_ref, mod_ref, g_ref, w_ref, b_ref, k_ref, v_ref):
    h = _norm_mod(x_ref[0], g_ref[...], mod_ref[0, 0:1, :], mod_ref[0, 1:2, :])
    kv = jnp.dot(h.astype(BF16), w_ref[...], preferred_element_type=F32) + b_ref[...]
    k_ref[0] = kv[:, :KV_W].T.astype(BF16)
    v_ref[0] = kv[:, KV_W:].astype(BF16)


def _ctx_kv(ctx, mod_c, norm_g, w_kv_bf, b_kv):
    b, lc, d = ctx.shape
    row = lambda bi: (bi, 0, 0)
    return pl.pallas_call(
        _ctx_kv_kernel,
        grid=(b,),
        in_specs=[pl.BlockSpec((1, lc, d), row),
                  _const_spec((1, 6, d)),
                  _const_spec((1, d)),
                  _const_spec((d, KV_W + VEXT_W)),
                  _const_spec((1, KV_W + VEXT_W))],
        out_specs=(pl.BlockSpec((1, KV_W, lc), row), pl.BlockSpec((1, lc, VEXT_W), row)),
        out_shape=(jax.ShapeDtypeStruct((b, KV_W, lc), BF16),
                   jax.ShapeDtypeStruct((b, lc, VEXT_W), BF16)),
        compiler_params=_params("arbitrary"),
        name="ctx_kv",
    )(ctx, mod_c, norm_g, w_kv_bf, b_kv)


def _attn_kernel(sink_ref, q_ref, *refs):
    nq = ATTN_Q_BLOCKS
    k_refs, v_refs = refs[:nq + 2], refs[nq + 2:2 * nq + 4]
    kx_ref, vx_ref, o_ref = refs[2 * nq + 4:]
    first_blk = pl.program_id(1) * nq
    last_blk = pl.num_programs(1) * nq - 1
    rows = GROUP * TQ
    pair_w = 2 * HEAD_DIM
    qi = lax.broadcasted_iota(jnp.int32, (rows, TQ), 0) & (TQ - 1)
    col = lax.broadcasted_iota(jnp.int32, (rows, TQ), 1)
    low_half = lax.broadcasted_iota(jnp.int32, (TQ, pair_w), 1) < HEAD_DIM
    row_id = lax.broadcasted_iota(jnp.int32, (rows, 1), 0)

    def scores(b, j):
        qrows = slice(b * TQ, (b + 1) * TQ)
        parts = []
        for g in range(GROUP):
            h = j * GROUP + g
            pair = q_ref[0, qrows, (h // 2) * pair_w:(h // 2 + 1) * pair_w]
            parts.append(jnp.where(low_half if h % 2 == 0 else ~low_half, pair, jnp.zeros_like(pair)))
        q4 = jnp.concatenate(parts, axis=0)
        ks = slice(j * HEAD_DIM, (j + 1) * HEAD_DIM)
        kt = jnp.concatenate([k_refs[b + d][0, ks, :] for d in range(3)], axis=1)
        s_loc = jnp.dot(q4, jnp.concatenate([kt, kt], axis=0), preferred_element_type=F32)
        kxt = kx_ref[0, ks, :]
        s_ctx = jnp.dot(q4, jnp.concatenate([kxt, kxt], axis=0), preferred_element_type=F32)
        return s_loc, s_ctx

    def probs(b, j, s_loc, s_ctx):
        keep_prev = (col >= qi) & (first_blk + b > 0)
        keep_next = (col <= qi) & (first_blk + b < last_blk)
        pieces = [jnp.where(keep_prev, s_loc[:, :TQ], NEG_INF), s_loc[:, TQ:2 * TQ],
                  jnp.where(keep_next, s_loc[:, 2 * TQ:], NEG_INF)]
        pieces += [s_ctx[:, c0:c0 + TQ] for c0 in range(0, s_ctx.shape[1], TQ)]
        sink = jnp.zeros((rows, 1), F32)
        for g in range(GROUP):
            sink = jnp.where(row_id // TQ == g, sink_ref[j * GROUP + g], sink)
        mx = pieces[0]
        for pc in pieces[1:]:
            mx = jnp.maximum(mx, pc)
        m = jnp.maximum(jnp.max(mx, axis=-1, keepdims=True), sink)
        p = jnp.concatenate([jnp.exp(pc - m).astype(BF16) for pc in pieces], axis=1)
        return p, jnp.exp(sink - m)

    def output(b, j, p, sink_p):
        vs = slice(j * pair_w, (j + 1) * pair_w)
        v_all = jnp.concatenate([v_refs[b + d][0, :, vs] for d in range(3)]
                                + [vx_ref[0, :, vs]], axis=0)
        r = jnp.dot(p, v_all, preferred_element_type=F32)
        den = pltpu.roll(r, HEAD_DIM, 1) + sink_p
        o = r / den
        for g in range(0, GROUP, 2):
            even = o[g * TQ:(g + 1) * TQ]
            odd = pltpu.roll(o[(g + 1) * TQ:(g + 2) * TQ], HEAD_DIM, 1)
            c0 = (j * GROUP + g) * HEAD_DIM
            o_ref[0, b * TQ:(b + 1) * TQ, c0:c0 + pair_w] = jnp.where(low_half, even, odd).astype(BF16)

    items = [(b, j) for b in range(nq) for j in range(N_KV_HEADS)]
    s_queue = [scores(*items[0]), scores(*items[1])]
    p_queue = [probs(*items[0], *s_queue.pop(0))]
    for i, item in enumerate(items):
        if i + 2 < len(items):
            s_queue.append(scores(*items[i + 2]))
        ready = p_queue.pop(0)
        if i + 1 < len(items):
            p_queue.append(probs(*items[i + 1], *s_queue.pop(0)))
        output(*item, *ready)


def _attention(q, kt, v, kxt, vx, sink):
    b, l, _ = q.shape
    lc = kxt.shape[2]
    nq = ATTN_Q_BLOCKS
    nblk = l // TQ
    assert nblk % nq == 0

    def blk(n, d):
        return jnp.clip(n * nq + d - 1, 0, nblk - 1)

    k_specs = [pl.BlockSpec((1, KV_W, TQ), lambda bi, n, d=d: (bi, 0, blk(n, d))) for d in range(nq + 2)]
    v_specs = [pl.BlockSpec((1, TQ, VEXT_W), lambda bi, n, d=d: (bi, blk(n, d), 0)) for d in range(nq + 2)]
    return pl.pallas_call(
        _attn_kernel,
        grid=(b, nblk // nq),
        in_specs=[pl.BlockSpec(memory_space=pltpu.SMEM),
                  pl.BlockSpec((1, nq * TQ, ATTN_W), lambda bi, n: (bi, n, 0))]
                 + k_specs + v_specs
                 + [pl.BlockSpec((1, KV_W, lc), lambda bi, n: (bi, 0, 0)),
                    pl.BlockSpec((1, lc, VEXT_W), lambda bi, n: (bi, 0, 0))],
        out_specs=pl.BlockSpec((1, nq * TQ, ATTN_W), lambda bi, n: (bi, n, 0)),
        out_shape=jax.ShapeDtypeStruct((b, l, ATTN_W), BF16),
        compiler_params=_params("arbitrary", "arbitrary"),
        name="attention",
    )(sink, q, *([kt] * (nq + 2)), *([v] * (nq + 2)), kxt, vx)


def _shift_rows(a, d):
    n = a.shape[0]
    return pltpu.roll(a, (-d) % n, 0)


def _merge_kernel(ao_ref, u_ref, up_ref, un_ref, sa_ref, sp_ref, x_ref, mod_ref,
                  wpool_ref, pscale_ref, wab_ref, wpb_ref, wout_ref, g2_ref, wr_ref, br_ref,
                  x1_ref, h2_ref, idx_ref, gate_ref, rank_ref, cnt_ref, carry_ref, *, seq_len):
    bi = pl.program_id(0)
    i = pl.program_id(1)
    last = pl.num_programs(1) - 1
    tm, sub = TM_MERGE, MERGE_SUB

    @pl.when((bi == 0) & (i == 0))
    def _():
        carry_ref[...] = jnp.zeros_like(carry_ref)

    u = u_ref[0]
    prev = jnp.where(i > 0, up_ref[0], 0.0)
    nxt = jnp.where(i < last, un_ref[0], 0.0)
    ext = jnp.concatenate([prev, u, nxt], axis=0)

    def pool_diff(r0):
        t = i * tm + r0 + lax.broadcasted_iota(jnp.int32, (sub, 1), 0)
        diffs = []
        for g, w in enumerate(POOL_WINDOWS):
            cs = slice(g * POOL_GROUP_W, (g + 1) * POOL_GROUP_W)
            e = ext[r0:r0 + sub + 2 * POOL_HALO, cs]
            acc = _shift_rows(e, -1) + e
            step = 1
            while 2 * step < w:
                acc = _shift_rows(acc, -step) + _shift_rows(acc, step)
                step *= 2
            win = acc[POOL_HALO:POOL_HALO + sub]
            half = w // 2
            cnt = (jnp.minimum(t + half, seq_len) - jnp.maximum(t - half, 0)).astype(F32)
            diffs.append((win / cnt - u[r0:r0 + sub, cs]).astype(BF16))
        return diffs

    def branches(r0, diffs):
        rs = slice(r0, r0 + sub)
        a = jnp.dot(ao_ref[0, rs, :], wab_ref[...], preferred_element_type=F32)
        mixed = [jnp.dot(df, wpool_ref[g], preferred_element_type=F32) for g, df in enumerate(diffs)]
        pool_o = jnp.concatenate(mixed, axis=1) * pscale_ref[...]
        p = jnp.dot(pool_o.astype(BF16), wpb_ref[...], preferred_element_type=F32)
        merged = sa_ref[0, rs, :].astype(F32) * a + sp_ref[0, rs, :].astype(F32) * p
        return merged.astype(BF16)

    def out_proj(merged):
        return jnp.dot(merged, wout_ref[...], preferred_element_type=F32)

    def finish(r0, o):
        rs = slice(r0, r0 + sub)
        x1 = x_ref[0, rs, :] + mod_ref[0, 2:3, :] * o
        x1_ref[0, rs, :] = x1
        h2 = _norm_mod(x1, g2_ref[...], mod_ref[0, 3:4, :], mod_ref[0, 4:5, :])
        h2_ref[rs, :] = _pack_rows(h2)
        return jnp.dot(h2.astype(BF16), wr_ref[...], preferred_element_type=F32) + br_ref[...]

    def route(r0, logits, carry):
        lt = logits.T
        eid = lax.broadcasted_iota(jnp.int32, (N_EXPERTS, sub), 0).astype(F32)
        work = lt
        vals, idxs, sels = [], [], []
        for _ in range(TOP_K):
            mx = jnp.max(work, axis=0, keepdims=True)
            ix = jnp.min(jnp.where(work == mx, eid, float(N_EXPERTS)), axis=0, keepdims=True)
            sel = eid == ix
            work = jnp.where(sel, -jnp.inf, work)
            vals.append(mx)
            idxs.append(ix)
            sels.append(sel)
        ex = [jnp.exp(v - vals[0]) for v in vals]
        tot = ex[0] + ex[1] + ex[2] + ex[3]
        pad = [jnp.zeros_like(tot)] * (SUBLANES - TOP_K)
        gates = jnp.concatenate([e_ / tot for e_ in ex] + pad, axis=0)
        gate_ref[r0:r0 + sub, :] = gates.T[:, :TOP_K]
        idx_ref[:, r0:r0 + sub] = jnp.concatenate(idxs, axis=0).astype(jnp.int32)
        member = (sels[0] | sels[1] | sels[2] | sels[3]).astype(F32)
        t_row = lax.broadcasted_iota(jnp.int32, (sub, sub), 0)
        t_col = lax.broadcasted_iota(jnp.int32, (sub, sub), 1)
        earlier = (t_row < t_col).astype(BF16)
        before = jnp.dot(member.astype(BF16), earlier, preferred_element_type=F32) + carry
        ranks = [jnp.sum(jnp.where(s, before, 0.0), axis=0, keepdims=True) for s in sels]
        rank_ref[:, r0:r0 + sub] = jnp.concatenate(ranks, axis=0).astype(jnp.int32)
        return carry + jnp.sum(member, axis=1, keepdims=True)

    starts = list(range(0, tm, sub))
    carry = carry_ref[...]
    merged = branches(starts[0], pool_diff(starts[0]))
    for si, r0 in enumerate(starts):
        o = out_proj(merged)
        if si + 1 < len(starts):
            merged = branches(starts[si + 1], pool_diff(starts[si + 1]))
        carry = route(r0, finish(r0, o), carry)
    carry_ref[...] = carry
    cnt_ref[...] = carry


def _merge_route(attn_o, u, ga, gp, x, mod, w_pool_bf, pool_scale, wab, wpb, wout,
                 norm2_g, w_router, b_router):
    b, l, d = x.shape
    tm = TM_MERGE
    nt = l // tm
    n = b * l
    hb = tm // POOL_HALO
    row = lambda bi, i: (bi, i, 0)
    flat = lambda bi, i: (bi * nt + i, 0)
    halo_prev = lambda bi, i: (bi, jnp.maximum(i * hb - 1, 0), 0)
    halo_next = lambda bi, i: (bi, jnp.minimum((i + 1) * hb, l // POOL_HALO - 1), 0)
    slot_major = lambda bi, i: (0, bi * nt + i)
    out_shape = (
        jax.ShapeDtypeStruct((b, l, d), F32),
        jax.ShapeDtypeStruct((n, ROW_WORDS), U32),
        jax.ShapeDtypeStruct((TOP_K, n), jnp.int32),
        jax.ShapeDtypeStruct((n, TOP_K), F32),
        jax.ShapeDtypeStruct((TOP_K, n), jnp.int32),
        jax.ShapeDtypeStruct((N_EXPERTS, 1), F32),
    )
    return pl.pallas_call(
        functools.partial(_merge_kernel, seq_len=l),
        grid=(b, nt),
        in_specs=[pl.BlockSpec((1, tm, d), row),
                  pl.BlockSpec((1, tm, d), row),
                  pl.BlockSpec((1, POOL_HALO, d), halo_prev),
                  pl.BlockSpec((1, POOL_HALO, d), halo_next),
                  pl.BlockSpec((1, tm, d), row),
                  pl.BlockSpec((1, tm, d), row),
                  pl.BlockSpec((1, tm, d), row),
                  pl.BlockSpec((1, 6, d), lambda bi, i: (bi, 0, 0)),
                  _const_spec(w_pool_bf.shape),
                  _const_spec((1, d)),
                  _const_spec((d, d)), _const_spec((d, d)), _const_spec((d, d)),
                  _const_spec((1, d)),
                  _const_spec((d, N_EXPERTS)),
                  _const_spec((1, N_EXPERTS))],
        out_specs=(pl.BlockSpec((1, tm, d), row),
                   pl.BlockSpec((tm, ROW_WORDS), flat),
                   pl.BlockSpec((TOP_K, tm), slot_major),
                   pl.BlockSpec((tm, TOP_K), flat),
                   pl.BlockSpec((TOP_K, tm), slot_major),
                   _const_spec((N_EXPERTS, 1))),
        out_shape=out_shape,
        scratch_shapes=[pltpu.VMEM((N_EXPERTS, 1), F32)],
        compiler_params=_params("arbitrary", "arbitrary"),
        name="merge_route",
    )(attn_o, u, u, u, ga, gp, x, mod, w_pool_bf, pool_scale, wab, wpb, wout,
      norm2_g, w_router, b_router)


def _sc_worker_id():
    return lax.axis_index("s") * SC_CORES + lax.axis_index("c")


def _sc_scatter_rows(rows, dest, pad_dest, out_rows):
    n, w = rows.shape
    n_win = n // (SC_WORKERS * SC_WINDOW)
    n_pad_win = pad_dest.shape[1]
    assert dest.shape == (TOP_K, n) and SC_WORKERS * n_win * SC_WINDOW == n
    mesh = plsc.VectorSubcoreMesh(core_axis_name="c", subcore_axis_name="s")
    zeros = jnp.zeros((SC_WINDOW, w), rows.dtype)

    @functools.partial(
        pl.kernel, mesh=mesh,
        out_type=jax.ShapeDtypeStruct((out_rows, w), rows.dtype),
        scratch_types=[pltpu.VMEM((TOP_K, SC_WINDOW), jnp.int32),
                       pltpu.VMEM((n_pad_win, SC_WINDOW), jnp.int32),
                       pltpu.VMEM((SC_WINDOW, w), rows.dtype),
                       pltpu.SemaphoreType.DMA],
        name="sc_scatter_rows",
    )
    def scatter(rows_hbm, dest_hbm, pad_hbm, zeros_hbm, out_hbm, idx_v, pad_v, rows_v, sem):
        wid = _sc_worker_id()

        @pl.loop(0, n_win)
        def _(j):
            base = (wid * n_win + j) * SC_WINDOW
            pltpu.sync_copy(rows_hbm.at[pl.ds(base, SC_WINDOW)], rows_v)
            pltpu.sync_copy(dest_hbm.at[:, pl.ds(base, SC_WINDOW)], idx_v)
            copies = [pltpu.async_copy(rows_v, out_hbm.at[idx_v.at[k]], sem) for k in range(TOP_K)]
            for cp in copies:
                cp.wait()

        pltpu.sync_copy(zeros_hbm, rows_v)
        pltpu.sync_copy(pad_hbm.at[wid], pad_v)
        copies = [pltpu.async_copy(rows_v, out_hbm.at[pad_v.at[j]], sem) for j in range(n_pad_win)]
        for cp in copies:
            cp.wait()

    return scatter(rows, dest, pad_dest, zeros)


def _expert_kernel(be_ref, first_ref, next_ref, valid_ref, nu_ref, x_ref, wgu_hbm, bgu_ref, wd_hbm,
                   bd_ref, y_ref, wgu_f, wd_f, wgu_s, wd_s, sem):
    i = pl.program_id(0)

    def weight_copies(e):
        return (pltpu.make_async_copy(wgu_hbm.at[e], wgu_f, sem.at[0]),
                pltpu.make_async_copy(wd_hbm.at[e], wd_f, sem.at[1]))

    def start_weights(e):
        for cp in weight_copies(e):
            cp.start(priority=1)

    @pl.when(i == 0)
    def _():
        start_weights(be_ref[0])

    @pl.when(i < nu_ref[0])
    def _():
        e = be_ref[i]

        @pl.when(first_ref[i] == 1)
        def _():
            for cp in weight_copies(e):
                cp.wait()
            ck = 16

            def cast_rows(c, carry):
                r = pl.ds(pl.multiple_of(c * ck, ck), ck)
                wgu_s[r, :] = wgu_f[r, :].astype(BF16)
                wd_s[r, :] = wd_f[r, :].astype(BF16)
                return carry

            lax.fori_loop(0, D_MODEL // ck, cast_rows, 0, unroll=2)

            @pl.when(next_ref[i] >= 0)
            def _():
                start_weights(next_ref[i])

        def ffn(m):
            x = _unpack_rows(x_ref[:m, :]).astype(BF16)
            bgu = bgu_ref[e]
            gt = jnp.dot(x, wgu_s[:, :D_FF], preferred_element_type=F32) + bgu[:, :D_FF]
            ln = jnp.dot(x, wgu_s[:, D_FF:], preferred_element_type=F32) + bgu[:, D_FF:]
            gt = jnp.minimum(gt, SWIGLU_LIMIT)
            ln = jnp.clip(ln, -SWIGLU_LIMIT, SWIGLU_LIMIT)
            act = gt * jax.nn.sigmoid(SWIGLU_ALPHA * gt) * (ln + 1.0)
            y = jnp.dot(act.astype(BF16), wd_s[...], preferred_element_type=F32) + bd_ref[e]
            y_ref[:m, :] = _pack_rows(y)
            if m < TM_EXPERT:
                y_ref[m:, :] = jnp.zeros((TM_EXPERT - m, ROW_WORDS), U32)

        quanta = (valid_ref[i] + EXPERT_ROW_QUANTUM - 1) // EXPERT_ROW_QUANTUM
        for q in range(1, TM_EXPERT // EXPERT_ROW_QUANTUM + 1):
            pl.when(quanta == q)(functools.partial(ffn, q * EXPERT_ROW_QUANTUM))

    @pl.when(i >= nu_ref[0])
    def _():
        y_ref[...] = jnp.zeros_like(y_ref)


def _experts(block_e, first, next_e, valid, n_used, xs, w_gu, b_gu, w_down, b_down):
    d = D_MODEL
    nb = block_e.shape[0]
    tm = TM_EXPERT

    xmap = lambda i, be, fi, nx, va, nu: (jnp.minimum(i, nu[0] - 1), 0)
    grid_spec = pltpu.PrefetchScalarGridSpec(
        num_scalar_prefetch=5,
        grid=(nb,),
        in_specs=[pl.BlockSpec((tm, ROW_WORDS), xmap),
                  pl.BlockSpec(memory_space=pl.ANY),
                  _const_spec((N_EXPERTS, 1, 2 * D_FF)),
                  pl.BlockSpec(memory_space=pl.ANY),
                  _const_spec((N_EXPERTS, 1, d))],
        out_specs=pl.BlockSpec((tm, ROW_WORDS), lambda i, be, fi, nx, va, nu: (i, 0)),
        scratch_shapes=[pltpu.VMEM((d, 2 * D_FF), F32), pltpu.VMEM((D_FF, d), F32),
                        pltpu.VMEM((d, 2 * D_FF), BF16), pltpu.VMEM((D_FF, d), BF16),
                        pltpu.SemaphoreType.DMA((2,))],
    )
    return pl.pallas_call(
        _expert_kernel,
        grid_spec=grid_spec,
        out_shape=jax.ShapeDtypeStruct((nb * tm, ROW_WORDS), U32),
        compiler_params=_params("arbitrary"),
        name="moe_experts",
    )(block_e, first, next_e, valid, n_used, xs, w_gu, b_gu.reshape(N_EXPERTS, 1, 2 * D_FF),
      w_down, b_down.reshape(N_EXPERTS, 1, d))


def _sc_gather_rows(table, idx):
    m = idx.shape[0]
    w = table.shape[1]
    win = SC_GATHER_WINDOW
    per_worker = m // SC_WORKERS
    n_win = per_worker // win
    assert per_worker * SC_WORKERS == m and n_win * win == per_worker and n_win % 2 == 0
    mesh = plsc.VectorSubcoreMesh(core_axis_name="c", subcore_axis_name="s")

    @functools.partial(
        pl.kernel, mesh=mesh,
        out_type=jax.ShapeDtypeStruct((m, w), table.dtype),
        scratch_types=[pltpu.VMEM((win,), jnp.int32), pltpu.VMEM((win,), jnp.int32),
                       pltpu.VMEM((win, w), table.dtype), pltpu.VMEM((win, w), table.dtype),
                       pltpu.SemaphoreType.DMA((4,))],
        name="sc_gather_rows",
    )
    def gather(table_hbm, idx_hbm, out_hbm, idx0, idx1, rows0, rows1, sems):
        wid = _sc_worker_id()
        idxs, rows = (idx0, idx1), (rows0, rows1)

        def out_window(j):
            return out_hbm.at[pl.ds(wid * per_worker + j * win, win)]

        def start_gather(j, slot):
            pltpu.sync_copy(idx_hbm.at[pl.ds(wid * per_worker + j * win, win)], idxs[slot])
            pltpu.async_copy(table_hbm.at[idxs[slot]], rows[slot], sems.at[slot])

        def wait_gather(slot):
            pltpu.make_async_copy(table_hbm.at[idxs[slot]], rows[slot], sems.at[slot]).wait()

        def wait_write(j, slot):
            pltpu.make_async_copy(rows[slot], out_window(j), sems.at[2 + slot]).wait()

        start_gather(0, 0)

        @pl.loop(0, n_win, step=2)
        def _(j0):
            for slot in range(2):
                j = j0 + slot
                other = 1 - slot

                @pl.when(j >= 1)
                def _():
                    wait_write(j - 1, other)

                @pl.when(j + 1 < n_win)
                def _():
                    start_gather(j + 1, other)

                wait_gather(slot)
                pltpu.async_copy(rows[slot], out_window(j), sems.at[2 + slot])

        wait_write(n_win - 1, (n_win - 1) % 2)

    return gather(table, idx)


def _combine_dense_kernel(y4_ref, gate_ref, x1_ref, mod_ref, fg_ref, o_ref):
    gate = gate_ref[...]
    y = gate[:, 0:1] * _unpack_rows(y4_ref[0])
    for k in range(1, TOP_K):
        y = y + gate[:, k:k + 1] * _unpack_rows(y4_ref[k])
    x2 = x1_ref[...] + mod_ref[0, 5:6, :] * y
    o_ref[...] = x2 * lax.rsqrt(jnp.mean(x2 * x2, axis=-1, keepdims=True) + NORM_EPS) * fg_ref[...]


def _combine_dense(y4, gate_w, x1, mod, final_g):
    b, l, d = x1.shape
    ts = TM_COMBINE
    nt = l // ts
    return pl.pallas_call(
        _combine_dense_kernel,
        grid=(b * nt,),
        in_specs=[pl.BlockSpec((TOP_K, ts, ROW_WORDS), lambda s: (0, s, 0)),
                  pl.BlockSpec((ts, TOP_K), lambda s: (s, 0)),
                  pl.BlockSpec((ts, d), lambda s: (s, 0)),
                  pl.BlockSpec((1, 6, d), lambda s: (s // nt, 0, 0)),
                  _const_spec((1, d))],
        out_specs=pl.BlockSpec((ts, d), lambda s: (s, 0)),
        out_shape=jax.ShapeDtypeStruct((b * l, d), F32),
        compiler_params=_params("arbitrary"),
        name="moe_combine",
    )(y4, gate_w, x1.reshape(b * l, d), mod, final_g).reshape(b, l, d)


def _moe_expert_rows(h2, top_idx, rank, counts, w_gu, b_gu, w_down, b_down):
    n = h2.shape[0]
    tm = TM_EXPERT
    nb = n * TOP_K // tm + N_EXPERTS
    cnt = counts[:, 0].astype(jnp.int32)
    padded = (cnt + tm - 1) // tm * tm
    pend = jnp.cumsum(padded)
    pstart = pend - padded
    expert_ids = jnp.arange(N_EXPERTS, dtype=jnp.int32)
    dest = jnp.sum(jnp.where(top_idx[..., None] == expert_ids, pstart, 0), axis=-1) + rank
    n_used = (pend[-1] // tm).reshape(1)
    block_start = jnp.arange(nb, dtype=jnp.int32) * tm
    block_e = jnp.minimum(jnp.sum((pend[None, :] <= block_start[:, None]).astype(jnp.int32), axis=1),
                          N_EXPERTS - 1)
    first = jnp.concatenate([jnp.ones((1,), jnp.int32),
                             (block_e[1:] != block_e[:-1]).astype(jnp.int32)])
    later_used = (expert_ids[None, :] > expert_ids[:, None]) & (cnt[None, :] > 0)
    next_of_expert = jnp.min(jnp.where(later_used, expert_ids[None, :], N_EXPERTS), axis=1)
    next_of_expert = jnp.where(next_of_expert < N_EXPERTS, next_of_expert, -1)
    next_e = jnp.sum(jnp.where(block_e[:, None] == expert_ids, next_of_expert, 0), axis=-1)
    used_end = jnp.sum(jnp.where(block_e[:, None] == expert_ids, pstart + cnt, 0), axis=-1)
    valid = jnp.clip(used_end - block_start, 0, tm)

    qm = EXPERT_ROW_QUANTUM
    pad_j = jnp.arange(qm, dtype=jnp.int32)[None, :]
    n_pad = (cnt + qm - 1) // qm * qm - cnt
    spare = nb * tm + expert_ids[:, None] * qm + pad_j
    pad_dest = jnp.where(pad_j < n_pad[:, None], (pstart + cnt)[:, None] + pad_j, spare)
    pad_dest = pad_dest.reshape(SC_WORKERS, N_EXPERTS * qm // (SC_WORKERS * SC_WINDOW), SC_WINDOW)

    xs = _sc_scatter_rows(h2, dest, pad_dest, nb * tm + N_EXPERTS * qm)
    ys = _experts(block_e, first, next_e, valid, n_used, xs, w_gu, b_gu, w_down, b_down)
    return _sc_gather_rows(ys, dest.reshape(-1)).reshape(TOP_K, n, ROW_WORDS)


def _rope_tables(seq_len):
    inv_freq = ROPE_BASE ** (-jnp.arange(ROPE_PAIRS, dtype=F32) / ROPE_PAIRS)
    reps = LANES // HEAD_DIM
    out = []
    for n_pos, on_row in ((seq_len // GRID_W, True), (GRID_W, False)):
        ang = jnp.arange(n_pos, dtype=F32)[:, None] * inv_freq
        zero = jnp.zeros_like(ang)
        for fn, sign in ((jnp.cos, 1.0), (jnp.sin, -1.0)):
            v = fn(ang)
            head = ([sign * v, v, zero, zero] if on_row else [zero, zero, sign * v, v])
            out.append(jnp.concatenate(head * reps, axis=1))
    return out


def kernel(x, c, ctx, c_ctx, w_ada, b_ada, norm1_g, norm2_g, w_in, b_in, attn_sink, w_pool,
           pool_scale, w_attn_br, w_pool_br, w_out, w_router, b_router, w_gu, b_gu, w_down,
           b_down, final_g):
    b, l, d = x.shape
    assert w_ada.shape[0] == 1, "single-layer block"

    cond = jnp.zeros((SUBLANES, d), F32).at[:b].set(c).at[b].set(c_ctx)
    mod = _adaln(cond, w_ada[0], b_ada[0])[:b + 1].reshape(b + 1, 6, d)
    mod_x, mod_c = mod[:b], mod[b:b + 1]

    rope = _rope_tables(l)
    v0 = ATTN_W + KV_W
    w_v = w_in[0][:, v0:v0 + KV_W].reshape(d, N_KV_HEADS, HEAD_DIM)
    w_v = jnp.concatenate([w_v, jnp.zeros_like(w_v)], axis=-1).reshape(d, VEXT_W)
    b_v = b_in[0][v0:v0 + KV_W].reshape(N_KV_HEADS, HEAD_DIM)
    b_v = jnp.concatenate([b_v, jnp.ones_like(b_v)], axis=-1).reshape(VEXT_W)
    w_in_bf = jnp.concatenate([w_in[0][:, :v0], w_v, w_in[0][:, v0 + KV_W:]], axis=1).astype(BF16)
    b_in2 = jnp.concatenate([b_in[0][:v0], b_v, b_in[0][v0 + KV_W:]]).reshape(1, IN_W_EXT)
    g1 = norm1_g[0].reshape(1, d)
    q, kt, v, u, ga, gp = _inproj(x, mod_x, g1, w_in_bf, b_in2, rope)
    kv_sl = slice(ATTN_W, ATTN_W + KV_W + VEXT_W)
    kxt, vx = _ctx_kv(ctx, mod_c, g1, w_in_bf[:, kv_sl], b_in2[:, kv_sl])

    attn_o = _attention(q, kt, v, kxt, vx, attn_sink[0])

    x1, h2, top_idx, gate_w, rank, counts = _merge_route(
        attn_o, u, ga, gp, x, mod_x, w_pool[0].astype(BF16), pool_scale[0].reshape(1, d),
        w_attn_br[0].astype(BF16), w_pool_br[0].astype(BF16), w_out[0].astype(BF16),
        norm2_g[0].reshape(1, d), w_router[0].astype(BF16), b_router[0].reshape(1, N_EXPERTS))

    y4 = _moe_expert_rows(h2, top_idx, rank, counts, w_gu[0], b_gu[0], w_down[0], b_down[0])
    return _combine_dense(y4, gate_w, x1, mod_x, final_g.reshape(1, d))
```

```python
import functools

import jax
import jax.numpy as jnp
from jax import lax
from jax.experimental import pallas as pl
from jax.experimental.pallas import tpu as pltpu
from jax.experimental.pallas import tpu_sc as plsc

D_MODEL = 1024
GRID_W = 64
HEAD_DIM = 64
N_HEADS = 16
N_KV_HEADS = 4
GROUP = N_HEADS // N_KV_HEADS
ATTN_W = N_HEADS * HEAD_DIM
KV_W = N_KV_HEADS * HEAD_DIM
WINDOW = 128
ATTN_SCALE = HEAD_DIM ** -0.5
ROPE_BASE = 10000.0
ROPE_PAIRS = HEAD_DIM // 4
POOL_WINDOWS = (2, 4, 8, 16)
POOL_GROUP_W = D_MODEL // len(POOL_WINDOWS)
IN_W = ATTN_W + 2 * KV_W + D_MODEL + 2 * D_MODEL
VEXT_W = N_KV_HEADS * 2 * HEAD_DIM
N_EXPERTS = 32
TOP_K = 4
D_FF = D_MODEL
SWIGLU_ALPHA = 1.702
SWIGLU_LIMIT = 7.0
NORM_EPS = 1e-5
NEG_INF = -1e30

LANES = 128
SUBLANES = 8
VMEM_LIMIT = 56 * 1024 * 1024

SC_CORES = 2
SC_SUBCORES = 16
SC_WORKERS = SC_CORES * SC_SUBCORES
SC_WINDOW = 128
SC_GATHER_WINDOW = 64

TM_INPROJ = 1024
INPROJ_SUB = 512
TQ = WINDOW
ATTN_Q_BLOCKS = 4
TM_MERGE = 1024
MERGE_SUB = 256
TM_EXPERT = 512
EXPERT_ROW_QUANTUM = 128
TM_COMBINE = 1024
POOL_HALO = SUBLANES

F32 = jnp.float32
BF16 = jnp.bfloat16
U32 = jnp.uint32
ROW_WORDS = D_MODEL // 2


def _pack_rows(value):
    bits = lax.bitcast_convert_type(value.astype(BF16).astype(F32), U32)
    return bits[:, :ROW_WORDS] | (bits[:, ROW_WORDS:] >> 16)


def _unpack_rows(words):
    hi = lax.bitcast_convert_type(words & jnp.uint32(0xFFFF0000), F32)
    lo = lax.bitcast_convert_type(words << 16, F32)
    return jnp.concatenate([hi, lo], axis=1)


def _params(*sem):
    return pltpu.CompilerParams(dimension_semantics=sem, vmem_limit_bytes=VMEM_LIMIT)


def _const_spec(shape):
    nd = len(shape)
    return pl.BlockSpec(shape, lambda *_: (0,) * nd, pipeline_mode=pl.Buffered(1))


def _adaln_kernel(c_ref, w_ref, b_ref, o_ref):
    c = c_ref[...]
    s = c * jax.nn.sigmoid(c)
    o_ref[...] = jnp.dot(s.astype(BF16), w_ref[...].astype(BF16),
                         preferred_element_type=F32) + b_ref[...]


def _adaln(cond, w_ada, b_ada):
    rows, d = cond.shape
    n = w_ada.shape[1]
    tn = 1024
    return pl.pallas_call(
        _adaln_kernel,
        grid=(n // tn,),
        in_specs=[_const_spec((rows, d)),
                  pl.BlockSpec((d, tn), lambda j: (0, j)),
                  pl.BlockSpec((1, tn), lambda j: (0, j))],
        out_specs=pl.BlockSpec((rows, tn), lambda j: (0, j)),
        out_shape=jax.ShapeDtypeStruct((rows, n), F32),
        compiler_params=_params("arbitrary"),
        name="adaln",
    )(cond, w_ada, b_ada.reshape(1, n))


def _norm_mod(x, g, shift, scale):
    y = x * lax.rsqrt(jnp.mean(x * x, axis=-1, keepdims=True) + NORM_EPS) * g
    return y * (1.0 + scale) + shift


def _rope(t, cos, sin_signed):
    w = t.shape[1]
    half = ROPE_PAIRS
    lane = lax.broadcasted_iota(jnp.int32, t.shape, 1)
    first = (lane & (2 * half - 1)) < half
    rot = jnp.where(first, pltpu.roll(t, w - half, 1), pltpu.roll(t, half, 1))
    reps = w // LANES
    cos_w = jnp.concatenate([cos] * reps, axis=1) if reps > 1 else cos
    sin_w = jnp.concatenate([sin_signed] * reps, axis=1) if reps > 1 else sin_signed
    return t * cos_w + rot * sin_w


def _widen_values(v):
    ones = jnp.ones((v.shape[0], HEAD_DIM), BF16)
    vb = v.astype(BF16)
    parts = []
    for j in range(N_KV_HEADS):
        parts += [vb[:, j * HEAD_DIM:(j + 1) * HEAD_DIM], ones]
    return jnp.concatenate(parts, axis=1)


def _token_table(row_ref, col_ref):
    gr = row_ref.shape[0]
    shape = (gr, GRID_W, LANES)
    full = (jnp.broadcast_to(row_ref[...][:, None, :], shape)
            + jnp.broadcast_to(col_ref[...][None, :, :], shape))
    return full.reshape(gr * GRID_W, LANES)


def _inproj_kernel(x_ref, mod_ref, g_ref, w_ref, b_ref, cr_ref, sr_ref, cc_ref, sc_ref,
                   q_ref, k_ref, v_ref, u_ref, ga_ref, gp_ref):
    sub = INPROJ_SUB
    cos_all = _token_table(cr_ref, cc_ref)
    sin_all = _token_table(sr_ref, sc_ref)
    cw = 512

    def work_items(r0):
        rs = slice(r0, r0 + sub)
        cos, sin = cos_all[rs], sin_all[rs]
        hb = _norm_mod(x_ref[0, rs, :], g_ref[...], mod_ref[0, 0:1, :], mod_ref[0, 1:2, :]).astype(BF16)

        def put_k(t):
            k_ref[0, :, rs] = _rope(t, cos, sin).T.astype(BF16)

        def put_v(t):
            v_ref[0, rs, :] = _widen_values(t)

        items = []
        for c0 in range(0, ATTN_W, cw):
            def put_q(t, c0=c0):
                q_ref[0, rs, c0:c0 + cw] = (_rope(t, cos, sin) * ATTN_SCALE).astype(BF16)
            items.append((c0, cw, put_q))
        items.append((ATTN_W, KV_W, put_k))
        items.append((ATTN_W + KV_W, KV_W, put_v))
        base = ATTN_W + 2 * KV_W
        for c0 in range(0, D_MODEL, cw):
            def put_u(t, c0=c0):
                u_ref[0, rs, c0:c0 + cw] = t
            items.append((base + c0, cw, put_u))
        for ref in (ga_ref, gp_ref):
            base += D_MODEL
            for c0 in range(0, D_MODEL, cw):
                def put_gate(t, ref=ref, c0=c0):
                    ref[0, rs, c0:c0 + cw] = jax.nn.sigmoid(t).astype(BF16)
                items.append((base + c0, cw, put_gate))
        return [(hb, c0, w, fn) for c0, w, fn in items]

    def proj(hb, c0, w):
        return jnp.dot(hb, w_ref[:, c0:c0 + w], preferred_element_type=F32) + b_ref[:, c0:c0 + w]

    items = [it for r0 in range(0, x_ref.shape[1], sub) for it in work_items(r0)]
    pending = proj(*items[0][:3])
    for ci, item in enumerate(items):
        ready = pending
        if ci + 1 < len(items):
            pending = proj(*items[ci + 1][:3])
        item[3](ready)


def _inproj(x, mod, norm_g, w_in_bf, b_in, rope):
    b, l, d = x.shape
    tm = TM_INPROJ
    row = lambda bi, i: (bi, i, 0)
    out_shape = (
        jax.ShapeDtypeStruct((b, l, ATTN_W), BF16),
        jax.ShapeDtypeStruct((b, KV_W, l), BF16),
        jax.ShapeDtypeStruct((b, l, VEXT_W), BF16),
        jax.ShapeDtypeStruct((b, l, D_MODEL), F32),
        jax.ShapeDtypeStruct((b, l, D_MODEL), BF16),
        jax.ShapeDtypeStruct((b, l, D_MODEL), BF16),
    )
    return pl.pallas_call(
        _inproj_kernel,
        grid=(b, l // tm),
        in_specs=[pl.BlockSpec((1, tm, d), row),
                  pl.BlockSpec((1, 6, d), lambda bi, i: (bi, 0, 0)),
                  _const_spec((1, d)),
                  _const_spec((d, IN_W)),
                  _const_spec((1, IN_W)),
                  pl.BlockSpec((tm // GRID_W, LANES), lambda bi, i: (i, 0)),
                  pl.BlockSpec((tm // GRID_W, LANES), lambda bi, i: (i, 0)),
                  _const_spec((GRID_W, LANES)),
                  _const_spec((GRID_W, LANES))],
        out_specs=(pl.BlockSpec((1, tm, ATTN_W), row),
                   pl.BlockSpec((1, KV_W, tm), lambda bi, i: (bi, 0, i)),
                   pl.BlockSpec((1, tm, VEXT_W), row),
                   pl.BlockSpec((1, tm, D_MODEL), row),
                   pl.BlockSpec((1, tm, D_MODEL), row),
                   pl.BlockSpec((1, tm, D_MODEL), row)),
        out_shape=out_shape,
        compiler_params=_params("arbitrary", "arbitrary"),
        name="inproj",
    )(x, mod, norm_g, w_in_bf, b_in, *rope)


def _ctx_kv_kernel(x_ref, mod_ref, g_ref, w_ref, b_ref, k_ref, v_ref):
    h = _norm_mod(x_ref[0], g_ref[...], mod_ref[0, 0:1, :], mod_ref[0, 1:2, :])
    kv = jnp.dot(h.astype(BF16), w_ref[...], preferred_element_type=F32) + b_ref[...]
    k_ref[0] = kv[:, :KV_W].T.astype(BF16)
    v_ref[0] = _widen_values(kv[:, KV_W:])


def _ctx_kv(ctx, mod_c, norm_g, w_kv_bf, b_kv):
    b, lc, d = ctx.shape
    row = lambda bi: (bi, 0, 0)
    return pl.pallas_call(
        _ctx_kv_kernel,
        grid=(b,),
        in_specs=[pl.BlockSpec((1, lc, d), row),
                  _const_spec((1, 6, d)),
                  _const_spec((1, d)),
                  _const_spec((d, 2 * KV_W)),
                  _const_spec((1, 2 * KV_W))],
        out_specs=(pl.BlockSpec((1, KV_W, lc), row), pl.BlockSpec((1, lc, VEXT_W), row)),
        out_shape=(jax.ShapeDtypeStruct((b, KV_W, lc), BF16),
                   jax.ShapeDtypeStruct((b, lc, VEXT_W), BF16)),
        compiler_params=_params("arbitrary"),
        name="ctx_kv",
    )(ctx, mod_c, norm_g, w_kv_bf, b_kv)


def _attn_kernel(sink_ref, q_ref, *refs):
    nq = ATTN_Q_BLOCKS
    k_refs, v_refs = refs[:nq + 2], refs[nq + 2:2 * nq + 4]
    kx_ref, vx_ref, o_ref = refs[2 * nq + 4:]
    first_blk = pl.program_id(1) * nq
    last_blk = pl.num_programs(1) * nq - 1
    rows = GROUP * TQ
    pair_w = 2 * HEAD_DIM
    qi = lax.broadcasted_iota(jnp.int32, (rows, TQ), 0) & (TQ - 1)
    col = lax.broadcasted_iota(jnp.int32, (rows, TQ), 1)
    low_half = lax.broadcasted_iota(jnp.int32, (TQ, pair_w), 1) < HEAD_DIM
    row_id = lax.broadcasted_iota(jnp.int32, (rows, 1), 0)

    def scores(b, j):
        qrows = slice(b * TQ, (b + 1) * TQ)
        parts = []
        for g in range(GROUP):
            h = j * GROUP + g
            pair = q_ref[0, qrows, (h // 2) * pair_w:(h // 2 + 1) * pair_w]
            parts.append(jnp.where(low_half if h % 2 == 0 else ~low_half, pair, jnp.zeros_like(pair)))
        q4 = jnp.concatenate(parts, axis=0)
        ks = slice(j * HEAD_DIM, (j + 1) * HEAD_DIM)
        kt = jnp.concatenate([k_refs[b + d][0, ks, :] for d in range(3)], axis=1)
        s_loc = jnp.dot(q4, jnp.concatenate([kt, kt], axis=0), preferred_element_type=F32)
        kxt = kx_ref[0, ks, :]
        s_ctx = jnp.dot(q4, jnp.concatenate([kxt, kxt], axis=0), preferred_element_type=F32)
        return s_loc, s_ctx

    def probs(b, j, s_loc, s_ctx):
        keep_prev = (col >= qi) & (first_blk + b > 0)
        keep_next = (col <= qi) & (first_blk + b < last_blk)
        pieces = [jnp.where(keep_prev, s_loc[:, :TQ], NEG_INF), s_loc[:, TQ:2 * TQ],
                  jnp.where(keep_next, s_loc[:, 2 * TQ:], NEG_INF)]
        pieces += [s_ctx[:, c0:c0 + TQ] for c0 in range(0, s_ctx.shape[1], TQ)]
        sink = jnp.zeros((rows, 1), F32)
        for g in range(GROUP):
            sink = jnp.where(row_id // TQ == g, sink_ref[j * GROUP + g], sink)
        mx = pieces[0]
        for pc in pieces[1:]:
            mx = jnp.maximum(mx, pc)
        m = jnp.maximum(jnp.max(mx, axis=-1, keepdims=True), sink)
        p = jnp.concatenate([jnp.exp(pc - m).astype(BF16) for pc in pieces], axis=1)
        return p, jnp.exp(sink - m)

    def output(b, j, p, sink_p):
        vs = slice(j * pair_w, (j + 1) * pair_w)
        v_all = jnp.concatenate([v_refs[b + d][0, :, vs] for d in range(3)]
                                + [vx_ref[0, :, vs]], axis=0)
        r = jnp.dot(p, v_all, preferred_element_type=F32)
        den = pltpu.roll(r, HEAD_DIM, 1) + sink_p
        o = r / den
        for g in range(0, GROUP, 2):
            even = o[g * TQ:(g + 1) * TQ]
            odd = pltpu.roll(o[(g + 1) * TQ:(g + 2) * TQ], HEAD_DIM, 1)
            c0 = (j * GROUP + g) * HEAD_DIM
            o_ref[0, b * TQ:(b + 1) * TQ, c0:c0 + pair_w] = jnp.where(low_half, even, odd).astype(BF16)

    items = [(b, j) for b in range(nq) for j in range(N_KV_HEADS)]
    s_queue = [scores(*items[0]), scores(*items[1])]
    p_queue = [probs(*items[0], *s_queue.pop(0))]
    for i, item in enumerate(items):
        if i + 2 < len(items):
            s_queue.append(scores(*items[i + 2]))
        ready = p_queue.pop(0)
        if i + 1 < len(items):
            p_queue.append(probs(*items[i + 1], *s_queue.pop(0)))
        output(*item, *ready)


def _attention(q, kt, v, kxt, vx, sink):
    b, l, _ = q.shape
    lc = kxt.shape[2]
    nq = ATTN_Q_BLOCKS
    nblk = l // TQ
    assert nblk % nq == 0

    def blk(n, d):
        return jnp.clip(n * nq + d - 1, 0, nblk - 1)

    k_specs = [pl.BlockSpec((1, KV_W, TQ), lambda bi, n, d=d: (bi, 0, blk(n, d))) for d in range(nq + 2)]
    v_specs = [pl.BlockSpec((1, TQ, VEXT_W), lambda bi, n, d=d: (bi, blk(n, d), 0)) for d in range(nq + 2)]
    return pl.pallas_call(
        _attn_kernel,
        grid=(b, nblk // nq),
        in_specs=[pl.BlockSpec(memory_space=pltpu.SMEM),
                  pl.BlockSpec((1, nq * TQ, ATTN_W), lambda bi, n: (bi, n, 0))]
                 + k_specs + v_specs
                 + [pl.BlockSpec((1, KV_W, lc), lambda bi, n: (bi, 0, 0)),
                    pl.BlockSpec((1, lc, VEXT_W), lambda bi, n: (bi, 0, 0))],
        out_specs=pl.BlockSpec((1, nq * TQ, ATTN_W), lambda bi, n: (bi, n, 0)),
        out_shape=jax.ShapeDtypeStruct((b, l, ATTN_W), BF16),
        compiler_params=_params("arbitrary", "arbitrary"),
        name="attention",
    )(sink, q, *([kt] * (nq + 2)), *([v] * (nq + 2)), kxt, vx)


def _shift_rows(a, d):
    n = a.shape[0]
    return pltpu.roll(a, (-d) % n, 0)


def _merge_kernel(ao_ref, u_ref, up_ref, un_ref, sa_ref, sp_ref, x_ref, mod_ref,
                  wpool_ref, pscale_ref, wab_ref, wpb_ref, wout_ref, g2_ref, wr_ref, br_ref,
                  x1_ref, h2_ref, idx_ref, gate_ref, rank_ref, cnt_ref, carry_ref, *, seq_len):
    bi = pl.program_id(0)
    i = pl.program_id(1)
    last = pl.num_programs(1) - 1
    tm, sub = TM_MERGE, MERGE_SUB

    @pl.when((bi == 0) & (i == 0))
    def _():
        carry_ref[...] = jnp.zeros_like(carry_ref)

    u = u_ref[0]
    prev = jnp.where(i > 0, up_ref[0], 0.0)
    nxt = jnp.where(i < last, un_ref[0], 0.0)
    ext = jnp.concatenate([prev, u, nxt], axis=0)

    def pool_diff(r0):
        t = i * tm + r0 + lax.broadcasted_iota(jnp.int32, (sub, 1), 0)
        diffs = []
        for g, w in enumerate(POOL_WINDOWS):
            cs = slice(g * POOL_GROUP_W, (g + 1) * POOL_GROUP_W)
            e = ext[r0:r0 + sub + 2 * POOL_HALO, cs]
            acc = _shift_rows(e, -1) + e
            step = 1
            while 2 * step < w:
                acc = _shift_rows(acc, -step) + _shift_rows(acc, step)
                step *= 2
            win = acc[POOL_HALO:POOL_HALO + sub]
            half = w // 2
            cnt = (jnp.minimum(t + half, seq_len) - jnp.maximum(t - half, 0)).astype(F32)
            diffs.append((win / cnt - u[r0:r0 + sub, cs]).astype(BF16))
        return diffs

    def branches(r0, diffs):
        rs = slice(r0, r0 + sub)
        a = jnp.dot(ao_ref[0, rs, :], wab_ref[...], preferred_element_type=F32)
        mixed = [jnp.dot(df, wpool_ref[g], preferred_element_type=F32) for g, df in enumerate(diffs)]
        pool_o = jnp.concatenate(mixed, axis=1) * pscale_ref[...]
        p = jnp.dot(pool_o.astype(BF16), wpb_ref[...], preferred_element_type=F32)
        merged = sa_ref[0, rs, :].astype(F32) * a + sp_ref[0, rs, :].astype(F32) * p
        return merged.astype(BF16)

    def out_proj(merged):
        return jnp.dot(merged, wout_ref[...], preferred_element_type=F32)

    def finish(r0, o):
        rs = slice(r0, r0 + sub)
        x1 = x_ref[0, rs, :] + mod_ref[0, 2:3, :] * o
        x1_ref[0, rs, :] = x1
        h2 = _norm_mod(x1, g2_ref[...], mod_ref[0, 3:4, :], mod_ref[0, 4:5, :])
        h2_ref[rs, :] = _pack_rows(h2)
        return jnp.dot(h2.astype(BF16), wr_ref[...], preferred_element_type=F32) + br_ref[...]

    def route(r0, logits, carry):
        lt = logits.T
        eid = lax.broadcasted_iota(jnp.int32, (N_EXPERTS, sub), 0).astype(F32)
        work = lt
        vals, idxs, sels = [], [], []
        for _ in range(TOP_K):
            mx = jnp.max(work, axis=0, keepdims=True)
            ix = jnp.min(jnp.where(work == mx, eid, float(N_EXPERTS)), axis=0, keepdims=True)
            sel = eid == ix
            work = jnp.where(sel, -jnp.inf, work)
            vals.append(mx)
            idxs.append(ix)
            sels.append(sel)
        ex = [jnp.exp(v - vals[0]) for v in vals]
        tot = ex[0] + ex[1] + ex[2] + ex[3]
        pad = [jnp.zeros_like(tot)] * (SUBLANES - TOP_K)
        gates = jnp.concatenate([e_ / tot for e_ in ex] + pad, axis=0)
        gate_ref[r0:r0 + sub, :] = gates.T[:, :TOP_K]
        idx_ref[:, r0:r0 + sub] = jnp.concatenate(idxs, axis=0).astype(jnp.int32)
        member = (sels[0] | sels[1] | sels[2] | sels[3]).astype(F32)
        t_row = lax.broadcasted_iota(jnp.int32, (sub, sub), 0)
        t_col = lax.broadcasted_iota(jnp.int32, (sub, sub), 1)
        earlier = (t_row < t_col).astype(BF16)
        before = jnp.dot(member.astype(BF16), earlier, preferred_element_type=F32) + carry
        ranks = [jnp.sum(jnp.where(s, before, 0.0), axis=0, keepdims=True) for s in sels]
        rank_ref[:, r0:r0 + sub] = jnp.concatenate(ranks, axis=0).astype(jnp.int32)
        return carry + jnp.sum(member, axis=1, keepdims=True)

    starts = list(range(0, tm, sub))
    carry = carry_ref[...]
    merged = branches(starts[0], pool_diff(starts[0]))
    for si, r0 in enumerate(starts):
        o = out_proj(merged)
        if si + 1 < len(starts):
            merged = branches(starts[si + 1], pool_diff(starts[si + 1]))
        carry = route(r0, finish(r0, o), carry)
    carry_ref[...] = carry
    cnt_ref[...] = carry


def _merge_route(attn_o, u, ga, gp, x, mod, w_pool_bf, pool_scale, wab, wpb, wout,
                 norm2_g, w_router, b_router):
    b, l, d = x.shape
    tm = TM_MERGE
    nt = l // tm
    n = b * l
    hb = tm // POOL_HALO
    row = lambda bi, i: (bi, i, 0)
    flat = lambda bi, i: (bi * nt + i, 0)
    halo_prev = lambda bi, i: (bi, jnp.maximum(i * hb - 1, 0), 0)
    halo_next = lambda bi, i: (bi, jnp.minimum((i + 1) * hb, l // POOL_HALO - 1), 0)
    slot_major = lambda bi, i: (0, bi * nt + i)
    out_shape = (
        jax.ShapeDtypeStruct((b, l, d), F32),
        jax.ShapeDtypeStruct((n, ROW_WORDS), U32),
        jax.ShapeDtypeStruct((TOP_K, n), jnp.int32),
        jax.ShapeDtypeStruct((n, TOP_K), F32),
        jax.ShapeDtypeStruct((TOP_K, n), jnp.int32),
        jax.ShapeDtypeStruct((N_EXPERTS, 1), F32),
    )
    return pl.pallas_call(
        functools.partial(_merge_kernel, seq_len=l),
        grid=(b, nt),
        in_specs=[pl.BlockSpec((1, tm, d), row),
                  pl.BlockSpec((1, tm, d), row),
                  pl.BlockSpec((1, POOL_HALO, d), halo_prev),
                  pl.BlockSpec((1, POOL_HALO, d), halo_next),
                  pl.BlockSpec((1, tm, d), row),
                  pl.BlockSpec((1, tm, d), row),
                  pl.BlockSpec((1, tm, d), row),
                  pl.BlockSpec((1, 6, d), lambda bi, i: (bi, 0, 0)),
                  _const_spec(w_pool_bf.shape),
                  _const_spec((1, d)),
                  _const_spec((d, d)), _const_spec((d, d)), _const_spec((d, d)),
                  _const_spec((1, d)),
                  _const_spec((d, N_EXPERTS)),
                  _const_spec((1, N_EXPERTS))],
        out_specs=(pl.BlockSpec((1, tm, d), row),
                   pl.BlockSpec((tm, ROW_WORDS), flat),
                   pl.BlockSpec((TOP_K, tm), slot_major),
                   pl.BlockSpec((tm, TOP_K), flat),
                   pl.BlockSpec((TOP_K, tm), slot_major),
                   _const_spec((N_EXPERTS, 1))),
        out_shape=out_shape,
        scratch_shapes=[pltpu.VMEM((N_EXPERTS, 1), F32)],
        compiler_params=_params("arbitrary", "arbitrary"),
        name="merge_route",
    )(attn_o, u, u, u, ga, gp, x, mod, w_pool_bf, pool_scale, wab, wpb, wout,
      norm2_g, w_router, b_router)


def _sc_worker_id():
    return lax.axis_index("s") * SC_CORES + lax.axis_index("c")


def _sc_scatter_rows(rows, dest, pad_dest, out_rows):
    n, w = rows.shape
    n_win = n // (SC_WORKERS * SC_WINDOW)
    n_pad_win = pad_dest.shape[1]
    assert dest.shape == (TOP_K, n) and SC_WORKERS * n_win * SC_WINDOW == n
    mesh = plsc.VectorSubcoreMesh(core_axis_name="c", subcore_axis_name="s")
    zeros = jnp.zeros((SC_WINDOW, w), rows.dtype)

    @functools.partial(
        pl.kernel, mesh=mesh,
        out_type=jax.ShapeDtypeStruct((out_rows, w), rows.dtype),
        scratch_types=[pltpu.VMEM((TOP_K, SC_WINDOW), jnp.int32),
                       pltpu.VMEM((n_pad_win, SC_WINDOW), jnp.int32),
                       pltpu.VMEM((SC_WINDOW, w), rows.dtype),
                       pltpu.SemaphoreType.DMA],
        name="sc_scatter_rows",
    )
    def scatter(rows_hbm, dest_hbm, pad_hbm, zeros_hbm, out_hbm, idx_v, pad_v, rows_v, sem):
        wid = _sc_worker_id()

        @pl.loop(0, n_win)
        def _(j):
            base = (wid * n_win + j) * SC_WINDOW
            pltpu.sync_copy(rows_hbm.at[pl.ds(base, SC_WINDOW)], rows_v)
            pltpu.sync_copy(dest_hbm.at[:, pl.ds(base, SC_WINDOW)], idx_v)
            copies = [pltpu.async_copy(rows_v, out_hbm.at[idx_v.at[k]], sem) for k in range(TOP_K)]
            for cp in copies:
                cp.wait()

        pltpu.sync_copy(zeros_hbm, rows_v)
        pltpu.sync_copy(pad_hbm.at[wid], pad_v)
        copies = [pltpu.async_copy(rows_v, out_hbm.at[pad_v.at[j]], sem) for j in range(n_pad_win)]
        for cp in copies:
            cp.wait()

    return scatter(rows, dest, pad_dest, zeros)


def _expert_kernel(be_ref, first_ref, next_ref, valid_ref, nu_ref, x_ref, wgu_hbm, bgu_ref, wd_hbm,
                   bd_ref, y_ref, wgu_f, wd_f, wgu_s, wd_s, sem):
    i = pl.program_id(0)

    def weight_copies(e):
        return (pltpu.make_async_copy(wgu_hbm.at[e], wgu_f, sem.at[0]),
                pltpu.make_async_copy(wd_hbm.at[e], wd_f, sem.at[1]))

    def start_weights(e):
        for cp in weight_copies(e):
            cp.start(priority=1)

    @pl.when(i == 0)
    def _():
        start_weights(be_ref[0])

    @pl.when(i < nu_ref[0])
    def _():
        e = be_ref[i]

        @pl.when(first_ref[i] == 1)
        def _():
            for cp in weight_copies(e):
                cp.wait()
            ck = 16

            def cast_rows(c, carry):
                r = pl.ds(pl.multiple_of(c * ck, ck), ck)
                wgu_s[r, :] = wgu_f[r, :].astype(BF16)
                wd_s[r, :] = wd_f[r, :].astype(BF16)
                return carry

            lax.fori_loop(0, D_MODEL // ck, cast_rows, 0, unroll=2)

            @pl.when(next_ref[i] >= 0)
            def _():
                start_weights(next_ref[i])

        def ffn(m):
            x = _unpack_rows(x_ref[:m, :]).astype(BF16)
            bgu = bgu_ref[e]
            gt = jnp.dot(x, wgu_s[:, :D_FF], preferred_element_type=F32) + bgu[:, :D_FF]
            ln = jnp.dot(x, wgu_s[:, D_FF:], preferred_element_type=F32) + bgu[:, D_FF:]
            gt = jnp.minimum(gt, SWIGLU_LIMIT)
            ln = jnp.clip(ln, -SWIGLU_LIMIT, SWIGLU_LIMIT)
            act = gt * jax.nn.sigmoid(SWIGLU_ALPHA * gt) * (ln + 1.0)
            y = jnp.dot(act.astype(BF16), wd_s[...], preferred_element_type=F32) + bd_ref[e]
            y_ref[:m, :] = _pack_rows(y)
            if m < TM_EXPERT:
                y_ref[m:, :] = jnp.zeros((TM_EXPERT - m, ROW_WORDS), U32)

        quanta = (valid_ref[i] + EXPERT_ROW_QUANTUM - 1) // EXPERT_ROW_QUANTUM
        for q in range(1, TM_EXPERT // EXPERT_ROW_QUANTUM + 1):
            pl.when(quanta == q)(functools.partial(ffn, q * EXPERT_ROW_QUANTUM))

    @pl.when(i >= nu_ref[0])
    def _():
        y_ref[...] = jnp.zeros_like(y_ref)


def _experts(block_e, first, next_e, valid, n_used, xs, w_gu, b_gu, w_down, b_down):
    d = D_MODEL
    nb = block_e.shape[0]
    tm = TM_EXPERT

    xmap = lambda i, be, fi, nx, va, nu: (jnp.minimum(i, nu[0] - 1), 0)
    grid_spec = pltpu.PrefetchScalarGridSpec(
        num_scalar_prefetch=5,
        grid=(nb,),
        in_specs=[pl.BlockSpec((tm, ROW_WORDS), xmap),
                  pl.BlockSpec(memory_space=pl.ANY),
                  _const_spec((N_EXPERTS, 1, 2 * D_FF)),
                  pl.BlockSpec(memory_space=pl.ANY),
                  _const_spec((N_EXPERTS, 1, d))],
        out_specs=pl.BlockSpec((tm, ROW_WORDS), lambda i, be, fi, nx, va, nu: (i, 0)),
        scratch_shapes=[pltpu.VMEM((d, 2 * D_FF), F32), pltpu.VMEM((D_FF, d), F32),
                        pltpu.VMEM((d, 2 * D_FF), BF16), pltpu.VMEM((D_FF, d), BF16),
                        pltpu.SemaphoreType.DMA((2,))],
    )
    return pl.pallas_call(
        _expert_kernel,
        grid_spec=grid_spec,
        out_shape=jax.ShapeDtypeStruct((nb * tm, ROW_WORDS), U32),
        compiler_params=_params("arbitrary"),
        name="moe_experts",
    )(block_e, first, next_e, valid, n_used, xs, w_gu, b_gu.reshape(N_EXPERTS, 1, 2 * D_FF),
      w_down, b_down.reshape(N_EXPERTS, 1, d))


def _sc_gather_rows(table, idx):
    m = idx.shape[0]
    w = table.shape[1]
    win = SC_GATHER_WINDOW
    per_worker = m // SC_WORKERS
    n_win = per_worker // win
    assert per_worker * SC_WORKERS == m and n_win * win == per_worker and n_win % 2 == 0
    mesh = plsc.VectorSubcoreMesh(core_axis_name="c", subcore_axis_name="s")

    @functools.partial(
        pl.kernel, mesh=mesh,
        out_type=jax.ShapeDtypeStruct((m, w), table.dtype),
        scratch_types=[pltpu.VMEM((win,), jnp.int32), pltpu.VMEM((win,), jnp.int32),
                       pltpu.VMEM((win, w), table.dtype), pltpu.VMEM((win, w), table.dtype),
                       pltpu.SemaphoreType.DMA((4,))],
        name="sc_gather_rows",
    )
    def gather(table_hbm, idx_hbm, out_hbm, idx0, idx1, rows0, rows1, sems):
        wid = _sc_worker_id()
        idxs, rows = (idx0, idx1), (rows0, rows1)

        def out_window(j):
            return out_hbm.at[pl.ds(wid * per_worker + j * win, win)]

        def start_gather(j, slot):
            pltpu.sync_copy(idx_hbm.at[pl.ds(wid * per_worker + j * win, win)], idxs[slot])
            pltpu.async_copy(table_hbm.at[idxs[slot]], rows[slot], sems.at[slot])

        def wait_gather(slot):
            pltpu.make_async_copy(table_hbm.at[idxs[slot]], rows[slot], sems.at[slot]).wait()

        def wait_write(j, slot):
            pltpu.make_async_copy(rows[slot], out_window(j), sems.at[2 + slot]).wait()

        start_gather(0, 0)

        @pl.loop(0, n_win, step=2)
        def _(j0):
            for slot in range(2):
                j = j0 + slot
                other = 1 - slot

                @pl.when(j >= 1)
                def _():
                    wait_write(j - 1, other)

                @pl.when(j + 1 < n_win)
                def _():
                    start_gather(j + 1, other)

                wait_gather(slot)
                pltpu.async_copy(rows[slot], out_window(j), sems.at[2 + slot])

        wait_write(n_win - 1, (n_win - 1) % 2)

    return gather(table, idx)


def _combine_dense_kernel(y4_ref, gate_ref, x1_ref, mod_ref, fg_ref, o_ref):
    gate = gate_ref[...]
    y = gate[:, 0:1] * _unpack_rows(y4_ref[0])
    for k in range(1, TOP_K):
        y = y + gate[:, k:k + 1] * _unpack_rows(y4_ref[k])
    x2 = x1_ref[...] + mod_ref[0, 5:6, :] * y
    o_ref[...] = x2 * lax.rsqrt(jnp.mean(x2 * x2, axis=-1, keepdims=True) + NORM_EPS) * fg_ref[...]


def _combine_dense(y4, gate_w, x1, mod, final_g):
    b, l, d = x1.shape
    ts = TM_COMBINE
    nt = l // ts
    return pl.pallas_call(
        _combine_dense_kernel,
        grid=(b * nt,),
        in_specs=[pl.BlockSpec((TOP_K, ts, ROW_WORDS), lambda s: (0, s, 0)),
                  pl.BlockSpec((ts, TOP_K), lambda s: (s, 0)),
                  pl.BlockSpec((ts, d), lambda s: (s, 0)),
                  pl.BlockSpec((1, 6, d), lambda s: (s // nt, 0, 0)),
                  _const_spec((1, d))],
        out_specs=pl.BlockSpec((ts, d), lambda s: (s, 0)),
        out_shape=jax.ShapeDtypeStruct((b * l, d), F32),
        compiler_params=_params("arbitrary"),
        name="moe_combine",
    )(y4, gate_w, x1.reshape(b * l, d), mod, final_g).reshape(b, l, d)


def _moe_expert_rows(h2, top_idx, rank, counts, w_gu, b_gu, w_down, b_down):
    n = h2.shape[0]
    tm = TM_EXPERT
    nb = n * TOP_K // tm + N_EXPERTS
    cnt = counts[:, 0].astype(jnp.int32)
    padded = (cnt + tm - 1) // tm * tm
    pend = jnp.cumsum(padded)
    pstart = pend - padded
    expert_ids = jnp.arange(N_EXPERTS, dtype=jnp.int32)
    dest = jnp.sum(jnp.where(top_idx[..., None] == expert_ids, pstart, 0), axis=-1) + rank
    n_used = (pend[-1] // tm).reshape(1)
    block_start = jnp.arange(nb, dtype=jnp.int32) * tm
    block_e = jnp.minimum(jnp.sum((pend[None, :] <= block_start[:, None]).astype(jnp.int32), axis=1),
                          N_EXPERTS - 1)
    first = jnp.concatenate([jnp.ones((1,), jnp.int32),
                             (block_e[1:] != block_e[:-1]).astype(jnp.int32)])
    later_used = (expert_ids[None, :] > expert_ids[:, None]) & (cnt[None, :] > 0)
    next_of_expert = jnp.min(jnp.where(later_used, expert_ids[None, :], N_EXPERTS), axis=1)
    next_of_expert = jnp.where(next_of_expert < N_EXPERTS, next_of_expert, -1)
    next_e = jnp.sum(jnp.where(block_e[:, None] == expert_ids, next_of_expert, 0), axis=-1)
    used_end = jnp.sum(jnp.where(block_e[:, None] == expert_ids, pstart + cnt, 0), axis=-1)
    valid = jnp.clip(used_end - block_start, 0, tm)

    qm = EXPERT_ROW_QUANTUM
    pad_j = jnp.arange(qm, dtype=jnp.int32)[None, :]
    n_pad = (cnt + qm - 1) // qm * qm - cnt
    spare = nb * tm + expert_ids[:, None] * qm + pad_j
    pad_dest = jnp.where(pad_j < n_pad[:, None], (pstart + cnt)[:, None] + pad_j, spare)
    pad_dest = pad_dest.reshape(SC_WORKERS, N_EXPERTS * qm // (SC_WORKERS * SC_WINDOW), SC_WINDOW)

    xs = _sc_scatter_rows(h2, dest, pad_dest, nb * tm + N_EXPERTS * qm)
    ys = _experts(block_e, first, next_e, valid, n_used, xs, w_gu, b_gu, w_down, b_down)
    return _sc_gather_rows(ys, dest.reshape(-1)).reshape(TOP_K, n, ROW_WORDS)


def _rope_tables(seq_len):
    inv_freq = ROPE_BASE ** (-jnp.arange(ROPE_PAIRS, dtype=F32) / ROPE_PAIRS)
    reps = LANES // HEAD_DIM
    out = []
    for n_pos, on_row in ((seq_len // GRID_W, True), (GRID_W, False)):
        ang = jnp.arange(n_pos, dtype=F32)[:, None] * inv_freq
        zero = jnp.zeros_like(ang)
        for fn, sign in ((jnp.cos, 1.0), (jnp.sin, -1.0)):
            v = fn(ang)
            head = ([sign * v, v, zero, zero] if on_row else [zero, zero, sign * v, v])
            out.append(jnp.concatenate(head * reps, axis=1))
    return out


def kernel(x, c, ctx, c_ctx, w_ada, b_ada, norm1_g, norm2_g, w_in, b_in, attn_sink, w_pool,
           pool_scale, w_attn_br, w_pool_br, w_out, w_router, b_router, w_gu, b_gu, w_down,
           b_down, final_g):
    b, l, d = x.shape
    assert w_ada.shape[0] == 1, "single-layer block"

    cond = jnp.zeros((SUBLANES, d), F32).at[:b].set(c).at[b].set(c_ctx)
    mod = _adaln(cond, w_ada[0], b_ada[0])[:b + 1].reshape(b + 1, 6, d)
    mod_x, mod_c = mod[:b], mod[b:b + 1]

    rope = _rope_tables(l)
    w_in_bf = w_in[0].astype(BF16)
    b_in2 = b_in[0].reshape(1, IN_W)
    g1 = norm1_g[0].reshape(1, d)
    q, kt, v, u, ga, gp = _inproj(x, mod_x, g1, w_in_bf, b_in2, rope)
    kv_sl = slice(ATTN_W, ATTN_W + 2 * KV_W)
    kxt, vx = _ctx_kv(ctx, mod_c, g1, w_in_bf[:, kv_sl], b_in2[:, kv_sl])

    attn_o = _attention(q, kt, v, kxt, vx, attn_sink[0])

    x1, h2, top_idx, gate_w, rank, counts = _merge_route(
        attn_o, u, ga, gp, x, mod_x, w_pool[0].astype(BF16), pool_scale[0].reshape(1, d),
        w_attn_br[0].astype(BF16), w_pool_br[0].astype(BF16), w_out[0].astype(BF16),
        norm2_g[0].reshape(1, d), w_router[0].astype(BF16), b_router[0].reshape(1, N_EXPERTS))

    y4 = _moe_expert_rows(h2, top_idx, rank, counts, w_gu[0], b_gu[0], w_down[0], b_down[0])
    return _combine_dense(y4, gate_w, x1, mod_x, final_g.reshape(1, d))
```

```python
import functools

import jax
import jax.numpy as jnp
from jax import lax
from jax.experimental import pallas as pl
from jax.experimental.pallas import tpu as pltpu
from jax.experimental.pallas import tpu_sc as plsc

D_MODEL = 1024
GRID_W = 64
HEAD_DIM = 64
N_HEADS = 16
N_KV_HEADS = 4
GROUP = N_HEADS // N_KV_HEADS
ATTN_W = N_HEADS * HEAD_DIM
KV_W = N_KV_HEADS * HEAD_DIM
WINDOW = 128
ATTN_SCALE = HEAD_DIM ** -0.5
ROPE_BASE = 10000.0
ROPE_PAIRS = HEAD_DIM // 4
POOL_WINDOWS = (2, 4, 8, 16)
POOL_GROUP_W = D_MODEL // len(POOL_WINDOWS)
IN_W = ATTN_W + 2 * KV_W + D_MODEL + 2 * D_MODEL
VEXT_W = N_KV_HEADS * 2 * HEAD_DIM
N_EXPERTS = 32
TOP_K = 4
D_FF = D_MODEL
SWIGLU_ALPHA = 1.702
SWIGLU_LIMIT = 7.0
NORM_EPS = 1e-5
NEG_INF = -1e30

LANES = 128
SUBLANES = 8
VMEM_LIMIT = 56 * 1024 * 1024

SC_CORES = 2
SC_SUBCORES = 16
SC_WORKERS = SC_CORES * SC_SUBCORES
SC_WINDOW = 128
SC_GATHER_WINDOW = 64

TM_INPROJ = 1024
INPROJ_SUB = 512
TQ = WINDOW
ATTN_Q_BLOCKS = 8
TM_MERGE = 1024
MERGE_SUB = 256
TM_EXPERT = 512
EXPERT_ROW_QUANTUM = 128
TM_COMBINE = 1024
POOL_HALO = SUBLANES

F32 = jnp.float32
BF16 = jnp.bfloat16
U32 = jnp.uint32
ROW_WORDS = D_MODEL // 2


def _pack_rows(value):
    bits = lax.bitcast_convert_type(value.astype(BF16).astype(F32), U32)
    return bits[:, :ROW_WORDS] | (bits[:, ROW_WORDS:] >> 16)


def _unpack_rows(words):
    hi = lax.bitcast_convert_type(words & jnp.uint32(0xFFFF0000), F32)
    lo = lax.bitcast_convert_type(words << 16, F32)
    return jnp.concatenate([hi, lo], axis=1)


def _params(*sem):
    return pltpu.CompilerParams(dimension_semantics=sem, vmem_limit_bytes=VMEM_LIMIT)


def _const_spec(shape):
    nd = len(shape)
    return pl.BlockSpec(shape, lambda *_: (0,) * nd, pipeline_mode=pl.Buffered(1))


def _adaln_kernel(c_ref, w_ref, b_ref, o_ref):
    c = c_ref[...]
    s = c * jax.nn.sigmoid(c)
    o_ref[...] = jnp.dot(s.astype(BF16), w_ref[...].astype(BF16),
                         preferred_element_type=F32) + b_ref[...]


def _adaln(cond, w_ada, b_ada):
    rows, d = cond.shape
    n = w_ada.shape[1]
    tn = 1024
    return pl.pallas_call(
        _adaln_kernel,
        grid=(n // tn,),
        in_specs=[_const_spec((rows, d)),
                  pl.BlockSpec((d, tn), lambda j: (0, j)),
                  pl.BlockSpec((1, tn), lambda j: (0, j))],
        out_specs=pl.BlockSpec((rows, tn), lambda j: (0, j)),
        out_shape=jax.ShapeDtypeStruct((rows, n), F32),
        compiler_params=_params("arbitrary"),
        name="adaln",
    )(cond, w_ada, b_ada.reshape(1, n))


def _norm_mod(x, g, shift, scale):
    y = x * lax.rsqrt(jnp.mean(x * x, axis=-1, keepdims=True) + NORM_EPS) * g
    return y * (1.0 + scale) + shift


def _rope(t, cos, sin_signed):
    w = t.shape[1]
    half = ROPE_PAIRS
    lane = lax.broadcasted_iota(jnp.int32, t.shape, 1)
    first = (lane & (2 * half - 1)) < half
    rot = jnp.where(first, pltpu.roll(t, w - half, 1), pltpu.roll(t, half, 1))
    reps = w // LANES
    cos_w = jnp.concatenate([cos] * reps, axis=1) if reps > 1 else cos
    sin_w = jnp.concatenate([sin_signed] * reps, axis=1) if reps > 1 else sin_signed
    return t * cos_w + rot * sin_w


def _widen_values(v):
    ones = jnp.ones((v.shape[0], HEAD_DIM), BF16)
    vb = v.astype(BF16)
    parts = []
    for j in range(N_KV_HEADS):
        parts += [vb[:, j * HEAD_DIM:(j + 1) * HEAD_DIM], ones]
    return jnp.concatenate(parts, axis=1)


def _token_table(row_ref, col_ref):
    gr = row_ref.shape[0]
    shape = (gr, GRID_W, LANES)
    full = (jnp.broadcast_to(row_ref[...][:, None, :], shape)
            + jnp.broadcast_to(col_ref[...][None, :, :], shape))
    return full.reshape(gr * GRID_W, LANES)


def _inproj_kernel(x_ref, mod_ref, g_ref, w_ref, b_ref, cr_ref, sr_ref, cc_ref, sc_ref,
                   q_ref, k_ref, v_ref, u_ref, ga_ref, gp_ref):
    sub = INPROJ_SUB
    cos_all = _token_table(cr_ref, cc_ref)
    sin_all = _token_table(sr_ref, sc_ref)
    cw = 512

    def work_items(r0):
        rs = slice(r0, r0 + sub)
        cos, sin = cos_all[rs], sin_all[rs]
        hb = _norm_mod(x_ref[0, rs, :], g_ref[...], mod_ref[0, 0:1, :], mod_ref[0, 1:2, :]).astype(BF16)

        def put_k(t):
            k_ref[0, :, rs] = _rope(t, cos, sin).T.astype(BF16)

        def put_v(t):
            v_ref[0, rs, :] = _widen_values(t)

        items = []
        for c0 in range(0, ATTN_W, cw):
            def put_q(t, c0=c0):
                q_ref[0, rs, c0:c0 + cw] = (_rope(t, cos, sin) * ATTN_SCALE).astype(BF16)
            items.append((c0, cw, put_q))
        items.append((ATTN_W, KV_W, put_k))
        items.append((ATTN_W + KV_W, KV_W, put_v))
        base = ATTN_W + 2 * KV_W
        for c0 in range(0, D_MODEL, cw):
            def put_u(t, c0=c0):
                u_ref[0, rs, c0:c0 + cw] = t
            items.append((base + c0, cw, put_u))
        for ref in (ga_ref, gp_ref):
            base += D_MODEL
            for c0 in range(0, D_MODEL, cw):
                def put_gate(t, ref=ref, c0=c0):
                    ref[0, rs, c0:c0 + cw] = jax.nn.sigmoid(t).astype(BF16)
                items.append((base + c0, cw, put_gate))
        return [(hb, c0, w, fn) for c0, w, fn in items]

    def proj(hb, c0, w):
        return jnp.dot(hb, w_ref[:, c0:c0 + w], preferred_element_type=F32) + b_ref[:, c0:c0 + w]

    items = [it for r0 in range(0, x_ref.shape[1], sub) for it in work_items(r0)]
    pending = proj(*items[0][:3])
    for ci, item in enumerate(items):
        ready = pending
        if ci + 1 < len(items):
            pending = proj(*items[ci + 1][:3])
        item[3](ready)


def _inproj(x, mod, norm_g, w_in_bf, b_in, rope):
    b, l, d = x.shape
    tm = TM_INPROJ
    row = lambda bi, i: (bi, i, 0)
    out_shape = (
        jax.ShapeDtypeStruct((b, l, ATTN_W), BF16),
        jax.ShapeDtypeStruct((b, KV_W, l), BF16),
        jax.ShapeDtypeStruct((b, l, VEXT_W), BF16),
        jax.ShapeDtypeStruct((b, l, D_MODEL), F32),
        jax.ShapeDtypeStruct((b, l, D_MODEL), BF16),
        jax.ShapeDtypeStruct((b, l, D_MODEL), BF16),
    )
    return pl.pallas_call(
        _inproj_kernel,
        grid=(b, l // tm),
        in_specs=[pl.BlockSpec((1, tm, d), row),
                  pl.BlockSpec((1, 6, d), lambda bi, i: (bi, 0, 0)),
                  _const_spec((1, d)),
                  _const_spec((d, IN_W)),
                  _const_spec((1, IN_W)),
                  pl.BlockSpec((tm // GRID_W, LANES), lambda bi, i: (i, 0)),
                  pl.BlockSpec((tm // GRID_W, LANES), lambda bi, i: (i, 0)),
                  _const_spec((GRID_W, LANES)),
                  _const_spec((GRID_W, LANES))],
        out_specs=(pl.BlockSpec((1, tm, ATTN_W), row),
                   pl.BlockSpec((1, KV_W, tm), lambda bi, i: (bi, 0, i)),
                   pl.BlockSpec((1, tm, VEXT_W), row),
                   pl.BlockSpec((1, tm, D_MODEL), row),
                   pl.BlockSpec((1, tm, D_MODEL), row),
                   pl.BlockSpec((1, tm, D_MODEL), row)),
        out_shape=out_shape,
        compiler_params=_params("arbitrary", "arbitrary"),
        name="inproj",
    )(x, mod, norm_g, w_in_bf, b_in, *rope)


def _ctx_kv_kernel(x_ref, mod_ref, g_ref, w_ref, b_ref, k_ref, v_ref):
    h = _norm_mod(x_ref[0], g_ref[...], mod_ref[0, 0:1, :], mod_ref[0, 1:2, :])
    kv = jnp.dot(h.astype(BF16), w_ref[...], preferred_element_type=F32) + b_ref[...]
    k_ref[0] = kv[:, :KV_W].T.astype(BF16)
    v_ref[0] = _widen_values(kv[:, KV_W:])


def _ctx_kv(ctx, mod_c, norm_g, w_kv_bf, b_kv):
    b, lc, d = ctx.shape
    row = lambda bi: (bi, 0, 0)
    return pl.pallas_call(
        _ctx_kv_kernel,
        grid=(b,),
        in_specs=[pl.BlockSpec((1, lc, d), row),
                  _const_spec((1, 6, d)),
                  _const_spec((1, d)),
                  _const_spec((d, 2 * KV_W)),
                  _const_spec((1, 2 * KV_W))],
        out_specs=(pl.BlockSpec((1, KV_W, lc), row), pl.BlockSpec((1, lc, VEXT_W), row)),
        out_shape=(jax.ShapeDtypeStruct((b, KV_W, lc), BF16),
                   jax.ShapeDtypeStruct((b, lc, VEXT_W), BF16)),
        compiler_params=_params("arbitrary"),
        name="ctx_kv",
    )(ctx, mod_c, norm_g, w_kv_bf, b_kv)


def _attn_kernel(sink_ref, q_ref, *refs):
    nq = ATTN_Q_BLOCKS
    k_refs, v_refs = refs[:nq + 2], refs[nq + 2:2 * nq + 4]
    kx_ref, vx_ref, o_ref = refs[2 * nq + 4:]
    first_blk = pl.program_id(1) * nq
    last_blk = pl.num_programs(1) * nq - 1
    rows = GROUP * TQ
    pair_w = 2 * HEAD_DIM
    qi = lax.broadcasted_iota(jnp.int32, (rows, TQ), 0) & (TQ - 1)
    col = lax.broadcasted_iota(jnp.int32, (rows, TQ), 1)
    low_half = lax.broadcasted_iota(jnp.int32, (TQ, pair_w), 1) < HEAD_DIM
    row_id = lax.broadcasted_iota(jnp.int32, (rows, 1), 0)

    def scores(b, j):
        qrows = slice(b * TQ, (b + 1) * TQ)
        parts = []
        for g in range(GROUP):
            h = j * GROUP + g
            pair = q_ref[0, qrows, (h // 2) * pair_w:(h // 2 + 1) * pair_w]
            parts.append(jnp.where(low_half if h % 2 == 0 else ~low_half, pair, jnp.zeros_like(pair)))
        q4 = jnp.concatenate(parts, axis=0)
        ks = slice(j * HEAD_DIM, (j + 1) * HEAD_DIM)
        kt = jnp.concatenate([k_refs[b + d][0, ks, :] for d in range(3)], axis=1)
        s_loc = jnp.dot(q4, jnp.concatenate([kt, kt], axis=0), preferred_element_type=F32)
        kxt = kx_ref[0, ks, :]
        s_ctx = jnp.dot(q4, jnp.concatenate([kxt, kxt], axis=0), preferred_element_type=F32)
        return s_loc, s_ctx

    def probs(b, j, s_loc, s_ctx):
        keep_prev = (col >= qi) & (first_blk + b > 0)
        keep_next = (col <= qi) & (first_blk + b < last_blk)
        pieces = [jnp.where(keep_prev, s_loc[:, :TQ], NEG_INF), s_loc[:, TQ:2 * TQ],
                  jnp.where(keep_next, s_loc[:, 2 * TQ:], NEG_INF)]
        pieces += [s_ctx[:, c0:c0 + TQ] for c0 in range(0, s_ctx.shape[1], TQ)]
        sink = jnp.zeros((rows, 1), F32)
        for g in range(GROUP):
            sink = jnp.where(row_id // TQ == g, sink_ref[j * GROUP + g], sink)
        mx = pieces[0]
        for pc in pieces[1:]:
            mx = jnp.maximum(mx, pc)
        m = jnp.maximum(jnp.max(mx, axis=-1, keepdims=True), sink)
        p = jnp.concatenate([jnp.exp(pc - m).astype(BF16) for pc in pieces], axis=1)
        return p, jnp.exp(sink - m)

    def output(b, j, p, sink_p):
        vs = slice(j * pair_w, (j + 1) * pair_w)
        v_all = jnp.concatenate([v_refs[b + d][0, :, vs] for d in range(3)]
                                + [vx_ref[0, :, vs]], axis=0)
        r = jnp.dot(p, v_all, preferred_element_type=F32)
        den = pltpu.roll(r, HEAD_DIM, 1) + sink_p
        o = r / den
        for g in range(0, GROUP, 2):
            even = o[g * TQ:(g + 1) * TQ]
            odd = pltpu.roll(o[(g + 1) * TQ:(g + 2) * TQ], HEAD_DIM, 1)
            c0 = (j * GROUP + g) * HEAD_DIM
            o_ref[0, b * TQ:(b + 1) * TQ, c0:c0 + pair_w] = jnp.where(low_half, even, odd).astype(BF16)

    items = [(b, j) for b in range(nq) for j in range(N_KV_HEADS)]
    s_queue = [scores(*items[0]), scores(*items[1])]
    p_queue = [probs(*items[0], *s_queue.pop(0))]
    for i, item in enumerate(items):
        if i + 2 < len(items):
            s_queue.append(scores(*items[i + 2]))
        ready = p_queue.pop(0)
        if i + 1 < len(items):
            p_queue.append(probs(*items[i + 1], *s_queue.pop(0)))
        output(*item, *ready)


def _attention(q, kt, v, kxt, vx, sink):
    b, l, _ = q.shape
    lc = kxt.shape[2]
    nq = ATTN_Q_BLOCKS
    nblk = l // TQ
    assert nblk % nq == 0

    def blk(n, d):
        return jnp.clip(n * nq + d - 1, 0, nblk - 1)

    k_specs = [pl.BlockSpec((1, KV_W, TQ), lambda bi, n, d=d: (bi, 0, blk(n, d))) for d in range(nq + 2)]
    v_specs = [pl.BlockSpec((1, TQ, VEXT_W), lambda bi, n, d=d: (bi, blk(n, d), 0)) for d in range(nq + 2)]
    return pl.pallas_call(
        _attn_kernel,
        grid=(b, nblk // nq),
        in_specs=[pl.BlockSpec(memory_space=pltpu.SMEM),
                  pl.BlockSpec((1, nq * TQ, ATTN_W), lambda bi, n: (bi, n, 0))]
                 + k_specs + v_specs
                 + [pl.BlockSpec((1, KV_W, lc), lambda bi, n: (bi, 0, 0)),
                    pl.BlockSpec((1, lc, VEXT_W), lambda bi, n: (bi, 0, 0))],
        out_specs=pl.BlockSpec((1, nq * TQ, ATTN_W), lambda bi, n: (bi, n, 0)),
        out_shape=jax.ShapeDtypeStruct((b, l, ATTN_W), BF16),
        compiler_params=_params("arbitrary", "arbitrary"),
        name="attention",
    )(sink, q, *([kt] * (nq + 2)), *([v] * (nq + 2)), kxt, vx)


def _shift_rows(a, d):
    n = a.shape[0]
    return pltpu.roll(a, (-d) % n, 0)


def _merge_kernel(ao_ref, u_ref, up_ref, un_ref, sa_ref, sp_ref, x_ref, mod_ref,
                  wpool_ref, pscale_ref, wab_ref, wpb_ref, wout_ref, g2_ref, wr_ref, br_ref,
                  x1_ref, h2_ref, idx_ref, gate_ref, rank_ref, cnt_ref, carry_ref, *, seq_len):
    bi = pl.program_id(0)
    i = pl.program_id(1)
    last = pl.num_programs(1) - 1
    tm, sub = TM_MERGE, MERGE_SUB

    @pl.when((bi == 0) & (i == 0))
    def _():
        carry_ref[...] = jnp.zeros_like(carry_ref)

    u = u_ref[0]
    prev = jnp.where(i > 0, up_ref[0], 0.0)
    nxt = jnp.where(i < last, un_ref[0], 0.0)
    ext = jnp.concatenate([prev, u, nxt], axis=0)

    def pool_diff(r0):
        t = i * tm + r0 + lax.broadcasted_iota(jnp.int32, (sub, 1), 0)
        diffs = []
        for g, w in enumerate(POOL_WINDOWS):
            cs = slice(g * POOL_GROUP_W, (g + 1) * POOL_GROUP_W)
            e = ext[r0:r0 + sub + 2 * POOL_HALO, cs]
            acc = _shift_rows(e, -1) + e
            step = 1
            while 2 * step < w:
                acc = _shift_rows(acc, -step) + _shift_rows(acc, step)
                step *= 2
            win = acc[POOL_HALO:POOL_HALO + sub]
            half = w // 2
            cnt = (jnp.minimum(t + half, seq_len) - jnp.maximum(t - half, 0)).astype(F32)
            diffs.append((win / cnt - u[r0:r0 + sub, cs]).astype(BF16))
        return diffs

    def branches(r0, diffs):
        rs = slice(r0, r0 + sub)
        a = jnp.dot(ao_ref[0, rs, :], wab_ref[...], preferred_element_type=F32)
        mixed = [jnp.dot(df, wpool_ref[g], preferred_element_type=F32) for g, df in enumerate(diffs)]
        pool_o = jnp.concatenate(mixed, axis=1) * pscale_ref[...]
        p = jnp.dot(pool_o.astype(BF16), wpb_ref[...], preferred_element_type=F32)
        merged = sa_ref[0, rs, :].astype(F32) * a + sp_ref[0, rs, :].astype(F32) * p
        return merged.astype(BF16)

    def out_proj(merged):
        return jnp.dot(merged, wout_ref[...], preferred_element_type=F32)

    def finish(r0, o):
        rs = slice(r0, r0 + sub)
        x1 = x_ref[0, rs, :] + mod_ref[0, 2:3, :] * o
        x1_ref[0, rs, :] = x1
        h2 = _norm_mod(x1, g2_ref[...], mod_ref[0, 3:4, :], mod_ref[0, 4:5, :])
        h2_ref[rs, :] = _pack_rows(h2)
        return jnp.dot(h2.astype(BF16), wr_ref[...], preferred_element_type=F32) + br_ref[...]

    def route(r0, logits, carry):
        lt = logits.T
        eid = lax.broadcasted_iota(jnp.int32, (N_EXPERTS, sub), 0).astype(F32)
        work = lt
        vals, idxs, sels = [], [], []
        for _ in range(TOP_K):
            mx = jnp.max(work, axis=0, keepdims=True)
            ix = jnp.min(jnp.where(work == mx, eid, float(N_EXPERTS)), axis=0, keepdims=True)
            sel = eid == ix
            work = jnp.where(sel, -jnp.inf, work)
            vals.append(mx)
            idxs.append(ix)
            sels.append(sel)
        ex = [jnp.exp(v - vals[0]) for v in vals]
        tot = ex[0] + ex[1] + ex[2] + ex[3]
        pad = [jnp.zeros_like(tot)] * (SUBLANES - TOP_K)
        gates = jnp.concatenate([e_ / tot for e_ in ex] + pad, axis=0)
        gate_ref[r0:r0 + sub, :] = gates.T[:, :TOP_K]
        idx_ref[:, r0:r0 + sub] = jnp.concatenate(idxs, axis=0).astype(jnp.int32)
        member = (sels[0] | sels[1] | sels[2] | sels[3]).astype(F32)
        t_row = lax.broadcasted_iota(jnp.int32, (sub, sub), 0)
        t_col = lax.broadcasted_iota(jnp.int32, (sub, sub), 1)
        earlier = (t_row < t_col).astype(BF16)
        before = jnp.dot(member.astype(BF16), earlier, preferred_element_type=F32) + carry
        ranks = [jnp.sum(jnp.where(s, before, 0.0), axis=0, keepdims=True) for s in sels]
        rank_ref[:, r0:r0 + sub] = jnp.concatenate(ranks, axis=0).astype(jnp.int32)
        return carry + jnp.sum(member, axis=1, keepdims=True)

    starts = list(range(0, tm, sub))
    carry = carry_ref[...]
    merged = branches(starts[0], pool_diff(starts[0]))
    for si, r0 in enumerate(starts):
        o = out_proj(merged)
        if si + 1 < len(starts):
            merged = branches(starts[si + 1], pool_diff(starts[si + 1]))
        carry = route(r0, finish(r0, o), carry)
    carry_ref[...] = carry
    cnt_ref[...] = carry


def _merge_route(attn_o, u, ga, gp, x, mod, w_pool_bf, pool_scale, wab, wpb, wout,
                 norm2_g, w_router, b_router):
    b, l, d = x.shape
    tm = TM_MERGE
    nt = l // tm
    n = b * l
    hb = tm // POOL_HALO
    row = lambda bi, i: (bi, i, 0)
    flat = lambda bi, i: (bi * nt + i, 0)
    halo_prev = lambda bi, i: (bi, jnp.maximum(i * hb - 1, 0), 0)
    halo_next = lambda bi, i: (bi, jnp.minimum((i + 1) * hb, l // POOL_HALO - 1), 0)
    slot_major = lambda bi, i: (0, bi * nt + i)
    out_shape = (
        jax.ShapeDtypeStruct((b, l, d), F32),
        jax.ShapeDtypeStruct((n, ROW_WORDS), U32),
        jax.ShapeDtypeStruct((TOP_K, n), jnp.int32),
        jax.ShapeDtypeStruct((n, TOP_K), F32),
        jax.ShapeDtypeStruct((TOP_K, n), jnp.int32),
        jax.ShapeDtypeStruct((N_EXPERTS, 1), F32),
    )
    return pl.pallas_call(
        functools.partial(_merge_kernel, seq_len=l),
        grid=(b, nt),
        in_specs=[pl.BlockSpec((1, tm, d), row),
                  pl.BlockSpec((1, tm, d), row),
                  pl.BlockSpec((1, POOL_HALO, d), halo_prev),
                  pl.BlockSpec((1, POOL_HALO, d), halo_next),
                  pl.BlockSpec((1, tm, d), row),
                  pl.BlockSpec((1, tm, d), row),
                  pl.BlockSpec((1, tm, d), row),
                  pl.BlockSpec((1, 6, d), lambda bi, i: (bi, 0, 0)),
                  _const_spec(w_pool_bf.shape),
                  _const_spec((1, d)),
                  _const_spec((d, d)), _const_spec((d, d)), _const_spec((d, d)),
                  _const_spec((1, d)),
                  _const_spec((d, N_EXPERTS)),
                  _const_spec((1, N_EXPERTS))],
        out_specs=(pl.BlockSpec((1, tm, d), row),
                   pl.BlockSpec((tm, ROW_WORDS), flat),
                   pl.BlockSpec((TOP_K, tm), slot_major),
                   pl.BlockSpec((tm, TOP_K), flat),
                   pl.BlockSpec((TOP_K, tm), slot_major),
                   _const_spec((N_EXPERTS, 1))),
        out_shape=out_shape,
        scratch_shapes=[pltpu.VMEM((N_EXPERTS, 1), F32)],
        compiler_params=_params("arbitrary", "arbitrary"),
        name="merge_route",
    )(attn_o, u, u, u, ga, gp, x, mod, w_pool_bf, pool_scale, wab, wpb, wout,
      norm2_g, w_router, b_router)


def _sc_worker_id():
    return lax.axis_index("s") * SC_CORES + lax.axis_index("c")


def _sc_scatter_rows(rows, dest, pad_dest, out_rows):
    n, w = rows.shape
    n_win = n // (SC_WORKERS * SC_WINDOW)
    n_pad_win = pad_dest.shape[1]
    assert dest.shape == (TOP_K, n) and SC_WORKERS * n_win * SC_WINDOW == n
    mesh = plsc.VectorSubcoreMesh(core_axis_name="c", subcore_axis_name="s")
    zeros = jnp.zeros((SC_WINDOW, w), rows.dtype)

    @functools.partial(
        pl.kernel, mesh=mesh,
        out_type=jax.ShapeDtypeStruct((out_rows, w), rows.dtype),
        scratch_types=[pltpu.VMEM((TOP_K, SC_WINDOW), jnp.int32),
                       pltpu.VMEM((n_pad_win, SC_WINDOW), jnp.int32),
                       pltpu.VMEM((SC_WINDOW, w), rows.dtype),
                       pltpu.SemaphoreType.DMA],
        name="sc_scatter_rows",
    )
    def scatter(rows_hbm, dest_hbm, pad_hbm, zeros_hbm, out_hbm, idx_v, pad_v, rows_v, sem):
        wid = _sc_worker_id()

        @pl.loop(0, n_win)
        def _(j):
            base = (wid * n_win + j) * SC_WINDOW
            pltpu.sync_copy(rows_hbm.at[pl.ds(base, SC_WINDOW)], rows_v)
            pltpu.sync_copy(dest_hbm.at[:, pl.ds(base, SC_WINDOW)], idx_v)
            copies = [pltpu.async_copy(rows_v, out_hbm.at[idx_v.at[k]], sem) for k in range(TOP_K)]
            for cp in copies:
                cp.wait()

        pltpu.sync_copy(zeros_hbm, rows_v)
        pltpu.sync_copy(pad_hbm.at[wid], pad_v)
        copies = [pltpu.async_copy(rows_v, out_hbm.at[pad_v.at[j]], sem) for j in range(n_pad_win)]
        for cp in copies:
            cp.wait()

    return scatter(rows, dest, pad_dest, zeros)


def _expert_kernel(be_ref, first_ref, next_ref, valid_ref, nu_ref, x_ref, wgu_hbm, bgu_ref, wd_hbm,
                   bd_ref, y_ref, wgu_f, wd_f, wgu_s, wd_s, sem):
    i = pl.program_id(0)

    def weight_copies(e):
        return (pltpu.make_async_copy(wgu_hbm.at[e], wgu_f, sem.at[0]),
                pltpu.make_async_copy(wd_hbm.at[e], wd_f, sem.at[1]))

    def start_weights(e):
        for cp in weight_copies(e):
            cp.start(priority=1)

    @pl.when(i == 0)
    def _():
        start_weights(be_ref[0])

    @pl.when(i < nu_ref[0])
    def _():
        e = be_ref[i]

        @pl.when(first_ref[i] == 1)
        def _():
            for cp in weight_copies(e):
                cp.wait()
            ck = 16

            def cast_rows(c, carry):
                r = pl.ds(pl.multiple_of(c * ck, ck), ck)
                wgu_s[r, :] = wgu_f[r, :].astype(BF16)
                wd_s[r, :] = wd_f[r, :].astype(BF16)
                return carry

            lax.fori_loop(0, D_MODEL // ck, cast_rows, 0, unroll=2)

            @pl.when(next_ref[i] >= 0)
            def _():
                start_weights(next_ref[i])

        def ffn(m):
            x = _unpack_rows(x_ref[:m, :]).astype(BF16)
            bgu = bgu_ref[e]
            gt = jnp.dot(x, wgu_s[:, :D_FF], preferred_element_type=F32) + bgu[:, :D_FF]
            ln = jnp.dot(x, wgu_s[:, D_FF:], preferred_element_type=F32) + bgu[:, D_FF:]
            gt = jnp.minimum(gt, SWIGLU_LIMIT)
            ln = jnp.clip(ln, -SWIGLU_LIMIT, SWIGLU_LIMIT)
            act = gt * jax.nn.sigmoid(SWIGLU_ALPHA * gt) * (ln + 1.0)
            y = jnp.dot(act.astype(BF16), wd_s[...], preferred_element_type=F32) + bd_ref[e]
            y_ref[:m, :] = _pack_rows(y)
            if m < TM_EXPERT:
                y_ref[m:, :] = jnp.zeros((TM_EXPERT - m, ROW_WORDS), U32)

        quanta = (valid_ref[i] + EXPERT_ROW_QUANTUM - 1) // EXPERT_ROW_QUANTUM
        for q in range(1, TM_EXPERT // EXPERT_ROW_QUANTUM + 1):
            pl.when(quanta == q)(functools.partial(ffn, q * EXPERT_ROW_QUANTUM))

    @pl.when(i >= nu_ref[0])
    def _():
        y_ref[...] = jnp.zeros_like(y_ref)


def _experts(block_e, first, next_e, valid, n_used, xs, w_gu, b_gu, w_down, b_down):
    d = D_MODEL
    nb = block_e.shape[0]
    tm = TM_EXPERT

    xmap = lambda i, be, fi, nx, va, nu: (jnp.minimum(i, nu[0] - 1), 0)
    grid_spec = pltpu.PrefetchScalarGridSpec(
        num_scalar_prefetch=5,
        grid=(nb,),
        in_specs=[pl.BlockSpec((tm, ROW_WORDS), xmap),
                  pl.BlockSpec(memory_space=pl.ANY),
                  _const_spec((N_EXPERTS, 1, 2 * D_FF)),
                  pl.BlockSpec(memory_space=pl.ANY),
                  _const_spec((N_EXPERTS, 1, d))],
        out_specs=pl.BlockSpec((tm, ROW_WORDS), lambda i, be, fi, nx, va, nu: (i, 0)),
        scratch_shapes=[pltpu.VMEM((d, 2 * D_FF), F32), pltpu.VMEM((D_FF, d), F32),
                        pltpu.VMEM((d, 2 * D_FF), BF16), pltpu.VMEM((D_FF, d), BF16),
                        pltpu.SemaphoreType.DMA((2,))],
    )
    return pl.pallas_call(
        _expert_kernel,
        grid_spec=grid_spec,
        out_shape=jax.ShapeDtypeStruct((nb * tm, ROW_WORDS), U32),
        compiler_params=_params("arbitrary"),
        name="moe_experts",
    )(block_e, first, next_e, valid, n_used, xs, w_gu, b_gu.reshape(N_EXPERTS, 1, 2 * D_FF),
      w_down, b_down.reshape(N_EXPERTS, 1, d))


def _sc_gather_rows(table, idx):
    m = idx.shape[0]
    w = table.shape[1]
    win = SC_GATHER_WINDOW
    per_worker = m // SC_WORKERS
    n_win = per_worker // win
    assert per_worker * SC_WORKERS == m and n_win * win == per_worker and n_win % 2 == 0
    mesh = plsc.VectorSubcoreMesh(core_axis_name="c", subcore_axis_name="s")

    @functools.partial(
        pl.kernel, mesh=mesh,
        out_type=jax.ShapeDtypeStruct((m, w), table.dtype),
        scratch_types=[pltpu.VMEM((win,), jnp.int32), pltpu.VMEM((win,), jnp.int32),
                       pltpu.VMEM((win, w), table.dtype), pltpu.VMEM((win, w), table.dtype),
                       pltpu.SemaphoreType.DMA((4,))],
        name="sc_gather_rows",
    )
    def gather(table_hbm, idx_hbm, out_hbm, idx0, idx1, rows0, rows1, sems):
        wid = _sc_worker_id()
        idxs, rows = (idx0, idx1), (rows0, rows1)

        def out_window(j):
            return out_hbm.at[pl.ds(wid * per_worker + j * win, win)]

        def start_gather(j, slot):
            pltpu.sync_copy(idx_hbm.at[pl.ds(wid * per_worker + j * win, win)], idxs[slot])
            pltpu.async_copy(table_hbm.at[idxs[slot]], rows[slot], sems.at[slot])

        def wait_gather(slot):
            pltpu.make_async_copy(table_hbm.at[idxs[slot]], rows[slot], sems.at[slot]).wait()

        def wait_write(j, slot):
            pltpu.make_async_copy(rows[slot], out_window(j), sems.at[2 + slot]).wait()

        start_gather(0, 0)

        @pl.loop(0, n_win, step=2)
        def _(j0):
            for slot in range(2):
                j = j0 + slot
                other = 1 - slot

                @pl.when(j >= 1)
                def _():
                    wait_write(j - 1, other)

                @pl.when(j + 1 < n_win)
                def _():
                    start_gather(j + 1, other)

                wait_gather(slot)
                pltpu.async_copy(rows[slot], out_window(j), sems.at[2 + slot])

        wait_write(n_win - 1, (n_win - 1) % 2)

    return gather(table, idx)


def _combine_dense_kernel(y4_ref, gate_ref, x1_ref, mod_ref, fg_ref, o_ref):
    gate = gate_ref[...]
    y = gate[:, 0:1] * _unpack_rows(y4_ref[0])
    for k in range(1, TOP_K):
        y = y + gate[:, k:k + 1] * _unpack_rows(y4_ref[k])
    x2 = x1_ref[...] + mod_ref[0, 5:6, :] * y
    o_ref[...] = x2 * lax.rsqrt(jnp.mean(x2 * x2, axis=-1, keepdims=True) + NORM_EPS) * fg_ref[...]


def _combine_dense(y4, gate_w, x1, mod, final_g):
    b, l, d = x1.shape
    ts = TM_COMBINE
    nt = l // ts
    return pl.pallas_call(
        _combine_dense_kernel,
        grid=(b * nt,),
        in_specs=[pl.BlockSpec((TOP_K, ts, ROW_WORDS), lambda s: (0, s, 0)),
                  pl.BlockSpec((ts, TOP_K), lambda s: (s, 0)),
                  pl.BlockSpec((ts, d), lambda s: (s, 0)),
                  pl.BlockSpec((1, 6, d), lambda s: (s // nt, 0, 0)),
                  _const_spec((1, d))],
        out_specs=pl.BlockSpec((ts, d), lambda s: (s, 0)),
        out_shape=jax.ShapeDtypeStruct((b * l, d), F32),
        compiler_params=_params("arbitrary"),
        name="moe_combine",
    )(y4, gate_w, x1.reshape(b * l, d), mod, final_g).reshape(b, l, d)


def _moe_expert_rows(h2, top_idx, rank, counts, w_gu, b_gu, w_down, b_down):
    n = h2.shape[0]
    tm = TM_EXPERT
    nb = n * TOP_K // tm + N_EXPERTS
    cnt = counts[:, 0].astype(jnp.int32)
    padded = (cnt + tm - 1) // tm * tm
    pend = jnp.cumsum(padded)
    pstart = pend - padded
    expert_ids = jnp.arange(N_EXPERTS, dtype=jnp.int32)
    dest = jnp.sum(jnp.where(top_idx[..., None] == expert_ids, pstart, 0), axis=-1) + rank
    n_used = (pend[-1] // tm).reshape(1)
    block_start = jnp.arange(nb, dtype=jnp.int32) * tm
    block_e = jnp.minimum(jnp.sum((pend[None, :] <= block_start[:, None]).astype(jnp.int32), axis=1),
                          N_EXPERTS - 1)
    first = jnp.concatenate([jnp.ones((1,), jnp.int32),
                             (block_e[1:] != block_e[:-1]).astype(jnp.int32)])
    later_used = (expert_ids[None, :] > expert_ids[:, None]) & (cnt[None, :] > 0)
    next_of_expert = jnp.min(jnp.where(later_used, expert_ids[None, :], N_EXPERTS), axis=1)
    next_of_expert = jnp.where(next_of_expert < N_EXPERTS, next_of_expert, -1)
    next_e = jnp.sum(jnp.where(block_e[:, None] == expert_ids, next_of_expert, 0), axis=-1)
    used_end = jnp.sum(jnp.where(block_e[:, None] == expert_ids, pstart + cnt, 0), axis=-1)
    valid = jnp.clip(used_end - block_start, 0, tm)

    qm = EXPERT_ROW_QUANTUM
    pad_j = jnp.arange(qm, dtype=jnp.int32)[None, :]
    n_pad = (cnt + qm - 1) // qm * qm - cnt
    spare = nb * tm + expert_ids[:, None] * qm + pad_j
    pad_dest = jnp.where(pad_j < n_pad[:, None], (pstart + cnt)[:, None] + pad_j, spare)
    pad_dest = pad_dest.reshape(SC_WORKERS, N_EXPERTS * qm // (SC_WORKERS * SC_WINDOW), SC_WINDOW)

    xs = _sc_scatter_rows(h2, dest, pad_dest, nb * tm + N_EXPERTS * qm)
    ys = _experts(block_e, first, next_e, valid, n_used, xs, w_gu, b_gu, w_down, b_down)
    return _sc_gather_rows(ys, dest.reshape(-1)).reshape(TOP_K, n, ROW_WORDS)


def _rope_tables(seq_len):
    inv_freq = ROPE_BASE ** (-jnp.arange(ROPE_PAIRS, dtype=F32) / ROPE_PAIRS)
    reps = LANES // HEAD_DIM
    out = []
    for n_pos, on_row in ((seq_len // GRID_W, True), (GRID_W, False)):
        ang = jnp.arange(n_pos, dtype=F32)[:, None] * inv_freq
        zero = jnp.zeros_like(ang)
        for fn, sign in ((jnp.cos, 1.0), (jnp.sin, -1.0)):
            v = fn(ang)
            head = ([sign * v, v, zero, zero] if on_row else [zero, zero, sign * v, v])
            out.append(jnp.concatenate(head * reps, axis=1))
    return out


def kernel(x, c, ctx, c_ctx, w_ada, b_ada, norm1_g, norm2_g, w_in, b_in, attn_sink, w_pool,
           pool_scale, w_attn_br, w_pool_br, w_out, w_router, b_router, w_gu, b_gu, w_down,
           b_down, final_g):
    b, l, d = x.shape
    assert w_ada.shape[0] == 1, "single-layer block"

    cond = jnp.zeros((SUBLANES, d), F32).at[:b].set(c).at[b].set(c_ctx)
    mod = _adaln(cond, w_ada[0], b_ada[0])[:b + 1].reshape(b + 1, 6, d)
    mod_x, mod_c = mod[:b], mod[b:b + 1]

    rope = _rope_tables(l)
    w_in_bf = w_in[0].astype(BF16)
    b_in2 = b_in[0].reshape(1, IN_W)
    g1 = norm1_g[0].reshape(1, d)
    q, kt, v, u, ga, gp = _inproj(x, mod_x, g1, w_in_bf, b_in2, rope)
    kv_sl = slice(ATTN_W, ATTN_W + 2 * KV_W)
    kxt, vx = _ctx_kv(ctx, mod_c, g1, w_in_bf[:, kv_sl], b_in2[:, kv_sl])

    attn_o = _attention(q, kt, v, kxt, vx, attn_sink[0])

    x1, h2, top_idx, gate_w, rank, counts = _merge_route(
        attn_o, u, ga, gp, x, mod_x, w_pool[0].astype(BF16), pool_scale[0].reshape(1, d),
        w_attn_br[0].astype(BF16), w_pool_br[0].astype(BF16), w_out[0].astype(BF16),
        norm2_g[0].reshape(1, d), w_router[0].astype(BF16), b_router[0].reshape(1, N_EXPERTS))

    y4 = _moe_expert_rows(h2, top_idx, rank, counts, w_gu[0], b_gu[0], w_down[0], b_down[0])
    return _combine_dense(y4, gate_w, x1, mod_x, final_g.reshape(1, d))
```

```python
import functools

import jax
import jax.numpy as jnp
from jax import lax
from jax.experimental import pallas as pl
from jax.experimental.pallas import tpu as pltpu
from jax.experimental.pallas import tpu_sc as plsc

D_MODEL = 1024
GRID_W = 64
HEAD_DIM = 64
N_HEADS = 16
N_KV_HEADS = 4
GROUP = N_HEADS // N_KV_HEADS
ATTN_W = N_HEADS * HEAD_DIM
KV_W = N_KV_HEADS * HEAD_DIM
WINDOW = 128
ATTN_SCALE = HEAD_DIM ** -0.5
ROPE_BASE = 10000.0
ROPE_PAIRS = HEAD_DIM // 4
POOL_WINDOWS = (2, 4, 8, 16)
POOL_GROUP_W = D_MODEL // len(POOL_WINDOWS)
IN_W = ATTN_W + 2 * KV_W + D_MODEL + 2 * D_MODEL
VEXT_W = N_KV_HEADS * 2 * HEAD_DIM
N_EXPERTS = 32
TOP_K = 4
D_FF = D_MODEL
SWIGLU_ALPHA = 1.702
SWIGLU_LIMIT = 7.0
NORM_EPS = 1e-5
NEG_INF = -1e30

LANES = 128
SUBLANES = 8
VMEM_LIMIT = 56 * 1024 * 1024

SC_CORES = 2
SC_SUBCORES = 16
SC_WORKERS = SC_CORES * SC_SUBCORES
SC_WINDOW = 128
SC_GATHER_WINDOW = 64

TM_INPROJ = 1024
INPROJ_SUB = 512
TQ = WINDOW
ATTN_Q_BLOCKS = 4
TM_MERGE = 1024
MERGE_SUB = 256
TM_EXPERT = 512
EXPERT_ROW_QUANTUM = 128
TM_COMBINE = 1024
TC_GATHER_ROWS = 1024
TC_GATHER_UNROLL = 8
POOL_HALO = SUBLANES

F32 = jnp.float32
BF16 = jnp.bfloat16
U32 = jnp.uint32
ROW_WORDS = D_MODEL // 2


def _pack_rows(value):
    bits = lax.bitcast_convert_type(value.astype(BF16).astype(F32), U32)
    return bits[:, :ROW_WORDS] | (bits[:, ROW_WORDS:] >> 16)


def _unpack_rows(words):
    hi = lax.bitcast_convert_type(words & jnp.uint32(0xFFFF0000), F32)
    lo = lax.bitcast_convert_type(words << 16, F32)
    return jnp.concatenate([hi, lo], axis=1)


def _params(*sem):
    return pltpu.CompilerParams(dimension_semantics=sem, vmem_limit_bytes=VMEM_LIMIT)


def _const_spec(shape):
    nd = len(shape)
    return pl.BlockSpec(shape, lambda *_: (0,) * nd, pipeline_mode=pl.Buffered(1))


def _adaln_kernel(c_ref, w_ref, b_ref, o_ref):
    c = c_ref[...]
    s = c * jax.nn.sigmoid(c)
    o_ref[...] = jnp.dot(s.astype(BF16), w_ref[...].astype(BF16),
                         preferred_element_type=F32) + b_ref[...]


def _adaln(cond, w_ada, b_ada):
    rows, d = cond.shape
    n = w_ada.shape[1]
    tn = 1024
    return pl.pallas_call(
        _adaln_kernel,
        grid=(n // tn,),
        in_specs=[_const_spec((rows, d)),
                  pl.BlockSpec((d, tn), lambda j: (0, j)),
                  pl.BlockSpec((1, tn), lambda j: (0, j))],
        out_specs=pl.BlockSpec((rows, tn), lambda j: (0, j)),
        out_shape=jax.ShapeDtypeStruct((rows, n), F32),
        compiler_params=_params("arbitrary"),
        name="adaln",
    )(cond, w_ada, b_ada.reshape(1, n))


def _norm_mod(x, g, shift, scale):
    y = x * lax.rsqrt(jnp.mean(x * x, axis=-1, keepdims=True) + NORM_EPS) * g
    return y * (1.0 + scale) + shift


def _rope(t, cos, sin_signed):
    w = t.shape[1]
    half = ROPE_PAIRS
    lane = lax.broadcasted_iota(jnp.int32, t.shape, 1)
    first = (lane & (2 * half - 1)) < half
    rot = jnp.where(first, pltpu.roll(t, w - half, 1), pltpu.roll(t, half, 1))
    reps = w // LANES
    cos_w = jnp.concatenate([cos] * reps, axis=1) if reps > 1 else cos
    sin_w = jnp.concatenate([sin_signed] * reps, axis=1) if reps > 1 else sin_signed
    return t * cos_w + rot * sin_w


def _widen_values(v):
    ones = jnp.ones((v.shape[0], HEAD_DIM), BF16)
    vb = v.astype(BF16)
    parts = []
    for j in range(N_KV_HEADS):
        parts += [vb[:, j * HEAD_DIM:(j + 1) * HEAD_DIM], ones]
    return jnp.concatenate(parts, axis=1)


def _token_table(row_ref, col_ref):
    gr = row_ref.shape[0]
    shape = (gr, GRID_W, LANES)
    full = (jnp.broadcast_to(row_ref[...][:, None, :], shape)
            + jnp.broadcast_to(col_ref[...][None, :, :], shape))
    return full.reshape(gr * GRID_W, LANES)


def _inproj_kernel(x_ref, mod_ref, g_ref, w_ref, b_ref, cr_ref, sr_ref, cc_ref, sc_ref,
                   q_ref, k_ref, v_ref, u_ref, ga_ref, gp_ref):
    sub = INPROJ_SUB
    cos_all = _token_table(cr_ref, cc_ref)
    sin_all = _token_table(sr_ref, sc_ref)
    cw = 512

    def work_items(r0):
        rs = slice(r0, r0 + sub)
        cos, sin = cos_all[rs], sin_all[rs]
        hb = _norm_mod(x_ref[0, rs, :], g_ref[...], mod_ref[0, 0:1, :], mod_ref[0, 1:2, :]).astype(BF16)

        def put_k(t):
            k_ref[0, :, rs] = _rope(t, cos, sin).T.astype(BF16)

        def put_v(t):
            v_ref[0, rs, :] = _widen_values(t)

        items = []
        for c0 in range(0, ATTN_W, cw):
            def put_q(t, c0=c0):
                q_ref[0, rs, c0:c0 + cw] = (_rope(t, cos, sin) * ATTN_SCALE).astype(BF16)
            items.append((c0, cw, put_q))
        items.append((ATTN_W, KV_W, put_k))
        items.append((ATTN_W + KV_W, KV_W, put_v))
        base = ATTN_W + 2 * KV_W
        for c0 in range(0, D_MODEL, cw):
            def put_u(t, c0=c0):
                u_ref[0, rs, c0:c0 + cw] = t
            items.append((base + c0, cw, put_u))
        for ref in (ga_ref, gp_ref):
            base += D_MODEL
            for c0 in range(0, D_MODEL, cw):
                def put_gate(t, ref=ref, c0=c0):
                    ref[0, rs, c0:c0 + cw] = jax.nn.sigmoid(t).astype(BF16)
                items.append((base + c0, cw, put_gate))
        return [(hb, c0, w, fn) for c0, w, fn in items]

    def proj(hb, c0, w):
        return jnp.dot(hb, w_ref[:, c0:c0 + w], preferred_element_type=F32) + b_ref[:, c0:c0 + w]

    items = [it for r0 in range(0, x_ref.shape[1], sub) for it in work_items(r0)]
    pending = proj(*items[0][:3])
    for ci, item in enumerate(items):
        ready = pending
        if ci + 1 < len(items):
            pending = proj(*items[ci + 1][:3])
        item[3](ready)


def _inproj(x, mod, norm_g, w_in_bf, b_in, rope):
    b, l, d = x.shape
    tm = TM_INPROJ
    row = lambda bi, i: (bi, i, 0)
    out_shape = (
        jax.ShapeDtypeStruct((b, l, ATTN_W), BF16),
        jax.ShapeDtypeStruct((b, KV_W, l), BF16),
        jax.ShapeDtypeStruct((b, l, VEXT_W), BF16),
        jax.ShapeDtypeStruct((b, l, D_MODEL), F32),
        jax.ShapeDtypeStruct((b, l, D_MODEL), BF16),
        jax.ShapeDtypeStruct((b, l, D_MODEL), BF16),
    )
    return pl.pallas_call(
        _inproj_kernel,
        grid=(b, l // tm),
        in_specs=[pl.BlockSpec((1, tm, d), row),
                  pl.BlockSpec((1, 6, d), lambda bi, i: (bi, 0, 0)),
                  _const_spec((1, d)),
                  _const_spec((d, IN_W)),
                  _const_spec((1, IN_W)),
                  pl.BlockSpec((tm // GRID_W, LANES), lambda bi, i: (i, 0)),
                  pl.BlockSpec((tm // GRID_W, LANES), lambda bi, i: (i, 0)),
                  _const_spec((GRID_W, LANES)),
                  _const_spec((GRID_W, LANES))],
        out_specs=(pl.BlockSpec((1, tm, ATTN_W), row),
                   pl.BlockSpec((1, KV_W, tm), lambda bi, i: (bi, 0, i)),
                   pl.BlockSpec((1, tm, VEXT_W), row),
                   pl.BlockSpec((1, tm, D_MODEL), row),
                   pl.BlockSpec((1, tm, D_MODEL), row),
                   pl.BlockSpec((1, tm, D_MODEL), row)),
        out_shape=out_shape,
        compiler_params=_params("arbitrary", "arbitrary"),
        name="inproj",
    )(x, mod, norm_g, w_in_bf, b_in, *rope)


def _ctx_kv_kernel(x_ref, mod_ref, g_ref, w_ref, b_ref, k_ref, v_ref):
    h = _norm_mod(x_ref[0], g_ref[...], mod_ref[0, 0:1, :], mod_ref[0, 1:2, :])
    kv = jnp.dot(h.astype(BF16), w_ref[...], preferred_element_type=F32) + b_ref[...]
    k_ref[0] = kv[:, :KV_W].T.astype(BF16)
    v_ref[0] = _widen_values(kv[:, KV_W:])


def _ctx_kv(ctx, mod_c, norm_g, w_kv_bf, b_kv):
    b, lc, d = ctx.shape
    row = lambda bi: (bi, 0, 0)
    return pl.pallas_call(
        _ctx_kv_kernel,
        grid=(b,),
        in_specs=[pl.BlockSpec((1, lc, d), row),
                  _const_spec((1, 6, d)),
                  _const_spec((1, d)),
                  _const_spec((d, 2 * KV_W)),
                  _const_spec((1, 2 * KV_W))],
        out_specs=(pl.BlockSpec((1, KV_W, lc), row), pl.BlockSpec((1, lc, VEXT_W), row)),
        out_shape=(jax.ShapeDtypeStruct((b, KV_W, lc), BF16),
                   jax.ShapeDtypeStruct((b, lc, VEXT_W), BF16)),
        compiler_params=_params("arbitrary"),
        name="ctx_kv",
    )(ctx, mod_c, norm_g, w_kv_bf, b_kv)


def _attn_kernel(sink_ref, q_ref, *refs):
    nq = ATTN_Q_BLOCKS
    k_refs, v_refs = refs[:nq + 2], refs[nq + 2:2 * nq + 4]
    kx_ref, vx_ref, o_ref = refs[2 * nq + 4:]
    first_blk = pl.program_id(1) * nq
    last_blk = pl.num_programs(1) * nq - 1
    rows = GROUP * TQ
    pair_w = 2 * HEAD_DIM
    qi = lax.broadcasted_iota(jnp.int32, (rows, TQ), 0) & (TQ - 1)
    col = lax.broadcasted_iota(jnp.int32, (rows, TQ), 1)
    low_half = lax.broadcasted_iota(jnp.int32, (TQ, pair_w), 1) < HEAD_DIM
    row_id = lax.broadcasted_iota(jnp.int32, (rows, 1), 0)

    def scores(b, j):
        qrows = slice(b * TQ, (b + 1) * TQ)
        parts = []
        for g in range(GROUP):
            h = j * GROUP + g
            pair = q_ref[0, qrows, (h // 2) * pair_w:(h // 2 + 1) * pair_w]
            parts.append(jnp.where(low_half if h % 2 == 0 else ~low_half, pair, jnp.zeros_like(pair)))
        q4 = jnp.concatenate(parts, axis=0)
        ks = slice(j * HEAD_DIM, (j + 1) * HEAD_DIM)
        kt = jnp.concatenate([k_refs[b + d][0, ks, :] for d in range(3)], axis=1)
        s_loc = jnp.dot(q4, jnp.concatenate([kt, kt], axis=0), preferred_element_type=F32)
        kxt = kx_ref[0, ks, :]
        s_ctx = jnp.dot(q4, jnp.concatenate([kxt, kxt], axis=0), preferred_element_type=F32)
        return s_loc, s_ctx

    def probs(b, j, s_loc, s_ctx):
        keep_prev = (col >= qi) & (first_blk + b > 0)
        keep_next = (col <= qi) & (first_blk + b < last_blk)
        pieces = [jnp.where(keep_prev, s_loc[:, :TQ], NEG_INF), s_loc[:, TQ:2 * TQ],
                  jnp.where(keep_next, s_loc[:, 2 * TQ:], NEG_INF)]
        pieces += [s_ctx[:, c0:c0 + TQ] for c0 in range(0, s_ctx.shape[1], TQ)]
        sink = jnp.zeros((rows, 1), F32)
        for g in range(GROUP):
            sink = jnp.where(row_id // TQ == g, sink_ref[j * GROUP + g], sink)
        mx = pieces[0]
        for pc in pieces[1:]:
            mx = jnp.maximum(mx, pc)
        m = jnp.maximum(jnp.max(mx, axis=-1, keepdims=True), sink)
        p = jnp.concatenate([jnp.exp(pc - m).astype(BF16) for pc in pieces], axis=1)
        return p, jnp.exp(sink - m)

    def output(b, j, p, sink_p):
        vs = slice(j * pair_w, (j + 1) * pair_w)
        v_all = jnp.concatenate([v_refs[b + d][0, :, vs] for d in range(3)]
                                + [vx_ref[0, :, vs]], axis=0)
        r = jnp.dot(p, v_all, preferred_element_type=F32)
        den = pltpu.roll(r, HEAD_DIM, 1) + sink_p
        o = r / den
        for g in range(0, GROUP, 2):
            even = o[g * TQ:(g + 1) * TQ]
            odd = pltpu.roll(o[(g + 1) * TQ:(g + 2) * TQ], HEAD_DIM, 1)
            c0 = (j * GROUP + g) * HEAD_DIM
            o_ref[0, b * TQ:(b + 1) * TQ, c0:c0 + pair_w] = jnp.where(low_half, even, odd).astype(BF16)

    items = [(b, j) for b in range(nq) for j in range(N_KV_HEADS)]
    s_queue = [scores(*items[0]), scores(*items[1])]
    p_queue = [probs(*items[0], *s_queue.pop(0))]
    for i, item in enumerate(items):
        if i + 2 < len(items):
            s_queue.append(scores(*items[i + 2]))
        ready = p_queue.pop(0)
        if i + 1 < len(items):
            p_queue.append(probs(*items[i + 1], *s_queue.pop(0)))
        output(*item, *ready)


def _attention(q, kt, v, kxt, vx, sink):
    b, l, _ = q.shape
    lc = kxt.shape[2]
    nq = ATTN_Q_BLOCKS
    nblk = l // TQ
    assert nblk % nq == 0

    def blk(n, d):
        return jnp.clip(n * nq + d - 1, 0, nblk - 1)

    k_specs = [pl.BlockSpec((1, KV_W, TQ), lambda bi, n, d=d: (bi, 0, blk(n, d))) for d in range(nq + 2)]
    v_specs = [pl.BlockSpec((1, TQ, VEXT_W), lambda bi, n, d=d: (bi, blk(n, d), 0)) for d in range(nq + 2)]
    return pl.pallas_call(
        _attn_kernel,
        grid=(b, nblk // nq),
        in_specs=[pl.BlockSpec(memory_space=pltpu.SMEM),
                  pl.BlockSpec((1, nq * TQ, ATTN_W), lambda bi, n: (bi, n, 0))]
                 + k_specs + v_specs
                 + [pl.BlockSpec((1, KV_W, lc), lambda bi, n: (bi, 0, 0)),
                    pl.BlockSpec((1, lc, VEXT_W), lambda bi, n: (bi, 0, 0))],
        out_specs=pl.BlockSpec((1, nq * TQ, ATTN_W), lambda bi, n: (bi, n, 0)),
        out_shape=jax.ShapeDtypeStruct((b, l, ATTN_W), BF16),
        compiler_params=_params("arbitrary", "arbitrary"),
        name="attention",
    )(sink, q, *([kt] * (nq + 2)), *([v] * (nq + 2)), kxt, vx)


def _shift_rows(a, d):
    n = a.shape[0]
    return pltpu.roll(a, (-d) % n, 0)


def _merge_kernel(ao_ref, u_ref, up_ref, un_ref, sa_ref, sp_ref, x_ref, mod_ref,
                  wpool_ref, pscale_ref, wab_ref, wpb_ref, wout_ref, g2_ref, wr_ref, br_ref,
                  x1_ref, h2_ref, idx_ref, gate_ref, rank_ref, cnt_ref, carry_ref, *, seq_len):
    bi = pl.program_id(0)
    i = pl.program_id(1)
    last = pl.num_programs(1) - 1
    tm, sub = TM_MERGE, MERGE_SUB

    @pl.when((bi == 0) & (i == 0))
    def _():
        carry_ref[...] = jnp.zeros_like(carry_ref)

    u = u_ref[0]
    prev = jnp.where(i > 0, up_ref[0], 0.0)
    nxt = jnp.where(i < last, un_ref[0], 0.0)
    ext = jnp.concatenate([prev, u, nxt], axis=0)

    def pool_diff(r0):
        t = i * tm + r0 + lax.broadcasted_iota(jnp.int32, (sub, 1), 0)
        diffs = []
        for g, w in enumerate(POOL_WINDOWS):
            cs = slice(g * POOL_GROUP_W, (g + 1) * POOL_GROUP_W)
            e = ext[r0:r0 + sub + 2 * POOL_HALO, cs]
            acc = _shift_rows(e, -1) + e
            step = 1
            while 2 * step < w:
                acc = _shift_rows(acc, -step) + _shift_rows(acc, step)
                step *= 2
            win = acc[POOL_HALO:POOL_HALO + sub]
            half = w // 2
            cnt = (jnp.minimum(t + half, seq_len) - jnp.maximum(t - half, 0)).astype(F32)
            diffs.append((win / cnt - u[r0:r0 + sub, cs]).astype(BF16))
        return diffs

    def branches(r0, diffs):
        rs = slice(r0, r0 + sub)
        a = jnp.dot(ao_ref[0, rs, :], wab_ref[...], preferred_element_type=F32)
        mixed = [jnp.dot(df, wpool_ref[g], preferred_element_type=F32) for g, df in enumerate(diffs)]
        pool_o = jnp.concatenate(mixed, axis=1) * pscale_ref[...]
        p = jnp.dot(pool_o.astype(BF16), wpb_ref[...], preferred_element_type=F32)
        merged = sa_ref[0, rs, :].astype(F32) * a + sp_ref[0, rs, :].astype(F32) * p
        return merged.astype(BF16)

    def out_proj(merged):
        return jnp.dot(merged, wout_ref[...], preferred_element_type=F32)

    def finish(r0, o):
        rs = slice(r0, r0 + sub)
        x1 = x_ref[0, rs, :] + mod_ref[0, 2:3, :] * o
        x1_ref[0, rs, :] = x1
        h2 = _norm_mod(x1, g2_ref[...], mod_ref[0, 3:4, :], mod_ref[0, 4:5, :])
        h2_ref[rs, :] = _pack_rows(h2)
        return jnp.dot(h2.astype(BF16), wr_ref[...], preferred_element_type=F32) + br_ref[...]

    def route(r0, logits, carry):
        lt = logits.T
        eid = lax.broadcasted_iota(jnp.int32, (N_EXPERTS, sub), 0).astype(F32)
        work = lt
        vals, idxs, sels = [], [], []
        for _ in range(TOP_K):
            mx = jnp.max(work, axis=0, keepdims=True)
            ix = jnp.min(jnp.where(work == mx, eid, float(N_EXPERTS)), axis=0, keepdims=True)
            sel = eid == ix
            work = jnp.where(sel, -jnp.inf, work)
            vals.append(mx)
            idxs.append(ix)
            sels.append(sel)
        ex = [jnp.exp(v - vals[0]) for v in vals]
        tot = ex[0] + ex[1] + ex[2] + ex[3]
        pad = [jnp.zeros_like(tot)] * (SUBLANES - TOP_K)
        gates = jnp.concatenate([e_ / tot for e_ in ex] + pad, axis=0)
        gate_ref[r0:r0 + sub, :] = gates.T[:, :TOP_K]
        idx_ref[:, r0:r0 + sub] = jnp.concatenate(idxs, axis=0).astype(jnp.int32)
        member = (sels[0] | sels[1] | sels[2] | sels[3]).astype(F32)
        t_row = lax.broadcasted_iota(jnp.int32, (sub, sub), 0)
        t_col = lax.broadcasted_iota(jnp.int32, (sub, sub), 1)
        earlier = (t_row < t_col).astype(BF16)
        before = jnp.dot(member.astype(BF16), earlier, preferred_element_type=F32) + carry
        ranks = [jnp.sum(jnp.where(s, before, 0.0), axis=0, keepdims=True) for s in sels]
        rank_ref[:, r0:r0 + sub] = jnp.concatenate(ranks, axis=0).astype(jnp.int32)
        return carry + jnp.sum(member, axis=1, keepdims=True)

    starts = list(range(0, tm, sub))
    carry = carry_ref[...]
    merged = branches(starts[0], pool_diff(starts[0]))
    for si, r0 in enumerate(starts):
        o = out_proj(merged)
        if si + 1 < len(starts):
            merged = branches(starts[si + 1], pool_diff(starts[si + 1]))
        carry = route(r0, finish(r0, o), carry)
    carry_ref[...] = carry
    cnt_ref[...] = carry


def _merge_route(attn_o, u, ga, gp, x, mod, w_pool_bf, pool_scale, wab, wpb, wout,
                 norm2_g, w_router, b_router):
    b, l, d = x.shape
    tm = TM_MERGE
    nt = l // tm
    n = b * l
    hb = tm // POOL_HALO
    row = lambda bi, i: (bi, i, 0)
    flat = lambda bi, i: (bi * nt + i, 0)
    halo_prev = lambda bi, i: (bi, jnp.maximum(i * hb - 1, 0), 0)
    halo_next = lambda bi, i: (bi, jnp.minimum((i + 1) * hb, l // POOL_HALO - 1), 0)
    slot_major = lambda bi, i: (0, bi * nt + i)
    out_shape = (
        jax.ShapeDtypeStruct((b, l, d), F32),
        jax.ShapeDtypeStruct((n, ROW_WORDS), U32),
        jax.ShapeDtypeStruct((TOP_K, n), jnp.int32),
        jax.ShapeDtypeStruct((n, TOP_K), F32),
        jax.ShapeDtypeStruct((TOP_K, n), jnp.int32),
        jax.ShapeDtypeStruct((N_EXPERTS, 1), F32),
    )
    return pl.pallas_call(
        functools.partial(_merge_kernel, seq_len=l),
        grid=(b, nt),
        in_specs=[pl.BlockSpec((1, tm, d), row),
                  pl.BlockSpec((1, tm, d), row),
                  pl.BlockSpec((1, POOL_HALO, d), halo_prev),
                  pl.BlockSpec((1, POOL_HALO, d), halo_next),
                  pl.BlockSpec((1, tm, d), row),
                  pl.BlockSpec((1, tm, d), row),
                  pl.BlockSpec((1, tm, d), row),
                  pl.BlockSpec((1, 6, d), lambda bi, i: (bi, 0, 0)),
                  _const_spec(w_pool_bf.shape),
                  _const_spec((1, d)),
                  _const_spec((d, d)), _const_spec((d, d)), _const_spec((d, d)),
                  _const_spec((1, d)),
                  _const_spec((d, N_EXPERTS)),
                  _const_spec((1, N_EXPERTS))],
        out_specs=(pl.BlockSpec((1, tm, d), row),
                   pl.BlockSpec((tm, ROW_WORDS), flat),
                   pl.BlockSpec((TOP_K, tm), slot_major),
                   pl.BlockSpec((tm, TOP_K), flat),
                   pl.BlockSpec((TOP_K, tm), slot_major),
                   _const_spec((N_EXPERTS, 1))),
        out_shape=out_shape,
        scratch_shapes=[pltpu.VMEM((N_EXPERTS, 1), F32)],
        compiler_params=_params("arbitrary", "arbitrary"),
        name="merge_route",
    )(attn_o, u, u, u, ga, gp, x, mod, w_pool_bf, pool_scale, wab, wpb, wout,
      norm2_g, w_router, b_router)


def _sc_worker_id():
    return lax.axis_index("s") * SC_CORES + lax.axis_index("c")


def _sc_scatter_rows(rows, dest, pad_dest, out_rows):
    n, w = rows.shape
    n_win = n // (SC_WORKERS * SC_WINDOW)
    n_pad_win = pad_dest.shape[1]
    assert dest.shape == (TOP_K, n) and SC_WORKERS * n_win * SC_WINDOW == n
    mesh = plsc.VectorSubcoreMesh(core_axis_name="c", subcore_axis_name="s")
    zeros = jnp.zeros((SC_WINDOW, w), rows.dtype)

    @functools.partial(
        pl.kernel, mesh=mesh,
        out_type=jax.ShapeDtypeStruct((out_rows, w), rows.dtype),
        scratch_types=[pltpu.VMEM((TOP_K, SC_WINDOW), jnp.int32),
                       pltpu.VMEM((n_pad_win, SC_WINDOW), jnp.int32),
                       pltpu.VMEM((SC_WINDOW, w), rows.dtype),
                       pltpu.SemaphoreType.DMA],
        name="sc_scatter_rows",
    )
    def scatter(rows_hbm, dest_hbm, pad_hbm, zeros_hbm, out_hbm, idx_v, pad_v, rows_v, sem):
        wid = _sc_worker_id()

        @pl.loop(0, n_win)
        def _(j):
            base = (wid * n_win + j) * SC_WINDOW
            pltpu.sync_copy(rows_hbm.at[pl.ds(base, SC_WINDOW)], rows_v)
            pltpu.sync_copy(dest_hbm.at[:, pl.ds(base, SC_WINDOW)], idx_v)
            copies = [pltpu.async_copy(rows_v, out_hbm.at[idx_v.at[k]], sem) for k in range(TOP_K)]
            for cp in copies:
                cp.wait()

        pltpu.sync_copy(zeros_hbm, rows_v)
        pltpu.sync_copy(pad_hbm.at[wid], pad_v)
        copies = [pltpu.async_copy(rows_v, out_hbm.at[pad_v.at[j]], sem) for j in range(n_pad_win)]
        for cp in copies:
            cp.wait()

    return scatter(rows, dest, pad_dest, zeros)


def _expert_kernel(be_ref, first_ref, next_ref, valid_ref, nu_ref, x_ref, wgu_hbm, bgu_ref, wd_hbm,
                   bd_ref, y_ref, wgu_f, wd_f, wgu_s, wd_s, sem):
    i = pl.program_id(0)

    def weight_copies(e):
        return (pltpu.make_async_copy(wgu_hbm.at[e], wgu_f, sem.at[0]),
                pltpu.make_async_copy(wd_hbm.at[e], wd_f, sem.at[1]))

    def start_weights(e):
        for cp in weight_copies(e):
            cp.start(priority=1)

    @pl.when(i == 0)
    def _():
        start_weights(be_ref[0])

    @pl.when(i < nu_ref[0])
    def _():
        e = be_ref[i]

        @pl.when(first_ref[i] == 1)
        def _():
            for cp in weight_copies(e):
                cp.wait()
            ck = 16

            def cast_rows(c, carry):
                r = pl.ds(pl.multiple_of(c * ck, ck), ck)
                wgu_s[r, :] = wgu_f[r, :].astype(BF16)
                wd_s[r, :] = wd_f[r, :].astype(BF16)
                return carry

            lax.fori_loop(0, D_MODEL // ck, cast_rows, 0, unroll=2)

            @pl.when(next_ref[i] >= 0)
            def _():
                start_weights(next_ref[i])

        def ffn(m):
            x = _unpack_rows(x_ref[:m, :]).astype(BF16)
            bgu = bgu_ref[e]
            gt = jnp.dot(x, wgu_s[:, :D_FF], preferred_element_type=F32) + bgu[:, :D_FF]
            ln = jnp.dot(x, wgu_s[:, D_FF:], preferred_element_type=F32) + bgu[:, D_FF:]
            gt = jnp.minimum(gt, SWIGLU_LIMIT)
            ln = jnp.clip(ln, -SWIGLU_LIMIT, SWIGLU_LIMIT)
            act = gt * jax.nn.sigmoid(SWIGLU_ALPHA * gt) * (ln + 1.0)
            y = jnp.dot(act.astype(BF16), wd_s[...], preferred_element_type=F32) + bd_ref[e]
            y_ref[:m, :] = _pack_rows(y)
            if m < TM_EXPERT:
                y_ref[m:, :] = jnp.zeros((TM_EXPERT - m, ROW_WORDS), U32)

        quanta = (valid_ref[i] + EXPERT_ROW_QUANTUM - 1) // EXPERT_ROW_QUANTUM
        for q in range(1, TM_EXPERT // EXPERT_ROW_QUANTUM + 1):
            pl.when(quanta == q)(functools.partial(ffn, q * EXPERT_ROW_QUANTUM))

    @pl.when(i >= nu_ref[0])
    def _():
        y_ref[...] = jnp.zeros_like(y_ref)


def _experts(block_e, first, next_e, valid, n_used, xs, w_gu, b_gu, w_down, b_down):
    d = D_MODEL
    nb = block_e.shape[0]
    tm = TM_EXPERT

    xmap = lambda i, be, fi, nx, va, nu: (jnp.minimum(i, nu[0] - 1), 0)
    grid_spec = pltpu.PrefetchScalarGridSpec(
        num_scalar_prefetch=5,
        grid=(nb,),
        in_specs=[pl.BlockSpec((tm, ROW_WORDS), xmap),
                  pl.BlockSpec(memory_space=pl.ANY),
                  _const_spec((N_EXPERTS, 1, 2 * D_FF)),
                  pl.BlockSpec(memory_space=pl.ANY),
                  _const_spec((N_EXPERTS, 1, d))],
        out_specs=pl.BlockSpec((tm, ROW_WORDS), lambda i, be, fi, nx, va, nu: (i, 0)),
        scratch_shapes=[pltpu.VMEM((d, 2 * D_FF), F32), pltpu.VMEM((D_FF, d), F32),
                        pltpu.VMEM((d, 2 * D_FF), BF16), pltpu.VMEM((D_FF, d), BF16),
                        pltpu.SemaphoreType.DMA((2,))],
    )
    return pl.pallas_call(
        _expert_kernel,
        grid_spec=grid_spec,
        out_shape=jax.ShapeDtypeStruct((nb * tm, ROW_WORDS), U32),
        compiler_params=_params("arbitrary"),
        name="moe_experts",
    )(block_e, first, next_e, valid, n_used, xs, w_gu, b_gu.reshape(N_EXPERTS, 1, 2 * D_FF),
      w_down, b_down.reshape(N_EXPERTS, 1, d))


def _sc_gather_rows(table, idx):
    m = idx.shape[0]
    w = table.shape[1]
    win = SC_GATHER_WINDOW
    per_worker = m // SC_WORKERS
    n_win = per_worker // win
    assert per_worker * SC_WORKERS == m and n_win * win == per_worker and n_win % 2 == 0
    mesh = plsc.VectorSubcoreMesh(core_axis_name="c", subcore_axis_name="s")

    @functools.partial(
        pl.kernel, mesh=mesh,
        out_type=jax.ShapeDtypeStruct((m, w), table.dtype),
        scratch_types=[pltpu.VMEM((win,), jnp.int32), pltpu.VMEM((win,), jnp.int32),
                       pltpu.VMEM((win, w), table.dtype), pltpu.VMEM((win, w), table.dtype),
                       pltpu.SemaphoreType.DMA((4,))],
        name="sc_gather_rows",
    )
    def gather(table_hbm, idx_hbm, out_hbm, idx0, idx1, rows0, rows1, sems):
        wid = _sc_worker_id()
        idxs, rows = (idx0, idx1), (rows0, rows1)

        def out_window(j):
            return out_hbm.at[pl.ds(wid * per_worker + j * win, win)]

        def start_gather(j, slot):
            pltpu.sync_copy(idx_hbm.at[pl.ds(wid * per_worker + j * win, win)], idxs[slot])
            pltpu.async_copy(table_hbm.at[idxs[slot]], rows[slot], sems.at[slot])

        def wait_gather(slot):
            pltpu.make_async_copy(table_hbm.at[idxs[slot]], rows[slot], sems.at[slot]).wait()

        def wait_write(j, slot):
            pltpu.make_async_copy(rows[slot], out_window(j), sems.at[2 + slot]).wait()

        start_gather(0, 0)

        @pl.loop(0, n_win, step=2)
        def _(j0):
            for slot in range(2):
                j = j0 + slot
                other = 1 - slot

                @pl.when(j >= 1)
                def _():
                    wait_write(j - 1, other)

                @pl.when(j + 1 < n_win)
                def _():
                    start_gather(j + 1, other)

                wait_gather(slot)
                pltpu.async_copy(rows[slot], out_window(j), sems.at[2 + slot])

        wait_write(n_win - 1, (n_win - 1) % 2)

    return gather(table, idx)


def _tc_gather_kernel(idx_ref, table_ref, o_ref, sem):
    n = o_ref.shape[0]

    def issue(it, c):
        for dr in range(TC_GATHER_UNROLL):
            r = it * TC_GATHER_UNROLL + dr
            pltpu.make_async_copy(table_ref.at[pl.ds(idx_ref[r], 1)], o_ref.at[pl.ds(r, 1)],
                                  sem).start(priority=dr % 2)
        return c

    lax.fori_loop(0, n // TC_GATHER_UNROLL, issue, 0)
    pltpu.make_async_copy(table_ref.at[pl.ds(0, n)], o_ref, sem).wait()


def _tc_gather_rows(table, idx):
    m = idx.shape[0]
    w = table.shape[1]
    rows = TC_GATHER_ROWS
    return pl.pallas_call(
        _tc_gather_kernel,
        grid=(m // rows,),
        in_specs=[pl.BlockSpec((rows,), lambda i: (i,), memory_space=pltpu.SMEM),
                  pl.BlockSpec(memory_space=pl.ANY)],
        out_specs=pl.BlockSpec((rows, w), lambda i: (i, 0)),
        out_shape=jax.ShapeDtypeStruct((m, w), table.dtype),
        scratch_shapes=[pltpu.SemaphoreType.DMA],
        compiler_params=_params("arbitrary"),
        name="tc_gather_rows",
    )(idx, table)


def _combine_dense_kernel(y3_ref, y1_ref, gate_ref, x1_ref, mod_ref, fg_ref, o_ref):
    gate = gate_ref[...]
    y = gate[:, TOP_K - 1:TOP_K] * _unpack_rows(y1_ref[...])
    for k in range(TOP_K - 1):
        y = y + gate[:, k:k + 1] * _unpack_rows(y3_ref[k])
    x2 = x1_ref[...] + mod_ref[0, 5:6, :] * y
    o_ref[...] = x2 * lax.rsqrt(jnp.mean(x2 * x2, axis=-1, keepdims=True) + NORM_EPS) * fg_ref[...]


def _combine_dense(y3, y1, gate_w, x1, mod, final_g):
    b, l, d = x1.shape
    ts = TM_COMBINE
    nt = l // ts
    return pl.pallas_call(
        _combine_dense_kernel,
        grid=(b * nt,),
        in_specs=[pl.BlockSpec((TOP_K - 1, ts, ROW_WORDS), lambda s: (0, s, 0)),
                  pl.BlockSpec((ts, ROW_WORDS), lambda s: (s, 0)),
                  pl.BlockSpec((ts, TOP_K), lambda s: (s, 0)),
                  pl.BlockSpec((ts, d), lambda s: (s, 0)),
                  pl.BlockSpec((1, 6, d), lambda s: (s // nt, 0, 0)),
                  _const_spec((1, d))],
        out_specs=pl.BlockSpec((ts, d), lambda s: (s, 0)),
        out_shape=jax.ShapeDtypeStruct((b * l, d), F32),
        compiler_params=_params("arbitrary"),
        name="moe_combine",
    )(y3, y1, gate_w, x1.reshape(b * l, d), mod, final_g).reshape(b, l, d)


def _moe_expert_rows(h2, top_idx, rank, counts, w_gu, b_gu, w_down, b_down):
    n = h2.shape[0]
    tm = TM_EXPERT
    nb = n * TOP_K // tm + N_EXPERTS
    cnt = counts[:, 0].astype(jnp.int32)
    padded = (cnt + tm - 1) // tm * tm
    pend = jnp.cumsum(padded)
    pstart = pend - padded
    expert_ids = jnp.arange(N_EXPERTS, dtype=jnp.int32)
    dest = jnp.sum(jnp.where(top_idx[..., None] == expert_ids, pstart, 0), axis=-1) + rank
    n_used = (pend[-1] // tm).reshape(1)
    block_start = jnp.arange(nb, dtype=jnp.int32) * tm
    block_e = jnp.minimum(jnp.sum((pend[None, :] <= block_start[:, None]).astype(jnp.int32), axis=1),
                          N_EXPERTS - 1)
    first = jnp.concatenate([jnp.ones((1,), jnp.int32),
                             (block_e[1:] != block_e[:-1]).astype(jnp.int32)])
    later_used = (expert_ids[None, :] > expert_ids[:, None]) & (cnt[None, :] > 0)
    next_of_expert = jnp.min(jnp.where(later_used, expert_ids[None, :], N_EXPERTS), axis=1)
    next_of_expert = jnp.where(next_of_expert < N_EXPERTS, next_of_expert, -1)
    next_e = jnp.sum(jnp.where(block_e[:, None] == expert_ids, next_of_expert, 0), axis=-1)
    used_end = jnp.sum(jnp.where(block_e[:, None] == expert_ids, pstart + cnt, 0), axis=-1)
    valid = jnp.clip(used_end - block_start, 0, tm)

    qm = EXPERT_ROW_QUANTUM
    pad_j = jnp.arange(qm, dtype=jnp.int32)[None, :]
    n_pad = (cnt + qm - 1) // qm * qm - cnt
    spare = nb * tm + expert_ids[:, None] * qm + pad_j
    pad_dest = jnp.where(pad_j < n_pad[:, None], (pstart + cnt)[:, None] + pad_j, spare)
    pad_dest = pad_dest.reshape(SC_WORKERS, N_EXPERTS * qm // (SC_WORKERS * SC_WINDOW), SC_WINDOW)

    xs = _sc_scatter_rows(h2, dest, pad_dest, nb * tm + N_EXPERTS * qm)
    ys = _experts(block_e, first, next_e, valid, n_used, xs, w_gu, b_gu, w_down, b_down)
    y3 = _sc_gather_rows(ys, dest[:TOP_K - 1].reshape(-1)).reshape(TOP_K - 1, n, ROW_WORDS)
    y1 = _tc_gather_rows(ys, dest[TOP_K - 1])
    return y3, y1


def _rope_tables(seq_len):
    inv_freq = ROPE_BASE ** (-jnp.arange(ROPE_PAIRS, dtype=F32) / ROPE_PAIRS)
    reps = LANES // HEAD_DIM
    out = []
    for n_pos, on_row in ((seq_len // GRID_W, True), (GRID_W, False)):
        ang = jnp.arange(n_pos, dtype=F32)[:, None] * inv_freq
        zero = jnp.zeros_like(ang)
        for fn, sign in ((jnp.cos, 1.0), (jnp.sin, -1.0)):
            v = fn(ang)
            head = ([sign * v, v, zero, zero] if on_row else [zero, zero, sign * v, v])
            out.append(jnp.concatenate(head * reps, axis=1))
    return out


def kernel(x, c, ctx, c_ctx, w_ada, b_ada, norm1_g, norm2_g, w_in, b_in, attn_sink, w_pool,
           pool_scale, w_attn_br, w_pool_br, w_out, w_router, b_router, w_gu, b_gu, w_down,
           b_down, final_g):
    b, l, d = x.shape
    assert w_ada.shape[0] == 1, "single-layer block"

    cond = jnp.zeros((SUBLANES, d), F32).at[:b].set(c).at[b].set(c_ctx)
    mod = _adaln(cond, w_ada[0], b_ada[0])[:b + 1].reshape(b + 1, 6, d)
    mod_x, mod_c = mod[:b], mod[b:b + 1]

    rope = _rope_tables(l)
    w_in_bf = w_in[0].astype(BF16)
    b_in2 = b_in[0].reshape(1, IN_W)
    g1 = norm1_g[0].reshape(1, d)
    q, kt, v, u, ga, gp = _inproj(x, mod_x, g1, w_in_bf, b_in2, rope)
    kv_sl = slice(ATTN_W, ATTN_W + 2 * KV_W)
    kxt, vx = _ctx_kv(ctx, mod_c, g1, w_in_bf[:, kv_sl], b_in2[:, kv_sl])

    attn_o = _attention(q, kt, v, kxt, vx, attn_sink[0])

    x1, h2, top_idx, gate_w, rank, counts = _merge_route(
        attn_o, u, ga, gp, x, mod_x, w_pool[0].astype(BF16), pool_scale[0].reshape(1, d),
        w_attn_br[0].astype(BF16), w_pool_br[0].astype(BF16), w_out[0].astype(BF16),
        norm2_g[0].reshape(1, d), w_router[0].astype(BF16), b_router[0].reshape(1, N_EXPERTS))

    y3, y1 = _moe_expert_rows(h2, top_idx, rank, counts, w_gu[0], b_gu[0], w_down[0], b_down[0])
    return _combine_dense(y3, y1, gate_w, x1, mod_x, final_g.reshape(1, d))
```

```python
import functools

import jax
import jax.numpy as jnp
from jax import lax
from jax.experimental import pallas as pl
from jax.experimental.pallas import tpu as pltpu
from jax.experimental.pallas import tpu_sc as plsc

D_MODEL = 1024
GRID_W = 64
HEAD_DIM = 64
N_HEADS = 16
N_KV_HEADS = 4
GROUP = N_HEADS // N_KV_HEADS
ATTN_W = N_HEADS * HEAD_DIM
KV_W = N_KV_HEADS * HEAD_DIM
WINDOW = 128
ATTN_SCALE = HEAD_DIM ** -0.5
ROPE_BASE = 10000.0
ROPE_PAIRS = HEAD_DIM // 4
POOL_WINDOWS = (2, 4, 8, 16)
POOL_GROUP_W = D_MODEL // len(POOL_WINDOWS)
IN_W = ATTN_W + 2 * KV_W + D_MODEL + 2 * D_MODEL
VEXT_W = N_KV_HEADS * 2 * HEAD_DIM
N_EXPERTS = 32
TOP_K = 4
D_FF = D_MODEL
SWIGLU_ALPHA = 1.702
SWIGLU_LIMIT = 7.0
NORM_EPS = 1e-5
NEG_INF = -1e30

LANES = 128
SUBLANES = 8
VMEM_LIMIT = 56 * 1024 * 1024

SC_CORES = 2
SC_SUBCORES = 16
SC_WORKERS = SC_CORES * SC_SUBCORES
SC_WINDOW = 128
SC_GATHER_WINDOW = 64

TM_INPROJ = 1024
INPROJ_SUB = 512
TQ = WINDOW
ATTN_Q_BLOCKS = 4
TM_MERGE = 1024
MERGE_SUB = 256
TM_EXPERT = 512
EXPERT_ROW_QUANTUM = 128
TM_COMBINE = 1024
COMBINE_CHUNKS = 2
POOL_HALO = SUBLANES

F32 = jnp.float32
BF16 = jnp.bfloat16
U32 = jnp.uint32
ROW_WORDS = D_MODEL // 2


def _pack_rows(value):
    bits = lax.bitcast_convert_type(value.astype(BF16).astype(F32), U32)
    return bits[:, :ROW_WORDS] | (bits[:, ROW_WORDS:] >> 16)


def _unpack_rows(words):
    hi = lax.bitcast_convert_type(words & jnp.uint32(0xFFFF0000), F32)
    lo = lax.bitcast_convert_type(words << 16, F32)
    return jnp.concatenate([hi, lo], axis=1)


def _params(*sem):
    return pltpu.CompilerParams(dimension_semantics=sem, vmem_limit_bytes=VMEM_LIMIT)


def _const_spec(shape):
    nd = len(shape)
    return pl.BlockSpec(shape, lambda *_: (0,) * nd, pipeline_mode=pl.Buffered(1))


def _adaln_kernel(c_ref, w_ref, b_ref, o_ref):
    c = c_ref[...]
    s = c * jax.nn.sigmoid(c)
    o_ref[...] = jnp.dot(s.astype(BF16), w_ref[...].astype(BF16),
                         preferred_element_type=F32) + b_ref[...]


def _adaln(cond, w_ada, b_ada):
    rows, d = cond.shape
    n = w_ada.shape[1]
    tn = 1024
    return pl.pallas_call(
        _adaln_kernel,
        grid=(n // tn,),
        in_specs=[_const_spec((rows, d)),
                  pl.BlockSpec((d, tn), lambda j: (0, j)),
                  pl.BlockSpec((1, tn), lambda j: (0, j))],
        out_specs=pl.BlockSpec((rows, tn), lambda j: (0, j)),
        out_shape=jax.ShapeDtypeStruct((rows, n), F32),
        compiler_params=_params("arbitrary"),
        name="adaln",
    )(cond, w_ada, b_ada.reshape(1, n))


def _norm_mod(x, g, shift, scale):
    y = x * lax.rsqrt(jnp.mean(x * x, axis=-1, keepdims=True) + NORM_EPS) * g
    return y * (1.0 + scale) + shift


def _rope(t, cos, sin_signed):
    w = t.shape[1]
    half = ROPE_PAIRS
    lane = lax.broadcasted_iota(jnp.int32, t.shape, 1)
    first = (lane & (2 * half - 1)) < half
    rot = jnp.where(first, pltpu.roll(t, w - half, 1), pltpu.roll(t, half, 1))
    reps = w // LANES
    cos_w = jnp.concatenate([cos] * reps, axis=1) if reps > 1 else cos
    sin_w = jnp.concatenate([sin_signed] * reps, axis=1) if reps > 1 else sin_signed
    return t * cos_w + rot * sin_w


def _widen_values(v):
    ones = jnp.ones((v.shape[0], HEAD_DIM), BF16)
    vb = v.astype(BF16)
    parts = []
    for j in range(N_KV_HEADS):
        parts += [vb[:, j * HEAD_DIM:(j + 1) * HEAD_DIM], ones]
    return jnp.concatenate(parts, axis=1)


def _token_table(row_ref, col_ref):
    gr = row_ref.shape[0]
    shape = (gr, GRID_W, LANES)
    full = (jnp.broadcast_to(row_ref[...][:, None, :], shape)
            + jnp.broadcast_to(col_ref[...][None, :, :], shape))
    return full.reshape(gr * GRID_W, LANES)


def _inproj_kernel(x_ref, mod_ref, g_ref, w_ref, b_ref, cr_ref, sr_ref, cc_ref, sc_ref,
                   q_ref, k_ref, v_ref, u_ref, ga_ref, gp_ref):
    sub = INPROJ_SUB
    cos_all = _token_table(cr_ref, cc_ref)
    sin_all = _token_table(sr_ref, sc_ref)
    cw = 512

    def work_items(r0):
        rs = slice(r0, r0 + sub)
        cos, sin = cos_all[rs], sin_all[rs]
        hb = _norm_mod(x_ref[0, rs, :], g_ref[...], mod_ref[0, 0:1, :], mod_ref[0, 1:2, :]).astype(BF16)

        def put_k(t):
            k_ref[0, :, rs] = _rope(t, cos, sin).T.astype(BF16)

        def put_v(t):
            v_ref[0, rs, :] = _widen_values(t)

        items = []
        for c0 in range(0, ATTN_W, cw):
            def put_q(t, c0=c0):
                q_ref[0, rs, c0:c0 + cw] = (_rope(t, cos, sin) * ATTN_SCALE).astype(BF16)
            items.append((c0, cw, put_q))
        items.append((ATTN_W, KV_W, put_k))
        items.append((ATTN_W + KV_W, KV_W, put_v))
        base = ATTN_W + 2 * KV_W
        for c0 in range(0, D_MODEL, cw):
            def put_u(t, c0=c0):
                u_ref[0, rs, c0:c0 + cw] = t
            items.append((base + c0, cw, put_u))
        for ref in (ga_ref, gp_ref):
            base += D_MODEL
            for c0 in range(0, D_MODEL, cw):
                def put_gate(t, ref=ref, c0=c0):
                    ref[0, rs, c0:c0 + cw] = jax.nn.sigmoid(t).astype(BF16)
                items.append((base + c0, cw, put_gate))
        return [(hb, c0, w, fn) for c0, w, fn in items]

    def proj(hb, c0, w):
        return jnp.dot(hb, w_ref[:, c0:c0 + w], preferred_element_type=F32) + b_ref[:, c0:c0 + w]

    items = [it for r0 in range(0, x_ref.shape[1], sub) for it in work_items(r0)]
    pending = proj(*items[0][:3])
    for ci, item in enumerate(items):
        ready = pending
        if ci + 1 < len(items):
            pending = proj(*items[ci + 1][:3])
        item[3](ready)


def _inproj(x, mod, norm_g, w_in_bf, b_in, rope):
    b, l, d = x.shape
    tm = TM_INPROJ
    row = lambda bi, i: (bi, i, 0)
    out_shape = (
        jax.ShapeDtypeStruct((b, l, ATTN_W), BF16),
        jax.ShapeDtypeStruct((b, KV_W, l), BF16),
        jax.ShapeDtypeStruct((b, l, VEXT_W), BF16),
        jax.ShapeDtypeStruct((b, l, D_MODEL), F32),
        jax.ShapeDtypeStruct((b, l, D_MODEL), BF16),
        jax.ShapeDtypeStruct((b, l, D_MODEL), BF16),
    )
    return pl.pallas_call(
        _inproj_kernel,
        grid=(b, l // tm),
        in_specs=[pl.BlockSpec((1, tm, d), row),
                  pl.BlockSpec((1, 6, d), lambda bi, i: (bi, 0, 0)),
                  _const_spec((1, d)),
                  _const_spec((d, IN_W)),
                  _const_spec((1, IN_W)),
                  pl.BlockSpec((tm // GRID_W, LANES), lambda bi, i: (i, 0)),
                  pl.BlockSpec((tm // GRID_W, LANES), lambda bi, i: (i, 0)),
                  _const_spec((GRID_W, LANES)),
                  _const_spec((GRID_W, LANES))],
        out_specs=(pl.BlockSpec((1, tm, ATTN_W), row),
                   pl.BlockSpec((1, KV_W, tm), lambda bi, i: (bi, 0, i)),
                   pl.BlockSpec((1, tm, VEXT_W), row),
                   pl.BlockSpec((1, tm, D_MODEL), row),
                   pl.BlockSpec((1, tm, D_MODEL), row),
                   pl.BlockSpec((1, tm, D_MODEL), row)),
        out_shape=out_shape,
        compiler_params=_params("arbitrary", "arbitrary"),
        name="inproj",
    )(x, mod, norm_g, w_in_bf, b_in, *rope)


def _ctx_kv_kernel(x_ref, mod_ref, g_ref, w_ref, b_ref, k_ref, v_ref):
    h = _norm_mod(x_ref[0], g_ref[...], mod_ref[0, 0:1, :], mod_ref[0, 1:2, :])
    kv = jnp.dot(h.astype(BF16), w_ref[...], preferred_element_type=F32) + b_ref[...]
    k_ref[0] = kv[:, :KV_W].T.astype(BF16)
    v_ref[0] = _widen_values(kv[:, KV_W:])


def _ctx_kv(ctx, mod_c, norm_g, w_kv_bf, b_kv):
    b, lc, d = ctx.shape
    row = lambda bi: (bi, 0, 0)
    return pl.pallas_call(
        _ctx_kv_kernel,
        grid=(b,),
        in_specs=[pl.BlockSpec((1, lc, d), row),
                  _const_spec((1, 6, d)),
                  _const_spec((1, d)),
                  _const_spec((d, 2 * KV_W)),
                  _const_spec((1, 2 * KV_W))],
        out_specs=(pl.BlockSpec((1, KV_W, lc), row), pl.BlockSpec((1, lc, VEXT_W), row)),
        out_shape=(jax.ShapeDtypeStruct((b, KV_W, lc), BF16),
                   jax.ShapeDtypeStruct((b, lc, VEXT_W), BF16)),
        compiler_params=_params("arbitrary"),
        name="ctx_kv",
    )(ctx, mod_c, norm_g, w_kv_bf, b_kv)


def _attn_kernel(sink_ref, q_ref, *refs):
    nq = ATTN_Q_BLOCKS
    k_refs, v_refs = refs[:nq + 2], refs[nq + 2:2 * nq + 4]
    kx_ref, vx_ref, o_ref = refs[2 * nq + 4:]
    first_blk = pl.program_id(1) * nq
    last_blk = pl.num_programs(1) * nq - 1
    rows = GROUP * TQ
    pair_w = 2 * HEAD_DIM
    qi = lax.broadcasted_iota(jnp.int32, (rows, TQ), 0) & (TQ - 1)
    col = lax.broadcasted_iota(jnp.int32, (rows, TQ), 1)
    low_half = lax.broadcasted_iota(jnp.int32, (TQ, pair_w), 1) < HEAD_DIM
    row_id = lax.broadcasted_iota(jnp.int32, (rows, 1), 0)

    def scores(b, j):
        qrows = slice(b * TQ, (b + 1) * TQ)
        parts = []
        for g in range(GROUP):
            h = j * GROUP + g
            pair = q_ref[0, qrows, (h // 2) * pair_w:(h // 2 + 1) * pair_w]
            parts.append(jnp.where(low_half if h % 2 == 0 else ~low_half, pair, jnp.zeros_like(pair)))
        q4 = jnp.concatenate(parts, axis=0)
        ks = slice(j * HEAD_DIM, (j + 1) * HEAD_DIM)
        kt = jnp.concatenate([k_refs[b + d][0, ks, :] for d in range(3)], axis=1)
        s_loc = jnp.dot(q4, jnp.concatenate([kt, kt], axis=0), preferred_element_type=F32)
        kxt = kx_ref[0, ks, :]
        s_ctx = jnp.dot(q4, jnp.concatenate([kxt, kxt], axis=0), preferred_element_type=F32)
        return s_loc, s_ctx

    def probs(b, j, s_loc, s_ctx):
        keep_prev = (col >= qi) & (first_blk + b > 0)
        keep_next = (col <= qi) & (first_blk + b < last_blk)
        pieces = [jnp.where(keep_prev, s_loc[:, :TQ], NEG_INF), s_loc[:, TQ:2 * TQ],
                  jnp.where(keep_next, s_loc[:, 2 * TQ:], NEG_INF)]
        pieces += [s_ctx[:, c0:c0 + TQ] for c0 in range(0, s_ctx.shape[1], TQ)]
        sink = jnp.zeros((rows, 1), F32)
        for g in range(GROUP):
            sink = jnp.where(row_id // TQ == g, sink_ref[j * GROUP + g], sink)
        mx = pieces[0]
        for pc in pieces[1:]:
            mx = jnp.maximum(mx, pc)
        m = jnp.maximum(jnp.max(mx, axis=-1, keepdims=True), sink)
        p = jnp.concatenate([jnp.exp(pc - m).astype(BF16) for pc in pieces], axis=1)
        return p, jnp.exp(sink - m)

    def output(b, j, p, sink_p):
        vs = slice(j * pair_w, (j + 1) * pair_w)
        v_all = jnp.concatenate([v_refs[b + d][0, :, vs] for d in range(3)]
                                + [vx_ref[0, :, vs]], axis=0)
        r = jnp.dot(p, v_all, preferred_element_type=F32)
        den = pltpu.roll(r, HEAD_DIM, 1) + sink_p
        o = r / den
        for g in range(0, GROUP, 2):
            even = o[g * TQ:(g + 1) * TQ]
            odd = pltpu.roll(o[(g + 1) * TQ:(g + 2) * TQ], HEAD_DIM, 1)
            c0 = (j * GROUP + g) * HEAD_DIM
            o_ref[0, b * TQ:(b + 1) * TQ, c0:c0 + pair_w] = jnp.where(low_half, even, odd).astype(BF16)

    items = [(b, j) for b in range(nq) for j in range(N_KV_HEADS)]
    s_queue = [scores(*items[0]), scores(*items[1])]
    p_queue = [probs(*items[0], *s_queue.pop(0))]
    for i, item in enumerate(items):
        if i + 2 < len(items):
            s_queue.append(scores(*items[i + 2]))
        ready = p_queue.pop(0)
        if i + 1 < len(items):
            p_queue.append(probs(*items[i + 1], *s_queue.pop(0)))
        output(*item, *ready)


def _attention(q, kt, v, kxt, vx, sink):
    b, l, _ = q.shape
    lc = kxt.shape[2]
    nq = ATTN_Q_BLOCKS
    nblk = l // TQ
    assert nblk % nq == 0

    def blk(n, d):
        return jnp.clip(n * nq + d - 1, 0, nblk - 1)

    k_specs = [pl.BlockSpec((1, KV_W, TQ), lambda bi, n, d=d: (bi, 0, blk(n, d))) for d in range(nq + 2)]
    v_specs = [pl.BlockSpec((1, TQ, VEXT_W), lambda bi, n, d=d: (bi, blk(n, d), 0)) for d in range(nq + 2)]
    return pl.pallas_call(
        _attn_kernel,
        grid=(b, nblk // nq),
        in_specs=[pl.BlockSpec(memory_space=pltpu.SMEM),
                  pl.BlockSpec((1, nq * TQ, ATTN_W), lambda bi, n: (bi, n, 0))]
                 + k_specs + v_specs
                 + [pl.BlockSpec((1, KV_W, lc), lambda bi, n: (bi, 0, 0)),
                    pl.BlockSpec((1, lc, VEXT_W), lambda bi, n: (bi, 0, 0))],
        out_specs=pl.BlockSpec((1, nq * TQ, ATTN_W), lambda bi, n: (bi, n, 0)),
        out_shape=jax.ShapeDtypeStruct((b, l, ATTN_W), BF16),
        compiler_params=_params("arbitrary", "arbitrary"),
        name="attention",
    )(sink, q, *([kt] * (nq + 2)), *([v] * (nq + 2)), kxt, vx)


def _shift_rows(a, d):
    n = a.shape[0]
    return pltpu.roll(a, (-d) % n, 0)


def _merge_kernel(ao_ref, u_ref, up_ref, un_ref, sa_ref, sp_ref, x_ref, mod_ref,
                  wpool_ref, pscale_ref, wab_ref, wpb_ref, wout_ref, g2_ref, wr_ref, br_ref,
                  x1_ref, h2_ref, idx_ref, gate_ref, rank_ref, cnt_ref, carry_ref, *, seq_len):
    bi = pl.program_id(0)
    i = pl.program_id(1)
    last = pl.num_programs(1) - 1
    tm, sub = TM_MERGE, MERGE_SUB

    @pl.when((bi == 0) & (i == 0))
    def _():
        carry_ref[...] = jnp.zeros_like(carry_ref)

    u = u_ref[0]
    prev = jnp.where(i > 0, up_ref[0], 0.0)
    nxt = jnp.where(i < last, un_ref[0], 0.0)
    ext = jnp.concatenate([prev, u, nxt], axis=0)

    def pool_diff(r0):
        t = i * tm + r0 + lax.broadcasted_iota(jnp.int32, (sub, 1), 0)
        diffs = []
        for g, w in enumerate(POOL_WINDOWS):
            cs = slice(g * POOL_GROUP_W, (g + 1) * POOL_GROUP_W)
            e = ext[r0:r0 + sub + 2 * POOL_HALO, cs]
            acc = _shift_rows(e, -1) + e
            step = 1
            while 2 * step < w:
                acc = _shift_rows(acc, -step) + _shift_rows(acc, step)
                step *= 2
            win = acc[POOL_HALO:POOL_HALO + sub]
            half = w // 2
            cnt = (jnp.minimum(t + half, seq_len) - jnp.maximum(t - half, 0)).astype(F32)
            diffs.append((win / cnt - u[r0:r0 + sub, cs]).astype(BF16))
        return diffs

    def branches(r0, diffs):
        rs = slice(r0, r0 + sub)
        a = jnp.dot(ao_ref[0, rs, :], wab_ref[...], preferred_element_type=F32)
        mixed = [jnp.dot(df, wpool_ref[g], preferred_element_type=F32) for g, df in enumerate(diffs)]
        pool_o = jnp.concatenate(mixed, axis=1) * pscale_ref[...]
        p = jnp.dot(pool_o.astype(BF16), wpb_ref[...], preferred_element_type=F32)
        merged = sa_ref[0, rs, :].astype(F32) * a + sp_ref[0, rs, :].astype(F32) * p
        return merged.astype(BF16)

    def out_proj(merged):
        return jnp.dot(merged, wout_ref[...], preferred_element_type=F32)

    def finish(r0, o):
        rs = slice(r0, r0 + sub)
        x1 = x_ref[0, rs, :] + mod_ref[0, 2:3, :] * o
        x1_ref[0, rs, :] = x1
        h2 = _norm_mod(x1, g2_ref[...], mod_ref[0, 3:4, :], mod_ref[0, 4:5, :])
        h2_ref[rs, :] = _pack_rows(h2)
        return jnp.dot(h2.astype(BF16), wr_ref[...], preferred_element_type=F32) + br_ref[...]

    def route(r0, logits, carry):
        lt = logits.T
        eid = lax.broadcasted_iota(jnp.int32, (N_EXPERTS, sub), 0).astype(F32)
        work = lt
        vals, idxs, sels = [], [], []
        for _ in range(TOP_K):
            mx = jnp.max(work, axis=0, keepdims=True)
            ix = jnp.min(jnp.where(work == mx, eid, float(N_EXPERTS)), axis=0, keepdims=True)
            sel = eid == ix
            work = jnp.where(sel, -jnp.inf, work)
            vals.append(mx)
            idxs.append(ix)
            sels.append(sel)
        ex = [jnp.exp(v - vals[0]) for v in vals]
        tot = ex[0] + ex[1] + ex[2] + ex[3]
        pad = [jnp.zeros_like(tot)] * (SUBLANES - TOP_K)
        gates = jnp.concatenate([e_ / tot for e_ in ex] + pad, axis=0)
        gate_ref[r0:r0 + sub, :] = gates.T[:, :TOP_K]
        idx_ref[:, r0:r0 + sub] = jnp.concatenate(idxs, axis=0).astype(jnp.int32)
        member = (sels[0] | sels[1] | sels[2] | sels[3]).astype(F32)
        t_row = lax.broadcasted_iota(jnp.int32, (sub, sub), 0)
        t_col = lax.broadcasted_iota(jnp.int32, (sub, sub), 1)
        earlier = (t_row < t_col).astype(BF16)
        before = jnp.dot(member.astype(BF16), earlier, preferred_element_type=F32) + carry
        ranks = [jnp.sum(jnp.where(s, before, 0.0), axis=0, keepdims=True) for s in sels]
        rank_ref[:, r0:r0 + sub] = jnp.concatenate(ranks, axis=0).astype(jnp.int32)
        return carry + jnp.sum(member, axis=1, keepdims=True)

    starts = list(range(0, tm, sub))
    carry = carry_ref[...]
    merged = branches(starts[0], pool_diff(starts[0]))
    for si, r0 in enumerate(starts):
        o = out_proj(merged)
        if si + 1 < len(starts):
            merged = branches(starts[si + 1], pool_diff(starts[si + 1]))
        carry = route(r0, finish(r0, o), carry)
    carry_ref[...] = carry
    cnt_ref[...] = carry


def _merge_route(attn_o, u, ga, gp, x, mod, w_pool_bf, pool_scale, wab, wpb, wout,
                 norm2_g, w_router, b_router):
    b, l, d = x.shape
    tm = TM_MERGE
    nt = l // tm
    n = b * l
    hb = tm // POOL_HALO
    row = lambda bi, i: (bi, i, 0)
    flat = lambda bi, i: (bi * nt + i, 0)
    halo_prev = lambda bi, i: (bi, jnp.maximum(i * hb - 1, 0), 0)
    halo_next = lambda bi, i: (bi, jnp.minimum((i + 1) * hb, l // POOL_HALO - 1), 0)
    slot_major = lambda bi, i: (0, bi * nt + i)
    out_shape = (
        jax.ShapeDtypeStruct((b, l, d), F32),
        jax.ShapeDtypeStruct((n, ROW_WORDS), U32),
        jax.ShapeDtypeStruct((TOP_K, n), jnp.int32),
        jax.ShapeDtypeStruct((n, TOP_K), F32),
        jax.ShapeDtypeStruct((TOP_K, n), jnp.int32),
        jax.ShapeDtypeStruct((N_EXPERTS, 1), F32),
    )
    return pl.pallas_call(
        functools.partial(_merge_kernel, seq_len=l),
        grid=(b, nt),
        in_specs=[pl.BlockSpec((1, tm, d), row),
                  pl.BlockSpec((1, tm, d), row),
                  pl.BlockSpec((1, POOL_HALO, d), halo_prev),
                  pl.BlockSpec((1, POOL_HALO, d), halo_next),
                  pl.BlockSpec((1, tm, d), row),
                  pl.BlockSpec((1, tm, d), row),
                  pl.BlockSpec((1, tm, d), row),
                  pl.BlockSpec((1, 6, d), lambda bi, i: (bi, 0, 0)),
                  _const_spec(w_pool_bf.shape),
                  _const_spec((1, d)),
                  _const_spec((d, d)), _const_spec((d, d)), _const_spec((d, d)),
                  _const_spec((1, d)),
                  _const_spec((d, N_EXPERTS)),
                  _const_spec((1, N_EXPERTS))],
        out_specs=(pl.BlockSpec((1, tm, d), row),
                   pl.BlockSpec((tm, ROW_WORDS), flat),
                   pl.BlockSpec((TOP_K, tm), slot_major),
                   pl.BlockSpec((tm, TOP_K), flat),
                   pl.BlockSpec((TOP_K, tm), slot_major),
                   _const_spec((N_EXPERTS, 1))),
        out_shape=out_shape,
        scratch_shapes=[pltpu.VMEM((N_EXPERTS, 1), F32)],
        compiler_params=_params("arbitrary", "arbitrary"),
        name="merge_route",
    )(attn_o, u, u, u, ga, gp, x, mod, w_pool_bf, pool_scale, wab, wpb, wout,
      norm2_g, w_router, b_router)


def _sc_worker_id():
    return lax.axis_index("s") * SC_CORES + lax.axis_index("c")


def _sc_scatter_rows(rows, dest, pad_dest, out_rows):
    n, w = rows.shape
    n_win = n // (SC_WORKERS * SC_WINDOW)
    n_pad_win = pad_dest.shape[1]
    assert dest.shape == (TOP_K, n) and SC_WORKERS * n_win * SC_WINDOW == n
    mesh = plsc.VectorSubcoreMesh(core_axis_name="c", subcore_axis_name="s")
    zeros = jnp.zeros((SC_WINDOW, w), rows.dtype)

    @functools.partial(
        pl.kernel, mesh=mesh,
        out_type=jax.ShapeDtypeStruct((out_rows, w), rows.dtype),
        scratch_types=[pltpu.VMEM((TOP_K, SC_WINDOW), jnp.int32),
                       pltpu.VMEM((n_pad_win, SC_WINDOW), jnp.int32),
                       pltpu.VMEM((SC_WINDOW, w), rows.dtype),
                       pltpu.SemaphoreType.DMA],
        name="sc_scatter_rows",
    )
    def scatter(rows_hbm, dest_hbm, pad_hbm, zeros_hbm, out_hbm, idx_v, pad_v, rows_v, sem):
        wid = _sc_worker_id()

        @pl.loop(0, n_win)
        def _(j):
            base = (wid * n_win + j) * SC_WINDOW
            pltpu.sync_copy(rows_hbm.at[pl.ds(base, SC_WINDOW)], rows_v)
            pltpu.sync_copy(dest_hbm.at[:, pl.ds(base, SC_WINDOW)], idx_v)
            copies = [pltpu.async_copy(rows_v, out_hbm.at[idx_v.at[k]], sem) for k in range(TOP_K)]
            for cp in copies:
                cp.wait()

        pltpu.sync_copy(zeros_hbm, rows_v)
        pltpu.sync_copy(pad_hbm.at[wid], pad_v)
        copies = [pltpu.async_copy(rows_v, out_hbm.at[pad_v.at[j]], sem) for j in range(n_pad_win)]
        for cp in copies:
            cp.wait()

    return scatter(rows, dest, pad_dest, zeros)


def _expert_kernel(be_ref, first_ref, next_ref, valid_ref, nu_ref, x_ref, wgu_hbm, bgu_ref, wd_hbm,
                   bd_ref, y_ref, wgu_f, wd_f, wgu_s, wd_s, sem):
    i = pl.program_id(0)

    def weight_copies(e):
        return (pltpu.make_async_copy(wgu_hbm.at[e], wgu_f, sem.at[0]),
                pltpu.make_async_copy(wd_hbm.at[e], wd_f, sem.at[1]))

    def start_weights(e):
        for cp in weight_copies(e):
            cp.start(priority=1)

    @pl.when(i == 0)
    def _():
        start_weights(be_ref[0])

    @pl.when(i < nu_ref[0])
    def _():
        e = be_ref[i]

        @pl.when(first_ref[i] == 1)
        def _():
            for cp in weight_copies(e):
                cp.wait()
            ck = 16

            def cast_rows(c, carry):
                r = pl.ds(pl.multiple_of(c * ck, ck), ck)
                wgu_s[r, :] = wgu_f[r, :].astype(BF16)
                wd_s[r, :] = wd_f[r, :].astype(BF16)
                return carry

            lax.fori_loop(0, D_MODEL // ck, cast_rows, 0, unroll=2)

            @pl.when(next_ref[i] >= 0)
            def _():
                start_weights(next_ref[i])

        def ffn(m):
            x = _unpack_rows(x_ref[:m, :]).astype(BF16)
            bgu = bgu_ref[e]
            gt = jnp.dot(x, wgu_s[:, :D_FF], preferred_element_type=F32) + bgu[:, :D_FF]
            ln = jnp.dot(x, wgu_s[:, D_FF:], preferred_element_type=F32) + bgu[:, D_FF:]
            gt = jnp.minimum(gt, SWIGLU_LIMIT)
            ln = jnp.clip(ln, -SWIGLU_LIMIT, SWIGLU_LIMIT)
            act = gt * jax.nn.sigmoid(SWIGLU_ALPHA * gt) * (ln + 1.0)
            y = jnp.dot(act.astype(BF16), wd_s[...], preferred_element_type=F32) + bd_ref[e]
            y_ref[:m, :] = _pack_rows(y)
            if m < TM_EXPERT:
                y_ref[m:, :] = jnp.zeros((TM_EXPERT - m, ROW_WORDS), U32)

        quanta = (valid_ref[i] + EXPERT_ROW_QUANTUM - 1) // EXPERT_ROW_QUANTUM
        for q in range(1, TM_EXPERT // EXPERT_ROW_QUANTUM + 1):
            pl.when(quanta == q)(functools.partial(ffn, q * EXPERT_ROW_QUANTUM))

    @pl.when(i >= nu_ref[0])
    def _():
        y_ref[...] = jnp.zeros_like(y_ref)


def _experts(block_e, first, next_e, valid, n_used, xs, w_gu, b_gu, w_down, b_down):
    d = D_MODEL
    nb = block_e.shape[0]
    tm = TM_EXPERT

    xmap = lambda i, be, fi, nx, va, nu: (jnp.minimum(i, nu[0] - 1), 0)
    grid_spec = pltpu.PrefetchScalarGridSpec(
        num_scalar_prefetch=5,
        grid=(nb,),
        in_specs=[pl.BlockSpec((tm, ROW_WORDS), xmap),
                  pl.BlockSpec(memory_space=pl.ANY),
                  _const_spec((N_EXPERTS, 1, 2 * D_FF)),
                  pl.BlockSpec(memory_space=pl.ANY),
                  _const_spec((N_EXPERTS, 1, d))],
        out_specs=pl.BlockSpec((tm, ROW_WORDS), lambda i, be, fi, nx, va, nu: (i, 0)),
        scratch_shapes=[pltpu.VMEM((d, 2 * D_FF), F32), pltpu.VMEM((D_FF, d), F32),
                        pltpu.VMEM((d, 2 * D_FF), BF16), pltpu.VMEM((D_FF, d), BF16),
                        pltpu.SemaphoreType.DMA((2,))],
    )
    return pl.pallas_call(
        _expert_kernel,
        grid_spec=grid_spec,
        out_shape=jax.ShapeDtypeStruct((nb * tm, ROW_WORDS), U32),
        compiler_params=_params("arbitrary"),
        name="moe_experts",
    )(block_e, first, next_e, valid, n_used, xs, w_gu, b_gu.reshape(N_EXPERTS, 1, 2 * D_FF),
      w_down, b_down.reshape(N_EXPERTS, 1, d))


def _sc_gather_rows(table, idx):
    m = idx.shape[0]
    w = table.shape[1]
    win = SC_GATHER_WINDOW
    per_worker = m // SC_WORKERS
    n_win = per_worker // win
    assert per_worker * SC_WORKERS == m and n_win * win == per_worker and n_win % 2 == 0
    mesh = plsc.VectorSubcoreMesh(core_axis_name="c", subcore_axis_name="s")

    @functools.partial(
        pl.kernel, mesh=mesh,
        out_type=jax.ShapeDtypeStruct((m, w), table.dtype),
        scratch_types=[pltpu.VMEM((win,), jnp.int32), pltpu.VMEM((win,), jnp.int32),
                       pltpu.VMEM((win, w), table.dtype), pltpu.VMEM((win, w), table.dtype),
                       pltpu.SemaphoreType.DMA((4,))],
        name="sc_gather_rows",
    )
    def gather(table_hbm, idx_hbm, out_hbm, idx0, idx1, rows0, rows1, sems):
        wid = _sc_worker_id()
        idxs, rows = (idx0, idx1), (rows0, rows1)

        def out_window(j):
            return out_hbm.at[pl.ds(wid * per_worker + j * win, win)]

        def start_gather(j, slot):
            pltpu.sync_copy(idx_hbm.at[pl.ds(wid * per_worker + j * win, win)], idxs[slot])
            pltpu.async_copy(table_hbm.at[idxs[slot]], rows[slot], sems.at[slot])

        def wait_gather(slot):
            pltpu.make_async_copy(table_hbm.at[idxs[slot]], rows[slot], sems.at[slot]).wait()

        def wait_write(j, slot):
            pltpu.make_async_copy(rows[slot], out_window(j), sems.at[2 + slot]).wait()

        start_gather(0, 0)

        @pl.loop(0, n_win, step=2)
        def _(j0):
            for slot in range(2):
                j = j0 + slot
                other = 1 - slot

                @pl.when(j >= 1)
                def _():
                    wait_write(j - 1, other)

                @pl.when(j + 1 < n_win)
                def _():
                    start_gather(j + 1, other)

                wait_gather(slot)
                pltpu.async_copy(rows[slot], out_window(j), sems.at[2 + slot])

        wait_write(n_win - 1, (n_win - 1) % 2)

    return gather(table, idx)


def _combine_dense_kernel(y4_ref, gate_ref, x1_ref, mod_ref, fg_ref, *rest):
    o_ref = rest[-1]
    gate = gate_ref[...]
    y = gate[:, 0:1] * _unpack_rows(y4_ref[0])
    for k in range(1, TOP_K):
        y = y + gate[:, k:k + 1] * _unpack_rows(y4_ref[k])
    x2 = x1_ref[...] + mod_ref[0, 5:6, :] * y
    o_ref[...] = x2 * lax.rsqrt(jnp.mean(x2 * x2, axis=-1, keepdims=True) + NORM_EPS) * fg_ref[...]


def _combine_dense(chunk, y4, gate_w, x1, mod, final_g, out_so_far):
    b, l, d = x1.shape
    ts = TM_COMBINE
    nt = l // ts
    steps = y4.shape[1] // ts
    tile = lambda s: chunk * steps + s
    in_specs = [pl.BlockSpec((TOP_K, ts, ROW_WORDS), lambda s: (0, s, 0)),
                pl.BlockSpec((ts, TOP_K), lambda s: (tile(s), 0)),
                pl.BlockSpec((ts, d), lambda s: (tile(s), 0)),
                pl.BlockSpec((1, 6, d), lambda s: (tile(s) // nt, 0, 0)),
                _const_spec((1, d))]
    args = [y4, gate_w, x1.reshape(b * l, d), mod, final_g]
    aliases = {}
    if out_so_far is not None:
        in_specs.append(pl.BlockSpec(memory_space=pl.ANY))
        args.append(out_so_far)
        aliases = {len(args) - 1: 0}
    return pl.pallas_call(
        _combine_dense_kernel,
        grid=(steps,),
        in_specs=in_specs,
        out_specs=pl.BlockSpec((ts, d), lambda s: (tile(s), 0)),
        out_shape=jax.ShapeDtypeStruct((b * l, d), F32),
        input_output_aliases=aliases,
        compiler_params=_params("arbitrary"),
        name="moe_combine",
    )(*args)


def _moe_expert_rows(h2, top_idx, rank, counts, w_gu, b_gu, w_down, b_down):
    n = h2.shape[0]
    tm = TM_EXPERT
    nb = n * TOP_K // tm + N_EXPERTS
    cnt = counts[:, 0].astype(jnp.int32)
    padded = (cnt + tm - 1) // tm * tm
    pend = jnp.cumsum(padded)
    pstart = pend - padded
    expert_ids = jnp.arange(N_EXPERTS, dtype=jnp.int32)
    dest = jnp.sum(jnp.where(top_idx[..., None] == expert_ids, pstart, 0), axis=-1) + rank
    n_used = (pend[-1] // tm).reshape(1)
    block_start = jnp.arange(nb, dtype=jnp.int32) * tm
    block_e = jnp.minimum(jnp.sum((pend[None, :] <= block_start[:, None]).astype(jnp.int32), axis=1),
                          N_EXPERTS - 1)
    first = jnp.concatenate([jnp.ones((1,), jnp.int32),
                             (block_e[1:] != block_e[:-1]).astype(jnp.int32)])
    later_used = (expert_ids[None, :] > expert_ids[:, None]) & (cnt[None, :] > 0)
    next_of_expert = jnp.min(jnp.where(later_used, expert_ids[None, :], N_EXPERTS), axis=1)
    next_of_expert = jnp.where(next_of_expert < N_EXPERTS, next_of_expert, -1)
    next_e = jnp.sum(jnp.where(block_e[:, None] == expert_ids, next_of_expert, 0), axis=-1)
    used_end = jnp.sum(jnp.where(block_e[:, None] == expert_ids, pstart + cnt, 0), axis=-1)
    valid = jnp.clip(used_end - block_start, 0, tm)

    qm = EXPERT_ROW_QUANTUM
    pad_j = jnp.arange(qm, dtype=jnp.int32)[None, :]
    n_pad = (cnt + qm - 1) // qm * qm - cnt
    spare = nb * tm + expert_ids[:, None] * qm + pad_j
    pad_dest = jnp.where(pad_j < n_pad[:, None], (pstart + cnt)[:, None] + pad_j, spare)
    pad_dest = pad_dest.reshape(SC_WORKERS, N_EXPERTS * qm // (SC_WORKERS * SC_WINDOW), SC_WINDOW)

    xs = _sc_scatter_rows(h2, dest, pad_dest, nb * tm + N_EXPERTS * qm)
    ys = _experts(block_e, first, next_e, valid, n_used, xs, w_gu, b_gu, w_down, b_down)
    return ys, dest


def _rope_tables(seq_len):
    inv_freq = ROPE_BASE ** (-jnp.arange(ROPE_PAIRS, dtype=F32) / ROPE_PAIRS)
    reps = LANES // HEAD_DIM
    out = []
    for n_pos, on_row in ((seq_len // GRID_W, True), (GRID_W, False)):
        ang = jnp.arange(n_pos, dtype=F32)[:, None] * inv_freq
        zero = jnp.zeros_like(ang)
        for fn, sign in ((jnp.cos, 1.0), (jnp.sin, -1.0)):
            v = fn(ang)
            head = ([sign * v, v, zero, zero] if on_row else [zero, zero, sign * v, v])
            out.append(jnp.concatenate(head * reps, axis=1))
    return out


def kernel(x, c, ctx, c_ctx, w_ada, b_ada, norm1_g, norm2_g, w_in, b_in, attn_sink, w_pool,
           pool_scale, w_attn_br, w_pool_br, w_out, w_router, b_router, w_gu, b_gu, w_down,
           b_down, final_g):
    b, l, d = x.shape
    assert w_ada.shape[0] == 1, "single-layer block"

    cond = jnp.zeros((SUBLANES, d), F32).at[:b].set(c).at[b].set(c_ctx)
    mod = _adaln(cond, w_ada[0], b_ada[0])[:b + 1].reshape(b + 1, 6, d)
    mod_x, mod_c = mod[:b], mod[b:b + 1]

    rope = _rope_tables(l)
    w_in_bf = w_in[0].astype(BF16)
    b_in2 = b_in[0].reshape(1, IN_W)
    g1 = norm1_g[0].reshape(1, d)
    q, kt, v, u, ga, gp = _inproj(x, mod_x, g1, w_in_bf, b_in2, rope)
    kv_sl = slice(ATTN_W, ATTN_W + 2 * KV_W)
    kxt, vx = _ctx_kv(ctx, mod_c, g1, w_in_bf[:, kv_sl], b_in2[:, kv_sl])

    attn_o = _attention(q, kt, v, kxt, vx, attn_sink[0])

    x1, h2, top_idx, gate_w, rank, counts = _merge_route(
        attn_o, u, ga, gp, x, mod_x, w_pool[0].astype(BF16), pool_scale[0].reshape(1, d),
        w_attn_br[0].astype(BF16), w_pool_br[0].astype(BF16), w_out[0].astype(BF16),
        norm2_g[0].reshape(1, d), w_router[0].astype(BF16), b_router[0].reshape(1, N_EXPERTS))

    ys, dest = _moe_expert_rows(h2, top_idx, rank, counts, w_gu[0], b_gu[0], w_down[0], b_down[0])

    n = b * l
    nc = n // COMBINE_CHUNKS
    out = None
    for ch in range(COMBINE_CHUNKS):
        idx = dest[:, ch * nc:(ch + 1) * nc].reshape(-1)
        y4 = _sc_gather_rows(ys, idx).reshape(TOP_K, nc, ROW_WORDS)
        out = _combine_dense(ch, y4, gate_w, x1, mod_x, final_g.reshape(1, d), out)
    return out.reshape(b, l, d)
```

```python
import functools

import jax
import jax.numpy as jnp
from jax import lax
from jax.experimental import pallas as pl
from jax.experimental.pallas import tpu as pltpu
from jax.experimental.pallas import tpu_sc as plsc

D_MODEL = 1024
GRID_W = 64
HEAD_DIM = 64
N_HEADS = 16
N_KV_HEADS = 4
GROUP = N_HEADS // N_KV_HEADS
ATTN_W = N_HEADS * HEAD_DIM
KV_W = N_KV_HEADS * HEAD_DIM
WINDOW = 128
ATTN_SCALE = HEAD_DIM ** -0.5
ROPE_BASE = 10000.0
ROPE_PAIRS = HEAD_DIM // 4
POOL_WINDOWS = (2, 4, 8, 16)
POOL_GROUP_W = D_MODEL // len(POOL_WINDOWS)
IN_W = ATTN_W + 2 * KV_W + D_MODEL + 2 * D_MODEL
VEXT_W = N_KV_HEADS * 2 * HEAD_DIM
N_EXPERTS = 32
TOP_K = 4
D_FF = D_MODEL
SWIGLU_ALPHA = 1.702
SWIGLU_LIMIT = 7.0
NORM_EPS = 1e-5
NEG_INF = -1e30

LANES = 128
SUBLANES = 8
VMEM_LIMIT = 56 * 1024 * 1024

SC_CORES = 2
SC_SUBCORES = 16
SC_WORKERS = SC_CORES * SC_SUBCORES
SC_WINDOW = 128
SC_GATHER_WINDOW = 64

TM_INPROJ = 1024
INPROJ_SUB = 512
TQ = WINDOW
ATTN_Q_BLOCKS = 4
TM_MERGE = 1024
MERGE_SUB = 256
TM_EXPERT = 512
EXPERT_ROW_QUANTUM = 128
TM_COMBINE = 1024
POOL_HALO = SUBLANES

F32 = jnp.float32
BF16 = jnp.bfloat16
U32 = jnp.uint32
ROW_WORDS = D_MODEL // 2


def _pack_rows(value):
    bits = lax.bitcast_convert_type(value.astype(BF16).astype(F32), U32)
    return bits[:, :ROW_WORDS] | (bits[:, ROW_WORDS:] >> 16)


def _unpack_rows(words):
    hi = lax.bitcast_convert_type(words & jnp.uint32(0xFFFF0000), F32)
    lo = lax.bitcast_convert_type(words << 16, F32)
    return jnp.concatenate([hi, lo], axis=1)


def _params(*sem):
    return pltpu.CompilerParams(dimension_semantics=sem, vmem_limit_bytes=VMEM_LIMIT)


def _const_spec(shape):
    nd = len(shape)
    return pl.BlockSpec(shape, lambda *_: (0,) * nd, pipeline_mode=pl.Buffered(1))


def _adaln_kernel(c_ref, w_ref, b_ref, o_ref):
    c = c_ref[...]
    s = c * jax.nn.sigmoid(c)
    o_ref[...] = jnp.dot(s.astype(BF16), w_ref[...].astype(BF16),
                         preferred_element_type=F32) + b_ref[...]


def _adaln(cond, w_ada, b_ada):
    rows, d = cond.shape
    n = w_ada.shape[1]
    tn = 1024
    return pl.pallas_call(
        _adaln_kernel,
        grid=(n // tn,),
        in_specs=[_const_spec((rows, d)),
                  pl.BlockSpec((d, tn), lambda j: (0, j)),
                  pl.BlockSpec((1, tn), lambda j: (0, j))],
        out_specs=pl.BlockSpec((rows, tn), lambda j: (0, j)),
        out_shape=jax.ShapeDtypeStruct((rows, n), F32),
        compiler_params=_params("arbitrary"),
        name="adaln",
    )(cond, w_ada, b_ada.reshape(1, n))


def _norm_mod(x, g, shift, scale):
    y = x * lax.rsqrt(jnp.mean(x * x, axis=-1, keepdims=True) + NORM_EPS) * g
    return y * (1.0 + scale) + shift


def _rope(t, cos, sin_signed):
    w = t.shape[1]
    half = ROPE_PAIRS
    lane = lax.broadcasted_iota(jnp.int32, t.shape, 1)
    first = (lane & (2 * half - 1)) < half
    rot = jnp.where(first, pltpu.roll(t, w - half, 1), pltpu.roll(t, half, 1))
    reps = w // LANES
    cos_w = jnp.concatenate([cos] * reps, axis=1) if reps > 1 else cos
    sin_w = jnp.concatenate([sin_signed] * reps, axis=1) if reps > 1 else sin_signed
    return t * cos_w + rot * sin_w


def _widen_values(v):
    ones = jnp.ones((v.shape[0], HEAD_DIM), BF16)
    vb = v.astype(BF16)
    parts = []
    for j in range(N_KV_HEADS):
        parts += [vb[:, j * HEAD_DIM:(j + 1) * HEAD_DIM], ones]
    return jnp.concatenate(parts, axis=1)


def _token_table(row_ref, col_ref):
    gr = row_ref.shape[0]
    shape = (gr, GRID_W, LANES)
    full = (jnp.broadcast_to(row_ref[...][:, None, :], shape)
            + jnp.broadcast_to(col_ref[...][None, :, :], shape))
    return full.reshape(gr * GRID_W, LANES)


def _inproj_kernel(x_ref, mod_ref, g_ref, w_ref, b_ref, cr_ref, sr_ref, cc_ref, sc_ref,
                   q_ref, k_ref, v_ref, u_ref, ga_ref, gp_ref):
    sub = INPROJ_SUB
    cos_all = _token_table(cr_ref, cc_ref)
    sin_all = _token_table(sr_ref, sc_ref)
    cw = 512

    def work_items(r0):
        rs = slice(r0, r0 + sub)
        cos, sin = cos_all[rs], sin_all[rs]
        hb = _norm_mod(x_ref[0, rs, :], g_ref[...], mod_ref[0, 0:1, :], mod_ref[0, 1:2, :]).astype(BF16)

        def put_k(t):
            k_ref[0, :, rs] = _rope(t, cos, sin).T.astype(BF16)

        def put_v(t):
            v_ref[0, rs, :] = _widen_values(t)

        items = []
        for c0 in range(0, ATTN_W, cw):
            def put_q(t, c0=c0):
                q_ref[0, rs, c0:c0 + cw] = (_rope(t, cos, sin) * ATTN_SCALE).astype(BF16)
            items.append((c0, cw, put_q))
        items.append((ATTN_W, KV_W, put_k))
        items.append((ATTN_W + KV_W, KV_W, put_v))
        base = ATTN_W + 2 * KV_W
        for c0 in range(0, D_MODEL, cw):
            def put_u(t, c0=c0):
                u_ref[0, rs, c0:c0 + cw] = t
            items.append((base + c0, cw, put_u))
        for ref in (ga_ref, gp_ref):
            base += D_MODEL
            for c0 in range(0, D_MODEL, cw):
                def put_gate(t, ref=ref, c0=c0):
                    ref[0, rs, c0:c0 + cw] = jax.nn.sigmoid(t).astype(BF16)
                items.append((base + c0, cw, put_gate))
        return [(hb, c0, w, fn) for c0, w, fn in items]

    def proj(hb, c0, w):
        return jnp.dot(hb, w_ref[:, c0:c0 + w], preferred_element_type=F32) + b_ref[:, c0:c0 + w]

    items = [it for r0 in range(0, x_ref.shape[1], sub) for it in work_items(r0)]
    pending = proj(*items[0][:3])
    for ci, item in enumerate(items):
        ready = pending
        if ci + 1 < len(items):
            pending = proj(*items[ci + 1][:3])
        item[3](ready)


def _inproj(x, mod, norm_g, w_in_bf, b_in, rope):
    b, l, d = x.shape
    tm = TM_INPROJ
    row = lambda bi, i: (bi, i, 0)
    out_shape = (
        jax.ShapeDtypeStruct((b, l, ATTN_W), BF16),
        jax.ShapeDtypeStruct((b, KV_W, l), BF16),
        jax.ShapeDtypeStruct((b, l, VEXT_W), BF16),
        jax.ShapeDtypeStruct((b, l, D_MODEL), F32),
        jax.ShapeDtypeStruct((b, l, D_MODEL), BF16),
        jax.ShapeDtypeStruct((b, l, D_MODEL), BF16),
    )
    return pl.pallas_call(
        _inproj_kernel,
        grid=(b, l // tm),
        in_specs=[pl.BlockSpec((1, tm, d), row),
                  pl.BlockSpec((1, 6, d), lambda bi, i: (bi, 0, 0)),
                  _const_spec((1, d)),
                  _const_spec((d, IN_W)),
                  _const_spec((1, IN_W)),
                  pl.BlockSpec((tm // GRID_W, LANES), lambda bi, i: (i, 0)),
                  pl.BlockSpec((tm // GRID_W, LANES), lambda bi, i: (i, 0)),
                  _const_spec((GRID_W, LANES)),
                  _const_spec((GRID_W, LANES))],
        out_specs=(pl.BlockSpec((1, tm, ATTN_W), row),
                   pl.BlockSpec((1, KV_W, tm), lambda bi, i: (bi, 0, i)),
                   pl.BlockSpec((1, tm, VEXT_W), row),
                   pl.BlockSpec((1, tm, D_MODEL), row),
                   pl.BlockSpec((1, tm, D_MODEL), row),
                   pl.BlockSpec((1, tm, D_MODEL), row)),
        out_shape=out_shape,
        compiler_params=_params("arbitrary", "arbitrary"),
        name="inproj",
    )(x, mod, norm_g, w_in_bf, b_in, *rope)


def _ctx_kv_kernel(x_ref, mod_ref, g_ref, w_ref, b_ref, k_ref, v_ref):
    h = _norm_mod(x_ref[0], g_ref[...], mod_ref[0, 0:1, :], mod_ref[0, 1:2, :])
    kv = jnp.dot(h.astype(BF16), w_ref[...], preferred_element_type=F32) + b_ref[...]
    k_ref[0] = kv[:, :KV_W].T.astype(BF16)
    v_ref[0] = _widen_values(kv[:, KV_W:])


def _ctx_kv(ctx, mod_c, norm_g, w_kv_bf, b_kv):
    b, lc, d = ctx.shape
    row = lambda bi: (bi, 0, 0)
    return pl.pallas_call(
        _ctx_kv_kernel,
        grid=(b,),
        in_specs=[pl.BlockSpec((1, lc, d), row),
                  _const_spec((1, 6, d)),
                  _const_spec((1, d)),
                  _const_spec((d, 2 * KV_W)),
                  _const_spec((1, 2 * KV_W))],
        out_specs=(pl.BlockSpec((1, KV_W, lc), row), pl.BlockSpec((1, lc, VEXT_W), row)),
        out_shape=(jax.ShapeDtypeStruct((b, KV_W, lc), BF16),
                   jax.ShapeDtypeStruct((b, lc, VEXT_W), BF16)),
        compiler_params=_params("arbitrary"),
        name="ctx_kv",
    )(ctx, mod_c, norm_g, w_kv_bf, b_kv)


def _attn_kernel(sink_ref, q_ref, *refs):
    nq = ATTN_Q_BLOCKS
    k_refs, v_refs = refs[:nq + 2], refs[nq + 2:2 * nq + 4]
    kx_ref, vx_ref, o_ref = refs[2 * nq + 4:]
    first_blk = pl.program_id(1) * nq
    last_blk = pl.num_programs(1) * nq - 1
    rows = GROUP * TQ
    pair_w = 2 * HEAD_DIM
    qi = lax.broadcasted_iota(jnp.int32, (rows, TQ), 0) & (TQ - 1)
    col = lax.broadcasted_iota(jnp.int32, (rows, TQ), 1)
    low_half = lax.broadcasted_iota(jnp.int32, (TQ, pair_w), 1) < HEAD_DIM
    row_id = lax.broadcasted_iota(jnp.int32, (rows, 1), 0)

    def scores(b, j):
        qrows = slice(b * TQ, (b + 1) * TQ)
        parts = []
        for g in range(GROUP):
            h = j * GROUP + g
            pair = q_ref[0, qrows, (h // 2) * pair_w:(h // 2 + 1) * pair_w]
            parts.append(jnp.where(low_half if h % 2 == 0 else ~low_half, pair, jnp.zeros_like(pair)))
        q4 = jnp.concatenate(parts, axis=0)
        ks = slice(j * HEAD_DIM, (j + 1) * HEAD_DIM)
        kt = jnp.concatenate([k_refs[b + d][0, ks, :] for d in range(3)], axis=1)
        s_loc = jnp.dot(q4, jnp.concatenate([kt, kt], axis=0), preferred_element_type=F32)
        kxt = kx_ref[0, ks, :]
        s_ctx = jnp.dot(q4, jnp.concatenate([kxt, kxt], axis=0), preferred_element_type=F32)
        return s_loc, s_ctx

    def probs(b, j, s_loc, s_ctx):
        keep_prev = (col >= qi) & (first_blk + b > 0)
        keep_next = (col <= qi) & (first_blk + b < last_blk)
        pieces = [jnp.where(keep_prev, s_loc[:, :TQ], NEG_INF), s_loc[:, TQ:2 * TQ],
                  jnp.where(keep_next, s_loc[:, 2 * TQ:], NEG_INF)]
        pieces += [s_ctx[:, c0:c0 + TQ] for c0 in range(0, s_ctx.shape[1], TQ)]
        sink = jnp.zeros((rows, 1), F32)
        for g in range(GROUP):
            sink = jnp.where(row_id // TQ == g, sink_ref[j * GROUP + g], sink)
        mx = pieces[0]
        for pc in pieces[1:]:
            mx = jnp.maximum(mx, pc)
        m = jnp.maximum(jnp.max(mx, axis=-1, keepdims=True), sink)
        p = jnp.concatenate([jnp.exp(pc - m).astype(BF16) for pc in pieces], axis=1)
        return p, jnp.exp(sink - m)

    def output(b, j, p, sink_p):
        vs = slice(j * pair_w, (j + 1) * pair_w)
        v_all = jnp.concatenate([v_refs[b + d][0, :, vs] for d in range(3)]
                                + [vx_ref[0, :, vs]], axis=0)
        r = jnp.dot(p, v_all, preferred_element_type=F32)
        den = pltpu.roll(r, HEAD_DIM, 1) + sink_p
        o = r / den
        for g in range(0, GROUP, 2):
            even = o[g * TQ:(g + 1) * TQ]
            odd = pltpu.roll(o[(g + 1) * TQ:(g + 2) * TQ], HEAD_DIM, 1)
            c0 = (j * GROUP + g) * HEAD_DIM
            o_ref[0, b * TQ:(b + 1) * TQ, c0:c0 + pair_w] = jnp.where(low_half, even, odd).astype(BF16)

    items = [(b, j) for b in range(nq) for j in range(N_KV_HEADS)]
    s_queue = [scores(*items[0]), scores(*items[1])]
    p_queue = [probs(*items[0], *s_queue.pop(0))]
    for i, item in enumerate(items):
        if i + 2 < len(items):
            s_queue.append(scores(*items[i + 2]))
        ready = p_queue.pop(0)
        if i + 1 < len(items):
            p_queue.append(probs(*items[i + 1], *s_queue.pop(0)))
        output(*item, *ready)


def _attention(q, kt, v, kxt, vx, sink):
    b, l, _ = q.shape
    lc = kxt.shape[2]
    nq = ATTN_Q_BLOCKS
    nblk = l // TQ
    assert nblk % nq == 0

    def blk(n, d):
        return jnp.clip(n * nq + d - 1, 0, nblk - 1)

    k_specs = [pl.BlockSpec((1, KV_W, TQ), lambda bi, n, d=d: (bi, 0, blk(n, d))) for d in range(nq + 2)]
    v_specs = [pl.BlockSpec((1, TQ, VEXT_W), lambda bi, n, d=d: (bi, blk(n, d), 0)) for d in range(nq + 2)]
    return pl.pallas_call(
        _attn_kernel,
        grid=(b, nblk // nq),
        in_specs=[pl.BlockSpec(memory_space=pltpu.SMEM),
                  pl.BlockSpec((1, nq * TQ, ATTN_W), lambda bi, n: (bi, n, 0))]
                 + k_specs + v_specs
                 + [pl.BlockSpec((1, KV_W, lc), lambda bi, n: (bi, 0, 0)),
                    pl.BlockSpec((1, lc, VEXT_W), lambda bi, n: (bi, 0, 0))],
        out_specs=pl.BlockSpec((1, nq * TQ, ATTN_W), lambda bi, n: (bi, n, 0)),
        out_shape=jax.ShapeDtypeStruct((b, l, ATTN_W), BF16),
        compiler_params=_params("arbitrary", "arbitrary"),
        name="attention",
    )(sink, q, *([kt] * (nq + 2)), *([v] * (nq + 2)), kxt, vx)


def _exact_zero(value):
    bits = lax.bitcast_convert_type(value[0:SUBLANES, 0:LANES].astype(F32), U32)
    return lax.bitcast_convert_type((bits >> 16) >> 16, F32)[0:1]


def _shift_rows(a, d):
    n = a.shape[0]
    return pltpu.roll(a, (-d) % n, 0)


def _merge_kernel(ao_ref, u_ref, up_ref, un_ref, sa_ref, sp_ref, x_ref, mod_ref,
                  wpool_ref, pscale_ref, wab_ref, wpb_ref, wout_ref, g2_ref, wr_ref, br_ref,
                  x1_ref, h2_ref, idx_ref, gate_ref, rank_ref, cnt_ref, carry_ref, *, seq_len):
    bi = pl.program_id(0)
    i = pl.program_id(1)
    last = pl.num_programs(1) - 1
    tm, sub = TM_MERGE, MERGE_SUB

    @pl.when((bi == 0) & (i == 0))
    def _():
        carry_ref[...] = jnp.zeros_like(carry_ref)

    u = u_ref[0]
    prev = jnp.where(i > 0, up_ref[0], 0.0)
    nxt = jnp.where(i < last, un_ref[0], 0.0)
    ext = jnp.concatenate([prev, u, nxt], axis=0)

    def pool_diff(r0, after=None):
        t = i * tm + r0 + lax.broadcasted_iota(jnp.int32, (sub, 1), 0)
        zero_row = None
        if after is not None:
            zero_row = jnp.concatenate([_exact_zero(after)] * (POOL_GROUP_W // LANES), axis=1)
        diffs = []
        for g, w in enumerate(POOL_WINDOWS):
            cs = slice(g * POOL_GROUP_W, (g + 1) * POOL_GROUP_W)
            e = ext[r0:r0 + sub + 2 * POOL_HALO, cs]
            if zero_row is not None:
                e = e + zero_row
            acc = _shift_rows(e, -1) + e
            step = 1
            while 2 * step < w:
                acc = _shift_rows(acc, -step) + _shift_rows(acc, step)
                step *= 2
            win = acc[POOL_HALO:POOL_HALO + sub]
            half = w // 2
            cnt = (jnp.minimum(t + half, seq_len) - jnp.maximum(t - half, 0)).astype(F32)
            diffs.append((win / cnt - u[r0:r0 + sub, cs]).astype(BF16))
        return diffs

    def branches(r0, diffs):
        rs = slice(r0, r0 + sub)
        a = jnp.dot(ao_ref[0, rs, :], wab_ref[...], preferred_element_type=F32)
        mixed = [jnp.dot(df, wpool_ref[g], preferred_element_type=F32) for g, df in enumerate(diffs)]
        pool_o = jnp.concatenate(mixed, axis=1) * pscale_ref[...]
        p = jnp.dot(pool_o.astype(BF16), wpb_ref[...], preferred_element_type=F32)
        merged = sa_ref[0, rs, :].astype(F32) * a + sp_ref[0, rs, :].astype(F32) * p
        return merged.astype(BF16)

    def out_proj(merged):
        return jnp.dot(merged, wout_ref[...], preferred_element_type=F32)

    def finish(r0, o):
        rs = slice(r0, r0 + sub)
        x1 = x_ref[0, rs, :] + mod_ref[0, 2:3, :] * o
        x1_ref[0, rs, :] = x1
        h2 = _norm_mod(x1, g2_ref[...], mod_ref[0, 3:4, :], mod_ref[0, 4:5, :])
        h2_ref[rs, :] = _pack_rows(h2)
        return jnp.dot(h2.astype(BF16), wr_ref[...], preferred_element_type=F32) + br_ref[...]

    def route(r0, logits, carry):
        lt = logits.T
        eid = lax.broadcasted_iota(jnp.int32, (N_EXPERTS, sub), 0).astype(F32)
        work = lt
        vals, idxs, sels = [], [], []
        for _ in range(TOP_K):
            mx = jnp.max(work, axis=0, keepdims=True)
            ix = jnp.min(jnp.where(work == mx, eid, float(N_EXPERTS)), axis=0, keepdims=True)
            sel = eid == ix
            work = jnp.where(sel, -jnp.inf, work)
            vals.append(mx)
            idxs.append(ix)
            sels.append(sel)
        ex = [jnp.exp(v - vals[0]) for v in vals]
        tot = ex[0] + ex[1] + ex[2] + ex[3]
        pad = [jnp.zeros_like(tot)] * (SUBLANES - TOP_K)
        gates = jnp.concatenate([e_ / tot for e_ in ex] + pad, axis=0)
        gate_ref[r0:r0 + sub, :] = gates.T[:, :TOP_K]
        idx_ref[:, r0:r0 + sub] = jnp.concatenate(idxs, axis=0).astype(jnp.int32)
        member = (sels[0] | sels[1] | sels[2] | sels[3]).astype(F32)
        t_row = lax.broadcasted_iota(jnp.int32, (sub, sub), 0)
        t_col = lax.broadcasted_iota(jnp.int32, (sub, sub), 1)
        earlier = (t_row < t_col).astype(BF16)
        before = jnp.dot(member.astype(BF16), earlier, preferred_element_type=F32) + carry
        ranks = [jnp.sum(jnp.where(s, before, 0.0), axis=0, keepdims=True) for s in sels]
        rank_ref[:, r0:r0 + sub] = jnp.concatenate(ranks, axis=0).astype(jnp.int32)
        return carry + jnp.sum(member, axis=1, keepdims=True)

    starts = list(range(0, tm, sub))
    carry = carry_ref[...]
    merged = branches(starts[0], pool_diff(starts[0]))
    for si, r0 in enumerate(starts):
        o = out_proj(merged)
        if si + 1 < len(starts):
            merged = branches(starts[si + 1], pool_diff(starts[si + 1], after=merged))
        carry = route(r0, finish(r0, o), carry)
    carry_ref[...] = carry
    cnt_ref[...] = carry


def _merge_route(attn_o, u, ga, gp, x, mod, w_pool_bf, pool_scale, wab, wpb, wout,
                 norm2_g, w_router, b_router):
    b, l, d = x.shape
    tm = TM_MERGE
    nt = l // tm
    n = b * l
    hb = tm // POOL_HALO
    row = lambda bi, i: (bi, i, 0)
    flat = lambda bi, i: (bi * nt + i, 0)
    halo_prev = lambda bi, i: (bi, jnp.maximum(i * hb - 1, 0), 0)
    halo_next = lambda bi, i: (bi, jnp.minimum((i + 1) * hb, l // POOL_HALO - 1), 0)
    slot_major = lambda bi, i: (0, bi * nt + i)
    out_shape = (
        jax.ShapeDtypeStruct((b, l, d), F32),
        jax.ShapeDtypeStruct((n, ROW_WORDS), U32),
        jax.ShapeDtypeStruct((TOP_K, n), jnp.int32),
        jax.ShapeDtypeStruct((n, TOP_K), F32),
        jax.ShapeDtypeStruct((TOP_K, n), jnp.int32),
        jax.ShapeDtypeStruct((N_EXPERTS, 1), F32),
    )
    return pl.pallas_call(
        functools.partial(_merge_kernel, seq_len=l),
        grid=(b, nt),
        in_specs=[pl.BlockSpec((1, tm, d), row),
                  pl.BlockSpec((1, tm, d), row),
                  pl.BlockSpec((1, POOL_HALO, d), halo_prev),
                  pl.BlockSpec((1, POOL_HALO, d), halo_next),
                  pl.BlockSpec((1, tm, d), row),
                  pl.BlockSpec((1, tm, d), row),
                  pl.BlockSpec((1, tm, d), row),
                  pl.BlockSpec((1, 6, d), lambda bi, i: (bi, 0, 0)),
                  _const_spec(w_pool_bf.shape),
                  _const_spec((1, d)),
                  _const_spec((d, d)), _const_spec((d, d)), _const_spec((d, d)),
                  _const_spec((1, d)),
                  _const_spec((d, N_EXPERTS)),
                  _const_spec((1, N_EXPERTS))],
        out_specs=(pl.BlockSpec((1, tm, d), row),
                   pl.BlockSpec((tm, ROW_WORDS), flat),
                   pl.BlockSpec((TOP_K, tm), slot_major),
                   pl.BlockSpec((tm, TOP_K), flat),
                   pl.BlockSpec((TOP_K, tm), slot_major),
                   _const_spec((N_EXPERTS, 1))),
        out_shape=out_shape,
        scratch_shapes=[pltpu.VMEM((N_EXPERTS, 1), F32)],
        compiler_params=_params("arbitrary", "arbitrary"),
        name="merge_route",
    )(attn_o, u, u, u, ga, gp, x, mod, w_pool_bf, pool_scale, wab, wpb, wout,
      norm2_g, w_router, b_router)


def _sc_worker_id():
    return lax.axis_index("s") * SC_CORES + lax.axis_index("c")


def _sc_scatter_rows(rows, dest, pad_dest, out_rows):
    n, w = rows.shape
    n_win = n // (SC_WORKERS * SC_WINDOW)
    n_pad_win = pad_dest.shape[1]
    assert dest.shape == (TOP_K, n) and SC_WORKERS * n_win * SC_WINDOW == n
    mesh = plsc.VectorSubcoreMesh(core_axis_name="c", subcore_axis_name="s")
    zeros = jnp.zeros((SC_WINDOW, w), rows.dtype)

    @functools.partial(
        pl.kernel, mesh=mesh,
        out_type=jax.ShapeDtypeStruct((out_rows, w), rows.dtype),
        scratch_types=[pltpu.VMEM((TOP_K, SC_WINDOW), jnp.int32),
                       pltpu.VMEM((n_pad_win, SC_WINDOW), jnp.int32),
                       pltpu.VMEM((SC_WINDOW, w), rows.dtype),
                       pltpu.SemaphoreType.DMA],
        name="sc_scatter_rows",
    )
    def scatter(rows_hbm, dest_hbm, pad_hbm, zeros_hbm, out_hbm, idx_v, pad_v, rows_v, sem):
        wid = _sc_worker_id()

        @pl.loop(0, n_win)
        def _(j):
            base = (wid * n_win + j) * SC_WINDOW
            pltpu.sync_copy(rows_hbm.at[pl.ds(base, SC_WINDOW)], rows_v)
            pltpu.sync_copy(dest_hbm.at[:, pl.ds(base, SC_WINDOW)], idx_v)
            copies = [pltpu.async_copy(rows_v, out_hbm.at[idx_v.at[k]], sem) for k in range(TOP_K)]
            for cp in copies:
                cp.wait()

        pltpu.sync_copy(zeros_hbm, rows_v)
        pltpu.sync_copy(pad_hbm.at[wid], pad_v)
        copies = [pltpu.async_copy(rows_v, out_hbm.at[pad_v.at[j]], sem) for j in range(n_pad_win)]
        for cp in copies:
            cp.wait()

    return scatter(rows, dest, pad_dest, zeros)


def _expert_kernel(be_ref, first_ref, next_ref, valid_ref, nu_ref, x_ref, wgu_hbm, bgu_ref, wd_hbm,
                   bd_ref, y_ref, wgu_f, wd_f, wgu_s, wd_s, sem):
    i = pl.program_id(0)

    def weight_copies(e):
        return (pltpu.make_async_copy(wgu_hbm.at[e], wgu_f, sem.at[0]),
                pltpu.make_async_copy(wd_hbm.at[e], wd_f, sem.at[1]))

    def start_weights(e):
        for cp in weight_copies(e):
            cp.start(priority=1)

    @pl.when(i == 0)
    def _():
        start_weights(be_ref[0])

    @pl.when(i < nu_ref[0])
    def _():
        e = be_ref[i]

        @pl.when(first_ref[i] == 1)
        def _():
            for cp in weight_copies(e):
                cp.wait()
            ck = 16

            def cast_rows(c, carry):
                r = pl.ds(pl.multiple_of(c * ck, ck), ck)
                wgu_s[r, :] = wgu_f[r, :].astype(BF16)
                wd_s[r, :] = wd_f[r, :].astype(BF16)
                return carry

            lax.fori_loop(0, D_MODEL // ck, cast_rows, 0, unroll=2)

            @pl.when(next_ref[i] >= 0)
            def _():
                start_weights(next_ref[i])

        def ffn(m):
            x = _unpack_rows(x_ref[:m, :]).astype(BF16)
            bgu = bgu_ref[e]
            gt = jnp.dot(x, wgu_s[:, :D_FF], preferred_element_type=F32) + bgu[:, :D_FF]
            ln = jnp.dot(x, wgu_s[:, D_FF:], preferred_element_type=F32) + bgu[:, D_FF:]
            gt = jnp.minimum(gt, SWIGLU_LIMIT)
            ln = jnp.clip(ln, -SWIGLU_LIMIT, SWIGLU_LIMIT)
            act = gt * jax.nn.sigmoid(SWIGLU_ALPHA * gt) * (ln + 1.0)
            y = jnp.dot(act.astype(BF16), wd_s[...], preferred_element_type=F32) + bd_ref[e]
            y_ref[:m, :] = _pack_rows(y)
            if m < TM_EXPERT:
                y_ref[m:, :] = jnp.zeros((TM_EXPERT - m, ROW_WORDS), U32)

        quanta = (valid_ref[i] + EXPERT_ROW_QUANTUM - 1) // EXPERT_ROW_QUANTUM
        for q in range(1, TM_EXPERT // EXPERT_ROW_QUANTUM + 1):
            pl.when(quanta == q)(functools.partial(ffn, q * EXPERT_ROW_QUANTUM))

    @pl.when(i >= nu_ref[0])
    def _():
        y_ref[...] = jnp.zeros_like(y_ref)


def _experts(block_e, first, next_e, valid, n_used, xs, w_gu, b_gu, w_down, b_down):
    d = D_MODEL
    nb = block_e.shape[0]
    tm = TM_EXPERT

    xmap = lambda i, be, fi, nx, va, nu: (jnp.minimum(i, nu[0] - 1), 0)
    grid_spec = pltpu.PrefetchScalarGridSpec(
        num_scalar_prefetch=5,
        grid=(nb,),
        in_specs=[pl.BlockSpec((tm, ROW_WORDS), xmap),
                  pl.BlockSpec(memory_space=pl.ANY),
                  _const_spec((N_EXPERTS, 1, 2 * D_FF)),
                  pl.BlockSpec(memory_space=pl.ANY),
                  _const_spec((N_EXPERTS, 1, d))],
        out_specs=pl.BlockSpec((tm, ROW_WORDS), lambda i, be, fi, nx, va, nu: (i, 0)),
        scratch_shapes=[pltpu.VMEM((d, 2 * D_FF), F32), pltpu.VMEM((D_FF, d), F32),
                        pltpu.VMEM((d, 2 * D_FF), BF16), pltpu.VMEM((D_FF, d), BF16),
                        pltpu.SemaphoreType.DMA((2,))],
    )
    return pl.pallas_call(
        _expert_kernel,
        grid_spec=grid_spec,
        out_shape=jax.ShapeDtypeStruct((nb * tm, ROW_WORDS), U32),
        compiler_params=_params("arbitrary"),
        name="moe_experts",
    )(block_e, first, next_e, valid, n_used, xs, w_gu, b_gu.reshape(N_EXPERTS, 1, 2 * D_FF),
      w_down, b_down.reshape(N_EXPERTS, 1, d))


def _sc_gather_rows(table, idx):
    m = idx.shape[0]
    w = table.shape[1]
    win = SC_GATHER_WINDOW
    per_worker = m // SC_WORKERS
    n_win = per_worker // win
    assert per_worker * SC_WORKERS == m and n_win * win == per_worker and n_win % 2 == 0
    mesh = plsc.VectorSubcoreMesh(core_axis_name="c", subcore_axis_name="s")

    @functools.partial(
        pl.kernel, mesh=mesh,
        out_type=jax.ShapeDtypeStruct((m, w), table.dtype),
        scratch_types=[pltpu.VMEM((win,), jnp.int32), pltpu.VMEM((win,), jnp.int32),
                       pltpu.VMEM((win, w), table.dtype), pltpu.VMEM((win, w), table.dtype),
                       pltpu.SemaphoreType.DMA((4,))],
        name="sc_gather_rows",
    )
    def gather(table_hbm, idx_hbm, out_hbm, idx0, idx1, rows0, rows1, sems):
        wid = _sc_worker_id()
        idxs, rows = (idx0, idx1), (rows0, rows1)

        def out_window(j):
            return out_hbm.at[pl.ds(wid * per_worker + j * win, win)]

        def start_gather(j, slot):
            pltpu.sync_copy(idx_hbm.at[pl.ds(wid * per_worker + j * win, win)], idxs[slot])
            pltpu.async_copy(table_hbm.at[idxs[slot]], rows[slot], sems.at[slot])

        def wait_gather(slot):
            pltpu.make_async_copy(table_hbm.at[idxs[slot]], rows[slot], sems.at[slot]).wait()

        def wait_write(j, slot):
            pltpu.make_async_copy(rows[slot], out_window(j), sems.at[2 + slot]).wait()

        start_gather(0, 0)

        @pl.loop(0, n_win, step=2)
        def _(j0):
            for slot in range(2):
                j = j0 + slot
                other = 1 - slot

                @pl.when(j >= 1)
                def _():
                    wait_write(j - 1, other)

                @pl.when(j + 1 < n_win)
                def _():
                    start_gather(j + 1, other)

                wait_gather(slot)
                pltpu.async_copy(rows[slot], out_window(j), sems.at[2 + slot])

        wait_write(n_win - 1, (n_win - 1) % 2)

    return gather(table, idx)


def _combine_dense_kernel(y4_ref, gate_ref, x1_ref, mod_ref, fg_ref, o_ref):
    gate = gate_ref[...]
    y = gate[:, 0:1] * _unpack_rows(y4_ref[0])
    for k in range(1, TOP_K):
        y = y + gate[:, k:k + 1] * _unpack_rows(y4_ref[k])
    x2 = x1_ref[...] + mod_ref[0, 5:6, :] * y
    o_ref[...] = x2 * lax.rsqrt(jnp.mean(x2 * x2, axis=-1, keepdims=True) + NORM_EPS) * fg_ref[...]


def _combine_dense(y4, gate_w, x1, mod, final_g):
    b, l, d = x1.shape
    ts = TM_COMBINE
    nt = l // ts
    return pl.pallas_call(
        _combine_dense_kernel,
        grid=(b * nt,),
        in_specs=[pl.BlockSpec((TOP_K, ts, ROW_WORDS), lambda s: (0, s, 0)),
                  pl.BlockSpec((ts, TOP_K), lambda s: (s, 0)),
                  pl.BlockSpec((ts, d), lambda s: (s, 0)),
                  pl.BlockSpec((1, 6, d), lambda s: (s // nt, 0, 0)),
                  _const_spec((1, d))],
        out_specs=pl.BlockSpec((ts, d), lambda s: (s, 0)),
        out_shape=jax.ShapeDtypeStruct((b * l, d), F32),
        compiler_params=_params("arbitrary"),
        name="moe_combine",
    )(y4, gate_w, x1.reshape(b * l, d), mod, final_g).reshape(b, l, d)


def _moe_expert_rows(h2, top_idx, rank, counts, w_gu, b_gu, w_down, b_down):
    n = h2.shape[0]
    tm = TM_EXPERT
    nb = n * TOP_K // tm + N_EXPERTS
    cnt = counts[:, 0].astype(jnp.int32)
    padded = (cnt + tm - 1) // tm * tm
    pend = jnp.cumsum(padded)
    pstart = pend - padded
    expert_ids = jnp.arange(N_EXPERTS, dtype=jnp.int32)
    dest = jnp.sum(jnp.where(top_idx[..., None] == expert_ids, pstart, 0), axis=-1) + rank
    n_used = (pend[-1] // tm).reshape(1)
    block_start = jnp.arange(nb, dtype=jnp.int32) * tm
    block_e = jnp.minimum(jnp.sum((pend[None, :] <= block_start[:, None]).astype(jnp.int32), axis=1),
                          N_EXPERTS - 1)
    first = jnp.concatenate([jnp.ones((1,), jnp.int32),
                             (block_e[1:] != block_e[:-1]).astype(jnp.int32)])
    later_used = (expert_ids[None, :] > expert_ids[:, None]) & (cnt[None, :] > 0)
    next_of_expert = jnp.min(jnp.where(later_used, expert_ids[None, :], N_EXPERTS), axis=1)
    next_of_expert = jnp.where(next_of_expert < N_EXPERTS, next_of_expert, -1)
    next_e = jnp.sum(jnp.where(block_e[:, None] == expert_ids, next_of_expert, 0), axis=-1)
    used_end = jnp.sum(jnp.where(block_e[:, None] == expert_ids, pstart + cnt, 0), axis=-1)
    valid = jnp.clip(used_end - block_start, 0, tm)

    qm = EXPERT_ROW_QUANTUM
    pad_j = jnp.arange(qm, dtype=jnp.int32)[None, :]
    n_pad = (cnt + qm - 1) // qm * qm - cnt
    spare = nb * tm + expert_ids[:, None] * qm + pad_j
    pad_dest = jnp.where(pad_j < n_pad[:, None], (pstart + cnt)[:, None] + pad_j, spare)
    pad_dest = pad_dest.reshape(SC_WORKERS, N_EXPERTS * qm // (SC_WORKERS * SC_WINDOW), SC_WINDOW)

    xs = _sc_scatter_rows(h2, dest, pad_dest, nb * tm + N_EXPERTS * qm)
    ys = _experts(block_e, first, next_e, valid, n_used, xs, w_gu, b_gu, w_down, b_down)
    return _sc_gather_rows(ys, dest.reshape(-1)).reshape(TOP_K, n, ROW_WORDS)


def _rope_tables(seq_len):
    inv_freq = ROPE_BASE ** (-jnp.arange(ROPE_PAIRS, dtype=F32) / ROPE_PAIRS)
    reps = LANES // HEAD_DIM
    out = []
    for n_pos, on_row in ((seq_len // GRID_W, True), (GRID_W, False)):
        ang = jnp.arange(n_pos, dtype=F32)[:, None] * inv_freq
        zero = jnp.zeros_like(ang)
        for fn, sign in ((jnp.cos, 1.0), (jnp.sin, -1.0)):
            v = fn(ang)
            head = ([sign * v, v, zero, zero] if on_row else [zero, zero, sign * v, v])
            out.append(jnp.concatenate(head * reps, axis=1))
    return out


def kernel(x, c, ctx, c_ctx, w_ada, b_ada, norm1_g, norm2_g, w_in, b_in, attn_sink, w_pool,
           pool_scale, w_attn_br, w_pool_br, w_out, w_router, b_router, w_gu, b_gu, w_down,
           b_down, final_g):
    b, l, d = x.shape
    assert w_ada.shape[0] == 1, "single-layer block"

    cond = jnp.zeros((SUBLANES, d), F32).at[:b].set(c).at[b].set(c_ctx)
    mod = _adaln(cond, w_ada[0], b_ada[0])[:b + 1].reshape(b + 1, 6, d)
    mod_x, mod_c = mod[:b], mod[b:b + 1]

    rope = _rope_tables(l)
    w_in_bf = w_in[0].astype(BF16)
    b_in2 = b_in[0].reshape(1, IN_W)
    g1 = norm1_g[0].reshape(1, d)
    q, kt, v, u, ga, gp = _inproj(x, mod_x, g1, w_in_bf, b_in2, rope)
    kv_sl = slice(ATTN_W, ATTN_W + 2 * KV_W)
    kxt, vx = _ctx_kv(ctx, mod_c, g1, w_in_bf[:, kv_sl], b_in2[:, kv_sl])

    attn_o = _attention(q, kt, v, kxt, vx, attn_sink[0])

    x1, h2, top_idx, gate_w, rank, counts = _merge_route(
        attn_o, u, ga, gp, x, mod_x, w_pool[0].astype(BF16), pool_scale[0].reshape(1, d),
        w_attn_br[0].astype(BF16), w_pool_br[0].astype(BF16), w_out[0].astype(BF16),
        norm2_g[0].reshape(1, d), w_router[0].astype(BF16), b_router[0].reshape(1, N_EXPERTS))

    y4 = _moe_expert_rows(h2, top_idx, rank, counts, w_gu[0], b_gu[0], w_down[0], b_down[0])
    return _combine_dense(y4, gate_w, x1, mod_x, final_g.reshape(1, d))
```

```python
import functools

import jax
import jax.numpy as jnp
from jax import lax
from jax.experimental import pallas as pl
from jax.experimental.pallas import tpu as pltpu
from jax.experimental.pallas import tpu_sc as plsc

D_MODEL = 1024
GRID_W = 64
HEAD_DIM = 64
N_HEADS = 16
N_KV_HEADS = 4
GROUP = N_HEADS // N_KV_HEADS
ATTN_W = N_HEADS * HEAD_DIM
KV_W = N_KV_HEADS * HEAD_DIM
WINDOW = 128
ATTN_SCALE = HEAD_DIM ** -0.5
ROPE_BASE = 10000.0
ROPE_PAIRS = HEAD_DIM // 4
POOL_WINDOWS = (2, 4, 8, 16)
POOL_GROUP_W = D_MODEL // len(POOL_WINDOWS)
IN_W = ATTN_W + 2 * KV_W + D_MODEL + 2 * D_MODEL
VEXT_W = N_KV_HEADS * 2 * HEAD_DIM
N_EXPERTS = 32
TOP_K = 4
D_FF = D_MODEL
SWIGLU_ALPHA = 1.702
SWIGLU_LIMIT = 7.0
NORM_EPS = 1e-5
NEG_INF = -1e30

LANES = 128
SUBLANES = 8
VMEM_LIMIT = 56 * 1024 * 1024

SC_CORES = 2
SC_SUBCORES = 16
SC_WORKERS = SC_CORES * SC_SUBCORES
SC_WINDOW = 128
SC_GATHER_WINDOW = 64

TM_INPROJ = 1024
INPROJ_SUB = 512
TQ = WINDOW
ATTN_Q_BLOCKS = 4
TM_MERGE = 1024
MERGE_SUB = 256
TM_EXPERT = 512
EXPERT_ROW_QUANTUM = 128
TM_COMBINE = 1024
POOL_HALO = SUBLANES

F32 = jnp.float32
BF16 = jnp.bfloat16
U32 = jnp.uint32
ROW_WORDS = D_MODEL // 2


def _pack_rows(value):
    bits = lax.bitcast_convert_type(value.astype(BF16).astype(F32), U32)
    return bits[:, :ROW_WORDS] | (bits[:, ROW_WORDS:] >> 16)


def _unpack_rows(words):
    hi = lax.bitcast_convert_type(words & jnp.uint32(0xFFFF0000), F32)
    lo = lax.bitcast_convert_type(words << 16, F32)
    return jnp.concatenate([hi, lo], axis=1)


def _params(*sem):
    return pltpu.CompilerParams(dimension_semantics=sem, vmem_limit_bytes=VMEM_LIMIT)


def _const_spec(shape):
    nd = len(shape)
    return pl.BlockSpec(shape, lambda *_: (0,) * nd, pipeline_mode=pl.Buffered(1))


def _adaln_kernel(c_ref, w_ref, b_ref, o_ref):
    c = c_ref[...]
    s = c * jax.nn.sigmoid(c)
    o_ref[...] = jnp.dot(s.astype(BF16), w_ref[...].astype(BF16),
                         preferred_element_type=F32) + b_ref[...]


def _adaln(cond, w_ada, b_ada):
    rows, d = cond.shape
    n = w_ada.shape[1]
    tn = 1024
    return pl.pallas_call(
        _adaln_kernel,
        grid=(n // tn,),
        in_specs=[_const_spec((rows, d)),
                  pl.BlockSpec((d, tn), lambda j: (0, j)),
                  pl.BlockSpec((1, tn), lambda j: (0, j))],
        out_specs=pl.BlockSpec((rows, tn), lambda j: (0, j)),
        out_shape=jax.ShapeDtypeStruct((rows, n), F32),
        compiler_params=_params("arbitrary"),
        name="adaln",
    )(cond, w_ada, b_ada.reshape(1, n))


def _norm_mod(x, g, shift, scale):
    y = x * lax.rsqrt(jnp.mean(x * x, axis=-1, keepdims=True) + NORM_EPS) * g
    return y * (1.0 + scale) + shift


def _rope(t, cos, sin_signed):
    w = t.shape[1]
    half = ROPE_PAIRS
    lane = lax.broadcasted_iota(jnp.int32, t.shape, 1)
    first = (lane & (2 * half - 1)) < half
    rot = jnp.where(first, pltpu.roll(t, w - half, 1), pltpu.roll(t, half, 1))
    reps = w // LANES
    cos_w = jnp.concatenate([cos] * reps, axis=1) if reps > 1 else cos
    sin_w = jnp.concatenate([sin_signed] * reps, axis=1) if reps > 1 else sin_signed
    return t * cos_w + rot * sin_w


def _widen_values(v):
    ones = jnp.ones((v.shape[0], HEAD_DIM), BF16)
    vb = v.astype(BF16)
    parts = []
    for j in range(N_KV_HEADS):
        parts += [vb[:, j * HEAD_DIM:(j + 1) * HEAD_DIM], ones]
    return jnp.concatenate(parts, axis=1)


def _token_table(row_ref, col_ref):
    gr = row_ref.shape[0]
    shape = (gr, GRID_W, LANES)
    full = (jnp.broadcast_to(row_ref[...][:, None, :], shape)
            + jnp.broadcast_to(col_ref[...][None, :, :], shape))
    return full.reshape(gr * GRID_W, LANES)


def _inproj_kernel(x_ref, mod_ref, g_ref, w_ref, b_ref, cr_ref, sr_ref, cc_ref, sc_ref,
                   q_ref, k_ref, v_ref, u_ref, ga_ref, gp_ref):
    sub = INPROJ_SUB
    cos_all = _token_table(cr_ref, cc_ref)
    sin_all = _token_table(sr_ref, sc_ref)
    cw = 512

    def work_items(r0):
        rs = slice(r0, r0 + sub)
        cos, sin = cos_all[rs], sin_all[rs]
        hb = _norm_mod(x_ref[0, rs, :], g_ref[...], mod_ref[0, 0:1, :], mod_ref[0, 1:2, :]).astype(BF16)

        def put_k(t):
            k_ref[0, :, rs] = _rope(t, cos, sin).T.astype(BF16)

        def put_v(t):
            v_ref[0, rs, :] = _widen_values(t)

        items = []
        for c0 in range(0, ATTN_W, cw):
            def put_q(t, c0=c0):
                q_ref[0, rs, c0:c0 + cw] = (_rope(t, cos, sin) * ATTN_SCALE).astype(BF16)
            items.append((c0, cw, put_q))
        items.append((ATTN_W, KV_W, put_k))
        items.append((ATTN_W + KV_W, KV_W, put_v))
        base = ATTN_W + 2 * KV_W
        for c0 in range(0, D_MODEL, cw):
            def put_u(t, c0=c0):
                u_ref[0, rs, c0:c0 + cw] = t
            items.append((base + c0, cw, put_u))
        for ref in (ga_ref, gp_ref):
            base += D_MODEL
            for c0 in range(0, D_MODEL, cw):
                def put_gate(t, ref=ref, c0=c0):
                    ref[0, rs, c0:c0 + cw] = jax.nn.sigmoid(t).astype(BF16)
                items.append((base + c0, cw, put_gate))
        return [(hb, c0, w, fn) for c0, w, fn in items]

    def proj(hb, c0, w):
        return jnp.dot(hb, w_ref[:, c0:c0 + w], preferred_element_type=F32) + b_ref[:, c0:c0 + w]

    items = [it for r0 in range(0, x_ref.shape[1], sub) for it in work_items(r0)]
    pending = proj(*items[0][:3])
    for ci, item in enumerate(items):
        ready = pending
        if ci + 1 < len(items):
            pending = proj(*items[ci + 1][:3])
        item[3](ready)


def _inproj(x, mod, norm_g, w_in_bf, b_in, rope):
    b, l, d = x.shape
    tm = TM_INPROJ
    row = lambda bi, i: (bi, i, 0)
    out_shape = (
        jax.ShapeDtypeStruct((b, l, ATTN_W), BF16),
        jax.ShapeDtypeStruct((b, KV_W, l), BF16),
        jax.ShapeDtypeStruct((b, l, VEXT_W), BF16),
        jax.ShapeDtypeStruct((b, l, D_MODEL), F32),
        jax.ShapeDtypeStruct((b, l, D_MODEL), BF16),
        jax.ShapeDtypeStruct((b, l, D_MODEL), BF16),
    )
    return pl.pallas_call(
        _inproj_kernel,
        grid=(b, l // tm),
        in_specs=[pl.BlockSpec((1, tm, d), row),
                  pl.BlockSpec((1, 6, d), lambda bi, i: (bi, 0, 0)),
                  _const_spec((1, d)),
                  _const_spec((d, IN_W)),
                  _const_spec((1, IN_W)),
                  pl.BlockSpec((tm // GRID_W, LANES), lambda bi, i: (i, 0)),
                  pl.BlockSpec((tm // GRID_W, LANES), lambda bi, i: (i, 0)),
                  _const_spec((GRID_W, LANES)),
                  _const_spec((GRID_W, LANES))],
        out_specs=(pl.BlockSpec((1, tm, ATTN_W), row),
                   pl.BlockSpec((1, KV_W, tm), lambda bi, i: (bi, 0, i)),
                   pl.BlockSpec((1, tm, VEXT_W), row),
                   pl.BlockSpec((1, tm, D_MODEL), row),
                   pl.BlockSpec((1, tm, D_MODEL), row),
                   pl.BlockSpec((1, tm, D_MODEL), row)),
        out_shape=out_shape,
        compiler_params=_params("arbitrary", "arbitrary"),
        name="inproj",
    )(x, mod, norm_g, w_in_bf, b_in, *rope)


def _ctx_kv_kernel(x_ref, mod_ref, g_ref, w_ref, b_ref, k_ref, v_ref):
    h = _norm_mod(x_ref[0], g_ref[...], mod_ref[0, 0:1, :], mod_ref[0, 1:2, :])
    kv = jnp.dot(h.astype(BF16), w_ref[...], preferred_element_type=F32) + b_ref[...]
    k_ref[0] = kv[:, :KV_W].T.astype(BF16)
    v_ref[0] = _widen_values(kv[:, KV_W:])


def _ctx_kv(ctx, mod_c, norm_g, w_kv_bf, b_kv):
    b, lc, d = ctx.shape
    row = lambda bi: (bi, 0, 0)
    return pl.pallas_call(
        _ctx_kv_kernel,
        grid=(b,),
        in_specs=[pl.BlockSpec((1, lc, d), row),
                  _const_spec((1, 6, d)),
                  _const_spec((1, d)),
                  _const_spec((d, 2 * KV_W)),
                  _const_spec((1, 2 * KV_W))],
        out_specs=(pl.BlockSpec((1, KV_W, lc), row), pl.BlockSpec((1, lc, VEXT_W), row)),
        out_shape=(jax.ShapeDtypeStruct((b, KV_W, lc), BF16),
                   jax.ShapeDtypeStruct((b, lc, VEXT_W), BF16)),
        compiler_params=_params("arbitrary"),
        name="ctx_kv",
    )(ctx, mod_c, norm_g, w_kv_bf, b_kv)


def _attn_kernel(sink_ref, q_ref, *refs):
    nq = ATTN_Q_BLOCKS
    k_refs, v_refs = refs[:nq + 2], refs[nq + 2:2 * nq + 4]
    kx_ref, vx_ref, o_ref = refs[2 * nq + 4:]
    first_blk = pl.program_id(1) * nq
    last_blk = pl.num_programs(1) * nq - 1
    rows = GROUP * TQ
    pair_w = 2 * HEAD_DIM
    qi = lax.broadcasted_iota(jnp.int32, (rows, TQ), 0) & (TQ - 1)
    col = lax.broadcasted_iota(jnp.int32, (rows, TQ), 1)
    low_half = lax.broadcasted_iota(jnp.int32, (TQ, pair_w), 1) < HEAD_DIM
    row_id = lax.broadcasted_iota(jnp.int32, (rows, 1), 0)

    def scores(b, j):
        qrows = slice(b * TQ, (b + 1) * TQ)
        parts = []
        for g in range(GROUP):
            h = j * GROUP + g
            pair = q_ref[0, qrows, (h // 2) * pair_w:(h // 2 + 1) * pair_w]
            parts.append(jnp.where(low_half if h % 2 == 0 else ~low_half, pair, jnp.zeros_like(pair)))
        q4 = jnp.concatenate(parts, axis=0)
        ks = slice(j * HEAD_DIM, (j + 1) * HEAD_DIM)
        kt = jnp.concatenate([k_refs[b + d][0, ks, :] for d in range(3)], axis=1)
        s_loc = jnp.dot(q4, jnp.concatenate([kt, kt], axis=0), preferred_element_type=F32)
        kxt = kx_ref[0, ks, :]
        s_ctx = jnp.dot(q4, jnp.concatenate([kxt, kxt], axis=0), preferred_element_type=F32)
        return s_loc, s_ctx

    def probs(b, j, s_loc, s_ctx):
        keep_prev = (col >= qi) & (first_blk + b > 0)
        keep_next = (col <= qi) & (first_blk + b < last_blk)
        pieces = [jnp.where(keep_prev, s_loc[:, :TQ], NEG_INF), s_loc[:, TQ:2 * TQ],
                  jnp.where(keep_next, s_loc[:, 2 * TQ:], NEG_INF)]
        pieces += [s_ctx[:, c0:c0 + TQ] for c0 in range(0, s_ctx.shape[1], TQ)]
        sink = jnp.zeros((rows, 1), F32)
        for g in range(GROUP):
            sink = jnp.where(row_id // TQ == g, sink_ref[j * GROUP + g], sink)
        mx = pieces[0]
        for pc in pieces[1:]:
            mx = jnp.maximum(mx, pc)
        m = jnp.maximum(jnp.max(mx, axis=-1, keepdims=True), sink)
        p = jnp.concatenate([jnp.exp(pc - m).astype(BF16) for pc in pieces], axis=1)
        return p, jnp.exp(sink - m)

    def output(b, j, p, sink_p):
        vs = slice(j * pair_w, (j + 1) * pair_w)
        v_all = jnp.concatenate([v_refs[b + d][0, :, vs] for d in range(3)]
                                + [vx_ref[0, :, vs]], axis=0)
        r = jnp.dot(p, v_all, preferred_element_type=F32)
        den = pltpu.roll(r, HEAD_DIM, 1) + sink_p
        o = r / den
        for g in range(0, GROUP, 2):
            even = o[g * TQ:(g + 1) * TQ]
            odd = pltpu.roll(o[(g + 1) * TQ:(g + 2) * TQ], HEAD_DIM, 1)
            c0 = (j * GROUP + g) * HEAD_DIM
            o_ref[0, b * TQ:(b + 1) * TQ, c0:c0 + pair_w] = jnp.where(low_half, even, odd).astype(BF16)

    items = [(b, j) for b in range(nq) for j in range(N_KV_HEADS)]
    s_queue = [scores(*items[0]), scores(*items[1])]
    p_queue = [probs(*items[0], *s_queue.pop(0))]
    for i, item in enumerate(items):
        if i + 2 < len(items):
            s_queue.append(scores(*items[i + 2]))
        ready = p_queue.pop(0)
        if i + 1 < len(items):
            p_queue.append(probs(*items[i + 1], *s_queue.pop(0)))
        output(*item, *ready)


def _attention(q, kt, v, kxt, vx, sink):
    b, l, _ = q.shape
    lc = kxt.shape[2]
    nq = ATTN_Q_BLOCKS
    nblk = l // TQ
    assert nblk % nq == 0

    def blk(n, d):
        return jnp.clip(n * nq + d - 1, 0, nblk - 1)

    k_specs = [pl.BlockSpec((1, KV_W, TQ), lambda bi, n, d=d: (bi, 0, blk(n, d))) for d in range(nq + 2)]
    v_specs = [pl.BlockSpec((1, TQ, VEXT_W), lambda bi, n, d=d: (bi, blk(n, d), 0)) for d in range(nq + 2)]
    return pl.pallas_call(
        _attn_kernel,
        grid=(b, nblk // nq),
        in_specs=[pl.BlockSpec(memory_space=pltpu.SMEM),
                  pl.BlockSpec((1, nq * TQ, ATTN_W), lambda bi, n: (bi, n, 0))]
                 + k_specs + v_specs
                 + [pl.BlockSpec((1, KV_W, lc), lambda bi, n: (bi, 0, 0)),
                    pl.BlockSpec((1, lc, VEXT_W), lambda bi, n: (bi, 0, 0))],
        out_specs=pl.BlockSpec((1, nq * TQ, ATTN_W), lambda bi, n: (bi, n, 0)),
        out_shape=jax.ShapeDtypeStruct((b, l, ATTN_W), BF16),
        compiler_params=_params("arbitrary", "arbitrary"),
        name="attention",
    )(sink, q, *([kt] * (nq + 2)), *([v] * (nq + 2)), kxt, vx)


def _exact_zero(value):
    bits = lax.bitcast_convert_type(value[0:SUBLANES, 0:LANES].astype(F32), U32)
    return lax.bitcast_convert_type((bits >> 16) >> 16, F32)[0:1]


def _shift_rows(a, d):
    n = a.shape[0]
    return pltpu.roll(a, (-d) % n, 0)


def _merge_kernel(ao_ref, u_ref, up_ref, un_ref, sa_ref, sp_ref, x_ref, mod_ref,
                  wpool_ref, pscale_ref, wab_ref, wpb_ref, wout_ref, g2_ref, wr_ref, br_ref,
                  x1_ref, h2_ref, idx_ref, gate_ref, rank_ref, cnt_ref, carry_ref, *, seq_len):
    bi = pl.program_id(0)
    i = pl.program_id(1)
    last = pl.num_programs(1) - 1
    tm, sub = TM_MERGE, MERGE_SUB

    @pl.when((bi == 0) & (i == 0))
    def _():
        carry_ref[...] = jnp.zeros_like(carry_ref)

    def window(r0, cs):
        lo, hi = r0 - POOL_HALO, r0 + sub + POOL_HALO
        parts = []
        if lo < 0:
            parts.append(jnp.where(i > 0, up_ref[0, :, cs], 0.0))
        parts.append(u_ref[0, max(lo, 0):min(hi, tm), cs])
        if hi > tm:
            parts.append(jnp.where(i < last, un_ref[0, :, cs], 0.0))
        return jnp.concatenate(parts, axis=0) if len(parts) > 1 else parts[0]

    def pool_diff(r0, after=None):
        t = i * tm + r0 + lax.broadcasted_iota(jnp.int32, (sub, 1), 0)
        zero_row = None
        if after is not None:
            zero_row = jnp.concatenate([_exact_zero(after)] * (POOL_GROUP_W // LANES), axis=1)
        diffs = []
        for g, w in enumerate(POOL_WINDOWS):
            cs = slice(g * POOL_GROUP_W, (g + 1) * POOL_GROUP_W)
            e = window(r0, cs)
            if zero_row is not None:
                e = e + zero_row
            acc = _shift_rows(e, -1) + e
            step = 1
            while 2 * step < w:
                acc = _shift_rows(acc, -step) + _shift_rows(acc, step)
                step *= 2
            win = acc[POOL_HALO:POOL_HALO + sub]
            half = w // 2
            cnt = (jnp.minimum(t + half, seq_len) - jnp.maximum(t - half, 0)).astype(F32)
            diffs.append((win / cnt - e[POOL_HALO:POOL_HALO + sub]).astype(BF16))
        return diffs

    def branches(r0, diffs):
        rs = slice(r0, r0 + sub)
        a = jnp.dot(ao_ref[0, rs, :], wab_ref[...], preferred_element_type=F32)
        mixed = [jnp.dot(df, wpool_ref[g], preferred_element_type=F32) for g, df in enumerate(diffs)]
        pool_o = jnp.concatenate(mixed, axis=1) * pscale_ref[...]
        p = jnp.dot(pool_o.astype(BF16), wpb_ref[...], preferred_element_type=F32)
        merged = sa_ref[0, rs, :].astype(F32) * a + sp_ref[0, rs, :].astype(F32) * p
        return merged.astype(BF16)

    def out_proj(merged):
        return jnp.dot(merged, wout_ref[...], preferred_element_type=F32)

    def finish(r0, o):
        rs = slice(r0, r0 + sub)
        x1 = x_ref[0, rs, :] + mod_ref[0, 2:3, :] * o
        x1_ref[0, rs, :] = x1
        h2 = _norm_mod(x1, g2_ref[...], mod_ref[0, 3:4, :], mod_ref[0, 4:5, :])
        h2_ref[rs, :] = _pack_rows(h2)
        return jnp.dot(h2.astype(BF16), wr_ref[...], preferred_element_type=F32) + br_ref[...]

    def route(r0, logits, carry):
        lt = logits.T
        eid = lax.broadcasted_iota(jnp.int32, (N_EXPERTS, sub), 0).astype(F32)
        work = lt
        vals, idxs, sels = [], [], []
        for _ in range(TOP_K):
            mx = jnp.max(work, axis=0, keepdims=True)
            ix = jnp.min(jnp.where(work == mx, eid, float(N_EXPERTS)), axis=0, keepdims=True)
            sel = eid == ix
            work = jnp.where(sel, -jnp.inf, work)
            vals.append(mx)
            idxs.append(ix)
            sels.append(sel)
        ex = [jnp.exp(v - vals[0]) for v in vals]
        tot = ex[0] + ex[1] + ex[2] + ex[3]
        pad = [jnp.zeros_like(tot)] * (SUBLANES - TOP_K)
        gates = jnp.concatenate([e_ / tot for e_ in ex] + pad, axis=0)
        gate_ref[r0:r0 + sub, :] = gates.T[:, :TOP_K]
        idx_ref[:, r0:r0 + sub] = jnp.concatenate(idxs, axis=0).astype(jnp.int32)
        member = (sels[0] | sels[1] | sels[2] | sels[3]).astype(F32)
        t_row = lax.broadcasted_iota(jnp.int32, (sub, sub), 0)
        t_col = lax.broadcasted_iota(jnp.int32, (sub, sub), 1)
        earlier = (t_row < t_col).astype(BF16)
        before = jnp.dot(member.astype(BF16), earlier, preferred_element_type=F32) + carry
        ranks = [jnp.sum(jnp.where(s, before, 0.0), axis=0, keepdims=True) for s in sels]
        rank_ref[:, r0:r0 + sub] = jnp.concatenate(ranks, axis=0).astype(jnp.int32)
        return carry + jnp.sum(member, axis=1, keepdims=True)

    starts = list(range(0, tm, sub))
    carry = carry_ref[...]
    merged = branches(starts[0], pool_diff(starts[0]))
    for si, r0 in enumerate(starts):
        o = out_proj(merged)
        if si + 1 < len(starts):
            merged = branches(starts[si + 1], pool_diff(starts[si + 1], after=merged))
        carry = route(r0, finish(r0, o), carry)
    carry_ref[...] = carry
    cnt_ref[...] = carry


def _merge_route(attn_o, u, ga, gp, x, mod, w_pool_bf, pool_scale, wab, wpb, wout,
                 norm2_g, w_router, b_router):
    b, l, d = x.shape
    tm = TM_MERGE
    nt = l // tm
    n = b * l
    hb = tm // POOL_HALO
    row = lambda bi, i: (bi, i, 0)
    flat = lambda bi, i: (bi * nt + i, 0)
    halo_prev = lambda bi, i: (bi, jnp.maximum(i * hb - 1, 0), 0)
    halo_next = lambda bi, i: (bi, jnp.minimum((i + 1) * hb, l // POOL_HALO - 1), 0)
    slot_major = lambda bi, i: (0, bi * nt + i)
    out_shape = (
        jax.ShapeDtypeStruct((b, l, d), F32),
        jax.ShapeDtypeStruct((n, ROW_WORDS), U32),
        jax.ShapeDtypeStruct((TOP_K, n), jnp.int32),
        jax.ShapeDtypeStruct((n, TOP_K), F32),
        jax.ShapeDtypeStruct((TOP_K, n), jnp.int32),
        jax.ShapeDtypeStruct((N_EXPERTS, 1), F32),
    )
    return pl.pallas_call(
        functools.partial(_merge_kernel, seq_len=l),
        grid=(b, nt),
        in_specs=[pl.BlockSpec((1, tm, d), row),
                  pl.BlockSpec((1, tm, d), row),
                  pl.BlockSpec((1, POOL_HALO, d), halo_prev),
                  pl.BlockSpec((1, POOL_HALO, d), halo_next),
                  pl.BlockSpec((1, tm, d), row),
                  pl.BlockSpec((1, tm, d), row),
                  pl.BlockSpec((1, tm, d), row),
                  pl.BlockSpec((1, 6, d), lambda bi, i: (bi, 0, 0)),
                  _const_spec(w_pool_bf.shape),
                  _const_spec((1, d)),
                  _const_spec((d, d)), _const_spec((d, d)), _const_spec((d, d)),
                  _const_spec((1, d)),
                  _const_spec((d, N_EXPERTS)),
                  _const_spec((1, N_EXPERTS))],
        out_specs=(pl.BlockSpec((1, tm, d), row),
                   pl.BlockSpec((tm, ROW_WORDS), flat),
                   pl.BlockSpec((TOP_K, tm), slot_major),
                   pl.BlockSpec((tm, TOP_K), flat),
                   pl.BlockSpec((TOP_K, tm), slot_major),
                   _const_spec((N_EXPERTS, 1))),
        out_shape=out_shape,
        scratch_shapes=[pltpu.VMEM((N_EXPERTS, 1), F32)],
        compiler_params=_params("arbitrary", "arbitrary"),
        name="merge_route",
    )(attn_o, u, u, u, ga, gp, x, mod, w_pool_bf, pool_scale, wab, wpb, wout,
      norm2_g, w_router, b_router)


def _sc_worker_id():
    return lax.axis_index("s") * SC_CORES + lax.axis_index("c")


def _sc_scatter_rows(rows, dest, pad_dest, out_rows):
    n, w = rows.shape
    n_win = n // (SC_WORKERS * SC_WINDOW)
    n_pad_win = pad_dest.shape[1]
    assert dest.shape == (TOP_K, n) and SC_WORKERS * n_win * SC_WINDOW == n
    mesh = plsc.VectorSubcoreMesh(core_axis_name="c", subcore_axis_name="s")
    zeros = jnp.zeros((SC_WINDOW, w), rows.dtype)

    @functools.partial(
        pl.kernel, mesh=mesh,
        out_type=jax.ShapeDtypeStruct((out_rows, w), rows.dtype),
        scratch_types=[pltpu.VMEM((TOP_K, SC_WINDOW), jnp.int32),
                       pltpu.VMEM((n_pad_win, SC_WINDOW), jnp.int32),
                       pltpu.VMEM((SC_WINDOW, w), rows.dtype),
                       pltpu.SemaphoreType.DMA],
        name="sc_scatter_rows",
    )
    def scatter(rows_hbm, dest_hbm, pad_hbm, zeros_hbm, out_hbm, idx_v, pad_v, rows_v, sem):
        wid = _sc_worker_id()

        @pl.loop(0, n_win)
        def _(j):
            base = (wid * n_win + j) * SC_WINDOW
            pltpu.sync_copy(rows_hbm.at[pl.ds(base, SC_WINDOW)], rows_v)
            pltpu.sync_copy(dest_hbm.at[:, pl.ds(base, SC_WINDOW)], idx_v)
            copies = [pltpu.async_copy(rows_v, out_hbm.at[idx_v.at[k]], sem) for k in range(TOP_K)]
            for cp in copies:
                cp.wait()

        pltpu.sync_copy(zeros_hbm, rows_v)
        pltpu.sync_copy(pad_hbm.at[wid], pad_v)
        copies = [pltpu.async_copy(rows_v, out_hbm.at[pad_v.at[j]], sem) for j in range(n_pad_win)]
        for cp in copies:
            cp.wait()

    return scatter(rows, dest, pad_dest, zeros)


def _expert_kernel(be_ref, first_ref, next_ref, valid_ref, nu_ref, x_ref, wgu_hbm, bgu_ref, wd_hbm,
                   bd_ref, y_ref, wgu_f, wd_f, wgu_s, wd_s, sem):
    i = pl.program_id(0)

    def weight_copies(e):
        return (pltpu.make_async_copy(wgu_hbm.at[e], wgu_f, sem.at[0]),
                pltpu.make_async_copy(wd_hbm.at[e], wd_f, sem.at[1]))

    def start_weights(e):
        for cp in weight_copies(e):
            cp.start(priority=1)

    @pl.when(i == 0)
    def _():
        start_weights(be_ref[0])

    @pl.when(i < nu_ref[0])
    def _():
        e = be_ref[i]

        @pl.when(first_ref[i] == 1)
        def _():
            for cp in weight_copies(e):
                cp.wait()
            ck = 16

            def cast_rows(c, carry):
                r = pl.ds(pl.multiple_of(c * ck, ck), ck)
                wgu_s[r, :] = wgu_f[r, :].astype(BF16)
                wd_s[r, :] = wd_f[r, :].astype(BF16)
                return carry

            lax.fori_loop(0, D_MODEL // ck, cast_rows, 0, unroll=2)

            @pl.when(next_ref[i] >= 0)
            def _():
                start_weights(next_ref[i])

        def ffn(m):
            x = _unpack_rows(x_ref[:m, :]).astype(BF16)
            bgu = bgu_ref[e]
            gt = jnp.dot(x, wgu_s[:, :D_FF], preferred_element_type=F32) + bgu[:, :D_FF]
            ln = jnp.dot(x, wgu_s[:, D_FF:], preferred_element_type=F32) + bgu[:, D_FF:]
            gt = jnp.minimum(gt, SWIGLU_LIMIT)
            ln = jnp.clip(ln, -SWIGLU_LIMIT, SWIGLU_LIMIT)
            act = gt * jax.nn.sigmoid(SWIGLU_ALPHA * gt) * (ln + 1.0)
            y = jnp.dot(act.astype(BF16), wd_s[...], preferred_element_type=F32) + bd_ref[e]
            y_ref[:m, :] = _pack_rows(y)
            if m < TM_EXPERT:
                y_ref[m:, :] = jnp.zeros((TM_EXPERT - m, ROW_WORDS), U32)

        quanta = (valid_ref[i] + EXPERT_ROW_QUANTUM - 1) // EXPERT_ROW_QUANTUM
        for q in range(1, TM_EXPERT // EXPERT_ROW_QUANTUM + 1):
            pl.when(quanta == q)(functools.partial(ffn, q * EXPERT_ROW_QUANTUM))

    @pl.when(i >= nu_ref[0])
    def _():
        y_ref[...] = jnp.zeros_like(y_ref)


def _experts(block_e, first, next_e, valid, n_used, xs, w_gu, b_gu, w_down, b_down):
    d = D_MODEL
    nb = block_e.shape[0]
    tm = TM_EXPERT

    xmap = lambda i, be, fi, nx, va, nu: (jnp.minimum(i, nu[0] - 1), 0)
    grid_spec = pltpu.PrefetchScalarGridSpec(
        num_scalar_prefetch=5,
        grid=(nb,),
        in_specs=[pl.BlockSpec((tm, ROW_WORDS), xmap),
                  pl.BlockSpec(memory_space=pl.ANY),
                  _const_spec((N_EXPERTS, 1, 2 * D_FF)),
                  pl.BlockSpec(memory_space=pl.ANY),
                  _const_spec((N_EXPERTS, 1, d))],
        out_specs=pl.BlockSpec((tm, ROW_WORDS), lambda i, be, fi, nx, va, nu: (i, 0)),
        scratch_shapes=[pltpu.VMEM((d, 2 * D_FF), F32), pltpu.VMEM((D_FF, d), F32),
                        pltpu.VMEM((d, 2 * D_FF), BF16), pltpu.VMEM((D_FF, d), BF16),
                        pltpu.SemaphoreType.DMA((2,))],
    )
    return pl.pallas_call(
        _expert_kernel,
        grid_spec=grid_spec,
        out_shape=jax.ShapeDtypeStruct((nb * tm, ROW_WORDS), U32),
        compiler_params=_params("arbitrary"),
        name="moe_experts",
    )(block_e, first, next_e, valid, n_used, xs, w_gu, b_gu.reshape(N_EXPERTS, 1, 2 * D_FF),
      w_down, b_down.reshape(N_EXPERTS, 1, d))


def _sc_gather_rows(table, idx):
    m = idx.shape[0]
    w = table.shape[1]
    win = SC_GATHER_WINDOW
    per_worker = m // SC_WORKERS
    n_win = per_worker // win
    assert per_worker * SC_WORKERS == m and n_win * win == per_worker and n_win % 2 == 0
    mesh = plsc.VectorSubcoreMesh(core_axis_name="c", subcore_axis_name="s")

    @functools.partial(
        pl.kernel, mesh=mesh,
        out_type=jax.ShapeDtypeStruct((m, w), table.dtype),
        scratch_types=[pltpu.VMEM((win,), jnp.int32), pltpu.VMEM((win,), jnp.int32),
                       pltpu.VMEM((win, w), table.dtype), pltpu.VMEM((win, w), table.dtype),
                       pltpu.SemaphoreType.DMA((4,))],
        name="sc_gather_rows",
    )
    def gather(table_hbm, idx_hbm, out_hbm, idx0, idx1, rows0, rows1, sems):
        wid = _sc_worker_id()
        idxs, rows = (idx0, idx1), (rows0, rows1)

        def out_window(j):
            return out_hbm.at[pl.ds(wid * per_worker + j * win, win)]

        def start_gather(j, slot):
            pltpu.sync_copy(idx_hbm.at[pl.ds(wid * per_worker + j * win, win)], idxs[slot])
            pltpu.async_copy(table_hbm.at[idxs[slot]], rows[slot], sems.at[slot])

        def wait_gather(slot):
            pltpu.make_async_copy(table_hbm.at[idxs[slot]], rows[slot], sems.at[slot]).wait()

        def wait_write(j, slot):
            pltpu.make_async_copy(rows[slot], out_window(j), sems.at[2 + slot]).wait()

        start_gather(0, 0)

        @pl.loop(0, n_win, step=2)
        def _(j0):
            for slot in range(2):
                j = j0 + slot
                other = 1 - slot

                @pl.when(j >= 1)
                def _():
                    wait_write(j - 1, other)

                @pl.when(j + 1 < n_win)
                def _():
                    start_gather(j + 1, other)

                wait_gather(slot)
                pltpu.async_copy(rows[slot], out_window(j), sems.at[2 + slot])

        wait_write(n_win - 1, (n_win - 1) % 2)

    return gather(table, idx)


def _combine_dense_kernel(y4_ref, gate_ref, x1_ref, mod_ref, fg_ref, o_ref):
    gate = gate_ref[...]
    y = gate[:, 0:1] * _unpack_rows(y4_ref[0])
    for k in range(1, TOP_K):
        y = y + gate[:, k:k + 1] * _unpack_rows(y4_ref[k])
    x2 = x1_ref[...] + mod_ref[0, 5:6, :] * y
    o_ref[...] = x2 * lax.rsqrt(jnp.mean(x2 * x2, axis=-1, keepdims=True) + NORM_EPS) * fg_ref[...]


def _combine_dense(y4, gate_w, x1, mod, final_g):
    b, l, d = x1.shape
    ts = TM_COMBINE
    nt = l // ts
    return pl.pallas_call(
        _combine_dense_kernel,
        grid=(b * nt,),
        in_specs=[pl.BlockSpec((TOP_K, ts, ROW_WORDS), lambda s: (0, s, 0)),
                  pl.BlockSpec((ts, TOP_K), lambda s: (s, 0)),
                  pl.BlockSpec((ts, d), lambda s: (s, 0)),
                  pl.BlockSpec((1, 6, d), lambda s: (s // nt, 0, 0)),
                  _const_spec((1, d))],
        out_specs=pl.BlockSpec((ts, d), lambda s: (s, 0)),
        out_shape=jax.ShapeDtypeStruct((b * l, d), F32),
        compiler_params=_params("arbitrary"),
        name="moe_combine",
    )(y4, gate_w, x1.reshape(b * l, d), mod, final_g).reshape(b, l, d)


def _moe_expert_rows(h2, top_idx, rank, counts, w_gu, b_gu, w_down, b_down):
    n = h2.shape[0]
    tm = TM_EXPERT
    nb = n * TOP_K // tm + N_EXPERTS
    cnt = counts[:, 0].astype(jnp.int32)
    padded = (cnt + tm - 1) // tm * tm
    pend = jnp.cumsum(padded)
    pstart = pend - padded
    expert_ids = jnp.arange(N_EXPERTS, dtype=jnp.int32)
    dest = jnp.sum(jnp.where(top_idx[..., None] == expert_ids, pstart, 0), axis=-1) + rank
    n_used = (pend[-1] // tm).reshape(1)
    block_start = jnp.arange(nb, dtype=jnp.int32) * tm
    block_e = jnp.minimum(jnp.sum((pend[None, :] <= block_start[:, None]).astype(jnp.int32), axis=1),
                          N_EXPERTS - 1)
    first = jnp.concatenate([jnp.ones((1,), jnp.int32),
                             (block_e[1:] != block_e[:-1]).astype(jnp.int32)])
    later_used = (expert_ids[None, :] > expert_ids[:, None]) & (cnt[None, :] > 0)
    next_of_expert = jnp.min(jnp.where(later_used, expert_ids[None, :], N_EXPERTS), axis=1)
    next_of_expert = jnp.where(next_of_expert < N_EXPERTS, next_of_expert, -1)
    next_e = jnp.sum(jnp.where(block_e[:, None] == expert_ids, next_of_expert, 0), axis=-1)
    used_end = jnp.sum(jnp.where(block_e[:, None] == expert_ids, pstart + cnt, 0), axis=-1)
    valid = jnp.clip(used_end - block_start, 0, tm)

    qm = EXPERT_ROW_QUANTUM
    pad_j = jnp.arange(qm, dtype=jnp.int32)[None, :]
    n_pad = (cnt + qm - 1) // qm * qm - cnt
    spare = nb * tm + expert_ids[:, None] * qm + pad_j
    pad_dest = jnp.where(pad_j < n_pad[:, None], (pstart + cnt)[:, None] + pad_j, spare)
    pad_dest = pad_dest.reshape(SC_WORKERS, N_EXPERTS * qm // (SC_WORKERS * SC_WINDOW), SC_WINDOW)

    xs = _sc_scatter_rows(h2, dest, pad_dest, nb * tm + N_EXPERTS * qm)
    ys = _experts(block_e, first, next_e, valid, n_used, xs, w_gu, b_gu, w_down, b_down)
    return _sc_gather_rows(ys, dest.reshape(-1)).reshape(TOP_K, n, ROW_WORDS)


def _rope_tables(seq_len):
    inv_freq = ROPE_BASE ** (-jnp.arange(ROPE_PAIRS, dtype=F32) / ROPE_PAIRS)
    reps = LANES // HEAD_DIM
    out = []
    for n_pos, on_row in ((seq_len // GRID_W, True), (GRID_W, False)):
        ang = jnp.arange(n_pos, dtype=F32)[:, None] * inv_freq
        zero = jnp.zeros_like(ang)
        for fn, sign in ((jnp.cos, 1.0), (jnp.sin, -1.0)):
            v = fn(ang)
            head = ([sign * v, v, zero, zero] if on_row else [zero, zero, sign * v, v])
            out.append(jnp.concatenate(head * reps, axis=1))
    return out


def kernel(x, c, ctx, c_ctx, w_ada, b_ada, norm1_g, norm2_g, w_in, b_in, attn_sink, w_pool,
           pool_scale, w_attn_br, w_pool_br, w_out, w_router, b_router, w_gu, b_gu, w_down,
           b_down, final_g):
    b, l, d = x.shape
    assert w_ada.shape[0] == 1, "single-layer block"

    cond = jnp.zeros((SUBLANES, d), F32).at[:b].set(c).at[b].set(c_ctx)
    mod = _adaln(cond, w_ada[0], b_ada[0])[:b + 1].reshape(b + 1, 6, d)
    mod_x, mod_c = mod[:b], mod[b:b + 1]

    rope = _rope_tables(l)
    w_in_bf = w_in[0].astype(BF16)
    b_in2 = b_in[0].reshape(1, IN_W)
    g1 = norm1_g[0].reshape(1, d)
    q, kt, v, u, ga, gp = _inproj(x, mod_x, g1, w_in_bf, b_in2, rope)
    kv_sl = slice(ATTN_W, ATTN_W + 2 * KV_W)
    kxt, vx = _ctx_kv(ctx, mod_c, g1, w_in_bf[:, kv_sl], b_in2[:, kv_sl])

    attn_o = _attention(q, kt, v, kxt, vx, attn_sink[0])

    x1, h2, top_idx, gate_w, rank, counts = _merge_route(
        attn_o, u, ga, gp, x, mod_x, w_pool[0].astype(BF16), pool_scale[0].reshape(1, d),
        w_attn_br[0].astype(BF16), w_pool_br[0].astype(BF16), w_out[0].astype(BF16),
        norm2_g[0].reshape(1, d), w_router[0].astype(BF16), b_router[0].reshape(1, N_EXPERTS))

    y4 = _moe_expert_rows(h2, top_idx, rank, counts, w_gu[0], b_gu[0], w_down[0], b_down[0])
    return _combine_dense(y4, gate_w, x1, mod_x, final_g.reshape(1, d))
```

```python
import functools

import jax
import jax.numpy as jnp
from jax import lax
from jax.experimental import pallas as pl
from jax.experimental.pallas import tpu as pltpu
from jax.experimental.pallas import tpu_sc as plsc

D_MODEL = 1024
GRID_W = 64
HEAD_DIM = 64
N_HEADS = 16
N_KV_HEADS = 4
GROUP = N_HEADS // N_KV_HEADS
ATTN_W = N_HEADS * HEAD_DIM
KV_W = N_KV_HEADS * HEAD_DIM
WINDOW = 128
ATTN_SCALE = HEAD_DIM ** -0.5
ROPE_BASE = 10000.0
ROPE_PAIRS = HEAD_DIM // 4
POOL_WINDOWS = (2, 4, 8, 16)
POOL_GROUP_W = D_MODEL // len(POOL_WINDOWS)
IN_W = ATTN_W + 2 * KV_W + D_MODEL + 2 * D_MODEL
VEXT_W = N_KV_HEADS * 2 * HEAD_DIM
N_EXPERTS = 32
TOP_K = 4
D_FF = D_MODEL
SWIGLU_ALPHA = 1.702
SWIGLU_LIMIT = 7.0
NORM_EPS = 1e-5
NEG_INF = -1e30

LANES = 128
SUBLANES = 8
VMEM_LIMIT = 56 * 1024 * 1024

SC_CORES = 2
SC_SUBCORES = 16
SC_WORKERS = SC_CORES * SC_SUBCORES
SC_WINDOW = 128
SC_GATHER_WINDOW = 64

TM_INPROJ = 1024
INPROJ_SUB = 512
TQ = WINDOW
ATTN_Q_BLOCKS = 4
TM_MERGE = 1024
MERGE_SUB = 256
TM_EXPERT = 512
EXPERT_ROW_QUANTUM = 128
TM_COMBINE = 1024
POOL_HALO = SUBLANES

F32 = jnp.float32
BF16 = jnp.bfloat16
U32 = jnp.uint32
ROW_WORDS = D_MODEL // 2


def _pack_rows(value):
    bits = lax.bitcast_convert_type(value.astype(BF16).astype(F32), U32)
    return bits[:, :ROW_WORDS] | (bits[:, ROW_WORDS:] >> 16)


def _unpack_rows(words):
    hi = lax.bitcast_convert_type(words & jnp.uint32(0xFFFF0000), F32)
    lo = lax.bitcast_convert_type(words << 16, F32)
    return jnp.concatenate([hi, lo], axis=1)


def _params(*sem):
    return pltpu.CompilerParams(dimension_semantics=sem, vmem_limit_bytes=VMEM_LIMIT)


def _const_spec(shape):
    nd = len(shape)
    return pl.BlockSpec(shape, lambda *_: (0,) * nd, pipeline_mode=pl.Buffered(1))


def _adaln_kernel(c_ref, w_ref, b_ref, o_ref):
    c = c_ref[...]
    s = c * jax.nn.sigmoid(c)
    o_ref[...] = jnp.dot(s.astype(BF16), w_ref[...].astype(BF16),
                         preferred_element_type=F32) + b_ref[...]


def _adaln(cond, w_ada, b_ada):
    rows, d = cond.shape
    n = w_ada.shape[1]
    tn = 1024
    return pl.pallas_call(
        _adaln_kernel,
        grid=(n // tn,),
        in_specs=[_const_spec((rows, d)),
                  pl.BlockSpec((d, tn), lambda j: (0, j)),
                  pl.BlockSpec((1, tn), lambda j: (0, j))],
        out_specs=pl.BlockSpec((rows, tn), lambda j: (0, j)),
        out_shape=jax.ShapeDtypeStruct((rows, n), F32),
        compiler_params=_params("arbitrary"),
        name="adaln",
    )(cond, w_ada, b_ada.reshape(1, n))


def _norm_mod(x, g, shift, scale):
    y = x * lax.rsqrt(jnp.mean(x * x, axis=-1, keepdims=True) + NORM_EPS) * g
    return y * (1.0 + scale) + shift


def _rope(t, cos, sin_signed):
    w = t.shape[1]
    half = ROPE_PAIRS
    lane = lax.broadcasted_iota(jnp.int32, t.shape, 1)
    first = (lane & (2 * half - 1)) < half
    rot = jnp.where(first, pltpu.roll(t, w - half, 1), pltpu.roll(t, half, 1))
    reps = w // LANES
    cos_w = jnp.concatenate([cos] * reps, axis=1) if reps > 1 else cos
    sin_w = jnp.concatenate([sin_signed] * reps, axis=1) if reps > 1 else sin_signed
    return t * cos_w + rot * sin_w


def _widen_values(v):
    ones = jnp.ones((v.shape[0], HEAD_DIM), BF16)
    vb = v.astype(BF16)
    parts = []
    for j in range(N_KV_HEADS):
        parts += [vb[:, j * HEAD_DIM:(j + 1) * HEAD_DIM], ones]
    return jnp.concatenate(parts, axis=1)


def _token_table(row_ref, col_ref):
    gr = row_ref.shape[0]
    shape = (gr, GRID_W, LANES)
    full = (jnp.broadcast_to(row_ref[...][:, None, :], shape)
            + jnp.broadcast_to(col_ref[...][None, :, :], shape))
    return full.reshape(gr * GRID_W, LANES)


def _inproj_kernel(x_ref, mod_ref, g_ref, w_ref, b_ref, cr_ref, sr_ref, cc_ref, sc_ref,
                   q_ref, k_ref, v_ref, u_ref, ga_ref, gp_ref):
    sub = INPROJ_SUB
    cos_all = _token_table(cr_ref, cc_ref)
    sin_all = _token_table(sr_ref, sc_ref)
    cw = 512

    def work_items(r0):
        rs = slice(r0, r0 + sub)
        cos, sin = cos_all[rs], sin_all[rs]
        hb = _norm_mod(x_ref[0, rs, :], g_ref[...], mod_ref[0, 0:1, :], mod_ref[0, 1:2, :]).astype(BF16)

        def put_k(t):
            k_ref[0, :, rs] = _rope(t, cos, sin).T.astype(BF16)

        def put_v(t):
            v_ref[0, rs, :] = _widen_values(t)

        items = []
        for c0 in range(0, ATTN_W, cw):
            def put_q(t, c0=c0):
                q_ref[0, rs, c0:c0 + cw] = (_rope(t, cos, sin) * ATTN_SCALE).astype(BF16)
            items.append((c0, cw, put_q))
        items.append((ATTN_W, KV_W, put_k))
        items.append((ATTN_W + KV_W, KV_W, put_v))
        base = ATTN_W + 2 * KV_W
        for c0 in range(0, D_MODEL, cw):
            def put_u(t, c0=c0):
                u_ref[0, rs, c0:c0 + cw] = t
            items.append((base + c0, cw, put_u))
        for ref in (ga_ref, gp_ref):
            base += D_MODEL
            for c0 in range(0, D_MODEL, cw):
                def put_gate(t, ref=ref, c0=c0):
                    ref[0, rs, c0:c0 + cw] = jax.nn.sigmoid(t).astype(BF16)
                items.append((base + c0, cw, put_gate))
        return [(hb, c0, w, fn) for c0, w, fn in items]

    def proj(hb, c0, w):
        return jnp.dot(hb, w_ref[:, c0:c0 + w], preferred_element_type=F32) + b_ref[:, c0:c0 + w]

    items = [it for r0 in range(0, x_ref.shape[1], sub) for it in work_items(r0)]
    pending = proj(*items[0][:3])
    for ci, item in enumerate(items):
        ready = pending
        if ci + 1 < len(items):
            pending = proj(*items[ci + 1][:3])
        item[3](ready)


def _inproj(x, mod, norm_g, w_in_bf, b_in, rope):
    b, l, d = x.shape
    tm = TM_INPROJ
    row = lambda bi, i: (bi, i, 0)
    out_shape = (
        jax.ShapeDtypeStruct((b, l, ATTN_W), BF16),
        jax.ShapeDtypeStruct((b, KV_W, l), BF16),
        jax.ShapeDtypeStruct((b, l, VEXT_W), BF16),
        jax.ShapeDtypeStruct((b, l, D_MODEL), F32),
        jax.ShapeDtypeStruct((b, l, D_MODEL), BF16),
        jax.ShapeDtypeStruct((b, l, D_MODEL), BF16),
    )
    return pl.pallas_call(
        _inproj_kernel,
        grid=(b, l // tm),
        in_specs=[pl.BlockSpec((1, tm, d), row),
                  pl.BlockSpec((1, 6, d), lambda bi, i: (bi, 0, 0)),
                  _const_spec((1, d)),
                  _const_spec((d, IN_W)),
                  _const_spec((1, IN_W)),
                  pl.BlockSpec((tm // GRID_W, LANES), lambda bi, i: (i, 0)),
                  pl.BlockSpec((tm // GRID_W, LANES), lambda bi, i: (i, 0)),
                  _const_spec((GRID_W, LANES)),
                  _const_spec((GRID_W, LANES))],
        out_specs=(pl.BlockSpec((1, tm, ATTN_W), row),
                   pl.BlockSpec((1, KV_W, tm), lambda bi, i: (bi, 0, i)),
                   pl.BlockSpec((1, tm, VEXT_W), row),
                   pl.BlockSpec((1, tm, D_MODEL), row),
                   pl.BlockSpec((1, tm, D_MODEL), row),
                   pl.BlockSpec((1, tm, D_MODEL), row)),
        out_shape=out_shape,
        compiler_params=_params("arbitrary", "arbitrary"),
        name="inproj",
    )(x, mod, norm_g, w_in_bf, b_in, *rope)


def _ctx_kv_kernel(x_ref, mod_ref, g_ref, w_ref, b_ref, k_ref, v_ref):
    h = _norm_mod(x_ref[0], g_ref[...], mod_ref[0, 0:1, :], mod_ref[0, 1:2, :])
    kv = jnp.dot(h.astype(BF16), w_ref[...], preferred_element_type=F32) + b_ref[...]
    k_ref[0] = kv[:, :KV_W].T.astype(BF16)
    v_ref[0] = _widen_values(kv[:, KV_W:])


def _ctx_kv(ctx, mod_c, norm_g, w_kv_bf, b_kv):
    b, lc, d = ctx.shape
    row = lambda bi: (bi, 0, 0)
    return pl.pallas_call(
        _ctx_kv_kernel,
        grid=(b,),
        in_specs=[pl.BlockSpec((1, lc, d), row),
                  _const_spec((1, 6, d)),
                  _const_spec((1, d)),
                  _const_spec((d, 2 * KV_W)),
                  _const_spec((1, 2 * KV_W))],
        out_specs=(pl.BlockSpec((1, KV_W, lc), row), pl.BlockSpec((1, lc, VEXT_W), row)),
        out_shape=(jax.ShapeDtypeStruct((b, KV_W, lc), BF16),
                   jax.ShapeDtypeStruct((b, lc, VEXT_W), BF16)),
        compiler_params=_params("arbitrary"),
        name="ctx_kv",
    )(ctx, mod_c, norm_g, w_kv_bf, b_kv)


def _attn_kernel(sink_ref, q_ref, *refs):
    nq = ATTN_Q_BLOCKS
    k_refs, v_refs = refs[:nq + 2], refs[nq + 2:2 * nq + 4]
    kx_ref, vx_ref, o_ref = refs[2 * nq + 4:]
    first_blk = pl.program_id(1) * nq
    last_blk = pl.num_programs(1) * nq - 1
    rows = GROUP * TQ
    pair_w = 2 * HEAD_DIM
    qi = lax.broadcasted_iota(jnp.int32, (rows, TQ), 0) & (TQ - 1)
    col = lax.broadcasted_iota(jnp.int32, (rows, TQ), 1)
    low_half = lax.broadcasted_iota(jnp.int32, (TQ, pair_w), 1) < HEAD_DIM
    row_id = lax.broadcasted_iota(jnp.int32, (rows, 1), 0)

    def scores(b, j):
        qrows = slice(b * TQ, (b + 1) * TQ)
        parts = []
        for g in range(GROUP):
            h = j * GROUP + g
            pair = q_ref[0, qrows, (h // 2) * pair_w:(h // 2 + 1) * pair_w]
            parts.append(jnp.where(low_half if h % 2 == 0 else ~low_half, pair, jnp.zeros_like(pair)))
        q4 = jnp.concatenate(parts, axis=0)
        ks = slice(j * HEAD_DIM, (j + 1) * HEAD_DIM)
        kt = jnp.concatenate([k_refs[b + d][0, ks, :] for d in range(3)], axis=1)
        s_loc = jnp.dot(q4, jnp.concatenate([kt, kt], axis=0), preferred_element_type=F32)
        kxt = kx_ref[0, ks, :]
        s_ctx = jnp.dot(q4, jnp.concatenate([kxt, kxt], axis=0), preferred_element_type=F32)
        return s_loc, s_ctx

    def probs(b, j, s_loc, s_ctx):
        keep_prev = (col >= qi) & (first_blk + b > 0)
        keep_next = (col <= qi) & (first_blk + b < last_blk)
        pieces = [jnp.where(keep_prev, s_loc[:, :TQ], NEG_INF), s_loc[:, TQ:2 * TQ],
                  jnp.where(keep_next, s_loc[:, 2 * TQ:], NEG_INF)]
        pieces += [s_ctx[:, c0:c0 + TQ] for c0 in range(0, s_ctx.shape[1], TQ)]
        sink = jnp.zeros((rows, 1), F32)
        for g in range(GROUP):
            sink = jnp.where(row_id // TQ == g, sink_ref[j * GROUP + g], sink)
        mx = pieces[0]
        for pc in pieces[1:]:
            mx = jnp.maximum(mx, pc)
        m = jnp.maximum(jnp.max(mx, axis=-1, keepdims=True), sink)
        p = jnp.concatenate([jnp.exp(pc - m).astype(BF16) for pc in pieces], axis=1)
        return p, jnp.exp(sink - m)

    def output(b, j, p, sink_p):
        vs = slice(j * pair_w, (j + 1) * pair_w)
        v_all = jnp.concatenate([v_refs[b + d][0, :, vs] for d in range(3)]
                                + [vx_ref[0, :, vs]], axis=0)
        r = jnp.dot(p, v_all, preferred_element_type=F32)
        den = pltpu.roll(r, HEAD_DIM, 1) + sink_p
        o = r / den
        for g in range(0, GROUP, 2):
            even = o[g * TQ:(g + 1) * TQ]
            odd = pltpu.roll(o[(g + 1) * TQ:(g + 2) * TQ], HEAD_DIM, 1)
            c0 = (j * GROUP + g) * HEAD_DIM
            o_ref[0, b * TQ:(b + 1) * TQ, c0:c0 + pair_w] = jnp.where(low_half, even, odd).astype(BF16)

    items = [(b, j) for b in range(nq) for j in range(N_KV_HEADS)]
    s_queue = [scores(*items[0]), scores(*items[1])]
    p_queue = [probs(*items[0], *s_queue.pop(0))]
    for i, item in enumerate(items):
        if i + 2 < len(items):
            s_queue.append(scores(*items[i + 2]))
        ready = p_queue.pop(0)
        if i + 1 < len(items):
            p_queue.append(probs(*items[i + 1], *s_queue.pop(0)))
        output(*item, *ready)


def _attention(q, kt, v, kxt, vx, sink):
    b, l, _ = q.shape
    lc = kxt.shape[2]
    nq = ATTN_Q_BLOCKS
    nblk = l // TQ
    assert nblk % nq == 0

    def blk(n, d):
        return jnp.clip(n * nq + d - 1, 0, nblk - 1)

    k_specs = [pl.BlockSpec((1, KV_W, TQ), lambda bi, n, d=d: (bi, 0, blk(n, d))) for d in range(nq + 2)]
    v_specs = [pl.BlockSpec((1, TQ, VEXT_W), lambda bi, n, d=d: (bi, blk(n, d), 0)) for d in range(nq + 2)]
    return pl.pallas_call(
        _attn_kernel,
        grid=(b, nblk // nq),
        in_specs=[pl.BlockSpec(memory_space=pltpu.SMEM),
                  pl.BlockSpec((1, nq * TQ, ATTN_W), lambda bi, n: (bi, n, 0))]
                 + k_specs + v_specs
                 + [pl.BlockSpec((1, KV_W, lc), lambda bi, n: (bi, 0, 0)),
                    pl.BlockSpec((1, lc, VEXT_W), lambda bi, n: (bi, 0, 0))],
        out_specs=pl.BlockSpec((1, nq * TQ, ATTN_W), lambda bi, n: (bi, n, 0)),
        out_shape=jax.ShapeDtypeStruct((b, l, ATTN_W), BF16),
        compiler_params=_params("arbitrary", "arbitrary"),
        name="attention",
    )(sink, q, *([kt] * (nq + 2)), *([v] * (nq + 2)), kxt, vx)


def _exact_zero(value):
    bits = lax.bitcast_convert_type(value[0:SUBLANES, 0:LANES].astype(F32), U32)
    return lax.bitcast_convert_type((bits >> 16) >> 16, F32)[0:1]


def _shift_rows(a, d):
    n = a.shape[0]
    return pltpu.roll(a, (-d) % n, 0)


def _merge_kernel(ao_ref, u_ref, up_ref, un_ref, sa_ref, sp_ref, x_ref, mod_ref,
                  wpool_ref, pscale_ref, wab_ref, wpb_ref, wout_ref, g2_ref, wr_ref, br_ref,
                  x1_ref, h2_ref, idx_ref, gate_ref, rank_ref, cnt_ref, carry_ref, *, seq_len):
    bi = pl.program_id(0)
    i = pl.program_id(1)
    last = pl.num_programs(1) - 1
    tm, sub = TM_MERGE, MERGE_SUB

    @pl.when((bi == 0) & (i == 0))
    def _():
        carry_ref[...] = jnp.zeros_like(carry_ref)

    def window(r0, cs):
        lo, hi = r0 - POOL_HALO, r0 + sub + POOL_HALO
        parts = []
        if lo < 0:
            parts.append(jnp.where(i > 0, up_ref[0, :, cs], 0.0))
        parts.append(u_ref[0, max(lo, 0):min(hi, tm), cs])
        if hi > tm:
            parts.append(jnp.where(i < last, un_ref[0, :, cs], 0.0))
        return jnp.concatenate(parts, axis=0) if len(parts) > 1 else parts[0]

    def pool_diff(r0, after=None):
        t = i * tm + r0 + lax.broadcasted_iota(jnp.int32, (sub, 1), 0)
        zero_row = None
        if after is not None:
            zero_row = jnp.concatenate([_exact_zero(after)] * (POOL_GROUP_W // LANES), axis=1)
        diffs = []
        for g, w in enumerate(POOL_WINDOWS):
            cs = slice(g * POOL_GROUP_W, (g + 1) * POOL_GROUP_W)
            e = window(r0, cs)
            if zero_row is not None:
                e = e + zero_row
            acc = _shift_rows(e, -1) + e
            step = 1
            while 2 * step < w:
                acc = _shift_rows(acc, -step) + _shift_rows(acc, step)
                step *= 2
            win = acc[POOL_HALO:POOL_HALO + sub]
            half = w // 2
            cnt = (jnp.minimum(t + half, seq_len) - jnp.maximum(t - half, 0)).astype(F32)
            diffs.append((win / cnt - e[POOL_HALO:POOL_HALO + sub]).astype(BF16))
        return diffs

    def branches(r0, diffs):
        rs = slice(r0, r0 + sub)
        a = jnp.dot(ao_ref[0, rs, :], wab_ref[...], preferred_element_type=F32)
        mixed = [jnp.dot(df, wpool_ref[g], preferred_element_type=F32) for g, df in enumerate(diffs)]
        pool_o = jnp.concatenate(mixed, axis=1) * pscale_ref[...]
        p = jnp.dot(pool_o.astype(BF16), wpb_ref[...], preferred_element_type=F32)
        merged = sa_ref[0, rs, :].astype(F32) * a + sp_ref[0, rs, :].astype(F32) * p
        return merged.astype(BF16)

    def out_proj(merged):
        return jnp.dot(merged, wout_ref[...], preferred_element_type=F32)

    def finish(r0, o):
        rs = slice(r0, r0 + sub)
        x1 = x_ref[0, rs, :] + mod_ref[0, 2:3, :] * o
        x1_ref[0, rs, :] = x1
        h2 = _norm_mod(x1, g2_ref[...], mod_ref[0, 3:4, :], mod_ref[0, 4:5, :])
        h2_ref[rs, :] = _pack_rows(h2)
        return jnp.dot(h2.astype(BF16), wr_ref[...], preferred_element_type=F32) + br_ref[...]

    def route(r0, logits, carry):
        lt = logits.T
        eid = lax.broadcasted_iota(jnp.int32, (N_EXPERTS, sub), 0).astype(F32)
        work = lt
        vals, idxs, sels = [], [], []
        for _ in range(TOP_K):
            mx = jnp.max(work, axis=0, keepdims=True)
            ix = jnp.min(jnp.where(work == mx, eid, float(N_EXPERTS)), axis=0, keepdims=True)
            sel = eid == ix
            work = jnp.where(sel, -jnp.inf, work)
            vals.append(mx)
            idxs.append(ix)
            sels.append(sel)
        ex = [jnp.exp(v - vals[0]) for v in vals]
        tot = ex[0] + ex[1] + ex[2] + ex[3]
        gate_ref[:, r0:r0 + sub] = jnp.concatenate([e_ / tot for e_ in ex], axis=0)
        idx_ref[:, r0:r0 + sub] = jnp.concatenate(idxs, axis=0).astype(jnp.int32)
        member = (sels[0] | sels[1] | sels[2] | sels[3]).astype(F32)
        t_row = lax.broadcasted_iota(jnp.int32, (sub, sub), 0)
        t_col = lax.broadcasted_iota(jnp.int32, (sub, sub), 1)
        earlier = (t_row < t_col).astype(BF16)
        before = jnp.dot(member.astype(BF16), earlier, preferred_element_type=F32) + carry
        ranks = [jnp.sum(jnp.where(s, before, 0.0), axis=0, keepdims=True) for s in sels]
        rank_ref[:, r0:r0 + sub] = jnp.concatenate(ranks, axis=0).astype(jnp.int32)
        return carry + jnp.sum(member, axis=1, keepdims=True)

    starts = list(range(0, tm, sub))
    carry = carry_ref[...]
    merged = branches(starts[0], pool_diff(starts[0]))
    for si, r0 in enumerate(starts):
        o = out_proj(merged)
        if si + 1 < len(starts):
            merged = branches(starts[si + 1], pool_diff(starts[si + 1], after=merged))
        carry = route(r0, finish(r0, o), carry)
    carry_ref[...] = carry
    cnt_ref[...] = carry


def _merge_route(attn_o, u, ga, gp, x, mod, w_pool_bf, pool_scale, wab, wpb, wout,
                 norm2_g, w_router, b_router):
    b, l, d = x.shape
    tm = TM_MERGE
    nt = l // tm
    n = b * l
    hb = tm // POOL_HALO
    row = lambda bi, i: (bi, i, 0)
    flat = lambda bi, i: (bi * nt + i, 0)
    halo_prev = lambda bi, i: (bi, jnp.maximum(i * hb - 1, 0), 0)
    halo_next = lambda bi, i: (bi, jnp.minimum((i + 1) * hb, l // POOL_HALO - 1), 0)
    slot_major = lambda bi, i: (0, bi * nt + i)
    out_shape = (
        jax.ShapeDtypeStruct((b, l, d), F32),
        jax.ShapeDtypeStruct((n, ROW_WORDS), U32),
        jax.ShapeDtypeStruct((TOP_K, n), jnp.int32),
        jax.ShapeDtypeStruct((TOP_K, n), F32),
        jax.ShapeDtypeStruct((TOP_K, n), jnp.int32),
        jax.ShapeDtypeStruct((N_EXPERTS, 1), F32),
    )
    return pl.pallas_call(
        functools.partial(_merge_kernel, seq_len=l),
        grid=(b, nt),
        in_specs=[pl.BlockSpec((1, tm, d), row),
                  pl.BlockSpec((1, tm, d), row),
                  pl.BlockSpec((1, POOL_HALO, d), halo_prev),
                  pl.BlockSpec((1, POOL_HALO, d), halo_next),
                  pl.BlockSpec((1, tm, d), row),
                  pl.BlockSpec((1, tm, d), row),
                  pl.BlockSpec((1, tm, d), row),
                  pl.BlockSpec((1, 6, d), lambda bi, i: (bi, 0, 0)),
                  _const_spec(w_pool_bf.shape),
                  _const_spec((1, d)),
                  _const_spec((d, d)), _const_spec((d, d)), _const_spec((d, d)),
                  _const_spec((1, d)),
                  _const_spec((d, N_EXPERTS)),
                  _const_spec((1, N_EXPERTS))],
        out_specs=(pl.BlockSpec((1, tm, d), row),
                   pl.BlockSpec((tm, ROW_WORDS), flat),
                   pl.BlockSpec((TOP_K, tm), slot_major),
                   pl.BlockSpec((TOP_K, tm), slot_major),
                   pl.BlockSpec((TOP_K, tm), slot_major),
                   _const_spec((N_EXPERTS, 1))),
        out_shape=out_shape,
        scratch_shapes=[pltpu.VMEM((N_EXPERTS, 1), F32)],
        compiler_params=_params("arbitrary", "arbitrary"),
        name="merge_route",
    )(attn_o, u, u, u, ga, gp, x, mod, w_pool_bf, pool_scale, wab, wpb, wout,
      norm2_g, w_router, b_router)


def _sc_worker_id():
    return lax.axis_index("s") * SC_CORES + lax.axis_index("c")


def _sc_scatter_rows(rows, dest, pad_dest, out_rows):
    n, w = rows.shape
    n_win = n // (SC_WORKERS * SC_WINDOW)
    n_pad_win = pad_dest.shape[1]
    assert dest.shape == (TOP_K, n) and SC_WORKERS * n_win * SC_WINDOW == n
    mesh = plsc.VectorSubcoreMesh(core_axis_name="c", subcore_axis_name="s")
    zeros = jnp.zeros((SC_WINDOW, w), rows.dtype)

    @functools.partial(
        pl.kernel, mesh=mesh,
        out_type=jax.ShapeDtypeStruct((out_rows, w), rows.dtype),
        scratch_types=[pltpu.VMEM((TOP_K, SC_WINDOW), jnp.int32),
                       pltpu.VMEM((n_pad_win, SC_WINDOW), jnp.int32),
                       pltpu.VMEM((SC_WINDOW, w), rows.dtype),
                       pltpu.SemaphoreType.DMA],
        name="sc_scatter_rows",
    )
    def scatter(rows_hbm, dest_hbm, pad_hbm, zeros_hbm, out_hbm, idx_v, pad_v, rows_v, sem):
        wid = _sc_worker_id()

        @pl.loop(0, n_win)
        def _(j):
            base = (wid * n_win + j) * SC_WINDOW
            pltpu.sync_copy(rows_hbm.at[pl.ds(base, SC_WINDOW)], rows_v)
            pltpu.sync_copy(dest_hbm.at[:, pl.ds(base, SC_WINDOW)], idx_v)
            copies = [pltpu.async_copy(rows_v, out_hbm.at[idx_v.at[k]], sem) for k in range(TOP_K)]
            for cp in copies:
                cp.wait()

        pltpu.sync_copy(zeros_hbm, rows_v)
        pltpu.sync_copy(pad_hbm.at[wid], pad_v)
        copies = [pltpu.async_copy(rows_v, out_hbm.at[pad_v.at[j]], sem) for j in range(n_pad_win)]
        for cp in copies:
            cp.wait()

    return scatter(rows, dest, pad_dest, zeros)


def _expert_kernel(be_ref, first_ref, next_ref, valid_ref, nu_ref, x_ref, wgu_hbm, bgu_ref, wd_hbm,
                   bd_ref, y_ref, wgu_f, wd_f, wgu_s, wd_s, sem):
    i = pl.program_id(0)

    def weight_copies(e):
        return (pltpu.make_async_copy(wgu_hbm.at[e], wgu_f, sem.at[0]),
                pltpu.make_async_copy(wd_hbm.at[e], wd_f, sem.at[1]))

    def start_weights(e):
        for cp in weight_copies(e):
            cp.start(priority=1)

    @pl.when(i == 0)
    def _():
        start_weights(be_ref[0])

    @pl.when(i < nu_ref[0])
    def _():
        e = be_ref[i]

        @pl.when(first_ref[i] == 1)
        def _():
            for cp in weight_copies(e):
                cp.wait()
            ck = 16

            def cast_rows(c, carry):
                r = pl.ds(pl.multiple_of(c * ck, ck), ck)
                wgu_s[r, :] = wgu_f[r, :].astype(BF16)
                wd_s[r, :] = wd_f[r, :].astype(BF16)
                return carry

            lax.fori_loop(0, D_MODEL // ck, cast_rows, 0, unroll=2)

            @pl.when(next_ref[i] >= 0)
            def _():
                start_weights(next_ref[i])

        def ffn(m):
            x = _unpack_rows(x_ref[:m, :]).astype(BF16)
            bgu = bgu_ref[e]
            gt = jnp.dot(x, wgu_s[:, :D_FF], preferred_element_type=F32) + bgu[:, :D_FF]
            ln = jnp.dot(x, wgu_s[:, D_FF:], preferred_element_type=F32) + bgu[:, D_FF:]
            gt = jnp.minimum(gt, SWIGLU_LIMIT)
            ln = jnp.clip(ln, -SWIGLU_LIMIT, SWIGLU_LIMIT)
            act = gt * jax.nn.sigmoid(SWIGLU_ALPHA * gt) * (ln + 1.0)
            y = jnp.dot(act.astype(BF16), wd_s[...], preferred_element_type=F32) + bd_ref[e]
            y_ref[:m, :] = _pack_rows(y)
            if m < TM_EXPERT:
                y_ref[m:, :] = jnp.zeros((TM_EXPERT - m, ROW_WORDS), U32)

        quanta = (valid_ref[i] + EXPERT_ROW_QUANTUM - 1) // EXPERT_ROW_QUANTUM
        for q in range(1, TM_EXPERT // EXPERT_ROW_QUANTUM + 1):
            pl.when(quanta == q)(functools.partial(ffn, q * EXPERT_ROW_QUANTUM))

    @pl.when(i >= nu_ref[0])
    def _():
        y_ref[...] = jnp.zeros_like(y_ref)


def _experts(block_e, first, next_e, valid, n_used, xs, w_gu, b_gu, w_down, b_down):
    d = D_MODEL
    nb = block_e.shape[0]
    tm = TM_EXPERT

    xmap = lambda i, be, fi, nx, va, nu: (jnp.minimum(i, nu[0] - 1), 0)
    grid_spec = pltpu.PrefetchScalarGridSpec(
        num_scalar_prefetch=5,
        grid=(nb,),
        in_specs=[pl.BlockSpec((tm, ROW_WORDS), xmap),
                  pl.BlockSpec(memory_space=pl.ANY),
                  _const_spec((N_EXPERTS, 1, 2 * D_FF)),
                  pl.BlockSpec(memory_space=pl.ANY),
                  _const_spec((N_EXPERTS, 1, d))],
        out_specs=pl.BlockSpec((tm, ROW_WORDS), lambda i, be, fi, nx, va, nu: (i, 0)),
        scratch_shapes=[pltpu.VMEM((d, 2 * D_FF), F32), pltpu.VMEM((D_FF, d), F32),
                        pltpu.VMEM((d, 2 * D_FF), BF16), pltpu.VMEM((D_FF, d), BF16),
                        pltpu.SemaphoreType.DMA((2,))],
    )
    return pl.pallas_call(
        _expert_kernel,
        grid_spec=grid_spec,
        out_shape=jax.ShapeDtypeStruct((nb * tm, ROW_WORDS), U32),
        compiler_params=_params("arbitrary"),
        name="moe_experts",
    )(block_e, first, next_e, valid, n_used, xs, w_gu, b_gu.reshape(N_EXPERTS, 1, 2 * D_FF),
      w_down, b_down.reshape(N_EXPERTS, 1, d))


def _sc_gather_rows(table, idx):
    m = idx.shape[0]
    w = table.shape[1]
    win = SC_GATHER_WINDOW
    per_worker = m // SC_WORKERS
    n_win = per_worker // win
    assert per_worker * SC_WORKERS == m and n_win * win == per_worker and n_win % 2 == 0
    mesh = plsc.VectorSubcoreMesh(core_axis_name="c", subcore_axis_name="s")

    @functools.partial(
        pl.kernel, mesh=mesh,
        out_type=jax.ShapeDtypeStruct((m, w), table.dtype),
        scratch_types=[pltpu.VMEM((win,), jnp.int32), pltpu.VMEM((win,), jnp.int32),
                       pltpu.VMEM((win, w), table.dtype), pltpu.VMEM((win, w), table.dtype),
                       pltpu.SemaphoreType.DMA((4,))],
        name="sc_gather_rows",
    )
    def gather(table_hbm, idx_hbm, out_hbm, idx0, idx1, rows0, rows1, sems):
        wid = _sc_worker_id()
        idxs, rows = (idx0, idx1), (rows0, rows1)

        def out_window(j):
            return out_hbm.at[pl.ds(wid * per_worker + j * win, win)]

        def start_gather(j, slot):
            pltpu.sync_copy(idx_hbm.at[pl.ds(wid * per_worker + j * win, win)], idxs[slot])
            pltpu.async_copy(table_hbm.at[idxs[slot]], rows[slot], sems.at[slot])

        def wait_gather(slot):
            pltpu.make_async_copy(table_hbm.at[idxs[slot]], rows[slot], sems.at[slot]).wait()

        def wait_write(j, slot):
            pltpu.make_async_copy(rows[slot], out_window(j), sems.at[2 + slot]).wait()

        start_gather(0, 0)

        @pl.loop(0, n_win, step=2)
        def _(j0):
            for slot in range(2):
                j = j0 + slot
                other = 1 - slot

                @pl.when(j >= 1)
                def _():
                    wait_write(j - 1, other)

                @pl.when(j + 1 < n_win)
                def _():
                    start_gather(j + 1, other)

                wait_gather(slot)
                pltpu.async_copy(rows[slot], out_window(j), sems.at[2 + slot])

        wait_write(n_win - 1, (n_win - 1) % 2)

    return gather(table, idx)


def _combine_dense_kernel(y4_ref, gate_ref, x1_ref, mod_ref, fg_ref, o_ref):
    g4 = gate_ref[...]
    gate = jnp.concatenate([g4, jnp.zeros_like(g4)], axis=0).T
    y = gate[:, 0:1] * _unpack_rows(y4_ref[0])
    for k in range(1, TOP_K):
        y = y + gate[:, k:k + 1] * _unpack_rows(y4_ref[k])
    x2 = x1_ref[...] + mod_ref[0, 5:6, :] * y
    o_ref[...] = x2 * lax.rsqrt(jnp.mean(x2 * x2, axis=-1, keepdims=True) + NORM_EPS) * fg_ref[...]


def _combine_dense(y4, gate_w, x1, mod, final_g):
    b, l, d = x1.shape
    ts = TM_COMBINE
    nt = l // ts
    return pl.pallas_call(
        _combine_dense_kernel,
        grid=(b * nt,),
        in_specs=[pl.BlockSpec((TOP_K, ts, ROW_WORDS), lambda s: (0, s, 0)),
                  pl.BlockSpec((TOP_K, ts), lambda s: (0, s)),
                  pl.BlockSpec((ts, d), lambda s: (s, 0)),
                  pl.BlockSpec((1, 6, d), lambda s: (s // nt, 0, 0)),
                  _const_spec((1, d))],
        out_specs=pl.BlockSpec((ts, d), lambda s: (s, 0)),
        out_shape=jax.ShapeDtypeStruct((b * l, d), F32),
        compiler_params=_params("arbitrary"),
        name="moe_combine",
    )(y4, gate_w, x1.reshape(b * l, d), mod, final_g).reshape(b, l, d)


def _moe_expert_rows(h2, top_idx, rank, counts, w_gu, b_gu, w_down, b_down):
    n = h2.shape[0]
    tm = TM_EXPERT
    nb = n * TOP_K // tm + N_EXPERTS
    cnt = counts[:, 0].astype(jnp.int32)
    padded = (cnt + tm - 1) // tm * tm
    pend = jnp.cumsum(padded)
    pstart = pend - padded
    expert_ids = jnp.arange(N_EXPERTS, dtype=jnp.int32)
    dest = jnp.sum(jnp.where(top_idx[..., None] == expert_ids, pstart, 0), axis=-1) + rank
    n_used = (pend[-1] // tm).reshape(1)
    block_start = jnp.arange(nb, dtype=jnp.int32) * tm
    block_e = jnp.minimum(jnp.sum((pend[None, :] <= block_start[:, None]).astype(jnp.int32), axis=1),
                          N_EXPERTS - 1)
    first = jnp.concatenate([jnp.ones((1,), jnp.int32),
                             (block_e[1:] != block_e[:-1]).astype(jnp.int32)])
    later_used = (expert_ids[None, :] > expert_ids[:, None]) & (cnt[None, :] > 0)
    next_of_expert = jnp.min(jnp.where(later_used, expert_ids[None, :], N_EXPERTS), axis=1)
    next_of_expert = jnp.where(next_of_expert < N_EXPERTS, next_of_expert, -1)
    next_e = jnp.sum(jnp.where(block_e[:, None] == expert_ids, next_of_expert, 0), axis=-1)
    used_end = jnp.sum(jnp.where(block_e[:, None] == expert_ids, pstart + cnt, 0), axis=-1)
    valid = jnp.clip(used_end - block_start, 0, tm)

    qm = EXPERT_ROW_QUANTUM
    pad_j = jnp.arange(qm, dtype=jnp.int32)[None, :]
    n_pad = (cnt + qm - 1) // qm * qm - cnt
    spare = nb * tm + expert_ids[:, None] * qm + pad_j
    pad_dest = jnp.where(pad_j < n_pad[:, None], (pstart + cnt)[:, None] + pad_j, spare)
    pad_dest = pad_dest.reshape(SC_WORKERS, N_EXPERTS * qm // (SC_WORKERS * SC_WINDOW), SC_WINDOW)

    xs = _sc_scatter_rows(h2, dest, pad_dest, nb * tm + N_EXPERTS * qm)
    ys = _experts(block_e, first, next_e, valid, n_used, xs, w_gu, b_gu, w_down, b_down)
    return _sc_gather_rows(ys, dest.reshape(-1)).reshape(TOP_K, n, ROW_WORDS)


def _rope_tables(seq_len):
    inv_freq = ROPE_BASE ** (-jnp.arange(ROPE_PAIRS, dtype=F32) / ROPE_PAIRS)
    reps = LANES // HEAD_DIM
    out = []
    for n_pos, on_row in ((seq_len // GRID_W, True), (GRID_W, False)):
        ang = jnp.arange(n_pos, dtype=F32)[:, None] * inv_freq
        zero = jnp.zeros_like(ang)
        for fn, sign in ((jnp.cos, 1.0), (jnp.sin, -1.0)):
            v = fn(ang)
            head = ([sign * v, v, zero, zero] if on_row else [zero, zero, sign * v, v])
            out.append(jnp.concatenate(head * reps, axis=1))
    return out


def kernel(x, c, ctx, c_ctx, w_ada, b_ada, norm1_g, norm2_g, w_in, b_in, attn_sink, w_pool,
           pool_scale, w_attn_br, w_pool_br, w_out, w_router, b_router, w_gu, b_gu, w_down,
           b_down, final_g):
    b, l, d = x.shape
    assert w_ada.shape[0] == 1, "single-layer block"

    cond = jnp.zeros((SUBLANES, d), F32).at[:b].set(c).at[b].set(c_ctx)
    mod = _adaln(cond, w_ada[0], b_ada[0])[:b + 1].reshape(b + 1, 6, d)
    mod_x, mod_c = mod[:b], mod[b:b + 1]

    rope = _rope_tables(l)
    w_in_bf = w_in[0].astype(BF16)
    b_in2 = b_in[0].reshape(1, IN_W)
    g1 = norm1_g[0].reshape(1, d)
    q, kt, v, u, ga, gp = _inproj(x, mod_x, g1, w_in_bf, b_in2, rope)
    kv_sl = slice(ATTN_W, ATTN_W + 2 * KV_W)
    kxt, vx = _ctx_kv(ctx, mod_c, g1, w_in_bf[:, kv_sl], b_in2[:, kv_sl])

    attn_o = _attention(q, kt, v, kxt, vx, attn_sink[0])

    x1, h2, top_idx, gate_w, rank, counts = _merge_route(
        attn_o, u, ga, gp, x, mod_x, w_pool[0].astype(BF16), pool_scale[0].reshape(1, d),
        w_attn_br[0].astype(BF16), w_pool_br[0].astype(BF16), w_out[0].astype(BF16),
        norm2_g[0].reshape(1, d), w_router[0].astype(BF16), b_router[0].reshape(1, N_EXPERTS))

    y4 = _moe_expert_rows(h2, top_idx, rank, counts, w_gu[0], b_gu[0], w_down[0], b_down[0])
    return _combine_dense(y4, gate_w, x1, mod_x, final_g.reshape(1, d))
```
